```python
import jax, jax.numpy as jnp
from jax import lax
import numpy as np

D_MODEL = 1024
BATCH = 8
SEQ = 4096
DEPTH = 1

D_FF = 2816
D_PLE = 256
D_GMLP = D_MODEL
N_SGU_GROUPS = 4
CHUNK = 128
D_POOL = D_MODEL
POOL_WINDOWS = (2, 4, 8, 16)
N_POOL_GROUPS = len(POOL_WINDOWS)
D_IN = 2 * D_GMLP + D_POOL + 2 * D_MODEL
EPS = 1e-6

kernel_name = "hybrid_sgu_pool_macaron_layer"


def _rmsnorm(x, g):
    xf = x.astype(jnp.float32)
    y = xf * lax.rsqrt(jnp.mean(xf * xf, axis=-1, keepdims=True) + EPS)
    return (y * g.astype(jnp.float32)).astype(x.dtype)


def _layernorm(x, g):
    xf = x.astype(jnp.float32)
    mu = jnp.mean(xf, axis=-1, keepdims=True)
    xc = xf - mu
    y = xc * lax.rsqrt(jnp.mean(xc * xc, axis=-1, keepdims=True) + EPS)
    return (y * g.astype(jnp.float32)).astype(x.dtype)


def _swiglu(xn, w_gate, w_up, w_down):
    return (jax.nn.silu(xn @ w_gate) * (xn @ w_up)) @ w_down


def _spatial_gating(u, v, norm_g, w_s, b_s):
    B, S, _ = v.shape
    dg = D_GMLP // N_SGU_GROUPS
    v = _layernorm(v, norm_g)
    vc = v.reshape(B, S // CHUNK, CHUNK, N_SGU_GROUPS, dg)
    causal = jnp.tril(jnp.ones((CHUNK, CHUNK), dtype=bool))
    ws = jnp.where(causal[None], w_s, 0.0).astype(v.dtype)
    sv = jnp.einsum('gts,bcsgd->bctgd', ws, vc) + b_s.T[:, :, None].astype(v.dtype)
    return u * sv.reshape(B, S, D_GMLP)


def _pool_mixer(c, pool_w, pool_scale):
    B, S, _ = c.shape
    dg = D_POOL // N_POOL_GROUPS
    cf = c.astype(jnp.float32)
    cs = jnp.concatenate([jnp.zeros((B, 1, D_POOL), jnp.float32), jnp.cumsum(cf, axis=1)], axis=1)
    t = jnp.arange(S)
    outs = []
    for gi, w in enumerate(POOL_WINDOWS):
        lo = jnp.maximum(t + 1 - w, 0)
        sl = slice(gi * dg, (gi + 1) * dg)
        csg = cs[:, :, sl]
        count = (t + 1 - lo).astype(jnp.float32)[None, :, None]
        mean = (csg[:, 1:] - csg[:, lo]) / count
        diff = (mean - cf[:, :, sl]).astype(c.dtype)
        outs.append(jnp.einsum('bsc,cd->bsd', diff, pool_w[gi]))
    return jnp.concatenate(outs, axis=-1) * pool_scale


def _token_mixer(xn, w_in, sgu_norm_g, sgu_w, sgu_b, pool_w, pool_scale, w_out_a, w_out_b, w_o):
    z = xn @ w_in
    i1 = D_GMLP
    i2 = 2 * D_GMLP
    i3 = i2 + D_POOL
    i4 = i3 + D_MODEL
    u = jax.nn.gelu(z[..., :i1])
    v = jax.nn.gelu(z[..., i1:i2])
    c = z[..., i2:i3]
    ga = z[..., i3:i4]
    gb = z[..., i4:]
    a = _spatial_gating(u, v, sgu_norm_g, sgu_w, sgu_b)
    b = _pool_mixer(c, pool_w, pool_scale)
    y = jax.nn.sigmoid(ga) * (a @ w_out_a) + jax.nn.sigmoid(gb) * (b @ w_out_b)
    return y @ w_o


def _gain(k, shape):
    return 1.0 + 0.02 * jax.random.normal(k, shape, jnp.float32)


def _w(k, shape, fan_in):
    return jax.random.normal(k, shape, jnp.float32) * (fan_in ** -0.5)


def _fwd_setup_inputs(seed: int = 0) -> dict:
    key = jax.random.key(seed)
    ks = jax.random.split(key, 32)
    L = DEPTH
    dgp = D_POOL // N_POOL_GROUPS
    return {
        "x": jax.random.normal(ks[0], (BATCH, SEQ, D_MODEL), jnp.float32),
        "p": jax.random.normal(ks[1], (DEPTH, BATCH, SEQ, D_PLE), jnp.float32),
        "ffn1_pre_g": _gain(ks[2], (L, D_MODEL)),
        "ffn1_w_gate": _w(ks[3], (L, D_MODEL, D_FF), D_MODEL),
        "ffn1_w_up": _w(ks[4], (L, D_MODEL, D_FF), D_MODEL),
        "ffn1_w_down": _w(ks[5], (L, D_FF, D_MODEL), D_FF),
        "ffn1_post_g": _gain(ks[6], (L, D_MODEL)),
        "mix_pre_g": _gain(ks[7], (L, D_MODEL)),
        "w_in": _w(ks[8], (L, D_MODEL, D_IN), D_MODEL),
        "sgu_norm_g": _gain(ks[9], (L, D_GMLP)),
        "sgu_w": _w(ks[10], (L, N_SGU_GROUPS, CHUNK, CHUNK), CHUNK),
        "sgu_b": _gain(ks[11], (L, N_SGU_GROUPS, CHUNK)),
        "pool_w": _w(ks[12], (L, N_POOL_GROUPS, dgp, dgp), dgp),
        "pool_scale": _gain(ks[13], (L, D_POOL)),
        "w_out_a": _w(ks[14], (L, D_GMLP, D_MODEL), D_GMLP),
        "w_out_b": _w(ks[15], (L, D_POOL, D_MODEL), D_POOL),
        "w_o": _w(ks[16], (L, D_MODEL, D_MODEL), D_MODEL),
        "mix_post_g": _gain(ks[17], (L, D_MODEL)),
        "ffn2_pre_g": _gain(ks[18], (L, D_MODEL)),
        "ffn2_w_gate": _w(ks[19], (L, D_MODEL, D_FF), D_MODEL),
        "ffn2_w_up": _w(ks[20], (L, D_MODEL, D_FF), D_MODEL),
        "ffn2_w_down": _w(ks[21], (L, D_FF, D_MODEL), D_FF),
        "ffn2_post_g": _gain(ks[22], (L, D_MODEL)),
        "ple_pre_g": _gain(ks[23], (L, D_MODEL)),
        "ple_w_gate": _w(ks[24], (L, D_MODEL, D_MODEL), D_MODEL),
        "ple_w_proj": _w(ks[25], (L, D_PLE, D_MODEL), D_PLE),
        "ple_post_g": _gain(ks[26], (L, D_MODEL)),
    }


def _fwd_reference(x, p, ffn1_pre_g, ffn1_w_gate, ffn1_w_up, ffn1_w_down, ffn1_post_g,
              mix_pre_g, w_in, sgu_norm_g, sgu_w, sgu_b, pool_w, pool_scale,
              w_out_a, w_out_b, w_o, mix_post_g,
              ffn2_pre_g, ffn2_w_gate, ffn2_w_up, ffn2_w_down, ffn2_post_g,
              ple_pre_g, ple_w_gate, ple_w_proj, ple_post_g):
    h = x
    for i in range(DEPTH):
        f = _swiglu(_rmsnorm(h, ffn1_pre_g[i]), ffn1_w_gate[i], ffn1_w_up[i], ffn1_w_down[i])
        h = h + 0.5 * _rmsnorm(f, ffn1_post_g[i])
        m = _token_mixer(_rmsnorm(h, mix_pre_g[i]), w_in[i], sgu_norm_g[i], sgu_w[i], sgu_b[i],
                         pool_w[i], pool_scale[i], w_out_a[i], w_out_b[i], w_o[i])
        h = h + _rmsnorm(m, mix_post_g[i])
        f = _swiglu(_rmsnorm(h, ffn2_pre_g[i]), ffn2_w_gate[i], ffn2_w_up[i], ffn2_w_down[i])
        h = h + 0.5 * _rmsnorm(f, ffn2_post_g[i])
        gate = jax.nn.sigmoid(_rmsnorm(h, ple_pre_g[i]) @ ple_w_gate[i])
        e = p[i] @ ple_w_proj[i]
        h = h + _rmsnorm(gate * e, ple_post_g[i])
    return h


import jax as _jax
import jax.numpy as _jnp

TWIN_FORMAT = 'train_step'
FWD_PARAMS = ['x', 'p', 'ffn1_pre_g', 'ffn1_w_gate', 'ffn1_w_up', 'ffn1_w_down', 'ffn1_post_g', 'mix_pre_g', 'w_in', 'sgu_norm_g', 'sgu_w', 'sgu_b', 'pool_w', 'pool_scale', 'w_out_a', 'w_out_b', 'w_o', 'mix_post_g', 'ffn2_pre_g', 'ffn2_w_gate', 'ffn2_w_up', 'ffn2_w_down', 'ffn2_post_g', 'ple_pre_g', 'ple_w_gate', 'ple_w_proj', 'ple_post_g']
TWIN_WEIGHTS = ['ffn1_pre_g', 'ffn1_w_gate', 'ffn1_w_up', 'ffn1_w_down', 'ffn1_post_g', 'mix_pre_g', 'w_in', 'sgu_norm_g', 'sgu_w', 'sgu_b', 'pool_w', 'pool_scale', 'w_out_a', 'w_out_b', 'w_o', 'mix_post_g', 'ffn2_pre_g', 'ffn2_w_gate', 'ffn2_w_up', 'ffn2_w_down', 'ffn2_post_g', 'ple_pre_g', 'ple_w_gate', 'ple_w_proj', 'ple_post_g']
TWIN_DIFF_INPUT = 'x'
TWIN_INPUTS = ['x', 'p', 'ffn1_pre_g', 'ffn1_w_gate', 'ffn1_w_up', 'ffn1_w_down', 'ffn1_post_g', 'mix_pre_g', 'w_in', 'sgu_norm_g', 'sgu_w', 'sgu_b', 'pool_w', 'pool_scale', 'w_out_a', 'w_out_b', 'w_o', 'mix_post_g', 'ffn2_pre_g', 'ffn2_w_gate', 'ffn2_w_up', 'ffn2_w_down', 'ffn2_post_g', 'ple_pre_g', 'ple_w_gate', 'ple_w_proj', 'ple_post_g', 'loss_target', 'm_ffn1_pre_g', 'm_ffn1_w_gate', 'm_ffn1_w_up', 'm_ffn1_w_down', 'm_ffn1_post_g', 'm_mix_pre_g', 'm_w_in', 'm_sgu_norm_g', 'm_sgu_w', 'm_sgu_b', 'm_pool_w', 'm_pool_scale', 'm_w_out_a', 'm_w_out_b', 'm_w_o', 'm_mix_post_g', 'm_ffn2_pre_g', 'm_ffn2_w_gate', 'm_ffn2_w_up', 'm_ffn2_w_down', 'm_ffn2_post_g', 'm_ple_pre_g', 'm_ple_w_gate', 'm_ple_w_proj', 'm_ple_post_g', 'v_ffn1_pre_g', 'v_ffn1_w_gate', 'v_ffn1_w_up', 'v_ffn1_w_down', 'v_ffn1_post_g', 'v_mix_pre_g', 'v_w_in', 'v_sgu_norm_g', 'v_sgu_w', 'v_sgu_b', 'v_pool_w', 'v_pool_scale', 'v_w_out_a', 'v_w_out_b', 'v_w_o', 'v_mix_post_g', 'v_ffn2_pre_g', 'v_ffn2_w_gate', 'v_ffn2_w_up', 'v_ffn2_w_down', 'v_ffn2_post_g', 'v_ple_pre_g', 'v_ple_w_gate', 'v_ple_w_proj', 'v_ple_post_g']
TWIN_OUTPUTS = ['loss', 'grad_x', 'grad_ffn1_pre_g', 'grad_ffn1_w_gate', 'grad_ffn1_w_up', 'grad_ffn1_w_down', 'grad_ffn1_post_g', 'grad_mix_pre_g', 'grad_w_in', 'grad_sgu_norm_g', 'grad_sgu_w', 'grad_sgu_b', 'grad_pool_w', 'grad_pool_scale', 'grad_w_out_a', 'grad_w_out_b', 'grad_w_o', 'grad_mix_post_g', 'grad_ffn2_pre_g', 'grad_ffn2_w_gate', 'grad_ffn2_w_up', 'grad_ffn2_w_down', 'grad_ffn2_post_g', 'grad_ple_pre_g', 'grad_ple_w_gate', 'grad_ple_w_proj', 'grad_ple_post_g', 'delta_ffn1_pre_g', 'delta_ffn1_w_gate', 'delta_ffn1_w_up', 'delta_ffn1_w_down', 'delta_ffn1_post_g', 'delta_mix_pre_g', 'delta_w_in', 'delta_sgu_norm_g', 'delta_sgu_w', 'delta_sgu_b', 'delta_pool_w', 'delta_pool_scale', 'delta_w_out_a', 'delta_w_out_b', 'delta_w_o', 'delta_mix_post_g', 'delta_ffn2_pre_g', 'delta_ffn2_w_gate', 'delta_ffn2_w_up', 'delta_ffn2_w_down', 'delta_ffn2_post_g', 'delta_ple_pre_g', 'delta_ple_w_gate', 'delta_ple_w_proj', 'delta_ple_post_g', 'new_m_ffn1_pre_g', 'new_m_ffn1_w_gate', 'new_m_ffn1_w_up', 'new_m_ffn1_w_down', 'new_m_ffn1_post_g', 'new_m_mix_pre_g', 'new_m_w_in', 'new_m_sgu_norm_g', 'new_m_sgu_w', 'new_m_sgu_b', 'new_m_pool_w', 'new_m_pool_scale', 'new_m_w_out_a', 'new_m_w_out_b', 'new_m_w_o', 'new_m_mix_post_g', 'new_m_ffn2_pre_g', 'new_m_ffn2_w_gate', 'new_m_ffn2_w_up', 'new_m_ffn2_w_down', 'new_m_ffn2_post_g', 'new_m_ple_pre_g', 'new_m_ple_w_gate', 'new_m_ple_w_proj', 'new_m_ple_post_g', 'new_v_ffn1_pre_g', 'new_v_ffn1_w_gate', 'new_v_ffn1_w_up', 'new_v_ffn1_w_down', 'new_v_ffn1_post_g', 'new_v_mix_pre_g', 'new_v_w_in', 'new_v_sgu_norm_g', 'new_v_sgu_w', 'new_v_sgu_b', 'new_v_pool_w', 'new_v_pool_scale', 'new_v_w_out_a', 'new_v_w_out_b', 'new_v_w_o', 'new_v_mix_post_g', 'new_v_ffn2_pre_g', 'new_v_ffn2_w_gate', 'new_v_ffn2_w_up', 'new_v_ffn2_w_down', 'new_v_ffn2_post_g', 'new_v_ple_pre_g', 'new_v_ple_w_gate', 'new_v_ple_w_proj', 'new_v_ple_post_g']
TWIN_LEAF_KINDS = {'loss': 'loss', 'grad_x': 'grad_x', 'grad_ffn1_pre_g': 'grad_w', 'grad_ffn1_w_gate': 'grad_w', 'grad_ffn1_w_up': 'grad_w', 'grad_ffn1_w_down': 'grad_w', 'grad_ffn1_post_g': 'grad_w', 'grad_mix_pre_g': 'grad_w', 'grad_w_in': 'grad_w', 'grad_sgu_norm_g': 'grad_w', 'grad_sgu_w': 'grad_w', 'grad_sgu_b': 'grad_w', 'grad_pool_w': 'grad_w', 'grad_pool_scale': 'grad_w', 'grad_w_out_a': 'grad_w', 'grad_w_out_b': 'grad_w', 'grad_w_o': 'grad_w', 'grad_mix_post_g': 'grad_w', 'grad_ffn2_pre_g': 'grad_w', 'grad_ffn2_w_gate': 'grad_w', 'grad_ffn2_w_up': 'grad_w', 'grad_ffn2_w_down': 'grad_w', 'grad_ffn2_post_g': 'grad_w', 'grad_ple_pre_g': 'grad_w', 'grad_ple_w_gate': 'grad_w', 'grad_ple_w_proj': 'grad_w', 'grad_ple_post_g': 'grad_w', 'delta_ffn1_pre_g': 'delta_w', 'delta_ffn1_w_gate': 'delta_w', 'delta_ffn1_w_up': 'delta_w', 'delta_ffn1_w_down': 'delta_w', 'delta_ffn1_post_g': 'delta_w', 'delta_mix_pre_g': 'delta_w', 'delta_w_in': 'delta_w', 'delta_sgu_norm_g': 'delta_w', 'delta_sgu_w': 'delta_w', 'delta_sgu_b': 'delta_w', 'delta_pool_w': 'delta_w', 'delta_pool_scale': 'delta_w', 'delta_w_out_a': 'delta_w', 'delta_w_out_b': 'delta_w', 'delta_w_o': 'delta_w', 'delta_mix_post_g': 'delta_w', 'delta_ffn2_pre_g': 'delta_w', 'delta_ffn2_w_gate': 'delta_w', 'delta_ffn2_w_up': 'delta_w', 'delta_ffn2_w_down': 'delta_w', 'delta_ffn2_post_g': 'delta_w', 'delta_ple_pre_g': 'delta_w', 'delta_ple_w_gate': 'delta_w', 'delta_ple_w_proj': 'delta_w', 'delta_ple_post_g': 'delta_w', 'new_m_ffn1_pre_g': 'new_m', 'new_m_ffn1_w_gate': 'new_m', 'new_m_ffn1_w_up': 'new_m', 'new_m_ffn1_w_down': 'new_m', 'new_m_ffn1_post_g': 'new_m', 'new_m_mix_pre_g': 'new_m', 'new_m_w_in': 'new_m', 'new_m_sgu_norm_g': 'new_m', 'new_m_sgu_w': 'new_m', 'new_m_sgu_b': 'new_m', 'new_m_pool_w': 'new_m', 'new_m_pool_scale': 'new_m', 'new_m_w_out_a': 'new_m', 'new_m_w_out_b': 'new_m', 'new_m_w_o': 'new_m', 'new_m_mix_post_g': 'new_m', 'new_m_ffn2_pre_g': 'new_m', 'new_m_ffn2_w_gate': 'new_m', 'new_m_ffn2_w_up': 'new_m', 'new_m_ffn2_w_down': 'new_m', 'new_m_ffn2_post_g': 'new_m', 'new_m_ple_pre_g': 'new_m', 'new_m_ple_w_gate': 'new_m', 'new_m_ple_w_proj': 'new_m', 'new_m_ple_post_g': 'new_m', 'new_v_ffn1_pre_g': 'new_v', 'new_v_ffn1_w_gate': 'new_v', 'new_v_ffn1_w_up': 'new_v', 'new_v_ffn1_w_down': 'new_v', 'new_v_ffn1_post_g': 'new_v', 'new_v_mix_pre_g': 'new_v', 'new_v_w_in': 'new_v', 'new_v_sgu_norm_g': 'new_v', 'new_v_sgu_w': 'new_v', 'new_v_sgu_b': 'new_v', 'new_v_pool_w': 'new_v', 'new_v_pool_scale': 'new_v', 'new_v_w_out_a': 'new_v', 'new_v_w_out_b': 'new_v', 'new_v_w_o': 'new_v', 'new_v_mix_post_g': 'new_v', 'new_v_ffn2_pre_g': 'new_v', 'new_v_ffn2_w_gate': 'new_v', 'new_v_ffn2_w_up': 'new_v', 'new_v_ffn2_w_down': 'new_v', 'new_v_ffn2_post_g': 'new_v', 'new_v_ple_pre_g': 'new_v', 'new_v_ple_w_gate': 'new_v', 'new_v_ple_w_proj': 'new_v', 'new_v_ple_post_g': 'new_v'}


def _forward(args):
    return _fwd_reference(*[args[k] for k in FWD_PARAMS])


def _output_shape():
    out = _jax.eval_shape(lambda: _forward(_fwd_setup_inputs(0)))
    return out.shape, out.dtype

N_MICROBATCH = 1
ADAM_LR = 0.001
ADAM_B1 = 0.9
ADAM_B2 = 0.999
ADAM_EPS = 1e-08
ADAM_WD = 0.01
ADAM_STEP = 10
PER_EXAMPLE_BATCH_AXIS = {'x': 0, 'p': 1, 'loss_target': 0}
SHARED_INPUTS = []
_WEIGHT_DTYPES = {'ffn1_pre_g': _jnp.float32, 'ffn1_w_gate': _jnp.float32, 'ffn1_w_up': _jnp.float32, 'ffn1_w_down': _jnp.float32, 'ffn1_post_g': _jnp.float32, 'mix_pre_g': _jnp.float32, 'w_in': _jnp.float32, 'sgu_norm_g': _jnp.float32, 'sgu_w': _jnp.float32, 'sgu_b': _jnp.float32, 'pool_w': _jnp.float32, 'pool_scale': _jnp.float32, 'w_out_a': _jnp.float32, 'w_out_b': _jnp.float32, 'w_o': _jnp.float32, 'mix_post_g': _jnp.float32, 'ffn2_pre_g': _jnp.float32, 'ffn2_w_gate': _jnp.float32, 'ffn2_w_up': _jnp.float32, 'ffn2_w_down': _jnp.float32, 'ffn2_post_g': _jnp.float32, 'ple_pre_g': _jnp.float32, 'ple_w_gate': _jnp.float32, 'ple_w_proj': _jnp.float32, 'ple_post_g': _jnp.float32}
MOMENT_SCALE = {'ffn1_pre_g': 4.505213e-01, 'ffn1_w_gate': 1.826857e-01, 'ffn1_w_up': 1.928542e-01, 'ffn1_w_down': 3.298453e-01, 'ffn1_post_g': 7.929582e+00, 'mix_pre_g': 6.488388e-01, 'w_in': 2.965486e-01, 'sgu_norm_g': 1.499389e-01, 'sgu_w': 2.089140e-01, 'sgu_b': 3.017904e-01, 'pool_w': 5.193381e-01, 'pool_scale': 5.529920e-01, 'w_out_a': 4.475656e-01, 'w_out_b': 5.264295e-01, 'w_o': 7.078034e-01, 'mix_post_g': 3.193764e+01, 'ffn2_pre_g': 4.205194e-01, 'ffn2_w_gate': 1.408324e-01, 'ffn2_w_up': 1.980593e-01, 'ffn2_w_down': 3.306219e-01, 'ffn2_post_g': 7.823357e+00, 'ple_pre_g': 1.324700e-01, 'ple_w_gate': 1.333556e-01, 'ple_w_proj': 3.332215e-01, 'ple_post_g': 3.219932e+01}


def _to_microbatches(a, axis):
    t = _jnp.moveaxis(a, axis, 0)
    t = t.reshape((N_MICROBATCH, t.shape[0] // N_MICROBATCH) + t.shape[1:])
    return _jnp.moveaxis(t, 1, axis + 1)


def setup_inputs(seed: int = 0) -> dict:
    inp = _fwd_setup_inputs(seed)
    key = _jax.random.fold_in(_jax.random.key(seed), 7919)
    shape, _ = _output_shape()
    out = dict(inp)
    out["loss_target"] = _jax.random.normal(_jax.random.fold_in(key, 0), shape, _jnp.float32)
    for i, name in enumerate(TWIN_WEIGHTS):
        w = inp[name].astype(_jnp.float32)
        if MOMENT_SCALE is None:
            s = _jnp.sqrt(_jnp.mean(_jnp.square(w)) + 1e-30)
        else:
            s = MOMENT_SCALE[name]
        km, kv = _jax.random.split(_jax.random.fold_in(key, i + 1))
        out[name] = w
        out["m_" + name] = s * _jax.random.normal(km, w.shape, _jnp.float32)
        out["v_" + name] = (s * s) * _jax.random.uniform(kv, w.shape, _jnp.float32, 0.5, 1.5)
    if N_MICROBATCH > 1:
        for name, axis in PER_EXAMPLE_BATCH_AXIS.items():
            out[name] = _to_microbatches(out[name], axis)
    return {'x': out['x'], 'p': out['p'], 'ffn1_pre_g': out['ffn1_pre_g'], 'ffn1_w_gate': out['ffn1_w_gate'], 'ffn1_w_up': out['ffn1_w_up'], 'ffn1_w_down': out['ffn1_w_down'], 'ffn1_post_g': out['ffn1_post_g'], 'mix_pre_g': out['mix_pre_g'], 'w_in': out['w_in'], 'sgu_norm_g': out['sgu_norm_g'], 'sgu_w': out['sgu_w'], 'sgu_b': out['sgu_b'], 'pool_w': out['pool_w'], 'pool_scale': out['pool_scale'], 'w_out_a': out['w_out_a'], 'w_out_b': out['w_out_b'], 'w_o': out['w_o'], 'mix_post_g': out['mix_post_g'], 'ffn2_pre_g': out['ffn2_pre_g'], 'ffn2_w_gate': out['ffn2_w_gate'], 'ffn2_w_up': out['ffn2_w_up'], 'ffn2_w_down': out['ffn2_w_down'], 'ffn2_post_g': out['ffn2_post_g'], 'ple_pre_g': out['ple_pre_g'], 'ple_w_gate': out['ple_w_gate'], 'ple_w_proj': out['ple_w_proj'], 'ple_post_g': out['ple_post_g'], 'loss_target': out['loss_target'], 'm_ffn1_pre_g': out['m_ffn1_pre_g'], 'm_ffn1_w_gate': out['m_ffn1_w_gate'], 'm_ffn1_w_up': out['m_ffn1_w_up'], 'm_ffn1_w_down': out['m_ffn1_w_down'], 'm_ffn1_post_g': out['m_ffn1_post_g'], 'm_mix_pre_g': out['m_mix_pre_g'], 'm_w_in': out['m_w_in'], 'm_sgu_norm_g': out['m_sgu_norm_g'], 'm_sgu_w': out['m_sgu_w'], 'm_sgu_b': out['m_sgu_b'], 'm_pool_w': out['m_pool_w'], 'm_pool_scale': out['m_pool_scale'], 'm_w_out_a': out['m_w_out_a'], 'm_w_out_b': out['m_w_out_b'], 'm_w_o': out['m_w_o'], 'm_mix_post_g': out['m_mix_post_g'], 'm_ffn2_pre_g': out['m_ffn2_pre_g'], 'm_ffn2_w_gate': out['m_ffn2_w_gate'], 'm_ffn2_w_up': out['m_ffn2_w_up'], 'm_ffn2_w_down': out['m_ffn2_w_down'], 'm_ffn2_post_g': out['m_ffn2_post_g'], 'm_ple_pre_g': out['m_ple_pre_g'], 'm_ple_w_gate': out['m_ple_w_gate'], 'm_ple_w_proj': out['m_ple_w_proj'], 'm_ple_post_g': out['m_ple_post_g'], 'v_ffn1_pre_g': out['v_ffn1_pre_g'], 'v_ffn1_w_gate': out['v_ffn1_w_gate'], 'v_ffn1_w_up': out['v_ffn1_w_up'], 'v_ffn1_w_down': out['v_ffn1_w_down'], 'v_ffn1_post_g': out['v_ffn1_post_g'], 'v_mix_pre_g': out['v_mix_pre_g'], 'v_w_in': out['v_w_in'], 'v_sgu_norm_g': out['v_sgu_norm_g'], 'v_sgu_w': out['v_sgu_w'], 'v_sgu_b': out['v_sgu_b'], 'v_pool_w': out['v_pool_w'], 'v_pool_scale': out['v_pool_scale'], 'v_w_out_a': out['v_w_out_a'], 'v_w_out_b': out['v_w_out_b'], 'v_w_o': out['v_w_o'], 'v_mix_post_g': out['v_mix_post_g'], 'v_ffn2_pre_g': out['v_ffn2_pre_g'], 'v_ffn2_w_gate': out['v_ffn2_w_gate'], 'v_ffn2_w_up': out['v_ffn2_w_up'], 'v_ffn2_w_down': out['v_ffn2_w_down'], 'v_ffn2_post_g': out['v_ffn2_post_g'], 'v_ple_pre_g': out['v_ple_pre_g'], 'v_ple_w_gate': out['v_ple_w_gate'], 'v_ple_w_proj': out['v_ple_w_proj'], 'v_ple_post_g': out['v_ple_post_g']}


def _loss(weights, diff, rest, loss_target):
    with _jax.named_scope("forward"):
        args = {**rest, TWIN_DIFF_INPUT: diff, **{k: w.astype(_WEIGHT_DTYPES[k]) for k, w in weights.items()}}
        y = _forward(args)
    with _jax.named_scope("loss_head"):
        err = _jnp.square(y.astype(_jnp.float32) - loss_target)
        return 0.5 * _jnp.sum(_jnp.mean(err, axis=-1)) if err.ndim else 0.5 * err


def _adamw(w, g, m, v):
    m = ADAM_B1 * m + (1.0 - ADAM_B1) * g
    v = ADAM_B2 * v + (1.0 - ADAM_B2) * _jnp.square(g)
    m_hat = m / (1.0 - ADAM_B1 ** ADAM_STEP)
    v_hat = v / (1.0 - ADAM_B2 ** ADAM_STEP)
    delta = -ADAM_LR * (m_hat / (_jnp.sqrt(v_hat) + ADAM_EPS) + ADAM_WD * w)
    return delta, m, v


def reference(x, p, ffn1_pre_g, ffn1_w_gate, ffn1_w_up, ffn1_w_down, ffn1_post_g, mix_pre_g, w_in, sgu_norm_g, sgu_w, sgu_b, pool_w, pool_scale, w_out_a, w_out_b, w_o, mix_post_g, ffn2_pre_g, ffn2_w_gate, ffn2_w_up, ffn2_w_down, ffn2_post_g, ple_pre_g, ple_w_gate, ple_w_proj, ple_post_g, loss_target, m_ffn1_pre_g, m_ffn1_w_gate, m_ffn1_w_up, m_ffn1_w_down, m_ffn1_post_g, m_mix_pre_g, m_w_in, m_sgu_norm_g, m_sgu_w, m_sgu_b, m_pool_w, m_pool_scale, m_w_out_a, m_w_out_b, m_w_o, m_mix_post_g, m_ffn2_pre_g, m_ffn2_w_gate, m_ffn2_w_up, m_ffn2_w_down, m_ffn2_post_g, m_ple_pre_g, m_ple_w_gate, m_ple_w_proj, m_ple_post_g, v_ffn1_pre_g, v_ffn1_w_gate, v_ffn1_w_up, v_ffn1_w_down, v_ffn1_post_g, v_mix_pre_g, v_w_in, v_sgu_norm_g, v_sgu_w, v_sgu_b, v_pool_w, v_pool_scale, v_w_out_a, v_w_out_b, v_w_o, v_mix_post_g, v_ffn2_pre_g, v_ffn2_w_gate, v_ffn2_w_up, v_ffn2_w_down, v_ffn2_post_g, v_ple_pre_g, v_ple_w_gate, v_ple_w_proj, v_ple_post_g):
    given = dict(x=x, p=p, ffn1_pre_g=ffn1_pre_g, ffn1_w_gate=ffn1_w_gate, ffn1_w_up=ffn1_w_up, ffn1_w_down=ffn1_w_down, ffn1_post_g=ffn1_post_g, mix_pre_g=mix_pre_g, w_in=w_in, sgu_norm_g=sgu_norm_g, sgu_w=sgu_w, sgu_b=sgu_b, pool_w=pool_w, pool_scale=pool_scale, w_out_a=w_out_a, w_out_b=w_out_b, w_o=w_o, mix_post_g=mix_post_g, ffn2_pre_g=ffn2_pre_g, ffn2_w_gate=ffn2_w_gate, ffn2_w_up=ffn2_w_up, ffn2_w_down=ffn2_w_down, ffn2_post_g=ffn2_post_g, ple_pre_g=ple_pre_g, ple_w_gate=ple_w_gate, ple_w_proj=ple_w_proj, ple_post_g=ple_post_g, loss_target=loss_target, m_ffn1_pre_g=m_ffn1_pre_g, m_ffn1_w_gate=m_ffn1_w_gate, m_ffn1_w_up=m_ffn1_w_up, m_ffn1_w_down=m_ffn1_w_down, m_ffn1_post_g=m_ffn1_post_g, m_mix_pre_g=m_mix_pre_g, m_w_in=m_w_in, m_sgu_norm_g=m_sgu_norm_g, m_sgu_w=m_sgu_w, m_sgu_b=m_sgu_b, m_pool_w=m_pool_w, m_pool_scale=m_pool_scale, m_w_out_a=m_w_out_a, m_w_out_b=m_w_out_b, m_w_o=m_w_o, m_mix_post_g=m_mix_post_g, m_ffn2_pre_g=m_ffn2_pre_g, m_ffn2_w_gate=m_ffn2_w_gate, m_ffn2_w_up=m_ffn2_w_up, m_ffn2_w_down=m_ffn2_w_down, m_ffn2_post_g=m_ffn2_post_g, m_ple_pre_g=m_ple_pre_g, m_ple_w_gate=m_ple_w_gate, m_ple_w_proj=m_ple_w_proj, m_ple_post_g=m_ple_post_g, v_ffn1_pre_g=v_ffn1_pre_g, v_ffn1_w_gate=v_ffn1_w_gate, v_ffn1_w_up=v_ffn1_w_up, v_ffn1_w_down=v_ffn1_w_down, v_ffn1_post_g=v_ffn1_post_g, v_mix_pre_g=v_mix_pre_g, v_w_in=v_w_in, v_sgu_norm_g=v_sgu_norm_g, v_sgu_w=v_sgu_w, v_sgu_b=v_sgu_b, v_pool_w=v_pool_w, v_pool_scale=v_pool_scale, v_w_out_a=v_w_out_a, v_w_out_b=v_w_out_b, v_w_o=v_w_o, v_mix_post_g=v_mix_post_g, v_ffn2_pre_g=v_ffn2_pre_g, v_ffn2_w_gate=v_ffn2_w_gate, v_ffn2_w_up=v_ffn2_w_up, v_ffn2_w_down=v_ffn2_w_down, v_ffn2_post_g=v_ffn2_post_g, v_ple_pre_g=v_ple_pre_g, v_ple_w_gate=v_ple_w_gate, v_ple_w_proj=v_ple_w_proj, v_ple_post_g=v_ple_post_g)
    weights = {n: given[n] for n in TWIN_WEIGHTS}
    shared = {n: given[n] for n in SHARED_INPUTS}
    per_example = {n: given[n] for n in ['x', 'p']}
    grad_fn = _jax.value_and_grad(_loss, argnums=(0, 1))

    def one_microbatch(ex, loss_target):
        ex = dict(ex)
        diff = ex.pop(TWIN_DIFF_INPUT)
        return grad_fn(weights, diff, {**shared, **ex}, loss_target)

    if N_MICROBATCH == 1:
        loss, (grad_w, grad_x) = one_microbatch(per_example, given["loss_target"])
    else:
        def body(carry, xs):
            loss_sum, grad_sum = carry
            l_k, (gw_k, gx_k) = one_microbatch(xs[0], xs[1])
            with _jax.named_scope("update"):
                return (loss_sum + l_k, _jax.tree.map(_jnp.add, grad_sum, gw_k)), gx_k

        init = (_jnp.zeros((), _jnp.float32), _jax.tree.map(_jnp.zeros_like, weights))
        (loss, grad_w), grad_x = _jax.lax.scan(body, init, (per_example, given["loss_target"]))
    with _jax.named_scope("update"):
        delta_w, new_m, new_v = {}, {}, {}
        for n in TWIN_WEIGHTS:
            delta_w[n], new_m[n], new_v[n] = _adamw(weights[n], grad_w[n], given["m_" + n], given["v_" + n])
    return (loss, grad_x, *[grad_w[n] for n in TWIN_WEIGHTS], *[delta_w[n] for n in TWIN_WEIGHTS],
            *[new_m[n] for n in TWIN_WEIGHTS], *[new_v[n] for n in TWIN_WEIGHTS])
```

```python
import functools

import jax
import jax.numpy as jnp
from jax import lax
from jax.experimental import pallas as pl
from jax.experimental.pallas import tpu as pltpu

F32 = jnp.float32
BF16 = jnp.bfloat16
EPS = 1e-6
CHUNK = 128
POOL_WINDOWS = (2, 4, 8, 16)
HALO = 16
N_CHIPS = 4
ADAM_LR, ADAM_B1, ADAM_B2, ADAM_EPS, ADAM_WD, ADAM_STEP = 0.001, 0.9, 0.999, 1e-08, 0.01, 10
VMEM_LIMIT_V7X = 58 * 1024 * 1024
MESH_IDS = pl.DeviceIdType.MESH
HBM_SPEC = pl.BlockSpec(memory_space=pltpu.HBM)
VMEM_SPEC = pl.BlockSpec(memory_space=pltpu.VMEM)
OTHER_CHIPS = ((1, 0), (0, 1), (1, 1))

NT = (((1,), (1,)), ((), ()))
TN = (((0,), (0,)), ((), ()))


def _params(*sem):
    return pltpu.CompilerParams(dimension_semantics=sem or None, vmem_limit_bytes=VMEM_LIMIT_V7X)


def _tile(t, want):
    return max(c for c in range(8, min(t, want) + 1, 8) if t % c == 0)


def _const(shape):
    return pl.BlockSpec(shape, lambda *_: (0,) * len(shape))


def _rows(tm, d, col=0):
    return pl.BlockSpec((tm, d), lambda i: (i, col))


def _kmajor(nk, tm, kb):
    return pl.BlockSpec((nk, tm, kb), lambda i: (0, i, 0))


def _dot(a, b):
    return jnp.dot(a, b, preferred_element_type=F32)


def _dot_nt(a, b):
    return lax.dot_general(a, b, NT, preferred_element_type=F32)


def _dot_tn(a, b):
    return lax.dot_general(a, b, TN, preferred_element_type=F32)


def _gelu(x):
    return 0.5 * x * (1.0 + jnp.tanh(0.7978845608028654 * (x + 0.044715 * x * x * x)))


def _gelu_grad(x):
    x2 = x * x
    th = jnp.tanh(0.7978845608028654 * (x + 0.044715 * x * x2))
    return 0.5 * (1.0 + th) + 0.5 * x * (1.0 - th * th) * 0.7978845608028654 * (1.0 + 3.0 * 0.044715 * x2)


def _sigmoid(x):
    return 1.0 / (1.0 + jnp.exp(-x))


def _rstd(h):
    return lax.rsqrt(jnp.mean(h * h, axis=-1, keepdims=True) + EPS)


def _rms_bwd(h, g, dy):
    r = _rstd(h)
    t = dy * g
    dh = r * t - h * (r * r * r) * jnp.mean(h * t, axis=-1, keepdims=True)
    return dh, jnp.sum(dy * h * r, axis=0, keepdims=True)


def _accumulate(ref, value):
    @pl.when(pl.program_id(0) == 0)
    def _():
        ref[...] = jnp.zeros_like(ref)

    ref[...] += value


def rms_cast(h, g, name):
    t, d = h.shape
    tm = _tile(t, 512)

    def body(h_ref, g_ref, o_ref):
        hv = h_ref[...]
        o_ref[...] = (hv * _rstd(hv) * g_ref[...]).astype(BF16)

    return pl.pallas_call(
        body, name=name, grid=(t // tm,), in_specs=[_rows(tm, d), _const((1, d))], out_specs=_rows(tm, d),
        out_shape=jax.ShapeDtypeStruct((t, d), BF16), compiler_params=_params("parallel"))(h, g)


def ffn_gu(xn, wg4, wu4, name):
    t, d = xn.shape
    nk, _, fk = wg4.shape
    tm = _tile(t, 512)

    def body(x_ref, wg_ref, wu_ref, g_ref, u_ref, a_ref):
        xv = x_ref[...]
        g = _dot(xv, wg_ref[0])
        u = _dot(xv, wu_ref[0])
        g_ref[0] = g.astype(BF16)
        u_ref[0] = u.astype(BF16)
        a_ref[0] = (g * _sigmoid(g) * u).astype(BF16)

    w_spec = pl.BlockSpec((1, d, fk), lambda k, i: (k, 0, 0))
    o_spec = pl.BlockSpec((1, tm, fk), lambda k, i: (k, i, 0))
    shape = jax.ShapeDtypeStruct((nk, t, fk), BF16)
    return pl.pallas_call(
        body, name=name, grid=(nk, t // tm), in_specs=[pl.BlockSpec((tm, d), lambda k, i: (i, 0)), w_spec, w_spec],
        out_specs=[o_spec] * 3, out_shape=[shape] * 3, compiler_params=_params("parallel", "parallel"))(xn, wg4, wu4)


def mm_norm_res(a3, w3, h_old, g_post, g_next, scale, name):
    nk, t, kb = a3.shape
    d = w3.shape[2]
    tm = _tile(t, 256)

    def body(a_ref, w_ref, h_ref, gp_ref, gn_ref, f_ref, hn_ref, xn_ref):
        f = _dot(a_ref[0], w_ref[0])
        for k in range(1, nk):
            f += _dot(a_ref[k], w_ref[k])
        f_ref[...] = f
        hn = h_ref[...] + scale * (f * _rstd(f) * gp_ref[...])
        hn_ref[...] = hn
        xn_ref[...] = (hn * _rstd(hn) * gn_ref[...]).astype(BF16)

    return pl.pallas_call(
        body, name=name, grid=(t // tm,),
        in_specs=[_kmajor(nk, tm, kb), _const((nk, kb, d)), _rows(tm, d), _const((1, d)), _const((1, d))],
        out_specs=[_rows(tm, d)] * 3,
        out_shape=[jax.ShapeDtypeStruct((t, d), F32), jax.ShapeDtypeStruct((t, d), F32), jax.ShapeDtypeStruct((t, d), BF16)],
        compiler_params=_params("parallel"))(a3, w3, h_old, g_post, g_next)


def mixer_in(xn, win4, name):
    t, d = xn.shape
    nk, _, nb = win4.shape
    tm = _tile(t, 512)

    def body(x_ref, w_ref, z_ref):
        z_ref[...] = _dot(x_ref[...], w_ref[0]).astype(BF16)

    return pl.pallas_call(
        body, name=name, grid=(nk, t // tm),
        in_specs=[pl.BlockSpec((tm, d), lambda k, i: (i, 0)), pl.BlockSpec((1, d, nb), lambda k, i: (k, 0, 0))],
        out_specs=pl.BlockSpec((tm, nb), lambda k, i: (i, k)), out_shape=jax.ShapeDtypeStruct((t, nk * nb), BF16),
        compiler_params=_params("parallel", "parallel"))(xn, win4)


def _causal_mask():
    row = lax.broadcasted_iota(jnp.int32, (CHUNK, CHUNK), 0)
    col = lax.broadcasted_iota(jnp.int32, (CHUNK, CHUNK), 1)
    return row >= col


def _layernorm_parts(v):
    mu = jnp.mean(v, axis=-1, keepdims=True)
    vc = v - mu
    r = lax.rsqrt(jnp.mean(vc * vc, axis=-1, keepdims=True) + EPS)
    return vc * r, r


def sgu_fwd(z, norm_g, sgu_w, sgu_b3, name):
    t = z.shape[0]
    d = norm_g.shape[1]
    ng = sgu_w.shape[0]
    dg = d // ng
    tm = _tile(t, 256)

    def body(zu_ref, zv_ref, ng_ref, w_ref, b_ref, a_ref):
        vhat, _ = _layernorm_parts(_gelu(zv_ref[...].astype(F32)))
        vn = (vhat * ng_ref[...]).astype(BF16)
        u = _gelu(zu_ref[...].astype(F32))
        mask = _causal_mask()
        for g in range(ng):
            wg = jnp.where(mask, w_ref[g], 0.0).astype(BF16)
            for ci in range(tm // CHUNK):
                rs, cs = slice(ci * CHUNK, (ci + 1) * CHUNK), slice(g * dg, (g + 1) * dg)
                sv = _dot(wg, vn[rs, cs]) + b_ref[g]
                a_ref[rs, cs] = (u[rs, cs] * sv).astype(BF16)

    return pl.pallas_call(
        body, name=name, grid=(t // tm,),
        in_specs=[_rows(tm, d, 0), _rows(tm, d, 1), _const((1, d)), _const((ng, CHUNK, CHUNK)), _const((ng, CHUNK, 1))],
        out_specs=_rows(tm, d), out_shape=jax.ShapeDtypeStruct((t, d), BF16),
        compiler_params=_params("parallel"))(z, z, norm_g, sgu_w, sgu_b3)


def pool_fwd(z, pool_w, pool_scale, name):
    t = z.shape[0]
    d = pool_scale.shape[1]
    ng = pool_w.shape[0]
    dg = d // ng
    tm = _tile(t, 256)
    per = tm // HALO

    def body(c_ref, prev_ref, w_ref, s_ref, diff_ref, b_ref):
        i = pl.program_id(0)
        cur = c_ref[...].astype(F32)
        prev = jnp.where(i > 0, prev_ref[...].astype(F32), 0.0)
        ext = jnp.concatenate([prev, cur], axis=0)
        tok = i * tm + lax.broadcasted_iota(jnp.int32, (tm, 1), 0)
        for g, win in enumerate(POOL_WINDOWS):
            cs = slice(g * dg, (g + 1) * dg)
            s = ext[:, cs]
            sh = 1
            while sh < win:
                s = s + pltpu.roll(s, sh, 0)
                sh *= 2
            cnt = jnp.minimum(tok + 1, win).astype(F32)
            diff = (s[HALO:] / cnt - cur[:, cs]).astype(BF16)
            diff_ref[:, cs] = diff
            b_ref[:, cs] = (_dot(diff, w_ref[g]) * s_ref[:, cs]).astype(BF16)

    return pl.pallas_call(
        body, name=name, grid=(t // tm,),
        in_specs=[_rows(tm, d, 2), pl.BlockSpec((HALO, d), lambda i: (jnp.maximum(i * per - 1, 0), 2)),
                  _const((ng, dg, dg)), _const((1, d))],
        out_specs=[_rows(tm, d)] * 2, out_shape=[jax.ShapeDtypeStruct((t, d), BF16)] * 2,
        compiler_params=_params("parallel"))(z, z, pool_w, pool_scale)


def mixer_y(a, b, z, woa, wob, name):
    t, d = a.shape
    tm = _tile(t, 256)

    def body(a_ref, b_ref, ga_ref, gb_ref, wa_ref, wb_ref, ya_ref, yb_ref, y_ref):
        ya = _dot(a_ref[...], wa_ref[...])
        yb = _dot(b_ref[...], wb_ref[...])
        ya_ref[...] = ya.astype(BF16)
        yb_ref[...] = yb.astype(BF16)
        y_ref[...] = (_sigmoid(ga_ref[...].astype(F32)) * ya + _sigmoid(gb_ref[...].astype(F32)) * yb).astype(BF16)

    return pl.pallas_call(
        body, name=name, grid=(t // tm,),
        in_specs=[_rows(tm, d), _rows(tm, d), _rows(tm, d, 3), _rows(tm, d, 4), _const((d, d)), _const((d, d))],
        out_specs=[_rows(tm, d)] * 3, out_shape=[jax.ShapeDtypeStruct((t, d), BF16)] * 3,
        compiler_params=_params("parallel"))(a, b, z, z, woa, wob)


def ple_fwd(xn, p, wpg, wpp, h, g_post, target, name):
    t, d = xn.shape
    dp = p.shape[1]
    tm = _tile(t, 256)

    def body(x_ref, p_ref, wg_ref, wp_ref, h_ref, g_ref, tg_ref, gate_ref, e_ref, q_ref, dh_ref, loss_ref):
        gate = _sigmoid(_dot(x_ref[...], wg_ref[...]))
        e = _dot(p_ref[...], wp_ref[...])
        q = gate * e
        gate_ref[...] = gate.astype(BF16)
        e_ref[...] = e.astype(BF16)
        q_ref[...] = q
        err = h_ref[...] + q * _rstd(q) * g_ref[...] - tg_ref[...]
        dh_ref[...] = err * (1.0 / d)
        _accumulate(loss_ref, jnp.full(loss_ref.shape, (0.5 / d) * jnp.sum(err * err), F32))

    return pl.pallas_call(
        body, name=name, grid=(t // tm,),
        in_specs=[_rows(tm, d), _rows(tm, dp), _const((d, d)), _const((dp, d)), _rows(tm, d), _const((1, d)), _rows(tm, d)],
        out_specs=[_rows(tm, d)] * 4 + [_const((8, 128))],
        out_shape=[jax.ShapeDtypeStruct((t, d), BF16), jax.ShapeDtypeStruct((t, d), BF16), jax.ShapeDtypeStruct((t, d), F32),
                   jax.ShapeDtypeStruct((t, d), F32), jax.ShapeDtypeStruct((8, 128), F32)],
        compiler_params=_params("arbitrary"))(xn, p, wpg, wpp, h, g_post, target)


def ple_bwd(dh, q, gate, e, g_post, wpg, h, g_pre, name):
    t, d = dh.shape
    tm = _tile(t, 256)

    def body(dh_ref, q_ref, gate_ref, e_ref, gp_ref, w_ref, h_ref, gn_ref, ds_ref, de_ref, dhp_ref, dgp_ref, dgn_ref):
        dhv = dh_ref[...]
        dq, dgp = _rms_bwd(q_ref[...], gp_ref[...], dhv)
        gate = gate_ref[...].astype(F32)
        ds = (dq * e_ref[...].astype(F32) * gate * (1.0 - gate)).astype(BF16)
        ds_ref[...] = ds
        de_ref[...] = (dq * gate).astype(BF16)
        dx, dgn = _rms_bwd(h_ref[...], gn_ref[...], _dot_nt(ds, w_ref[...]))
        dhp_ref[...] = dhv + dx
        _accumulate(dgp_ref, dgp)
        _accumulate(dgn_ref, dgn)

    return pl.pallas_call(
        body, name=name, grid=(t // tm,),
        in_specs=[_rows(tm, d)] * 4 + [_const((1, d)), _const((d, d)), _rows(tm, d), _const((1, d))],
        out_specs=[_rows(tm, d)] * 3 + [_const((1, d))] * 2,
        out_shape=[jax.ShapeDtypeStruct((t, d), BF16), jax.ShapeDtypeStruct((t, d), BF16), jax.ShapeDtypeStruct((t, d), F32),
                   jax.ShapeDtypeStruct((1, d), F32), jax.ShapeDtypeStruct((1, d), F32)],
        compiler_params=_params("arbitrary"))(dh, q, gate, e, g_post, wpg, h, g_pre)


def ffn_bwd_a(dh, f, g_post, wd4, g4, u4, scale, name):
    t, d = dh.shape
    nk, fk, _ = wd4.shape
    tm = _tile(t, 256)

    def body(dh_ref, f_ref, gp_ref, w_ref, g_ref, u_ref, df_ref, dg_ref, du_ref, dgp_ref):
        df, dgp = _rms_bwd(f_ref[...], gp_ref[...], dh_ref[...])
        df = (scale * df).astype(BF16)
        df_ref[...] = df
        _accumulate(dgp_ref, scale * dgp)
        for k in range(nk):
            da = _dot_nt(df, w_ref[k])
            g = g_ref[k].astype(F32)
            sg = _sigmoid(g)
            du_ref[k] = (da * g * sg).astype(BF16)
            dg_ref[k] = (da * u_ref[k].astype(F32) * sg * (1.0 + g * (1.0 - sg))).astype(BF16)

    return pl.pallas_call(
        body, name=name, grid=(t // tm,),
        in_specs=[_rows(tm, d), _rows(tm, d), _const((1, d)), _const((nk, fk, d)), _kmajor(nk, tm, fk), _kmajor(nk, tm, fk)],
        out_specs=[_rows(tm, d), _kmajor(nk, tm, fk), _kmajor(nk, tm, fk), _const((1, d))],
        out_shape=[jax.ShapeDtypeStruct((t, d), BF16), jax.ShapeDtypeStruct((nk, t, fk), BF16),
                   jax.ShapeDtypeStruct((nk, t, fk), BF16), jax.ShapeDtypeStruct((1, d), F32)],
        compiler_params=_params("arbitrary"))(dh, f, g_post, wd4, g4, u4)


def dx_norm_bwd(pairs, h, g_pre, dh_in, name):
    t, d = h.shape
    tm = _tile(t, 256)
    n = len(pairs)

    def body(*refs):
        dys, ws = refs[:n], refs[n:2 * n]
        h_ref, g_ref, dhi_ref, dho_ref, dg_ref = refs[2 * n:]
        acc = None
        for (_, w4, columns), dy_ref, w_ref in zip(pairs, dys, ws):
            nk, _, kb = w4.shape
            for k in range(nk):
                part = _dot_nt(dy_ref[:, k * kb:(k + 1) * kb] if columns else dy_ref[k], w_ref[k])
                acc = part if acc is None else acc + part
        dx, dg = _rms_bwd(h_ref[...], g_ref[...], acc)
        dho_ref[...] = dhi_ref[...] + dx
        _accumulate(dg_ref, dg)

    dy_specs = [_rows(tm, dy.shape[1]) if columns else _kmajor(dy.shape[0], tm, dy.shape[2]) for dy, _, columns in pairs]
    return pl.pallas_call(
        body, name=name, grid=(t // tm,),
        in_specs=dy_specs + [_const(w4.shape) for _, w4, _ in pairs] + [_rows(tm, d), _const((1, d)), _rows(tm, d)],
        out_specs=[_rows(tm, d), _const((1, d))],
        out_shape=[jax.ShapeDtypeStruct((t, d), F32), jax.ShapeDtypeStruct((1, d), F32)],
        compiler_params=_params("arbitrary"))(*[dy for dy, _, _ in pairs], *[w4 for _, w4, _ in pairs], h, g_pre, dh_in)


def mixer_bwd_y(dh, m, g_post, w_o, ya, yb, z, woa, wob, name):
    t, d = dh.shape
    tm = _tile(t, 256)

    def body(dh_ref, m_ref, gp_ref, wo_ref, ya_ref, yb_ref, ga_ref, gb_ref, wa_ref, wb_ref,
             dm_ref, dya_ref, dyb_ref, dga_ref, dgb_ref, da_ref, db_ref, dgp_ref):
        dm, dgp = _rms_bwd(m_ref[...], gp_ref[...], dh_ref[...])
        dm = dm.astype(BF16)
        dm_ref[...] = dm
        _accumulate(dgp_ref, dgp)
        dy = _dot_nt(dm, wo_ref[...])
        sa = _sigmoid(ga_ref[...].astype(F32))
        sb = _sigmoid(gb_ref[...].astype(F32))
        dya = (dy * sa).astype(BF16)
        dyb = (dy * sb).astype(BF16)
        dya_ref[...] = dya
        dyb_ref[...] = dyb
        dga_ref[...] = (dy * ya_ref[...].astype(F32) * sa * (1.0 - sa)).astype(BF16)
        dgb_ref[...] = (dy * yb_ref[...].astype(F32) * sb * (1.0 - sb)).astype(BF16)
        da_ref[...] = _dot_nt(dya, wa_ref[...]).astype(BF16)
        db_ref[...] = _dot_nt(dyb, wb_ref[...]).astype(BF16)

    return pl.pallas_call(
        body, name=name, grid=(t // tm,),
        in_specs=[_rows(tm, d), _rows(tm, d), _const((1, d)), _const((d, d)), _rows(tm, d), _rows(tm, d),
                  _rows(tm, d, 3), _rows(tm, d, 4), _const((d, d)), _const((d, d))],
        out_specs=[_rows(tm, d)] * 7 + [_const((1, d))],
        out_shape=[jax.ShapeDtypeStruct((t, d), BF16)] * 7 + [jax.ShapeDtypeStruct((1, d), F32)],
        compiler_params=_params("arbitrary"))(dh, m, g_post, w_o, ya, yb, z, z, woa, wob)


def sgu_bwd(z, da, norm_g, sgu_w, sgu_b3, name):
    t, d = da.shape
    ng = sgu_w.shape[0]
    dg = d // ng
    tm = _tile(t, 256)
    steps = t // tm

    def body(zu_ref, zv_ref, da_ref, ng_ref, w_ref, b_ref, dzu_ref, dzv_ref, dw_ref, db_ref, dng_ref, dvn_ref, dsv_ref):
        i = pl.program_id(0)
        zv = zv_ref[...].astype(F32)
        zu = zu_ref[...].astype(F32)
        vhat, r = _layernorm_parts(_gelu(zv))
        gain = ng_ref[...]
        vn = (vhat * gain).astype(BF16)
        u = _gelu(zu)
        dav = da_ref[...].astype(F32)
        gu = _gelu_grad(zu)
        mask = _causal_mask()

        @pl.when(i == 0)
        def _():
            dw_ref[...] = jnp.zeros_like(dw_ref)
            dsv_ref[...] = jnp.zeros_like(dsv_ref)

        for g in range(ng):
            wg = jnp.where(mask, w_ref[g], 0.0).astype(BF16)
            dw = jnp.zeros((CHUNK, CHUNK), F32)
            dsv_sum = jnp.zeros((CHUNK, dg), F32)
            for ci in range(tm // CHUNK):
                rs, cs = slice(ci * CHUNK, (ci + 1) * CHUNK), slice(g * dg, (g + 1) * dg)
                vn_blk = vn[rs, cs]
                sv = _dot(wg, vn_blk) + b_ref[g]
                dzu_ref[rs, cs] = (dav[rs, cs] * sv * gu[rs, cs]).astype(BF16)
                dsv = dav[rs, cs] * u[rs, cs]
                dsv_sum += dsv
                dsv = dsv.astype(BF16)
                dw += _dot_nt(dsv, vn_blk)
                dvn_ref[rs, cs] = _dot_tn(wg, dsv)
            dw_ref[g] += dw
            dsv_ref[:, cs] += dsv_sum

        dvn = dvn_ref[...]
        _accumulate(dng_ref, jnp.sum(dvn * vhat, axis=0, keepdims=True))
        dvh = dvn * gain
        dv = r * (dvh - jnp.mean(dvh, axis=-1, keepdims=True) - vhat * jnp.mean(dvh * vhat, axis=-1, keepdims=True))
        dzv_ref[...] = (dv * _gelu_grad(zv)).astype(BF16)

        @pl.when(i == steps - 1)
        def _():
            for g in range(ng):
                dw_ref[g] = jnp.where(mask, dw_ref[g], 0.0)
                row_sum = jnp.sum(dsv_ref[:, g * dg:(g + 1) * dg], axis=1, keepdims=True)
                db_ref[g] = jnp.broadcast_to(row_sum, (CHUNK, CHUNK))

    return pl.pallas_call(
        body, name=name, grid=(steps,),
        in_specs=[_rows(tm, d, 0), _rows(tm, d, 1), _rows(tm, d), _const((1, d)), _const((ng, CHUNK, CHUNK)), _const((ng, CHUNK, 1))],
        out_specs=[_rows(tm, d), _rows(tm, d), _const((ng, CHUNK, CHUNK)), _const((ng, CHUNK, CHUNK)), _const((1, d))],
        out_shape=[jax.ShapeDtypeStruct((t, d), BF16), jax.ShapeDtypeStruct((t, d), BF16),
                   jax.ShapeDtypeStruct((ng, CHUNK, CHUNK), F32), jax.ShapeDtypeStruct((ng, CHUNK, CHUNK), F32),
                   jax.ShapeDtypeStruct((1, d), F32)],
        scratch_shapes=[pltpu.VMEM((tm, d), F32), pltpu.VMEM((CHUNK, d), F32)],
        compiler_params=_params("arbitrary"))(z, z, da, norm_g, sgu_w, sgu_b3)


def pool_bwd(db, diff, pool_w, pool_scale, name):
    t, d = db.shape
    ng = pool_w.shape[0]
    dg = d // ng
    tm = _tile(t, 256)
    per = tm // HALO
    steps = t // tm

    def body(db_ref, next_ref, diff_ref, w_ref, s_ref, dc_ref, dw_ref, ds_ref):
        i = pl.program_id(0)
        dbc = db_ref[...].astype(F32)
        nxt = jnp.where(i < steps - 1, next_ref[...].astype(F32), 0.0)
        ext = jnp.concatenate([dbc, nxt], axis=0)
        rows = tm + HALO
        tok = i * tm + lax.broadcasted_iota(jnp.int32, (rows, 1), 0)

        @pl.when(i == 0)
        def _():
            dw_ref[...] = jnp.zeros_like(dw_ref)
            ds_ref[...] = jnp.zeros_like(ds_ref)

        for g, win in enumerate(POOL_WINDOWS):
            cs = slice(g * dg, (g + 1) * dg)
            dp = (ext[:, cs] * s_ref[:, cs]).astype(BF16)
            dd = _dot_nt(dp, w_ref[g])
            s = dd / jnp.minimum(tok + 1, win).astype(F32)
            sh = 1
            while sh < win:
                s = s + pltpu.roll(s, rows - sh, 0)
                sh *= 2
            dc_ref[:, cs] = (s[:tm] - dd[:tm]).astype(BF16)
            dfg = diff_ref[:, cs]
            ds_ref[:, cs] += jnp.sum(dbc[:, cs] * _dot(dfg, w_ref[g]), axis=0, keepdims=True)
            dw_ref[g] += _dot_tn(dfg, dp[:tm])

    return pl.pallas_call(
        body, name=name, grid=(steps,),
        in_specs=[_rows(tm, d), pl.BlockSpec((HALO, d), lambda i: (jnp.minimum((i + 1) * per, t // HALO - 1), 0)),
                  _rows(tm, d), _const((ng, dg, dg)), _const((1, d))],
        out_specs=[_rows(tm, d), _const((ng, dg, dg)), _const((1, d))],
        out_shape=[jax.ShapeDtypeStruct((t, d), BF16), jax.ShapeDtypeStruct((ng, dg, dg), F32), jax.ShapeDtypeStruct((1, d), F32)],
        compiler_params=_params("arbitrary"))(db, db, diff, pool_w, pool_scale)


def dw_tn(x, dy, nk, name, x_kmajor=False, dy_mode="same"):
    t = x.shape[-2]
    kx = x.shape[-1]
    n = dy.shape[-1] // nk if dy_mode == "cols" else dy.shape[-1]
    tt = _tile(t, 512)
    steps = t // tt

    def body(x_ref, dy_ref, o_ref, acc_ref):
        s = pl.program_id(1)

        @pl.when(s == 0)
        def _():
            acc_ref[...] = jnp.zeros_like(acc_ref)

        acc_ref[...] += _dot_tn(x_ref[0] if x_kmajor else x_ref[...], dy_ref[0] if dy_mode == "kmajor" else dy_ref[...])

        @pl.when(s == steps - 1)
        def _():
            o_ref[0] = acc_ref[...].astype(BF16)

    x_spec = pl.BlockSpec((1, tt, kx), lambda k, s: (k, s, 0)) if x_kmajor else pl.BlockSpec((tt, kx), lambda k, s: (s, 0))
    dy_spec = {"kmajor": pl.BlockSpec((1, tt, n), lambda k, s: (k, s, 0)), "cols": pl.BlockSpec((tt, n), lambda k, s: (s, k)),
               "same": pl.BlockSpec((tt, n), lambda k, s: (s, 0))}[dy_mode]
    return pl.pallas_call(
        body, name=name, grid=(nk, steps), in_specs=[x_spec, dy_spec],
        out_specs=pl.BlockSpec((1, kx, n), lambda k, s: (k, 0, 0)), out_shape=jax.ShapeDtypeStruct((nk, kx, n), BF16),
        scratch_shapes=[pltpu.VMEM((kx, n), F32)], compiler_params=_params("parallel", "arbitrary"))(x, dy)


def _place():
    x, y, c = lax.axis_index("x"), lax.axis_index("y"), lax.axis_index("c")
    chips = [((1 - x) if fx else x, (1 - y) if fy else y) for fx, fy in OTHER_CHIPS]
    return x, y, c, chips


def _half(rows, which):
    return pl.ds(pl.multiple_of(which * (rows // 2), 16), rows // 2)


def allgather_weights(shards):
    n = len(shards)

    def body(*refs):
        ins, outs = refs[:n], refs[n:2 * n]
        local_sems, ici_send, ici_recv, d2d_send, d2d_recv = refs[2 * n:]
        x, y, c, chips = _place()
        me = 2 * x + y
        started = []
        for i in range(n):
            own = pltpu.make_async_copy(ins[i], outs[i].at[me], local_sems.at[i])
            own.start()
            started.append(own)
        sends = []
        for i in range(n):
            mine = _half(ins[i].shape[0], c)
            for j, (cx, cy) in enumerate(chips):
                cp = pltpu.make_async_remote_copy(
                    src_ref=ins[i].at[mine], dst_ref=outs[i].at[me, mine], send_sem=ici_send.at[i, j],
                    recv_sem=ici_recv.at[i, j], device_id=(cx, cy, c), device_id_type=MESH_IDS)
                cp.start()
                sends.append(cp)
        for i in range(n):
            mine = _half(ins[i].shape[0], c)
            for j, (cx, cy) in enumerate(chips):
                landed = outs[i].at[2 * cx + cy, mine]
                pltpu.make_async_remote_copy(
                    src_ref=landed, dst_ref=landed, send_sem=ici_send.at[i, j], recv_sem=ici_recv.at[i, j],
                    device_id=(cx, cy, c), device_id_type=MESH_IDS).wait_recv()
                cp = pltpu.make_async_remote_copy(
                    src_ref=landed, dst_ref=landed, send_sem=d2d_send.at[i, j], recv_sem=d2d_recv.at[i, j],
                    device_id=(x, y, 1 - c), device_id_type=MESH_IDS)
                cp.start()
                sends.append(cp)
        for i in range(n):
            theirs = _half(ins[i].shape[0], 1 - c)
            for j, (cx, cy) in enumerate(chips):
                passed = outs[i].at[2 * cx + cy, theirs]
                pltpu.make_async_remote_copy(
                    src_ref=passed, dst_ref=passed, send_sem=d2d_send.at[i, j], recv_sem=d2d_recv.at[i, j],
                    device_id=(x, y, 1 - c), device_id_type=MESH_IDS).wait_recv()
        for cp in sends:
            cp.wait_send()
        for own in started:
            own.wait()

    return pl.pallas_call(
        body, name="allgather_weights", in_specs=[HBM_SPEC] * n, out_specs=[HBM_SPEC] * n,
        out_shape=[jax.ShapeDtypeStruct((N_CHIPS,) + s.shape, s.dtype) for s in shards],
        scratch_shapes=[pltpu.SemaphoreType.DMA((n,))] + [pltpu.SemaphoreType.DMA((n, 3))] * 4,
        compiler_params=pltpu.CompilerParams(has_side_effects=True))(*shards)


def exchange_halves(grads):
    n = len(grads)

    def body(*refs):
        ins, outs = refs[:n], refs[n:2 * n]
        send_sems, recv_sems = refs[2 * n:]
        x, y, c, _ = _place()
        copies = []
        for i in range(n):
            cp = pltpu.make_async_remote_copy(
                src_ref=ins[i].at[:, _half(ins[i].shape[1], 1 - c)], dst_ref=outs[i], send_sem=send_sems.at[i],
                recv_sem=recv_sems.at[i], device_id=(x, y, 1 - c), device_id_type=MESH_IDS)
            cp.start()
            copies.append(cp)
        for cp in copies:
            cp.wait()

    return pl.pallas_call(
        body, name="exchange_halves", in_specs=[HBM_SPEC] * n, out_specs=[HBM_SPEC] * n,
        out_shape=[jax.ShapeDtypeStruct((g.shape[0], g.shape[1] // 2, g.shape[2]), g.dtype) for g in grads],
        scratch_shapes=[pltpu.SemaphoreType.DMA((n,))] * 2,
        compiler_params=pltpu.CompilerParams(has_side_effects=True))(*grads)


def scatter_to_owners(sums):
    n = len(sums)

    def body(*refs):
        ins, outs = refs[:n], refs[n:2 * n]
        send_sems, recv_sems = refs[2 * n:]
        _, _, c, chips = _place()
        copies = []
        for i in range(n):
            for j, (cx, cy) in enumerate(chips):
                cp = pltpu.make_async_remote_copy(
                    src_ref=ins[i].at[2 * cx + cy], dst_ref=outs[i].at[j], send_sem=send_sems.at[i, j],
                    recv_sem=recv_sems.at[i, j], device_id=(cx, cy, c), device_id_type=MESH_IDS)
                cp.start()
                copies.append(cp)
        for cp in copies:
            cp.wait()

    return pl.pallas_call(
        body, name="scatter_to_owners", in_specs=[HBM_SPEC] * n, out_specs=[HBM_SPEC] * n,
        out_shape=[jax.ShapeDtypeStruct((3,) + s.shape[1:], s.dtype) for s in sums],
        scratch_shapes=[pltpu.SemaphoreType.DMA((n, 3))] * 2,
        compiler_params=pltpu.CompilerParams(has_side_effects=True))(*sums)


def share_with_sibling(halves):
    n = len(halves)

    def body(*refs):
        ins, outs = refs[:n], refs[n:2 * n]
        local_sems, send_sems, recv_sems = refs[2 * n:]
        x, y, c, _ = _place()
        copies = []
        for i in range(n):
            own = pltpu.make_async_copy(ins[i], outs[i].at[c], local_sems.at[i])
            own.start()
            cp = pltpu.make_async_remote_copy(
                src_ref=ins[i], dst_ref=outs[i].at[c], send_sem=send_sems.at[i], recv_sem=recv_sems.at[i],
                device_id=(x, y, 1 - c), device_id_type=MESH_IDS)
            cp.start()
            copies += [own, cp]
        for cp in copies:
            cp.wait()

    return pl.pallas_call(
        body, name="share_with_sibling", in_specs=[HBM_SPEC] * n, out_specs=[HBM_SPEC] * n,
        out_shape=[jax.ShapeDtypeStruct((2,) + h.shape, h.dtype) for h in halves],
        scratch_shapes=[pltpu.SemaphoreType.DMA((n,))] * 3,
        compiler_params=pltpu.CompilerParams(has_side_effects=True))(*halves)


def add_halves(grad4, recv4, core, name):
    nk, r, cdim = grad4.shape
    half = r // 2
    view = grad4.reshape(nk, 2, half, cdim)

    def body(core_ref, g_ref, r_ref, o_ref):
        o_ref[0] = (g_ref[0, 0].astype(F32) + r_ref[0].astype(F32)).astype(BF16)

    return pl.pallas_call(
        body, name=name,
        grid_spec=pltpu.PrefetchScalarGridSpec(
            num_scalar_prefetch=1, grid=(nk,),
            in_specs=[pl.BlockSpec((1, 1, half, cdim), lambda k, core_ref: (k, core_ref[0], 0, 0)),
                      pl.BlockSpec((1, half, cdim), lambda k, core_ref: (k, 0, 0))],
            out_specs=pl.BlockSpec((1, half, cdim), lambda k, core_ref: (k, 0, 0))),
        out_shape=jax.ShapeDtypeStruct((nk, half, cdim), BF16), compiler_params=_params("parallel"))(core, view, recv4)


def add_chips(sum4, recv3, chip, name):
    _, half, cdim = sum4.shape

    def body(chip_ref, s_ref, r_ref, o_ref):
        o_ref[...] = ((s_ref[0].astype(F32) + r_ref[0].astype(F32)) + r_ref[1].astype(F32)) + r_ref[2].astype(F32)

    return pl.pallas_call(
        body, name=name,
        grid_spec=pltpu.PrefetchScalarGridSpec(
            num_scalar_prefetch=1, grid=(1,),
            in_specs=[pl.BlockSpec((1, half, cdim), lambda k, chip_ref: (chip_ref[0], 0, 0)),
                      pl.BlockSpec((3, half, cdim), lambda k, chip_ref: (0, 0, 0))],
            out_specs=pl.BlockSpec((half, cdim), lambda k, chip_ref: (0, 0))),
        out_shape=jax.ShapeDtypeStruct((half, cdim), F32), compiler_params=_params("arbitrary"))(chip, sum4, recv3)


def _adamw_math(w, g, m, v):
    m = ADAM_B1 * m + (1.0 - ADAM_B1) * g
    v = ADAM_B2 * v + (1.0 - ADAM_B2) * (g * g)
    m_hat = m / (1.0 - ADAM_B1 ** ADAM_STEP)
    v_hat = v / (1.0 - ADAM_B2 ** ADAM_STEP)
    return -ADAM_LR * (m_hat / (jnp.sqrt(v_hat) + ADAM_EPS) + ADAM_WD * w), m, v


def adamw(w, g, m, v, name):
    r, cdim = w.shape
    tr = _tile(r, 256)

    def body(w_ref, g_ref, m_ref, v_ref, d_ref, mo_ref, vo_ref):
        d_ref[...], mo_ref[...], vo_ref[...] = _adamw_math(w_ref[...], g_ref[...], m_ref[...], v_ref[...])

    spec = pl.BlockSpec((tr, cdim), lambda i: (i, 0))
    return pl.pallas_call(
        body, name=name, grid=(r // tr,), in_specs=[spec] * 4, out_specs=[spec] * 3,
        out_shape=[jax.ShapeDtypeStruct((r, cdim), F32)] * 3, compiler_params=_params("parallel"))(w, g, m, v)


def small_allreduce_adamw(g, w, m, v):
    r, cdim = g.shape
    n_dev = 8

    def body(g_ref, w_ref, m_ref, v_ref, go_ref, d_ref, mo_ref, vo_ref, buf, send_sems, recv_sems):
        x, y, c = lax.axis_index("x"), lax.axis_index("y"), lax.axis_index("c")
        me = 4 * x + 2 * y + c
        buf[me] = g_ref[...]
        peers = []
        for rel in range(1, n_dev):
            fx, fy, fc = rel >> 2 & 1, rel >> 1 & 1, rel & 1
            peers.append(((1 - x) if fx else x, (1 - y) if fy else y, (1 - c) if fc else c))
        copies = []
        for k, peer in enumerate(peers):
            cp = pltpu.make_async_remote_copy(
                src_ref=g_ref, dst_ref=buf.at[me], send_sem=send_sems.at[k], recv_sem=recv_sems.at[k],
                device_id=peer, device_id_type=MESH_IDS)
            cp.start()
            copies.append(cp)
        for k, (px, py, pc) in enumerate(peers):
            slot = buf.at[4 * px + 2 * py + pc]
            pltpu.make_async_remote_copy(
                src_ref=slot, dst_ref=slot, send_sem=send_sems.at[k], recv_sem=recv_sems.at[k],
                device_id=(px, py, pc), device_id_type=MESH_IDS).wait_recv()
        for cp in copies:
            cp.wait_send()
        total = buf[0]
        for dev in range(1, n_dev):
            total = total + buf[dev]
        go_ref[...] = total
        d_ref[...], mo_ref[...], vo_ref[...] = _adamw_math(w_ref[...], total, m_ref[...], v_ref[...])

    return pl.pallas_call(
        body, name="small_allreduce_adamw", in_specs=[VMEM_SPEC] * 4, out_specs=[VMEM_SPEC] * 4,
        out_shape=[jax.ShapeDtypeStruct((r, cdim), F32)] * 4,
        scratch_shapes=[pltpu.VMEM((n_dev, r, cdim), F32), pltpu.SemaphoreType.DMA((n_dev - 1,)),
                        pltpu.SemaphoreType.DMA((n_dev - 1,))],
        compiler_params=pltpu.CompilerParams(has_side_effects=True, vmem_limit_bytes=VMEM_LIMIT_V7X))(g, w, m, v)


BIG = ("ffn1_w_gate", "ffn1_w_up", "ffn1_w_down", "w_in", "pool_w", "w_out_a", "w_out_b", "w_o",
       "ffn2_w_gate", "ffn2_w_up", "ffn2_w_down", "ple_w_gate", "ple_w_proj")
GAINS = ("ffn1_pre_g", "ffn1_post_g", "mix_pre_g", "sgu_norm_g", "pool_scale", "mix_post_g",
         "ffn2_pre_g", "ffn2_post_g", "ple_pre_g", "ple_post_g")
SMALL = GAINS + ("sgu_b", "sgu_w")
WEIGHTS = ("ffn1_pre_g", "ffn1_w_gate", "ffn1_w_up", "ffn1_w_down", "ffn1_post_g", "mix_pre_g", "w_in", "sgu_norm_g",
           "sgu_w", "sgu_b", "pool_w", "pool_scale", "w_out_a", "w_out_b", "w_o", "mix_post_g", "ffn2_pre_g",
           "ffn2_w_gate", "ffn2_w_up", "ffn2_w_down", "ffn2_post_g", "ple_pre_g", "ple_w_gate", "ple_w_proj", "ple_post_g")
SMALL_ROWS = 8


def _shard2d(a):
    return a.reshape(-1, a.shape[-1])


def _pack_small(parts):
    rows = []
    for name in SMALL:
        a = parts[name].astype(F32).reshape(-1)
        width = 1024
        n_rows = -(-a.shape[0] // width)
        padded = -(-n_rows // SMALL_ROWS) * SMALL_ROWS
        rows.append(jnp.pad(a, (0, padded * width - a.shape[0])).reshape(padded, width))
    return jnp.concatenate(rows, axis=0)


def _unpack_small(packed, like):
    out, row = {}, 0
    for name in SMALL:
        size = like[name].size
        n_rows = -(-size // 1024)
        padded = -(-n_rows // SMALL_ROWS) * SMALL_ROWS
        out[name] = packed[row:row + padded].reshape(-1)[:size].reshape(like[name].shape)
        row += padded
    return out


def _ffn_fwd(xn, h, w, pre, g_post, g_next, tag):
    g4, u4, a4 = ffn_gu(xn, w[pre + "w_gate"], w[pre + "w_up"], tag + "_gu")
    f, h_new, xn_next = mm_norm_res(a4, w[pre + "w_down"], h, g_post, g_next, 0.5, tag + "_down")
    return dict(xn=xn, h=h, g4=g4, u4=u4, a4=a4, f=f), h_new, xn_next


def _ffn_bwd(dh, saved, w, pre, g_post, g_pre, tag):
    df, dg4, du4, d_post = ffn_bwd_a(dh, saved["f"], g_post, w[pre + "w_down"], saved["g4"], saved["u4"], 0.5, tag + "_bwd_a")
    nk = N_CHIPS
    grads = {
        pre + "w_down": dw_tn(saved["a4"], df, nk, tag + "_dw_down", x_kmajor=True, dy_mode="same"),
        pre + "w_gate": dw_tn(saved["xn"], dg4, nk, tag + "_dw_gate", dy_mode="kmajor"),
        pre + "w_up": dw_tn(saved["xn"], du4, nk, tag + "_dw_up", dy_mode="kmajor"),
    }
    dh_prev, d_pre = dx_norm_bwd([(dg4, w[pre + "w_gate"], False), (du4, w[pre + "w_up"], False)], saved["h"], g_pre, dh,
                                 tag + "_bwd_x")
    return dh_prev, grads, d_post, d_pre


def kernel(x, p, ffn1_pre_g, ffn1_w_gate, ffn1_w_up, ffn1_w_down, ffn1_post_g, mix_pre_g, w_in, sgu_norm_g, sgu_w, sgu_b, pool_w, pool_scale, w_out_a, w_out_b, w_o, mix_post_g, ffn2_pre_g, ffn2_w_gate, ffn2_w_up, ffn2_w_down, ffn2_post_g, ple_pre_g, ple_w_gate, ple_w_proj, ple_post_g, loss_target, m_ffn1_pre_g, m_ffn1_w_gate, m_ffn1_w_up, m_ffn1_w_down, m_ffn1_post_g, m_mix_pre_g, m_w_in, m_sgu_norm_g, m_sgu_w, m_sgu_b, m_pool_w, m_pool_scale, m_w_out_a, m_w_out_b, m_w_o, m_mix_post_g, m_ffn2_pre_g, m_ffn2_w_gate, m_ffn2_w_up, m_ffn2_w_down, m_ffn2_post_g, m_ple_pre_g, m_ple_w_gate, m_ple_w_proj, m_ple_post_g, v_ffn1_pre_g, v_ffn1_w_gate, v_ffn1_w_up, v_ffn1_w_down, v_ffn1_post_g, v_mix_pre_g, v_w_in, v_sgu_norm_g, v_sgu_w, v_sgu_b, v_pool_w, v_pool_scale, v_w_out_a, v_w_out_b, v_w_o, v_mix_post_g, v_ffn2_pre_g, v_ffn2_w_gate, v_ffn2_w_up, v_ffn2_w_down, v_ffn2_post_g, v_ple_pre_g, v_ple_w_gate, v_ple_w_proj, v_ple_post_g):
    given = dict(locals())
    weights = {n: given[n] for n in WEIGHTS}
    moments_m = {n: given["m_" + n] for n in WEIGHTS}
    moments_v = {n: given["v_" + n] for n in WEIGHTS}
    core = lax.axis_index("c").astype(jnp.int32).reshape(1)
    chip = (2 * lax.axis_index("x") + lax.axis_index("y")).astype(jnp.int32).reshape(1)

    gathered = allgather_weights([_shard2d(weights[n][0]).astype(BF16) for n in BIG])
    w = dict(zip(BIG, gathered))
    d = x.shape[-1]
    full = {n: w[n].reshape(-1, d) for n in ("w_out_a", "w_out_b", "w_o", "ple_w_gate")}
    n_groups = pool_w.shape[1]
    rows_per = pool_w.shape[2]
    dgp = pool_w.shape[3]
    pool_full = w["pool_w"].reshape(N_CHIPS, n_groups, rows_per, dgp).transpose(1, 0, 2, 3).reshape(n_groups, N_CHIPS * rows_per, dgp)
    proj_full = w["ple_w_proj"].transpose(1, 0, 2).reshape(ple_w_proj.shape[1], -1)
    gain = {n: weights[n] for n in GAINS}
    sgu_w3 = sgu_w[0]
    sgu_b3 = sgu_b[0][:, :, None]

    h0 = x[0]
    tgt = loss_target[0]
    p_bf = p[0, 0].astype(BF16)
    xn1 = rms_cast(h0, gain["ffn1_pre_g"], "ffn1_pre_norm")
    s1, h1, xn2 = _ffn_fwd(xn1, h0, w, "ffn1_", gain["ffn1_post_g"], gain["mix_pre_g"], "ffn1")
    z = mixer_in(xn2, w["w_in"], "mixer_in")
    a = sgu_fwd(z, gain["sgu_norm_g"], sgu_w3, sgu_b3, "sgu_fwd")
    diff, b = pool_fwd(z, pool_full, gain["pool_scale"], "pool_fwd")
    ya, yb, y = mixer_y(a, b, z, full["w_out_a"], full["w_out_b"], "mixer_y")
    m, h2, xn3 = mm_norm_res(y[None], full["w_o"][None], h1, gain["mix_post_g"], gain["ffn2_pre_g"], 1.0, "mixer_out")
    s2, h3, xn4 = _ffn_fwd(xn3, h2, w, "ffn2_", gain["ffn2_post_g"], gain["ple_pre_g"], "ffn2")
    gate, e, q, dh4, loss_part = ple_fwd(xn4, p_bf, full["ple_w_gate"], proj_full, h3, gain["ple_post_g"], tgt, "ple_fwd")
    loss = lax.psum(loss_part[0, 0], ("x", "y", "c"))

    small_g, big_g = {}, {}
    ds, de, dh3, small_g["ple_post_g"], small_g["ple_pre_g"] = ple_bwd(
        dh4, q, gate, e, gain["ple_post_g"], full["ple_w_gate"], h3, gain["ple_pre_g"], "ple_bwd")
    big_g["ple_w_gate"] = dw_tn(xn4, ds, 1, "dw_ple_gate").reshape(N_CHIPS, -1, d)
    big_g["ple_w_proj"] = dw_tn(p_bf, de, N_CHIPS, "dw_ple_proj", dy_mode="cols")
    dh2, g2, small_g["ffn2_post_g"], small_g["ffn2_pre_g"] = _ffn_bwd(dh3, s2, w, "ffn2_", gain["ffn2_post_g"], gain["ffn2_pre_g"], "ffn2")
    big_g.update(g2)
    dm, dya, dyb, dga, dgb, da, db, small_g["mix_post_g"] = mixer_bwd_y(
        dh2, m, gain["mix_post_g"], full["w_o"], ya, yb, z, full["w_out_a"], full["w_out_b"], "mixer_bwd_y")
    big_g["w_o"] = dw_tn(y, dm, 1, "dw_o").reshape(N_CHIPS, -1, d)
    big_g["w_out_a"] = dw_tn(a, dya, 1, "dw_out_a").reshape(N_CHIPS, -1, d)
    big_g["w_out_b"] = dw_tn(b, dyb, 1, "dw_out_b").reshape(N_CHIPS, -1, d)
    dzu, dzv, d_sgu_w, d_sgu_b, small_g["sgu_norm_g"] = sgu_bwd(z, da, gain["sgu_norm_g"], sgu_w3, sgu_b3, "sgu_bwd")
    dc, d_pool_w, small_g["pool_scale"] = pool_bwd(db, diff, pool_full, gain["pool_scale"], "pool_bwd")
    big_g["pool_w"] = d_pool_w.astype(BF16).reshape(n_groups, N_CHIPS, rows_per, dgp).transpose(1, 0, 2, 3).reshape(
        N_CHIPS, n_groups * rows_per, dgp)
    dz = jnp.concatenate([dzu, dzv, dc, dga, dgb], axis=1)
    big_g["w_in"] = dw_tn(xn2, dz, N_CHIPS, "dw_in", dy_mode="cols")
    dh1, small_g["mix_pre_g"] = dx_norm_bwd([(dz, w["w_in"], True)], h1, gain["mix_pre_g"], dh2, "mixer_bwd_x")
    dh0, g1, small_g["ffn1_post_g"], small_g["ffn1_pre_g"] = _ffn_bwd(dh1, s1, w, "ffn1_", gain["ffn1_post_g"], gain["ffn1_pre_g"], "ffn1")
    big_g.update(g1)
    small_g["sgu_w"] = d_sgu_w
    small_g["sgu_b"] = d_sgu_b[:, :, 0]

    partial = [big_g[n] for n in BIG]
    from_sibling = exchange_halves(partial)
    chip_sums = [add_halves(g, r, core, "add_halves_" + n) for n, g, r in zip(BIG, partial, from_sibling)]
    from_chips = scatter_to_owners(chip_sums)
    reduced = [add_chips(s, r, chip, "add_chips_" + n) for n, s, r in zip(BIG, chip_sums, from_chips)]
    shared = share_with_sibling(reduced)
    grad, delta, new_m, new_v = {}, {}, {}, {}
    for n, g2d in zip(BIG, shared):
        shape = weights[n].shape
        g2d = g2d.reshape(-1, g2d.shape[-1])
        dl, nm, nv = adamw(_shard2d(weights[n][0]), g2d, _shard2d(moments_m[n][0]), _shard2d(moments_v[n][0]), "adamw_" + n)
        grad[n], delta[n], new_m[n], new_v[n] = (t.reshape(shape) for t in (g2d, dl, nm, nv))

    packed = small_allreduce_adamw(_pack_small(small_g), _pack_small({n: weights[n] for n in SMALL}),
                                   _pack_small({n: moments_m[n] for n in SMALL}), _pack_small({n: moments_v[n] for n in SMALL}))
    like = {n: weights[n] for n in SMALL}
    for store, block in zip((grad, delta, new_m, new_v), packed):
        store.update(_unpack_small(block, like))

    return (loss, dh0[None], *[grad[n] for n in WEIGHTS], *[delta[n] for n in WEIGHTS],
            *[new_m[n] for n in WEIGHTS], *[new_v[n] for n in WEIGHTS])
```

```python
import functools

import jax
import jax.numpy as jnp
from jax import lax
from jax.experimental import pallas as pl
from jax.experimental.pallas import tpu as pltpu

F32 = jnp.float32
BF16 = jnp.bfloat16
EPS = 1e-6
CHUNK = 128
POOL_WINDOWS = (2, 4, 8, 16)
HALO = 16
N_CHIPS = 4
ADAM_LR, ADAM_B1, ADAM_B2, ADAM_EPS, ADAM_WD, ADAM_STEP = 0.001, 0.9, 0.999, 1e-08, 0.01, 10
VMEM_LIMIT_V7X = 58 * 1024 * 1024
MESH_IDS = pl.DeviceIdType.MESH
HBM_SPEC = pl.BlockSpec(memory_space=pltpu.HBM)
VMEM_SPEC = pl.BlockSpec(memory_space=pltpu.VMEM)
SEM_SPEC = pl.BlockSpec(memory_space=pltpu.SEMAPHORE)
ANY_SPEC = pl.BlockSpec(memory_space=pl.ANY)
DATAFLOW = pltpu.SideEffectType.DATAFLOW_SIDE_EFFECTING
OTHER_CHIPS = ((1, 0), (0, 1), (1, 1))

NT = (((1,), (1,)), ((), ()))
TN = (((0,), (0,)), ((), ()))


def _params(*sem):
    return pltpu.CompilerParams(dimension_semantics=sem or None, vmem_limit_bytes=VMEM_LIMIT_V7X)


def _tile(t, want):
    return max(c for c in range(8, min(t, want) + 1, 8) if t % c == 0)


def _const(shape):
    return pl.BlockSpec(shape, lambda *_: (0,) * len(shape))


def _rows(tm, d, col=0):
    return pl.BlockSpec((tm, d), lambda i: (i, col))


def _kmajor(nk, tm, kb):
    return pl.BlockSpec((nk, tm, kb), lambda i: (0, i, 0))


def _dot(a, b):
    return jnp.dot(a, b, preferred_element_type=F32)


def _dot_nt(a, b):
    return lax.dot_general(a, b, NT, preferred_element_type=F32)


def _dot_tn(a, b):
    return lax.dot_general(a, b, TN, preferred_element_type=F32)


def _gelu(x):
    return 0.5 * x * (1.0 + jnp.tanh(0.7978845608028654 * (x + 0.044715 * x * x * x)))


def _gelu_grad(x):
    x2 = x * x
    th = jnp.tanh(0.7978845608028654 * (x + 0.044715 * x * x2))
    return 0.5 * (1.0 + th) + 0.5 * x * (1.0 - th * th) * 0.7978845608028654 * (1.0 + 3.0 * 0.044715 * x2)


def _sigmoid(x):
    return 1.0 / (1.0 + jnp.exp(-x))


def _rstd(h):
    return lax.rsqrt(jnp.mean(h * h, axis=-1, keepdims=True) + EPS)


def _rms_bwd(h, g, dy):
    r = _rstd(h)
    t = dy * g
    dh = r * t - h * (r * r * r) * jnp.mean(h * t, axis=-1, keepdims=True)
    return dh, jnp.sum(dy * h * r, axis=0, keepdims=True)


def _ordered_after(body, n_in, deps):
    if not deps:
        return body
    return lambda *refs: body(*refs[:n_in], *refs[n_in + len(deps):])


def _accumulate(ref, value):
    @pl.when(pl.program_id(0) == 0)
    def _():
        ref[...] = jnp.zeros_like(ref)

    ref[...] += value


def rms_cast(h, g, name):
    t, d = h.shape
    tm = _tile(t, 512)

    def body(h_ref, g_ref, o_ref):
        hv = h_ref[...]
        o_ref[...] = (hv * _rstd(hv) * g_ref[...]).astype(BF16)

    return pl.pallas_call(
        body, name=name, grid=(t // tm,), in_specs=[_rows(tm, d), _const((1, d))], out_specs=_rows(tm, d),
        out_shape=jax.ShapeDtypeStruct((t, d), BF16), compiler_params=_params("parallel"))(h, g)


def ffn_gu(xn, wg4, wu4, name):
    t, d = xn.shape
    nk, _, fk = wg4.shape
    tm = _tile(t, 512)

    def body(x_ref, wg_ref, wu_ref, g_ref, u_ref, a_ref):
        xv = x_ref[...]
        g = _dot(xv, wg_ref[0])
        u = _dot(xv, wu_ref[0])
        g_ref[0] = g.astype(BF16)
        u_ref[0] = u.astype(BF16)
        a_ref[0] = (g * _sigmoid(g) * u).astype(BF16)

    w_spec = pl.BlockSpec((1, d, fk), lambda k, i: (k, 0, 0))
    o_spec = pl.BlockSpec((1, tm, fk), lambda k, i: (k, i, 0))
    shape = jax.ShapeDtypeStruct((nk, t, fk), BF16)
    return pl.pallas_call(
        body, name=name, grid=(nk, t // tm), in_specs=[pl.BlockSpec((tm, d), lambda k, i: (i, 0)), w_spec, w_spec],
        out_specs=[o_spec] * 3, out_shape=[shape] * 3, compiler_params=_params("parallel", "parallel"))(xn, wg4, wu4)


def mm_norm_res(a3, w3, h_old, g_post, g_next, scale, name):
    nk, t, kb = a3.shape
    d = w3.shape[2]
    tm = _tile(t, 256)

    def body(a_ref, w_ref, h_ref, gp_ref, gn_ref, f_ref, hn_ref, xn_ref):
        f = _dot(a_ref[0], w_ref[0])
        for k in range(1, nk):
            f += _dot(a_ref[k], w_ref[k])
        f_ref[...] = f
        hn = h_ref[...] + scale * (f * _rstd(f) * gp_ref[...])
        hn_ref[...] = hn
        xn_ref[...] = (hn * _rstd(hn) * gn_ref[...]).astype(BF16)

    return pl.pallas_call(
        body, name=name, grid=(t // tm,),
        in_specs=[_kmajor(nk, tm, kb), _const((nk, kb, d)), _rows(tm, d), _const((1, d)), _const((1, d))],
        out_specs=[_rows(tm, d)] * 3,
        out_shape=[jax.ShapeDtypeStruct((t, d), F32), jax.ShapeDtypeStruct((t, d), F32), jax.ShapeDtypeStruct((t, d), BF16)],
        compiler_params=_params("parallel"))(a3, w3, h_old, g_post, g_next)


def mixer_in(xn, win4, name):
    t, d = xn.shape
    nk, _, nb = win4.shape
    tm = _tile(t, 512)

    def body(x_ref, w_ref, z_ref):
        z_ref[...] = _dot(x_ref[...], w_ref[0]).astype(BF16)

    return pl.pallas_call(
        body, name=name, grid=(nk, t // tm),
        in_specs=[pl.BlockSpec((tm, d), lambda k, i: (i, 0)), pl.BlockSpec((1, d, nb), lambda k, i: (k, 0, 0))],
        out_specs=pl.BlockSpec((tm, nb), lambda k, i: (i, k)), out_shape=jax.ShapeDtypeStruct((t, nk * nb), BF16),
        compiler_params=_params("parallel", "parallel"))(xn, win4)


def _causal_mask():
    row = lax.broadcasted_iota(jnp.int32, (CHUNK, CHUNK), 0)
    col = lax.broadcasted_iota(jnp.int32, (CHUNK, CHUNK), 1)
    return row >= col


def _layernorm_parts(v):
    mu = jnp.mean(v, axis=-1, keepdims=True)
    vc = v - mu
    r = lax.rsqrt(jnp.mean(vc * vc, axis=-1, keepdims=True) + EPS)
    return vc * r, r


def sgu_fwd(z, norm_g, sgu_w, sgu_b3, name):
    t = z.shape[0]
    d = norm_g.shape[1]
    ng = sgu_w.shape[0]
    dg = d // ng
    tm = _tile(t, 256)

    def body(zu_ref, zv_ref, ng_ref, w_ref, b_ref, a_ref):
        vhat, _ = _layernorm_parts(_gelu(zv_ref[...].astype(F32)))
        vn = (vhat * ng_ref[...]).astype(BF16)
        u = _gelu(zu_ref[...].astype(F32))
        mask = _causal_mask()
        for g in range(ng):
            wg = jnp.where(mask, w_ref[g], 0.0).astype(BF16)
            for ci in range(tm // CHUNK):
                rs, cs = slice(ci * CHUNK, (ci + 1) * CHUNK), slice(g * dg, (g + 1) * dg)
                sv = _dot(wg, vn[rs, cs]) + b_ref[g]
                a_ref[rs, cs] = (u[rs, cs] * sv).astype(BF16)

    return pl.pallas_call(
        body, name=name, grid=(t // tm,),
        in_specs=[_rows(tm, d, 0), _rows(tm, d, 1), _const((1, d)), _const((ng, CHUNK, CHUNK)), _const((ng, CHUNK, 1))],
        out_specs=_rows(tm, d), out_shape=jax.ShapeDtypeStruct((t, d), BF16),
        compiler_params=_params("parallel"))(z, z, norm_g, sgu_w, sgu_b3)


def pool_fwd(z, pool_w, pool_scale, name):
    t = z.shape[0]
    d = pool_scale.shape[1]
    ng = pool_w.shape[0]
    dg = d // ng
    tm = _tile(t, 256)
    per = tm // HALO

    def body(c_ref, prev_ref, w_ref, s_ref, diff_ref, b_ref):
        i = pl.program_id(0)
        cur = c_ref[...].astype(F32)
        prev = jnp.where(i > 0, prev_ref[...].astype(F32), 0.0)
        ext = jnp.concatenate([prev, cur], axis=0)
        tok = i * tm + lax.broadcasted_iota(jnp.int32, (tm, 1), 0)
        for g, win in enumerate(POOL_WINDOWS):
            cs = slice(g * dg, (g + 1) * dg)
            s = ext[:, cs]
            sh = 1
            while sh < win:
                s = s + pltpu.roll(s, sh, 0)
                sh *= 2
            cnt = jnp.minimum(tok + 1, win).astype(F32)
            diff = (s[HALO:] / cnt - cur[:, cs]).astype(BF16)
            diff_ref[:, cs] = diff
            b_ref[:, cs] = (_dot(diff, w_ref[g]) * s_ref[:, cs]).astype(BF16)

    return pl.pallas_call(
        body, name=name, grid=(t // tm,),
        in_specs=[_rows(tm, d, 2), pl.BlockSpec((HALO, d), lambda i: (jnp.maximum(i * per - 1, 0), 2)),
                  _const((ng, dg, dg)), _const((1, d))],
        out_specs=[_rows(tm, d)] * 2, out_shape=[jax.ShapeDtypeStruct((t, d), BF16)] * 2,
        compiler_params=_params("parallel"))(z, z, pool_w, pool_scale)


def mixer_y(a, b, z, woa, wob, name):
    t, d = a.shape
    tm = _tile(t, 256)

    def body(a_ref, b_ref, ga_ref, gb_ref, wa_ref, wb_ref, ya_ref, yb_ref, y_ref):
        ya = _dot(a_ref[...], wa_ref[...])
        yb = _dot(b_ref[...], wb_ref[...])
        ya_ref[...] = ya.astype(BF16)
        yb_ref[...] = yb.astype(BF16)
        y_ref[...] = (_sigmoid(ga_ref[...].astype(F32)) * ya + _sigmoid(gb_ref[...].astype(F32)) * yb).astype(BF16)

    return pl.pallas_call(
        body, name=name, grid=(t // tm,),
        in_specs=[_rows(tm, d), _rows(tm, d), _rows(tm, d, 3), _rows(tm, d, 4), _const((d, d)), _const((d, d))],
        out_specs=[_rows(tm, d)] * 3, out_shape=[jax.ShapeDtypeStruct((t, d), BF16)] * 3,
        compiler_params=_params("parallel"))(a, b, z, z, woa, wob)


def ple_fwd(xn, p, wpg, wpp, h, g_post, target, name):
    t, d = xn.shape
    dp = p.shape[1]
    tm = _tile(t, 256)

    def body(x_ref, p_ref, wg_ref, wp_ref, h_ref, g_ref, tg_ref, gate_ref, e_ref, q_ref, dh_ref, loss_ref):
        gate = _sigmoid(_dot(x_ref[...], wg_ref[...]))
        e = _dot(p_ref[...], wp_ref[...])
        q = gate * e
        gate_ref[...] = gate.astype(BF16)
        e_ref[...] = e.astype(BF16)
        q_ref[...] = q
        err = h_ref[...] + q * _rstd(q) * g_ref[...] - tg_ref[...]
        dh_ref[...] = err * (1.0 / d)
        _accumulate(loss_ref, jnp.full(loss_ref.shape, (0.5 / d) * jnp.sum(err * err), F32))

    return pl.pallas_call(
        body, name=name, grid=(t // tm,),
        in_specs=[_rows(tm, d), _rows(tm, dp), _const((d, d)), _const((dp, d)), _rows(tm, d), _const((1, d)), _rows(tm, d)],
        out_specs=[_rows(tm, d)] * 4 + [_const((8, 128))],
        out_shape=[jax.ShapeDtypeStruct((t, d), BF16), jax.ShapeDtypeStruct((t, d), BF16), jax.ShapeDtypeStruct((t, d), F32),
                   jax.ShapeDtypeStruct((t, d), F32), jax.ShapeDtypeStruct((8, 128), F32)],
        compiler_params=_params("arbitrary"))(xn, p, wpg, wpp, h, g_post, target)


def ple_bwd(dh, q, gate, e, g_post, wpg, h, g_pre, name):
    t, d = dh.shape
    tm = _tile(t, 256)

    def body(dh_ref, q_ref, gate_ref, e_ref, gp_ref, w_ref, h_ref, gn_ref, ds_ref, de_ref, dhp_ref, dgp_ref, dgn_ref):
        dhv = dh_ref[...]
        dq, dgp = _rms_bwd(q_ref[...], gp_ref[...], dhv)
        gate = gate_ref[...].astype(F32)
        ds = (dq * e_ref[...].astype(F32) * gate * (1.0 - gate)).astype(BF16)
        ds_ref[...] = ds
        de_ref[...] = (dq * gate).astype(BF16)
        dx, dgn = _rms_bwd(h_ref[...], gn_ref[...], _dot_nt(ds, w_ref[...]))
        dhp_ref[...] = dhv + dx
        _accumulate(dgp_ref, dgp)
        _accumulate(dgn_ref, dgn)

    return pl.pallas_call(
        body, name=name, grid=(t // tm,),
        in_specs=[_rows(tm, d)] * 4 + [_const((1, d)), _const((d, d)), _rows(tm, d), _const((1, d))],
        out_specs=[_rows(tm, d)] * 3 + [_const((1, d))] * 2,
        out_shape=[jax.ShapeDtypeStruct((t, d), BF16), jax.ShapeDtypeStruct((t, d), BF16), jax.ShapeDtypeStruct((t, d), F32),
                   jax.ShapeDtypeStruct((1, d), F32), jax.ShapeDtypeStruct((1, d), F32)],
        compiler_params=_params("arbitrary"))(dh, q, gate, e, g_post, wpg, h, g_pre)


def ffn_bwd_a(dh, f, g_post, wd4, g4, u4, scale, name, deps=()):
    t, d = dh.shape
    nk, fk, _ = wd4.shape
    tm = _tile(t, 256)

    def body(dh_ref, f_ref, gp_ref, w_ref, g_ref, u_ref, df_ref, dg_ref, du_ref, dgp_ref):
        df, dgp = _rms_bwd(f_ref[...], gp_ref[...], dh_ref[...])
        df = (scale * df).astype(BF16)
        df_ref[...] = df
        _accumulate(dgp_ref, scale * dgp)
        for k in range(nk):
            da = _dot_nt(df, w_ref[k])
            g = g_ref[k].astype(F32)
            sg = _sigmoid(g)
            du_ref[k] = (da * g * sg).astype(BF16)
            dg_ref[k] = (da * u_ref[k].astype(F32) * sg * (1.0 + g * (1.0 - sg))).astype(BF16)

    return pl.pallas_call(
        _ordered_after(body, 6, deps), name=name, grid=(t // tm,),
        in_specs=[_rows(tm, d), _rows(tm, d), _const((1, d)), _const((nk, fk, d)), _kmajor(nk, tm, fk), _kmajor(nk, tm, fk)]
        + [ANY_SPEC] * len(deps),
        out_specs=[_rows(tm, d), _kmajor(nk, tm, fk), _kmajor(nk, tm, fk), _const((1, d))],
        out_shape=[jax.ShapeDtypeStruct((t, d), BF16), jax.ShapeDtypeStruct((nk, t, fk), BF16),
                   jax.ShapeDtypeStruct((nk, t, fk), BF16), jax.ShapeDtypeStruct((1, d), F32)],
        compiler_params=_params("arbitrary"))(dh, f, g_post, wd4, g4, u4, *deps)


def dx_norm_bwd(pairs, h, g_pre, dh_in, name, deps=()):
    t, d = h.shape
    tm = _tile(t, 256)
    n = len(pairs)

    def body(*refs):
        dys, ws = refs[:n], refs[n:2 * n]
        h_ref, g_ref, dhi_ref, dho_ref, dg_ref = refs[2 * n:]
        acc = None
        for (_, w4, columns), dy_ref, w_ref in zip(pairs, dys, ws):
            nk, _, kb = w4.shape
            for k in range(nk):
                part = _dot_nt(dy_ref[:, k * kb:(k + 1) * kb] if columns else dy_ref[k], w_ref[k])
                acc = part if acc is None else acc + part
        dx, dg = _rms_bwd(h_ref[...], g_ref[...], acc)
        dho_ref[...] = dhi_ref[...] + dx
        _accumulate(dg_ref, dg)

    dy_specs = [_rows(tm, dy.shape[1]) if columns else _kmajor(dy.shape[0], tm, dy.shape[2]) for dy, _, columns in pairs]
    return pl.pallas_call(
        _ordered_after(body, 2 * n + 3, deps), name=name, grid=(t // tm,),
        in_specs=dy_specs + [_const(w4.shape) for _, w4, _ in pairs] + [_rows(tm, d), _const((1, d)), _rows(tm, d)]
        + [ANY_SPEC] * len(deps),
        out_specs=[_rows(tm, d), _const((1, d))],
        out_shape=[jax.ShapeDtypeStruct((t, d), F32), jax.ShapeDtypeStruct((1, d), F32)],
        compiler_params=_params("arbitrary"))(*[dy for dy, _, _ in pairs], *[w4 for _, w4, _ in pairs], h, g_pre, dh_in, *deps)


def mixer_bwd_y(dh, m, g_post, w_o, ya, yb, z, woa, wob, name, deps=()):
    t, d = dh.shape
    tm = _tile(t, 256)

    def body(dh_ref, m_ref, gp_ref, wo_ref, ya_ref, yb_ref, ga_ref, gb_ref, wa_ref, wb_ref,
             dm_ref, dya_ref, dyb_ref, dga_ref, dgb_ref, da_ref, db_ref, dgp_ref):
        dm, dgp = _rms_bwd(m_ref[...], gp_ref[...], dh_ref[...])
        dm = dm.astype(BF16)
        dm_ref[...] = dm
        _accumulate(dgp_ref, dgp)
        dy = _dot_nt(dm, wo_ref[...])
        sa = _sigmoid(ga_ref[...].astype(F32))
        sb = _sigmoid(gb_ref[...].astype(F32))
        dya = (dy * sa).astype(BF16)
        dyb = (dy * sb).astype(BF16)
        dya_ref[...] = dya
        dyb_ref[...] = dyb
        dga_ref[...] = (dy * ya_ref[...].astype(F32) * sa * (1.0 - sa)).astype(BF16)
        dgb_ref[...] = (dy * yb_ref[...].astype(F32) * sb * (1.0 - sb)).astype(BF16)
        da_ref[...] = _dot_nt(dya, wa_ref[...]).astype(BF16)
        db_ref[...] = _dot_nt(dyb, wb_ref[...]).astype(BF16)

    return pl.pallas_call(
        _ordered_after(body, 10, deps), name=name, grid=(t // tm,),
        in_specs=[_rows(tm, d), _rows(tm, d), _const((1, d)), _const((d, d)), _rows(tm, d), _rows(tm, d),
                  _rows(tm, d, 3), _rows(tm, d, 4), _const((d, d)), _const((d, d))] + [ANY_SPEC] * len(deps),
        out_specs=[_rows(tm, d)] * 7 + [_const((1, d))],
        out_shape=[jax.ShapeDtypeStruct((t, d), BF16)] * 7 + [jax.ShapeDtypeStruct((1, d), F32)],
        compiler_params=_params("arbitrary"))(dh, m, g_post, w_o, ya, yb, z, z, woa, wob, *deps)


def sgu_bwd(z, da, norm_g, sgu_w, sgu_b3, name):
    t, d = da.shape
    ng = sgu_w.shape[0]
    dg = d // ng
    tm = _tile(t, 256)
    steps = t // tm

    def body(zu_ref, zv_ref, da_ref, ng_ref, w_ref, b_ref, dzu_ref, dzv_ref, dw_ref, db_ref, dng_ref, dvn_ref, dsv_ref):
        i = pl.program_id(0)
        zv = zv_ref[...].astype(F32)
        zu = zu_ref[...].astype(F32)
        vhat, r = _layernorm_parts(_gelu(zv))
        gain = ng_ref[...]
        vn = (vhat * gain).astype(BF16)
        u = _gelu(zu)
        dav = da_ref[...].astype(F32)
        gu = _gelu_grad(zu)
        mask = _causal_mask()

        @pl.when(i == 0)
        def _():
            dw_ref[...] = jnp.zeros_like(dw_ref)
            dsv_ref[...] = jnp.zeros_like(dsv_ref)

        for g in range(ng):
            wg = jnp.where(mask, w_ref[g], 0.0).astype(BF16)
            dw = jnp.zeros((CHUNK, CHUNK), F32)
            dsv_sum = jnp.zeros((CHUNK, dg), F32)
            for ci in range(tm // CHUNK):
                rs, cs = slice(ci * CHUNK, (ci + 1) * CHUNK), slice(g * dg, (g + 1) * dg)
                vn_blk = vn[rs, cs]
                sv = _dot(wg, vn_blk) + b_ref[g]
                dzu_ref[rs, cs] = (dav[rs, cs] * sv * gu[rs, cs]).astype(BF16)
                dsv = dav[rs, cs] * u[rs, cs]
                dsv_sum += dsv
                dsv = dsv.astype(BF16)
                dw += _dot_nt(dsv, vn_blk)
                dvn_ref[rs, cs] = _dot_tn(wg, dsv)
            dw_ref[g] += dw
            dsv_ref[:, cs] += dsv_sum

        dvn = dvn_ref[...]
        _accumulate(dng_ref, jnp.sum(dvn * vhat, axis=0, keepdims=True))
        dvh = dvn * gain
        dv = r * (dvh - jnp.mean(dvh, axis=-1, keepdims=True) - vhat * jnp.mean(dvh * vhat, axis=-1, keepdims=True))
        dzv_ref[...] = (dv * _gelu_grad(zv)).astype(BF16)

        @pl.when(i == steps - 1)
        def _():
            for g in range(ng):
                dw_ref[g] = jnp.where(mask, dw_ref[g], 0.0)
                row_sum = jnp.sum(dsv_ref[:, g * dg:(g + 1) * dg], axis=1, keepdims=True)
                db_ref[g] = jnp.broadcast_to(row_sum, (CHUNK, CHUNK))

    return pl.pallas_call(
        body, name=name, grid=(steps,),
        in_specs=[_rows(tm, d, 0), _rows(tm, d, 1), _rows(tm, d), _const((1, d)), _const((ng, CHUNK, CHUNK)), _const((ng, CHUNK, 1))],
        out_specs=[_rows(tm, d), _rows(tm, d), _const((ng, CHUNK, CHUNK)), _const((ng, CHUNK, CHUNK)), _const((1, d))],
        out_shape=[jax.ShapeDtypeStruct((t, d), BF16), jax.ShapeDtypeStruct((t, d), BF16),
                   jax.ShapeDtypeStruct((ng, CHUNK, CHUNK), F32), jax.ShapeDtypeStruct((ng, CHUNK, CHUNK), F32),
                   jax.ShapeDtypeStruct((1, d), F32)],
        scratch_shapes=[pltpu.VMEM((tm, d), F32), pltpu.VMEM((CHUNK, d), F32)],
        compiler_params=_params("arbitrary"))(z, z, da, norm_g, sgu_w, sgu_b3)


def pool_bwd(db, diff, pool_w, pool_scale, name):
    t, d = db.shape
    ng = pool_w.shape[0]
    dg = d // ng
    tm = _tile(t, 256)
    per = tm // HALO
    steps = t // tm

    def body(db_ref, next_ref, diff_ref, w_ref, s_ref, dc_ref, dw_ref, ds_ref):
        i = pl.program_id(0)
        dbc = db_ref[...].astype(F32)
        nxt = jnp.where(i < steps - 1, next_ref[...].astype(F32), 0.0)
        ext = jnp.concatenate([dbc, nxt], axis=0)
        rows = tm + HALO
        tok = i * tm + lax.broadcasted_iota(jnp.int32, (rows, 1), 0)

        @pl.when(i == 0)
        def _():
            dw_ref[...] = jnp.zeros_like(dw_ref)
            ds_ref[...] = jnp.zeros_like(ds_ref)

        for g, win in enumerate(POOL_WINDOWS):
            cs = slice(g * dg, (g + 1) * dg)
            dp = (ext[:, cs] * s_ref[:, cs]).astype(BF16)
            dd = _dot_nt(dp, w_ref[g])
            s = dd / jnp.minimum(tok + 1, win).astype(F32)
            sh = 1
            while sh < win:
                s = s + pltpu.roll(s, rows - sh, 0)
                sh *= 2
            dc_ref[:, cs] = (s[:tm] - dd[:tm]).astype(BF16)
            dfg = diff_ref[:, cs]
            ds_ref[:, cs] += jnp.sum(dbc[:, cs] * _dot(dfg, w_ref[g]), axis=0, keepdims=True)
            dw_ref[g] += _dot_tn(dfg, dp[:tm])

    return pl.pallas_call(
        body, name=name, grid=(steps,),
        in_specs=[_rows(tm, d), pl.BlockSpec((HALO, d), lambda i: (jnp.minimum((i + 1) * per, t // HALO - 1), 0)),
                  _rows(tm, d), _const((ng, dg, dg)), _const((1, d))],
        out_specs=[_rows(tm, d), _const((ng, dg, dg)), _const((1, d))],
        out_shape=[jax.ShapeDtypeStruct((t, d), BF16), jax.ShapeDtypeStruct((ng, dg, dg), F32), jax.ShapeDtypeStruct((1, d), F32)],
        compiler_params=_params("arbitrary"))(db, db, diff, pool_w, pool_scale)


def dw_tn(x, dy, nk, name, x_kmajor=False, dy_mode="same"):
    t = x.shape[-2]
    kx = x.shape[-1]
    n = dy.shape[-1] // nk if dy_mode == "cols" else dy.shape[-1]
    tt = _tile(t, 512)
    steps = t // tt

    def body(x_ref, dy_ref, o_ref, acc_ref):
        s = pl.program_id(1)

        @pl.when(s == 0)
        def _():
            acc_ref[...] = jnp.zeros_like(acc_ref)

        acc_ref[...] += _dot_tn(x_ref[0] if x_kmajor else x_ref[...], dy_ref[0] if dy_mode == "kmajor" else dy_ref[...])

        @pl.when(s == steps - 1)
        def _():
            o_ref[0] = acc_ref[...].astype(BF16)

    x_spec = pl.BlockSpec((1, tt, kx), lambda k, s: (k, s, 0)) if x_kmajor else pl.BlockSpec((tt, kx), lambda k, s: (s, 0))
    dy_spec = {"kmajor": pl.BlockSpec((1, tt, n), lambda k, s: (k, s, 0)), "cols": pl.BlockSpec((tt, n), lambda k, s: (s, k)),
               "same": pl.BlockSpec((tt, n), lambda k, s: (s, 0))}[dy_mode]
    return pl.pallas_call(
        body, name=name, grid=(nk, steps), in_specs=[x_spec, dy_spec],
        out_specs=pl.BlockSpec((1, kx, n), lambda k, s: (k, 0, 0)), out_shape=jax.ShapeDtypeStruct((nk, kx, n), BF16),
        scratch_shapes=[pltpu.VMEM((kx, n), F32)], compiler_params=_params("parallel", "arbitrary"))(x, dy)


def _place():
    x, y, c = lax.axis_index("x"), lax.axis_index("y"), lax.axis_index("c")
    chips = [((1 - x) if fx else x, (1 - y) if fy else y) for fx, fy in OTHER_CHIPS]
    return x, y, c, chips


def _half(rows, which):
    return pl.ds(pl.multiple_of(which * (rows // 2), 16), rows // 2)


def _hbm(a):
    return pltpu.with_memory_space_constraint(a, pltpu.HBM)


def cast_place(w2d, chip, name, deps=()):
    r, cdim = w2d.shape
    tr = _tile(r, 512)

    def body(chip_ref, w_ref, *rest):
        rest[-1][0] = w_ref[...].astype(BF16)

    return pl.pallas_call(
        body, name=name,
        grid_spec=pltpu.PrefetchScalarGridSpec(
            num_scalar_prefetch=1, grid=(r // tr,),
            in_specs=[pl.BlockSpec((tr, cdim), lambda i, chip_ref: (i, 0))] + [ANY_SPEC] * len(deps),
            out_specs=pl.BlockSpec((1, tr, cdim), lambda i, chip_ref: (chip_ref[0], i, 0))),
        out_shape=jax.ShapeDtypeStruct((N_CHIPS, r, cdim), BF16), compiler_params=_params("parallel"))(chip, w2d, *deps)


def _gather_copy(buf, sends, recvs, i, j, me, chip_xy, c):
    cx, cy = chip_xy
    mine = _half(buf.shape[1], c)
    return pltpu.make_async_remote_copy(
        src_ref=buf.at[me, mine], dst_ref=buf.at[me, mine], send_sem=sends.at[3 * i + j], recv_sem=recvs.at[3 * i + j],
        device_id=(cx, cy, c), device_id_type=MESH_IDS)


def allgather_start(bufs, name):
    n = len(bufs)

    def body(*refs):
        ins = refs[:n]
        sends, recvs = refs[n], refs[n + 1]
        token = refs[2 * n + 2]
        x, y, c, chips = _place()
        for i in range(n):
            for j, chip_xy in enumerate(chips):
                _gather_copy(ins[i], sends, recvs, i, j, 2 * x + y, chip_xy, c).start()
        token[...] = jnp.zeros_like(token)

    out = pl.pallas_call(
        body, name=name, in_specs=[HBM_SPEC] * n,
        out_specs=[SEM_SPEC, SEM_SPEC] + [HBM_SPEC] * n + [VMEM_SPEC],
        out_shape=[pltpu.SemaphoreType.DMA((3 * n,)), pltpu.SemaphoreType.DMA((3 * n,))]
        + [pltpu.HBM(b.shape, b.dtype) for b in bufs] + [jax.ShapeDtypeStruct((8, 128), F32)],
        input_output_aliases={i: i + 2 for i in range(n)},
        compiler_params=pltpu.CompilerParams(has_side_effects=DATAFLOW))(*[_hbm(b) for b in bufs])
    return out[0], out[1], list(out[2:2 + n]), out[2 + n]


def allgather_wait(sends, recvs, bufs, after, name):
    n = len(bufs)

    def body(*refs):
        ins = refs[:n]
        send_sems, recv_sems = refs[n], refs[n + 1]
        x, y, c, chips = _place()
        for i in range(n):
            for j, (cx, cy) in enumerate(chips):
                mine = _half(ins[i].shape[1], c)
                cp = pltpu.make_async_remote_copy(
                    src_ref=ins[i].at[2 * x + y, mine], dst_ref=ins[i].at[2 * cx + cy, mine], send_sem=send_sems.at[3 * i + j],
                    recv_sem=recv_sems.at[3 * i + j], device_id=(cx, cy, c), device_id_type=MESH_IDS)
                cp.wait_send()
                cp.wait_recv()

    return pl.pallas_call(
        body, name=name, in_specs=[HBM_SPEC] * n + [SEM_SPEC, SEM_SPEC, ANY_SPEC], out_specs=[HBM_SPEC] * n,
        out_shape=[pltpu.HBM(b.shape, b.dtype) for b in bufs], input_output_aliases={i: i for i in range(n)},
        compiler_params=pltpu.CompilerParams(has_side_effects=DATAFLOW))(*bufs, sends, recvs, after)


def d2d_forward(bufs, name):
    n = len(bufs)

    def body(*refs):
        ins = refs[:n]
        send_sems, recv_sems = refs[2 * n:]
        x, y, c, chips = _place()
        copies = []
        for i in range(n):
            mine = _half(ins[i].shape[1], c)
            for j, (cx, cy) in enumerate(chips):
                landed = ins[i].at[2 * cx + cy, mine]
                cp = pltpu.make_async_remote_copy(
                    src_ref=landed, dst_ref=landed, send_sem=send_sems.at[i, j], recv_sem=recv_sems.at[i, j],
                    device_id=(x, y, 1 - c), device_id_type=MESH_IDS)
                cp.start()
                copies.append(cp)
        for i in range(n):
            theirs = _half(ins[i].shape[1], 1 - c)
            for j, (cx, cy) in enumerate(chips):
                passed = ins[i].at[2 * cx + cy, theirs]
                pltpu.make_async_remote_copy(
                    src_ref=passed, dst_ref=passed, send_sem=send_sems.at[i, j], recv_sem=recv_sems.at[i, j],
                    device_id=(x, y, 1 - c), device_id_type=MESH_IDS).wait_recv()
        for cp in copies:
            cp.wait_send()

    return pl.pallas_call(
        body, name=name, in_specs=[HBM_SPEC] * n, out_specs=[HBM_SPEC] * n,
        out_shape=[jax.ShapeDtypeStruct(b.shape, b.dtype) for b in bufs], input_output_aliases={i: i for i in range(n)},
        scratch_shapes=[pltpu.SemaphoreType.DMA((n, 3))] * 2,
        compiler_params=pltpu.CompilerParams(has_side_effects=True))(*bufs)


def exchange_halves(grads, name):
    n = len(grads)

    def body(*refs):
        ins, outs = refs[:n], refs[n:2 * n]
        send_sems, recv_sems = refs[2 * n:]
        x, y, c, _ = _place()
        copies = []
        for i in range(n):
            cp = pltpu.make_async_remote_copy(
                src_ref=ins[i].at[:, _half(ins[i].shape[1], 1 - c)], dst_ref=outs[i], send_sem=send_sems.at[i],
                recv_sem=recv_sems.at[i], device_id=(x, y, 1 - c), device_id_type=MESH_IDS)
            cp.start()
            copies.append(cp)
        for cp in copies:
            cp.wait()

    return pl.pallas_call(
        body, name=name, in_specs=[HBM_SPEC] * n, out_specs=[HBM_SPEC] * n,
        out_shape=[jax.ShapeDtypeStruct((g.shape[0], g.shape[1] // 2, g.shape[2]), g.dtype) for g in grads],
        scratch_shapes=[pltpu.SemaphoreType.DMA((n,))] * 2,
        compiler_params=pltpu.CompilerParams(has_side_effects=True))(*grads)


def _scatter_copy(src, land, sends, recvs, i, j, chip_xy, c):
    cx, cy = chip_xy
    return pltpu.make_async_remote_copy(
        src_ref=src.at[2 * cx + cy], dst_ref=land.at[j], send_sem=sends.at[3 * i + j], recv_sem=recvs.at[3 * i + j],
        device_id=(cx, cy, c), device_id_type=MESH_IDS)


def scatter_start(sums, name):
    n = len(sums)
    lands = [lax.empty((3,) + s.shape[1:], s.dtype) for s in sums]

    def body(*refs):
        srcs, zones = refs[:n], refs[n:2 * n]
        sends, recvs = refs[2 * n], refs[2 * n + 1]
        token = refs[4 * n + 2]
        _, _, c, chips = _place()
        for i in range(n):
            for j, chip_xy in enumerate(chips):
                _scatter_copy(srcs[i], zones[i], sends, recvs, i, j, chip_xy, c).start()
        token[...] = jnp.zeros_like(token)

    out = pl.pallas_call(
        body, name=name, in_specs=[HBM_SPEC] * (2 * n),
        out_specs=[SEM_SPEC, SEM_SPEC] + [HBM_SPEC] * (2 * n) + [VMEM_SPEC],
        out_shape=[pltpu.SemaphoreType.DMA((3 * n,)), pltpu.SemaphoreType.DMA((3 * n,))]
        + [pltpu.HBM(a.shape, a.dtype) for a in sums + lands] + [jax.ShapeDtypeStruct((8, 128), F32)],
        input_output_aliases={i: i + 2 for i in range(2 * n)},
        compiler_params=pltpu.CompilerParams(has_side_effects=DATAFLOW))(*[_hbm(a) for a in sums + lands])
    return out[0], out[1], list(out[2:2 + n]), list(out[2 + n:2 + 2 * n]), out[2 + 2 * n]


def scatter_wait(sends, recvs, sums, lands, after, name):
    n = len(sums)

    def body(*refs):
        srcs, zones = refs[:n], refs[n:2 * n]
        send_sems, recv_sems = refs[2 * n], refs[2 * n + 1]
        _, _, c, chips = _place()
        for i in range(n):
            for j, chip_xy in enumerate(chips):
                cp = _scatter_copy(srcs[i], zones[i], send_sems, recv_sems, i, j, chip_xy, c)
                cp.wait_send()
                cp.wait_recv()

    out = pl.pallas_call(
        body, name=name, in_specs=[HBM_SPEC] * (2 * n) + [SEM_SPEC, SEM_SPEC, ANY_SPEC], out_specs=[HBM_SPEC] * (2 * n),
        out_shape=[pltpu.HBM(a.shape, a.dtype) for a in sums + lands], input_output_aliases={i: i for i in range(2 * n)},
        compiler_params=pltpu.CompilerParams(has_side_effects=DATAFLOW))(*sums, *lands, sends, recvs, after)
    return list(out[:n]), list(out[n:])


def send_to_sibling(halves, name):
    n = len(halves)

    def body(*refs):
        ins, outs = refs[:n], refs[n:2 * n]
        send_sems, recv_sems = refs[2 * n:]
        x, y, c, _ = _place()
        copies = []
        for i in range(n):
            cp = pltpu.make_async_remote_copy(
                src_ref=ins[i], dst_ref=outs[i], send_sem=send_sems.at[i], recv_sem=recv_sems.at[i],
                device_id=(x, y, 1 - c), device_id_type=MESH_IDS)
            cp.start()
            copies.append(cp)
        for cp in copies:
            cp.wait()

    return pl.pallas_call(
        body, name=name, in_specs=[HBM_SPEC] * n, out_specs=[HBM_SPEC] * n,
        out_shape=[jax.ShapeDtypeStruct(h.shape, h.dtype) for h in halves],
        scratch_shapes=[pltpu.SemaphoreType.DMA((n,))] * 2,
        compiler_params=pltpu.CompilerParams(has_side_effects=True))(*halves)


def add_halves(grad4, recv4, core, name):
    nk, r, cdim = grad4.shape
    half = r // 2
    view = grad4.reshape(nk, 2, half, cdim)

    def body(core_ref, g_ref, r_ref, o_ref):
        o_ref[0] = (g_ref[0, 0].astype(F32) + r_ref[0].astype(F32)).astype(BF16)

    return pl.pallas_call(
        body, name=name,
        grid_spec=pltpu.PrefetchScalarGridSpec(
            num_scalar_prefetch=1, grid=(nk,),
            in_specs=[pl.BlockSpec((1, 1, half, cdim), lambda k, core_ref: (k, core_ref[0], 0, 0)),
                      pl.BlockSpec((1, half, cdim), lambda k, core_ref: (k, 0, 0))],
            out_specs=pl.BlockSpec((1, half, cdim), lambda k, core_ref: (k, 0, 0))),
        out_shape=jax.ShapeDtypeStruct((nk, half, cdim), BF16), compiler_params=_params("parallel"))(core, view, recv4)


def add_chips(sum4, recv3, chip, name):
    _, half, cdim = sum4.shape

    def body(chip_ref, s_ref, r_ref, o_ref):
        o_ref[...] = ((s_ref[0].astype(F32) + r_ref[0].astype(F32)) + r_ref[1].astype(F32)) + r_ref[2].astype(F32)

    return pl.pallas_call(
        body, name=name,
        grid_spec=pltpu.PrefetchScalarGridSpec(
            num_scalar_prefetch=1, grid=(1,),
            in_specs=[pl.BlockSpec((1, half, cdim), lambda k, chip_ref: (chip_ref[0], 0, 0)),
                      pl.BlockSpec((3, half, cdim), lambda k, chip_ref: (0, 0, 0))],
            out_specs=pl.BlockSpec((half, cdim), lambda k, chip_ref: (0, 0))),
        out_shape=jax.ShapeDtypeStruct((half, cdim), F32), compiler_params=_params("arbitrary"))(chip, sum4, recv3)


def _adamw_math(w, g, m, v):
    m = ADAM_B1 * m + (1.0 - ADAM_B1) * g
    v = ADAM_B2 * v + (1.0 - ADAM_B2) * (g * g)
    m_hat = m / (1.0 - ADAM_B1 ** ADAM_STEP)
    v_hat = v / (1.0 - ADAM_B2 ** ADAM_STEP)
    return -ADAM_LR * (m_hat / (jnp.sqrt(v_hat) + ADAM_EPS) + ADAM_WD * w), m, v


def adamw_halves(w, own, other, m, v, core, name):
    r, cdim = w.shape
    half = r // 2
    tr = _tile(half, 256)
    steps = half // tr

    def body(core_ref, w_ref, own_ref, other_ref, m_ref, v_ref, g_ref, d_ref, mo_ref, vo_ref):
        g = jnp.where(pl.program_id(0) == core_ref[0], own_ref[...], other_ref[...])
        g_ref[...] = g
        d_ref[...], mo_ref[...], vo_ref[...] = _adamw_math(w_ref[...], g, m_ref[...], v_ref[...])

    whole = pl.BlockSpec((tr, cdim), lambda hh, i, core_ref: (hh * steps + i, 0))
    part = pl.BlockSpec((tr, cdim), lambda hh, i, core_ref: (i, 0))
    return pl.pallas_call(
        body, name=name,
        grid_spec=pltpu.PrefetchScalarGridSpec(
            num_scalar_prefetch=1, grid=(2, steps), in_specs=[whole, part, part, whole, whole], out_specs=[whole] * 4),
        out_shape=[jax.ShapeDtypeStruct((r, cdim), F32)] * 4,
        compiler_params=_params("parallel", "parallel"))(core, w, own, other, m, v)


def small_allreduce_adamw(g, w, m, v):
    r, cdim = g.shape

    def body(g_ref, w_ref, m_ref, v_ref, go_ref, d_ref, mo_ref, vo_ref, pair, quad, d2d_send, d2d_recv, ici_send, ici_recv):
        x, y, c, chips = _place()
        me = 2 * x + y
        pair[c] = g_ref[...]
        swap = pltpu.make_async_remote_copy(
            src_ref=g_ref, dst_ref=pair.at[c], send_sem=d2d_send, recv_sem=d2d_recv, device_id=(x, y, 1 - c),
            device_id_type=MESH_IDS)
        swap.start()
        swap.wait()
        quad[me] = pair[0] + pair[1]
        copies = []
        for j, (cx, cy) in enumerate(chips):
            cp = pltpu.make_async_remote_copy(
                src_ref=quad.at[me], dst_ref=quad.at[me], send_sem=ici_send.at[j], recv_sem=ici_recv.at[j],
                device_id=(cx, cy, c), device_id_type=MESH_IDS)
            cp.start()
            copies.append(cp)
        for j, (cx, cy) in enumerate(chips):
            slot = quad.at[2 * cx + cy]
            pltpu.make_async_remote_copy(
                src_ref=slot, dst_ref=slot, send_sem=ici_send.at[j], recv_sem=ici_recv.at[j], device_id=(cx, cy, c),
                device_id_type=MESH_IDS).wait_recv()
        for cp in copies:
            cp.wait_send()
        total = (quad[0] + quad[1]) + (quad[2] + quad[3])
        go_ref[...] = total
        d_ref[...], mo_ref[...], vo_ref[...] = _adamw_math(w_ref[...], total, m_ref[...], v_ref[...])

    return pl.pallas_call(
        body, name="small_allreduce_adamw", in_specs=[VMEM_SPEC] * 4, out_specs=[VMEM_SPEC] * 4,
        out_shape=[jax.ShapeDtypeStruct((r, cdim), F32)] * 4,
        scratch_shapes=[pltpu.VMEM((2, r, cdim), F32), pltpu.VMEM((N_CHIPS, r, cdim), F32), pltpu.SemaphoreType.DMA,
                        pltpu.SemaphoreType.DMA, pltpu.SemaphoreType.DMA((3,)), pltpu.SemaphoreType.DMA((3,))],
        compiler_params=pltpu.CompilerParams(has_side_effects=True, vmem_limit_bytes=VMEM_LIMIT_V7X))(g, w, m, v)


GROUPS = (
    ("ffn1", ("ffn1_w_gate", "ffn1_w_up", "ffn1_w_down")),
    ("mixer", ("w_in", "pool_w", "w_out_a", "w_out_b", "w_o")),
    ("ffn2", ("ffn2_w_gate", "ffn2_w_up", "ffn2_w_down")),
    ("ple", ("ple_w_gate", "ple_w_proj")),
)
GAINS = ("ffn1_pre_g", "ffn1_post_g", "mix_pre_g", "sgu_norm_g", "pool_scale", "mix_post_g",
         "ffn2_pre_g", "ffn2_post_g", "ple_pre_g", "ple_post_g")
SMALL = GAINS + ("sgu_b", "sgu_w")
WEIGHTS = ("ffn1_pre_g", "ffn1_w_gate", "ffn1_w_up", "ffn1_w_down", "ffn1_post_g", "mix_pre_g", "w_in", "sgu_norm_g",
           "sgu_w", "sgu_b", "pool_w", "pool_scale", "w_out_a", "w_out_b", "w_o", "mix_post_g", "ffn2_pre_g",
           "ffn2_w_gate", "ffn2_w_up", "ffn2_w_down", "ffn2_post_g", "ple_pre_g", "ple_w_gate", "ple_w_proj", "ple_post_g")
SMALL_ROWS = 8


def _shard2d(a):
    return a.reshape(-1, a.shape[-1])


def _pack_small(parts):
    rows = []
    for name in SMALL:
        a = parts[name].astype(F32).reshape(-1)
        width = 1024
        n_rows = -(-a.shape[0] // width)
        padded = -(-n_rows // SMALL_ROWS) * SMALL_ROWS
        rows.append(jnp.pad(a, (0, padded * width - a.shape[0])).reshape(padded, width))
    return jnp.concatenate(rows, axis=0)


def _unpack_small(packed, like):
    out, row = {}, 0
    for name in SMALL:
        size = like[name].size
        n_rows = -(-size // 1024)
        padded = -(-n_rows // SMALL_ROWS) * SMALL_ROWS
        out[name] = packed[row:row + padded].reshape(-1)[:size].reshape(like[name].shape)
        row += padded
    return out


def _ffn_fwd(xn, h, w, pre, g_post, g_next, tag):
    g4, u4, a4 = ffn_gu(xn, w[pre + "w_gate"], w[pre + "w_up"], tag + "_gu")
    f, h_new, xn_next = mm_norm_res(a4, w[pre + "w_down"], h, g_post, g_next, 0.5, tag + "_down")
    return dict(xn=xn, h=h, g4=g4, u4=u4, a4=a4, f=f), h_new, xn_next


def _ffn_bwd_w(dh, saved, w, pre, g_post, tag, deps):
    df, dg4, du4, d_post = ffn_bwd_a(dh, saved["f"], g_post, w[pre + "w_down"], saved["g4"], saved["u4"], 0.5, tag + "_bwd_a",
                                     deps=deps)
    nk = N_CHIPS
    grads = {
        pre + "w_down": dw_tn(saved["a4"], df, nk, tag + "_dw_down", x_kmajor=True, dy_mode="same"),
        pre + "w_gate": dw_tn(saved["xn"], dg4, nk, tag + "_dw_gate", dy_mode="kmajor"),
        pre + "w_up": dw_tn(saved["xn"], du4, nk, tag + "_dw_up", dy_mode="kmajor"),
    }
    return dg4, du4, grads, d_post


def _ffn_bwd_x(dh, dg4, du4, saved, w, pre, g_pre, tag, deps):
    return dx_norm_bwd([(dg4, w[pre + "w_gate"], False), (du4, w[pre + "w_up"], False)], saved["h"], g_pre, dh,
                       tag + "_bwd_x", deps=deps)


def _gather_group(names, weights, chip, tag, deps):
    bufs = [cast_place(_shard2d(weights[n][0]), chip, "cast_" + n, deps=deps) for n in names]
    sends, recvs, bufs, token = allgather_start(bufs, "allgather_start_" + tag)
    return (sends, recvs, bufs), token


def _gathered(started, names, after, tag):
    sends, recvs, bufs = started
    landed = allgather_wait(sends, recvs, bufs, after, "allgather_wait_" + tag)
    return dict(zip(names, d2d_forward(landed, "d2d_forward_" + tag)))


def _reduce_start(names, big_g, core, tag):
    partial = [big_g[n] for n in names]
    from_sibling = exchange_halves(partial, "exchange_halves_" + tag)
    chip_sums = [add_halves(g, r, core, "add_halves_" + n) for n, g, r in zip(names, partial, from_sibling)]
    sends, recvs, sums, lands, token = scatter_start(chip_sums, "scatter_start_" + tag)
    return (names, sends, recvs, sums, lands), token


def _reduce_finish(started, after, weights, moments_m, moments_v, core, chip, tag, results):
    names, sends, recvs, sums, lands = started
    sums, lands = scatter_wait(sends, recvs, sums, lands, after, "scatter_wait_" + tag)
    reduced = [add_chips(s, r, chip, "add_chips_" + n) for n, s, r in zip(names, sums, lands)]
    others = send_to_sibling(reduced, "send_to_sibling_" + tag)
    last = None
    for n, own, other in zip(names, reduced, others):
        shape = weights[n].shape
        outs = adamw_halves(_shard2d(weights[n][0]), own, other, _shard2d(moments_m[n][0]), _shard2d(moments_v[n][0]), core,
                            "adamw_" + n)
        for store, value in zip(results, outs):
            store[n] = value.reshape(shape)
        last = outs[1]
    return last


def kernel(x, p, ffn1_pre_g, ffn1_w_gate, ffn1_w_up, ffn1_w_down, ffn1_post_g, mix_pre_g, w_in, sgu_norm_g, sgu_w, sgu_b, pool_w, pool_scale, w_out_a, w_out_b, w_o, mix_post_g, ffn2_pre_g, ffn2_w_gate, ffn2_w_up, ffn2_w_down, ffn2_post_g, ple_pre_g, ple_w_gate, ple_w_proj, ple_post_g, loss_target, m_ffn1_pre_g, m_ffn1_w_gate, m_ffn1_w_up, m_ffn1_w_down, m_ffn1_post_g, m_mix_pre_g, m_w_in, m_sgu_norm_g, m_sgu_w, m_sgu_b, m_pool_w, m_pool_scale, m_w_out_a, m_w_out_b, m_w_o, m_mix_post_g, m_ffn2_pre_g, m_ffn2_w_gate, m_ffn2_w_up, m_ffn2_w_down, m_ffn2_post_g, m_ple_pre_g, m_ple_w_gate, m_ple_w_proj, m_ple_post_g, v_ffn1_pre_g, v_ffn1_w_gate, v_ffn1_w_up, v_ffn1_w_down, v_ffn1_post_g, v_mix_pre_g, v_w_in, v_sgu_norm_g, v_sgu_w, v_sgu_b, v_pool_w, v_pool_scale, v_w_out_a, v_w_out_b, v_w_o, v_mix_post_g, v_ffn2_pre_g, v_ffn2_w_gate, v_ffn2_w_up, v_ffn2_w_down, v_ffn2_post_g, v_ple_pre_g, v_ple_w_gate, v_ple_w_proj, v_ple_post_g):
    given = dict(locals())
    weights = {n: given[n] for n in WEIGHTS}
    moments_m = {n: given["m_" + n] for n in WEIGHTS}
    moments_v = {n: given["v_" + n] for n in WEIGHTS}
    core = lax.axis_index("c").astype(jnp.int32).reshape(1)
    chip = (2 * lax.axis_index("x") + lax.axis_index("y")).astype(jnp.int32).reshape(1)

    d = x.shape[-1]
    token = ()
    gathering = {}
    for tag, names in GROUPS:
        gathering[tag], tok = _gather_group(names, weights, chip, tag, token)
        token = (tok,)
    gain = {n: weights[n] for n in GAINS}
    sgu_w3 = sgu_w[0]
    sgu_b3 = sgu_b[0][:, :, None]
    groups = dict(GROUPS)

    h0 = x[0]
    tgt = loss_target[0]
    p_bf = p[0, 0].astype(BF16)
    w = _gathered(gathering["ffn1"], groups["ffn1"], token[0], "ffn1")
    xn1 = rms_cast(h0, gain["ffn1_pre_g"], "ffn1_pre_norm")
    s1, h1, xn2 = _ffn_fwd(xn1, h0, w, "ffn1_", gain["ffn1_post_g"], gain["mix_pre_g"], "ffn1")
    w.update(_gathered(gathering["mixer"], groups["mixer"], h1, "mixer"))
    full = {n: w[n].reshape(-1, d) for n in ("w_out_a", "w_out_b", "w_o")}
    n_groups = pool_w.shape[1]
    rows_per = pool_w.shape[2]
    dgp = pool_w.shape[3]
    pool_full = w["pool_w"].reshape(N_CHIPS, n_groups, rows_per, dgp).transpose(1, 0, 2, 3).reshape(n_groups, N_CHIPS * rows_per, dgp)
    z = mixer_in(xn2, w["w_in"], "mixer_in")
    a = sgu_fwd(z, gain["sgu_norm_g"], sgu_w3, sgu_b3, "sgu_fwd")
    diff, b = pool_fwd(z, pool_full, gain["pool_scale"], "pool_fwd")
    ya, yb, y = mixer_y(a, b, z, full["w_out_a"], full["w_out_b"], "mixer_y")
    m, h2, xn3 = mm_norm_res(y[None], full["w_o"][None], h1, gain["mix_post_g"], gain["ffn2_pre_g"], 1.0, "mixer_out")
    w.update(_gathered(gathering["ffn2"], groups["ffn2"], h2, "ffn2"))
    s2, h3, xn4 = _ffn_fwd(xn3, h2, w, "ffn2_", gain["ffn2_post_g"], gain["ple_pre_g"], "ffn2")
    w.update(_gathered(gathering["ple"], groups["ple"], h3, "ple"))
    full["ple_w_gate"] = w["ple_w_gate"].reshape(-1, d)
    proj_full = w["ple_w_proj"].transpose(1, 0, 2).reshape(ple_w_proj.shape[1], -1)
    gate, e, q, dh4, loss_part = ple_fwd(xn4, p_bf, full["ple_w_gate"], proj_full, h3, gain["ple_post_g"], tgt, "ple_fwd")
    loss = lax.psum(loss_part[0, 0], ("x", "y", "c"))

    small_g, big_g = {}, {}
    grad, delta, new_m, new_v = {}, {}, {}, {}
    results = (grad, delta, new_m, new_v)
    finish = functools.partial(_reduce_finish, weights=weights, moments_m=moments_m, moments_v=moments_v, core=core, chip=chip,
                               results=results)
    ds, de, dh3, small_g["ple_post_g"], small_g["ple_pre_g"] = ple_bwd(
        dh4, q, gate, e, gain["ple_post_g"], full["ple_w_gate"], h3, gain["ple_pre_g"], "ple_bwd")
    big_g["ple_w_gate"] = dw_tn(xn4, ds, 1, "dw_ple_gate").reshape(N_CHIPS, -1, d)
    big_g["ple_w_proj"] = dw_tn(p_bf, de, N_CHIPS, "dw_ple_proj", dy_mode="cols")
    reducing_ple, tok_ple = _reduce_start(groups["ple"], big_g, core, "ple")

    dg4, du4, g2, small_g["ffn2_post_g"] = _ffn_bwd_w(dh3, s2, w, "ffn2_", gain["ffn2_post_g"], "ffn2", (tok_ple,))
    big_g.update(g2)
    dh2, small_g["ffn2_pre_g"] = _ffn_bwd_x(dh3, dg4, du4, s2, w, "ffn2_", gain["ffn2_pre_g"], "ffn2", ())
    reducing_ffn2, tok_ffn2 = _reduce_start(groups["ffn2"], big_g, core, "ffn2")
    done_ple = finish(reducing_ple, after=tok_ffn2, tag="ple")

    dm, dya, dyb, dga, dgb, da, db, small_g["mix_post_g"] = mixer_bwd_y(
        dh2, m, gain["mix_post_g"], full["w_o"], ya, yb, z, full["w_out_a"], full["w_out_b"], "mixer_bwd_y",
        deps=(tok_ffn2, done_ple))
    big_g["w_o"] = dw_tn(y, dm, 1, "dw_o").reshape(N_CHIPS, -1, d)
    big_g["w_out_a"] = dw_tn(a, dya, 1, "dw_out_a").reshape(N_CHIPS, -1, d)
    big_g["w_out_b"] = dw_tn(b, dyb, 1, "dw_out_b").reshape(N_CHIPS, -1, d)
    dzu, dzv, d_sgu_w, d_sgu_b, small_g["sgu_norm_g"] = sgu_bwd(z, da, gain["sgu_norm_g"], sgu_w3, sgu_b3, "sgu_bwd")
    dc, d_pool_w, small_g["pool_scale"] = pool_bwd(db, diff, pool_full, gain["pool_scale"], "pool_bwd")
    big_g["pool_w"] = d_pool_w.astype(BF16).reshape(n_groups, N_CHIPS, rows_per, dgp).transpose(1, 0, 2, 3).reshape(
        N_CHIPS, n_groups * rows_per, dgp)
    dz = jnp.concatenate([dzu, dzv, dc, dga, dgb], axis=1)
    big_g["w_in"] = dw_tn(xn2, dz, N_CHIPS, "dw_in", dy_mode="cols")
    dh1, small_g["mix_pre_g"] = dx_norm_bwd([(dz, w["w_in"], True)], h1, gain["mix_pre_g"], dh2, "mixer_bwd_x")
    reducing_mixer, tok_mixer = _reduce_start(groups["mixer"], big_g, core, "mixer")
    done_ffn2 = finish(reducing_ffn2, after=tok_mixer, tag="ffn2")

    dg4, du4, g1, small_g["ffn1_post_g"] = _ffn_bwd_w(dh1, s1, w, "ffn1_", gain["ffn1_post_g"], "ffn1", (tok_mixer, done_ffn2))
    big_g.update(g1)
    reducing_ffn1, tok_ffn1 = _reduce_start(groups["ffn1"], big_g, core, "ffn1")
    done_mixer = finish(reducing_mixer, after=tok_ffn1, tag="mixer")
    dh0, small_g["ffn1_pre_g"] = _ffn_bwd_x(dh1, dg4, du4, s1, w, "ffn1_", gain["ffn1_pre_g"], "ffn1", (tok_ffn1, done_mixer))
    finish(reducing_ffn1, after=dh0, tag="ffn1")
    small_g["sgu_w"] = d_sgu_w
    small_g["sgu_b"] = d_sgu_b[:, :, 0]

    packed = small_allreduce_adamw(_pack_small(small_g), _pack_small({n: weights[n] for n in SMALL}),
                                   _pack_small({n: moments_m[n] for n in SMALL}), _pack_small({n: moments_v[n] for n in SMALL}))
    like = {n: weights[n] for n in SMALL}
    for store, block in zip((grad, delta, new_m, new_v), packed):
        store.update(_unpack_small(block, like))

    return (loss, dh0[None], *[grad[n] for n in WEIGHTS], *[delta[n] for n in WEIGHTS],
            *[new_m[n] for n in WEIGHTS], *[new_v[n] for n in WEIGHTS])
```

```python
import functools

import jax
import jax.numpy as jnp
from jax import lax
from jax.experimental import pallas as pl
from jax.experimental.pallas import tpu as pltpu

F32 = jnp.float32
BF16 = jnp.bfloat16
EPS = 1e-6
CHUNK = 128
POOL_WINDOWS = (2, 4, 8, 16)
HALO = 16
N_CHIPS = 4
ADAM_LR, ADAM_B1, ADAM_B2, ADAM_EPS, ADAM_WD, ADAM_STEP = 0.001, 0.9, 0.999, 1e-08, 0.01, 10
VMEM_LIMIT_V7X = 58 * 1024 * 1024
MESH_IDS = pl.DeviceIdType.MESH
HBM_SPEC = pl.BlockSpec(memory_space=pltpu.HBM)
VMEM_SPEC = pl.BlockSpec(memory_space=pltpu.VMEM)
SEM_SPEC = pl.BlockSpec(memory_space=pltpu.SEMAPHORE)
ANY_SPEC = pl.BlockSpec(memory_space=pl.ANY)
DATAFLOW = pltpu.SideEffectType.DATAFLOW_SIDE_EFFECTING
OTHER_CHIPS = ((1, 0), (0, 1), (1, 1))
DZ_SLOT = (2, 3, 4, 0, 1)

NT = (((1,), (1,)), ((), ()))
TN = (((0,), (0,)), ((), ()))


def _params(*sem, **more):
    return pltpu.CompilerParams(dimension_semantics=sem or None, vmem_limit_bytes=VMEM_LIMIT_V7X, **more)


def _tile(t, want):
    return max(c for c in range(8, min(t, want) + 1, 8) if t % c == 0)


def _const(shape):
    return pl.BlockSpec(shape, lambda *_: (0,) * len(shape))


def _rows(tm, d, col=0):
    return pl.BlockSpec((tm, d), lambda i: (i, col))


def _kmajor(nk, tm, kb):
    return pl.BlockSpec((nk, tm, kb), lambda i: (0, i, 0))


def _dot(a, b):
    return jnp.dot(a, b, preferred_element_type=F32)


def _dot_nt(a, b):
    return lax.dot_general(a, b, NT, preferred_element_type=F32)


def _dot_tn(a, b):
    return lax.dot_general(a, b, TN, preferred_element_type=F32)


def _gelu(x):
    return 0.5 * x * (1.0 + jnp.tanh(0.7978845608028654 * (x + 0.044715 * x * x * x)))


def _gelu_and_grad(x):
    x2 = x * x
    th = jnp.tanh(0.7978845608028654 * (x + 0.044715 * x * x2))
    cdf = 0.5 * (1.0 + th)
    return x * cdf, cdf + 0.5 * x * (1.0 - th * th) * 0.7978845608028654 * (1.0 + 3.0 * 0.044715 * x2)


def _sigmoid(x):
    return 1.0 / (1.0 + jnp.exp(-x))


def _rstd(h):
    return lax.rsqrt(jnp.mean(h * h, axis=-1, keepdims=True) + EPS)


def _rms_bwd(h, g, dy):
    r = _rstd(h)
    t = dy * g
    dh = r * t - h * (r * r * r) * jnp.mean(h * t, axis=-1, keepdims=True)
    return dh, jnp.sum(dy * h * r, axis=0, keepdims=True)


def _ordered_after(body, n_in, deps):
    if not deps:
        return body
    return lambda *refs: body(*refs[:n_in], *refs[n_in + len(deps):])


def _accumulate(ref, value):
    @pl.when(pl.program_id(0) == 0)
    def _():
        ref[...] = jnp.zeros_like(ref)

    ref[...] += value


def rms_cast(h, g, name):
    t, d = h.shape
    tm = _tile(t, 512)

    def body(h_ref, g_ref, o_ref):
        hv = h_ref[...]
        o_ref[...] = (hv * _rstd(hv) * g_ref[...]).astype(BF16)

    return pl.pallas_call(
        body, name=name, grid=(t // tm,), in_specs=[_rows(tm, d), _const((1, d))], out_specs=_rows(tm, d),
        out_shape=jax.ShapeDtypeStruct((t, d), BF16), compiler_params=_params("parallel"))(h, g)


def ffn_gu(xn, wgt4, wut4, name):
    t, d = xn.shape
    nk, fk, _ = wgt4.shape
    tm = _tile(t, 512)

    def body(x_ref, wg_ref, wu_ref, a_ref, s_ref, t_ref):
        xv = x_ref[...]
        g = _dot_nt(xv, wg_ref[0])
        u = _dot_nt(xv, wu_ref[0])
        sg = _sigmoid(g)
        s = g * sg
        a_ref[0] = (s * u).astype(BF16)
        s_ref[0] = s.astype(BF16)
        t_ref[0] = (u * sg * (1.0 + g * (1.0 - sg))).astype(BF16)

    w_spec = pl.BlockSpec((1, fk, d), lambda k, i: (k, 0, 0))
    o_spec = pl.BlockSpec((1, tm, fk), lambda k, i: (k, i, 0))
    shape = jax.ShapeDtypeStruct((nk, t, fk), BF16)
    return pl.pallas_call(
        body, name=name, grid=(nk, t // tm), in_specs=[pl.BlockSpec((tm, d), lambda k, i: (i, 0)), w_spec, w_spec],
        out_specs=[o_spec] * 3, out_shape=[shape] * 3, compiler_params=_params("parallel", "parallel"))(xn, wgt4, wut4)


def mm_norm_res(a3, w3, h_old, g_post, g_next, scale, name):
    nk, t, kb = a3.shape
    d = w3.shape[2]
    tm = _tile(t, 256)

    def body(a_ref, w_ref, h_ref, gp_ref, gn_ref, f_ref, hn_ref, xn_ref):
        f = _dot(a_ref[0], w_ref[0])
        for k in range(1, nk):
            f += _dot(a_ref[k], w_ref[k])
        f_ref[...] = f
        hn = h_ref[...] + scale * (f * _rstd(f) * gp_ref[...])
        hn_ref[...] = hn
        xn_ref[...] = (hn * _rstd(hn) * gn_ref[...]).astype(BF16)

    return pl.pallas_call(
        body, name=name, grid=(t // tm,),
        in_specs=[_kmajor(nk, tm, kb), _const((nk, kb, d)), _rows(tm, d), _const((1, d)), _const((1, d))],
        out_specs=[_rows(tm, d)] * 3,
        out_shape=[jax.ShapeDtypeStruct((t, d), F32), jax.ShapeDtypeStruct((t, d), F32), jax.ShapeDtypeStruct((t, d), BF16)],
        compiler_params=_params("parallel"))(a3, w3, h_old, g_post, g_next)


def mixer_in(xn, win4, name):
    t, d = xn.shape
    nk, _, nb = win4.shape
    tm = _tile(t, 512)

    def body(x_ref, w_ref, z_ref):
        z_ref[...] = _dot(x_ref[...], w_ref[0]).astype(BF16)

    return pl.pallas_call(
        body, name=name, grid=(nk, t // tm),
        in_specs=[pl.BlockSpec((tm, d), lambda k, i: (i, 0)), pl.BlockSpec((1, d, nb), lambda k, i: (k, 0, 0))],
        out_specs=pl.BlockSpec((tm, nb), lambda k, i: (i, k)), out_shape=jax.ShapeDtypeStruct((t, nk * nb), BF16),
        compiler_params=_params("parallel", "parallel"))(xn, win4)


def _causal_mask():
    row = lax.broadcasted_iota(jnp.int32, (CHUNK, CHUNK), 0)
    col = lax.broadcasted_iota(jnp.int32, (CHUNK, CHUNK), 1)
    return row >= col


def _layernorm_parts(v):
    mu = jnp.mean(v, axis=-1, keepdims=True)
    vc = v - mu
    r = lax.rsqrt(jnp.mean(vc * vc, axis=-1, keepdims=True) + EPS)
    return vc * r, r


def sgu_fwd(z, norm_g, sgu_w, sgu_b3, name):
    t = z.shape[0]
    d = norm_g.shape[1]
    ng = sgu_w.shape[0]
    dg = d // ng
    tm = _tile(t, 256)

    def body(zu_ref, zv_ref, ng_ref, w_ref, b_ref, a_ref):
        vhat, _ = _layernorm_parts(_gelu(zv_ref[...].astype(F32)))
        vn = (vhat * ng_ref[...]).astype(BF16)
        u = _gelu(zu_ref[...].astype(F32))
        mask = _causal_mask()
        for g in range(ng):
            wg = jnp.where(mask, w_ref[g], 0.0).astype(BF16)
            for ci in range(tm // CHUNK):
                rs, cs = slice(ci * CHUNK, (ci + 1) * CHUNK), slice(g * dg, (g + 1) * dg)
                sv = _dot(wg, vn[rs, cs]) + b_ref[g]
                a_ref[rs, cs] = (u[rs, cs] * sv).astype(BF16)

    return pl.pallas_call(
        body, name=name, grid=(t // tm,),
        in_specs=[_rows(tm, d, 0), _rows(tm, d, 1), _const((1, d)), _const((ng, CHUNK, CHUNK)), _const((ng, CHUNK, 1))],
        out_specs=_rows(tm, d), out_shape=jax.ShapeDtypeStruct((t, d), BF16),
        compiler_params=_params("parallel"))(z, z, norm_g, sgu_w, sgu_b3)


def pool_fwd(z, pool_w, pool_scale, name):
    t = z.shape[0]
    d = pool_scale.shape[1]
    ng = pool_w.shape[0]
    dg = d // ng
    tm = _tile(t, 256)
    per = tm // HALO

    def body(c_ref, prev_ref, w_ref, s_ref, diff_ref, b_ref):
        i = pl.program_id(0)
        cur = c_ref[...].astype(F32)
        prev = jnp.where(i > 0, prev_ref[...].astype(F32), 0.0)
        ext = jnp.concatenate([prev, cur], axis=0)
        tok = i * tm + lax.broadcasted_iota(jnp.int32, (tm, 1), 0)
        for g, win in enumerate(POOL_WINDOWS):
            cs = slice(g * dg, (g + 1) * dg)
            s = ext[:, cs]
            sh = 1
            while sh < win:
                s = s + pltpu.roll(s, sh, 0)
                sh *= 2
            cnt = jnp.minimum(tok + 1, win).astype(F32)
            diff = (s[HALO:] / cnt - cur[:, cs]).astype(BF16)
            diff_ref[:, cs] = diff
            b_ref[:, cs] = (_dot(diff, w_ref[g]) * s_ref[:, cs]).astype(BF16)

    return pl.pallas_call(
        body, name=name, grid=(t // tm,),
        in_specs=[_rows(tm, d, 2), pl.BlockSpec((HALO, d), lambda i: (jnp.maximum(i * per - 1, 0), 2)),
                  _const((ng, dg, dg)), _const((1, d))],
        out_specs=[_rows(tm, d)] * 2, out_shape=[jax.ShapeDtypeStruct((t, d), BF16)] * 2,
        compiler_params=_params("parallel"))(z, z, pool_w, pool_scale)


def mixer_y(a, b, z, woa, wob, name):
    t, d = a.shape
    tm = _tile(t, 256)

    def body(a_ref, b_ref, ga_ref, gb_ref, wa_ref, wb_ref, ya_ref, yb_ref, y_ref):
        ya = _dot(a_ref[...], wa_ref[...])
        yb = _dot(b_ref[...], wb_ref[...])
        ya_ref[...] = ya.astype(BF16)
        yb_ref[...] = yb.astype(BF16)
        y_ref[...] = (_sigmoid(ga_ref[...].astype(F32)) * ya + _sigmoid(gb_ref[...].astype(F32)) * yb).astype(BF16)

    return pl.pallas_call(
        body, name=name, grid=(t // tm,),
        in_specs=[_rows(tm, d), _rows(tm, d), _rows(tm, d, 3), _rows(tm, d, 4), _const((d, d)), _const((d, d))],
        out_specs=[_rows(tm, d)] * 3, out_shape=[jax.ShapeDtypeStruct((t, d), BF16)] * 3,
        compiler_params=_params("parallel"))(a, b, z, z, woa, wob)


def ple_fwd(xn, p, wpg, wpp, h, g_post, target, name):
    t, d = xn.shape
    dp = p.shape[1]
    tm = _tile(t, 256)

    def body(x_ref, p_ref, wg_ref, wp_ref, h_ref, g_ref, tg_ref, gate_ref, e_ref, q_ref, dh_ref, loss_ref):
        gate = _sigmoid(_dot(x_ref[...], wg_ref[...]))
        e = _dot(p_ref[...], wp_ref[...])
        q = gate * e
        gate_ref[...] = gate.astype(BF16)
        e_ref[...] = e.astype(BF16)
        q_ref[...] = q
        err = h_ref[...] + q * _rstd(q) * g_ref[...] - tg_ref[...]
        dh_ref[...] = err * (1.0 / d)
        _accumulate(loss_ref, jnp.full(loss_ref.shape, (0.5 / d) * jnp.sum(err * err), F32))

    return pl.pallas_call(
        body, name=name, grid=(t // tm,),
        in_specs=[_rows(tm, d), _rows(tm, dp), _const((d, d)), _const((dp, d)), _rows(tm, d), _const((1, d)), _rows(tm, d)],
        out_specs=[_rows(tm, d)] * 4 + [_const((8, 128))],
        out_shape=[jax.ShapeDtypeStruct((t, d), BF16), jax.ShapeDtypeStruct((t, d), BF16), jax.ShapeDtypeStruct((t, d), F32),
                   jax.ShapeDtypeStruct((t, d), F32), jax.ShapeDtypeStruct((8, 128), F32)],
        compiler_params=_params("arbitrary"))(xn, p, wpg, wpp, h, g_post, target)


def ple_bwd(dh, q, gate, e, g_post, wpg, h, g_pre, name):
    t, d = dh.shape
    tm = _tile(t, 256)

    def body(dh_ref, q_ref, gate_ref, e_ref, gp_ref, w_ref, h_ref, gn_ref, ds_ref, de_ref, dhp_ref, dgp_ref, dgn_ref):
        dhv = dh_ref[...]
        dq, dgp = _rms_bwd(q_ref[...], gp_ref[...], dhv)
        gate = gate_ref[...].astype(F32)
        ds = (dq * e_ref[...].astype(F32) * gate * (1.0 - gate)).astype(BF16)
        ds_ref[...] = ds
        de_ref[...] = (dq * gate).astype(BF16)
        dx, dgn = _rms_bwd(h_ref[...], gn_ref[...], _dot_nt(ds, w_ref[...]))
        dhp_ref[...] = dhv + dx
        _accumulate(dgp_ref, dgp)
        _accumulate(dgn_ref, dgn)

    return pl.pallas_call(
        body, name=name, grid=(t // tm,),
        in_specs=[_rows(tm, d)] * 4 + [_const((1, d)), _const((d, d)), _rows(tm, d), _const((1, d))],
        out_specs=[_rows(tm, d)] * 3 + [_const((1, d))] * 2,
        out_shape=[jax.ShapeDtypeStruct((t, d), BF16), jax.ShapeDtypeStruct((t, d), BF16), jax.ShapeDtypeStruct((t, d), F32),
                   jax.ShapeDtypeStruct((1, d), F32), jax.ShapeDtypeStruct((1, d), F32)],
        compiler_params=_params("arbitrary"))(dh, q, gate, e, g_post, wpg, h, g_pre)


def ffn_bwd_a(dh, f, g_post, wd4, s4, t4, scale, name, deps=()):
    t, d = dh.shape
    nk, fk, _ = wd4.shape
    tm = _tile(t, 256)

    def body(dh_ref, f_ref, gp_ref, w_ref, s_ref, t_ref, df_ref, dg_ref, du_ref, dgp_ref):
        df, dgp = _rms_bwd(f_ref[...], gp_ref[...], dh_ref[...])
        df = (scale * df).astype(BF16)
        df_ref[...] = df
        _accumulate(dgp_ref, scale * dgp)
        for k in range(nk):
            da = _dot_nt(df, w_ref[k])
            du_ref[k] = (da * s_ref[k].astype(F32)).astype(BF16)
            dg_ref[k] = (da * t_ref[k].astype(F32)).astype(BF16)

    return pl.pallas_call(
        _ordered_after(body, 6, deps), name=name, grid=(t // tm,),
        in_specs=[_rows(tm, d), _rows(tm, d), _const((1, d)), _const((nk, fk, d)), _kmajor(nk, tm, fk), _kmajor(nk, tm, fk)]
        + [ANY_SPEC] * len(deps),
        out_specs=[_rows(tm, d), _kmajor(nk, tm, fk), _kmajor(nk, tm, fk), _const((1, d))],
        out_shape=[jax.ShapeDtypeStruct((t, d), BF16), jax.ShapeDtypeStruct((nk, t, fk), BF16),
                   jax.ShapeDtypeStruct((nk, t, fk), BF16), jax.ShapeDtypeStruct((1, d), F32)],
        compiler_params=_params("arbitrary"))(dh, f, g_post, wd4, s4, t4, *deps)


def dx_norm_bwd(pairs, h, g_pre, dh_in, name, deps=()):
    t, d = h.shape
    tm = _tile(t, 256)
    n = len(pairs)

    def body(*refs):
        dys, ws = refs[:n], refs[n:2 * n]
        h_ref, g_ref, dhi_ref, dho_ref, dg_ref = refs[2 * n:]
        acc = None
        for (_, w4, sections), dy_ref, w_ref in zip(pairs, dys, ws):
            if sections:
                wide = w4.shape[2]
                edges = sorted(set(range(0, 5 * d + 1, d)) | set(range(0, 5 * d + 1, wide)))
                parts = [_dot_nt(dy_ref[DZ_SLOT[lo // d], :, lo % d:lo % d + hi - lo], w_ref[lo // wide, :, lo % wide:lo % wide + hi - lo])
                         for lo, hi in zip(edges[:-1], edges[1:])]
            else:
                parts = [_dot(dy_ref[k], w_ref[k]) for k in range(w4.shape[0])]
            for part in parts:
                acc = part if acc is None else acc + part
        dx, dg = _rms_bwd(h_ref[...], g_ref[...], acc)
        dho_ref[...] = dhi_ref[...] + dx
        _accumulate(dg_ref, dg)

    dy_specs = [_kmajor(dy.shape[0], tm, dy.shape[2]) for dy, _, _ in pairs]
    return pl.pallas_call(
        _ordered_after(body, 2 * n + 3, deps), name=name, grid=(t // tm,),
        in_specs=dy_specs + [_const(w4.shape) for _, w4, _ in pairs] + [_rows(tm, d), _const((1, d)), _rows(tm, d)]
        + [ANY_SPEC] * len(deps),
        out_specs=[_rows(tm, d), _const((1, d))],
        out_shape=[jax.ShapeDtypeStruct((t, d), F32), jax.ShapeDtypeStruct((1, d), F32)],
        compiler_params=_params("arbitrary"))(*[dy for dy, _, _ in pairs], *[w4 for _, w4, _ in pairs], h, g_pre, dh_in, *deps)


def mixer_bwd_y(dh, m, g_post, w_o, ya, yb, z, woa, wob, name, deps=()):
    t, d = dh.shape
    tm = _tile(t, 256)

    def body(dh_ref, m_ref, gp_ref, wo_ref, ya_ref, yb_ref, ga_ref, gb_ref, wa_ref, wb_ref,
             dm_ref, dya_ref, dyb_ref, dz_ref, da_ref, db_ref, dgp_ref):
        dm, dgp = _rms_bwd(m_ref[...], gp_ref[...], dh_ref[...])
        dm = dm.astype(BF16)
        dm_ref[...] = dm
        _accumulate(dgp_ref, dgp)
        dy = _dot_nt(dm, wo_ref[...])
        sa = _sigmoid(ga_ref[...].astype(F32))
        sb = _sigmoid(gb_ref[...].astype(F32))
        dya = (dy * sa).astype(BF16)
        dyb = (dy * sb).astype(BF16)
        dya_ref[...] = dya
        dyb_ref[...] = dyb
        dz_ref[0] = (dy * ya_ref[...].astype(F32) * sa * (1.0 - sa)).astype(BF16)
        dz_ref[1] = (dy * yb_ref[...].astype(F32) * sb * (1.0 - sb)).astype(BF16)
        da_ref[...] = _dot_nt(dya, wa_ref[...]).astype(BF16)
        db_ref[...] = _dot_nt(dyb, wb_ref[...]).astype(BF16)

    return pl.pallas_call(
        _ordered_after(body, 10, deps), name=name, grid=(t // tm,),
        in_specs=[_rows(tm, d), _rows(tm, d), _const((1, d)), _const((d, d)), _rows(tm, d), _rows(tm, d),
                  _rows(tm, d, 3), _rows(tm, d, 4), _const((d, d)), _const((d, d))] + [ANY_SPEC] * len(deps),
        out_specs=[_rows(tm, d)] * 3 + [pl.BlockSpec((2, tm, d), lambda i: (0, i, 0))] + [_rows(tm, d)] * 2 + [_const((1, d))],
        out_shape=[jax.ShapeDtypeStruct((t, d), BF16)] * 3 + [jax.ShapeDtypeStruct((5, t, d), BF16)]
        + [jax.ShapeDtypeStruct((t, d), BF16)] * 2 + [jax.ShapeDtypeStruct((1, d), F32)],
        compiler_params=_params("arbitrary"))(dh, m, g_post, w_o, ya, yb, z, z, woa, wob, *deps)


def sgu_bwd(z, da, dz, norm_g, sgu_w, sgu_b3, name):
    t, d = da.shape
    ng = sgu_w.shape[0]
    dg = d // ng
    tm = _tile(t, 256)
    steps = t // tm

    def body(zu_ref, zv_ref, da_ref, ng_ref, w_ref, b_ref, _, dz_ref, dw_ref, db_ref, dng_ref, dvn_ref, dsv_ref):
        i = pl.program_id(0)
        zv = zv_ref[...].astype(F32)
        zu = zu_ref[...].astype(F32)
        v, gv = _gelu_and_grad(zv)
        vhat, r = _layernorm_parts(v)
        gain = ng_ref[...]
        vn = (vhat * gain).astype(BF16)
        u, gu = _gelu_and_grad(zu)
        dav = da_ref[...].astype(F32)
        mask = _causal_mask()

        @pl.when(i == 0)
        def _():
            dw_ref[...] = jnp.zeros_like(dw_ref)
            dsv_ref[...] = jnp.zeros_like(dsv_ref)

        for g in range(ng):
            wg = jnp.where(mask, w_ref[g], 0.0).astype(BF16)
            dw = jnp.zeros((CHUNK, CHUNK), F32)
            dsv_sum = jnp.zeros((CHUNK, dg), F32)
            for ci in range(tm // CHUNK):
                rs, cs = slice(ci * CHUNK, (ci + 1) * CHUNK), slice(g * dg, (g + 1) * dg)
                vn_blk = vn[rs, cs]
                sv = _dot(wg, vn_blk) + b_ref[g]
                dz_ref[0, rs, cs] = (dav[rs, cs] * sv * gu[rs, cs]).astype(BF16)
                dsv = dav[rs, cs] * u[rs, cs]
                dsv_sum += dsv
                dsv = dsv.astype(BF16)
                dw += _dot_nt(dsv, vn_blk)
                dvn_ref[rs, cs] = _dot_tn(wg, dsv)
            dw_ref[g] += dw
            dsv_ref[:, cs] += dsv_sum

        dvn = dvn_ref[...]
        _accumulate(dng_ref, jnp.sum(dvn * vhat, axis=0, keepdims=True))
        dvh = dvn * gain
        dv = r * (dvh - jnp.mean(dvh, axis=-1, keepdims=True) - vhat * jnp.mean(dvh * vhat, axis=-1, keepdims=True))
        dz_ref[1] = (dv * gv).astype(BF16)

        @pl.when(i == steps - 1)
        def _():
            for g in range(ng):
                dw_ref[g] = jnp.where(mask, dw_ref[g], 0.0)
                row_sum = jnp.sum(dsv_ref[:, g * dg:(g + 1) * dg], axis=1, keepdims=True)
                db_ref[g] = jnp.broadcast_to(row_sum, (CHUNK, CHUNK))

    return pl.pallas_call(
        body, name=name, grid=(steps,),
        in_specs=[_rows(tm, d, 0), _rows(tm, d, 1), _rows(tm, d), _const((1, d)), _const((ng, CHUNK, CHUNK)), _const((ng, CHUNK, 1)),
                  ANY_SPEC],
        out_specs=[pl.BlockSpec((2, tm, d), lambda i: (DZ_SLOT[0] // 2, i, 0)), _const((ng, CHUNK, CHUNK)),
                   _const((ng, CHUNK, CHUNK)), _const((1, d))],
        out_shape=[jax.ShapeDtypeStruct(dz.shape, BF16), jax.ShapeDtypeStruct((ng, CHUNK, CHUNK), F32),
                   jax.ShapeDtypeStruct((ng, CHUNK, CHUNK), F32), jax.ShapeDtypeStruct((1, d), F32)],
        scratch_shapes=[pltpu.VMEM((tm, d), F32), pltpu.VMEM((CHUNK, d), F32)], input_output_aliases={6: 0},
        compiler_params=_params("arbitrary"))(z, z, da, norm_g, sgu_w, sgu_b3, dz)


def pool_bwd(db, diff, dz, pool_w, pool_scale, name):
    t, d = db.shape
    ng = pool_w.shape[0]
    dg = d // ng
    tm = _tile(t, 256)
    per = tm // HALO
    steps = t // tm

    def body(db_ref, next_ref, diff_ref, w_ref, s_ref, _, dc_ref, dw_ref, ds_ref):
        i = pl.program_id(0)
        dbc = db_ref[...].astype(F32)
        nxt = jnp.where(i < steps - 1, next_ref[...].astype(F32), 0.0)
        ext = jnp.concatenate([dbc, nxt], axis=0)
        rows = tm + HALO
        tok = i * tm + lax.broadcasted_iota(jnp.int32, (rows, 1), 0)

        @pl.when(i == 0)
        def _():
            dw_ref[...] = jnp.zeros_like(dw_ref)
            ds_ref[...] = jnp.zeros_like(ds_ref)

        for g, win in enumerate(POOL_WINDOWS):
            cs = slice(g * dg, (g + 1) * dg)
            dp = (ext[:, cs] * s_ref[:, cs]).astype(BF16)
            dd = _dot_nt(dp, w_ref[g])
            s = dd / jnp.minimum(tok + 1, win).astype(F32)
            sh = 1
            while sh < win:
                s = s + pltpu.roll(s, rows - sh, 0)
                sh *= 2
            dc_ref[0, :, cs] = (s[:tm] - dd[:tm]).astype(BF16)
            dfg = diff_ref[:, cs]
            ds_ref[:, cs] += jnp.sum(dbc[:, cs] * _dot(dfg, w_ref[g]), axis=0, keepdims=True)
            dw_ref[g] += _dot_tn(dfg, dp[:tm])

    return pl.pallas_call(
        body, name=name, grid=(steps,),
        in_specs=[_rows(tm, d), pl.BlockSpec((HALO, d), lambda i: (jnp.minimum((i + 1) * per, t // HALO - 1), 0)),
                  _rows(tm, d), _const((ng, dg, dg)), _const((1, d)), ANY_SPEC],
        out_specs=[pl.BlockSpec((1, tm, d), lambda i: (DZ_SLOT[2], i, 0)), _const((ng, dg, dg)), _const((1, d))],
        out_shape=[jax.ShapeDtypeStruct(dz.shape, BF16), jax.ShapeDtypeStruct((ng, dg, dg), F32), jax.ShapeDtypeStruct((1, d), F32)],
        input_output_aliases={5: 0},
        compiler_params=_params("arbitrary"))(db, db, diff, pool_w, pool_scale, dz)


def dw_tn(x, dy, nk, name, x_kmajor=False, dy_mode="same"):
    t = x.shape[-2]
    kx = x.shape[-1]
    n = dy.shape[-1] // nk if dy_mode == "cols" else dy.shape[-1]
    tt = _tile(t, 512)
    steps = t // tt

    def body(x_ref, dy_ref, o_ref, acc_ref):
        s = pl.program_id(1)

        @pl.when(s == 0)
        def _():
            acc_ref[...] = jnp.zeros_like(acc_ref)

        acc_ref[...] += _dot_tn(x_ref[0] if x_kmajor else x_ref[...], dy_ref[0] if dy_mode == "kmajor" else dy_ref[...])

        @pl.when(s == steps - 1)
        def _():
            o_ref[0] = acc_ref[...].astype(BF16)

    x_spec = pl.BlockSpec((1, tt, kx), lambda k, s: (k, s, 0)) if x_kmajor else pl.BlockSpec((tt, kx), lambda k, s: (s, 0))
    dy_spec = {"kmajor": pl.BlockSpec((1, tt, n), lambda k, s: (k, s, 0)), "cols": pl.BlockSpec((tt, n), lambda k, s: (s, k)),
               "same": pl.BlockSpec((tt, n), lambda k, s: (s, 0))}[dy_mode]
    return pl.pallas_call(
        body, name=name, grid=(nk, steps), in_specs=[x_spec, dy_spec],
        out_specs=pl.BlockSpec((1, kx, n), lambda k, s: (k, 0, 0)), out_shape=jax.ShapeDtypeStruct((nk, kx, n), BF16),
        scratch_shapes=[pltpu.VMEM((kx, n), F32)], compiler_params=_params("parallel", "arbitrary"))(x, dy)


def _place():
    x, y, c = lax.axis_index("x"), lax.axis_index("y"), lax.axis_index("c")
    chips = [((1 - x) if fx else x, (1 - y) if fy else y) for fx, fy in OTHER_CHIPS]
    return x, y, c, chips


def _half(rows, which):
    return pl.ds(pl.multiple_of(which * (rows // 2), 16), rows // 2)


def _hbm(a):
    return pltpu.with_memory_space_constraint(a, pltpu.HBM)


def cast_place(w2d, chip, name, deps=()):
    r, cdim = w2d.shape
    tr = _tile(r, 512)

    def body(chip_ref, w_ref, *rest):
        rest[-1][0] = w_ref[...].astype(BF16)

    return pl.pallas_call(
        body, name=name,
        grid_spec=pltpu.PrefetchScalarGridSpec(
            num_scalar_prefetch=1, grid=(r // tr,),
            in_specs=[pl.BlockSpec((tr, cdim), lambda i, chip_ref: (i, 0))] + [ANY_SPEC] * len(deps),
            out_specs=pl.BlockSpec((1, tr, cdim), lambda i, chip_ref: (chip_ref[0], i, 0))),
        out_shape=jax.ShapeDtypeStruct((N_CHIPS, r, cdim), BF16), compiler_params=_params("parallel"))(chip, w2d, *deps)


def _gather_copy(buf, sends, recvs, i, j, me, chip_xy, c):
    cx, cy = chip_xy
    mine = _half(buf.shape[1], c)
    return pltpu.make_async_remote_copy(
        src_ref=buf.at[me, mine], dst_ref=buf.at[me, mine], send_sem=sends.at[3 * i + j], recv_sem=recvs.at[3 * i + j],
        device_id=(cx, cy, c), device_id_type=MESH_IDS)


def allgather_start(bufs, name):
    n = len(bufs)

    def body(*refs):
        ins = refs[:n]
        sends, recvs = refs[n], refs[n + 1]
        token = refs[2 * n + 2]
        x, y, c, chips = _place()
        for i in range(n):
            for j, chip_xy in enumerate(chips):
                _gather_copy(ins[i], sends, recvs, i, j, 2 * x + y, chip_xy, c).start()
        token[...] = jnp.zeros_like(token)

    out = pl.pallas_call(
        body, name=name, in_specs=[HBM_SPEC] * n,
        out_specs=[SEM_SPEC, SEM_SPEC] + [HBM_SPEC] * n + [VMEM_SPEC],
        out_shape=[pltpu.SemaphoreType.DMA((3 * n,)), pltpu.SemaphoreType.DMA((3 * n,))]
        + [pltpu.HBM(b.shape, b.dtype) for b in bufs] + [jax.ShapeDtypeStruct((8, 128), F32)],
        input_output_aliases={i: i + 2 for i in range(n)},
        compiler_params=pltpu.CompilerParams(has_side_effects=DATAFLOW))(*[_hbm(b) for b in bufs])
    return out[0], out[1], list(out[2:2 + n]), out[2 + n]


def allgather_wait(sends, recvs, bufs, after, name):
    n = len(bufs)

    def body(*refs):
        ins = refs[:n]
        send_sems, recv_sems = refs[n], refs[n + 1]
        x, y, c, chips = _place()
        for i in range(n):
            for j, (cx, cy) in enumerate(chips):
                mine = _half(ins[i].shape[1], c)
                cp = pltpu.make_async_remote_copy(
                    src_ref=ins[i].at[2 * x + y, mine], dst_ref=ins[i].at[2 * cx + cy, mine], send_sem=send_sems.at[3 * i + j],
                    recv_sem=recv_sems.at[3 * i + j], device_id=(cx, cy, c), device_id_type=MESH_IDS)
                cp.wait_send()
                cp.wait_recv()

    return pl.pallas_call(
        body, name=name, in_specs=[HBM_SPEC] * n + [SEM_SPEC, SEM_SPEC, ANY_SPEC], out_specs=[HBM_SPEC] * n,
        out_shape=[pltpu.HBM(b.shape, b.dtype) for b in bufs], input_output_aliases={i: i for i in range(n)},
        compiler_params=pltpu.CompilerParams(has_side_effects=DATAFLOW))(*bufs, sends, recvs, after)


def d2d_forward(bufs, name):
    n = len(bufs)

    def body(*refs):
        ins = refs[:n]
        send_sems, recv_sems = refs[2 * n:]
        x, y, c, chips = _place()
        copies = []
        for i in range(n):
            mine = _half(ins[i].shape[1], c)
            for j, (cx, cy) in enumerate(chips):
                landed = ins[i].at[2 * cx + cy, mine]
                cp = pltpu.make_async_remote_copy(
                    src_ref=landed, dst_ref=landed, send_sem=send_sems.at[i, j], recv_sem=recv_sems.at[i, j],
                    device_id=(x, y, 1 - c), device_id_type=MESH_IDS)
                cp.start()
                copies.append(cp)
        for i in range(n):
            theirs = _half(ins[i].shape[1], 1 - c)
            for j, (cx, cy) in enumerate(chips):
                passed = ins[i].at[2 * cx + cy, theirs]
                pltpu.make_async_remote_copy(
                    src_ref=passed, dst_ref=passed, send_sem=send_sems.at[i, j], recv_sem=recv_sems.at[i, j],
                    device_id=(x, y, 1 - c), device_id_type=MESH_IDS).wait_recv()
        for cp in copies:
            cp.wait_send()

    return pl.pallas_call(
        body, name=name, in_specs=[HBM_SPEC] * n, out_specs=[HBM_SPEC] * n,
        out_shape=[jax.ShapeDtypeStruct(b.shape, b.dtype) for b in bufs], input_output_aliases={i: i for i in range(n)},
        scratch_shapes=[pltpu.SemaphoreType.DMA((n, 3))] * 2,
        compiler_params=pltpu.CompilerParams(has_side_effects=True))(*bufs)


def exchange_halves(grads, name):
    n = len(grads)

    def body(*refs):
        ins, outs = refs[:n], refs[n:2 * n]
        send_sems, recv_sems = refs[2 * n:]
        x, y, c, _ = _place()
        copies = []
        for i in range(n):
            cp = pltpu.make_async_remote_copy(
                src_ref=ins[i].at[:, _half(ins[i].shape[1], 1 - c)], dst_ref=outs[i], send_sem=send_sems.at[i],
                recv_sem=recv_sems.at[i], device_id=(x, y, 1 - c), device_id_type=MESH_IDS)
            cp.start()
            copies.append(cp)
        for cp in copies:
            cp.wait()

    return pl.pallas_call(
        body, name=name, in_specs=[HBM_SPEC] * n, out_specs=[HBM_SPEC] * n,
        out_shape=[jax.ShapeDtypeStruct((g.shape[0], g.shape[1] // 2, g.shape[2]), g.dtype) for g in grads],
        scratch_shapes=[pltpu.SemaphoreType.DMA((n,))] * 2,
        compiler_params=pltpu.CompilerParams(has_side_effects=True))(*grads)


def _scatter_copy(src, land, sends, recvs, i, j, chip_xy, c):
    cx, cy = chip_xy
    return pltpu.make_async_remote_copy(
        src_ref=src.at[2 * cx + cy], dst_ref=land.at[j], send_sem=sends.at[3 * i + j], recv_sem=recvs.at[3 * i + j],
        device_id=(cx, cy, c), device_id_type=MESH_IDS)


def scatter_start(sums, name):
    n = len(sums)
    lands = [lax.empty((3,) + s.shape[1:], s.dtype) for s in sums]

    def body(*refs):
        srcs, zones = refs[:n], refs[n:2 * n]
        sends, recvs = refs[2 * n], refs[2 * n + 1]
        token = refs[4 * n + 2]
        _, _, c, chips = _place()
        for i in range(n):
            for j, chip_xy in enumerate(chips):
                _scatter_copy(srcs[i], zones[i], sends, recvs, i, j, chip_xy, c).start()
        token[...] = jnp.zeros_like(token)

    out = pl.pallas_call(
        body, name=name, in_specs=[HBM_SPEC] * (2 * n),
        out_specs=[SEM_SPEC, SEM_SPEC] + [HBM_SPEC] * (2 * n) + [VMEM_SPEC],
        out_shape=[pltpu.SemaphoreType.DMA((3 * n,)), pltpu.SemaphoreType.DMA((3 * n,))]
        + [pltpu.HBM(a.shape, a.dtype) for a in sums + lands] + [jax.ShapeDtypeStruct((8, 128), F32)],
        input_output_aliases={i: i + 2 for i in range(2 * n)},
        compiler_params=pltpu.CompilerParams(has_side_effects=DATAFLOW))(*[_hbm(a) for a in sums + lands])
    return out[0], out[1], list(out[2:2 + n]), list(out[2 + n:2 + 2 * n]), out[2 + 2 * n]


def scatter_wait(sends, recvs, sums, lands, after, name):
    n = len(sums)

    def body(*refs):
        srcs, zones = refs[:n], refs[n:2 * n]
        send_sems, recv_sems = refs[2 * n], refs[2 * n + 1]
        _, _, c, chips = _place()
        for i in range(n):
            for j, chip_xy in enumerate(chips):
                cp = _scatter_copy(srcs[i], zones[i], send_sems, recv_sems, i, j, chip_xy, c)
                cp.wait_send()
                cp.wait_recv()

    out = pl.pallas_call(
        body, name=name, in_specs=[HBM_SPEC] * (2 * n) + [SEM_SPEC, SEM_SPEC, ANY_SPEC], out_specs=[HBM_SPEC] * (2 * n),
        out_shape=[pltpu.HBM(a.shape, a.dtype) for a in sums + lands], input_output_aliases={i: i for i in range(2 * n)},
        compiler_params=pltpu.CompilerParams(has_side_effects=DATAFLOW))(*sums, *lands, sends, recvs, after)
    return list(out[:n]), list(out[n:])


def send_to_sibling(halves, name):
    n = len(halves)

    def body(*refs):
        ins, outs = refs[:n], refs[n:2 * n]
        send_sems, recv_sems = refs[2 * n:]
        x, y, c, _ = _place()
        copies = []
        for i in range(n):
            cp = pltpu.make_async_remote_copy(
                src_ref=ins[i], dst_ref=outs[i], send_sem=send_sems.at[i], recv_sem=recv_sems.at[i],
                device_id=(x, y, 1 - c), device_id_type=MESH_IDS)
            cp.start()
            copies.append(cp)
        for cp in copies:
            cp.wait()

    return pl.pallas_call(
        body, name=name, in_specs=[HBM_SPEC] * n, out_specs=[HBM_SPEC] * n,
        out_shape=[jax.ShapeDtypeStruct(h.shape, h.dtype) for h in halves],
        scratch_shapes=[pltpu.SemaphoreType.DMA((n,))] * 2,
        compiler_params=pltpu.CompilerParams(has_side_effects=True))(*halves)


def add_halves(grad4, recv4, core, name):
    nk, r, cdim = grad4.shape
    half = r // 2
    view = grad4.reshape(nk, 2, half, cdim)

    def body(core_ref, g_ref, r_ref, o_ref):
        o_ref[0] = (g_ref[0, 0].astype(F32) + r_ref[0].astype(F32)).astype(BF16)

    return pl.pallas_call(
        body, name=name,
        grid_spec=pltpu.PrefetchScalarGridSpec(
            num_scalar_prefetch=1, grid=(nk,),
            in_specs=[pl.BlockSpec((1, 1, half, cdim), lambda k, core_ref: (k, core_ref[0], 0, 0)),
                      pl.BlockSpec((1, half, cdim), lambda k, core_ref: (k, 0, 0))],
            out_specs=pl.BlockSpec((1, half, cdim), lambda k, core_ref: (k, 0, 0))),
        out_shape=jax.ShapeDtypeStruct((nk, half, cdim), BF16), compiler_params=_params("parallel"))(core, view, recv4)


def add_chips(sum4, recv3, chip, name):
    _, half, cdim = sum4.shape

    def body(chip_ref, s_ref, r_ref, o_ref):
        o_ref[...] = ((s_ref[0].astype(F32) + r_ref[0].astype(F32)) + r_ref[1].astype(F32)) + r_ref[2].astype(F32)

    return pl.pallas_call(
        body, name=name,
        grid_spec=pltpu.PrefetchScalarGridSpec(
            num_scalar_prefetch=1, grid=(1,),
            in_specs=[pl.BlockSpec((1, half, cdim), lambda k, chip_ref: (chip_ref[0], 0, 0)),
                      pl.BlockSpec((3, half, cdim), lambda k, chip_ref: (0, 0, 0))],
            out_specs=pl.BlockSpec((half, cdim), lambda k, chip_ref: (0, 0))),
        out_shape=jax.ShapeDtypeStruct((half, cdim), F32), compiler_params=_params("arbitrary"))(chip, sum4, recv3)


def _adamw_math(w, g, m, v):
    m = ADAM_B1 * m + (1.0 - ADAM_B1) * g
    v = ADAM_B2 * v + (1.0 - ADAM_B2) * (g * g)
    m_hat = m / (1.0 - ADAM_B1 ** ADAM_STEP)
    v_hat = v / (1.0 - ADAM_B2 ** ADAM_STEP)
    return -ADAM_LR * (m_hat / (jnp.sqrt(v_hat) + ADAM_EPS) + ADAM_WD * w), m, v


def adamw_halves(w, own, other, m, v, core, name):
    r, cdim = w.shape
    half = r // 2
    tr = _tile(half, 256)
    steps = half // tr

    def body(core_ref, w_ref, own_ref, other_ref, m_ref, v_ref, g_ref, d_ref, mo_ref, vo_ref):
        g = jnp.where(pl.program_id(0) == core_ref[0], own_ref[...], other_ref[...])
        g_ref[...] = g
        d_ref[...], mo_ref[...], vo_ref[...] = _adamw_math(w_ref[...], g, m_ref[...], v_ref[...])

    whole = pl.BlockSpec((tr, cdim), lambda hh, i, core_ref: (hh * steps + i, 0))
    part = pl.BlockSpec((tr, cdim), lambda hh, i, core_ref: (i, 0))
    return pl.pallas_call(
        body, name=name,
        grid_spec=pltpu.PrefetchScalarGridSpec(
            num_scalar_prefetch=1, grid=(2, steps), in_specs=[whole, part, part, whole, whole], out_specs=[whole] * 4),
        out_shape=[jax.ShapeDtypeStruct((r, cdim), F32)] * 4,
        compiler_params=_params("parallel", "parallel"))(core, w, own, other, m, v)


def small_allreduce_adamw(g, w, m, v):
    r, cdim = g.shape

    def body(g_ref, w_ref, m_ref, v_ref, go_ref, d_ref, mo_ref, vo_ref, pair, quad, d2d_send, d2d_recv, ici_send, ici_recv):
        x, y, c, chips = _place()
        me = 2 * x + y
        pair[c] = g_ref[...]
        swap = pltpu.make_async_remote_copy(
            src_ref=g_ref, dst_ref=pair.at[c], send_sem=d2d_send, recv_sem=d2d_recv, device_id=(x, y, 1 - c),
            device_id_type=MESH_IDS)
        swap.start()
        swap.wait()
        quad[me] = pair[0] + pair[1]
        copies = []
        for j, (cx, cy) in enumerate(chips):
            cp = pltpu.make_async_remote_copy(
                src_ref=quad.at[me], dst_ref=quad.at[me], send_sem=ici_send.at[j], recv_sem=ici_recv.at[j],
                device_id=(cx, cy, c), device_id_type=MESH_IDS)
            cp.start()
            copies.append(cp)
        for j, (cx, cy) in enumerate(chips):
            slot = quad.at[2 * cx + cy]
            pltpu.make_async_remote_copy(
                src_ref=slot, dst_ref=slot, send_sem=ici_send.at[j], recv_sem=ici_recv.at[j], device_id=(cx, cy, c),
                device_id_type=MESH_IDS).wait_recv()
        for cp in copies:
            cp.wait_send()
        total = (quad[0] + quad[1]) + (quad[2] + quad[3])
        go_ref[...] = total
        d_ref[...], mo_ref[...], vo_ref[...] = _adamw_math(w_ref[...], total, m_ref[...], v_ref[...])

    return pl.pallas_call(
        body, name="small_allreduce_adamw", in_specs=[VMEM_SPEC] * 4, out_specs=[VMEM_SPEC] * 4,
        out_shape=[jax.ShapeDtypeStruct((r, cdim), F32)] * 4,
        scratch_shapes=[pltpu.VMEM((2, r, cdim), F32), pltpu.VMEM((N_CHIPS, r, cdim), F32), pltpu.SemaphoreType.DMA,
                        pltpu.SemaphoreType.DMA, pltpu.SemaphoreType.DMA((3,)), pltpu.SemaphoreType.DMA((3,))],
        compiler_params=pltpu.CompilerParams(has_side_effects=True, vmem_limit_bytes=VMEM_LIMIT_V7X))(g, w, m, v)


GROUPS = (
    ("ffn1", ("ffn1_w_gate", "ffn1_w_up", "ffn1_w_down")),
    ("mixer", ("w_in", "pool_w", "w_out_a", "w_out_b", "w_o")),
    ("ffn2", ("ffn2_w_gate", "ffn2_w_up", "ffn2_w_down")),
    ("ple", ("ple_w_gate", "ple_w_proj")),
)
GAINS = ("ffn1_pre_g", "ffn1_post_g", "mix_pre_g", "sgu_norm_g", "pool_scale", "mix_post_g",
         "ffn2_pre_g", "ffn2_post_g", "ple_pre_g", "ple_post_g")
SMALL = GAINS + ("sgu_b", "sgu_w")
WEIGHTS = ("ffn1_pre_g", "ffn1_w_gate", "ffn1_w_up", "ffn1_w_down", "ffn1_post_g", "mix_pre_g", "w_in", "sgu_norm_g",
           "sgu_w", "sgu_b", "pool_w", "pool_scale", "w_out_a", "w_out_b", "w_o", "mix_post_g", "ffn2_pre_g",
           "ffn2_w_gate", "ffn2_w_up", "ffn2_w_down", "ffn2_post_g", "ple_pre_g", "ple_w_gate", "ple_w_proj", "ple_post_g")
SMALL_ROWS = 8


TRANSPOSED = ("ffn1_w_gate", "ffn1_w_up", "ffn2_w_gate", "ffn2_w_up")


def _shard2d(name, a):
    a = a[0]
    return a.T if name in TRANSPOSED else a.reshape(-1, a.shape[-1])


def _unshard2d(name, a2d, shape):
    return (a2d.T if name in TRANSPOSED else a2d).reshape(shape)


def _pack_small(parts):
    rows = []
    for name in SMALL:
        a = parts[name].astype(F32).reshape(-1)
        width = 1024
        n_rows = -(-a.shape[0] // width)
        padded = -(-n_rows // SMALL_ROWS) * SMALL_ROWS
        rows.append(jnp.pad(a, (0, padded * width - a.shape[0])).reshape(padded, width))
    return jnp.concatenate(rows, axis=0)


def _unpack_small(packed, like):
    out, row = {}, 0
    for name in SMALL:
        size = like[name].size
        n_rows = -(-size // 1024)
        padded = -(-n_rows // SMALL_ROWS) * SMALL_ROWS
        out[name] = packed[row:row + padded].reshape(-1)[:size].reshape(like[name].shape)
        row += padded
    return out


def _ffn_fwd(xn, h, w, pre, g_post, g_next, tag):
    a4, s4, t4 = ffn_gu(xn, w[pre + "w_gate"], w[pre + "w_up"], tag + "_gu")
    f, h_new, xn_next = mm_norm_res(a4, w[pre + "w_down"], h, g_post, g_next, 0.5, tag + "_down")
    return dict(xn=xn, h=h, a4=a4, s4=s4, t4=t4, f=f), h_new, xn_next


def _ffn_bwd_w(dh, saved, w, pre, g_post, tag, deps):
    df, dg4, du4, d_post = ffn_bwd_a(dh, saved["f"], g_post, w[pre + "w_down"], saved["s4"], saved["t4"], 0.5, tag + "_bwd_a",
                                     deps=deps)
    nk = N_CHIPS
    grads = {
        pre + "w_down": dw_tn(saved["a4"], df, nk, tag + "_dw_down", x_kmajor=True, dy_mode="same"),
        pre + "w_gate": dw_tn(dg4, saved["xn"], nk, tag + "_dw_gate", x_kmajor=True, dy_mode="same"),
        pre + "w_up": dw_tn(du4, saved["xn"], nk, tag + "_dw_up", x_kmajor=True, dy_mode="same"),
    }
    return dg4, du4, grads, d_post


def _ffn_bwd_x(dh, dg4, du4, saved, w, pre, g_pre, tag, deps):
    return dx_norm_bwd([(dg4, w[pre + "w_gate"], False), (du4, w[pre + "w_up"], False)], saved["h"], g_pre, dh,
                       tag + "_bwd_x", deps=deps)


def _gather_group(names, weights, chip, tag, deps):
    bufs = [cast_place(_shard2d(n, weights[n]), chip, "cast_" + n, deps=deps) for n in names]
    sends, recvs, bufs, token = allgather_start(bufs, "allgather_start_" + tag)
    return (sends, recvs, bufs), token


def _gathered(started, names, after, tag):
    sends, recvs, bufs = started
    landed = allgather_wait(sends, recvs, bufs, after, "allgather_wait_" + tag)
    return dict(zip(names, d2d_forward(landed, "d2d_forward_" + tag)))


def _reduce_start(names, big_g, core, tag):
    partial = [big_g[n] for n in names]
    from_sibling = exchange_halves(partial, "exchange_halves_" + tag)
    chip_sums = [add_halves(g, r, core, "add_halves_" + n) for n, g, r in zip(names, partial, from_sibling)]
    sends, recvs, sums, lands, token = scatter_start(chip_sums, "scatter_start_" + tag)
    return (names, sends, recvs, sums, lands), token


def _reduce_finish(started, after, weights, moments_m, moments_v, core, chip, tag, results):
    names, sends, recvs, sums, lands = started
    sums, lands = scatter_wait(sends, recvs, sums, lands, after, "scatter_wait_" + tag)
    reduced = [add_chips(s, r, chip, "add_chips_" + n) for n, s, r in zip(names, sums, lands)]
    others = send_to_sibling(reduced, "send_to_sibling_" + tag)
    last = None
    for n, own, other in zip(names, reduced, others):
        shape = weights[n].shape
        outs = adamw_halves(_shard2d(n, weights[n]), own, other, _shard2d(n, moments_m[n]), _shard2d(n, moments_v[n]), core,
                            "adamw_" + n)
        for store, value in zip(results, outs):
            store[n] = _unshard2d(n, value, shape)
        last = outs[1]
    return last


def kernel(x, p, ffn1_pre_g, ffn1_w_gate, ffn1_w_up, ffn1_w_down, ffn1_post_g, mix_pre_g, w_in, sgu_norm_g, sgu_w, sgu_b, pool_w, pool_scale, w_out_a, w_out_b, w_o, mix_post_g, ffn2_pre_g, ffn2_w_gate, ffn2_w_up, ffn2_w_down, ffn2_post_g, ple_pre_g, ple_w_gate, ple_w_proj, ple_post_g, loss_target, m_ffn1_pre_g, m_ffn1_w_gate, m_ffn1_w_up, m_ffn1_w_down, m_ffn1_post_g, m_mix_pre_g, m_w_in, m_sgu_norm_g, m_sgu_w, m_sgu_b, m_pool_w, m_pool_scale, m_w_out_a, m_w_out_b, m_w_o, m_mix_post_g, m_ffn2_pre_g, m_ffn2_w_gate, m_ffn2_w_up, m_ffn2_w_down, m_ffn2_post_g, m_ple_pre_g, m_ple_w_gate, m_ple_w_proj, m_ple_post_g, v_ffn1_pre_g, v_ffn1_w_gate, v_ffn1_w_up, v_ffn1_w_down, v_ffn1_post_g, v_mix_pre_g, v_w_in, v_sgu_norm_g, v_sgu_w, v_sgu_b, v_pool_w, v_pool_scale, v_w_out_a, v_w_out_b, v_w_o, v_mix_post_g, v_ffn2_pre_g, v_ffn2_w_gate, v_ffn2_w_up, v_ffn2_w_down, v_ffn2_post_g, v_ple_pre_g, v_ple_w_gate, v_ple_w_proj, v_ple_post_g):
    given = dict(locals())
    weights = {n: given[n] for n in WEIGHTS}
    moments_m = {n: given["m_" + n] for n in WEIGHTS}
    moments_v = {n: given["v_" + n] for n in WEIGHTS}
    core = lax.axis_index("c").astype(jnp.int32).reshape(1)
    chip = (2 * lax.axis_index("x") + lax.axis_index("y")).astype(jnp.int32).reshape(1)

    d = x.shape[-1]
    token = ()
    gathering = {}
    for tag, names in GROUPS:
        gathering[tag], tok = _gather_group(names, weights, chip, tag, token)
        token = (tok,)
    gain = {n: weights[n] for n in GAINS}
    sgu_w3 = sgu_w[0]
    sgu_b3 = sgu_b[0][:, :, None]
    groups = dict(GROUPS)

    h0 = x[0]
    tgt = loss_target[0]
    p_bf = p[0, 0].astype(BF16)
    w = _gathered(gathering["ffn1"], groups["ffn1"], token[0], "ffn1")
    xn1 = rms_cast(h0, gain["ffn1_pre_g"], "ffn1_pre_norm")
    s1, h1, xn2 = _ffn_fwd(xn1, h0, w, "ffn1_", gain["ffn1_post_g"], gain["mix_pre_g"], "ffn1")
    w.update(_gathered(gathering["mixer"], groups["mixer"], h1, "mixer"))
    full = {n: w[n].reshape(-1, d) for n in ("w_out_a", "w_out_b", "w_o")}
    n_groups = pool_w.shape[1]
    rows_per = pool_w.shape[2]
    dgp = pool_w.shape[3]
    pool_full = w["pool_w"].reshape(N_CHIPS, n_groups, rows_per, dgp).transpose(1, 0, 2, 3).reshape(n_groups, N_CHIPS * rows_per, dgp)
    z = mixer_in(xn2, w["w_in"], "mixer_in")
    a = sgu_fwd(z, gain["sgu_norm_g"], sgu_w3, sgu_b3, "sgu_fwd")
    diff, b = pool_fwd(z, pool_full, gain["pool_scale"], "pool_fwd")
    ya, yb, y = mixer_y(a, b, z, full["w_out_a"], full["w_out_b"], "mixer_y")
    m, h2, xn3 = mm_norm_res(y[None], full["w_o"][None], h1, gain["mix_post_g"], gain["ffn2_pre_g"], 1.0, "mixer_out")
    w.update(_gathered(gathering["ffn2"], groups["ffn2"], h2, "ffn2"))
    s2, h3, xn4 = _ffn_fwd(xn3, h2, w, "ffn2_", gain["ffn2_post_g"], gain["ple_pre_g"], "ffn2")
    w.update(_gathered(gathering["ple"], groups["ple"], h3, "ple"))
    full["ple_w_gate"] = w["ple_w_gate"].reshape(-1, d)
    proj_full = w["ple_w_proj"].transpose(1, 0, 2).reshape(ple_w_proj.shape[1], -1)
    gate, e, q, dh4, loss_part = ple_fwd(xn4, p_bf, full["ple_w_gate"], proj_full, h3, gain["ple_post_g"], tgt, "ple_fwd")
    loss = lax.psum(loss_part[0, 0], ("x", "y", "c"))

    small_g, big_g = {}, {}
    grad, delta, new_m, new_v = {}, {}, {}, {}
    results = (grad, delta, new_m, new_v)
    finish = functools.partial(_reduce_finish, weights=weights, moments_m=moments_m, moments_v=moments_v, core=core, chip=chip,
                               results=results)
    ds, de, dh3, small_g["ple_post_g"], small_g["ple_pre_g"] = ple_bwd(
        dh4, q, gate, e, gain["ple_post_g"], full["ple_w_gate"], h3, gain["ple_pre_g"], "ple_bwd")
    big_g["ple_w_gate"] = dw_tn(xn4, ds, 1, "dw_ple_gate").reshape(N_CHIPS, -1, d)
    big_g["ple_w_proj"] = dw_tn(p_bf, de, N_CHIPS, "dw_ple_proj", dy_mode="cols")
    reducing_ple, tok_ple = _reduce_start(groups["ple"], big_g, core, "ple")

    dg4, du4, g2, small_g["ffn2_post_g"] = _ffn_bwd_w(dh3, s2, w, "ffn2_", gain["ffn2_post_g"], "ffn2", (tok_ple,))
    big_g.update(g2)
    dh2, small_g["ffn2_pre_g"] = _ffn_bwd_x(dh3, dg4, du4, s2, w, "ffn2_", gain["ffn2_pre_g"], "ffn2", ())
    reducing_ffn2, tok_ffn2 = _reduce_start(groups["ffn2"], big_g, core, "ffn2")
    done_ple = finish(reducing_ple, after=tok_ffn2, tag="ple")

    dm, dya, dyb, dz, da, db, small_g["mix_post_g"] = mixer_bwd_y(
        dh2, m, gain["mix_post_g"], full["w_o"], ya, yb, z, full["w_out_a"], full["w_out_b"], "mixer_bwd_y",
        deps=(tok_ffn2, done_ple))
    big_g["w_o"] = dw_tn(y, dm, 1, "dw_o").reshape(N_CHIPS, -1, d)
    big_g["w_out_a"] = dw_tn(a, dya, 1, "dw_out_a").reshape(N_CHIPS, -1, d)
    big_g["w_out_b"] = dw_tn(b, dyb, 1, "dw_out_b").reshape(N_CHIPS, -1, d)
    dz, d_sgu_w, d_sgu_b, small_g["sgu_norm_g"] = sgu_bwd(z, da, dz, gain["sgu_norm_g"], sgu_w3, sgu_b3, "sgu_bwd")
    dz, d_pool_w, small_g["pool_scale"] = pool_bwd(db, diff, dz, pool_full, gain["pool_scale"], "pool_bwd")
    big_g["pool_w"] = d_pool_w.astype(BF16).reshape(n_groups, N_CHIPS, rows_per, dgp).transpose(1, 0, 2, 3).reshape(
        N_CHIPS, n_groups * rows_per, dgp)
    dw_sections = dw_tn(xn2, dz, len(DZ_SLOT), "dw_in", dy_mode="kmajor")
    big_g["w_in"] = jnp.concatenate([dw_sections[slot] for slot in DZ_SLOT], axis=1).reshape(d, N_CHIPS, -1).transpose(1, 0, 2)
    dh1, small_g["mix_pre_g"] = dx_norm_bwd([(dz, w["w_in"], True)], h1, gain["mix_pre_g"], dh2, "mixer_bwd_x")
    reducing_mixer, tok_mixer = _reduce_start(groups["mixer"], big_g, core, "mixer")
    done_ffn2 = finish(reducing_ffn2, after=tok_mixer, tag="ffn2")

    dg4, du4, g1, small_g["ffn1_post_g"] = _ffn_bwd_w(dh1, s1, w, "ffn1_", gain["ffn1_post_g"], "ffn1", (tok_mixer, done_ffn2))
    big_g.update(g1)
    reducing_ffn1, tok_ffn1 = _reduce_start(groups["ffn1"], big_g, core, "ffn1")
    done_mixer = finish(reducing_mixer, after=tok_ffn1, tag="mixer")
    dh0, small_g["ffn1_pre_g"] = _ffn_bwd_x(dh1, dg4, du4, s1, w, "ffn1_", gain["ffn1_pre_g"], "ffn1", (tok_ffn1, done_mixer))
    finish(reducing_ffn1, after=dh0, tag="ffn1")
    small_g["sgu_w"] = d_sgu_w
    small_g["sgu_b"] = d_sgu_b[:, :, 0]

    packed = small_allreduce_adamw(_pack_small(small_g), _pack_small({n: weights[n] for n in SMALL}),
                                   _pack_small({n: moments_m[n] for n in SMALL}), _pack_small({n: moments_v[n] for n in SMALL}))
    like = {n: weights[n] for n in SMALL}
    for store, block in zip((grad, delta, new_m, new_v), packed):
        store.update(_unpack_small(block, like))

    return (loss, dh0[None], *[grad[n] for n in WEIGHTS], *[delta[n] for n in WEIGHTS],
            *[new_m[n] for n in WEIGHTS], *[new_v[n] for n in WEIGHTS])
```

```python
import functools

import jax
import jax.numpy as jnp
from jax import lax
from jax.experimental import pallas as pl
from jax.experimental.pallas import tpu as pltpu

F32 = jnp.float32
BF16 = jnp.bfloat16
EPS = 1e-6
CHUNK = 128
POOL_WINDOWS = (2, 4, 8, 16)
HALO = 16
N_CHIPS = 4
ADAM_LR, ADAM_B1, ADAM_B2, ADAM_EPS, ADAM_WD, ADAM_STEP = 0.001, 0.9, 0.999, 1e-08, 0.01, 10
VMEM_LIMIT_V7X = 58 * 1024 * 1024
MESH_IDS = pl.DeviceIdType.MESH
HBM_SPEC = pl.BlockSpec(memory_space=pltpu.HBM)
VMEM_SPEC = pl.BlockSpec(memory_space=pltpu.VMEM)
SEM_SPEC = pl.BlockSpec(memory_space=pltpu.SEMAPHORE)
ANY_SPEC = pl.BlockSpec(memory_space=pl.ANY)
DATAFLOW = pltpu.SideEffectType.DATAFLOW_SIDE_EFFECTING
OTHER_CHIPS = ((1, 0), (0, 1), (1, 1))
DZ_SLOT = (2, 3, 4, 0, 1)
DW_TOKENS = 4096
DW_IN_TILE = 256

NT = (((1,), (1,)), ((), ()))
TN = (((0,), (0,)), ((), ()))


def _params(*sem, **more):
    return pltpu.CompilerParams(dimension_semantics=sem or None, vmem_limit_bytes=VMEM_LIMIT_V7X, **more)


def _tile(t, want):
    return max(c for c in range(8, min(t, want) + 1, 8) if t % c == 0)


def _const(shape):
    return pl.BlockSpec(shape, lambda *_: (0,) * len(shape))


def _rows(tm, d, col=0):
    return pl.BlockSpec((tm, d), lambda i: (i, col))


def _kmajor(nk, tm, kb):
    return pl.BlockSpec((nk, tm, kb), lambda i: (0, i, 0))


def _dot(a, b):
    return jnp.dot(a, b, preferred_element_type=F32)


def _dot_nt(a, b):
    return lax.dot_general(a, b, NT, preferred_element_type=F32)


def _dot_tn(a, b):
    return lax.dot_general(a, b, TN, preferred_element_type=F32)


def _gelu(x):
    return 0.5 * x * (1.0 + jnp.tanh(0.7978845608028654 * (x + 0.044715 * x * x * x)))


def _gelu_and_grad(x):
    x2 = x * x
    th = jnp.tanh(0.7978845608028654 * (x + 0.044715 * x * x2))
    cdf = 0.5 * (1.0 + th)
    return x * cdf, cdf + 0.5 * x * (1.0 - th * th) * 0.7978845608028654 * (1.0 + 3.0 * 0.044715 * x2)


def _sigmoid(x):
    return 1.0 / (1.0 + jnp.exp(-x))


def _rstd(h):
    return lax.rsqrt(jnp.mean(h * h, axis=-1, keepdims=True) + EPS)


def _rms_bwd(h, g, dy):
    r = _rstd(h)
    t = dy * g
    dh = r * t - h * (r * r * r) * jnp.mean(h * t, axis=-1, keepdims=True)
    return dh, jnp.sum(dy * h * r, axis=0, keepdims=True)


def _ordered_after(body, n_in, deps):
    if not deps:
        return body
    return lambda *refs: body(*refs[:n_in], *refs[n_in + len(deps):])


def _accumulate(ref, value):
    @pl.when(pl.program_id(0) == 0)
    def _():
        ref[...] = jnp.zeros_like(ref)

    ref[...] += value


def rms_cast(h, g, name):
    t, d = h.shape
    tm = _tile(t, 512)

    def body(h_ref, g_ref, o_ref):
        hv = h_ref[...]
        o_ref[...] = (hv * _rstd(hv) * g_ref[...]).astype(BF16)

    return pl.pallas_call(
        body, name=name, grid=(t // tm,), in_specs=[_rows(tm, d), _const((1, d))], out_specs=_rows(tm, d),
        out_shape=jax.ShapeDtypeStruct((t, d), BF16), compiler_params=_params("parallel"))(h, g)


def ffn_gu(xn, wgt4, wut4, name):
    t, d = xn.shape
    nk, fk, _ = wgt4.shape
    tm = _tile(t, 512)

    def body(x_ref, wg_ref, wu_ref, a_ref, s_ref, t_ref):
        xv = x_ref[...]
        g = _dot_nt(xv, wg_ref[0])
        u = _dot_nt(xv, wu_ref[0])
        sg = _sigmoid(g)
        s = g * sg
        a_ref[0] = (s * u).astype(BF16)
        s_ref[0] = s.astype(BF16)
        t_ref[0] = (u * sg * (1.0 + g * (1.0 - sg))).astype(BF16)

    w_spec = pl.BlockSpec((1, fk, d), lambda k, i: (k, 0, 0))
    o_spec = pl.BlockSpec((1, tm, fk), lambda k, i: (k, i, 0))
    shape = jax.ShapeDtypeStruct((nk, t, fk), BF16)
    return pl.pallas_call(
        body, name=name, grid=(nk, t // tm), in_specs=[pl.BlockSpec((tm, d), lambda k, i: (i, 0)), w_spec, w_spec],
        out_specs=[o_spec] * 3, out_shape=[shape] * 3, compiler_params=_params("parallel", "parallel"))(xn, wgt4, wut4)


def mm_norm_res(a3, w3, h_old, g_post, g_next, scale, name):
    nk, t, kb = a3.shape
    d = w3.shape[2]
    tm = _tile(t, 256)

    def body(a_ref, w_ref, h_ref, gp_ref, gn_ref, f_ref, hn_ref, xn_ref):
        f = _dot(a_ref[0], w_ref[0])
        for k in range(1, nk):
            f += _dot(a_ref[k], w_ref[k])
        f_ref[...] = f
        hn = h_ref[...] + scale * (f * _rstd(f) * gp_ref[...])
        hn_ref[...] = hn
        xn_ref[...] = (hn * _rstd(hn) * gn_ref[...]).astype(BF16)

    return pl.pallas_call(
        body, name=name, grid=(t // tm,),
        in_specs=[_kmajor(nk, tm, kb), _const((nk, kb, d)), _rows(tm, d), _const((1, d)), _const((1, d))],
        out_specs=[_rows(tm, d)] * 3,
        out_shape=[jax.ShapeDtypeStruct((t, d), F32), jax.ShapeDtypeStruct((t, d), F32), jax.ShapeDtypeStruct((t, d), BF16)],
        compiler_params=_params("parallel"))(a3, w3, h_old, g_post, g_next)


def mixer_in(xn, win4, name):
    t, d = xn.shape
    nk, _, nb = win4.shape
    tm = _tile(t, 512)

    def body(x_ref, w_ref, z_ref):
        z_ref[...] = _dot(x_ref[...], w_ref[0]).astype(BF16)

    return pl.pallas_call(
        body, name=name, grid=(nk, t // tm),
        in_specs=[pl.BlockSpec((tm, d), lambda k, i: (i, 0)), pl.BlockSpec((1, d, nb), lambda k, i: (k, 0, 0))],
        out_specs=pl.BlockSpec((tm, nb), lambda k, i: (i, k)), out_shape=jax.ShapeDtypeStruct((t, nk * nb), BF16),
        compiler_params=_params("parallel", "parallel"))(xn, win4)


def _causal_mask():
    row = lax.broadcasted_iota(jnp.int32, (CHUNK, CHUNK), 0)
    col = lax.broadcasted_iota(jnp.int32, (CHUNK, CHUNK), 1)
    return row >= col


def _layernorm_parts(v):
    mu = jnp.mean(v, axis=-1, keepdims=True)
    vc = v - mu
    r = lax.rsqrt(jnp.mean(vc * vc, axis=-1, keepdims=True) + EPS)
    return vc * r, r


def sgu_fwd(z, norm_g, sgu_w, sgu_b3, name):
    t = z.shape[0]
    d = norm_g.shape[1]
    ng = sgu_w.shape[0]
    dg = d // ng
    tm = _tile(t, 256)

    def body(zu_ref, zv_ref, ng_ref, w_ref, b_ref, a_ref):
        vhat, _ = _layernorm_parts(_gelu(zv_ref[...].astype(F32)))
        vn = (vhat * ng_ref[...]).astype(BF16)
        u = _gelu(zu_ref[...].astype(F32))
        mask = _causal_mask()
        for g in range(ng):
            wg = jnp.where(mask, w_ref[g], 0.0).astype(BF16)
            for ci in range(tm // CHUNK):
                rs, cs = slice(ci * CHUNK, (ci + 1) * CHUNK), slice(g * dg, (g + 1) * dg)
                sv = _dot(wg, vn[rs, cs]) + b_ref[g]
                a_ref[rs, cs] = (u[rs, cs] * sv).astype(BF16)

    return pl.pallas_call(
        body, name=name, grid=(t // tm,),
        in_specs=[_rows(tm, d, 0), _rows(tm, d, 1), _const((1, d)), _const((ng, CHUNK, CHUNK)), _const((ng, CHUNK, 1))],
        out_specs=_rows(tm, d), out_shape=jax.ShapeDtypeStruct((t, d), BF16),
        compiler_params=_params("parallel"))(z, z, norm_g, sgu_w, sgu_b3)


def pool_fwd(z, pool_w, pool_scale, name):
    t = z.shape[0]
    d = pool_scale.shape[1]
    ng = pool_w.shape[0]
    dg = d // ng
    tm = _tile(t, 256)
    per = tm // HALO

    def body(c_ref, prev_ref, w_ref, s_ref, diff_ref, b_ref):
        i = pl.program_id(0)
        cur = c_ref[...].astype(F32)
        prev = jnp.where(i > 0, prev_ref[...].astype(F32), 0.0)
        ext = jnp.concatenate([prev, cur], axis=0)
        tok = i * tm + lax.broadcasted_iota(jnp.int32, (tm, 1), 0)
        for g, win in enumerate(POOL_WINDOWS):
            cs = slice(g * dg, (g + 1) * dg)
            s = ext[:, cs]
            sh = 1
            while sh < win:
                s = s + pltpu.roll(s, sh, 0)
                sh *= 2
            cnt = jnp.minimum(tok + 1, win).astype(F32)
            diff = (s[HALO:] / cnt - cur[:, cs]).astype(BF16)
            diff_ref[:, cs] = diff
            b_ref[:, cs] = (_dot(diff, w_ref[g]) * s_ref[:, cs]).astype(BF16)

    return pl.pallas_call(
        body, name=name, grid=(t // tm,),
        in_specs=[_rows(tm, d, 2), pl.BlockSpec((HALO, d), lambda i: (jnp.maximum(i * per - 1, 0), 2)),
                  _const((ng, dg, dg)), _const((1, d))],
        out_specs=[_rows(tm, d)] * 2, out_shape=[jax.ShapeDtypeStruct((t, d), BF16)] * 2,
        compiler_params=_params("parallel"))(z, z, pool_w, pool_scale)


def mixer_y(a, b, z, woa, wob, name):
    t, d = a.shape
    tm = _tile(t, 256)

    def body(a_ref, b_ref, ga_ref, gb_ref, wa_ref, wb_ref, ya_ref, yb_ref, y_ref):
        ya = _dot(a_ref[...], wa_ref[...])
        yb = _dot(b_ref[...], wb_ref[...])
        ya_ref[...] = ya.astype(BF16)
        yb_ref[...] = yb.astype(BF16)
        y_ref[...] = (_sigmoid(ga_ref[...].astype(F32)) * ya + _sigmoid(gb_ref[...].astype(F32)) * yb).astype(BF16)

    return pl.pallas_call(
        body, name=name, grid=(t // tm,),
        in_specs=[_rows(tm, d), _rows(tm, d), _rows(tm, d, 3), _rows(tm, d, 4), _const((d, d)), _const((d, d))],
        out_specs=[_rows(tm, d)] * 3, out_shape=[jax.ShapeDtypeStruct((t, d), BF16)] * 3,
        compiler_params=_params("parallel"))(a, b, z, z, woa, wob)


def ple_fwd(xn, p, wpg, wpp, h, g_post, target, name):
    t, d = xn.shape
    dp = p.shape[1]
    tm = _tile(t, 256)

    def body(x_ref, p_ref, wg_ref, wp_ref, h_ref, g_ref, tg_ref, gate_ref, e_ref, q_ref, dh_ref, loss_ref):
        gate = _sigmoid(_dot(x_ref[...], wg_ref[...]))
        e = _dot(p_ref[...], wp_ref[...])
        q = gate * e
        gate_ref[...] = gate.astype(BF16)
        e_ref[...] = e.astype(BF16)
        q_ref[...] = q
        err = h_ref[...] + q * _rstd(q) * g_ref[...] - tg_ref[...]
        dh_ref[...] = err * (1.0 / d)
        _accumulate(loss_ref, jnp.full(loss_ref.shape, (0.5 / d) * jnp.sum(err * err), F32))

    return pl.pallas_call(
        body, name=name, grid=(t // tm,),
        in_specs=[_rows(tm, d), _rows(tm, dp), _const((d, d)), _const((dp, d)), _rows(tm, d), _const((1, d)), _rows(tm, d)],
        out_specs=[_rows(tm, d)] * 4 + [_const((8, 128))],
        out_shape=[jax.ShapeDtypeStruct((t, d), BF16), jax.ShapeDtypeStruct((t, d), BF16), jax.ShapeDtypeStruct((t, d), F32),
                   jax.ShapeDtypeStruct((t, d), F32), jax.ShapeDtypeStruct((8, 128), F32)],
        compiler_params=_params("arbitrary"))(xn, p, wpg, wpp, h, g_post, target)


def ple_bwd(dh, q, gate, e, g_post, wpg, h, g_pre, name):
    t, d = dh.shape
    tm = _tile(t, 256)

    def body(dh_ref, q_ref, gate_ref, e_ref, gp_ref, w_ref, h_ref, gn_ref, ds_ref, de_ref, dhp_ref, dgp_ref, dgn_ref):
        dhv = dh_ref[...]
        dq, dgp = _rms_bwd(q_ref[...], gp_ref[...], dhv)
        gate = gate_ref[...].astype(F32)
        ds = (dq * e_ref[...].astype(F32) * gate * (1.0 - gate)).astype(BF16)
        ds_ref[...] = ds
        de_ref[...] = (dq * gate).astype(BF16)
        dx, dgn = _rms_bwd(h_ref[...], gn_ref[...], _dot_nt(ds, w_ref[...]))
        dhp_ref[...] = dhv + dx
        _accumulate(dgp_ref, dgp)
        _accumulate(dgn_ref, dgn)

    return pl.pallas_call(
        body, name=name, grid=(t // tm,),
        in_specs=[_rows(tm, d)] * 4 + [_const((1, d)), _const((d, d)), _rows(tm, d), _const((1, d))],
        out_specs=[_rows(tm, d)] * 3 + [_const((1, d))] * 2,
        out_shape=[jax.ShapeDtypeStruct((t, d), BF16), jax.ShapeDtypeStruct((t, d), BF16), jax.ShapeDtypeStruct((t, d), F32),
                   jax.ShapeDtypeStruct((1, d), F32), jax.ShapeDtypeStruct((1, d), F32)],
        compiler_params=_params("arbitrary"))(dh, q, gate, e, g_post, wpg, h, g_pre)


def ffn_bwd_a(dh, f, g_post, wd4, s4, t4, scale, name, deps=()):
    t, d = dh.shape
    nk, fk, _ = wd4.shape
    tm = _tile(t, 256)

    def body(dh_ref, f_ref, gp_ref, w_ref, s_ref, t_ref, df_ref, dg_ref, du_ref, dgp_ref):
        df, dgp = _rms_bwd(f_ref[...], gp_ref[...], dh_ref[...])
        df = (scale * df).astype(BF16)
        df_ref[...] = df
        _accumulate(dgp_ref, scale * dgp)
        for k in range(nk):
            da = _dot_nt(df, w_ref[k])
            du_ref[k] = (da * s_ref[k].astype(F32)).astype(BF16)
            dg_ref[k] = (da * t_ref[k].astype(F32)).astype(BF16)

    return pl.pallas_call(
        _ordered_after(body, 6, deps), name=name, grid=(t // tm,),
        in_specs=[_rows(tm, d), _rows(tm, d), _const((1, d)), _const((nk, fk, d)), _kmajor(nk, tm, fk), _kmajor(nk, tm, fk)]
        + [ANY_SPEC] * len(deps),
        out_specs=[_rows(tm, d), _kmajor(nk, tm, fk), _kmajor(nk, tm, fk), _const((1, d))],
        out_shape=[jax.ShapeDtypeStruct((t, d), BF16), jax.ShapeDtypeStruct((nk, t, fk), BF16),
                   jax.ShapeDtypeStruct((nk, t, fk), BF16), jax.ShapeDtypeStruct((1, d), F32)],
        compiler_params=_params("arbitrary"))(dh, f, g_post, wd4, s4, t4, *deps)


def dx_norm_bwd(pairs, h, g_pre, dh_in, name, deps=()):
    t, d = h.shape
    tm = _tile(t, 256)
    n = len(pairs)

    def body(*refs):
        dys, ws = refs[:n], refs[n:2 * n]
        h_ref, g_ref, dhi_ref, dho_ref, dg_ref = refs[2 * n:]
        acc = None
        for (_, w4, sections), dy_ref, w_ref in zip(pairs, dys, ws):
            if sections:
                wide = w4.shape[2]
                edges = sorted(set(range(0, 5 * d + 1, d)) | set(range(0, 5 * d + 1, wide)))
                parts = [_dot_nt(dy_ref[DZ_SLOT[lo // d], :, lo % d:lo % d + hi - lo], w_ref[lo // wide, :, lo % wide:lo % wide + hi - lo])
                         for lo, hi in zip(edges[:-1], edges[1:])]
            else:
                parts = [_dot(dy_ref[k], w_ref[k]) for k in range(w4.shape[0])]
            for part in parts:
                acc = part if acc is None else acc + part
        dx, dg = _rms_bwd(h_ref[...], g_ref[...], acc)
        dho_ref[...] = dhi_ref[...] + dx
        _accumulate(dg_ref, dg)

    dy_specs = [_kmajor(dy.shape[0], tm, dy.shape[2]) for dy, _, _ in pairs]
    return pl.pallas_call(
        _ordered_after(body, 2 * n + 3, deps), name=name, grid=(t // tm,),
        in_specs=dy_specs + [_const(w4.shape) for _, w4, _ in pairs] + [_rows(tm, d), _const((1, d)), _rows(tm, d)]
        + [ANY_SPEC] * len(deps),
        out_specs=[_rows(tm, d), _const((1, d))],
        out_shape=[jax.ShapeDtypeStruct((t, d), F32), jax.ShapeDtypeStruct((1, d), F32)],
        compiler_params=_params("arbitrary"))(*[dy for dy, _, _ in pairs], *[w4 for _, w4, _ in pairs], h, g_pre, dh_in, *deps)


def mixer_bwd_y(dh, m, g_post, w_o, ya, yb, z, woa, wob, name, deps=()):
    t, d = dh.shape
    tm = _tile(t, 256)

    def body(dh_ref, m_ref, gp_ref, wo_ref, ya_ref, yb_ref, ga_ref, gb_ref, wa_ref, wb_ref,
             dm_ref, dya_ref, dyb_ref, dz_ref, da_ref, db_ref, dgp_ref):
        dm, dgp = _rms_bwd(m_ref[...], gp_ref[...], dh_ref[...])
        dm = dm.astype(BF16)
        dm_ref[...] = dm
        _accumulate(dgp_ref, dgp)
        dy = _dot_nt(dm, wo_ref[...])
        sa = _sigmoid(ga_ref[...].astype(F32))
        sb = _sigmoid(gb_ref[...].astype(F32))
        dya = (dy * sa).astype(BF16)
        dyb = (dy * sb).astype(BF16)
        dya_ref[...] = dya
        dyb_ref[...] = dyb
        dz_ref[0] = (dy * ya_ref[...].astype(F32) * sa * (1.0 - sa)).astype(BF16)
        dz_ref[1] = (dy * yb_ref[...].astype(F32) * sb * (1.0 - sb)).astype(BF16)
        da_ref[...] = _dot_nt(dya, wa_ref[...]).astype(BF16)
        db_ref[...] = _dot_nt(dyb, wb_ref[...]).astype(BF16)

    return pl.pallas_call(
        _ordered_after(body, 10, deps), name=name, grid=(t // tm,),
        in_specs=[_rows(tm, d), _rows(tm, d), _const((1, d)), _const((d, d)), _rows(tm, d), _rows(tm, d),
                  _rows(tm, d, 3), _rows(tm, d, 4), _const((d, d)), _const((d, d))] + [ANY_SPEC] * len(deps),
        out_specs=[_rows(tm, d)] * 3 + [pl.BlockSpec((2, tm, d), lambda i: (0, i, 0))] + [_rows(tm, d)] * 2 + [_const((1, d))],
        out_shape=[jax.ShapeDtypeStruct((t, d), BF16)] * 3 + [jax.ShapeDtypeStruct((5, t, d), BF16)]
        + [jax.ShapeDtypeStruct((t, d), BF16)] * 2 + [jax.ShapeDtypeStruct((1, d), F32)],
        compiler_params=_params("arbitrary"))(dh, m, g_post, w_o, ya, yb, z, z, woa, wob, *deps)


def sgu_bwd(z, da, dz, norm_g, sgu_w, sgu_b3, name):
    t, d = da.shape
    ng = sgu_w.shape[0]
    dg = d // ng
    tm = _tile(t, 256)
    steps = t // tm

    def body(zu_ref, zv_ref, da_ref, ng_ref, w_ref, b_ref, _, dz_ref, dw_ref, db_ref, dng_ref, dvn_ref, dsv_ref):
        i = pl.program_id(0)
        zv = zv_ref[...].astype(F32)
        zu = zu_ref[...].astype(F32)
        v, gv = _gelu_and_grad(zv)
        vhat, r = _layernorm_parts(v)
        gain = ng_ref[...]
        vn = (vhat * gain).astype(BF16)
        u, gu = _gelu_and_grad(zu)
        dav = da_ref[...].astype(F32)
        mask = _causal_mask()

        @pl.when(i == 0)
        def _():
            dw_ref[...] = jnp.zeros_like(dw_ref)
            dsv_ref[...] = jnp.zeros_like(dsv_ref)

        for g in range(ng):
            wg = jnp.where(mask, w_ref[g], 0.0).astype(BF16)
            dw = jnp.zeros((CHUNK, CHUNK), F32)
            dsv_sum = jnp.zeros((CHUNK, dg), F32)
            for ci in range(tm // CHUNK):
                rs, cs = slice(ci * CHUNK, (ci + 1) * CHUNK), slice(g * dg, (g + 1) * dg)
                vn_blk = vn[rs, cs]
                sv = _dot(wg, vn_blk) + b_ref[g]
                dz_ref[0, rs, cs] = (dav[rs, cs] * sv * gu[rs, cs]).astype(BF16)
                dsv = dav[rs, cs] * u[rs, cs]
                dsv_sum += dsv
                dsv = dsv.astype(BF16)
                dw += _dot_nt(dsv, vn_blk)
                dvn_ref[rs, cs] = _dot_tn(wg, dsv)
            dw_ref[g] += dw
            dsv_ref[:, cs] += dsv_sum

        dvn = dvn_ref[...]
        _accumulate(dng_ref, jnp.sum(dvn * vhat, axis=0, keepdims=True))
        dvh = dvn * gain
        dv = r * (dvh - jnp.mean(dvh, axis=-1, keepdims=True) - vhat * jnp.mean(dvh * vhat, axis=-1, keepdims=True))
        dz_ref[1] = (dv * gv).astype(BF16)

        @pl.when(i == steps - 1)
        def _():
            for g in range(ng):
                dw_ref[g] = jnp.where(mask, dw_ref[g], 0.0)
                row_sum = jnp.sum(dsv_ref[:, g * dg:(g + 1) * dg], axis=1, keepdims=True)
                db_ref[g] = jnp.broadcast_to(row_sum, (CHUNK, CHUNK))

    return pl.pallas_call(
        body, name=name, grid=(steps,),
        in_specs=[_rows(tm, d, 0), _rows(tm, d, 1), _rows(tm, d), _const((1, d)), _const((ng, CHUNK, CHUNK)), _const((ng, CHUNK, 1)),
                  ANY_SPEC],
        out_specs=[pl.BlockSpec((2, tm, d), lambda i: (DZ_SLOT[0] // 2, i, 0)), _const((ng, CHUNK, CHUNK)),
                   _const((ng, CHUNK, CHUNK)), _const((1, d))],
        out_shape=[jax.ShapeDtypeStruct(dz.shape, BF16), jax.ShapeDtypeStruct((ng, CHUNK, CHUNK), F32),
                   jax.ShapeDtypeStruct((ng, CHUNK, CHUNK), F32), jax.ShapeDtypeStruct((1, d), F32)],
        scratch_shapes=[pltpu.VMEM((tm, d), F32), pltpu.VMEM((CHUNK, d), F32)], input_output_aliases={6: 0},
        compiler_params=_params("arbitrary"))(z, z, da, norm_g, sgu_w, sgu_b3, dz)


def pool_bwd(db, diff, dz, pool_w, pool_scale, name):
    t, d = db.shape
    ng = pool_w.shape[0]
    dg = d // ng
    tm = _tile(t, 256)
    per = tm // HALO
    steps = t // tm

    def body(db_ref, next_ref, diff_ref, w_ref, s_ref, _, dc_ref, dw_ref, ds_ref):
        i = pl.program_id(0)
        dbc = db_ref[...].astype(F32)
        nxt = jnp.where(i < steps - 1, next_ref[...].astype(F32), 0.0)
        ext = jnp.concatenate([dbc, nxt], axis=0)
        rows = tm + HALO
        tok = i * tm + lax.broadcasted_iota(jnp.int32, (rows, 1), 0)

        @pl.when(i == 0)
        def _():
            dw_ref[...] = jnp.zeros_like(dw_ref)
            ds_ref[...] = jnp.zeros_like(ds_ref)

        for g, win in enumerate(POOL_WINDOWS):
            cs = slice(g * dg, (g + 1) * dg)
            dp = (ext[:, cs] * s_ref[:, cs]).astype(BF16)
            dd = _dot_nt(dp, w_ref[g])
            s = dd / jnp.minimum(tok + 1, win).astype(F32)
            sh = 1
            while sh < win:
                s = s + pltpu.roll(s, rows - sh, 0)
                sh *= 2
            dc_ref[0, :, cs] = (s[:tm] - dd[:tm]).astype(BF16)
            dfg = diff_ref[:, cs]
            ds_ref[:, cs] += jnp.sum(dbc[:, cs] * _dot(dfg, w_ref[g]), axis=0, keepdims=True)
            dw_ref[g] += _dot_tn(dfg, dp[:tm])

    return pl.pallas_call(
        body, name=name, grid=(steps,),
        in_specs=[_rows(tm, d), pl.BlockSpec((HALO, d), lambda i: (jnp.minimum((i + 1) * per, t // HALO - 1), 0)),
                  _rows(tm, d), _const((ng, dg, dg)), _const((1, d)), ANY_SPEC],
        out_specs=[pl.BlockSpec((1, tm, d), lambda i: (DZ_SLOT[2], i, 0)), _const((ng, dg, dg)), _const((1, d))],
        out_shape=[jax.ShapeDtypeStruct(dz.shape, BF16), jax.ShapeDtypeStruct((ng, dg, dg), F32), jax.ShapeDtypeStruct((1, d), F32)],
        input_output_aliases={5: 0},
        compiler_params=_params("arbitrary"))(db, db, diff, pool_w, pool_scale, dz)


def dw_tn(x, dy, nk, name, x_kmajor=False, dy_mode="same"):
    t = x.shape[-2]
    kx = x.shape[-1]
    n = dy.shape[-1] // nk if dy_mode == "cols" else dy.shape[-1]
    tt = _tile(t, DW_TOKENS)
    steps = t // tt

    def body(x_ref, dy_ref, o_ref, acc_ref):
        s = pl.program_id(1)
        part = _dot_tn(x_ref[0] if x_kmajor else x_ref[...], dy_ref[0] if dy_mode == "kmajor" else dy_ref[...])
        if steps == 1:
            o_ref[0] = part.astype(BF16)
            return

        @pl.when(s == 0)
        def _():
            acc_ref[...] = jnp.zeros_like(acc_ref)

        acc_ref[...] += part

        @pl.when(s == steps - 1)
        def _():
            o_ref[0] = acc_ref[...].astype(BF16)

    x_spec = pl.BlockSpec((1, tt, kx), lambda k, s: (k, s, 0)) if x_kmajor else pl.BlockSpec((tt, kx), lambda k, s: (s, 0))
    dy_spec = {"kmajor": pl.BlockSpec((1, tt, n), lambda k, s: (k, s, 0)), "cols": pl.BlockSpec((tt, n), lambda k, s: (s, k)),
               "same": pl.BlockSpec((tt, n), lambda k, s: (s, 0))}[dy_mode]
    return pl.pallas_call(
        body, name=name, grid=(nk, steps), in_specs=[x_spec, dy_spec],
        out_specs=pl.BlockSpec((1, kx, n), lambda k, s: (k, 0, 0)), out_shape=jax.ShapeDtypeStruct((nk, kx, n), BF16),
        scratch_shapes=[pltpu.VMEM((kx, n) if steps > 1 else (8, 128), F32)],
        compiler_params=_params("parallel", "arbitrary"))(x, dy)


def dw_in_tiles(xn, dz, nk, name):
    t, d = xn.shape
    sections = len(DZ_SLOT)
    per_section = d // DW_IN_TILE
    per_shard = sections * per_section // nk

    def body(x_ref, dy_ref, o_ref):
        o_ref[0] = _dot_tn(x_ref[...], dy_ref[0]).astype(BF16)

    def slot(j):
        return (j // per_section + DZ_SLOT[0]) % sections

    return pl.pallas_call(
        body, name=name, grid=(sections * per_section,),
        in_specs=[pl.BlockSpec((t, d), lambda j: (0, 0)), pl.BlockSpec((1, t, DW_IN_TILE), lambda j: (slot(j), 0, j % per_section))],
        out_specs=pl.BlockSpec((1, d, DW_IN_TILE), lambda j: (j // per_shard, 0, j % per_shard)),
        out_shape=jax.ShapeDtypeStruct((nk, d, sections * d // nk), BF16), compiler_params=_params("parallel"))(xn, dz)


def _place():
    x, y, c = lax.axis_index("x"), lax.axis_index("y"), lax.axis_index("c")
    chips = [((1 - x) if fx else x, (1 - y) if fy else y) for fx, fy in OTHER_CHIPS]
    return x, y, c, chips


def _half(rows, which):
    return pl.ds(pl.multiple_of(which * (rows // 2), 16), rows // 2)


def _hbm(a):
    return pltpu.with_memory_space_constraint(a, pltpu.HBM)


def _by_shape(arrays):
    buckets = {}
    for i, a in enumerate(arrays):
        buckets.setdefault(a.shape, []).append(i)
    return list(buckets.values())


def cast_place(shards, chip, name, deps=()):
    n = len(shards)
    r, cdim = shards[0].shape
    tr = _tile(r, 256)

    def body(chip_ref, *refs):
        for w_ref, o_ref in zip(refs[:n], refs[n + len(deps):]):
            o_ref[0] = w_ref[...].astype(BF16)

    return pl.pallas_call(
        body, name=name,
        grid_spec=pltpu.PrefetchScalarGridSpec(
            num_scalar_prefetch=1, grid=(r // tr,),
            in_specs=[pl.BlockSpec((tr, cdim), lambda i, chip_ref: (i, 0))] * n + [ANY_SPEC] * len(deps),
            out_specs=[pl.BlockSpec((1, tr, cdim), lambda i, chip_ref: (chip_ref[0], i, 0))] * n),
        out_shape=[jax.ShapeDtypeStruct((N_CHIPS, r, cdim), BF16)] * n, compiler_params=_params("parallel"))(chip, *shards, *deps)


def _gather_copy(buf, sends, recvs, i, j, me, chip_xy, c):
    cx, cy = chip_xy
    mine = _half(buf.shape[1], c)
    return pltpu.make_async_remote_copy(
        src_ref=buf.at[me, mine], dst_ref=buf.at[me, mine], send_sem=sends.at[3 * i + j], recv_sem=recvs.at[3 * i + j],
        device_id=(cx, cy, c), device_id_type=MESH_IDS)


def allgather_start(bufs, name):
    n = len(bufs)

    def body(*refs):
        ins = refs[:n]
        sends, recvs = refs[n], refs[n + 1]
        token = refs[2 * n + 2]
        x, y, c, chips = _place()
        for i in range(n):
            for j, chip_xy in enumerate(chips):
                _gather_copy(ins[i], sends, recvs, i, j, 2 * x + y, chip_xy, c).start()
        token[...] = jnp.zeros_like(token)

    out = pl.pallas_call(
        body, name=name, in_specs=[HBM_SPEC] * n,
        out_specs=[SEM_SPEC, SEM_SPEC] + [HBM_SPEC] * n + [VMEM_SPEC],
        out_shape=[pltpu.SemaphoreType.DMA((3 * n,)), pltpu.SemaphoreType.DMA((3 * n,))]
        + [pltpu.HBM(b.shape, b.dtype) for b in bufs] + [jax.ShapeDtypeStruct((8, 128), F32)],
        input_output_aliases={i: i + 2 for i in range(n)},
        compiler_params=pltpu.CompilerParams(has_side_effects=DATAFLOW))(*[_hbm(b) for b in bufs])
    return out[0], out[1], list(out[2:2 + n]), out[2 + n]


def allgather_wait(sends, recvs, bufs, after, name):
    n = len(bufs)

    def body(*refs):
        ins = refs[:n]
        send_sems, recv_sems = refs[n], refs[n + 1]
        x, y, c, chips = _place()
        for i in range(n):
            for j, (cx, cy) in enumerate(chips):
                mine = _half(ins[i].shape[1], c)
                cp = pltpu.make_async_remote_copy(
                    src_ref=ins[i].at[2 * x + y, mine], dst_ref=ins[i].at[2 * cx + cy, mine], send_sem=send_sems.at[3 * i + j],
                    recv_sem=recv_sems.at[3 * i + j], device_id=(cx, cy, c), device_id_type=MESH_IDS)
                cp.wait_send()
                cp.wait_recv()

    return pl.pallas_call(
        body, name=name, in_specs=[HBM_SPEC] * n + [SEM_SPEC, SEM_SPEC, ANY_SPEC], out_specs=[HBM_SPEC] * n,
        out_shape=[pltpu.HBM(b.shape, b.dtype) for b in bufs], input_output_aliases={i: i for i in range(n)},
        compiler_params=pltpu.CompilerParams(has_side_effects=DATAFLOW))(*bufs, sends, recvs, after)


def d2d_forward(bufs, name):
    n = len(bufs)

    def body(*refs):
        ins = refs[:n]
        send_sems, recv_sems = refs[2 * n:]
        x, y, c, chips = _place()
        copies = []
        for i in range(n):
            mine = _half(ins[i].shape[1], c)
            for j, (cx, cy) in enumerate(chips):
                landed = ins[i].at[2 * cx + cy, mine]
                cp = pltpu.make_async_remote_copy(
                    src_ref=landed, dst_ref=landed, send_sem=send_sems.at[i, j], recv_sem=recv_sems.at[i, j],
                    device_id=(x, y, 1 - c), device_id_type=MESH_IDS)
                cp.start()
                copies.append(cp)
        for i in range(n):
            theirs = _half(ins[i].shape[1], 1 - c)
            for j, (cx, cy) in enumerate(chips):
                passed = ins[i].at[2 * cx + cy, theirs]
                pltpu.make_async_remote_copy(
                    src_ref=passed, dst_ref=passed, send_sem=send_sems.at[i, j], recv_sem=recv_sems.at[i, j],
                    device_id=(x, y, 1 - c), device_id_type=MESH_IDS).wait_recv()
        for cp in copies:
            cp.wait_send()

    return pl.pallas_call(
        body, name=name, in_specs=[HBM_SPEC] * n, out_specs=[HBM_SPEC] * n,
        out_shape=[jax.ShapeDtypeStruct(b.shape, b.dtype) for b in bufs], input_output_aliases={i: i for i in range(n)},
        scratch_shapes=[pltpu.SemaphoreType.DMA((n, 3))] * 2,
        compiler_params=pltpu.CompilerParams(has_side_effects=True))(*bufs)


def exchange_halves(grads, name):
    n = len(grads)

    def body(*refs):
        ins, outs = refs[:n], refs[n:2 * n]
        send_sems, recv_sems = refs[2 * n:]
        x, y, c, _ = _place()
        copies = []
        for i in range(n):
            cp = pltpu.make_async_remote_copy(
                src_ref=ins[i].at[:, _half(ins[i].shape[1], 1 - c)], dst_ref=outs[i], send_sem=send_sems.at[i],
                recv_sem=recv_sems.at[i], device_id=(x, y, 1 - c), device_id_type=MESH_IDS)
            cp.start()
            copies.append(cp)
        for cp in copies:
            cp.wait()

    return pl.pallas_call(
        body, name=name, in_specs=[HBM_SPEC] * n, out_specs=[HBM_SPEC] * n,
        out_shape=[jax.ShapeDtypeStruct((g.shape[0], g.shape[1] // 2, g.shape[2]), g.dtype) for g in grads],
        scratch_shapes=[pltpu.SemaphoreType.DMA((n,))] * 2,
        compiler_params=pltpu.CompilerParams(has_side_effects=True))(*grads)


def _scatter_copy(src, land, sends, recvs, i, j, chip_xy, c):
    cx, cy = chip_xy
    return pltpu.make_async_remote_copy(
        src_ref=src.at[2 * cx + cy], dst_ref=land.at[j], send_sem=sends.at[3 * i + j], recv_sem=recvs.at[3 * i + j],
        device_id=(cx, cy, c), device_id_type=MESH_IDS)


def scatter_start(sums, name):
    n = len(sums)
    lands = [lax.empty((3,) + s.shape[1:], s.dtype) for s in sums]

    def body(*refs):
        srcs, zones = refs[:n], refs[n:2 * n]
        sends, recvs = refs[2 * n], refs[2 * n + 1]
        token = refs[4 * n + 2]
        _, _, c, chips = _place()
        for i in range(n):
            for j, chip_xy in enumerate(chips):
                _scatter_copy(srcs[i], zones[i], sends, recvs, i, j, chip_xy, c).start()
        token[...] = jnp.zeros_like(token)

    out = pl.pallas_call(
        body, name=name, in_specs=[HBM_SPEC] * (2 * n),
        out_specs=[SEM_SPEC, SEM_SPEC] + [HBM_SPEC] * (2 * n) + [VMEM_SPEC],
        out_shape=[pltpu.SemaphoreType.DMA((3 * n,)), pltpu.SemaphoreType.DMA((3 * n,))]
        + [pltpu.HBM(a.shape, a.dtype) for a in sums + lands] + [jax.ShapeDtypeStruct((8, 128), F32)],
        input_output_aliases={i: i + 2 for i in range(2 * n)},
        compiler_params=pltpu.CompilerParams(has_side_effects=DATAFLOW))(*[_hbm(a) for a in sums + lands])
    return out[0], out[1], list(out[2:2 + n]), list(out[2 + n:2 + 2 * n]), out[2 + 2 * n]


def scatter_wait(sends, recvs, sums, lands, after, name):
    n = len(sums)

    def body(*refs):
        srcs, zones = refs[:n], refs[n:2 * n]
        send_sems, recv_sems = refs[2 * n], refs[2 * n + 1]
        _, _, c, chips = _place()
        for i in range(n):
            for j, chip_xy in enumerate(chips):
                cp = _scatter_copy(srcs[i], zones[i], send_sems, recv_sems, i, j, chip_xy, c)
                cp.wait_send()
                cp.wait_recv()

    out = pl.pallas_call(
        body, name=name, in_specs=[HBM_SPEC] * (2 * n) + [SEM_SPEC, SEM_SPEC, ANY_SPEC], out_specs=[HBM_SPEC] * (2 * n),
        out_shape=[pltpu.HBM(a.shape, a.dtype) for a in sums + lands], input_output_aliases={i: i for i in range(2 * n)},
        compiler_params=pltpu.CompilerParams(has_side_effects=DATAFLOW))(*sums, *lands, sends, recvs, after)
    return list(out[:n]), list(out[n:])


def send_to_sibling(halves, name):
    n = len(halves)

    def body(*refs):
        ins, outs = refs[:n], refs[n:2 * n]
        send_sems, recv_sems = refs[2 * n:]
        x, y, c, _ = _place()
        copies = []
        for i in range(n):
            cp = pltpu.make_async_remote_copy(
                src_ref=ins[i], dst_ref=outs[i], send_sem=send_sems.at[i], recv_sem=recv_sems.at[i],
                device_id=(x, y, 1 - c), device_id_type=MESH_IDS)
            cp.start()
            copies.append(cp)
        for cp in copies:
            cp.wait()

    return pl.pallas_call(
        body, name=name, in_specs=[HBM_SPEC] * n, out_specs=[HBM_SPEC] * n,
        out_shape=[jax.ShapeDtypeStruct(h.shape, h.dtype) for h in halves],
        scratch_shapes=[pltpu.SemaphoreType.DMA((n,))] * 2,
        compiler_params=pltpu.CompilerParams(has_side_effects=True))(*halves)


def add_halves(grads, recvs, core, name):
    n = len(grads)
    nk, r, cdim = grads[0].shape
    half = r // 2
    views = [g.reshape(nk, 2, half, cdim) for g in grads]

    def body(core_ref, *refs):
        for g_ref, r_ref, o_ref in zip(refs[:n], refs[n:2 * n], refs[2 * n:]):
            o_ref[0] = (g_ref[0, 0].astype(F32) + r_ref[0].astype(F32)).astype(BF16)

    return pl.pallas_call(
        body, name=name,
        grid_spec=pltpu.PrefetchScalarGridSpec(
            num_scalar_prefetch=1, grid=(nk,),
            in_specs=[pl.BlockSpec((1, 1, half, cdim), lambda k, core_ref: (k, core_ref[0], 0, 0))] * n
            + [pl.BlockSpec((1, half, cdim), lambda k, core_ref: (k, 0, 0))] * n,
            out_specs=[pl.BlockSpec((1, half, cdim), lambda k, core_ref: (k, 0, 0))] * n),
        out_shape=[jax.ShapeDtypeStruct((nk, half, cdim), BF16)] * n, compiler_params=_params("parallel"))(core, *views, *recvs)


def add_chips(sums, lands, chip, name):
    n = len(sums)
    _, half, cdim = sums[0].shape
    tr = _tile(half, 128)

    def body(chip_ref, *refs):
        for s_ref, r_ref, o_ref in zip(refs[:n], refs[n:2 * n], refs[2 * n:]):
            o_ref[...] = ((s_ref[0].astype(F32) + r_ref[0].astype(F32)) + r_ref[1].astype(F32)) + r_ref[2].astype(F32)

    return pl.pallas_call(
        body, name=name,
        grid_spec=pltpu.PrefetchScalarGridSpec(
            num_scalar_prefetch=1, grid=(half // tr,),
            in_specs=[pl.BlockSpec((1, tr, cdim), lambda i, chip_ref: (chip_ref[0], i, 0))] * n
            + [pl.BlockSpec((3, tr, cdim), lambda i, chip_ref: (0, i, 0))] * n,
            out_specs=[pl.BlockSpec((tr, cdim), lambda i, chip_ref: (i, 0))] * n),
        out_shape=[jax.ShapeDtypeStruct((half, cdim), F32)] * n, compiler_params=_params("parallel"))(chip, *sums, *lands)


def _adamw_math(w, g, m, v):
    m = ADAM_B1 * m + (1.0 - ADAM_B1) * g
    v = ADAM_B2 * v + (1.0 - ADAM_B2) * (g * g)
    m_hat = m / (1.0 - ADAM_B1 ** ADAM_STEP)
    v_hat = v / (1.0 - ADAM_B2 ** ADAM_STEP)
    return -ADAM_LR * (m_hat / (jnp.sqrt(v_hat) + ADAM_EPS) + ADAM_WD * w), m, v


def adamw_halves(ws, owns, others, ms, vs, core, name):
    n = len(ws)
    r, cdim = ws[0].shape
    half = r // 2
    tr = _tile(half, max(8, 256 // n))
    steps = half // tr

    def body(core_ref, *refs):
        ins, outs = refs[:5 * n], refs[5 * n:]
        for a in range(n):
            w_ref, own_ref, other_ref, m_ref, v_ref = ins[a::n]
            g = jnp.where(pl.program_id(0) == core_ref[0], own_ref[...], other_ref[...])
            outs[a][...] = g
            outs[n + a][...], outs[2 * n + a][...], outs[3 * n + a][...] = _adamw_math(w_ref[...], g, m_ref[...], v_ref[...])

    whole = pl.BlockSpec((tr, cdim), lambda hh, i, core_ref: (hh * steps + i, 0))
    part = pl.BlockSpec((tr, cdim), lambda hh, i, core_ref: (i, 0))
    out = pl.pallas_call(
        body, name=name,
        grid_spec=pltpu.PrefetchScalarGridSpec(
            num_scalar_prefetch=1, grid=(2, steps), in_specs=[whole] * n + [part] * (2 * n) + [whole] * (2 * n),
            out_specs=[whole] * (4 * n)),
        out_shape=[jax.ShapeDtypeStruct((r, cdim), F32)] * (4 * n),
        compiler_params=_params("parallel", "parallel"))(core, *ws, *owns, *others, *ms, *vs)
    return [out[a::n] for a in range(n)]


def small_allreduce_adamw(g, w, m, v):
    r, cdim = g.shape

    def body(g_ref, w_ref, m_ref, v_ref, go_ref, d_ref, mo_ref, vo_ref, pair, quad, d2d_send, d2d_recv, ici_send, ici_recv):
        x, y, c, chips = _place()
        me = 2 * x + y
        pair[c] = g_ref[...]
        swap = pltpu.make_async_remote_copy(
            src_ref=g_ref, dst_ref=pair.at[c], send_sem=d2d_send, recv_sem=d2d_recv, device_id=(x, y, 1 - c),
            device_id_type=MESH_IDS)
        swap.start()
        swap.wait()
        quad[me] = pair[0] + pair[1]
        copies = []
        for j, (cx, cy) in enumerate(chips):
            cp = pltpu.make_async_remote_copy(
                src_ref=quad.at[me], dst_ref=quad.at[me], send_sem=ici_send.at[j], recv_sem=ici_recv.at[j],
                device_id=(cx, cy, c), device_id_type=MESH_IDS)
            cp.start()
            copies.append(cp)
        for j, (cx, cy) in enumerate(chips):
            slot = quad.at[2 * cx + cy]
            pltpu.make_async_remote_copy(
                src_ref=slot, dst_ref=slot, send_sem=ici_send.at[j], recv_sem=ici_recv.at[j], device_id=(cx, cy, c),
                device_id_type=MESH_IDS).wait_recv()
        for cp in copies:
            cp.wait_send()
        total = (quad[0] + quad[1]) + (quad[2] + quad[3])
        go_ref[...] = total
        d_ref[...], mo_ref[...], vo_ref[...] = _adamw_math(w_ref[...], total, m_ref[...], v_ref[...])

    return pl.pallas_call(
        body, name="small_allreduce_adamw", in_specs=[VMEM_SPEC] * 4, out_specs=[VMEM_SPEC] * 4,
        out_shape=[jax.ShapeDtypeStruct((r, cdim), F32)] * 4,
        scratch_shapes=[pltpu.VMEM((2, r, cdim), F32), pltpu.VMEM((N_CHIPS, r, cdim), F32), pltpu.SemaphoreType.DMA,
                        pltpu.SemaphoreType.DMA, pltpu.SemaphoreType.DMA((3,)), pltpu.SemaphoreType.DMA((3,))],
        compiler_params=pltpu.CompilerParams(has_side_effects=True, vmem_limit_bytes=VMEM_LIMIT_V7X))(g, w, m, v)


GROUPS = (
    ("ffn1", ("ffn1_w_gate", "ffn1_w_up", "ffn1_w_down")),
    ("mixer", ("w_in", "pool_w", "w_out_a", "w_out_b", "w_o")),
    ("ffn2", ("ffn2_w_gate", "ffn2_w_up", "ffn2_w_down")),
    ("ple", ("ple_w_gate", "ple_w_proj")),
)
GATHERS = (
    ("ffn1_in", ("ffn1_w_gate", "ffn1_w_up")),
    ("ffn1_out", ("ffn1_w_down",)),
) + GROUPS[1:]
GAINS = ("ffn1_pre_g", "ffn1_post_g", "mix_pre_g", "sgu_norm_g", "pool_scale", "mix_post_g",
         "ffn2_pre_g", "ffn2_post_g", "ple_pre_g", "ple_post_g")
SMALL = GAINS + ("sgu_b", "sgu_w")
PACKED = SMALL + ("loss",)
WEIGHTS = ("ffn1_pre_g", "ffn1_w_gate", "ffn1_w_up", "ffn1_w_down", "ffn1_post_g", "mix_pre_g", "w_in", "sgu_norm_g",
           "sgu_w", "sgu_b", "pool_w", "pool_scale", "w_out_a", "w_out_b", "w_o", "mix_post_g", "ffn2_pre_g",
           "ffn2_w_gate", "ffn2_w_up", "ffn2_w_down", "ffn2_post_g", "ple_pre_g", "ple_w_gate", "ple_w_proj", "ple_post_g")
SMALL_ROWS = 8


TRANSPOSED = ("ffn1_w_gate", "ffn1_w_up", "ffn2_w_gate", "ffn2_w_up")


def _shard2d(name, a):
    a = a[0]
    return a.T if name in TRANSPOSED else a.reshape(-1, a.shape[-1])


def _unshard2d(name, a2d, shape):
    return (a2d.T if name in TRANSPOSED else a2d).reshape(shape)


def _pack_small(parts):
    rows = []
    for name in PACKED:
        a = parts[name].astype(F32).reshape(-1)
        width = 1024
        n_rows = -(-a.shape[0] // width)
        padded = -(-n_rows // SMALL_ROWS) * SMALL_ROWS
        rows.append(jnp.pad(a, (0, padded * width - a.shape[0])).reshape(padded, width))
    return jnp.concatenate(rows, axis=0)


def _unpack_small(packed, like):
    out, row = {}, 0
    for name in PACKED:
        size = like[name].size
        n_rows = -(-size // 1024)
        padded = -(-n_rows // SMALL_ROWS) * SMALL_ROWS
        out[name] = packed[row:row + padded].reshape(-1)[:size].reshape(like[name].shape)
        row += padded
    return out


def _ffn_fwd(xn, h, w, pre, g_post, g_next, tag, late_weights=None):
    a4, s4, t4 = ffn_gu(xn, w[pre + "w_gate"], w[pre + "w_up"], tag + "_gu")
    if late_weights is not None:
        w.update(late_weights(a4))
    f, h_new, xn_next = mm_norm_res(a4, w[pre + "w_down"], h, g_post, g_next, 0.5, tag + "_down")
    return dict(xn=xn, h=h, a4=a4, s4=s4, t4=t4, f=f), h_new, xn_next


def _ffn_bwd_w(dh, saved, w, pre, g_post, tag, deps):
    df, dg4, du4, d_post = ffn_bwd_a(dh, saved["f"], g_post, w[pre + "w_down"], saved["s4"], saved["t4"], 0.5, tag + "_bwd_a",
                                     deps=deps)
    nk = N_CHIPS
    grads = {
        pre + "w_down": dw_tn(saved["a4"], df, nk, tag + "_dw_down", x_kmajor=True, dy_mode="same"),
        pre + "w_gate": dw_tn(dg4, saved["xn"], nk, tag + "_dw_gate", x_kmajor=True, dy_mode="same"),
        pre + "w_up": dw_tn(du4, saved["xn"], nk, tag + "_dw_up", x_kmajor=True, dy_mode="same"),
    }
    return dg4, du4, grads, d_post


def _ffn_bwd_x(dh, dg4, du4, saved, w, pre, g_pre, tag, deps):
    return dx_norm_bwd([(dg4, w[pre + "w_gate"], False), (du4, w[pre + "w_up"], False)], saved["h"], g_pre, dh,
                       tag + "_bwd_x", deps=deps)


def _gather_group(names, weights, chip, tag, deps):
    shards = [_shard2d(n, weights[n]) for n in names]
    bufs = [None] * len(names)
    for b, idx in enumerate(_by_shape(shards)):
        for i, buf in zip(idx, cast_place([shards[i] for i in idx], chip, f"cast_{tag}_{b}", deps=deps)):
            bufs[i] = buf
    sends, recvs, bufs, token = allgather_start(bufs, "allgather_start_" + tag)
    return (sends, recvs, bufs), token


def _gathered(started, names, after, tag):
    sends, recvs, bufs = started
    landed = allgather_wait(sends, recvs, bufs, after, "allgather_wait_" + tag)
    return dict(zip(names, d2d_forward(landed, "d2d_forward_" + tag)))


def _reduce_start(names, big_g, core, tag):
    partial = [big_g[n] for n in names]
    from_sibling = exchange_halves(partial, "exchange_halves_" + tag)
    chip_sums = [None] * len(names)
    for b, idx in enumerate(_by_shape(partial)):
        for i, s in zip(idx, add_halves([partial[i] for i in idx], [from_sibling[i] for i in idx], core, f"add_halves_{tag}_{b}")):
            chip_sums[i] = s
    sends, recvs, sums, lands, token = scatter_start(chip_sums, "scatter_start_" + tag)
    return (names, sends, recvs, sums, lands), token


def _reduce_finish(started, after, weights, moments_m, moments_v, core, chip, tag, results):
    names, sends, recvs, sums, lands = started
    sums, lands = scatter_wait(sends, recvs, sums, lands, after, "scatter_wait_" + tag)
    buckets = _by_shape(sums)
    reduced = [None] * len(names)
    for b, idx in enumerate(buckets):
        for i, r in zip(idx, add_chips([sums[i] for i in idx], [lands[i] for i in idx], chip, f"add_chips_{tag}_{b}")):
            reduced[i] = r
    others = send_to_sibling(reduced, "send_to_sibling_" + tag)
    last = None
    for b, idx in enumerate(buckets):
        in_bucket = [names[i] for i in idx]
        outs = adamw_halves([_shard2d(n, weights[n]) for n in in_bucket], [reduced[i] for i in idx], [others[i] for i in idx],
                            [_shard2d(n, moments_m[n]) for n in in_bucket], [_shard2d(n, moments_v[n]) for n in in_bucket],
                            core, f"adamw_{tag}_{b}")
        for n, per_weight in zip(in_bucket, outs):
            for store, value in zip(results, per_weight):
                store[n] = _unshard2d(n, value, weights[n].shape)
            last = per_weight[1]
    return last


def kernel(x, p, ffn1_pre_g, ffn1_w_gate, ffn1_w_up, ffn1_w_down, ffn1_post_g, mix_pre_g, w_in, sgu_norm_g, sgu_w, sgu_b, pool_w, pool_scale, w_out_a, w_out_b, w_o, mix_post_g, ffn2_pre_g, ffn2_w_gate, ffn2_w_up, ffn2_w_down, ffn2_post_g, ple_pre_g, ple_w_gate, ple_w_proj, ple_post_g, loss_target, m_ffn1_pre_g, m_ffn1_w_gate, m_ffn1_w_up, m_ffn1_w_down, m_ffn1_post_g, m_mix_pre_g, m_w_in, m_sgu_norm_g, m_sgu_w, m_sgu_b, m_pool_w, m_pool_scale, m_w_out_a, m_w_out_b, m_w_o, m_mix_post_g, m_ffn2_pre_g, m_ffn2_w_gate, m_ffn2_w_up, m_ffn2_w_down, m_ffn2_post_g, m_ple_pre_g, m_ple_w_gate, m_ple_w_proj, m_ple_post_g, v_ffn1_pre_g, v_ffn1_w_gate, v_ffn1_w_up, v_ffn1_w_down, v_ffn1_post_g, v_mix_pre_g, v_w_in, v_sgu_norm_g, v_sgu_w, v_sgu_b, v_pool_w, v_pool_scale, v_w_out_a, v_w_out_b, v_w_o, v_mix_post_g, v_ffn2_pre_g, v_ffn2_w_gate, v_ffn2_w_up, v_ffn2_w_down, v_ffn2_post_g, v_ple_pre_g, v_ple_w_gate, v_ple_w_proj, v_ple_post_g):
    given = dict(locals())
    weights = {n: given[n] for n in WEIGHTS}
    moments_m = {n: given["m_" + n] for n in WEIGHTS}
    moments_v = {n: given["v_" + n] for n in WEIGHTS}
    core = lax.axis_index("c").astype(jnp.int32).reshape(1)
    chip = (2 * lax.axis_index("x") + lax.axis_index("y")).astype(jnp.int32).reshape(1)

    d = x.shape[-1]
    token = ()
    gathering = {}
    for tag, names in GATHERS:
        gathering[tag], tok = _gather_group(names, weights, chip, tag, token)
        token = (tok,)
    gain = {n: weights[n] for n in GAINS}
    sgu_w3 = sgu_w[0]
    sgu_b3 = sgu_b[0][:, :, None]
    groups = dict(GROUPS + GATHERS)

    h0 = x[0]
    tgt = loss_target[0]
    p_bf = p[0, 0].astype(BF16)
    w = _gathered(gathering["ffn1_in"], groups["ffn1_in"], token[0], "ffn1_in")
    xn1 = rms_cast(h0, gain["ffn1_pre_g"], "ffn1_pre_norm")
    s1, h1, xn2 = _ffn_fwd(xn1, h0, w, "ffn1_", gain["ffn1_post_g"], gain["mix_pre_g"], "ffn1",
                           lambda a4: _gathered(gathering["ffn1_out"], groups["ffn1_out"], a4, "ffn1_out"))
    w.update(_gathered(gathering["mixer"], groups["mixer"], h1, "mixer"))
    full = {n: w[n].reshape(-1, d) for n in ("w_out_a", "w_out_b", "w_o")}
    n_groups = pool_w.shape[1]
    rows_per = pool_w.shape[2]
    dgp = pool_w.shape[3]
    pool_full = w["pool_w"].reshape(N_CHIPS, n_groups, rows_per, dgp).transpose(1, 0, 2, 3).reshape(n_groups, N_CHIPS * rows_per, dgp)
    z = mixer_in(xn2, w["w_in"], "mixer_in")
    a = sgu_fwd(z, gain["sgu_norm_g"], sgu_w3, sgu_b3, "sgu_fwd")
    diff, b = pool_fwd(z, pool_full, gain["pool_scale"], "pool_fwd")
    ya, yb, y = mixer_y(a, b, z, full["w_out_a"], full["w_out_b"], "mixer_y")
    m, h2, xn3 = mm_norm_res(y[None], full["w_o"][None], h1, gain["mix_post_g"], gain["ffn2_pre_g"], 1.0, "mixer_out")
    w.update(_gathered(gathering["ffn2"], groups["ffn2"], h2, "ffn2"))
    s2, h3, xn4 = _ffn_fwd(xn3, h2, w, "ffn2_", gain["ffn2_post_g"], gain["ple_pre_g"], "ffn2")
    w.update(_gathered(gathering["ple"], groups["ple"], h3, "ple"))
    full["ple_w_gate"] = w["ple_w_gate"].reshape(-1, d)
    proj_full = w["ple_w_proj"].transpose(1, 0, 2).reshape(ple_w_proj.shape[1], -1)
    gate, e, q, dh4, loss_part = ple_fwd(xn4, p_bf, full["ple_w_gate"], proj_full, h3, gain["ple_post_g"], tgt, "ple_fwd")

    small_g, big_g = {"loss": loss_part[0, :1]}, {}
    grad, delta, new_m, new_v = {}, {}, {}, {}
    results = (grad, delta, new_m, new_v)
    finish = functools.partial(_reduce_finish, weights=weights, moments_m=moments_m, moments_v=moments_v, core=core, chip=chip,
                               results=results)
    ds, de, dh3, small_g["ple_post_g"], small_g["ple_pre_g"] = ple_bwd(
        dh4, q, gate, e, gain["ple_post_g"], full["ple_w_gate"], h3, gain["ple_pre_g"], "ple_bwd")
    big_g["ple_w_gate"] = dw_tn(xn4, ds, 1, "dw_ple_gate").reshape(N_CHIPS, -1, d)
    big_g["ple_w_proj"] = dw_tn(p_bf, de, N_CHIPS, "dw_ple_proj", dy_mode="cols")
    reducing_ple, tok_ple = _reduce_start(groups["ple"], big_g, core, "ple")

    dg4, du4, g2, small_g["ffn2_post_g"] = _ffn_bwd_w(dh3, s2, w, "ffn2_", gain["ffn2_post_g"], "ffn2", (tok_ple,))
    big_g.update(g2)
    dh2, small_g["ffn2_pre_g"] = _ffn_bwd_x(dh3, dg4, du4, s2, w, "ffn2_", gain["ffn2_pre_g"], "ffn2", ())
    reducing_ffn2, tok_ffn2 = _reduce_start(groups["ffn2"], big_g, core, "ffn2")
    done_ple = finish(reducing_ple, after=tok_ffn2, tag="ple")

    dm, dya, dyb, dz, da, db, small_g["mix_post_g"] = mixer_bwd_y(
        dh2, m, gain["mix_post_g"], full["w_o"], ya, yb, z, full["w_out_a"], full["w_out_b"], "mixer_bwd_y",
        deps=(tok_ffn2, done_ple))
    big_g["w_o"] = dw_tn(y, dm, 1, "dw_o").reshape(N_CHIPS, -1, d)
    big_g["w_out_a"] = dw_tn(a, dya, 1, "dw_out_a").reshape(N_CHIPS, -1, d)
    big_g["w_out_b"] = dw_tn(b, dyb, 1, "dw_out_b").reshape(N_CHIPS, -1, d)
    dz, d_sgu_w, d_sgu_b, small_g["sgu_norm_g"] = sgu_bwd(z, da, dz, gain["sgu_norm_g"], sgu_w3, sgu_b3, "sgu_bwd")
    dz, d_pool_w, small_g["pool_scale"] = pool_bwd(db, diff, dz, pool_full, gain["pool_scale"], "pool_bwd")
    big_g["pool_w"] = d_pool_w.astype(BF16).reshape(n_groups, N_CHIPS, rows_per, dgp).transpose(1, 0, 2, 3).reshape(
        N_CHIPS, n_groups * rows_per, dgp)
    big_g["w_in"] = dw_in_tiles(xn2, dz, N_CHIPS, "dw_in")
    dh1, small_g["mix_pre_g"] = dx_norm_bwd([(dz, w["w_in"], True)], h1, gain["mix_pre_g"], dh2, "mixer_bwd_x")
    reducing_mixer, tok_mixer = _reduce_start(groups["mixer"], big_g, core, "mixer")
    done_ffn2 = finish(reducing_ffn2, after=tok_mixer, tag="ffn2")

    dg4, du4, g1, small_g["ffn1_post_g"] = _ffn_bwd_w(dh1, s1, w, "ffn1_", gain["ffn1_post_g"], "ffn1", (tok_mixer, done_ffn2))
    big_g.update(g1)
    reducing_ffn1, tok_ffn1 = _reduce_start(groups["ffn1"], big_g, core, "ffn1")
    done_mixer = finish(reducing_mixer, after=tok_ffn1, tag="mixer")
    dh0, small_g["ffn1_pre_g"] = _ffn_bwd_x(dh1, dg4, du4, s1, w, "ffn1_", gain["ffn1_pre_g"], "ffn1", (tok_ffn1, done_mixer))
    finish(reducing_ffn1, after=dh0, tag="ffn1")
    small_g["sgu_w"] = d_sgu_w
    small_g["sgu_b"] = d_sgu_b[:, :, 0]

    no_state = {"loss": jnp.zeros((1,), F32)}
    packed = small_allreduce_adamw(
        _pack_small(small_g), _pack_small({n: weights[n] for n in SMALL} | no_state),
        _pack_small({n: moments_m[n] for n in SMALL} | no_state), _pack_small({n: moments_v[n] for n in SMALL} | no_state))
    like = {n: weights[n] for n in SMALL} | no_state
    for store, block in zip((grad, delta, new_m, new_v), packed):
        store.update(_unpack_small(block, like))

    return (grad["loss"][0], dh0[None], *[grad[n] for n in WEIGHTS], *[delta[n] for n in WEIGHTS],
            *[new_m[n] for n in WEIGHTS], *[new_v[n] for n in WEIGHTS])
```

```python
import functools

import jax
import jax.numpy as jnp
from jax import lax
from jax.experimental import pallas as pl
from jax.experimental.pallas import tpu as pltpu

F32 = jnp.float32
BF16 = jnp.bfloat16
EPS = 1e-6
CHUNK = 128
POOL_WINDOWS = (2, 4, 8, 16)
HALO = 16
N_CHIPS = 4
ADAM_LR, ADAM_B1, ADAM_B2, ADAM_EPS, ADAM_WD, ADAM_STEP = 0.001, 0.9, 0.999, 1e-08, 0.01, 10
VMEM_LIMIT_V7X = 58 * 1024 * 1024
MESH_IDS = pl.DeviceIdType.MESH
HBM_SPEC = pl.BlockSpec(memory_space=pltpu.HBM)
VMEM_SPEC = pl.BlockSpec(memory_space=pltpu.VMEM)
SEM_SPEC = pl.BlockSpec(memory_space=pltpu.SEMAPHORE)
ANY_SPEC = pl.BlockSpec(memory_space=pl.ANY)
DATAFLOW = pltpu.SideEffectType.DATAFLOW_SIDE_EFFECTING
OTHER_CHIPS = ((1, 0), (0, 1), (1, 1))
DZ_SLOT = (2, 3, 4, 0, 1)
DW_TOKENS = 4096
DW_IN_TILE = 256

NT = (((1,), (1,)), ((), ()))
TN = (((0,), (0,)), ((), ()))


def _params(*sem, **more):
    return pltpu.CompilerParams(dimension_semantics=sem or None, vmem_limit_bytes=VMEM_LIMIT_V7X, **more)


def _tile(t, want):
    return max(c for c in range(8, min(t, want) + 1, 8) if t % c == 0)


def _const(shape):
    return pl.BlockSpec(shape, lambda *_: (0,) * len(shape))


def _rows(tm, d, col=0):
    return pl.BlockSpec((tm, d), lambda i: (i, col))


def _kmajor(nk, tm, kb):
    return pl.BlockSpec((nk, tm, kb), lambda i: (0, i, 0))


def _dot(a, b):
    return jnp.dot(a, b, preferred_element_type=F32)


def _dot_nt(a, b):
    return lax.dot_general(a, b, NT, preferred_element_type=F32)


def _dot_tn(a, b):
    return lax.dot_general(a, b, TN, preferred_element_type=F32)


def _gelu(x):
    return 0.5 * x * (1.0 + jnp.tanh(0.7978845608028654 * (x + 0.044715 * x * x * x)))


def _gelu_and_grad(x):
    x2 = x * x
    th = jnp.tanh(0.7978845608028654 * (x + 0.044715 * x * x2))
    cdf = 0.5 * (1.0 + th)
    return x * cdf, cdf + 0.5 * x * (1.0 - th * th) * 0.7978845608028654 * (1.0 + 3.0 * 0.044715 * x2)


def _sigmoid(x):
    return 1.0 / (1.0 + jnp.exp(-x))


def _rstd(h):
    return lax.rsqrt(jnp.mean(h * h, axis=-1, keepdims=True) + EPS)


def _rms_bwd(h, g, dy):
    r = _rstd(h)
    t = dy * g
    dh = r * t - h * (r * r * r) * jnp.mean(h * t, axis=-1, keepdims=True)
    return dh, jnp.sum(dy * h * r, axis=0, keepdims=True)


def _ordered_after(body, n_in, deps):
    if not deps:
        return body
    return lambda *refs: body(*refs[:n_in], *refs[n_in + len(deps):])


def _accumulate(ref, value):
    @pl.when(pl.program_id(0) == 0)
    def _():
        ref[...] = jnp.zeros_like(ref)

    ref[...] += value


def rms_cast(h, g, name):
    t, d = h.shape
    tm = _tile(t, 512)

    def body(h_ref, g_ref, o_ref):
        hv = h_ref[...]
        o_ref[...] = (hv * _rstd(hv) * g_ref[...]).astype(BF16)

    return pl.pallas_call(
        body, name=name, grid=(t // tm,), in_specs=[_rows(tm, d), _const((1, d))], out_specs=_rows(tm, d),
        out_shape=jax.ShapeDtypeStruct((t, d), BF16), compiler_params=_params("parallel"))(h, g)


def ffn_gu(xn, wgt4, wut4, name):
    t, d = xn.shape
    nk, fk, _ = wgt4.shape
    tm = _tile(t, 512)

    def body(x_ref, wg_ref, wu_ref, a_ref, s_ref, t_ref):
        xv = x_ref[...]
        g = _dot_nt(xv, wg_ref[0])
        u = _dot_nt(xv, wu_ref[0])
        sg = _sigmoid(g)
        s = g * sg
        a_ref[0] = (s * u).astype(BF16)
        s_ref[0] = s.astype(BF16)
        t_ref[0] = (u * sg * (1.0 + g * (1.0 - sg))).astype(BF16)

    w_spec = pl.BlockSpec((1, fk, d), lambda k, i: (k, 0, 0))
    o_spec = pl.BlockSpec((1, tm, fk), lambda k, i: (k, i, 0))
    shape = jax.ShapeDtypeStruct((nk, t, fk), BF16)
    return pl.pallas_call(
        body, name=name, grid=(nk, t // tm), in_specs=[pl.BlockSpec((tm, d), lambda k, i: (i, 0)), w_spec, w_spec],
        out_specs=[o_spec] * 3, out_shape=[shape] * 3, compiler_params=_params("parallel", "parallel"))(xn, wgt4, wut4)


def mm_norm_res(a3, w3, h_old, g_post, g_next, scale, name):
    nk, t, kb = a3.shape
    d = w3.shape[2]
    tm = _tile(t, 256)

    def body(a_ref, w_ref, h_ref, gp_ref, gn_ref, f_ref, hn_ref, xn_ref):
        f = _dot(a_ref[0], w_ref[0])
        for k in range(1, nk):
            f += _dot(a_ref[k], w_ref[k])
        f_ref[...] = f
        hn = h_ref[...] + scale * (f * _rstd(f) * gp_ref[...])
        hn_ref[...] = hn
        xn_ref[...] = (hn * _rstd(hn) * gn_ref[...]).astype(BF16)

    return pl.pallas_call(
        body, name=name, grid=(t // tm,),
        in_specs=[_kmajor(nk, tm, kb), _const((nk, kb, d)), _rows(tm, d), _const((1, d)), _const((1, d))],
        out_specs=[_rows(tm, d)] * 3,
        out_shape=[jax.ShapeDtypeStruct((t, d), F32), jax.ShapeDtypeStruct((t, d), F32), jax.ShapeDtypeStruct((t, d), BF16)],
        compiler_params=_params("parallel"))(a3, w3, h_old, g_post, g_next)


def mixer_in(xn, win4, name):
    t, d = xn.shape
    nk, _, nb = win4.shape
    tm = _tile(t, 512)

    def body(x_ref, w_ref, z_ref):
        z_ref[...] = _dot(x_ref[...], w_ref[0]).astype(BF16)

    return pl.pallas_call(
        body, name=name, grid=(nk, t // tm),
        in_specs=[pl.BlockSpec((tm, d), lambda k, i: (i, 0)), pl.BlockSpec((1, d, nb), lambda k, i: (k, 0, 0))],
        out_specs=pl.BlockSpec((tm, nb), lambda k, i: (i, k)), out_shape=jax.ShapeDtypeStruct((t, nk * nb), BF16),
        compiler_params=_params("parallel", "parallel"))(xn, win4)


def _causal_mask():
    row = lax.broadcasted_iota(jnp.int32, (CHUNK, CHUNK), 0)
    col = lax.broadcasted_iota(jnp.int32, (CHUNK, CHUNK), 1)
    return row >= col


def _layernorm_parts(v):
    mu = jnp.mean(v, axis=-1, keepdims=True)
    vc = v - mu
    r = lax.rsqrt(jnp.mean(vc * vc, axis=-1, keepdims=True) + EPS)
    return vc * r, r


def sgu_fwd(z, norm_g, sgu_w, sgu_b3, name):
    t = z.shape[0]
    d = norm_g.shape[1]
    ng = sgu_w.shape[0]
    dg = d // ng
    tm = _tile(t, 256)

    def body(zu_ref, zv_ref, ng_ref, w_ref, b_ref, a_ref):
        vhat, _ = _layernorm_parts(_gelu(zv_ref[...].astype(F32)))
        vn = (vhat * ng_ref[...]).astype(BF16)
        u = _gelu(zu_ref[...].astype(F32))
        mask = _causal_mask()
        for g in range(ng):
            wg = jnp.where(mask, w_ref[g], 0.0).astype(BF16)
            for ci in range(tm // CHUNK):
                rs, cs = slice(ci * CHUNK, (ci + 1) * CHUNK), slice(g * dg, (g + 1) * dg)
                sv = _dot(wg, vn[rs, cs]) + b_ref[g]
                a_ref[rs, cs] = (u[rs, cs] * sv).astype(BF16)

    return pl.pallas_call(
        body, name=name, grid=(t // tm,),
        in_specs=[_rows(tm, d, 0), _rows(tm, d, 1), _const((1, d)), _const((ng, CHUNK, CHUNK)), _const((ng, CHUNK, 1))],
        out_specs=_rows(tm, d), out_shape=jax.ShapeDtypeStruct((t, d), BF16),
        compiler_params=_params("parallel"))(z, z, norm_g, sgu_w, sgu_b3)


def pool_fwd(z, pool_w, pool_scale, name):
    t = z.shape[0]
    d = pool_scale.shape[1]
    ng = pool_w.shape[0]
    dg = d // ng
    tm = _tile(t, 256)
    per = tm // HALO

    def body(c_ref, prev_ref, w_ref, s_ref, diff_ref, b_ref):
        i = pl.program_id(0)
        cur = c_ref[...].astype(F32)
        prev = jnp.where(i > 0, prev_ref[...].astype(F32), 0.0)
        ext = jnp.concatenate([prev, cur], axis=0)
        tok = i * tm + lax.broadcasted_iota(jnp.int32, (tm, 1), 0)
        for g, win in enumerate(POOL_WINDOWS):
            cs = slice(g * dg, (g + 1) * dg)
            s = ext[:, cs]
            sh = 1
            while sh < win:
                s = s + pltpu.roll(s, sh, 0)
                sh *= 2
            cnt = jnp.minimum(tok + 1, win).astype(F32)
            diff = (s[HALO:] / cnt - cur[:, cs]).astype(BF16)
            diff_ref[:, cs] = diff
            b_ref[:, cs] = (_dot(diff, w_ref[g]) * s_ref[:, cs]).astype(BF16)

    return pl.pallas_call(
        body, name=name, grid=(t // tm,),
        in_specs=[_rows(tm, d, 2), pl.BlockSpec((HALO, d), lambda i: (jnp.maximum(i * per - 1, 0), 2)),
                  _const((ng, dg, dg)), _const((1, d))],
        out_specs=[_rows(tm, d)] * 2, out_shape=[jax.ShapeDtypeStruct((t, d), BF16)] * 2,
        compiler_params=_params("parallel"))(z, z, pool_w, pool_scale)


def mixer_y(a, b, z, woa, wob, name):
    t, d = a.shape
    tm = _tile(t, 256)

    def body(a_ref, b_ref, ga_ref, gb_ref, wa_ref, wb_ref, ya_ref, yb_ref, y_ref):
        ya = _dot(a_ref[...], wa_ref[...])
        yb = _dot(b_ref[...], wb_ref[...])
        ya_ref[...] = ya.astype(BF16)
        yb_ref[...] = yb.astype(BF16)
        y_ref[...] = (_sigmoid(ga_ref[...].astype(F32)) * ya + _sigmoid(gb_ref[...].astype(F32)) * yb).astype(BF16)

    return pl.pallas_call(
        body, name=name, grid=(t // tm,),
        in_specs=[_rows(tm, d), _rows(tm, d), _rows(tm, d, 3), _rows(tm, d, 4), _const((d, d)), _const((d, d))],
        out_specs=[_rows(tm, d)] * 3, out_shape=[jax.ShapeDtypeStruct((t, d), BF16)] * 3,
        compiler_params=_params("parallel"))(a, b, z, z, woa, wob)


def ple_loss(xn, p, wpg, wpp, h, g_post, g_pre, target, name):
    t, d = xn.shape
    dp = p.shape[1]
    tm = _tile(t, 256)

    def body(x_ref, p_ref, wg_ref, wp_ref, h_ref, gp_ref, gn_ref, tg_ref, ds_ref, de_ref, dhp_ref, dgp_ref, dgn_ref, loss_ref):
        gate = _sigmoid(_dot(x_ref[...], wg_ref[...]))
        e = _dot(p_ref[...], wp_ref[...])
        q = gate * e
        hv = h_ref[...]
        err = hv + q * _rstd(q) * gp_ref[...] - tg_ref[...]
        _accumulate(loss_ref, jnp.full(loss_ref.shape, (0.5 / d) * jnp.sum(err * err), F32))
        dhv = err * (1.0 / d)
        dq, dgp = _rms_bwd(q, gp_ref[...], dhv)
        ds = (dq * e * gate * (1.0 - gate)).astype(BF16)
        ds_ref[...] = ds
        de_ref[...] = (dq * gate).astype(BF16)
        dx, dgn = _rms_bwd(hv, gn_ref[...], _dot_nt(ds, wg_ref[...]))
        dhp_ref[...] = dhv + dx
        _accumulate(dgp_ref, dgp)
        _accumulate(dgn_ref, dgn)

    return pl.pallas_call(
        body, name=name, grid=(t // tm,),
        in_specs=[_rows(tm, d), _rows(tm, dp), _const((d, d)), _const((dp, d)), _rows(tm, d), _const((1, d)), _const((1, d)),
                  _rows(tm, d)],
        out_specs=[_rows(tm, d)] * 3 + [_const((1, d))] * 2 + [_const((8, 128))],
        out_shape=[jax.ShapeDtypeStruct((t, d), BF16), jax.ShapeDtypeStruct((t, d), BF16), jax.ShapeDtypeStruct((t, d), F32),
                   jax.ShapeDtypeStruct((1, d), F32), jax.ShapeDtypeStruct((1, d), F32), jax.ShapeDtypeStruct((8, 128), F32)],
        compiler_params=_params("arbitrary"))(xn, p, wpg, wpp, h, g_post, g_pre, target)


def ffn_bwd_a(dh, f, g_post, wd4, s4, t4, scale, name, deps=()):
    t, d = dh.shape
    nk, fk, _ = wd4.shape
    tm = _tile(t, 256)

    def body(dh_ref, f_ref, gp_ref, w_ref, s_ref, t_ref, df_ref, dg_ref, du_ref, dgp_ref):
        df, dgp = _rms_bwd(f_ref[...], gp_ref[...], dh_ref[...])
        df = (scale * df).astype(BF16)
        df_ref[...] = df
        _accumulate(dgp_ref, scale * dgp)
        for k in range(nk):
            da = _dot_nt(df, w_ref[k])
            du_ref[k] = (da * s_ref[k].astype(F32)).astype(BF16)
            dg_ref[k] = (da * t_ref[k].astype(F32)).astype(BF16)

    return pl.pallas_call(
        _ordered_after(body, 6, deps), name=name, grid=(t // tm,),
        in_specs=[_rows(tm, d), _rows(tm, d), _const((1, d)), _const((nk, fk, d)), _kmajor(nk, tm, fk), _kmajor(nk, tm, fk)]
        + [ANY_SPEC] * len(deps),
        out_specs=[_rows(tm, d), _kmajor(nk, tm, fk), _kmajor(nk, tm, fk), _const((1, d))],
        out_shape=[jax.ShapeDtypeStruct((t, d), BF16), jax.ShapeDtypeStruct((nk, t, fk), BF16),
                   jax.ShapeDtypeStruct((nk, t, fk), BF16), jax.ShapeDtypeStruct((1, d), F32)],
        compiler_params=_params("arbitrary"))(dh, f, g_post, wd4, s4, t4, *deps)


def dx_norm_bwd(pairs, h, g_pre, dh_in, name, deps=()):
    t, d = h.shape
    tm = _tile(t, 256)
    n = len(pairs)

    def body(*refs):
        dys, ws = refs[:n], refs[n:2 * n]
        h_ref, g_ref, dhi_ref, dho_ref, dg_ref = refs[2 * n:]
        acc = None
        for (_, w4, sections), dy_ref, w_ref in zip(pairs, dys, ws):
            if sections:
                wide = w4.shape[2]
                edges = sorted(set(range(0, 5 * d + 1, d)) | set(range(0, 5 * d + 1, wide)))
                parts = [_dot_nt(dy_ref[DZ_SLOT[lo // d], :, lo % d:lo % d + hi - lo], w_ref[lo // wide, :, lo % wide:lo % wide + hi - lo])
                         for lo, hi in zip(edges[:-1], edges[1:])]
            else:
                parts = [_dot(dy_ref[k], w_ref[k]) for k in range(w4.shape[0])]
            for part in parts:
                acc = part if acc is None else acc + part
        dx, dg = _rms_bwd(h_ref[...], g_ref[...], acc)
        dho_ref[...] = dhi_ref[...] + dx
        _accumulate(dg_ref, dg)

    dy_specs = [_kmajor(dy.shape[0], tm, dy.shape[2]) for dy, _, _ in pairs]
    return pl.pallas_call(
        _ordered_after(body, 2 * n + 3, deps), name=name, grid=(t // tm,),
        in_specs=dy_specs + [_const(w4.shape) for _, w4, _ in pairs] + [_rows(tm, d), _const((1, d)), _rows(tm, d)]
        + [ANY_SPEC] * len(deps),
        out_specs=[_rows(tm, d), _const((1, d))],
        out_shape=[jax.ShapeDtypeStruct((t, d), F32), jax.ShapeDtypeStruct((1, d), F32)],
        compiler_params=_params("arbitrary"))(*[dy for dy, _, _ in pairs], *[w4 for _, w4, _ in pairs], h, g_pre, dh_in, *deps)


def mixer_bwd_y(dh, m, g_post, w_o, ya, yb, z, woa, wob, name, deps=()):
    t, d = dh.shape
    tm = _tile(t, 256)

    def body(dh_ref, m_ref, gp_ref, wo_ref, ya_ref, yb_ref, ga_ref, gb_ref, wa_ref, wb_ref,
             dm_ref, dya_ref, dyb_ref, dz_ref, da_ref, db_ref, dgp_ref):
        dm, dgp = _rms_bwd(m_ref[...], gp_ref[...], dh_ref[...])
        dm = dm.astype(BF16)
        dm_ref[...] = dm
        _accumulate(dgp_ref, dgp)
        dy = _dot_nt(dm, wo_ref[...])
        sa = _sigmoid(ga_ref[...].astype(F32))
        sb = _sigmoid(gb_ref[...].astype(F32))
        dya = (dy * sa).astype(BF16)
        dyb = (dy * sb).astype(BF16)
        dya_ref[...] = dya
        dyb_ref[...] = dyb
        dz_ref[0] = (dy * ya_ref[...].astype(F32) * sa * (1.0 - sa)).astype(BF16)
        dz_ref[1] = (dy * yb_ref[...].astype(F32) * sb * (1.0 - sb)).astype(BF16)
        da_ref[...] = _dot_nt(dya, wa_ref[...]).astype(BF16)
        db_ref[...] = _dot_nt(dyb, wb_ref[...]).astype(BF16)

    return pl.pallas_call(
        _ordered_after(body, 10, deps), name=name, grid=(t // tm,),
        in_specs=[_rows(tm, d), _rows(tm, d), _const((1, d)), _const((d, d)), _rows(tm, d), _rows(tm, d),
                  _rows(tm, d, 3), _rows(tm, d, 4), _const((d, d)), _const((d, d))] + [ANY_SPEC] * len(deps),
        out_specs=[_rows(tm, d)] * 3 + [pl.BlockSpec((2, tm, d), lambda i: (0, i, 0))] + [_rows(tm, d)] * 2 + [_const((1, d))],
        out_shape=[jax.ShapeDtypeStruct((t, d), BF16)] * 3 + [jax.ShapeDtypeStruct((5, t, d), BF16)]
        + [jax.ShapeDtypeStruct((t, d), BF16)] * 2 + [jax.ShapeDtypeStruct((1, d), F32)],
        compiler_params=_params("arbitrary"))(dh, m, g_post, w_o, ya, yb, z, z, woa, wob, *deps)


def sgu_bwd(z, da, dz, norm_g, sgu_w, sgu_b3, name):
    t, d = da.shape
    ng = sgu_w.shape[0]
    dg = d // ng
    tm = _tile(t, 256)
    steps = t // tm

    def body(zu_ref, zv_ref, da_ref, ng_ref, w_ref, b_ref, _, dz_ref, dw_ref, db_ref, dng_ref, dvn_ref, dsv_ref):
        i = pl.program_id(0)
        zv = zv_ref[...].astype(F32)
        zu = zu_ref[...].astype(F32)
        v, gv = _gelu_and_grad(zv)
        vhat, r = _layernorm_parts(v)
        gain = ng_ref[...]
        vn = (vhat * gain).astype(BF16)
        u, gu = _gelu_and_grad(zu)
        dav = da_ref[...].astype(F32)
        mask = _causal_mask()

        @pl.when(i == 0)
        def _():
            dw_ref[...] = jnp.zeros_like(dw_ref)
            dsv_ref[...] = jnp.zeros_like(dsv_ref)

        for g in range(ng):
            wg = jnp.where(mask, w_ref[g], 0.0).astype(BF16)
            dw = jnp.zeros((CHUNK, CHUNK), F32)
            dsv_sum = jnp.zeros((CHUNK, dg), F32)
            for ci in range(tm // CHUNK):
                rs, cs = slice(ci * CHUNK, (ci + 1) * CHUNK), slice(g * dg, (g + 1) * dg)
                vn_blk = vn[rs, cs]
                sv = _dot(wg, vn_blk) + b_ref[g]
                dz_ref[0, rs, cs] = (dav[rs, cs] * sv * gu[rs, cs]).astype(BF16)
                dsv = dav[rs, cs] * u[rs, cs]
                dsv_sum += dsv
                dsv = dsv.astype(BF16)
                dw += _dot_nt(dsv, vn_blk)
                dvn_ref[rs, cs] = _dot_tn(wg, dsv)
            dw_ref[g] += dw
            dsv_ref[:, cs] += dsv_sum

        dvn = dvn_ref[...]
        _accumulate(dng_ref, jnp.sum(dvn * vhat, axis=0, keepdims=True))
        dvh = dvn * gain
        dv = r * (dvh - jnp.mean(dvh, axis=-1, keepdims=True) - vhat * jnp.mean(dvh * vhat, axis=-1, keepdims=True))
        dz_ref[1] = (dv * gv).astype(BF16)

        @pl.when(i == steps - 1)
        def _():
            for g in range(ng):
                dw_ref[g] = jnp.where(mask, dw_ref[g], 0.0)
                row_sum = jnp.sum(dsv_ref[:, g * dg:(g + 1) * dg], axis=1, keepdims=True)
                db_ref[g] = jnp.broadcast_to(row_sum, (CHUNK, CHUNK))

    return pl.pallas_call(
        body, name=name, grid=(steps,),
        in_specs=[_rows(tm, d, 0), _rows(tm, d, 1), _rows(tm, d), _const((1, d)), _const((ng, CHUNK, CHUNK)), _const((ng, CHUNK, 1)),
                  ANY_SPEC],
        out_specs=[pl.BlockSpec((2, tm, d), lambda i: (DZ_SLOT[0] // 2, i, 0)), _const((ng, CHUNK, CHUNK)),
                   _const((ng, CHUNK, CHUNK)), _const((1, d))],
        out_shape=[jax.ShapeDtypeStruct(dz.shape, BF16), jax.ShapeDtypeStruct((ng, CHUNK, CHUNK), F32),
                   jax.ShapeDtypeStruct((ng, CHUNK, CHUNK), F32), jax.ShapeDtypeStruct((1, d), F32)],
        scratch_shapes=[pltpu.VMEM((tm, d), F32), pltpu.VMEM((CHUNK, d), F32)], input_output_aliases={6: 0},
        compiler_params=_params("arbitrary"))(z, z, da, norm_g, sgu_w, sgu_b3, dz)


def pool_bwd(db, diff, dz, pool_w, pool_scale, name):
    t, d = db.shape
    ng = pool_w.shape[0]
    dg = d // ng
    tm = _tile(t, 256)
    per = tm // HALO
    steps = t // tm

    def body(db_ref, next_ref, diff_ref, w_ref, s_ref, _, dc_ref, dw_ref, ds_ref):
        i = pl.program_id(0)
        dbc = db_ref[...].astype(F32)
        nxt = jnp.where(i < steps - 1, next_ref[...].astype(F32), 0.0)
        ext = jnp.concatenate([dbc, nxt], axis=0)
        rows = tm + HALO
        tok = i * tm + lax.broadcasted_iota(jnp.int32, (rows, 1), 0)

        @pl.when(i == 0)
        def _():
            dw_ref[...] = jnp.zeros_like(dw_ref)
            ds_ref[...] = jnp.zeros_like(ds_ref)

        for g, win in enumerate(POOL_WINDOWS):
            cs = slice(g * dg, (g + 1) * dg)
            dp = (ext[:, cs] * s_ref[:, cs]).astype(BF16)
            dd = _dot_nt(dp, w_ref[g])
            s = dd / jnp.minimum(tok + 1, win).astype(F32)
            sh = 1
            while sh < win:
                s = s + pltpu.roll(s, rows - sh, 0)
                sh *= 2
            dc_ref[0, :, cs] = (s[:tm] - dd[:tm]).astype(BF16)
            dfg = diff_ref[:, cs]
            ds_ref[:, cs] += jnp.sum(dbc[:, cs] * _dot(dfg, w_ref[g]), axis=0, keepdims=True)
            dw_ref[g] += _dot_tn(dfg, dp[:tm])

    return pl.pallas_call(
        body, name=name, grid=(steps,),
        in_specs=[_rows(tm, d), pl.BlockSpec((HALO, d), lambda i: (jnp.minimum((i + 1) * per, t // HALO - 1), 0)),
                  _rows(tm, d), _const((ng, dg, dg)), _const((1, d)), ANY_SPEC],
        out_specs=[pl.BlockSpec((1, tm, d), lambda i: (DZ_SLOT[2], i, 0)), _const((ng, dg, dg)), _const((1, d))],
        out_shape=[jax.ShapeDtypeStruct(dz.shape, BF16), jax.ShapeDtypeStruct((ng, dg, dg), F32), jax.ShapeDtypeStruct((1, d), F32)],
        input_output_aliases={5: 0},
        compiler_params=_params("arbitrary"))(db, db, diff, pool_w, pool_scale, dz)


def dw_tn(x, dy, nk, name, x_kmajor=False, dy_mode="same"):
    t = x.shape[-2]
    kx = x.shape[-1]
    n = dy.shape[-1] // nk if dy_mode == "cols" else dy.shape[-1]
    tt = _tile(t, DW_TOKENS)
    steps = t // tt

    def body(x_ref, dy_ref, o_ref, acc_ref):
        s = pl.program_id(1)
        part = _dot_tn(x_ref[0] if x_kmajor else x_ref[...], dy_ref[0] if dy_mode == "kmajor" else dy_ref[...])
        if steps == 1:
            o_ref[0] = part.astype(BF16)
            return

        @pl.when(s == 0)
        def _():
            acc_ref[...] = jnp.zeros_like(acc_ref)

        acc_ref[...] += part

        @pl.when(s == steps - 1)
        def _():
            o_ref[0] = acc_ref[...].astype(BF16)

    x_spec = pl.BlockSpec((1, tt, kx), lambda k, s: (k, s, 0)) if x_kmajor else pl.BlockSpec((tt, kx), lambda k, s: (s, 0))
    dy_spec = {"kmajor": pl.BlockSpec((1, tt, n), lambda k, s: (k, s, 0)), "cols": pl.BlockSpec((tt, n), lambda k, s: (s, k)),
               "same": pl.BlockSpec((tt, n), lambda k, s: (s, 0))}[dy_mode]
    return pl.pallas_call(
        body, name=name, grid=(nk, steps), in_specs=[x_spec, dy_spec],
        out_specs=pl.BlockSpec((1, kx, n), lambda k, s: (k, 0, 0)), out_shape=jax.ShapeDtypeStruct((nk, kx, n), BF16),
        scratch_shapes=[pltpu.VMEM((kx, n) if steps > 1 else (8, 128), F32)],
        compiler_params=_params("parallel", "arbitrary"))(x, dy)


def dw_in_tiles(xn, dz, nk, name):
    t, d = xn.shape
    sections = len(DZ_SLOT)
    per_section = d // DW_IN_TILE
    per_shard = sections * per_section // nk

    def body(x_ref, dy_ref, o_ref):
        o_ref[0] = _dot_tn(x_ref[...], dy_ref[0]).astype(BF16)

    def slot(j):
        return (j // per_section + DZ_SLOT[0]) % sections

    return pl.pallas_call(
        body, name=name, grid=(sections * per_section,),
        in_specs=[pl.BlockSpec((t, d), lambda j: (0, 0)), pl.BlockSpec((1, t, DW_IN_TILE), lambda j: (slot(j), 0, j % per_section))],
        out_specs=pl.BlockSpec((1, d, DW_IN_TILE), lambda j: (j // per_shard, 0, j % per_shard)),
        out_shape=jax.ShapeDtypeStruct((nk, d, sections * d // nk), BF16), compiler_params=_params("parallel"))(xn, dz)


def _place():
    x, y, c = lax.axis_index("x"), lax.axis_index("y"), lax.axis_index("c")
    chips = [((1 - x) if fx else x, (1 - y) if fy else y) for fx, fy in OTHER_CHIPS]
    return x, y, c, chips


def _half(rows, which):
    return pl.ds(pl.multiple_of(which * (rows // 2), 16), rows // 2)


def _hbm(a):
    return pltpu.with_memory_space_constraint(a, pltpu.HBM)


def _by_shape(arrays):
    buckets = {}
    for i, a in enumerate(arrays):
        buckets.setdefault(a.shape, []).append(i)
    return list(buckets.values())


def cast_place(shards, chip, name, deps=()):
    n = len(shards)
    r, cdim = shards[0].shape
    tr = _tile(r, 256)

    def body(chip_ref, *refs):
        for w_ref, o_ref in zip(refs[:n], refs[n + len(deps):]):
            o_ref[0] = w_ref[...].astype(BF16)

    return pl.pallas_call(
        body, name=name,
        grid_spec=pltpu.PrefetchScalarGridSpec(
            num_scalar_prefetch=1, grid=(r // tr,),
            in_specs=[pl.BlockSpec((tr, cdim), lambda i, chip_ref: (i, 0))] * n + [ANY_SPEC] * len(deps),
            out_specs=[pl.BlockSpec((1, tr, cdim), lambda i, chip_ref: (chip_ref[0], i, 0))] * n),
        out_shape=[jax.ShapeDtypeStruct((N_CHIPS, r, cdim), BF16)] * n, compiler_params=_params("parallel"))(chip, *shards, *deps)


def _gather_copy(buf, sends, recvs, i, j, me, chip_xy, c):
    cx, cy = chip_xy
    mine = _half(buf.shape[1], c)
    return pltpu.make_async_remote_copy(
        src_ref=buf.at[me, mine], dst_ref=buf.at[me, mine], send_sem=sends.at[3 * i + j], recv_sem=recvs.at[3 * i + j],
        device_id=(cx, cy, c), device_id_type=MESH_IDS)


def allgather_start(bufs, name):
    n = len(bufs)

    def body(*refs):
        ins = refs[:n]
        sends, recvs = refs[n], refs[n + 1]
        token = refs[2 * n + 2]
        x, y, c, chips = _place()
        for i in range(n):
            for j, chip_xy in enumerate(chips):
                _gather_copy(ins[i], sends, recvs, i, j, 2 * x + y, chip_xy, c).start()
        token[...] = jnp.zeros_like(token)

    out = pl.pallas_call(
        body, name=name, in_specs=[HBM_SPEC] * n,
        out_specs=[SEM_SPEC, SEM_SPEC] + [HBM_SPEC] * n + [VMEM_SPEC],
        out_shape=[pltpu.SemaphoreType.DMA((3 * n,)), pltpu.SemaphoreType.DMA((3 * n,))]
        + [pltpu.HBM(b.shape, b.dtype) for b in bufs] + [jax.ShapeDtypeStruct((8, 128), F32)],
        input_output_aliases={i: i + 2 for i in range(n)},
        compiler_params=pltpu.CompilerParams(has_side_effects=DATAFLOW))(*[_hbm(b) for b in bufs])
    return out[0], out[1], list(out[2:2 + n]), out[2 + n]


def allgather_wait(sends, recvs, bufs, after, name):
    n = len(bufs)

    def body(*refs):
        ins = refs[:n]
        send_sems, recv_sems = refs[n], refs[n + 1]
        x, y, c, chips = _place()
        for i in range(n):
            for j, (cx, cy) in enumerate(chips):
                mine = _half(ins[i].shape[1], c)
                cp = pltpu.make_async_remote_copy(
                    src_ref=ins[i].at[2 * x + y, mine], dst_ref=ins[i].at[2 * cx + cy, mine], send_sem=send_sems.at[3 * i + j],
                    recv_sem=recv_sems.at[3 * i + j], device_id=(cx, cy, c), device_id_type=MESH_IDS)
                cp.wait_send()
                cp.wait_recv()

    return pl.pallas_call(
        body, name=name, in_specs=[HBM_SPEC] * n + [SEM_SPEC, SEM_SPEC, ANY_SPEC], out_specs=[HBM_SPEC] * n,
        out_shape=[pltpu.HBM(b.shape, b.dtype) for b in bufs], input_output_aliases={i: i for i in range(n)},
        compiler_params=pltpu.CompilerParams(has_side_effects=DATAFLOW))(*bufs, sends, recvs, after)


def d2d_forward(bufs, name):
    n = len(bufs)

    def body(*refs):
        ins = refs[:n]
        send_sems, recv_sems = refs[2 * n:]
        x, y, c, chips = _place()
        copies = []
        for i in range(n):
            mine = _half(ins[i].shape[1], c)
            for j, (cx, cy) in enumerate(chips):
                landed = ins[i].at[2 * cx + cy, mine]
                cp = pltpu.make_async_remote_copy(
                    src_ref=landed, dst_ref=landed, send_sem=send_sems.at[i, j], recv_sem=recv_sems.at[i, j],
                    device_id=(x, y, 1 - c), device_id_type=MESH_IDS)
                cp.start()
                copies.append(cp)
        for i in range(n):
            theirs = _half(ins[i].shape[1], 1 - c)
            for j, (cx, cy) in enumerate(chips):
                passed = ins[i].at[2 * cx + cy, theirs]
                pltpu.make_async_remote_copy(
                    src_ref=passed, dst_ref=passed, send_sem=send_sems.at[i, j], recv_sem=recv_sems.at[i, j],
                    device_id=(x, y, 1 - c), device_id_type=MESH_IDS).wait_recv()
        for cp in copies:
            cp.wait_send()

    return pl.pallas_call(
        body, name=name, in_specs=[HBM_SPEC] * n, out_specs=[HBM_SPEC] * n,
        out_shape=[jax.ShapeDtypeStruct(b.shape, b.dtype) for b in bufs], input_output_aliases={i: i for i in range(n)},
        scratch_shapes=[pltpu.SemaphoreType.DMA((n, 3))] * 2,
        compiler_params=pltpu.CompilerParams(has_side_effects=True))(*bufs)


def exchange_halves(grads, name):
    n = len(grads)

    def body(*refs):
        ins, outs = refs[:n], refs[n:2 * n]
        send_sems, recv_sems = refs[2 * n:]
        x, y, c, _ = _place()
        copies = []
        for i in range(n):
            cp = pltpu.make_async_remote_copy(
                src_ref=ins[i].at[:, _half(ins[i].shape[1], 1 - c)], dst_ref=outs[i], send_sem=send_sems.at[i],
                recv_sem=recv_sems.at[i], device_id=(x, y, 1 - c), device_id_type=MESH_IDS)
            cp.start()
            copies.append(cp)
        for cp in copies:
            cp.wait()

    return pl.pallas_call(
        body, name=name, in_specs=[HBM_SPEC] * n, out_specs=[HBM_SPEC] * n,
        out_shape=[jax.ShapeDtypeStruct((g.shape[0], g.shape[1] // 2, g.shape[2]), g.dtype) for g in grads],
        scratch_shapes=[pltpu.SemaphoreType.DMA((n,))] * 2,
        compiler_params=pltpu.CompilerParams(has_side_effects=True))(*grads)


def _scatter_copy(src, land, sends, recvs, i, j, chip_xy, c):
    cx, cy = chip_xy
    return pltpu.make_async_remote_copy(
        src_ref=src.at[2 * cx + cy], dst_ref=land.at[j], send_sem=sends.at[3 * i + j], recv_sem=recvs.at[3 * i + j],
        device_id=(cx, cy, c), device_id_type=MESH_IDS)


def scatter_start(sums, name):
    n = len(sums)
    lands = [lax.empty((3,) + s.shape[1:], s.dtype) for s in sums]

    def body(*refs):
        srcs, zones = refs[:n], refs[n:2 * n]
        sends, recvs = refs[2 * n], refs[2 * n + 1]
        token = refs[4 * n + 2]
        _, _, c, chips = _place()
        for i in range(n):
            for j, chip_xy in enumerate(chips):
                _scatter_copy(srcs[i], zones[i], sends, recvs, i, j, chip_xy, c).start()
        token[...] = jnp.zeros_like(token)

    out = pl.pallas_call(
        body, name=name, in_specs=[HBM_SPEC] * (2 * n),
        out_specs=[SEM_SPEC, SEM_SPEC] + [HBM_SPEC] * (2 * n) + [VMEM_SPEC],
        out_shape=[pltpu.SemaphoreType.DMA((3 * n,)), pltpu.SemaphoreType.DMA((3 * n,))]
        + [pltpu.HBM(a.shape, a.dtype) for a in sums + lands] + [jax.ShapeDtypeStruct((8, 128), F32)],
        input_output_aliases={i: i + 2 for i in range(2 * n)},
        compiler_params=pltpu.CompilerParams(has_side_effects=DATAFLOW))(*[_hbm(a) for a in sums + lands])
    return out[0], out[1], list(out[2:2 + n]), list(out[2 + n:2 + 2 * n]), out[2 + 2 * n]


def scatter_wait(sends, recvs, sums, lands, after, name):
    n = len(sums)

    def body(*refs):
        srcs, zones = refs[:n], refs[n:2 * n]
        send_sems, recv_sems = refs[2 * n], refs[2 * n + 1]
        _, _, c, chips = _place()
        for i in range(n):
            for j, chip_xy in enumerate(chips):
                cp = _scatter_copy(srcs[i], zones[i], send_sems, recv_sems, i, j, chip_xy, c)
                cp.wait_send()
                cp.wait_recv()

    out = pl.pallas_call(
        body, name=name, in_specs=[HBM_SPEC] * (2 * n) + [SEM_SPEC, SEM_SPEC, ANY_SPEC], out_specs=[HBM_SPEC] * (2 * n),
        out_shape=[pltpu.HBM(a.shape, a.dtype) for a in sums + lands], input_output_aliases={i: i for i in range(2 * n)},
        compiler_params=pltpu.CompilerParams(has_side_effects=DATAFLOW))(*sums, *lands, sends, recvs, after)
    return list(out[:n]), list(out[n:])


def send_to_sibling(halves, name):
    n = len(halves)

    def body(*refs):
        ins, outs = refs[:n], refs[n:2 * n]
        send_sems, recv_sems = refs[2 * n:]
        x, y, c, _ = _place()
        copies = []
        for i in range(n):
            cp = pltpu.make_async_remote_copy(
                src_ref=ins[i], dst_ref=outs[i], send_sem=send_sems.at[i], recv_sem=recv_sems.at[i],
                device_id=(x, y, 1 - c), device_id_type=MESH_IDS)
            cp.start()
            copies.append(cp)
        for cp in copies:
            cp.wait()

    return pl.pallas_call(
        body, name=name, in_specs=[HBM_SPEC] * n, out_specs=[HBM_SPEC] * n,
        out_shape=[jax.ShapeDtypeStruct(h.shape, h.dtype) for h in halves],
        scratch_shapes=[pltpu.SemaphoreType.DMA((n,))] * 2,
        compiler_params=pltpu.CompilerParams(has_side_effects=True))(*halves)


def add_halves(grads, recvs, core, name):
    n = len(grads)
    nk, r, cdim = grads[0].shape
    half = r // 2
    views = [g.reshape(nk, 2, half, cdim) for g in grads]

    def body(core_ref, *refs):
        for g_ref, r_ref, o_ref in zip(refs[:n], refs[n:2 * n], refs[2 * n:]):
            o_ref[0] = (g_ref[0, 0].astype(F32) + r_ref[0].astype(F32)).astype(BF16)

    return pl.pallas_call(
        body, name=name,
        grid_spec=pltpu.PrefetchScalarGridSpec(
            num_scalar_prefetch=1, grid=(nk,),
            in_specs=[pl.BlockSpec((1, 1, half, cdim), lambda k, core_ref: (k, core_ref[0], 0, 0))] * n
            + [pl.BlockSpec((1, half, cdim), lambda k, core_ref: (k, 0, 0))] * n,
            out_specs=[pl.BlockSpec((1, half, cdim), lambda k, core_ref: (k, 0, 0))] * n),
        out_shape=[jax.ShapeDtypeStruct((nk, half, cdim), BF16)] * n, compiler_params=_params("parallel"))(core, *views, *recvs)


def add_chips(sums, lands, chip, name):
    n = len(sums)
    _, half, cdim = sums[0].shape
    tr = _tile(half, 128)

    def body(chip_ref, *refs):
        for s_ref, r_ref, o_ref in zip(refs[:n], refs[n:2 * n], refs[2 * n:]):
            o_ref[...] = ((s_ref[0].astype(F32) + r_ref[0].astype(F32)) + r_ref[1].astype(F32)) + r_ref[2].astype(F32)

    return pl.pallas_call(
        body, name=name,
        grid_spec=pltpu.PrefetchScalarGridSpec(
            num_scalar_prefetch=1, grid=(half // tr,),
            in_specs=[pl.BlockSpec((1, tr, cdim), lambda i, chip_ref: (chip_ref[0], i, 0))] * n
            + [pl.BlockSpec((3, tr, cdim), lambda i, chip_ref: (0, i, 0))] * n,
            out_specs=[pl.BlockSpec((tr, cdim), lambda i, chip_ref: (i, 0))] * n),
        out_shape=[jax.ShapeDtypeStruct((half, cdim), F32)] * n, compiler_params=_params("parallel"))(chip, *sums, *lands)


def _adamw_math(w, g, m, v):
    m = ADAM_B1 * m + (1.0 - ADAM_B1) * g
    v = ADAM_B2 * v + (1.0 - ADAM_B2) * (g * g)
    m_hat = m / (1.0 - ADAM_B1 ** ADAM_STEP)
    v_hat = v / (1.0 - ADAM_B2 ** ADAM_STEP)
    return -ADAM_LR * (m_hat / (jnp.sqrt(v_hat) + ADAM_EPS) + ADAM_WD * w), m, v


def adamw_halves(ws, owns, others, ms, vs, core, name):
    n = len(ws)
    r, cdim = ws[0].shape
    half = r // 2
    tr = _tile(half, max(8, 256 // n))
    steps = half // tr

    def body(core_ref, *refs):
        ins, outs = refs[:5 * n], refs[5 * n:]
        for a in range(n):
            w_ref, own_ref, other_ref, m_ref, v_ref = ins[a::n]
            g = jnp.where(pl.program_id(0) == core_ref[0], own_ref[...], other_ref[...])
            outs[a][...] = g
            outs[n + a][...], outs[2 * n + a][...], outs[3 * n + a][...] = _adamw_math(w_ref[...], g, m_ref[...], v_ref[...])

    whole = pl.BlockSpec((tr, cdim), lambda hh, i, core_ref: (hh * steps + i, 0))
    own = pl.BlockSpec((tr, cdim), lambda hh, i, core_ref: (jnp.where(hh == core_ref[0], i, 0), 0))
    other = pl.BlockSpec((tr, cdim), lambda hh, i, core_ref: (jnp.where(hh == core_ref[0], 0, i), 0))
    out = pl.pallas_call(
        body, name=name,
        grid_spec=pltpu.PrefetchScalarGridSpec(
            num_scalar_prefetch=1, grid=(2, steps), in_specs=[whole] * n + [own] * n + [other] * n + [whole] * (2 * n),
            out_specs=[whole] * (4 * n)),
        out_shape=[jax.ShapeDtypeStruct((r, cdim), F32)] * (4 * n),
        compiler_params=_params("parallel", "parallel"))(core, *ws, *owns, *others, *ms, *vs)
    return [out[a::n] for a in range(n)]


def small_allreduce_adamw(g, w, m, v):
    r, cdim = g.shape

    def body(g_ref, w_ref, m_ref, v_ref, go_ref, d_ref, mo_ref, vo_ref, pair, quad, d2d_send, d2d_recv, ici_send, ici_recv):
        x, y, c, chips = _place()
        me = 2 * x + y
        pair[c] = g_ref[...]
        swap = pltpu.make_async_remote_copy(
            src_ref=g_ref, dst_ref=pair.at[c], send_sem=d2d_send, recv_sem=d2d_recv, device_id=(x, y, 1 - c),
            device_id_type=MESH_IDS)
        swap.start()
        swap.wait()
        quad[me] = pair[0] + pair[1]
        copies = []
        for j, (cx, cy) in enumerate(chips):
            cp = pltpu.make_async_remote_copy(
                src_ref=quad.at[me], dst_ref=quad.at[me], send_sem=ici_send.at[j], recv_sem=ici_recv.at[j],
                device_id=(cx, cy, c), device_id_type=MESH_IDS)
            cp.start()
            copies.append(cp)
        for j, (cx, cy) in enumerate(chips):
            slot = quad.at[2 * cx + cy]
            pltpu.make_async_remote_copy(
                src_ref=slot, dst_ref=slot, send_sem=ici_send.at[j], recv_sem=ici_recv.at[j], device_id=(cx, cy, c),
                device_id_type=MESH_IDS).wait_recv()
        for cp in copies:
            cp.wait_send()
        total = (quad[0] + quad[1]) + (quad[2] + quad[3])
        go_ref[...] = total
        d_ref[...], mo_ref[...], vo_ref[...] = _adamw_math(w_ref[...], total, m_ref[...], v_ref[...])

    return pl.pallas_call(
        body, name="small_allreduce_adamw", in_specs=[VMEM_SPEC] * 4, out_specs=[VMEM_SPEC] * 4,
        out_shape=[jax.ShapeDtypeStruct((r, cdim), F32)] * 4,
        scratch_shapes=[pltpu.VMEM((2, r, cdim), F32), pltpu.VMEM((N_CHIPS, r, cdim), F32), pltpu.SemaphoreType.DMA,
                        pltpu.SemaphoreType.DMA, pltpu.SemaphoreType.DMA((3,)), pltpu.SemaphoreType.DMA((3,))],
        compiler_params=pltpu.CompilerParams(has_side_effects=True, vmem_limit_bytes=VMEM_LIMIT_V7X))(g, w, m, v)


GROUPS = (
    ("ffn1", ("ffn1_w_gate", "ffn1_w_up", "ffn1_w_down")),
    ("mixer", ("w_in", "pool_w", "w_out_a", "w_out_b", "w_o")),
    ("ffn2", ("ffn2_w_gate", "ffn2_w_up", "ffn2_w_down")),
    ("ple", ("ple_w_gate", "ple_w_proj")),
)
GATHERS = (
    ("ffn1_in", ("ffn1_w_gate", "ffn1_w_up")),
    ("ffn1_out", ("ffn1_w_down",)),
) + GROUPS[1:]
GAINS = ("ffn1_pre_g", "ffn1_post_g", "mix_pre_g", "sgu_norm_g", "pool_scale", "mix_post_g",
         "ffn2_pre_g", "ffn2_post_g", "ple_pre_g", "ple_post_g")
SMALL = GAINS + ("sgu_b", "sgu_w")
PACKED = SMALL + ("loss",)
WEIGHTS = ("ffn1_pre_g", "ffn1_w_gate", "ffn1_w_up", "ffn1_w_down", "ffn1_post_g", "mix_pre_g", "w_in", "sgu_norm_g",
           "sgu_w", "sgu_b", "pool_w", "pool_scale", "w_out_a", "w_out_b", "w_o", "mix_post_g", "ffn2_pre_g",
           "ffn2_w_gate", "ffn2_w_up", "ffn2_w_down", "ffn2_post_g", "ple_pre_g", "ple_w_gate", "ple_w_proj", "ple_post_g")
PACK_ROWS = 16


TRANSPOSED = ("ffn1_w_gate", "ffn1_w_up", "ffn2_w_gate", "ffn2_w_up")


def _shard2d(name, a):
    a = a[0]
    return a.T if name in TRANSPOSED else a.reshape(-1, a.shape[-1])


def _unshard2d(name, a2d, shape):
    return (a2d.T if name in TRANSPOSED else a2d).reshape(shape)


def pack_rows(gains, sgu_b, loss_tile, name):
    n = len(gains)
    d = gains[0].shape[1]
    g = sgu_b.shape[0]

    def body(*refs):
        o_ref = refs[-1]
        o_ref[...] = jnp.zeros_like(o_ref)
        for i in range(n):
            o_ref[i:i + 1, :] = refs[i][...]
        o_ref[n:n + g, 0:CHUNK] = refs[n][...]
        o_ref[n + g:n + g + 1, 0:CHUNK] = refs[n + 1][0:1, :]

    return pl.pallas_call(
        body, name=name, in_specs=[VMEM_SPEC] * (n + 2), out_specs=VMEM_SPEC,
        out_shape=jax.ShapeDtypeStruct((PACK_ROWS, d), F32))(*gains, sgu_b, loss_tile)


def _pack_small(parts, tag):
    d = parts[GAINS[0]].shape[-1]
    rows = pack_rows([parts[n] for n in GAINS], parts["sgu_b"].reshape(-1, CHUNK), parts["loss"], "pack_" + tag)
    return jnp.concatenate([rows, parts["sgu_w"].reshape(-1, d)], axis=0)


def _unpack_small(packed, like):
    n, g = len(GAINS), like["sgu_b"].size // CHUNK
    out = {name: packed[i:i + 1] for i, name in enumerate(GAINS)}
    out["sgu_b"] = packed[n:n + g, :CHUNK].reshape(like["sgu_b"].shape)
    out["loss"] = packed[n + g, 0]
    out["sgu_w"] = packed[PACK_ROWS:].reshape(like["sgu_w"].shape)
    return out


def _ffn_fwd(xn, h, w, pre, g_post, g_next, tag, late_weights=None):
    a4, s4, t4 = ffn_gu(xn, w[pre + "w_gate"], w[pre + "w_up"], tag + "_gu")
    if late_weights is not None:
        w.update(late_weights(a4))
    f, h_new, xn_next = mm_norm_res(a4, w[pre + "w_down"], h, g_post, g_next, 0.5, tag + "_down")
    return dict(xn=xn, h=h, a4=a4, s4=s4, t4=t4, f=f), h_new, xn_next


def _ffn_bwd_w(dh, saved, w, pre, g_post, tag, deps):
    df, dg4, du4, d_post = ffn_bwd_a(dh, saved["f"], g_post, w[pre + "w_down"], saved["s4"], saved["t4"], 0.5, tag + "_bwd_a",
                                     deps=deps)
    nk = N_CHIPS
    grads = {
        pre + "w_down": dw_tn(saved["a4"], df, nk, tag + "_dw_down", x_kmajor=True, dy_mode="same"),
        pre + "w_gate": dw_tn(dg4, saved["xn"], nk, tag + "_dw_gate", x_kmajor=True, dy_mode="same"),
        pre + "w_up": dw_tn(du4, saved["xn"], nk, tag + "_dw_up", x_kmajor=True, dy_mode="same"),
    }
    return dg4, du4, grads, d_post


def _ffn_bwd_x(dh, dg4, du4, saved, w, pre, g_pre, tag, deps):
    return dx_norm_bwd([(dg4, w[pre + "w_gate"], False), (du4, w[pre + "w_up"], False)], saved["h"], g_pre, dh,
                       tag + "_bwd_x", deps=deps)


def _gather_group(names, weights, chip, tag, deps):
    shards = [_shard2d(n, weights[n]) for n in names]
    bufs = [None] * len(names)
    for b, idx in enumerate(_by_shape(shards)):
        for i, buf in zip(idx, cast_place([shards[i] for i in idx], chip, f"cast_{tag}_{b}", deps=deps)):
            bufs[i] = buf
    sends, recvs, bufs, token = allgather_start(bufs, "allgather_start_" + tag)
    return (sends, recvs, bufs), token


def _gathered(started, names, after, tag):
    sends, recvs, bufs = started
    landed = allgather_wait(sends, recvs, bufs, after, "allgather_wait_" + tag)
    return dict(zip(names, d2d_forward(landed, "d2d_forward_" + tag)))


def _reduce_start(names, big_g, core, tag):
    partial = [big_g[n] for n in names]
    from_sibling = exchange_halves(partial, "exchange_halves_" + tag)
    chip_sums = [None] * len(names)
    for b, idx in enumerate(_by_shape(partial)):
        for i, s in zip(idx, add_halves([partial[i] for i in idx], [from_sibling[i] for i in idx], core, f"add_halves_{tag}_{b}")):
            chip_sums[i] = s
    sends, recvs, sums, lands, token = scatter_start(chip_sums, "scatter_start_" + tag)
    return (names, sends, recvs, sums, lands), token


def _reduce_finish(started, after, weights, moments_m, moments_v, core, chip, tag, results):
    names, sends, recvs, sums, lands = started
    sums, lands = scatter_wait(sends, recvs, sums, lands, after, "scatter_wait_" + tag)
    buckets = _by_shape(sums)
    reduced = [None] * len(names)
    for b, idx in enumerate(buckets):
        for i, r in zip(idx, add_chips([sums[i] for i in idx], [lands[i] for i in idx], chip, f"add_chips_{tag}_{b}")):
            reduced[i] = r
    others = send_to_sibling(reduced, "send_to_sibling_" + tag)
    last = None
    for b, idx in enumerate(buckets):
        in_bucket = [names[i] for i in idx]
        outs = adamw_halves([_shard2d(n, weights[n]) for n in in_bucket], [reduced[i] for i in idx], [others[i] for i in idx],
                            [_shard2d(n, moments_m[n]) for n in in_bucket], [_shard2d(n, moments_v[n]) for n in in_bucket],
                            core, f"adamw_{tag}_{b}")
        for n, per_weight in zip(in_bucket, outs):
            for store, value in zip(results, per_weight):
                store[n] = _unshard2d(n, value, weights[n].shape)
            last = per_weight[1]
    return last


def kernel(x, p, ffn1_pre_g, ffn1_w_gate, ffn1_w_up, ffn1_w_down, ffn1_post_g, mix_pre_g, w_in, sgu_norm_g, sgu_w, sgu_b, pool_w, pool_scale, w_out_a, w_out_b, w_o, mix_post_g, ffn2_pre_g, ffn2_w_gate, ffn2_w_up, ffn2_w_down, ffn2_post_g, ple_pre_g, ple_w_gate, ple_w_proj, ple_post_g, loss_target, m_ffn1_pre_g, m_ffn1_w_gate, m_ffn1_w_up, m_ffn1_w_down, m_ffn1_post_g, m_mix_pre_g, m_w_in, m_sgu_norm_g, m_sgu_w, m_sgu_b, m_pool_w, m_pool_scale, m_w_out_a, m_w_out_b, m_w_o, m_mix_post_g, m_ffn2_pre_g, m_ffn2_w_gate, m_ffn2_w_up, m_ffn2_w_down, m_ffn2_post_g, m_ple_pre_g, m_ple_w_gate, m_ple_w_proj, m_ple_post_g, v_ffn1_pre_g, v_ffn1_w_gate, v_ffn1_w_up, v_ffn1_w_down, v_ffn1_post_g, v_mix_pre_g, v_w_in, v_sgu_norm_g, v_sgu_w, v_sgu_b, v_pool_w, v_pool_scale, v_w_out_a, v_w_out_b, v_w_o, v_mix_post_g, v_ffn2_pre_g, v_ffn2_w_gate, v_ffn2_w_up, v_ffn2_w_down, v_ffn2_post_g, v_ple_pre_g, v_ple_w_gate, v_ple_w_proj, v_ple_post_g):
    given = dict(locals())
    weights = {n: given[n] for n in WEIGHTS}
    moments_m = {n: given["m_" + n] for n in WEIGHTS}
    moments_v = {n: given["v_" + n] for n in WEIGHTS}
    core = lax.axis_index("c").astype(jnp.int32).reshape(1)
    chip = (2 * lax.axis_index("x") + lax.axis_index("y")).astype(jnp.int32).reshape(1)

    d = x.shape[-1]
    token = ()
    gathering = {}
    for tag, names in GATHERS:
        gathering[tag], tok = _gather_group(names, weights, chip, tag, token)
        token = (tok,)
    gain = {n: weights[n] for n in GAINS}
    sgu_w3 = sgu_w[0]
    sgu_b3 = sgu_b[0][:, :, None]
    groups = dict(GROUPS + GATHERS)

    h0 = x[0]
    tgt = loss_target[0]
    p_bf = p[0, 0].astype(BF16)
    w = _gathered(gathering["ffn1_in"], groups["ffn1_in"], token[0], "ffn1_in")
    xn1 = rms_cast(h0, gain["ffn1_pre_g"], "ffn1_pre_norm")
    s1, h1, xn2 = _ffn_fwd(xn1, h0, w, "ffn1_", gain["ffn1_post_g"], gain["mix_pre_g"], "ffn1",
                           lambda a4: _gathered(gathering["ffn1_out"], groups["ffn1_out"], a4, "ffn1_out"))
    w.update(_gathered(gathering["mixer"], groups["mixer"], h1, "mixer"))
    full = {n: w[n].reshape(-1, d) for n in ("w_out_a", "w_out_b", "w_o")}
    n_groups = pool_w.shape[1]
    rows_per = pool_w.shape[2]
    dgp = pool_w.shape[3]
    pool_full = w["pool_w"].reshape(N_CHIPS, n_groups, rows_per, dgp).transpose(1, 0, 2, 3).reshape(n_groups, N_CHIPS * rows_per, dgp)
    z = mixer_in(xn2, w["w_in"], "mixer_in")
    a = sgu_fwd(z, gain["sgu_norm_g"], sgu_w3, sgu_b3, "sgu_fwd")
    diff, b = pool_fwd(z, pool_full, gain["pool_scale"], "pool_fwd")
    ya, yb, y = mixer_y(a, b, z, full["w_out_a"], full["w_out_b"], "mixer_y")
    m, h2, xn3 = mm_norm_res(y[None], full["w_o"][None], h1, gain["mix_post_g"], gain["ffn2_pre_g"], 1.0, "mixer_out")
    w.update(_gathered(gathering["ffn2"], groups["ffn2"], h2, "ffn2"))
    s2, h3, xn4 = _ffn_fwd(xn3, h2, w, "ffn2_", gain["ffn2_post_g"], gain["ple_pre_g"], "ffn2")
    w.update(_gathered(gathering["ple"], groups["ple"], h3, "ple"))
    full["ple_w_gate"] = w["ple_w_gate"].reshape(-1, d)
    proj_full = w["ple_w_proj"].transpose(1, 0, 2).reshape(ple_w_proj.shape[1], -1)

    small_g, big_g = {}, {}
    ds, de, dh3, small_g["ple_post_g"], small_g["ple_pre_g"], loss_part = ple_loss(
        xn4, p_bf, full["ple_w_gate"], proj_full, h3, gain["ple_post_g"], gain["ple_pre_g"], tgt, "ple_loss")
    small_g["loss"] = loss_part
    grad, delta, new_m, new_v = {}, {}, {}, {}
    results = (grad, delta, new_m, new_v)
    finish = functools.partial(_reduce_finish, weights=weights, moments_m=moments_m, moments_v=moments_v, core=core, chip=chip,
                               results=results)
    big_g["ple_w_gate"] = dw_tn(xn4, ds, 1, "dw_ple_gate").reshape(N_CHIPS, -1, d)
    big_g["ple_w_proj"] = dw_tn(p_bf, de, N_CHIPS, "dw_ple_proj", dy_mode="cols")
    reducing_ple, tok_ple = _reduce_start(groups["ple"], big_g, core, "ple")

    dg4, du4, g2, small_g["ffn2_post_g"] = _ffn_bwd_w(dh3, s2, w, "ffn2_", gain["ffn2_post_g"], "ffn2", (tok_ple,))
    big_g.update(g2)
    dh2, small_g["ffn2_pre_g"] = _ffn_bwd_x(dh3, dg4, du4, s2, w, "ffn2_", gain["ffn2_pre_g"], "ffn2", ())
    reducing_ffn2, tok_ffn2 = _reduce_start(groups["ffn2"], big_g, core, "ffn2")
    done_ple = finish(reducing_ple, after=tok_ffn2, tag="ple")

    dm, dya, dyb, dz, da, db, small_g["mix_post_g"] = mixer_bwd_y(
        dh2, m, gain["mix_post_g"], full["w_o"], ya, yb, z, full["w_out_a"], full["w_out_b"], "mixer_bwd_y",
        deps=(tok_ffn2, done_ple))
    big_g["w_o"] = dw_tn(y, dm, 1, "dw_o").reshape(N_CHIPS, -1, d)
    big_g["w_out_a"] = dw_tn(a, dya, 1, "dw_out_a").reshape(N_CHIPS, -1, d)
    big_g["w_out_b"] = dw_tn(b, dyb, 1, "dw_out_b").reshape(N_CHIPS, -1, d)
    dz, d_sgu_w, d_sgu_b, small_g["sgu_norm_g"] = sgu_bwd(z, da, dz, gain["sgu_norm_g"], sgu_w3, sgu_b3, "sgu_bwd")
    dz, d_pool_w, small_g["pool_scale"] = pool_bwd(db, diff, dz, pool_full, gain["pool_scale"], "pool_bwd")
    big_g["pool_w"] = d_pool_w.astype(BF16).reshape(n_groups, N_CHIPS, rows_per, dgp).transpose(1, 0, 2, 3).reshape(
        N_CHIPS, n_groups * rows_per, dgp)
    big_g["w_in"] = dw_in_tiles(xn2, dz, N_CHIPS, "dw_in")
    dh1, small_g["mix_pre_g"] = dx_norm_bwd([(dz, w["w_in"], True)], h1, gain["mix_pre_g"], dh2, "mixer_bwd_x")
    reducing_mixer, tok_mixer = _reduce_start(groups["mixer"], big_g, core, "mixer")
    done_ffn2 = finish(reducing_ffn2, after=tok_mixer, tag="ffn2")

    dg4, du4, g1, small_g["ffn1_post_g"] = _ffn_bwd_w(dh1, s1, w, "ffn1_", gain["ffn1_post_g"], "ffn1", (tok_mixer, done_ffn2))
    big_g.update(g1)
    reducing_ffn1, tok_ffn1 = _reduce_start(groups["ffn1"], big_g, core, "ffn1")
    done_mixer = finish(reducing_mixer, after=tok_ffn1, tag="mixer")
    dh0, small_g["ffn1_pre_g"] = _ffn_bwd_x(dh1, dg4, du4, s1, w, "ffn1_", gain["ffn1_pre_g"], "ffn1", (tok_ffn1, done_mixer))
    finish(reducing_ffn1, after=dh0, tag="ffn1")
    small_g["sgu_w"] = d_sgu_w
    small_g["sgu_b"] = d_sgu_b[:, :, 0]

    no_state = {"loss": jnp.zeros((8, 128), F32)}
    packed = small_allreduce_adamw(
        _pack_small(small_g, "grads"), _pack_small({n: weights[n] for n in SMALL} | no_state, "weights"),
        _pack_small({n: moments_m[n] for n in SMALL} | no_state, "m"), _pack_small({n: moments_v[n] for n in SMALL} | no_state, "v"))
    like = {n: weights[n] for n in SMALL}
    for store, block in zip((grad, delta, new_m, new_v), packed):
        store.update(_unpack_small(block, like))

    return (grad["loss"], dh0[None], *[grad[n] for n in WEIGHTS], *[delta[n] for n in WEIGHTS],
            *[new_m[n] for n in WEIGHTS], *[new_v[n] for n in WEIGHTS])
```

```python
import functools

import jax
import jax.numpy as jnp
from jax import lax
from jax.experimental import pallas as pl
from jax.experimental.pallas import tpu as pltpu

F32 = jnp.float32
BF16 = jnp.bfloat16
EPS = 1e-6
CHUNK = 128
POOL_WINDOWS = (2, 4, 8, 16)
HALO = 16
N_CHIPS = 4
ADAM_LR, ADAM_B1, ADAM_B2, ADAM_EPS, ADAM_WD, ADAM_STEP = 0.001, 0.9, 0.999, 1e-08, 0.01, 10
VMEM_LIMIT_V7X = 58 * 1024 * 1024
MESH_IDS = pl.DeviceIdType.MESH
HBM_SPEC = pl.BlockSpec(memory_space=pltpu.HBM)
VMEM_SPEC = pl.BlockSpec(memory_space=pltpu.VMEM)
SEM_SPEC = pl.BlockSpec(memory_space=pltpu.SEMAPHORE)
ANY_SPEC = pl.BlockSpec(memory_space=pl.ANY)
DATAFLOW = pltpu.SideEffectType.DATAFLOW_SIDE_EFFECTING
OTHER_CHIPS = ((1, 0), (0, 1), (1, 1))
DZ_SLOT = (2, 3, 4, 0, 1)
ROW_CHUNK = 128
DW_TOKENS = 4096
DW_IN_TILE = 256

NT = (((1,), (1,)), ((), ()))
TN = (((0,), (0,)), ((), ()))


def _params(*sem, **more):
    return pltpu.CompilerParams(dimension_semantics=sem or None, vmem_limit_bytes=VMEM_LIMIT_V7X, **more)


def _tile(t, want):
    return max(c for c in range(8, min(t, want) + 1, 8) if t % c == 0)


def _const(shape):
    return pl.BlockSpec(shape, lambda *_: (0,) * len(shape))


def _rows(tm, d, col=0):
    return pl.BlockSpec((tm, d), lambda i: (i, col))


def _kmajor(nk, tm, kb):
    return pl.BlockSpec((nk, tm, kb), lambda i: (0, i, 0))


def _dot(a, b):
    return jnp.dot(a, b, preferred_element_type=F32)


def _dot_nt(a, b):
    return lax.dot_general(a, b, NT, preferred_element_type=F32)


def _dot_tn(a, b):
    return lax.dot_general(a, b, TN, preferred_element_type=F32)


def _gelu(x):
    return 0.5 * x * (1.0 + jnp.tanh(0.7978845608028654 * (x + 0.044715 * x * x * x)))


def _gelu_and_grad(x):
    x2 = x * x
    th = jnp.tanh(0.7978845608028654 * (x + 0.044715 * x * x2))
    cdf = 0.5 * (1.0 + th)
    return x * cdf, cdf + 0.5 * x * (1.0 - th * th) * 0.7978845608028654 * (1.0 + 3.0 * 0.044715 * x2)


def _sigmoid(x):
    return 1.0 / (1.0 + jnp.exp(-x))


def _rstd(h):
    return lax.rsqrt(jnp.mean(h * h, axis=-1, keepdims=True) + EPS)


def _rms_bwd(h, g, dy):
    r = _rstd(h)
    t = dy * g
    dh = r * t - h * (r * r * r) * jnp.mean(h * t, axis=-1, keepdims=True)
    return dh, jnp.sum(dy * h * r, axis=0, keepdims=True)


def _ordered_after(body, n_in, deps):
    if not deps:
        return body
    return lambda *refs: body(*refs[:n_in], *refs[n_in + len(deps):])


def _row_chunks(tm):
    size = ROW_CHUNK if tm % ROW_CHUNK == 0 else tm
    return [slice(lo, lo + size) for lo in range(0, tm, size)]


def _accumulate(ref, value):
    @pl.when(pl.program_id(0) == 0)
    def _():
        ref[...] = jnp.zeros_like(ref)

    ref[...] += value


def rms_cast(h, g, name):
    t, d = h.shape
    tm = _tile(t, 512)

    def body(h_ref, g_ref, o_ref):
        hv = h_ref[...]
        o_ref[...] = (hv * _rstd(hv) * g_ref[...]).astype(BF16)

    return pl.pallas_call(
        body, name=name, grid=(t // tm,), in_specs=[_rows(tm, d), _const((1, d))], out_specs=_rows(tm, d),
        out_shape=jax.ShapeDtypeStruct((t, d), BF16), compiler_params=_params("parallel"))(h, g)


def ffn_gu(xn, wgt4, wut4, name):
    t, d = xn.shape
    nk, fk, _ = wgt4.shape
    tm = _tile(t, 512)

    def body(x_ref, wg_ref, wu_ref, a_ref, s_ref, t_ref):
        xv = x_ref[...]
        g = _dot_nt(xv, wg_ref[0])
        u = _dot_nt(xv, wu_ref[0])
        sg = _sigmoid(g)
        s = g * sg
        a_ref[0] = (s * u).astype(BF16)
        s_ref[0] = s.astype(BF16)
        t_ref[0] = (u * sg * (1.0 + g * (1.0 - sg))).astype(BF16)

    w_spec = pl.BlockSpec((1, fk, d), lambda k, i: (k, 0, 0))
    o_spec = pl.BlockSpec((1, tm, fk), lambda k, i: (k, i, 0))
    shape = jax.ShapeDtypeStruct((nk, t, fk), BF16)
    return pl.pallas_call(
        body, name=name, grid=(nk, t // tm), in_specs=[pl.BlockSpec((tm, d), lambda k, i: (i, 0)), w_spec, w_spec],
        out_specs=[o_spec] * 3, out_shape=[shape] * 3, compiler_params=_params("parallel", "parallel"))(xn, wgt4, wut4)


def mm_norm_res(a3, w3, h_old, g_post, g_next, scale, name):
    nk, t, kb = a3.shape
    d = w3.shape[2]
    tm = _tile(t, 256)

    def body(a_ref, w_ref, h_ref, gp_ref, gn_ref, f_ref, hn_ref, xn_ref):
        for rows in _row_chunks(tm):
            f = _dot(a_ref[0, rows, :], w_ref[0])
            for k in range(1, nk):
                f += _dot(a_ref[k, rows, :], w_ref[k])
            f_ref[rows, :] = f
            hn = h_ref[rows, :] + scale * (f * _rstd(f) * gp_ref[...])
            hn_ref[rows, :] = hn
            xn_ref[rows, :] = (hn * _rstd(hn) * gn_ref[...]).astype(BF16)

    return pl.pallas_call(
        body, name=name, grid=(t // tm,),
        in_specs=[_kmajor(nk, tm, kb), _const((nk, kb, d)), _rows(tm, d), _const((1, d)), _const((1, d))],
        out_specs=[_rows(tm, d)] * 3,
        out_shape=[jax.ShapeDtypeStruct((t, d), F32), jax.ShapeDtypeStruct((t, d), F32), jax.ShapeDtypeStruct((t, d), BF16)],
        compiler_params=_params("parallel"))(a3, w3, h_old, g_post, g_next)


def mixer_in(xn, win4, name):
    t, d = xn.shape
    nk, _, nb = win4.shape
    tm = _tile(t, 512)

    def body(x_ref, w_ref, z_ref):
        z_ref[...] = _dot(x_ref[...], w_ref[0]).astype(BF16)

    return pl.pallas_call(
        body, name=name, grid=(nk, t // tm),
        in_specs=[pl.BlockSpec((tm, d), lambda k, i: (i, 0)), pl.BlockSpec((1, d, nb), lambda k, i: (k, 0, 0))],
        out_specs=pl.BlockSpec((tm, nb), lambda k, i: (i, k)), out_shape=jax.ShapeDtypeStruct((t, nk * nb), BF16),
        compiler_params=_params("parallel", "parallel"))(xn, win4)


def _causal_mask():
    row = lax.broadcasted_iota(jnp.int32, (CHUNK, CHUNK), 0)
    col = lax.broadcasted_iota(jnp.int32, (CHUNK, CHUNK), 1)
    return row >= col


def _layernorm_parts(v):
    mu = jnp.mean(v, axis=-1, keepdims=True)
    vc = v - mu
    r = lax.rsqrt(jnp.mean(vc * vc, axis=-1, keepdims=True) + EPS)
    return vc * r, r


def sgu_fwd(z, norm_g, sgu_w, sgu_b3, name):
    t = z.shape[0]
    d = norm_g.shape[1]
    ng = sgu_w.shape[0]
    dg = d // ng
    tm = _tile(t, 256)

    def body(zu_ref, zv_ref, ng_ref, w_ref, b_ref, a_ref):
        vhat, _ = _layernorm_parts(_gelu(zv_ref[...].astype(F32)))
        vn = (vhat * ng_ref[...]).astype(BF16)
        u = _gelu(zu_ref[...].astype(F32))
        mask = _causal_mask()
        for g in range(ng):
            wg = jnp.where(mask, w_ref[g], 0.0).astype(BF16)
            for ci in range(tm // CHUNK):
                rs, cs = slice(ci * CHUNK, (ci + 1) * CHUNK), slice(g * dg, (g + 1) * dg)
                sv = _dot(wg, vn[rs, cs]) + b_ref[g]
                a_ref[rs, cs] = (u[rs, cs] * sv).astype(BF16)

    return pl.pallas_call(
        body, name=name, grid=(t // tm,),
        in_specs=[_rows(tm, d, 0), _rows(tm, d, 1), _const((1, d)), _const((ng, CHUNK, CHUNK)), _const((ng, CHUNK, 1))],
        out_specs=_rows(tm, d), out_shape=jax.ShapeDtypeStruct((t, d), BF16),
        compiler_params=_params("parallel"))(z, z, norm_g, sgu_w, sgu_b3)


def pool_fwd(z, pool_w, pool_scale, name):
    t = z.shape[0]
    d = pool_scale.shape[1]
    ng = pool_w.shape[0]
    dg = d // ng
    tm = _tile(t, 256)
    per = tm // HALO

    def body(c_ref, prev_ref, w_ref, s_ref, diff_ref, b_ref):
        i = pl.program_id(0)
        cur = c_ref[...].astype(F32)
        prev = jnp.where(i > 0, prev_ref[...].astype(F32), 0.0)
        ext = jnp.concatenate([prev, cur], axis=0)
        tok = i * tm + lax.broadcasted_iota(jnp.int32, (tm, 1), 0)
        for g, win in enumerate(POOL_WINDOWS):
            cs = slice(g * dg, (g + 1) * dg)
            s = ext[:, cs]
            sh = 1
            while sh < win:
                s = s + pltpu.roll(s, sh, 0)
                sh *= 2
            cnt = jnp.minimum(tok + 1, win).astype(F32)
            diff = (s[HALO:] / cnt - cur[:, cs]).astype(BF16)
            diff_ref[:, cs] = diff
            b_ref[:, cs] = (_dot(diff, w_ref[g]) * s_ref[:, cs]).astype(BF16)

    return pl.pallas_call(
        body, name=name, grid=(t // tm,),
        in_specs=[_rows(tm, d, 2), pl.BlockSpec((HALO, d), lambda i: (jnp.maximum(i * per - 1, 0), 2)),
                  _const((ng, dg, dg)), _const((1, d))],
        out_specs=[_rows(tm, d)] * 2, out_shape=[jax.ShapeDtypeStruct((t, d), BF16)] * 2,
        compiler_params=_params("parallel"))(z, z, pool_w, pool_scale)


def mixer_y(a, b, z, woa, wob, name):
    t, d = a.shape
    tm = _tile(t, 256)

    def body(a_ref, b_ref, ga_ref, gb_ref, wa_ref, wb_ref, ya_ref, yb_ref, y_ref):
        ya = _dot(a_ref[...], wa_ref[...])
        yb = _dot(b_ref[...], wb_ref[...])
        ya_ref[...] = ya.astype(BF16)
        yb_ref[...] = yb.astype(BF16)
        y_ref[...] = (_sigmoid(ga_ref[...].astype(F32)) * ya + _sigmoid(gb_ref[...].astype(F32)) * yb).astype(BF16)

    return pl.pallas_call(
        body, name=name, grid=(t // tm,),
        in_specs=[_rows(tm, d), _rows(tm, d), _rows(tm, d, 3), _rows(tm, d, 4), _const((d, d)), _const((d, d))],
        out_specs=[_rows(tm, d)] * 3, out_shape=[jax.ShapeDtypeStruct((t, d), BF16)] * 3,
        compiler_params=_params("parallel"))(a, b, z, z, woa, wob)


def ple_loss(xn, p, wpg, wpp, h, g_post, g_pre, target, name):
    t, d = xn.shape
    dp = p.shape[1]
    tm = _tile(t, 256)

    def body(x_ref, p_ref, wg_ref, wp_ref, h_ref, gp_ref, gn_ref, tg_ref, ds_ref, de_ref, dhp_ref, dgp_ref, dgn_ref, loss_ref):
        gate = _sigmoid(_dot(x_ref[...], wg_ref[...]))
        e = _dot(p_ref[...], wp_ref[...])
        q = gate * e
        hv = h_ref[...]
        err = hv + q * _rstd(q) * gp_ref[...] - tg_ref[...]
        _accumulate(loss_ref, jnp.full(loss_ref.shape, (0.5 / d) * jnp.sum(err * err), F32))
        dhv = err * (1.0 / d)
        dq, dgp = _rms_bwd(q, gp_ref[...], dhv)
        ds = (dq * e * gate * (1.0 - gate)).astype(BF16)
        ds_ref[...] = ds
        de_ref[...] = (dq * gate).astype(BF16)
        dx, dgn = _rms_bwd(hv, gn_ref[...], _dot_nt(ds, wg_ref[...]))
        dhp_ref[...] = dhv + dx
        _accumulate(dgp_ref, dgp)
        _accumulate(dgn_ref, dgn)

    return pl.pallas_call(
        body, name=name, grid=(t // tm,),
        in_specs=[_rows(tm, d), _rows(tm, dp), _const((d, d)), _const((dp, d)), _rows(tm, d), _const((1, d)), _const((1, d)),
                  _rows(tm, d)],
        out_specs=[_rows(tm, d)] * 3 + [_const((1, d))] * 2 + [_const((8, 128))],
        out_shape=[jax.ShapeDtypeStruct((t, d), BF16), jax.ShapeDtypeStruct((t, d), BF16), jax.ShapeDtypeStruct((t, d), F32),
                   jax.ShapeDtypeStruct((1, d), F32), jax.ShapeDtypeStruct((1, d), F32), jax.ShapeDtypeStruct((8, 128), F32)],
        compiler_params=_params("arbitrary"))(xn, p, wpg, wpp, h, g_post, g_pre, target)


def ffn_bwd_a(dh, f, g_post, wd4, s4, t4, scale, name, deps=()):
    t, d = dh.shape
    nk, fk, _ = wd4.shape
    tm = _tile(t, 256)

    def body(dh_ref, f_ref, gp_ref, w_ref, s_ref, t_ref, df_ref, dg_ref, du_ref, dgp_ref):
        df, dgp = _rms_bwd(f_ref[...], gp_ref[...], dh_ref[...])
        df = (scale * df).astype(BF16)
        df_ref[...] = df
        _accumulate(dgp_ref, scale * dgp)
        for k in range(nk):
            da = _dot_nt(df, w_ref[k])
            du_ref[k] = (da * s_ref[k].astype(F32)).astype(BF16)
            dg_ref[k] = (da * t_ref[k].astype(F32)).astype(BF16)

    return pl.pallas_call(
        _ordered_after(body, 6, deps), name=name, grid=(t // tm,),
        in_specs=[_rows(tm, d), _rows(tm, d), _const((1, d)), _const((nk, fk, d)), _kmajor(nk, tm, fk), _kmajor(nk, tm, fk)]
        + [ANY_SPEC] * len(deps),
        out_specs=[_rows(tm, d), _kmajor(nk, tm, fk), _kmajor(nk, tm, fk), _const((1, d))],
        out_shape=[jax.ShapeDtypeStruct((t, d), BF16), jax.ShapeDtypeStruct((nk, t, fk), BF16),
                   jax.ShapeDtypeStruct((nk, t, fk), BF16), jax.ShapeDtypeStruct((1, d), F32)],
        compiler_params=_params("arbitrary"))(dh, f, g_post, wd4, s4, t4, *deps)


def dx_norm_bwd(pairs, h, g_pre, dh_in, name, deps=()):
    t, d = h.shape
    tm = _tile(t, 256)
    n = len(pairs)

    def body(*refs):
        dys, ws = refs[:n], refs[n:2 * n]
        h_ref, g_ref, dhi_ref, dho_ref, dg_ref = refs[2 * n:]
        acc = None
        for (_, w4, sections), dy_ref, w_ref in zip(pairs, dys, ws):
            if sections:
                wide = w4.shape[2]
                edges = sorted(set(range(0, 5 * d + 1, d)) | set(range(0, 5 * d + 1, wide)))
                parts = [_dot_nt(dy_ref[DZ_SLOT[lo // d], :, lo % d:lo % d + hi - lo], w_ref[lo // wide, :, lo % wide:lo % wide + hi - lo])
                         for lo, hi in zip(edges[:-1], edges[1:])]
            else:
                parts = [_dot(dy_ref[k], w_ref[k]) for k in range(w4.shape[0])]
            for part in parts:
                acc = part if acc is None else acc + part
        dx, dg = _rms_bwd(h_ref[...], g_ref[...], acc)
        dho_ref[...] = dhi_ref[...] + dx
        _accumulate(dg_ref, dg)

    dy_specs = [_kmajor(dy.shape[0], tm, dy.shape[2]) for dy, _, _ in pairs]
    return pl.pallas_call(
        _ordered_after(body, 2 * n + 3, deps), name=name, grid=(t // tm,),
        in_specs=dy_specs + [_const(w4.shape) for _, w4, _ in pairs] + [_rows(tm, d), _const((1, d)), _rows(tm, d)]
        + [ANY_SPEC] * len(deps),
        out_specs=[_rows(tm, d), _const((1, d))],
        out_shape=[jax.ShapeDtypeStruct((t, d), F32), jax.ShapeDtypeStruct((1, d), F32)],
        compiler_params=_params("arbitrary"))(*[dy for dy, _, _ in pairs], *[w4 for _, w4, _ in pairs], h, g_pre, dh_in, *deps)


def mixer_bwd_y(dh, m, g_post, w_o, ya, yb, z, woa, wob, name, deps=()):
    t, d = dh.shape
    tm = _tile(t, 256)

    def body(dh_ref, m_ref, gp_ref, wo_ref, ya_ref, yb_ref, ga_ref, gb_ref, wa_ref, wb_ref,
             dm_ref, dya_ref, dyb_ref, dz_ref, da_ref, db_ref, dgp_ref):
        dm, dgp = _rms_bwd(m_ref[...], gp_ref[...], dh_ref[...])
        dm = dm.astype(BF16)
        dm_ref[...] = dm
        _accumulate(dgp_ref, dgp)
        dy = _dot_nt(dm, wo_ref[...])
        sa = _sigmoid(ga_ref[...].astype(F32))
        sb = _sigmoid(gb_ref[...].astype(F32))
        dya = (dy * sa).astype(BF16)
        dyb = (dy * sb).astype(BF16)
        dya_ref[...] = dya
        dyb_ref[...] = dyb
        dz_ref[0] = (dy * ya_ref[...].astype(F32) * sa * (1.0 - sa)).astype(BF16)
        dz_ref[1] = (dy * yb_ref[...].astype(F32) * sb * (1.0 - sb)).astype(BF16)
        da_ref[...] = _dot_nt(dya, wa_ref[...]).astype(BF16)
        db_ref[...] = _dot_nt(dyb, wb_ref[...]).astype(BF16)

    return pl.pallas_call(
        _ordered_after(body, 10, deps), name=name, grid=(t // tm,),
        in_specs=[_rows(tm, d), _rows(tm, d), _const((1, d)), _const((d, d)), _rows(tm, d), _rows(tm, d),
                  _rows(tm, d, 3), _rows(tm, d, 4), _const((d, d)), _const((d, d))] + [ANY_SPEC] * len(deps),
        out_specs=[_rows(tm, d)] * 3 + [pl.BlockSpec((2, tm, d), lambda i: (0, i, 0))] + [_rows(tm, d)] * 2 + [_const((1, d))],
        out_shape=[jax.ShapeDtypeStruct((t, d), BF16)] * 3 + [jax.ShapeDtypeStruct((5, t, d), BF16)]
        + [jax.ShapeDtypeStruct((t, d), BF16)] * 2 + [jax.ShapeDtypeStruct((1, d), F32)],
        compiler_params=_params("arbitrary"))(dh, m, g_post, w_o, ya, yb, z, z, woa, wob, *deps)


def sgu_bwd(z, da, dz, norm_g, sgu_w, sgu_b3, name):
    t, d = da.shape
    ng = sgu_w.shape[0]
    dg = d // ng
    tm = _tile(t, 256)
    steps = t // tm

    def body(zu_ref, zv_ref, da_ref, ng_ref, w_ref, b_ref, _, dz_ref, dw_ref, db_ref, dng_ref, dvn_ref, dsv_ref):
        i = pl.program_id(0)
        zv = zv_ref[...].astype(F32)
        zu = zu_ref[...].astype(F32)
        v, gv = _gelu_and_grad(zv)
        vhat, r = _layernorm_parts(v)
        gain = ng_ref[...]
        vn = (vhat * gain).astype(BF16)
        u, gu = _gelu_and_grad(zu)
        dav = da_ref[...].astype(F32)
        mask = _causal_mask()

        @pl.when(i == 0)
        def _():
            dw_ref[...] = jnp.zeros_like(dw_ref)
            dsv_ref[...] = jnp.zeros_like(dsv_ref)

        for g in range(ng):
            wg = jnp.where(mask, w_ref[g], 0.0).astype(BF16)
            dw = jnp.zeros((CHUNK, CHUNK), F32)
            dsv_sum = jnp.zeros((CHUNK, dg), F32)
            for ci in range(tm // CHUNK):
                rs, cs = slice(ci * CHUNK, (ci + 1) * CHUNK), slice(g * dg, (g + 1) * dg)
                vn_blk = vn[rs, cs]
                sv = _dot(wg, vn_blk) + b_ref[g]
                dz_ref[0, rs, cs] = (dav[rs, cs] * sv * gu[rs, cs]).astype(BF16)
                dsv = dav[rs, cs] * u[rs, cs]
                dsv_sum += dsv
                dsv = dsv.astype(BF16)
                dw += _dot_nt(dsv, vn_blk)
                dvn_ref[rs, cs] = _dot_tn(wg, dsv)
            dw_ref[g] += dw
            dsv_ref[:, cs] += dsv_sum

        dvn = dvn_ref[...]
        _accumulate(dng_ref, jnp.sum(dvn * vhat, axis=0, keepdims=True))
        dvh = dvn * gain
        dv = r * (dvh - jnp.mean(dvh, axis=-1, keepdims=True) - vhat * jnp.mean(dvh * vhat, axis=-1, keepdims=True))
        dz_ref[1] = (dv * gv).astype(BF16)

        @pl.when(i == steps - 1)
        def _():
            for g in range(ng):
                dw_ref[g] = jnp.where(mask, dw_ref[g], 0.0)
                row_sum = jnp.sum(dsv_ref[:, g * dg:(g + 1) * dg], axis=1, keepdims=True)
                db_ref[g] = jnp.broadcast_to(row_sum, (CHUNK, CHUNK))

    return pl.pallas_call(
        body, name=name, grid=(steps,),
        in_specs=[_rows(tm, d, 0), _rows(tm, d, 1), _rows(tm, d), _const((1, d)), _const((ng, CHUNK, CHUNK)), _const((ng, CHUNK, 1)),
                  ANY_SPEC],
        out_specs=[pl.BlockSpec((2, tm, d), lambda i: (DZ_SLOT[0] // 2, i, 0)), _const((ng, CHUNK, CHUNK)),
                   _const((ng, CHUNK, CHUNK)), _const((1, d))],
        out_shape=[jax.ShapeDtypeStruct(dz.shape, BF16), jax.ShapeDtypeStruct((ng, CHUNK, CHUNK), F32),
                   jax.ShapeDtypeStruct((ng, CHUNK, CHUNK), F32), jax.ShapeDtypeStruct((1, d), F32)],
        scratch_shapes=[pltpu.VMEM((tm, d), F32), pltpu.VMEM((CHUNK, d), F32)], input_output_aliases={6: 0},
        compiler_params=_params("arbitrary"))(z, z, da, norm_g, sgu_w, sgu_b3, dz)


def pool_bwd(db, diff, dz, pool_w, pool_scale, name):
    t, d = db.shape
    ng = pool_w.shape[0]
    dg = d // ng
    tm = _tile(t, 256)
    per = tm // HALO
    steps = t // tm

    def body(db_ref, next_ref, diff_ref, w_ref, s_ref, _, dc_ref, dw_ref, ds_ref):
        i = pl.program_id(0)
        dbc = db_ref[...].astype(F32)
        nxt = jnp.where(i < steps - 1, next_ref[...].astype(F32), 0.0)
        ext = jnp.concatenate([dbc, nxt], axis=0)
        rows = tm + HALO
        tok = i * tm + lax.broadcasted_iota(jnp.int32, (rows, 1), 0)

        @pl.when(i == 0)
        def _():
            dw_ref[...] = jnp.zeros_like(dw_ref)
            ds_ref[...] = jnp.zeros_like(ds_ref)

        for g, win in enumerate(POOL_WINDOWS):
            cs = slice(g * dg, (g + 1) * dg)
            dp = (ext[:, cs] * s_ref[:, cs]).astype(BF16)
            dd = _dot_nt(dp, w_ref[g])
            s = dd / jnp.minimum(tok + 1, win).astype(F32)
            sh = 1
            while sh < win:
                s = s + pltpu.roll(s, rows - sh, 0)
                sh *= 2
            dc_ref[0, :, cs] = (s[:tm] - dd[:tm]).astype(BF16)
            dfg = diff_ref[:, cs]
            ds_ref[:, cs] += jnp.sum(dbc[:, cs] * _dot(dfg, w_ref[g]), axis=0, keepdims=True)
            dw_ref[g] += _dot_tn(dfg, dp[:tm])

    return pl.pallas_call(
        body, name=name, grid=(steps,),
        in_specs=[_rows(tm, d), pl.BlockSpec((HALO, d), lambda i: (jnp.minimum((i + 1) * per, t // HALO - 1), 0)),
                  _rows(tm, d), _const((ng, dg, dg)), _const((1, d)), ANY_SPEC],
        out_specs=[pl.BlockSpec((1, tm, d), lambda i: (DZ_SLOT[2], i, 0)), _const((ng, dg, dg)), _const((1, d))],
        out_shape=[jax.ShapeDtypeStruct(dz.shape, BF16), jax.ShapeDtypeStruct((ng, dg, dg), F32), jax.ShapeDtypeStruct((1, d), F32)],
        input_output_aliases={5: 0},
        compiler_params=_params("arbitrary"))(db, db, diff, pool_w, pool_scale, dz)


def dw_tn(x, dy, nk, name, x_kmajor=False, dy_mode="same"):
    t = x.shape[-2]
    kx = x.shape[-1]
    n = dy.shape[-1] // nk if dy_mode == "cols" else dy.shape[-1]
    tt = _tile(t, DW_TOKENS)
    steps = t // tt

    def body(x_ref, dy_ref, o_ref, acc_ref):
        s = pl.program_id(1)
        part = _dot_tn(x_ref[0] if x_kmajor else x_ref[...], dy_ref[0] if dy_mode == "kmajor" else dy_ref[...])
        if steps == 1:
            o_ref[0] = part.astype(BF16)
            return

        @pl.when(s == 0)
        def _():
            acc_ref[...] = jnp.zeros_like(acc_ref)

        acc_ref[...] += part

        @pl.when(s == steps - 1)
        def _():
            o_ref[0] = acc_ref[...].astype(BF16)

    x_spec = pl.BlockSpec((1, tt, kx), lambda k, s: (k, s, 0)) if x_kmajor else pl.BlockSpec((tt, kx), lambda k, s: (s, 0))
    dy_spec = {"kmajor": pl.BlockSpec((1, tt, n), lambda k, s: (k, s, 0)), "cols": pl.BlockSpec((tt, n), lambda k, s: (s, k)),
               "same": pl.BlockSpec((tt, n), lambda k, s: (s, 0))}[dy_mode]
    return pl.pallas_call(
        body, name=name, grid=(nk, steps), in_specs=[x_spec, dy_spec],
        out_specs=pl.BlockSpec((1, kx, n), lambda k, s: (k, 0, 0)), out_shape=jax.ShapeDtypeStruct((nk, kx, n), BF16),
        scratch_shapes=[pltpu.VMEM((kx, n) if steps > 1 else (8, 128), F32)],
        compiler_params=_params("parallel", "arbitrary"))(x, dy)


def dw_in_tiles(xn, dz, nk, name):
    t, d = xn.shape
    sections = len(DZ_SLOT)
    per_section = d // DW_IN_TILE
    per_shard = sections * per_section // nk

    def body(x_ref, dy_ref, o_ref):
        o_ref[0] = _dot_tn(x_ref[...], dy_ref[0]).astype(BF16)

    def slot(j):
        return (j // per_section + DZ_SLOT[0]) % sections

    return pl.pallas_call(
        body, name=name, grid=(sections * per_section,),
        in_specs=[pl.BlockSpec((t, d), lambda j: (0, 0)), pl.BlockSpec((1, t, DW_IN_TILE), lambda j: (slot(j), 0, j % per_section))],
        out_specs=pl.BlockSpec((1, d, DW_IN_TILE), lambda j: (j // per_shard, 0, j % per_shard)),
        out_shape=jax.ShapeDtypeStruct((nk, d, sections * d // nk), BF16), compiler_params=_params("parallel"))(xn, dz)


def _place():
    x, y, c = lax.axis_index("x"), lax.axis_index("y"), lax.axis_index("c")
    chips = [((1 - x) if fx else x, (1 - y) if fy else y) for fx, fy in OTHER_CHIPS]
    return x, y, c, chips


def _half(rows, which):
    return pl.ds(pl.multiple_of(which * (rows // 2), 16), rows // 2)


def _hbm(a):
    return pltpu.with_memory_space_constraint(a, pltpu.HBM)


def _by_shape(arrays):
    buckets = {}
    for i, a in enumerate(arrays):
        buckets.setdefault(a.shape, []).append(i)
    return list(buckets.values())


def cast_place(shards, chip, name, deps=()):
    n = len(shards)
    r, cdim = shards[0].shape
    tr = _tile(r, 256)

    def body(chip_ref, *refs):
        for w_ref, o_ref in zip(refs[:n], refs[n + len(deps):]):
            o_ref[0] = w_ref[...].astype(BF16)

    return pl.pallas_call(
        body, name=name,
        grid_spec=pltpu.PrefetchScalarGridSpec(
            num_scalar_prefetch=1, grid=(r // tr,),
            in_specs=[pl.BlockSpec((tr, cdim), lambda i, chip_ref: (i, 0))] * n + [ANY_SPEC] * len(deps),
            out_specs=[pl.BlockSpec((1, tr, cdim), lambda i, chip_ref: (chip_ref[0], i, 0))] * n),
        out_shape=[jax.ShapeDtypeStruct((N_CHIPS, r, cdim), BF16)] * n, compiler_params=_params("parallel"))(chip, *shards, *deps)


def _gather_copy(buf, sends, recvs, i, j, me, chip_xy, c):
    cx, cy = chip_xy
    mine = _half(buf.shape[1], c)
    return pltpu.make_async_remote_copy(
        src_ref=buf.at[me, mine], dst_ref=buf.at[me, mine], send_sem=sends.at[3 * i + j], recv_sem=recvs.at[3 * i + j],
        device_id=(cx, cy, c), device_id_type=MESH_IDS)


def allgather_start(bufs, name):
    n = len(bufs)

    def body(*refs):
        ins = refs[:n]
        sends, recvs = refs[n], refs[n + 1]
        token = refs[2 * n + 2]
        x, y, c, chips = _place()
        for i in range(n):
            for j, chip_xy in enumerate(chips):
                _gather_copy(ins[i], sends, recvs, i, j, 2 * x + y, chip_xy, c).start()
        token[...] = jnp.zeros_like(token)

    out = pl.pallas_call(
        body, name=name, in_specs=[HBM_SPEC] * n,
        out_specs=[SEM_SPEC, SEM_SPEC] + [HBM_SPEC] * n + [VMEM_SPEC],
        out_shape=[pltpu.SemaphoreType.DMA((3 * n,)), pltpu.SemaphoreType.DMA((3 * n,))]
        + [pltpu.HBM(b.shape, b.dtype) for b in bufs] + [jax.ShapeDtypeStruct((8, 128), F32)],
        input_output_aliases={i: i + 2 for i in range(n)},
        compiler_params=pltpu.CompilerParams(has_side_effects=DATAFLOW))(*[_hbm(b) for b in bufs])
    return out[0], out[1], list(out[2:2 + n]), out[2 + n]


def allgather_wait(sends, recvs, bufs, after, name):
    n = len(bufs)

    def body(*refs):
        ins = refs[:n]
        send_sems, recv_sems = refs[n], refs[n + 1]
        x, y, c, chips = _place()
        for i in range(n):
            for j, (cx, cy) in enumerate(chips):
                mine = _half(ins[i].shape[1], c)
                cp = pltpu.make_async_remote_copy(
                    src_ref=ins[i].at[2 * x + y, mine], dst_ref=ins[i].at[2 * cx + cy, mine], send_sem=send_sems.at[3 * i + j],
                    recv_sem=recv_sems.at[3 * i + j], device_id=(cx, cy, c), device_id_type=MESH_IDS)
                cp.wait_send()
                cp.wait_recv()

    return pl.pallas_call(
        body, name=name, in_specs=[HBM_SPEC] * n + [SEM_SPEC, SEM_SPEC, ANY_SPEC], out_specs=[HBM_SPEC] * n,
        out_shape=[pltpu.HBM(b.shape, b.dtype) for b in bufs], input_output_aliases={i: i for i in range(n)},
        compiler_params=pltpu.CompilerParams(has_side_effects=DATAFLOW))(*bufs, sends, recvs, after)


def d2d_forward(bufs, name):
    n = len(bufs)

    def body(*refs):
        ins = refs[:n]
        send_sems, recv_sems = refs[2 * n:]
        x, y, c, chips = _place()
        copies = []
        for i in range(n):
            mine = _half(ins[i].shape[1], c)
            for j, (cx, cy) in enumerate(chips):
                landed = ins[i].at[2 * cx + cy, mine]
                cp = pltpu.make_async_remote_copy(
                    src_ref=landed, dst_ref=landed, send_sem=send_sems.at[i, j], recv_sem=recv_sems.at[i, j],
                    device_id=(x, y, 1 - c), device_id_type=MESH_IDS)
                cp.start()
                copies.append(cp)
        for i in range(n):
            theirs = _half(ins[i].shape[1], 1 - c)
            for j, (cx, cy) in enumerate(chips):
                passed = ins[i].at[2 * cx + cy, theirs]
                pltpu.make_async_remote_copy(
                    src_ref=passed, dst_ref=passed, send_sem=send_sems.at[i, j], recv_sem=recv_sems.at[i, j],
                    device_id=(x, y, 1 - c), device_id_type=MESH_IDS).wait_recv()
        for cp in copies:
            cp.wait_send()

    return pl.pallas_call(
        body, name=name, in_specs=[HBM_SPEC] * n, out_specs=[HBM_SPEC] * n,
        out_shape=[jax.ShapeDtypeStruct(b.shape, b.dtype) for b in bufs], input_output_aliases={i: i for i in range(n)},
        scratch_shapes=[pltpu.SemaphoreType.DMA((n, 3))] * 2,
        compiler_params=pltpu.CompilerParams(has_side_effects=True))(*bufs)


def _sibling_copy(src, land, sends, recvs, i, x, y, c, halves):
    part = src.at[:, _half(src.shape[1], 1 - c)] if halves else src
    return pltpu.make_async_remote_copy(
        src_ref=part, dst_ref=land, send_sem=sends.at[i], recv_sem=recvs.at[i], device_id=(x, y, 1 - c),
        device_id_type=MESH_IDS)


def sibling_start(arrays, halves, name):
    n = len(arrays)
    lands = [lax.empty((a.shape[0], a.shape[1] // 2, a.shape[2]) if halves else a.shape, a.dtype) for a in arrays]

    def body(*refs):
        srcs, zones = refs[:n], refs[n:2 * n]
        sends, recvs = refs[2 * n], refs[2 * n + 1]
        token = refs[4 * n + 2]
        x, y, c, _ = _place()
        for i in range(n):
            _sibling_copy(srcs[i], zones[i], sends, recvs, i, x, y, c, halves).start()
        token[...] = jnp.zeros_like(token)

    out = pl.pallas_call(
        body, name=name, in_specs=[HBM_SPEC] * (2 * n),
        out_specs=[SEM_SPEC, SEM_SPEC] + [HBM_SPEC] * (2 * n) + [VMEM_SPEC],
        out_shape=[pltpu.SemaphoreType.DMA((n,)), pltpu.SemaphoreType.DMA((n,))]
        + [pltpu.HBM(a.shape, a.dtype) for a in arrays + lands] + [jax.ShapeDtypeStruct((8, 128), F32)],
        input_output_aliases={i: i + 2 for i in range(2 * n)},
        compiler_params=pltpu.CompilerParams(has_side_effects=DATAFLOW))(*[_hbm(a) for a in arrays + lands])
    return (out[0], out[1], list(out[2:2 + n]), list(out[2 + n:2 + 2 * n])), out[2 + 2 * n]


def sibling_wait(started, halves, after, name):
    sends, recvs, arrays, lands = started
    n = len(arrays)

    def body(*refs):
        srcs, zones = refs[:n], refs[n:2 * n]
        send_sems, recv_sems = refs[2 * n], refs[2 * n + 1]
        x, y, c, _ = _place()
        for i in range(n):
            cp = _sibling_copy(srcs[i], zones[i], send_sems, recv_sems, i, x, y, c, halves)
            cp.wait_send()
            cp.wait_recv()

    out = pl.pallas_call(
        body, name=name, in_specs=[HBM_SPEC] * (2 * n) + [SEM_SPEC, SEM_SPEC, ANY_SPEC], out_specs=[HBM_SPEC] * (2 * n),
        out_shape=[pltpu.HBM(a.shape, a.dtype) for a in arrays + lands], input_output_aliases={i: i for i in range(2 * n)},
        compiler_params=pltpu.CompilerParams(has_side_effects=DATAFLOW))(*arrays, *lands, sends, recvs, after)
    return list(out[:n]), list(out[n:])


def _scatter_copy(src, land, sends, recvs, i, j, chip_xy, c):
    cx, cy = chip_xy
    return pltpu.make_async_remote_copy(
        src_ref=src.at[2 * cx + cy], dst_ref=land.at[j], send_sem=sends.at[3 * i + j], recv_sem=recvs.at[3 * i + j],
        device_id=(cx, cy, c), device_id_type=MESH_IDS)


def scatter_start(sums, name):
    n = len(sums)
    lands = [lax.empty((3,) + s.shape[1:], s.dtype) for s in sums]

    def body(*refs):
        srcs, zones = refs[:n], refs[n:2 * n]
        sends, recvs = refs[2 * n], refs[2 * n + 1]
        token = refs[4 * n + 2]
        _, _, c, chips = _place()
        for i in range(n):
            for j, chip_xy in enumerate(chips):
                _scatter_copy(srcs[i], zones[i], sends, recvs, i, j, chip_xy, c).start()
        token[...] = jnp.zeros_like(token)

    out = pl.pallas_call(
        body, name=name, in_specs=[HBM_SPEC] * (2 * n),
        out_specs=[SEM_SPEC, SEM_SPEC] + [HBM_SPEC] * (2 * n) + [VMEM_SPEC],
        out_shape=[pltpu.SemaphoreType.DMA((3 * n,)), pltpu.SemaphoreType.DMA((3 * n,))]
        + [pltpu.HBM(a.shape, a.dtype) for a in sums + lands] + [jax.ShapeDtypeStruct((8, 128), F32)],
        input_output_aliases={i: i + 2 for i in range(2 * n)},
        compiler_params=pltpu.CompilerParams(has_side_effects=DATAFLOW))(*[_hbm(a) for a in sums + lands])
    return out[0], out[1], list(out[2:2 + n]), list(out[2 + n:2 + 2 * n]), out[2 + 2 * n]


def scatter_wait(sends, recvs, sums, lands, after, name):
    n = len(sums)

    def body(*refs):
        srcs, zones = refs[:n], refs[n:2 * n]
        send_sems, recv_sems = refs[2 * n], refs[2 * n + 1]
        _, _, c, chips = _place()
        for i in range(n):
            for j, chip_xy in enumerate(chips):
                cp = _scatter_copy(srcs[i], zones[i], send_sems, recv_sems, i, j, chip_xy, c)
                cp.wait_send()
                cp.wait_recv()

    out = pl.pallas_call(
        body, name=name, in_specs=[HBM_SPEC] * (2 * n) + [SEM_SPEC, SEM_SPEC, ANY_SPEC], out_specs=[HBM_SPEC] * (2 * n),
        out_shape=[pltpu.HBM(a.shape, a.dtype) for a in sums + lands], input_output_aliases={i: i for i in range(2 * n)},
        compiler_params=pltpu.CompilerParams(has_side_effects=DATAFLOW))(*sums, *lands, sends, recvs, after)
    return list(out[:n]), list(out[n:])


def add_halves(grads, recvs, core, name):
    n = len(grads)
    nk, r, cdim = grads[0].shape
    half = r // 2
    views = [g.reshape(nk, 2, half, cdim) for g in grads]

    def body(core_ref, *refs):
        for g_ref, r_ref, o_ref in zip(refs[:n], refs[n:2 * n], refs[2 * n:]):
            o_ref[0] = (g_ref[0, 0].astype(F32) + r_ref[0].astype(F32)).astype(BF16)

    return pl.pallas_call(
        body, name=name,
        grid_spec=pltpu.PrefetchScalarGridSpec(
            num_scalar_prefetch=1, grid=(nk,),
            in_specs=[pl.BlockSpec((1, 1, half, cdim), lambda k, core_ref: (k, core_ref[0], 0, 0))] * n
            + [pl.BlockSpec((1, half, cdim), lambda k, core_ref: (k, 0, 0))] * n,
            out_specs=[pl.BlockSpec((1, half, cdim), lambda k, core_ref: (k, 0, 0))] * n),
        out_shape=[jax.ShapeDtypeStruct((nk, half, cdim), BF16)] * n, compiler_params=_params("parallel"))(core, *views, *recvs)


def add_chips(sums, lands, chip, name):
    n = len(sums)
    _, half, cdim = sums[0].shape
    tr = _tile(half, 128)

    def body(chip_ref, *refs):
        for s_ref, r_ref, o_ref in zip(refs[:n], refs[n:2 * n], refs[2 * n:]):
            o_ref[...] = ((s_ref[0].astype(F32) + r_ref[0].astype(F32)) + r_ref[1].astype(F32)) + r_ref[2].astype(F32)

    return pl.pallas_call(
        body, name=name,
        grid_spec=pltpu.PrefetchScalarGridSpec(
            num_scalar_prefetch=1, grid=(half // tr,),
            in_specs=[pl.BlockSpec((1, tr, cdim), lambda i, chip_ref: (chip_ref[0], i, 0))] * n
            + [pl.BlockSpec((3, tr, cdim), lambda i, chip_ref: (0, i, 0))] * n,
            out_specs=[pl.BlockSpec((tr, cdim), lambda i, chip_ref: (i, 0))] * n),
        out_shape=[jax.ShapeDtypeStruct((half, cdim), F32)] * n, compiler_params=_params("parallel"))(chip, *sums, *lands)


def _adamw_math(w, g, m, v):
    m = ADAM_B1 * m + (1.0 - ADAM_B1) * g
    v = ADAM_B2 * v + (1.0 - ADAM_B2) * (g * g)
    m_hat = m / (1.0 - ADAM_B1 ** ADAM_STEP)
    v_hat = v / (1.0 - ADAM_B2 ** ADAM_STEP)
    return -ADAM_LR * (m_hat / (jnp.sqrt(v_hat) + ADAM_EPS) + ADAM_WD * w), m, v


def adamw_halves(ws, owns, others, ms, vs, core, name):
    n = len(ws)
    r, cdim = ws[0].shape
    half = r // 2
    tr = _tile(half, max(8, 256 // n))
    steps = half // tr

    def body(core_ref, *refs):
        ins, outs = refs[:5 * n], refs[5 * n:]
        for a in range(n):
            w_ref, own_ref, other_ref, m_ref, v_ref = ins[a::n]
            g = jnp.where(pl.program_id(0) == core_ref[0], own_ref[...], other_ref[...])
            outs[a][...] = g
            outs[n + a][...], outs[2 * n + a][...], outs[3 * n + a][...] = _adamw_math(w_ref[...], g, m_ref[...], v_ref[...])

    whole = pl.BlockSpec((tr, cdim), lambda hh, i, core_ref: (hh * steps + i, 0))
    own = pl.BlockSpec((tr, cdim), lambda hh, i, core_ref: (jnp.where(hh == core_ref[0], i, 0), 0))
    other = pl.BlockSpec((tr, cdim), lambda hh, i, core_ref: (jnp.where(hh == core_ref[0], 0, i), 0))
    out = pl.pallas_call(
        body, name=name,
        grid_spec=pltpu.PrefetchScalarGridSpec(
            num_scalar_prefetch=1, grid=(2, steps), in_specs=[whole] * n + [own] * n + [other] * n + [whole] * (2 * n),
            out_specs=[whole] * (4 * n)),
        out_shape=[jax.ShapeDtypeStruct((r, cdim), F32)] * (4 * n),
        compiler_params=_params("parallel", "parallel"))(core, *ws, *owns, *others, *ms, *vs)
    return [out[a::n] for a in range(n)]


def small_allreduce_adamw(g, w, m, v):
    r, cdim = g.shape

    def body(g_ref, w_ref, m_ref, v_ref, go_ref, d_ref, mo_ref, vo_ref, pair, quad, d2d_send, d2d_recv, ici_send, ici_recv):
        x, y, c, chips = _place()
        me = 2 * x + y
        pair[c] = g_ref[...]
        swap = pltpu.make_async_remote_copy(
            src_ref=g_ref, dst_ref=pair.at[c], send_sem=d2d_send, recv_sem=d2d_recv, device_id=(x, y, 1 - c),
            device_id_type=MESH_IDS)
        swap.start()
        swap.wait()
        quad[me] = pair[0] + pair[1]
        copies = []
        for j, (cx, cy) in enumerate(chips):
            cp = pltpu.make_async_remote_copy(
                src_ref=quad.at[me], dst_ref=quad.at[me], send_sem=ici_send.at[j], recv_sem=ici_recv.at[j],
                device_id=(cx, cy, c), device_id_type=MESH_IDS)
            cp.start()
            copies.append(cp)
        for j, (cx, cy) in enumerate(chips):
            slot = quad.at[2 * cx + cy]
            pltpu.make_async_remote_copy(
                src_ref=slot, dst_ref=slot, send_sem=ici_send.at[j], recv_sem=ici_recv.at[j], device_id=(cx, cy, c),
                device_id_type=MESH_IDS).wait_recv()
        for cp in copies:
            cp.wait_send()
        total = (quad[0] + quad[1]) + (quad[2] + quad[3])
        go_ref[...] = total
        d_ref[...], mo_ref[...], vo_ref[...] = _adamw_math(w_ref[...], total, m_ref[...], v_ref[...])

    return pl.pallas_call(
        body, name="small_allreduce_adamw", in_specs=[VMEM_SPEC] * 4, out_specs=[VMEM_SPEC] * 4,
        out_shape=[jax.ShapeDtypeStruct((r, cdim), F32)] * 4,
        scratch_shapes=[pltpu.VMEM((2, r, cdim), F32), pltpu.VMEM((N_CHIPS, r, cdim), F32), pltpu.SemaphoreType.DMA,
                        pltpu.SemaphoreType.DMA, pltpu.SemaphoreType.DMA((3,)), pltpu.SemaphoreType.DMA((3,))],
        compiler_params=pltpu.CompilerParams(has_side_effects=True, vmem_limit_bytes=VMEM_LIMIT_V7X))(g, w, m, v)


GROUPS = (
    ("ffn1", ("ffn1_w_gate", "ffn1_w_up", "ffn1_w_down")),
    ("mixer", ("w_in", "pool_w", "w_out_a", "w_out_b", "w_o")),
    ("ffn2", ("ffn2_w_gate", "ffn2_w_up", "ffn2_w_down")),
    ("ple", ("ple_w_gate", "ple_w_proj")),
)
GATHERS = (
    ("ffn1_in", ("ffn1_w_gate", "ffn1_w_up")),
    ("ffn1_out", ("ffn1_w_down",)),
) + GROUPS[1:]
GAINS = ("ffn1_pre_g", "ffn1_post_g", "mix_pre_g", "sgu_norm_g", "pool_scale", "mix_post_g",
         "ffn2_pre_g", "ffn2_post_g", "ple_pre_g", "ple_post_g")
SMALL = GAINS + ("sgu_b", "sgu_w")
PACKED = SMALL + ("loss",)
WEIGHTS = ("ffn1_pre_g", "ffn1_w_gate", "ffn1_w_up", "ffn1_w_down", "ffn1_post_g", "mix_pre_g", "w_in", "sgu_norm_g",
           "sgu_w", "sgu_b", "pool_w", "pool_scale", "w_out_a", "w_out_b", "w_o", "mix_post_g", "ffn2_pre_g",
           "ffn2_w_gate", "ffn2_w_up", "ffn2_w_down", "ffn2_post_g", "ple_pre_g", "ple_w_gate", "ple_w_proj", "ple_post_g")
PACK_ROWS = 16


TRANSPOSED = ("ffn1_w_gate", "ffn1_w_up", "ffn2_w_gate", "ffn2_w_up")


def _shard2d(name, a):
    a = a[0]
    return a.T if name in TRANSPOSED else a.reshape(-1, a.shape[-1])


def _unshard2d(name, a2d, shape):
    return (a2d.T if name in TRANSPOSED else a2d).reshape(shape)


def pack_rows(gains, sgu_b, loss_tile, name):
    n = len(gains)
    d = gains[0].shape[1]
    g = sgu_b.shape[0]

    def body(*refs):
        o_ref = refs[-1]
        o_ref[...] = jnp.zeros_like(o_ref)
        for i in range(n):
            o_ref[i:i + 1, :] = refs[i][...]
        o_ref[n:n + g, 0:CHUNK] = refs[n][...]
        o_ref[n + g:n + g + 1, 0:CHUNK] = refs[n + 1][0:1, :]

    return pl.pallas_call(
        body, name=name, in_specs=[VMEM_SPEC] * (n + 2), out_specs=VMEM_SPEC,
        out_shape=jax.ShapeDtypeStruct((PACK_ROWS, d), F32))(*gains, sgu_b, loss_tile)


def _pack_small(parts, tag):
    d = parts[GAINS[0]].shape[-1]
    rows = pack_rows([parts[n] for n in GAINS], parts["sgu_b"].reshape(-1, CHUNK), parts["loss"], "pack_" + tag)
    return jnp.concatenate([rows, parts["sgu_w"].reshape(-1, d)], axis=0)


def _unpack_small(packed, like):
    n, g = len(GAINS), like["sgu_b"].size // CHUNK
    out = {name: packed[i:i + 1] for i, name in enumerate(GAINS)}
    out["sgu_b"] = packed[n:n + g, :CHUNK].reshape(like["sgu_b"].shape)
    out["loss"] = packed[n + g, 0]
    out["sgu_w"] = packed[PACK_ROWS:].reshape(like["sgu_w"].shape)
    return out


def _ffn_fwd(xn, h, w, pre, g_post, g_next, tag, late_weights=None):
    a4, s4, t4 = ffn_gu(xn, w[pre + "w_gate"], w[pre + "w_up"], tag + "_gu")
    if late_weights is not None:
        w.update(late_weights(a4))
    f, h_new, xn_next = mm_norm_res(a4, w[pre + "w_down"], h, g_post, g_next, 0.5, tag + "_down")
    return dict(xn=xn, h=h, a4=a4, s4=s4, t4=t4, f=f), h_new, xn_next


def _ffn_bwd_w(dh, saved, w, pre, g_post, tag, deps):
    df, dg4, du4, d_post = ffn_bwd_a(dh, saved["f"], g_post, w[pre + "w_down"], saved["s4"], saved["t4"], 0.5, tag + "_bwd_a",
                                     deps=deps)
    nk = N_CHIPS
    grads = {
        pre + "w_down": dw_tn(saved["a4"], df, nk, tag + "_dw_down", x_kmajor=True, dy_mode="same"),
        pre + "w_gate": dw_tn(dg4, saved["xn"], nk, tag + "_dw_gate", x_kmajor=True, dy_mode="same"),
        pre + "w_up": dw_tn(du4, saved["xn"], nk, tag + "_dw_up", x_kmajor=True, dy_mode="same"),
    }
    return dg4, du4, grads, d_post


def _ffn_bwd_x(dh, dg4, du4, saved, w, pre, g_pre, tag, deps):
    return dx_norm_bwd([(dg4, w[pre + "w_gate"], False), (du4, w[pre + "w_up"], False)], saved["h"], g_pre, dh,
                       tag + "_bwd_x", deps=deps)


def _gather_group(names, weights, chip, tag, deps):
    shards = [_shard2d(n, weights[n]) for n in names]
    bufs = [None] * len(names)
    for b, idx in enumerate(_by_shape(shards)):
        for i, buf in zip(idx, cast_place([shards[i] for i in idx], chip, f"cast_{tag}_{b}", deps=deps)):
            bufs[i] = buf
    sends, recvs, bufs, token = allgather_start(bufs, "allgather_start_" + tag)
    return (sends, recvs, bufs), token


def _gathered(started, names, after, tag):
    sends, recvs, bufs = started
    landed = allgather_wait(sends, recvs, bufs, after, "allgather_wait_" + tag)
    return dict(zip(names, d2d_forward(landed, "d2d_forward_" + tag)))


def _exchange_start(names, big_g, tag):
    return sibling_start([big_g[n] for n in names], True, "exchange_start_" + tag)


def _scatter_begin(names, exchanging, after, core, tag):
    partial, from_sibling = sibling_wait(exchanging, True, after, "exchange_wait_" + tag)
    chip_sums = [None] * len(names)
    for b, idx in enumerate(_by_shape(partial)):
        for i, s in zip(idx, add_halves([partial[i] for i in idx], [from_sibling[i] for i in idx], core, f"add_halves_{tag}_{b}")):
            chip_sums[i] = s
    sends, recvs, sums, lands, token = scatter_start(chip_sums, "scatter_start_" + tag)
    return (names, sends, recvs, sums, lands), token


def _share_begin(scattering, after, chip, tag):
    names, sends, recvs, sums, lands = scattering
    sums, lands = scatter_wait(sends, recvs, sums, lands, after, "scatter_wait_" + tag)
    reduced = [None] * len(names)
    for b, idx in enumerate(_by_shape(sums)):
        for i, r in zip(idx, add_chips([sums[i] for i in idx], [lands[i] for i in idx], chip, f"add_chips_{tag}_{b}")):
            reduced[i] = r
    sharing, token = sibling_start(reduced, False, "share_start_" + tag)
    return (names, sharing), token


def _update(shared, after, weights, moments_m, moments_v, core, tag, results):
    names, sharing = shared
    reduced, others = sibling_wait(sharing, False, after, "share_wait_" + tag)
    last = None
    for b, idx in enumerate(_by_shape(reduced)):
        in_bucket = [names[i] for i in idx]
        outs = adamw_halves([_shard2d(n, weights[n]) for n in in_bucket], [reduced[i] for i in idx], [others[i] for i in idx],
                            [_shard2d(n, moments_m[n]) for n in in_bucket], [_shard2d(n, moments_v[n]) for n in in_bucket],
                            core, f"adamw_{tag}_{b}")
        for n, per_weight in zip(in_bucket, outs):
            for store, value in zip(results, per_weight):
                store[n] = _unshard2d(n, value, weights[n].shape)
            last = per_weight[1]
    return last


def kernel(x, p, ffn1_pre_g, ffn1_w_gate, ffn1_w_up, ffn1_w_down, ffn1_post_g, mix_pre_g, w_in, sgu_norm_g, sgu_w, sgu_b, pool_w, pool_scale, w_out_a, w_out_b, w_o, mix_post_g, ffn2_pre_g, ffn2_w_gate, ffn2_w_up, ffn2_w_down, ffn2_post_g, ple_pre_g, ple_w_gate, ple_w_proj, ple_post_g, loss_target, m_ffn1_pre_g, m_ffn1_w_gate, m_ffn1_w_up, m_ffn1_w_down, m_ffn1_post_g, m_mix_pre_g, m_w_in, m_sgu_norm_g, m_sgu_w, m_sgu_b, m_pool_w, m_pool_scale, m_w_out_a, m_w_out_b, m_w_o, m_mix_post_g, m_ffn2_pre_g, m_ffn2_w_gate, m_ffn2_w_up, m_ffn2_w_down, m_ffn2_post_g, m_ple_pre_g, m_ple_w_gate, m_ple_w_proj, m_ple_post_g, v_ffn1_pre_g, v_ffn1_w_gate, v_ffn1_w_up, v_ffn1_w_down, v_ffn1_post_g, v_mix_pre_g, v_w_in, v_sgu_norm_g, v_sgu_w, v_sgu_b, v_pool_w, v_pool_scale, v_w_out_a, v_w_out_b, v_w_o, v_mix_post_g, v_ffn2_pre_g, v_ffn2_w_gate, v_ffn2_w_up, v_ffn2_w_down, v_ffn2_post_g, v_ple_pre_g, v_ple_w_gate, v_ple_w_proj, v_ple_post_g):
    given = dict(locals())
    weights = {n: given[n] for n in WEIGHTS}
    moments_m = {n: given["m_" + n] for n in WEIGHTS}
    moments_v = {n: given["v_" + n] for n in WEIGHTS}
    core = lax.axis_index("c").astype(jnp.int32).reshape(1)
    chip = (2 * lax.axis_index("x") + lax.axis_index("y")).astype(jnp.int32).reshape(1)

    d = x.shape[-1]
    token = ()
    gathering = {}
    for tag, names in GATHERS:
        gathering[tag], tok = _gather_group(names, weights, chip, tag, token)
        token = (tok,)
    gain = {n: weights[n] for n in GAINS}
    sgu_w3 = sgu_w[0]
    sgu_b3 = sgu_b[0][:, :, None]
    groups = dict(GROUPS + GATHERS)

    h0 = x[0]
    tgt = loss_target[0]
    p_bf = p[0, 0].astype(BF16)
    w = _gathered(gathering["ffn1_in"], groups["ffn1_in"], token[0], "ffn1_in")
    xn1 = rms_cast(h0, gain["ffn1_pre_g"], "ffn1_pre_norm")
    s1, h1, xn2 = _ffn_fwd(xn1, h0, w, "ffn1_", gain["ffn1_post_g"], gain["mix_pre_g"], "ffn1",
                           lambda a4: _gathered(gathering["ffn1_out"], groups["ffn1_out"], a4, "ffn1_out"))
    w.update(_gathered(gathering["mixer"], groups["mixer"], h1, "mixer"))
    full = {n: w[n].reshape(-1, d) for n in ("w_out_a", "w_out_b", "w_o")}
    n_groups = pool_w.shape[1]
    rows_per = pool_w.shape[2]
    dgp = pool_w.shape[3]
    pool_full = w["pool_w"].reshape(N_CHIPS, n_groups, rows_per, dgp).transpose(1, 0, 2, 3).reshape(n_groups, N_CHIPS * rows_per, dgp)
    z = mixer_in(xn2, w["w_in"], "mixer_in")
    a = sgu_fwd(z, gain["sgu_norm_g"], sgu_w3, sgu_b3, "sgu_fwd")
    diff, b = pool_fwd(z, pool_full, gain["pool_scale"], "pool_fwd")
    ya, yb, y = mixer_y(a, b, z, full["w_out_a"], full["w_out_b"], "mixer_y")
    m, h2, xn3 = mm_norm_res(y[None], full["w_o"][None], h1, gain["mix_post_g"], gain["ffn2_pre_g"], 1.0, "mixer_out")
    w.update(_gathered(gathering["ffn2"], groups["ffn2"], h2, "ffn2"))
    s2, h3, xn4 = _ffn_fwd(xn3, h2, w, "ffn2_", gain["ffn2_post_g"], gain["ple_pre_g"], "ffn2")
    w.update(_gathered(gathering["ple"], groups["ple"], h3, "ple"))
    full["ple_w_gate"] = w["ple_w_gate"].reshape(-1, d)
    proj_full = w["ple_w_proj"].transpose(1, 0, 2).reshape(ple_w_proj.shape[1], -1)

    small_g, big_g = {}, {}
    ds, de, dh3, small_g["ple_post_g"], small_g["ple_pre_g"], loss_part = ple_loss(
        xn4, p_bf, full["ple_w_gate"], proj_full, h3, gain["ple_post_g"], gain["ple_pre_g"], tgt, "ple_loss")
    small_g["loss"] = loss_part
    grad, delta, new_m, new_v = {}, {}, {}, {}
    results = (grad, delta, new_m, new_v)
    update = functools.partial(_update, weights=weights, moments_m=moments_m, moments_v=moments_v, core=core, results=results)

    def reduce_behind(tag, previous):
        exchanging, tok = _exchange_start(groups[tag], big_g, tag)
        if previous is not None:
            shared, tok = _share_begin(previous[1], tok, chip, previous[0])
        scattering, tok_scatter = _scatter_begin(groups[tag], exchanging, tok, core, tag)
        done = () if previous is None else (update(shared, tok_scatter, tag=previous[0]),)
        return (tag, scattering), (tok_scatter,) + done

    big_g["ple_w_gate"] = dw_tn(xn4, ds, 1, "dw_ple_gate").reshape(N_CHIPS, -1, d)
    big_g["ple_w_proj"] = dw_tn(p_bf, de, N_CHIPS, "dw_ple_proj", dy_mode="cols")
    reducing, deps = reduce_behind("ple", None)

    dg4, du4, g2, small_g["ffn2_post_g"] = _ffn_bwd_w(dh3, s2, w, "ffn2_", gain["ffn2_post_g"], "ffn2", deps)
    big_g.update(g2)
    dh2, small_g["ffn2_pre_g"] = _ffn_bwd_x(dh3, dg4, du4, s2, w, "ffn2_", gain["ffn2_pre_g"], "ffn2", ())
    reducing, deps = reduce_behind("ffn2", reducing)

    dm, dya, dyb, dz, da, db, small_g["mix_post_g"] = mixer_bwd_y(
        dh2, m, gain["mix_post_g"], full["w_o"], ya, yb, z, full["w_out_a"], full["w_out_b"], "mixer_bwd_y", deps=deps)
    big_g["w_o"] = dw_tn(y, dm, 1, "dw_o").reshape(N_CHIPS, -1, d)
    big_g["w_out_a"] = dw_tn(a, dya, 1, "dw_out_a").reshape(N_CHIPS, -1, d)
    big_g["w_out_b"] = dw_tn(b, dyb, 1, "dw_out_b").reshape(N_CHIPS, -1, d)
    dz, d_sgu_w, d_sgu_b, small_g["sgu_norm_g"] = sgu_bwd(z, da, dz, gain["sgu_norm_g"], sgu_w3, sgu_b3, "sgu_bwd")
    dz, d_pool_w, small_g["pool_scale"] = pool_bwd(db, diff, dz, pool_full, gain["pool_scale"], "pool_bwd")
    big_g["pool_w"] = d_pool_w.astype(BF16).reshape(n_groups, N_CHIPS, rows_per, dgp).transpose(1, 0, 2, 3).reshape(
        N_CHIPS, n_groups * rows_per, dgp)
    big_g["w_in"] = dw_in_tiles(xn2, dz, N_CHIPS, "dw_in")
    dh1, small_g["mix_pre_g"] = dx_norm_bwd([(dz, w["w_in"], True)], h1, gain["mix_pre_g"], dh2, "mixer_bwd_x")
    reducing, deps = reduce_behind("mixer", reducing)

    dg4, du4, g1, small_g["ffn1_post_g"] = _ffn_bwd_w(dh1, s1, w, "ffn1_", gain["ffn1_post_g"], "ffn1", deps)
    big_g.update(g1)
    reducing, deps = reduce_behind("ffn1", reducing)
    dh0, small_g["ffn1_pre_g"] = _ffn_bwd_x(dh1, dg4, du4, s1, w, "ffn1_", gain["ffn1_pre_g"], "ffn1", deps)
    shared, tok = _share_begin(reducing[1], dh0, chip, "ffn1")
    update(shared, tok, tag="ffn1")
    small_g["sgu_w"] = d_sgu_w
    small_g["sgu_b"] = d_sgu_b[:, :, 0]

    no_state = {"loss": jnp.zeros((8, 128), F32)}
    packed = small_allreduce_adamw(
        _pack_small(small_g, "grads"), _pack_small({n: weights[n] for n in SMALL} | no_state, "weights"),
        _pack_small({n: moments_m[n] for n in SMALL} | no_state, "m"), _pack_small({n: moments_v[n] for n in SMALL} | no_state, "v"))
    like = {n: weights[n] for n in SMALL}
    for store, block in zip((grad, delta, new_m, new_v), packed):
        store.update(_unpack_small(block, like))

    return (grad["loss"], dh0[None], *[grad[n] for n in WEIGHTS], *[delta[n] for n in WEIGHTS],
            *[new_m[n] for n in WEIGHTS], *[new_v[n] for n in WEIGHTS])
```

```python
import functools

import jax
import jax.numpy as jnp
from jax import lax
from jax.experimental import pallas as pl
from jax.experimental.pallas import tpu as pltpu

F32 = jnp.float32
BF16 = jnp.bfloat16
EPS = 1e-6
CHUNK = 128
POOL_WINDOWS = (2, 4, 8, 16)
HALO = 16
N_CHIPS = 4
ADAM_LR, ADAM_B1, ADAM_B2, ADAM_EPS, ADAM_WD, ADAM_STEP = 0.001, 0.9, 0.999, 1e-08, 0.01, 10
VMEM_LIMIT_V7X = 58 * 1024 * 1024
MESH_IDS = pl.DeviceIdType.MESH
HBM_SPEC = pl.BlockSpec(memory_space=pltpu.HBM)
VMEM_SPEC = pl.BlockSpec(memory_space=pltpu.VMEM)
SEM_SPEC = pl.BlockSpec(memory_space=pltpu.SEMAPHORE)
ANY_SPEC = pl.BlockSpec(memory_space=pl.ANY)
DATAFLOW = pltpu.SideEffectType.DATAFLOW_SIDE_EFFECTING
OTHER_CHIPS = ((1, 0), (0, 1), (1, 1))
DZ_SLOT = (2, 3, 4, 0, 1)
DW_TOKENS = 4096
DW_IN_TILE = 256

NT = (((1,), (1,)), ((), ()))
TN = (((0,), (0,)), ((), ()))


def _params(*sem, **more):
    return pltpu.CompilerParams(dimension_semantics=sem or None, vmem_limit_bytes=VMEM_LIMIT_V7X, **more)


def _tile(t, want):
    return max(c for c in range(8, min(t, want) + 1, 8) if t % c == 0)


def _const(shape):
    return pl.BlockSpec(shape, lambda *_: (0,) * len(shape))


def _rows(tm, d, col=0):
    return pl.BlockSpec((tm, d), lambda i: (i, col))


def _kmajor(nk, tm, kb):
    return pl.BlockSpec((nk, tm, kb), lambda i: (0, i, 0))


def _dot(a, b):
    return jnp.dot(a, b, preferred_element_type=F32)


def _dot_nt(a, b):
    return lax.dot_general(a, b, NT, preferred_element_type=F32)


def _dot_tn(a, b):
    return lax.dot_general(a, b, TN, preferred_element_type=F32)


def _gelu(x):
    return 0.5 * x * (1.0 + jnp.tanh(0.7978845608028654 * (x + 0.044715 * x * x * x)))


def _gelu_and_grad(x):
    x2 = x * x
    th = jnp.tanh(0.7978845608028654 * (x + 0.044715 * x * x2))
    cdf = 0.5 * (1.0 + th)
    return x * cdf, cdf + 0.5 * x * (1.0 - th * th) * 0.7978845608028654 * (1.0 + 3.0 * 0.044715 * x2)


def _sigmoid(x):
    return 1.0 / (1.0 + jnp.exp(-x))


def _rstd(h):
    return lax.rsqrt(jnp.mean(h * h, axis=-1, keepdims=True) + EPS)


def _rms_bwd(h, g, dy):
    r = _rstd(h)
    t = dy * g
    dh = r * t - h * (r * r * r) * jnp.mean(h * t, axis=-1, keepdims=True)
    return dh, jnp.sum(dy * h * r, axis=0, keepdims=True)


def _ordered_after(body, n_in, deps):
    if not deps:
        return body
    return lambda *refs: body(*refs[:n_in], *refs[n_in + len(deps):])


def _accumulate(ref, value):
    @pl.when(pl.program_id(0) == 0)
    def _():
        ref[...] = jnp.zeros_like(ref)

    ref[...] += value


def rms_cast(h, g, name):
    t, d = h.shape
    tm = _tile(t, 512)

    def body(h_ref, g_ref, o_ref):
        hv = h_ref[...]
        o_ref[...] = (hv * _rstd(hv) * g_ref[...]).astype(BF16)

    return pl.pallas_call(
        body, name=name, grid=(t // tm,), in_specs=[_rows(tm, d), _const((1, d))], out_specs=_rows(tm, d),
        out_shape=jax.ShapeDtypeStruct((t, d), BF16), compiler_params=_params("parallel"))(h, g)


def ffn_gu(xn, wgt4, wut4, name):
    t, d = xn.shape
    nk, fk, _ = wgt4.shape
    tm = _tile(t, 512)

    def body(x_ref, wg_ref, wu_ref, a_ref, s_ref, t_ref):
        xv = x_ref[...]
        g = _dot_nt(xv, wg_ref[0])
        u = _dot_nt(xv, wu_ref[0])
        sg = _sigmoid(g)
        s = g * sg
        a_ref[0] = (s * u).astype(BF16)
        s_ref[0] = s.astype(BF16)
        t_ref[0] = (u * sg * (1.0 + g * (1.0 - sg))).astype(BF16)

    w_spec = pl.BlockSpec((1, fk, d), lambda k, i: (k, 0, 0))
    o_spec = pl.BlockSpec((1, tm, fk), lambda k, i: (k, i, 0))
    shape = jax.ShapeDtypeStruct((nk, t, fk), BF16)
    return pl.pallas_call(
        body, name=name, grid=(nk, t // tm), in_specs=[pl.BlockSpec((tm, d), lambda k, i: (i, 0)), w_spec, w_spec],
        out_specs=[o_spec] * 3, out_shape=[shape] * 3, compiler_params=_params("parallel", "parallel"))(xn, wgt4, wut4)


def mm_norm_res(a3, w3, h_old, g_post, g_next, scale, name, deps=()):
    nk, t, kb = a3.shape
    d = w3.shape[2]
    tm = _tile(t, 256)

    def body(a_ref, w_ref, h_ref, gp_ref, gn_ref, f_ref, hn_ref, xn_ref):
        f = _dot(a_ref[0], w_ref[0])
        for k in range(1, nk):
            f += _dot(a_ref[k], w_ref[k])
        f_ref[...] = f
        hn = h_ref[...] + scale * (f * _rstd(f) * gp_ref[...])
        hn_ref[...] = hn
        xn_ref[...] = (hn * _rstd(hn) * gn_ref[...]).astype(BF16)

    return pl.pallas_call(
        _ordered_after(body, 5, deps), name=name, grid=(t // tm,),
        in_specs=[_kmajor(nk, tm, kb), _const((nk, kb, d)), _rows(tm, d), _const((1, d)), _const((1, d))] + [ANY_SPEC] * len(deps),
        out_specs=[_rows(tm, d)] * 3,
        out_shape=[jax.ShapeDtypeStruct((t, d), F32), jax.ShapeDtypeStruct((t, d), F32), jax.ShapeDtypeStruct((t, d), BF16)],
        compiler_params=_params("parallel"))(a3, w3, h_old, g_post, g_next, *deps)


def mixer_in(xn, win4, name):
    t, d = xn.shape
    nk, _, nb = win4.shape
    tm = _tile(t, 512)

    def body(x_ref, w_ref, z_ref):
        z_ref[...] = _dot(x_ref[...], w_ref[0]).astype(BF16)

    return pl.pallas_call(
        body, name=name, grid=(nk, t // tm),
        in_specs=[pl.BlockSpec((tm, d), lambda k, i: (i, 0)), pl.BlockSpec((1, d, nb), lambda k, i: (k, 0, 0))],
        out_specs=pl.BlockSpec((tm, nb), lambda k, i: (i, k)), out_shape=jax.ShapeDtypeStruct((t, nk * nb), BF16),
        compiler_params=_params("parallel", "parallel"))(xn, win4)


def _causal_mask():
    row = lax.broadcasted_iota(jnp.int32, (CHUNK, CHUNK), 0)
    col = lax.broadcasted_iota(jnp.int32, (CHUNK, CHUNK), 1)
    return row >= col


def _layernorm_parts(v):
    mu = jnp.mean(v, axis=-1, keepdims=True)
    vc = v - mu
    r = lax.rsqrt(jnp.mean(vc * vc, axis=-1, keepdims=True) + EPS)
    return vc * r, r


def sgu_fwd(z, norm_g, sgu_w, sgu_b3, name):
    t = z.shape[0]
    d = norm_g.shape[1]
    ng = sgu_w.shape[0]
    dg = d // ng
    tm = _tile(t, 256)

    def body(zu_ref, zv_ref, ng_ref, w_ref, b_ref, a_ref):
        vhat, _ = _layernorm_parts(_gelu(zv_ref[...].astype(F32)))
        vn = (vhat * ng_ref[...]).astype(BF16)
        u = _gelu(zu_ref[...].astype(F32))
        mask = _causal_mask()
        for g in range(ng):
            wg = jnp.where(mask, w_ref[g], 0.0).astype(BF16)
            for ci in range(tm // CHUNK):
                rs, cs = slice(ci * CHUNK, (ci + 1) * CHUNK), slice(g * dg, (g + 1) * dg)
                sv = _dot(wg, vn[rs, cs]) + b_ref[g]
                a_ref[rs, cs] = (u[rs, cs] * sv).astype(BF16)

    return pl.pallas_call(
        body, name=name, grid=(t // tm,),
        in_specs=[_rows(tm, d, 0), _rows(tm, d, 1), _const((1, d)), _const((ng, CHUNK, CHUNK)), _const((ng, CHUNK, 1))],
        out_specs=_rows(tm, d), out_shape=jax.ShapeDtypeStruct((t, d), BF16),
        compiler_params=_params("parallel"))(z, z, norm_g, sgu_w, sgu_b3)


def pool_fwd(z, pool_w, pool_scale, name):
    t = z.shape[0]
    d = pool_scale.shape[1]
    ng = pool_w.shape[0]
    dg = d // ng
    tm = _tile(t, 256)
    per = tm // HALO

    def body(c_ref, prev_ref, w_ref, s_ref, diff_ref, b_ref):
        i = pl.program_id(0)
        cur = c_ref[...].astype(F32)
        prev = jnp.where(i > 0, prev_ref[...].astype(F32), 0.0)
        ext = jnp.concatenate([prev, cur], axis=0)
        tok = i * tm + lax.broadcasted_iota(jnp.int32, (tm, 1), 0)
        for g, win in enumerate(POOL_WINDOWS):
            cs = slice(g * dg, (g + 1) * dg)
            s = ext[:, cs]
            sh = 1
            while sh < win:
                s = s + pltpu.roll(s, sh, 0)
                sh *= 2
            cnt = jnp.minimum(tok + 1, win).astype(F32)
            diff = (s[HALO:] / cnt - cur[:, cs]).astype(BF16)
            diff_ref[:, cs] = diff
            b_ref[:, cs] = (_dot(diff, w_ref[g]) * s_ref[:, cs]).astype(BF16)

    return pl.pallas_call(
        body, name=name, grid=(t // tm,),
        in_specs=[_rows(tm, d, 2), pl.BlockSpec((HALO, d), lambda i: (jnp.maximum(i * per - 1, 0), 2)),
                  _const((ng, dg, dg)), _const((1, d))],
        out_specs=[_rows(tm, d)] * 2, out_shape=[jax.ShapeDtypeStruct((t, d), BF16)] * 2,
        compiler_params=_params("parallel"))(z, z, pool_w, pool_scale)


def mixer_y(a, b, z, woa, wob, name):
    t, d = a.shape
    tm = _tile(t, 256)

    def body(a_ref, b_ref, ga_ref, gb_ref, wa_ref, wb_ref, ya_ref, yb_ref, y_ref):
        ya = _dot(a_ref[...], wa_ref[...])
        yb = _dot(b_ref[...], wb_ref[...])
        ya_ref[...] = ya.astype(BF16)
        yb_ref[...] = yb.astype(BF16)
        y_ref[...] = (_sigmoid(ga_ref[...].astype(F32)) * ya + _sigmoid(gb_ref[...].astype(F32)) * yb).astype(BF16)

    return pl.pallas_call(
        body, name=name, grid=(t // tm,),
        in_specs=[_rows(tm, d), _rows(tm, d), _rows(tm, d, 3), _rows(tm, d, 4), _const((d, d)), _const((d, d))],
        out_specs=[_rows(tm, d)] * 3, out_shape=[jax.ShapeDtypeStruct((t, d), BF16)] * 3,
        compiler_params=_params("parallel"))(a, b, z, z, woa, wob)


def ple_loss(xn, p, wpg, wpp, h, g_post, g_pre, target, name):
    t, d = xn.shape
    dp = p.shape[1]
    tm = _tile(t, 256)

    def body(x_ref, p_ref, wg_ref, wp_ref, h_ref, gp_ref, gn_ref, tg_ref, ds_ref, de_ref, dhp_ref, dgp_ref, dgn_ref, loss_ref):
        gate = _sigmoid(_dot(x_ref[...], wg_ref[...]))
        e = _dot(p_ref[...], wp_ref[...])
        q = gate * e
        hv = h_ref[...]
        err = hv + q * _rstd(q) * gp_ref[...] - tg_ref[...]
        _accumulate(loss_ref, jnp.full(loss_ref.shape, (0.5 / d) * jnp.sum(err * err), F32))
        dhv = err * (1.0 / d)
        dq, dgp = _rms_bwd(q, gp_ref[...], dhv)
        ds = (dq * e * gate * (1.0 - gate)).astype(BF16)
        ds_ref[...] = ds
        de_ref[...] = (dq * gate).astype(BF16)
        dx, dgn = _rms_bwd(hv, gn_ref[...], _dot_nt(ds, wg_ref[...]))
        dhp_ref[...] = dhv + dx
        _accumulate(dgp_ref, dgp)
        _accumulate(dgn_ref, dgn)

    return pl.pallas_call(
        body, name=name, grid=(t // tm,),
        in_specs=[_rows(tm, d), _rows(tm, dp), _const((d, d)), _const((dp, d)), _rows(tm, d), _const((1, d)), _const((1, d)),
                  _rows(tm, d)],
        out_specs=[_rows(tm, d)] * 3 + [_const((1, d))] * 2 + [_const((8, 128))],
        out_shape=[jax.ShapeDtypeStruct((t, d), BF16), jax.ShapeDtypeStruct((t, d), BF16), jax.ShapeDtypeStruct((t, d), F32),
                   jax.ShapeDtypeStruct((1, d), F32), jax.ShapeDtypeStruct((1, d), F32), jax.ShapeDtypeStruct((8, 128), F32)],
        compiler_params=_params("arbitrary"))(xn, p, wpg, wpp, h, g_post, g_pre, target)


def ffn_bwd_a(dh, f, g_post, wd4, s4, t4, scale, name, deps=()):
    t, d = dh.shape
    nk, fk, _ = wd4.shape
    tm = _tile(t, 256)

    def body(dh_ref, f_ref, gp_ref, w_ref, s_ref, t_ref, df_ref, dg_ref, du_ref, dgp_ref):
        df, dgp = _rms_bwd(f_ref[...], gp_ref[...], dh_ref[...])
        df = (scale * df).astype(BF16)
        df_ref[...] = df
        _accumulate(dgp_ref, scale * dgp)
        for k in range(nk):
            da = _dot_nt(df, w_ref[k])
            du_ref[k] = (da * s_ref[k].astype(F32)).astype(BF16)
            dg_ref[k] = (da * t_ref[k].astype(F32)).astype(BF16)

    return pl.pallas_call(
        _ordered_after(body, 6, deps), name=name, grid=(t // tm,),
        in_specs=[_rows(tm, d), _rows(tm, d), _const((1, d)), _const((nk, fk, d)), _kmajor(nk, tm, fk), _kmajor(nk, tm, fk)]
        + [ANY_SPEC] * len(deps),
        out_specs=[_rows(tm, d), _kmajor(nk, tm, fk), _kmajor(nk, tm, fk), _const((1, d))],
        out_shape=[jax.ShapeDtypeStruct((t, d), BF16), jax.ShapeDtypeStruct((nk, t, fk), BF16),
                   jax.ShapeDtypeStruct((nk, t, fk), BF16), jax.ShapeDtypeStruct((1, d), F32)],
        compiler_params=_params("arbitrary"))(dh, f, g_post, wd4, s4, t4, *deps)


def dx_norm_bwd(pairs, h, g_pre, dh_in, name, deps=()):
    t, d = h.shape
    tm = _tile(t, 256)
    n = len(pairs)

    def body(*refs):
        dys, ws = refs[:n], refs[n:2 * n]
        h_ref, g_ref, dhi_ref, dho_ref, dg_ref = refs[2 * n:]
        acc = None
        for (_, w4, sections), dy_ref, w_ref in zip(pairs, dys, ws):
            if sections:
                wide = w4.shape[2]
                edges = sorted(set(range(0, 5 * d + 1, d)) | set(range(0, 5 * d + 1, wide)))
                parts = [_dot_nt(dy_ref[DZ_SLOT[lo // d], :, lo % d:lo % d + hi - lo], w_ref[lo // wide, :, lo % wide:lo % wide + hi - lo])
                         for lo, hi in zip(edges[:-1], edges[1:])]
            else:
                parts = [_dot(dy_ref[k], w_ref[k]) for k in range(w4.shape[0])]
            for part in parts:
                acc = part if acc is None else acc + part
        dx, dg = _rms_bwd(h_ref[...], g_ref[...], acc)
        dho_ref[...] = dhi_ref[...] + dx
        _accumulate(dg_ref, dg)

    dy_specs = [_kmajor(dy.shape[0], tm, dy.shape[2]) for dy, _, _ in pairs]
    return pl.pallas_call(
        _ordered_after(body, 2 * n + 3, deps), name=name, grid=(t // tm,),
        in_specs=dy_specs + [_const(w4.shape) for _, w4, _ in pairs] + [_rows(tm, d), _const((1, d)), _rows(tm, d)]
        + [ANY_SPEC] * len(deps),
        out_specs=[_rows(tm, d), _const((1, d))],
        out_shape=[jax.ShapeDtypeStruct((t, d), F32), jax.ShapeDtypeStruct((1, d), F32)],
        compiler_params=_params("arbitrary"))(*[dy for dy, _, _ in pairs], *[w4 for _, w4, _ in pairs], h, g_pre, dh_in, *deps)


def mixer_bwd_y(dh, m, g_post, w_o, ya, yb, z, woa, wob, name, deps=()):
    t, d = dh.shape
    tm = _tile(t, 256)

    def body(dh_ref, m_ref, gp_ref, wo_ref, ya_ref, yb_ref, ga_ref, gb_ref, wa_ref, wb_ref,
             dm_ref, dya_ref, dyb_ref, dz_ref, da_ref, db_ref, dgp_ref):
        dm, dgp = _rms_bwd(m_ref[...], gp_ref[...], dh_ref[...])
        dm = dm.astype(BF16)
        dm_ref[...] = dm
        _accumulate(dgp_ref, dgp)
        dy = _dot_nt(dm, wo_ref[...])
        sa = _sigmoid(ga_ref[...].astype(F32))
        sb = _sigmoid(gb_ref[...].astype(F32))
        dya = (dy * sa).astype(BF16)
        dyb = (dy * sb).astype(BF16)
        dya_ref[...] = dya
        dyb_ref[...] = dyb
        dz_ref[0] = (dy * ya_ref[...].astype(F32) * sa * (1.0 - sa)).astype(BF16)
        dz_ref[1] = (dy * yb_ref[...].astype(F32) * sb * (1.0 - sb)).astype(BF16)
        da_ref[...] = _dot_nt(dya, wa_ref[...]).astype(BF16)
        db_ref[...] = _dot_nt(dyb, wb_ref[...]).astype(BF16)

    return pl.pallas_call(
        _ordered_after(body, 10, deps), name=name, grid=(t // tm,),
        in_specs=[_rows(tm, d), _rows(tm, d), _const((1, d)), _const((d, d)), _rows(tm, d), _rows(tm, d),
                  _rows(tm, d, 3), _rows(tm, d, 4), _const((d, d)), _const((d, d))] + [ANY_SPEC] * len(deps),
        out_specs=[_rows(tm, d)] * 3 + [pl.BlockSpec((2, tm, d), lambda i: (0, i, 0))] + [_rows(tm, d)] * 2 + [_const((1, d))],
        out_shape=[jax.ShapeDtypeStruct((t, d), BF16)] * 3 + [jax.ShapeDtypeStruct((5, t, d), BF16)]
        + [jax.ShapeDtypeStruct((t, d), BF16)] * 2 + [jax.ShapeDtypeStruct((1, d), F32)],
        compiler_params=_params("arbitrary"))(dh, m, g_post, w_o, ya, yb, z, z, woa, wob, *deps)


def sgu_bwd(z, da, dz, norm_g, sgu_w, sgu_b3, name):
    t, d = da.shape
    ng = sgu_w.shape[0]
    dg = d // ng
    tm = _tile(t, 256)
    steps = t // tm

    def body(zu_ref, zv_ref, da_ref, ng_ref, w_ref, b_ref, _, dz_ref, dw_ref, db_ref, dng_ref, dvn_ref, dsv_ref):
        i = pl.program_id(0)
        zv = zv_ref[...].astype(F32)
        zu = zu_ref[...].astype(F32)
        v, gv = _gelu_and_grad(zv)
        vhat, r = _layernorm_parts(v)
        gain = ng_ref[...]
        vn = (vhat * gain).astype(BF16)
        u, gu = _gelu_and_grad(zu)
        dav = da_ref[...].astype(F32)
        mask = _causal_mask()

        @pl.when(i == 0)
        def _():
            dw_ref[...] = jnp.zeros_like(dw_ref)
            dsv_ref[...] = jnp.zeros_like(dsv_ref)

        for g in range(ng):
            wg = jnp.where(mask, w_ref[g], 0.0).astype(BF16)
            dw = jnp.zeros((CHUNK, CHUNK), F32)
            dsv_sum = jnp.zeros((CHUNK, dg), F32)
            for ci in range(tm // CHUNK):
                rs, cs = slice(ci * CHUNK, (ci + 1) * CHUNK), slice(g * dg, (g + 1) * dg)
                vn_blk = vn[rs, cs]
                sv = _dot(wg, vn_blk) + b_ref[g]
                dz_ref[0, rs, cs] = (dav[rs, cs] * sv * gu[rs, cs]).astype(BF16)
                dsv = dav[rs, cs] * u[rs, cs]
                dsv_sum += dsv
                dsv = dsv.astype(BF16)
                dw += _dot_nt(dsv, vn_blk)
                dvn_ref[rs, cs] = _dot_tn(wg, dsv)
            dw_ref[g] += dw
            dsv_ref[:, cs] += dsv_sum

        dvn = dvn_ref[...]
        _accumulate(dng_ref, jnp.sum(dvn * vhat, axis=0, keepdims=True))
        dvh = dvn * gain
        dv = r * (dvh - jnp.mean(dvh, axis=-1, keepdims=True) - vhat * jnp.mean(dvh * vhat, axis=-1, keepdims=True))
        dz_ref[1] = (dv * gv).astype(BF16)

        @pl.when(i == steps - 1)
        def _():
            for g in range(ng):
                dw_ref[g] = jnp.where(mask, dw_ref[g], 0.0)
                row_sum = jnp.sum(dsv_ref[:, g * dg:(g + 1) * dg], axis=1, keepdims=True)
                db_ref[g] = jnp.broadcast_to(row_sum, (CHUNK, CHUNK))

    return pl.pallas_call(
        body, name=name, grid=(steps,),
        in_specs=[_rows(tm, d, 0), _rows(tm, d, 1), _rows(tm, d), _const((1, d)), _const((ng, CHUNK, CHUNK)), _const((ng, CHUNK, 1)),
                  ANY_SPEC],
        out_specs=[pl.BlockSpec((2, tm, d), lambda i: (DZ_SLOT[0] // 2, i, 0)), _const((ng, CHUNK, CHUNK)),
                   _const((ng, CHUNK, CHUNK)), _const((1, d))],
        out_shape=[jax.ShapeDtypeStruct(dz.shape, BF16), jax.ShapeDtypeStruct((ng, CHUNK, CHUNK), F32),
                   jax.ShapeDtypeStruct((ng, CHUNK, CHUNK), F32), jax.ShapeDtypeStruct((1, d), F32)],
        scratch_shapes=[pltpu.VMEM((tm, d), F32), pltpu.VMEM((CHUNK, d), F32)], input_output_aliases={6: 0},
        compiler_params=_params("arbitrary"))(z, z, da, norm_g, sgu_w, sgu_b3, dz)


def pool_bwd(db, diff, dz, pool_w, pool_scale, name):
    t, d = db.shape
    ng = pool_w.shape[0]
    dg = d // ng
    tm = _tile(t, 256)
    per = tm // HALO
    steps = t // tm

    def body(db_ref, next_ref, diff_ref, w_ref, s_ref, _, dc_ref, dw_ref, ds_ref):
        i = pl.program_id(0)
        dbc = db_ref[...].astype(F32)
        nxt = jnp.where(i < steps - 1, next_ref[...].astype(F32), 0.0)
        ext = jnp.concatenate([dbc, nxt], axis=0)
        rows = tm + HALO
        tok = i * tm + lax.broadcasted_iota(jnp.int32, (rows, 1), 0)

        @pl.when(i == 0)
        def _():
            dw_ref[...] = jnp.zeros_like(dw_ref)
            ds_ref[...] = jnp.zeros_like(ds_ref)

        for g, win in enumerate(POOL_WINDOWS):
            cs = slice(g * dg, (g + 1) * dg)
            dp = (ext[:, cs] * s_ref[:, cs]).astype(BF16)
            dd = _dot_nt(dp, w_ref[g])
            s = dd / jnp.minimum(tok + 1, win).astype(F32)
            sh = 1
            while sh < win:
                s = s + pltpu.roll(s, rows - sh, 0)
                sh *= 2
            dc_ref[0, :, cs] = (s[:tm] - dd[:tm]).astype(BF16)
            dfg = diff_ref[:, cs]
            ds_ref[:, cs] += jnp.sum(dbc[:, cs] * _dot(dfg, w_ref[g]), axis=0, keepdims=True)
            dw_ref[g] += _dot_tn(dfg, dp[:tm])

    return pl.pallas_call(
        body, name=name, grid=(steps,),
        in_specs=[_rows(tm, d), pl.BlockSpec((HALO, d), lambda i: (jnp.minimum((i + 1) * per, t // HALO - 1), 0)),
                  _rows(tm, d), _const((ng, dg, dg)), _const((1, d)), ANY_SPEC],
        out_specs=[pl.BlockSpec((1, tm, d), lambda i: (DZ_SLOT[2], i, 0)), _const((ng, dg, dg)), _const((1, d))],
        out_shape=[jax.ShapeDtypeStruct(dz.shape, BF16), jax.ShapeDtypeStruct((ng, dg, dg), F32), jax.ShapeDtypeStruct((1, d), F32)],
        input_output_aliases={5: 0},
        compiler_params=_params("arbitrary"))(db, db, diff, pool_w, pool_scale, dz)


def dw_tn(x, dy, nk, name, x_kmajor=False, dy_mode="same"):
    t = x.shape[-2]
    kx = x.shape[-1]
    n = dy.shape[-1] // nk if dy_mode == "cols" else dy.shape[-1]
    tt = _tile(t, DW_TOKENS)
    steps = t // tt

    def body(x_ref, dy_ref, o_ref, acc_ref):
        s = pl.program_id(1)
        part = _dot_tn(x_ref[0] if x_kmajor else x_ref[...], dy_ref[0] if dy_mode == "kmajor" else dy_ref[...])
        if steps == 1:
            o_ref[0] = part.astype(BF16)
            return

        @pl.when(s == 0)
        def _():
            acc_ref[...] = jnp.zeros_like(acc_ref)

        acc_ref[...] += part

        @pl.when(s == steps - 1)
        def _():
            o_ref[0] = acc_ref[...].astype(BF16)

    x_spec = pl.BlockSpec((1, tt, kx), lambda k, s: (k, s, 0)) if x_kmajor else pl.BlockSpec((tt, kx), lambda k, s: (s, 0))
    dy_spec = {"kmajor": pl.BlockSpec((1, tt, n), lambda k, s: (k, s, 0)), "cols": pl.BlockSpec((tt, n), lambda k, s: (s, k)),
               "same": pl.BlockSpec((tt, n), lambda k, s: (s, 0))}[dy_mode]
    return pl.pallas_call(
        body, name=name, grid=(nk, steps), in_specs=[x_spec, dy_spec],
        out_specs=pl.BlockSpec((1, kx, n), lambda k, s: (k, 0, 0)), out_shape=jax.ShapeDtypeStruct((nk, kx, n), BF16),
        scratch_shapes=[pltpu.VMEM((kx, n) if steps > 1 else (8, 128), F32)],
        compiler_params=_params("parallel", "arbitrary"))(x, dy)


def dw_in_tiles(xn, dz, nk, name):
    t, d = xn.shape
    sections = len(DZ_SLOT)
    per_section = d // DW_IN_TILE
    per_shard = sections * per_section // nk

    def body(x_ref, dy_ref, o_ref):
        o_ref[0] = _dot_tn(x_ref[...], dy_ref[0]).astype(BF16)

    def slot(j):
        return (j // per_section + DZ_SLOT[0]) % sections

    return pl.pallas_call(
        body, name=name, grid=(sections * per_section,),
        in_specs=[pl.BlockSpec((t, d), lambda j: (0, 0)), pl.BlockSpec((1, t, DW_IN_TILE), lambda j: (slot(j), 0, j % per_section))],
        out_specs=pl.BlockSpec((1, d, DW_IN_TILE), lambda j: (j // per_shard, 0, j % per_shard)),
        out_shape=jax.ShapeDtypeStruct((nk, d, sections * d // nk), BF16), compiler_params=_params("parallel"))(xn, dz)


def _place():
    x, y, c = lax.axis_index("x"), lax.axis_index("y"), lax.axis_index("c")
    chips = [((1 - x) if fx else x, (1 - y) if fy else y) for fx, fy in OTHER_CHIPS]
    return x, y, c, chips


def _half(rows, which):
    return pl.ds(pl.multiple_of(which * (rows // 2), 16), rows // 2)


def _hbm(a):
    return pltpu.with_memory_space_constraint(a, pltpu.HBM)


def _by_shape(arrays):
    buckets = {}
    for i, a in enumerate(arrays):
        buckets.setdefault(a.shape, []).append(i)
    return list(buckets.values())


def cast_place(shards, chip, name, deps=()):
    n = len(shards)
    r, cdim = shards[0].shape
    tr = _tile(r, 256)

    def body(chip_ref, *refs):
        for w_ref, o_ref in zip(refs[:n], refs[n + len(deps):]):
            o_ref[0] = w_ref[...].astype(BF16)

    return pl.pallas_call(
        body, name=name,
        grid_spec=pltpu.PrefetchScalarGridSpec(
            num_scalar_prefetch=1, grid=(r // tr,),
            in_specs=[pl.BlockSpec((tr, cdim), lambda i, chip_ref: (i, 0))] * n + [ANY_SPEC] * len(deps),
            out_specs=[pl.BlockSpec((1, tr, cdim), lambda i, chip_ref: (chip_ref[0], i, 0))] * n),
        out_shape=[jax.ShapeDtypeStruct((N_CHIPS, r, cdim), BF16)] * n, compiler_params=_params("parallel"))(chip, *shards, *deps)


def _gather_copy(buf, sends, recvs, i, j, me, chip_xy, c):
    cx, cy = chip_xy
    mine = _half(buf.shape[1], c)
    return pltpu.make_async_remote_copy(
        src_ref=buf.at[me, mine], dst_ref=buf.at[me, mine], send_sem=sends.at[3 * i + j], recv_sem=recvs.at[3 * i + j],
        device_id=(cx, cy, c), device_id_type=MESH_IDS)


def allgather_start(bufs, name):
    n = len(bufs)

    def body(*refs):
        ins = refs[:n]
        sends, recvs = refs[n], refs[n + 1]
        token = refs[2 * n + 2]
        x, y, c, chips = _place()
        for i in range(n):
            for j, chip_xy in enumerate(chips):
                _gather_copy(ins[i], sends, recvs, i, j, 2 * x + y, chip_xy, c).start()
        token[...] = jnp.zeros_like(token)

    out = pl.pallas_call(
        body, name=name, in_specs=[HBM_SPEC] * n,
        out_specs=[SEM_SPEC, SEM_SPEC] + [HBM_SPEC] * n + [VMEM_SPEC],
        out_shape=[pltpu.SemaphoreType.DMA((3 * n,)), pltpu.SemaphoreType.DMA((3 * n,))]
        + [pltpu.HBM(b.shape, b.dtype) for b in bufs] + [jax.ShapeDtypeStruct((8, 128), F32)],
        input_output_aliases={i: i + 2 for i in range(n)},
        compiler_params=pltpu.CompilerParams(has_side_effects=DATAFLOW))(*[_hbm(b) for b in bufs])
    return out[0], out[1], list(out[2:2 + n]), out[2 + n]


def allgather_wait(sends, recvs, bufs, after, name):
    n = len(bufs)

    def body(*refs):
        ins = refs[:n]
        send_sems, recv_sems = refs[n], refs[n + 1]
        x, y, c, chips = _place()
        for i in range(n):
            for j, (cx, cy) in enumerate(chips):
                mine = _half(ins[i].shape[1], c)
                cp = pltpu.make_async_remote_copy(
                    src_ref=ins[i].at[2 * x + y, mine], dst_ref=ins[i].at[2 * cx + cy, mine], send_sem=send_sems.at[3 * i + j],
                    recv_sem=recv_sems.at[3 * i + j], device_id=(cx, cy, c), device_id_type=MESH_IDS)
                cp.wait_send()
                cp.wait_recv()

    return pl.pallas_call(
        body, name=name, in_specs=[HBM_SPEC] * n + [SEM_SPEC, SEM_SPEC, ANY_SPEC], out_specs=[HBM_SPEC] * n,
        out_shape=[pltpu.HBM(b.shape, b.dtype) for b in bufs], input_output_aliases={i: i for i in range(n)},
        compiler_params=pltpu.CompilerParams(has_side_effects=DATAFLOW))(*bufs, sends, recvs, after)


def d2d_forward(bufs, name):
    n = len(bufs)

    def body(*refs):
        ins = refs[:n]
        send_sems, recv_sems = refs[2 * n:]
        x, y, c, chips = _place()
        copies = []
        for i in range(n):
            mine = _half(ins[i].shape[1], c)
            for j, (cx, cy) in enumerate(chips):
                landed = ins[i].at[2 * cx + cy, mine]
                cp = pltpu.make_async_remote_copy(
                    src_ref=landed, dst_ref=landed, send_sem=send_sems.at[i, j], recv_sem=recv_sems.at[i, j],
                    device_id=(x, y, 1 - c), device_id_type=MESH_IDS)
                cp.start()
                copies.append(cp)
        for i in range(n):
            theirs = _half(ins[i].shape[1], 1 - c)
            for j, (cx, cy) in enumerate(chips):
                passed = ins[i].at[2 * cx + cy, theirs]
                pltpu.make_async_remote_copy(
                    src_ref=passed, dst_ref=passed, send_sem=send_sems.at[i, j], recv_sem=recv_sems.at[i, j],
                    device_id=(x, y, 1 - c), device_id_type=MESH_IDS).wait_recv()
        for cp in copies:
            cp.wait_send()

    return pl.pallas_call(
        body, name=name, in_specs=[HBM_SPEC] * n, out_specs=[HBM_SPEC] * n,
        out_shape=[jax.ShapeDtypeStruct(b.shape, b.dtype) for b in bufs], input_output_aliases={i: i for i in range(n)},
        scratch_shapes=[pltpu.SemaphoreType.DMA((n, 3))] * 2,
        compiler_params=pltpu.CompilerParams(has_side_effects=True))(*bufs)


def _forward_copy(buf, sends, recvs, i, j, chip_xy, x, y, c):
    cx, cy = chip_xy
    rows = buf.shape[1]
    return pltpu.make_async_remote_copy(
        src_ref=buf.at[2 * cx + cy, _half(rows, c)], dst_ref=buf.at[2 * cx + cy, _half(rows, 1 - c)],
        send_sem=sends.at[3 * i + j], recv_sem=recvs.at[3 * i + j], device_id=(x, y, 1 - c), device_id_type=MESH_IDS)


def d2d_forward_start(bufs, name):
    n = len(bufs)

    def body(*refs):
        ins = refs[:n]
        sends, recvs = refs[n], refs[n + 1]
        token = refs[2 * n + 2]
        x, y, c, chips = _place()
        for i in range(n):
            mine = _half(ins[i].shape[1], c)
            for j, (cx, cy) in enumerate(chips):
                landed = ins[i].at[2 * cx + cy, mine]
                pltpu.make_async_remote_copy(
                    src_ref=landed, dst_ref=landed, send_sem=sends.at[3 * i + j], recv_sem=recvs.at[3 * i + j],
                    device_id=(x, y, 1 - c), device_id_type=MESH_IDS).start()
        token[...] = jnp.zeros_like(token)

    out = pl.pallas_call(
        body, name=name, in_specs=[HBM_SPEC] * n,
        out_specs=[SEM_SPEC, SEM_SPEC] + [HBM_SPEC] * n + [VMEM_SPEC],
        out_shape=[pltpu.SemaphoreType.DMA((3 * n,)), pltpu.SemaphoreType.DMA((3 * n,))]
        + [pltpu.HBM(b.shape, b.dtype) for b in bufs] + [jax.ShapeDtypeStruct((8, 128), F32)],
        input_output_aliases={i: i + 2 for i in range(n)},
        compiler_params=pltpu.CompilerParams(has_side_effects=DATAFLOW))(*[_hbm(b) for b in bufs])
    return (out[0], out[1], list(out[2:2 + n])), out[2 + n]


def d2d_forward_wait(started, after, name):
    sends, recvs, bufs = started
    n = len(bufs)

    def body(*refs):
        ins = refs[:n]
        send_sems, recv_sems = refs[n], refs[n + 1]
        x, y, c, chips = _place()
        for i in range(n):
            for j, chip_xy in enumerate(chips):
                cp = _forward_copy(ins[i], send_sems, recv_sems, i, j, chip_xy, x, y, c)
                cp.wait_send()
                cp.wait_recv()

    return pl.pallas_call(
        body, name=name, in_specs=[HBM_SPEC] * n + [SEM_SPEC, SEM_SPEC, ANY_SPEC], out_specs=[HBM_SPEC] * n,
        out_shape=[pltpu.HBM(b.shape, b.dtype) for b in bufs], input_output_aliases={i: i for i in range(n)},
        compiler_params=pltpu.CompilerParams(has_side_effects=DATAFLOW))(*bufs, sends, recvs, after)


def _sibling_copy(src, land, sends, recvs, i, x, y, c, halves):
    part = src.at[:, _half(src.shape[1], 1 - c)] if halves else src
    return pltpu.make_async_remote_copy(
        src_ref=part, dst_ref=land, send_sem=sends.at[i], recv_sem=recvs.at[i], device_id=(x, y, 1 - c),
        device_id_type=MESH_IDS)


def sibling_start(arrays, halves, name):
    n = len(arrays)
    lands = [lax.empty((a.shape[0], a.shape[1] // 2, a.shape[2]) if halves else a.shape, a.dtype) for a in arrays]

    def body(*refs):
        srcs, zones = refs[:n], refs[n:2 * n]
        sends, recvs = refs[2 * n], refs[2 * n + 1]
        token = refs[4 * n + 2]
        x, y, c, _ = _place()
        for i in range(n):
            _sibling_copy(srcs[i], zones[i], sends, recvs, i, x, y, c, halves).start()
        token[...] = jnp.zeros_like(token)

    out = pl.pallas_call(
        body, name=name, in_specs=[HBM_SPEC] * (2 * n),
        out_specs=[SEM_SPEC, SEM_SPEC] + [HBM_SPEC] * (2 * n) + [VMEM_SPEC],
        out_shape=[pltpu.SemaphoreType.DMA((n,)), pltpu.SemaphoreType.DMA((n,))]
        + [pltpu.HBM(a.shape, a.dtype) for a in arrays + lands] + [jax.ShapeDtypeStruct((8, 128), F32)],
        input_output_aliases={i: i + 2 for i in range(2 * n)},
        compiler_params=pltpu.CompilerParams(has_side_effects=DATAFLOW))(*[_hbm(a) for a in arrays + lands])
    return (out[0], out[1], list(out[2:2 + n]), list(out[2 + n:2 + 2 * n])), out[2 + 2 * n]


def sibling_wait(started, halves, after, name):
    sends, recvs, arrays, lands = started
    n = len(arrays)
    after = tuple(after) if isinstance(after, (tuple, list)) else (after,)

    def body(*refs):
        srcs, zones = refs[:n], refs[n:2 * n]
        send_sems, recv_sems = refs[2 * n], refs[2 * n + 1]
        x, y, c, _ = _place()
        for i in range(n):
            cp = _sibling_copy(srcs[i], zones[i], send_sems, recv_sems, i, x, y, c, halves)
            cp.wait_send()
            cp.wait_recv()

    out = pl.pallas_call(
        body, name=name, in_specs=[HBM_SPEC] * (2 * n) + [SEM_SPEC, SEM_SPEC] + [ANY_SPEC] * len(after),
        out_specs=[HBM_SPEC] * (2 * n),
        out_shape=[pltpu.HBM(a.shape, a.dtype) for a in arrays + lands], input_output_aliases={i: i for i in range(2 * n)},
        compiler_params=pltpu.CompilerParams(has_side_effects=DATAFLOW))(*arrays, *lands, sends, recvs, *after)
    return list(out[:n]), list(out[n:])


def _scatter_copy(src, land, sends, recvs, i, j, chip_xy, c):
    cx, cy = chip_xy
    return pltpu.make_async_remote_copy(
        src_ref=src.at[2 * cx + cy], dst_ref=land.at[j], send_sem=sends.at[3 * i + j], recv_sem=recvs.at[3 * i + j],
        device_id=(cx, cy, c), device_id_type=MESH_IDS)


def scatter_start(sums, name):
    n = len(sums)
    lands = [lax.empty((3,) + s.shape[1:], s.dtype) for s in sums]

    def body(*refs):
        srcs, zones = refs[:n], refs[n:2 * n]
        sends, recvs = refs[2 * n], refs[2 * n + 1]
        token = refs[4 * n + 2]
        _, _, c, chips = _place()
        for i in range(n):
            for j, chip_xy in enumerate(chips):
                _scatter_copy(srcs[i], zones[i], sends, recvs, i, j, chip_xy, c).start()
        token[...] = jnp.zeros_like(token)

    out = pl.pallas_call(
        body, name=name, in_specs=[HBM_SPEC] * (2 * n),
        out_specs=[SEM_SPEC, SEM_SPEC] + [HBM_SPEC] * (2 * n) + [VMEM_SPEC],
        out_shape=[pltpu.SemaphoreType.DMA((3 * n,)), pltpu.SemaphoreType.DMA((3 * n,))]
        + [pltpu.HBM(a.shape, a.dtype) for a in sums + lands] + [jax.ShapeDtypeStruct((8, 128), F32)],
        input_output_aliases={i: i + 2 for i in range(2 * n)},
        compiler_params=pltpu.CompilerParams(has_side_effects=DATAFLOW))(*[_hbm(a) for a in sums + lands])
    return out[0], out[1], list(out[2:2 + n]), list(out[2 + n:2 + 2 * n]), out[2 + 2 * n]


def scatter_wait(sends, recvs, sums, lands, after, name):
    n = len(sums)

    def body(*refs):
        srcs, zones = refs[:n], refs[n:2 * n]
        send_sems, recv_sems = refs[2 * n], refs[2 * n + 1]
        _, _, c, chips = _place()
        for i in range(n):
            for j, chip_xy in enumerate(chips):
                cp = _scatter_copy(srcs[i], zones[i], send_sems, recv_sems, i, j, chip_xy, c)
                cp.wait_send()
                cp.wait_recv()

    out = pl.pallas_call(
        body, name=name, in_specs=[HBM_SPEC] * (2 * n) + [SEM_SPEC, SEM_SPEC, ANY_SPEC], out_specs=[HBM_SPEC] * (2 * n),
        out_shape=[pltpu.HBM(a.shape, a.dtype) for a in sums + lands], input_output_aliases={i: i for i in range(2 * n)},
        compiler_params=pltpu.CompilerParams(has_side_effects=DATAFLOW))(*sums, *lands, sends, recvs, after)
    return list(out[:n]), list(out[n:])


def add_halves(grads, recvs, core, name):
    n = len(grads)
    nk, r, cdim = grads[0].shape
    half = r // 2
    views = [g.reshape(nk, 2, half, cdim) for g in grads]

    def body(core_ref, *refs):
        for g_ref, r_ref, o_ref in zip(refs[:n], refs[n:2 * n], refs[2 * n:]):
            o_ref[0] = (g_ref[0, 0].astype(F32) + r_ref[0].astype(F32)).astype(BF16)

    return pl.pallas_call(
        body, name=name,
        grid_spec=pltpu.PrefetchScalarGridSpec(
            num_scalar_prefetch=1, grid=(nk,),
            in_specs=[pl.BlockSpec((1, 1, half, cdim), lambda k, core_ref: (k, core_ref[0], 0, 0))] * n
            + [pl.BlockSpec((1, half, cdim), lambda k, core_ref: (k, 0, 0))] * n,
            out_specs=[pl.BlockSpec((1, half, cdim), lambda k, core_ref: (k, 0, 0))] * n),
        out_shape=[jax.ShapeDtypeStruct((nk, half, cdim), BF16)] * n, compiler_params=_params("parallel"))(core, *views, *recvs)


def add_chips(sums, lands, chip, name):
    n = len(sums)
    _, half, cdim = sums[0].shape
    tr = _tile(half, 128)

    def body(chip_ref, *refs):
        for s_ref, r_ref, o_ref in zip(refs[:n], refs[n:2 * n], refs[2 * n:]):
            o_ref[...] = ((s_ref[0].astype(F32) + r_ref[0].astype(F32)) + r_ref[1].astype(F32)) + r_ref[2].astype(F32)

    return pl.pallas_call(
        body, name=name,
        grid_spec=pltpu.PrefetchScalarGridSpec(
            num_scalar_prefetch=1, grid=(half // tr,),
            in_specs=[pl.BlockSpec((1, tr, cdim), lambda i, chip_ref: (chip_ref[0], i, 0))] * n
            + [pl.BlockSpec((3, tr, cdim), lambda i, chip_ref: (0, i, 0))] * n,
            out_specs=[pl.BlockSpec((tr, cdim), lambda i, chip_ref: (i, 0))] * n),
        out_shape=[jax.ShapeDtypeStruct((half, cdim), F32)] * n, compiler_params=_params("parallel"))(chip, *sums, *lands)


def _adamw_math(w, g, m, v):
    m = ADAM_B1 * m + (1.0 - ADAM_B1) * g
    v = ADAM_B2 * v + (1.0 - ADAM_B2) * (g * g)
    m_hat = m / (1.0 - ADAM_B1 ** ADAM_STEP)
    v_hat = v / (1.0 - ADAM_B2 ** ADAM_STEP)
    return -ADAM_LR * (m_hat / (jnp.sqrt(v_hat) + ADAM_EPS) + ADAM_WD * w), m, v


def adamw_halves(ws, owns, others, ms, vs, core, name, deps=()):
    n = len(ws)
    r, cdim = ws[0].shape
    half = r // 2
    tr = _tile(half, max(8, 256 // n))
    steps = half // tr

    def body(core_ref, *refs):
        ins, outs = refs[:5 * n], refs[5 * n + len(deps):]
        for a in range(n):
            w_ref, own_ref, other_ref, m_ref, v_ref = ins[a::n]
            g = jnp.where(pl.program_id(0) == core_ref[0], own_ref[...], other_ref[...])
            outs[a][...] = g
            outs[n + a][...], outs[2 * n + a][...], outs[3 * n + a][...] = _adamw_math(w_ref[...], g, m_ref[...], v_ref[...])

    whole = pl.BlockSpec((tr, cdim), lambda hh, i, core_ref: (hh * steps + i, 0))
    own = pl.BlockSpec((tr, cdim), lambda hh, i, core_ref: (jnp.where(hh == core_ref[0], i, 0), 0))
    other = pl.BlockSpec((tr, cdim), lambda hh, i, core_ref: (jnp.where(hh == core_ref[0], 0, i), 0))
    out = pl.pallas_call(
        body, name=name,
        grid_spec=pltpu.PrefetchScalarGridSpec(
            num_scalar_prefetch=1, grid=(2, steps),
            in_specs=[whole] * n + [own] * n + [other] * n + [whole] * (2 * n) + [ANY_SPEC] * len(deps),
            out_specs=[whole] * (4 * n)),
        out_shape=[jax.ShapeDtypeStruct((r, cdim), F32)] * (4 * n),
        compiler_params=_params("parallel", "parallel"))(core, *ws, *owns, *others, *ms, *vs, *deps)
    return [out[a::n] for a in range(n)]


def small_allreduce_adamw(g, w, m, v):
    r, cdim = g.shape

    def body(g_ref, w_ref, m_ref, v_ref, go_ref, d_ref, mo_ref, vo_ref, pair, quad, d2d_send, d2d_recv, ici_send, ici_recv):
        x, y, c, chips = _place()
        me = 2 * x + y
        pair[c] = g_ref[...]
        swap = pltpu.make_async_remote_copy(
            src_ref=g_ref, dst_ref=pair.at[c], send_sem=d2d_send, recv_sem=d2d_recv, device_id=(x, y, 1 - c),
            device_id_type=MESH_IDS)
        swap.start()
        swap.wait()
        quad[me] = pair[0] + pair[1]
        copies = []
        for j, (cx, cy) in enumerate(chips):
            cp = pltpu.make_async_remote_copy(
                src_ref=quad.at[me], dst_ref=quad.at[me], send_sem=ici_send.at[j], recv_sem=ici_recv.at[j],
                device_id=(cx, cy, c), device_id_type=MESH_IDS)
            cp.start()
            copies.append(cp)
        for j, (cx, cy) in enumerate(chips):
            slot = quad.at[2 * cx + cy]
            pltpu.make_async_remote_copy(
                src_ref=slot, dst_ref=slot, send_sem=ici_send.at[j], recv_sem=ici_recv.at[j], device_id=(cx, cy, c),
                device_id_type=MESH_IDS).wait_recv()
        for cp in copies:
            cp.wait_send()
        total = (quad[0] + quad[1]) + (quad[2] + quad[3])
        go_ref[...] = total
        d_ref[...], mo_ref[...], vo_ref[...] = _adamw_math(w_ref[...], total, m_ref[...], v_ref[...])

    return pl.pallas_call(
        body, name="small_allreduce_adamw", in_specs=[VMEM_SPEC] * 4, out_specs=[VMEM_SPEC] * 4,
        out_shape=[jax.ShapeDtypeStruct((r, cdim), F32)] * 4,
        scratch_shapes=[pltpu.VMEM((2, r, cdim), F32), pltpu.VMEM((N_CHIPS, r, cdim), F32), pltpu.SemaphoreType.DMA,
                        pltpu.SemaphoreType.DMA, pltpu.SemaphoreType.DMA((3,)), pltpu.SemaphoreType.DMA((3,))],
        compiler_params=pltpu.CompilerParams(has_side_effects=True, vmem_limit_bytes=VMEM_LIMIT_V7X))(g, w, m, v)


GROUPS = (
    ("ffn1", ("ffn1_w_gate", "ffn1_w_up", "ffn1_w_down")),
    ("mixer", ("w_in", "pool_w", "w_out_a", "w_out_b", "w_o")),
    ("ffn2", ("ffn2_w_gate", "ffn2_w_up", "ffn2_w_down")),
    ("ple", ("ple_w_gate", "ple_w_proj")),
)
GATHERS = (
    ("ffn1_in", ("ffn1_w_gate", "ffn1_w_up")),
    ("ffn1_out", ("ffn1_w_down",)),
) + GROUPS[1:]
GAINS = ("ffn1_pre_g", "ffn1_post_g", "mix_pre_g", "sgu_norm_g", "pool_scale", "mix_post_g",
         "ffn2_pre_g", "ffn2_post_g", "ple_pre_g", "ple_post_g")
SMALL = GAINS + ("sgu_b", "sgu_w")
PACKED = SMALL + ("loss",)
WEIGHTS = ("ffn1_pre_g", "ffn1_w_gate", "ffn1_w_up", "ffn1_w_down", "ffn1_post_g", "mix_pre_g", "w_in", "sgu_norm_g",
           "sgu_w", "sgu_b", "pool_w", "pool_scale", "w_out_a", "w_out_b", "w_o", "mix_post_g", "ffn2_pre_g",
           "ffn2_w_gate", "ffn2_w_up", "ffn2_w_down", "ffn2_post_g", "ple_pre_g", "ple_w_gate", "ple_w_proj", "ple_post_g")
PACK_ROWS = 16


TRANSPOSED = ("ffn1_w_gate", "ffn1_w_up", "ffn2_w_gate", "ffn2_w_up")


def _shard2d(name, a):
    a = a[0]
    return a.T if name in TRANSPOSED else a.reshape(-1, a.shape[-1])


def _unshard2d(name, a2d, shape):
    return (a2d.T if name in TRANSPOSED else a2d).reshape(shape)


def pack_rows(gains, sgu_b, loss_tile, name):
    n = len(gains)
    d = gains[0].shape[1]
    g = sgu_b.shape[0]

    def body(*refs):
        o_ref = refs[-1]
        o_ref[...] = jnp.zeros_like(o_ref)
        for i in range(n):
            o_ref[i:i + 1, :] = refs[i][...]
        o_ref[n:n + g, 0:CHUNK] = refs[n][...]
        o_ref[n + g:n + g + 1, 0:CHUNK] = refs[n + 1][0:1, :]

    return pl.pallas_call(
        body, name=name, in_specs=[VMEM_SPEC] * (n + 2), out_specs=VMEM_SPEC,
        out_shape=jax.ShapeDtypeStruct((PACK_ROWS, d), F32))(*gains, sgu_b, loss_tile)


def _pack_small(parts, tag):
    d = parts[GAINS[0]].shape[-1]
    rows = pack_rows([parts[n] for n in GAINS], parts["sgu_b"].reshape(-1, CHUNK), parts["loss"], "pack_" + tag)
    return jnp.concatenate([rows, parts["sgu_w"].reshape(-1, d)], axis=0)


def _unpack_small(packed, like):
    n, g = len(GAINS), like["sgu_b"].size // CHUNK
    out = {name: packed[i:i + 1] for i, name in enumerate(GAINS)}
    out["sgu_b"] = packed[n:n + g, :CHUNK].reshape(like["sgu_b"].shape)
    out["loss"] = packed[n + g, 0]
    out["sgu_w"] = packed[PACK_ROWS:].reshape(like["sgu_w"].shape)
    return out


def _ffn_fwd(xn, h, w, pre, g_post, g_next, tag, between=None):
    a4, s4, t4 = ffn_gu(xn, w[pre + "w_gate"], w[pre + "w_up"], tag + "_gu")
    deps = ()
    if between is not None:
        more, deps = between(a4)
        w.update(more)
    f, h_new, xn_next = mm_norm_res(a4, w[pre + "w_down"], h, g_post, g_next, 0.5, tag + "_down", deps=deps)
    return dict(xn=xn, h=h, a4=a4, s4=s4, t4=t4, f=f), h_new, xn_next


def _ffn_bwd_w(dh, saved, w, pre, g_post, tag, deps):
    df, dg4, du4, d_post = ffn_bwd_a(dh, saved["f"], g_post, w[pre + "w_down"], saved["s4"], saved["t4"], 0.5, tag + "_bwd_a",
                                     deps=deps)
    nk = N_CHIPS
    grads = {
        pre + "w_down": dw_tn(saved["a4"], df, nk, tag + "_dw_down", x_kmajor=True, dy_mode="same"),
        pre + "w_gate": dw_tn(dg4, saved["xn"], nk, tag + "_dw_gate", x_kmajor=True, dy_mode="same"),
        pre + "w_up": dw_tn(du4, saved["xn"], nk, tag + "_dw_up", x_kmajor=True, dy_mode="same"),
    }
    return dg4, du4, grads, d_post


def _ffn_bwd_x(dh, dg4, du4, saved, w, pre, g_pre, tag, deps):
    return dx_norm_bwd([(dg4, w[pre + "w_gate"], False), (du4, w[pre + "w_up"], False)], saved["h"], g_pre, dh,
                       tag + "_bwd_x", deps=deps)


def _gather_group(names, weights, chip, tag, deps):
    shards = [_shard2d(n, weights[n]) for n in names]
    bufs = [None] * len(names)
    for b, idx in enumerate(_by_shape(shards)):
        for i, buf in zip(idx, cast_place([shards[i] for i in idx], chip, f"cast_{tag}_{b}", deps=deps)):
            bufs[i] = buf
    sends, recvs, bufs, token = allgather_start(bufs, "allgather_start_" + tag)
    return (sends, recvs, bufs), token


def _gathered(started, names, after, tag):
    sends, recvs, bufs = started
    landed = allgather_wait(sends, recvs, bufs, after, "allgather_wait_" + tag)
    return dict(zip(names, d2d_forward(landed, "d2d_forward_" + tag)))


def _forward_early(started, after, tag):
    sends, recvs, bufs = started
    landed = allgather_wait(sends, recvs, bufs, after, "allgather_wait_" + tag)
    return d2d_forward_start(landed, "d2d_forward_start_" + tag)


def _forwarded(forwarding, names, after, tag):
    return dict(zip(names, d2d_forward_wait(forwarding, after, "d2d_forward_wait_" + tag)))


def _exchange_start(names, big_g, tag):
    return sibling_start([big_g[n] for n in names], True, "exchange_start_" + tag)


def _scatter_begin(names, exchanging, after, core, tag):
    partial, from_sibling = sibling_wait(exchanging, True, after, "exchange_wait_" + tag)
    chip_sums = [None] * len(names)
    for b, idx in enumerate(_by_shape(partial)):
        for i, s in zip(idx, add_halves([partial[i] for i in idx], [from_sibling[i] for i in idx], core, f"add_halves_{tag}_{b}")):
            chip_sums[i] = s
    sends, recvs, sums, lands, token = scatter_start(chip_sums, "scatter_start_" + tag)
    return (names, sends, recvs, sums, lands), token


def _share_begin(scattering, after, chip, tag):
    names, sends, recvs, sums, lands = scattering
    sums, lands = scatter_wait(sends, recvs, sums, lands, after, "scatter_wait_" + tag)
    reduced = [None] * len(names)
    for b, idx in enumerate(_by_shape(sums)):
        for i, r in zip(idx, add_chips([sums[i] for i in idx], [lands[i] for i in idx], chip, f"add_chips_{tag}_{b}")):
            reduced[i] = r
    sharing, token = sibling_start(reduced, False, "share_start_" + tag)
    return (names, sharing), token


def _update(shared, after, weights, moments_m, moments_v, core, tag, results, held):
    names, sharing = shared
    reduced, others = sibling_wait(sharing, False, after, "share_wait_" + tag)
    buckets = _by_shape(reduced)

    def update_bucket(b, idx, deps):
        in_bucket = [names[i] for i in idx]
        outs = adamw_halves([_shard2d(n, weights[n]) for n in in_bucket], [reduced[i] for i in idx], [others[i] for i in idx],
                            [_shard2d(n, moments_m[n]) for n in in_bucket], [_shard2d(n, moments_v[n]) for n in in_bucket],
                            core, f"adamw_{tag}_{b}", deps=deps)
        for n, per_weight in zip(in_bucket, outs):
            for store, value in zip(results, per_weight):
                store[n] = _unshard2d(n, value, weights[n].shape)
        return outs[-1][1]

    for b, idx in enumerate(buckets[:-1]):
        held.append(functools.partial(update_bucket, b, idx))
    return update_bucket(len(buckets) - 1, buckets[-1], ())


def kernel(x, p, ffn1_pre_g, ffn1_w_gate, ffn1_w_up, ffn1_w_down, ffn1_post_g, mix_pre_g, w_in, sgu_norm_g, sgu_w, sgu_b, pool_w, pool_scale, w_out_a, w_out_b, w_o, mix_post_g, ffn2_pre_g, ffn2_w_gate, ffn2_w_up, ffn2_w_down, ffn2_post_g, ple_pre_g, ple_w_gate, ple_w_proj, ple_post_g, loss_target, m_ffn1_pre_g, m_ffn1_w_gate, m_ffn1_w_up, m_ffn1_w_down, m_ffn1_post_g, m_mix_pre_g, m_w_in, m_sgu_norm_g, m_sgu_w, m_sgu_b, m_pool_w, m_pool_scale, m_w_out_a, m_w_out_b, m_w_o, m_mix_post_g, m_ffn2_pre_g, m_ffn2_w_gate, m_ffn2_w_up, m_ffn2_w_down, m_ffn2_post_g, m_ple_pre_g, m_ple_w_gate, m_ple_w_proj, m_ple_post_g, v_ffn1_pre_g, v_ffn1_w_gate, v_ffn1_w_up, v_ffn1_w_down, v_ffn1_post_g, v_mix_pre_g, v_w_in, v_sgu_norm_g, v_sgu_w, v_sgu_b, v_pool_w, v_pool_scale, v_w_out_a, v_w_out_b, v_w_o, v_mix_post_g, v_ffn2_pre_g, v_ffn2_w_gate, v_ffn2_w_up, v_ffn2_w_down, v_ffn2_post_g, v_ple_pre_g, v_ple_w_gate, v_ple_w_proj, v_ple_post_g):
    given = dict(locals())
    weights = {n: given[n] for n in WEIGHTS}
    moments_m = {n: given["m_" + n] for n in WEIGHTS}
    moments_v = {n: given["v_" + n] for n in WEIGHTS}
    core = lax.axis_index("c").astype(jnp.int32).reshape(1)
    chip = (2 * lax.axis_index("x") + lax.axis_index("y")).astype(jnp.int32).reshape(1)

    d = x.shape[-1]
    token = ()
    gathering = {}
    for tag, names in GATHERS:
        gathering[tag], tok = _gather_group(names, weights, chip, tag, token)
        token = (tok,)
    gain = {n: weights[n] for n in GAINS}
    sgu_w3 = sgu_w[0]
    sgu_b3 = sgu_b[0][:, :, None]
    groups = dict(GROUPS + GATHERS)

    h0 = x[0]
    tgt = loss_target[0]
    p_bf = p[0, 0].astype(BF16)
    w = _gathered(gathering["ffn1_in"], groups["ffn1_in"], token[0], "ffn1_in")
    xn1 = rms_cast(h0, gain["ffn1_pre_g"], "ffn1_pre_norm")
    s1, h1, xn2 = _ffn_fwd(xn1, h0, w, "ffn1_", gain["ffn1_post_g"], gain["mix_pre_g"], "ffn1",
                           lambda a4: (_gathered(gathering["ffn1_out"], groups["ffn1_out"], a4, "ffn1_out"), ()))
    w.update(_gathered(gathering["mixer"], groups["mixer"], h1, "mixer"))
    full = {n: w[n].reshape(-1, d) for n in ("w_out_a", "w_out_b", "w_o")}
    n_groups = pool_w.shape[1]
    rows_per = pool_w.shape[2]
    dgp = pool_w.shape[3]
    pool_full = w["pool_w"].reshape(N_CHIPS, n_groups, rows_per, dgp).transpose(1, 0, 2, 3).reshape(n_groups, N_CHIPS * rows_per, dgp)
    z = mixer_in(xn2, w["w_in"], "mixer_in")
    a = sgu_fwd(z, gain["sgu_norm_g"], sgu_w3, sgu_b3, "sgu_fwd")
    diff, b = pool_fwd(z, pool_full, gain["pool_scale"], "pool_fwd")
    ya, yb, y = mixer_y(a, b, z, full["w_out_a"], full["w_out_b"], "mixer_y")
    forwarding, tok = _forward_early(gathering["ffn2"], y, "ffn2")
    m, h2, xn3 = mm_norm_res(y[None], full["w_o"][None], h1, gain["mix_post_g"], gain["ffn2_pre_g"], 1.0, "mixer_out",
                             deps=(tok,))
    w.update(_forwarded(forwarding, groups["ffn2"], h2, "ffn2"))
    early = {}

    def forward_ple(a4):
        early["ple"], tok_ple = _forward_early(gathering["ple"], a4, "ple")
        return {}, (tok_ple,)

    s2, h3, xn4 = _ffn_fwd(xn3, h2, w, "ffn2_", gain["ffn2_post_g"], gain["ple_pre_g"], "ffn2", forward_ple)
    w.update(_forwarded(early["ple"], groups["ple"], h3, "ple"))
    full["ple_w_gate"] = w["ple_w_gate"].reshape(-1, d)
    proj_full = w["ple_w_proj"].transpose(1, 0, 2).reshape(ple_w_proj.shape[1], -1)

    small_g, big_g = {}, {}
    ds, de, dh3, small_g["ple_post_g"], small_g["ple_pre_g"], loss_part = ple_loss(
        xn4, p_bf, full["ple_w_gate"], proj_full, h3, gain["ple_post_g"], gain["ple_pre_g"], tgt, "ple_loss")
    small_g["loss"] = loss_part
    grad, delta, new_m, new_v = {}, {}, {}, {}
    results = (grad, delta, new_m, new_v)
    held = []
    update = functools.partial(_update, weights=weights, moments_m=moments_m, moments_v=moments_v, core=core, results=results,
                               held=held)

    def reduce_behind(tag, previous):
        exchanging, tok = _exchange_start(groups[tag], big_g, tag)
        if previous is not None:
            shared, tok = _share_begin(previous[1], tok, chip, previous[0])
        scattering, tok_scatter = _scatter_begin(groups[tag], exchanging, tok, core, tag)
        done = () if previous is None else (update(shared, tok_scatter, tag=previous[0]),)
        return (tag, scattering), (tok_scatter,) + done

    big_g["ple_w_gate"] = dw_tn(xn4, ds, 1, "dw_ple_gate").reshape(N_CHIPS, -1, d)
    big_g["ple_w_proj"] = dw_tn(p_bf, de, N_CHIPS, "dw_ple_proj", dy_mode="cols")
    reducing, deps = reduce_behind("ple", None)

    dg4, du4, g2, small_g["ffn2_post_g"] = _ffn_bwd_w(dh3, s2, w, "ffn2_", gain["ffn2_post_g"], "ffn2", deps)
    big_g.update(g2)
    dh2, small_g["ffn2_pre_g"] = _ffn_bwd_x(dh3, dg4, du4, s2, w, "ffn2_", gain["ffn2_pre_g"], "ffn2", ())
    reducing, deps = reduce_behind("ffn2", reducing)

    dm, dya, dyb, dz, da, db, small_g["mix_post_g"] = mixer_bwd_y(
        dh2, m, gain["mix_post_g"], full["w_o"], ya, yb, z, full["w_out_a"], full["w_out_b"], "mixer_bwd_y", deps=deps)
    big_g["w_o"] = dw_tn(y, dm, 1, "dw_o").reshape(N_CHIPS, -1, d)
    big_g["w_out_a"] = dw_tn(a, dya, 1, "dw_out_a").reshape(N_CHIPS, -1, d)
    big_g["w_out_b"] = dw_tn(b, dyb, 1, "dw_out_b").reshape(N_CHIPS, -1, d)
    dz, d_sgu_w, d_sgu_b, small_g["sgu_norm_g"] = sgu_bwd(z, da, dz, gain["sgu_norm_g"], sgu_w3, sgu_b3, "sgu_bwd")
    dz, d_pool_w, small_g["pool_scale"] = pool_bwd(db, diff, dz, pool_full, gain["pool_scale"], "pool_bwd")
    big_g["pool_w"] = d_pool_w.astype(BF16).reshape(n_groups, N_CHIPS, rows_per, dgp).transpose(1, 0, 2, 3).reshape(
        N_CHIPS, n_groups * rows_per, dgp)
    big_g["w_in"] = dw_in_tiles(xn2, dz, N_CHIPS, "dw_in")
    dh1, small_g["mix_pre_g"] = dx_norm_bwd([(dz, w["w_in"], True)], h1, gain["mix_pre_g"], dh2, "mixer_bwd_x")
    reducing, deps = reduce_behind("mixer", reducing)

    dg4, du4, g1, small_g["ffn1_post_g"] = _ffn_bwd_w(dh1, s1, w, "ffn1_", gain["ffn1_post_g"], "ffn1", deps)
    big_g.update(g1)
    reducing, deps = reduce_behind("ffn1", reducing)
    dh0, small_g["ffn1_pre_g"] = _ffn_bwd_x(dh1, dg4, du4, s1, w, "ffn1_", gain["ffn1_pre_g"], "ffn1", deps)
    shared, tok = _share_begin(reducing[1], dh0, chip, "ffn1")
    update(shared, (tok, *[make((tok,)) for make in held]), tag="ffn1")
    small_g["sgu_w"] = d_sgu_w
    small_g["sgu_b"] = d_sgu_b[:, :, 0]

    no_state = {"loss": jnp.zeros((8, 128), F32)}
    packed = small_allreduce_adamw(
        _pack_small(small_g, "grads"), _pack_small({n: weights[n] for n in SMALL} | no_state, "weights"),
        _pack_small({n: moments_m[n] for n in SMALL} | no_state, "m"), _pack_small({n: moments_v[n] for n in SMALL} | no_state, "v"))
    like = {n: weights[n] for n in SMALL}
    for store, block in zip((grad, delta, new_m, new_v), packed):
        store.update(_unpack_small(block, like))

    return (grad["loss"], dh0[None], *[grad[n] for n in WEIGHTS], *[delta[n] for n in WEIGHTS],
            *[new_m[n] for n in WEIGHTS], *[new_v[n] for n in WEIGHTS])
```

```python
import functools

import jax
import jax.numpy as jnp
from jax import lax
from jax.experimental import pallas as pl
from jax.experimental.pallas import tpu as pltpu
from jax.experimental.pallas import tpu_sc as plsc

F32 = jnp.float32
BF16 = jnp.bfloat16
EPS = 1e-6
CHUNK = 128
POOL_WINDOWS = (2, 4, 8, 16)
HALO = 16
N_CHIPS = 4
ADAM_LR, ADAM_B1, ADAM_B2, ADAM_EPS, ADAM_WD, ADAM_STEP = 0.001, 0.9, 0.999, 1e-08, 0.01, 10
VMEM_LIMIT_V7X = 58 * 1024 * 1024
MESH_IDS = pl.DeviceIdType.MESH
HBM_SPEC = pl.BlockSpec(memory_space=pltpu.HBM)
VMEM_SPEC = pl.BlockSpec(memory_space=pltpu.VMEM)
SEM_SPEC = pl.BlockSpec(memory_space=pltpu.SEMAPHORE)
ANY_SPEC = pl.BlockSpec(memory_space=pl.ANY)
DATAFLOW = pltpu.SideEffectType.DATAFLOW_SIDE_EFFECTING
OTHER_CHIPS = ((1, 0), (0, 1), (1, 1))
DZ_SLOT = (2, 3, 4, 0, 1)
SC_TILES, SC_LANES, SC_ROWS = 32, 16, 8
DW_TOKENS = 4096
DW_IN_TILE = 256

NT = (((1,), (1,)), ((), ()))
TN = (((0,), (0,)), ((), ()))


def _params(*sem, **more):
    return pltpu.CompilerParams(dimension_semantics=sem or None, vmem_limit_bytes=VMEM_LIMIT_V7X, **more)


def _tile(t, want):
    return max(c for c in range(8, min(t, want) + 1, 8) if t % c == 0)


def _const(shape):
    return pl.BlockSpec(shape, lambda *_: (0,) * len(shape))


def _rows(tm, d, col=0):
    return pl.BlockSpec((tm, d), lambda i: (i, col))


def _kmajor(nk, tm, kb):
    return pl.BlockSpec((nk, tm, kb), lambda i: (0, i, 0))


def _dot(a, b):
    return jnp.dot(a, b, preferred_element_type=F32)


def _dot_nt(a, b):
    return lax.dot_general(a, b, NT, preferred_element_type=F32)


def _dot_tn(a, b):
    return lax.dot_general(a, b, TN, preferred_element_type=F32)


def _gelu(x):
    return 0.5 * x * (1.0 + jnp.tanh(0.7978845608028654 * (x + 0.044715 * x * x * x)))


def _gelu_and_grad(x):
    x2 = x * x
    th = jnp.tanh(0.7978845608028654 * (x + 0.044715 * x * x2))
    cdf = 0.5 * (1.0 + th)
    return x * cdf, cdf + 0.5 * x * (1.0 - th * th) * 0.7978845608028654 * (1.0 + 3.0 * 0.044715 * x2)


def _sigmoid(x):
    return 1.0 / (1.0 + jnp.exp(-x))


def _rstd(h):
    return lax.rsqrt(jnp.mean(h * h, axis=-1, keepdims=True) + EPS)


def _rms_bwd(h, g, dy):
    r = _rstd(h)
    t = dy * g
    dh = r * t - h * (r * r * r) * jnp.mean(h * t, axis=-1, keepdims=True)
    return dh, jnp.sum(dy * h * r, axis=0, keepdims=True)


def _ordered_after(body, n_in, deps):
    if not deps:
        return body
    return lambda *refs: body(*refs[:n_in], *refs[n_in + len(deps):])


def _accumulate(ref, value):
    @pl.when(pl.program_id(0) == 0)
    def _():
        ref[...] = jnp.zeros_like(ref)

    ref[...] += value


def rms_cast(h, g, name):
    t, d = h.shape
    tm = _tile(t, 512)

    def body(h_ref, g_ref, o_ref):
        hv = h_ref[...]
        o_ref[...] = (hv * _rstd(hv) * g_ref[...]).astype(BF16)

    return pl.pallas_call(
        body, name=name, grid=(t // tm,), in_specs=[_rows(tm, d), _const((1, d))], out_specs=_rows(tm, d),
        out_shape=jax.ShapeDtypeStruct((t, d), BF16), compiler_params=_params("parallel"))(h, g)


def ffn_gu(xn, wgt4, wut4, name):
    t, d = xn.shape
    nk, fk, _ = wgt4.shape
    tm = _tile(t, 512)

    def body(x_ref, wg_ref, wu_ref, a_ref, s_ref, t_ref):
        xv = x_ref[...]
        g = _dot_nt(xv, wg_ref[0])
        u = _dot_nt(xv, wu_ref[0])
        sg = _sigmoid(g)
        s = g * sg
        a_ref[0] = (s * u).astype(BF16)
        s_ref[0] = s.astype(BF16)
        t_ref[0] = (u * sg * (1.0 + g * (1.0 - sg))).astype(BF16)

    w_spec = pl.BlockSpec((1, fk, d), lambda k, i: (k, 0, 0))
    o_spec = pl.BlockSpec((1, tm, fk), lambda k, i: (k, i, 0))
    shape = jax.ShapeDtypeStruct((nk, t, fk), BF16)
    return pl.pallas_call(
        body, name=name, grid=(nk, t // tm), in_specs=[pl.BlockSpec((tm, d), lambda k, i: (i, 0)), w_spec, w_spec],
        out_specs=[o_spec] * 3, out_shape=[shape] * 3, compiler_params=_params("parallel", "parallel"))(xn, wgt4, wut4)


def mm_norm_res(a3, w3, h_old, g_post, g_next, scale, name, deps=()):
    nk, t, kb = a3.shape
    d = w3.shape[2]
    tm = _tile(t, 256)

    def body(a_ref, w_ref, h_ref, gp_ref, gn_ref, f_ref, hn_ref, xn_ref):
        f = _dot(a_ref[0], w_ref[0])
        for k in range(1, nk):
            f += _dot(a_ref[k], w_ref[k])
        f_ref[...] = f
        hn = h_ref[...] + scale * (f * _rstd(f) * gp_ref[...])
        hn_ref[...] = hn
        xn_ref[...] = (hn * _rstd(hn) * gn_ref[...]).astype(BF16)

    return pl.pallas_call(
        _ordered_after(body, 5, deps), name=name, grid=(t // tm,),
        in_specs=[_kmajor(nk, tm, kb), _const((nk, kb, d)), _rows(tm, d), _const((1, d)), _const((1, d))] + [ANY_SPEC] * len(deps),
        out_specs=[_rows(tm, d)] * 3,
        out_shape=[jax.ShapeDtypeStruct((t, d), F32), jax.ShapeDtypeStruct((t, d), F32), jax.ShapeDtypeStruct((t, d), BF16)],
        compiler_params=_params("parallel"))(a3, w3, h_old, g_post, g_next, *deps)


def mixer_in(xn, win4, name):
    t, d = xn.shape
    nk, _, nb = win4.shape
    tm = _tile(t, 512)

    def body(x_ref, w_ref, z_ref):
        z_ref[...] = _dot(x_ref[...], w_ref[0]).astype(BF16)

    return pl.pallas_call(
        body, name=name, grid=(nk, t // tm),
        in_specs=[pl.BlockSpec((tm, d), lambda k, i: (i, 0)), pl.BlockSpec((1, d, nb), lambda k, i: (k, 0, 0))],
        out_specs=pl.BlockSpec((tm, nb), lambda k, i: (i, k)), out_shape=jax.ShapeDtypeStruct((t, nk * nb), BF16),
        compiler_params=_params("parallel", "parallel"))(xn, win4)


def _causal_mask():
    row = lax.broadcasted_iota(jnp.int32, (CHUNK, CHUNK), 0)
    col = lax.broadcasted_iota(jnp.int32, (CHUNK, CHUNK), 1)
    return row >= col


def _layernorm_parts(v):
    mu = jnp.mean(v, axis=-1, keepdims=True)
    vc = v - mu
    r = lax.rsqrt(jnp.mean(vc * vc, axis=-1, keepdims=True) + EPS)
    return vc * r, r


def sgu_fwd(z, norm_g, sgu_w, sgu_b3, name):
    t = z.shape[0]
    d = norm_g.shape[1]
    ng = sgu_w.shape[0]
    dg = d // ng
    tm = _tile(t, 256)

    def body(zu_ref, zv_ref, ng_ref, w_ref, b_ref, a_ref):
        vhat, _ = _layernorm_parts(_gelu(zv_ref[...].astype(F32)))
        vn = (vhat * ng_ref[...]).astype(BF16)
        u = _gelu(zu_ref[...].astype(F32))
        mask = _causal_mask()
        for g in range(ng):
            wg = jnp.where(mask, w_ref[g], 0.0).astype(BF16)
            for ci in range(tm // CHUNK):
                rs, cs = slice(ci * CHUNK, (ci + 1) * CHUNK), slice(g * dg, (g + 1) * dg)
                sv = _dot(wg, vn[rs, cs]) + b_ref[g]
                a_ref[rs, cs] = (u[rs, cs] * sv).astype(BF16)

    return pl.pallas_call(
        body, name=name, grid=(t // tm,),
        in_specs=[_rows(tm, d, 0), _rows(tm, d, 1), _const((1, d)), _const((ng, CHUNK, CHUNK)), _const((ng, CHUNK, 1))],
        out_specs=_rows(tm, d), out_shape=jax.ShapeDtypeStruct((t, d), BF16),
        compiler_params=_params("parallel"))(z, z, norm_g, sgu_w, sgu_b3)


def pool_fwd(z, pool_w, pool_scale, name):
    t = z.shape[0]
    d = pool_scale.shape[1]
    ng = pool_w.shape[0]
    dg = d // ng
    tm = _tile(t, 256)
    per = tm // HALO

    def body(c_ref, prev_ref, w_ref, s_ref, diff_ref, b_ref):
        i = pl.program_id(0)
        cur = c_ref[...].astype(F32)
        prev = jnp.where(i > 0, prev_ref[...].astype(F32), 0.0)
        ext = jnp.concatenate([prev, cur], axis=0)
        tok = i * tm + lax.broadcasted_iota(jnp.int32, (tm, 1), 0)
        for g, win in enumerate(POOL_WINDOWS):
            cs = slice(g * dg, (g + 1) * dg)
            s = ext[:, cs]
            sh = 1
            while sh < win:
                s = s + pltpu.roll(s, sh, 0)
                sh *= 2
            cnt = jnp.minimum(tok + 1, win).astype(F32)
            diff = (s[HALO:] / cnt - cur[:, cs]).astype(BF16)
            diff_ref[:, cs] = diff
            b_ref[:, cs] = (_dot(diff, w_ref[g]) * s_ref[:, cs]).astype(BF16)

    return pl.pallas_call(
        body, name=name, grid=(t // tm,),
        in_specs=[_rows(tm, d, 2), pl.BlockSpec((HALO, d), lambda i: (jnp.maximum(i * per - 1, 0), 2)),
                  _const((ng, dg, dg)), _const((1, d))],
        out_specs=[_rows(tm, d)] * 2, out_shape=[jax.ShapeDtypeStruct((t, d), BF16)] * 2,
        compiler_params=_params("parallel"))(z, z, pool_w, pool_scale)


def mixer_y(a, b, z, woa, wob, name):
    t, d = a.shape
    tm = _tile(t, 256)

    def body(a_ref, b_ref, ga_ref, gb_ref, wa_ref, wb_ref, ya_ref, yb_ref, y_ref):
        ya = _dot(a_ref[...], wa_ref[...])
        yb = _dot(b_ref[...], wb_ref[...])
        ya_ref[...] = ya.astype(BF16)
        yb_ref[...] = yb.astype(BF16)
        y_ref[...] = (_sigmoid(ga_ref[...].astype(F32)) * ya + _sigmoid(gb_ref[...].astype(F32)) * yb).astype(BF16)

    return pl.pallas_call(
        body, name=name, grid=(t // tm,),
        in_specs=[_rows(tm, d), _rows(tm, d), _rows(tm, d, 3), _rows(tm, d, 4), _const((d, d)), _const((d, d))],
        out_specs=[_rows(tm, d)] * 3, out_shape=[jax.ShapeDtypeStruct((t, d), BF16)] * 3,
        compiler_params=_params("parallel"))(a, b, z, z, woa, wob)


def ple_loss(xn, p, wpg, wpp, h, g_post, g_pre, target, name):
    t, d = xn.shape
    dp = p.shape[1]
    tm = _tile(t, 256)

    def body(x_ref, p_ref, wg_ref, wp_ref, h_ref, gp_ref, gn_ref, tg_ref, ds_ref, de_ref, dhp_ref, dgp_ref, dgn_ref, loss_ref):
        gate = _sigmoid(_dot(x_ref[...], wg_ref[...]))
        e = _dot(p_ref[...], wp_ref[...])
        q = gate * e
        hv = h_ref[...]
        err = hv + q * _rstd(q) * gp_ref[...] - tg_ref[...]
        _accumulate(loss_ref, jnp.full(loss_ref.shape, (0.5 / d) * jnp.sum(err * err), F32))
        dhv = err * (1.0 / d)
        dq, dgp = _rms_bwd(q, gp_ref[...], dhv)
        ds = (dq * e * gate * (1.0 - gate)).astype(BF16)
        ds_ref[...] = ds
        de_ref[...] = (dq * gate).astype(BF16)
        dx, dgn = _rms_bwd(hv, gn_ref[...], _dot_nt(ds, wg_ref[...]))
        dhp_ref[...] = dhv + dx
        _accumulate(dgp_ref, dgp)
        _accumulate(dgn_ref, dgn)

    return pl.pallas_call(
        body, name=name, grid=(t // tm,),
        in_specs=[_rows(tm, d), _rows(tm, dp), _const((d, d)), _const((dp, d)), _rows(tm, d), _const((1, d)), _const((1, d)),
                  _rows(tm, d)],
        out_specs=[_rows(tm, d)] * 3 + [_const((1, d))] * 2 + [_const((8, 128))],
        out_shape=[jax.ShapeDtypeStruct((t, d), BF16), jax.ShapeDtypeStruct((t, d), BF16), jax.ShapeDtypeStruct((t, d), F32),
                   jax.ShapeDtypeStruct((1, d), F32), jax.ShapeDtypeStruct((1, d), F32), jax.ShapeDtypeStruct((8, 128), F32)],
        compiler_params=_params("arbitrary"))(xn, p, wpg, wpp, h, g_post, g_pre, target)


def ffn_bwd_a(dh, f, g_post, wd4, s4, t4, scale, name, deps=()):
    t, d = dh.shape
    nk, fk, _ = wd4.shape
    tm = _tile(t, 256)

    def body(dh_ref, f_ref, gp_ref, w_ref, s_ref, t_ref, df_ref, dg_ref, du_ref, dgp_ref):
        df, dgp = _rms_bwd(f_ref[...], gp_ref[...], dh_ref[...])
        df = (scale * df).astype(BF16)
        df_ref[...] = df
        _accumulate(dgp_ref, scale * dgp)
        for k in range(nk):
            da = _dot_nt(df, w_ref[k])
            du_ref[k] = (da * s_ref[k].astype(F32)).astype(BF16)
            dg_ref[k] = (da * t_ref[k].astype(F32)).astype(BF16)

    return pl.pallas_call(
        _ordered_after(body, 6, deps), name=name, grid=(t // tm,),
        in_specs=[_rows(tm, d), _rows(tm, d), _const((1, d)), _const((nk, fk, d)), _kmajor(nk, tm, fk), _kmajor(nk, tm, fk)]
        + [ANY_SPEC] * len(deps),
        out_specs=[_rows(tm, d), _kmajor(nk, tm, fk), _kmajor(nk, tm, fk), _const((1, d))],
        out_shape=[jax.ShapeDtypeStruct((t, d), BF16), jax.ShapeDtypeStruct((nk, t, fk), BF16),
                   jax.ShapeDtypeStruct((nk, t, fk), BF16), jax.ShapeDtypeStruct((1, d), F32)],
        compiler_params=_params("arbitrary"))(dh, f, g_post, wd4, s4, t4, *deps)


def dx_norm_bwd(pairs, h, g_pre, dh_in, name, deps=()):
    t, d = h.shape
    tm = _tile(t, 256)
    n = len(pairs)

    def body(*refs):
        dys, ws = refs[:n], refs[n:2 * n]
        h_ref, g_ref, dhi_ref, dho_ref, dg_ref = refs[2 * n:]
        acc = None
        for (_, w4, sections), dy_ref, w_ref in zip(pairs, dys, ws):
            if sections:
                wide = w4.shape[2]
                edges = sorted(set(range(0, 5 * d + 1, d)) | set(range(0, 5 * d + 1, wide)))
                parts = [_dot_nt(dy_ref[DZ_SLOT[lo // d], :, lo % d:lo % d + hi - lo], w_ref[lo // wide, :, lo % wide:lo % wide + hi - lo])
                         for lo, hi in zip(edges[:-1], edges[1:])]
            else:
                parts = [_dot(dy_ref[k], w_ref[k]) for k in range(w4.shape[0])]
            for part in parts:
                acc = part if acc is None else acc + part
        dx, dg = _rms_bwd(h_ref[...], g_ref[...], acc)
        dho_ref[...] = dhi_ref[...] + dx
        _accumulate(dg_ref, dg)

    dy_specs = [_kmajor(dy.shape[0], tm, dy.shape[2]) for dy, _, _ in pairs]
    return pl.pallas_call(
        _ordered_after(body, 2 * n + 3, deps), name=name, grid=(t // tm,),
        in_specs=dy_specs + [_const(w4.shape) for _, w4, _ in pairs] + [_rows(tm, d), _const((1, d)), _rows(tm, d)]
        + [ANY_SPEC] * len(deps),
        out_specs=[_rows(tm, d), _const((1, d))],
        out_shape=[jax.ShapeDtypeStruct((t, d), F32), jax.ShapeDtypeStruct((1, d), F32)],
        compiler_params=_params("arbitrary"))(*[dy for dy, _, _ in pairs], *[w4 for _, w4, _ in pairs], h, g_pre, dh_in, *deps)


def mixer_bwd_y(dh, m, g_post, w_o, ya, yb, z, woa, wob, name, deps=()):
    t, d = dh.shape
    tm = _tile(t, 256)

    def body(dh_ref, m_ref, gp_ref, wo_ref, ya_ref, yb_ref, ga_ref, gb_ref, wa_ref, wb_ref,
             dm_ref, dya_ref, dyb_ref, dz_ref, da_ref, db_ref, dgp_ref):
        dm, dgp = _rms_bwd(m_ref[...], gp_ref[...], dh_ref[...])
        dm = dm.astype(BF16)
        dm_ref[...] = dm
        _accumulate(dgp_ref, dgp)
        dy = _dot_nt(dm, wo_ref[...])
        sa = _sigmoid(ga_ref[...].astype(F32))
        sb = _sigmoid(gb_ref[...].astype(F32))
        dya = (dy * sa).astype(BF16)
        dyb = (dy * sb).astype(BF16)
        dya_ref[...] = dya
        dyb_ref[...] = dyb
        dz_ref[0] = (dy * ya_ref[...].astype(F32) * sa * (1.0 - sa)).astype(BF16)
        dz_ref[1] = (dy * yb_ref[...].astype(F32) * sb * (1.0 - sb)).astype(BF16)
        da_ref[...] = _dot_nt(dya, wa_ref[...]).astype(BF16)
        db_ref[...] = _dot_nt(dyb, wb_ref[...]).astype(BF16)

    return pl.pallas_call(
        _ordered_after(body, 10, deps), name=name, grid=(t // tm,),
        in_specs=[_rows(tm, d), _rows(tm, d), _const((1, d)), _const((d, d)), _rows(tm, d), _rows(tm, d),
                  _rows(tm, d, 3), _rows(tm, d, 4), _const((d, d)), _const((d, d))] + [ANY_SPEC] * len(deps),
        out_specs=[_rows(tm, d)] * 3 + [pl.BlockSpec((2, tm, d), lambda i: (0, i, 0))] + [_rows(tm, d)] * 2 + [_const((1, d))],
        out_shape=[jax.ShapeDtypeStruct((t, d), BF16)] * 3 + [jax.ShapeDtypeStruct((5, t, d), BF16)]
        + [jax.ShapeDtypeStruct((t, d), BF16)] * 2 + [jax.ShapeDtypeStruct((1, d), F32)],
        compiler_params=_params("arbitrary"))(dh, m, g_post, w_o, ya, yb, z, z, woa, wob, *deps)


def sgu_bwd(z, da, dz, norm_g, sgu_w, sgu_b3, name):
    t, d = da.shape
    ng = sgu_w.shape[0]
    dg = d // ng
    tm = _tile(t, 256)
    steps = t // tm

    def body(zu_ref, zv_ref, da_ref, ng_ref, w_ref, b_ref, _, dz_ref, dw_ref, db_ref, dng_ref, dvn_ref, dsv_ref):
        i = pl.program_id(0)
        zv = zv_ref[...].astype(F32)
        zu = zu_ref[...].astype(F32)
        v, gv = _gelu_and_grad(zv)
        vhat, r = _layernorm_parts(v)
        gain = ng_ref[...]
        vn = (vhat * gain).astype(BF16)
        u, gu = _gelu_and_grad(zu)
        dav = da_ref[...].astype(F32)
        mask = _causal_mask()

        @pl.when(i == 0)
        def _():
            dw_ref[...] = jnp.zeros_like(dw_ref)
            dsv_ref[...] = jnp.zeros_like(dsv_ref)

        for g in range(ng):
            wg = jnp.where(mask, w_ref[g], 0.0).astype(BF16)
            dw = jnp.zeros((CHUNK, CHUNK), F32)
            dsv_sum = jnp.zeros((CHUNK, dg), F32)
            for ci in range(tm // CHUNK):
                rs, cs = slice(ci * CHUNK, (ci + 1) * CHUNK), slice(g * dg, (g + 1) * dg)
                vn_blk = vn[rs, cs]
                sv = _dot(wg, vn_blk) + b_ref[g]
                dz_ref[0, rs, cs] = (dav[rs, cs] * sv * gu[rs, cs]).astype(BF16)
                dsv = dav[rs, cs] * u[rs, cs]
                dsv_sum += dsv
                dsv = dsv.astype(BF16)
                dw += _dot_nt(dsv, vn_blk)
                dvn_ref[rs, cs] = _dot_tn(wg, dsv)
            dw_ref[g] += dw
            dsv_ref[:, cs] += dsv_sum

        dvn = dvn_ref[...]
        _accumulate(dng_ref, jnp.sum(dvn * vhat, axis=0, keepdims=True))
        dvh = dvn * gain
        dv = r * (dvh - jnp.mean(dvh, axis=-1, keepdims=True) - vhat * jnp.mean(dvh * vhat, axis=-1, keepdims=True))
        dz_ref[1] = (dv * gv).astype(BF16)

        @pl.when(i == steps - 1)
        def _():
            for g in range(ng):
                dw_ref[g] = jnp.where(mask, dw_ref[g], 0.0)
                row_sum = jnp.sum(dsv_ref[:, g * dg:(g + 1) * dg], axis=1, keepdims=True)
                db_ref[g] = jnp.broadcast_to(row_sum, (CHUNK, CHUNK))

    return pl.pallas_call(
        body, name=name, grid=(steps,),
        in_specs=[_rows(tm, d, 0), _rows(tm, d, 1), _rows(tm, d), _const((1, d)), _const((ng, CHUNK, CHUNK)), _const((ng, CHUNK, 1)),
                  ANY_SPEC],
        out_specs=[pl.BlockSpec((2, tm, d), lambda i: (DZ_SLOT[0] // 2, i, 0)), _const((ng, CHUNK, CHUNK)),
                   _const((ng, CHUNK, CHUNK)), _const((1, d))],
        out_shape=[jax.ShapeDtypeStruct(dz.shape, BF16), jax.ShapeDtypeStruct((ng, CHUNK, CHUNK), F32),
                   jax.ShapeDtypeStruct((ng, CHUNK, CHUNK), F32), jax.ShapeDtypeStruct((1, d), F32)],
        scratch_shapes=[pltpu.VMEM((tm, d), F32), pltpu.VMEM((CHUNK, d), F32)], input_output_aliases={6: 0},
        compiler_params=_params("arbitrary"))(z, z, da, norm_g, sgu_w, sgu_b3, dz)


def pool_bwd(db, diff, dz, pool_w, pool_scale, name):
    t, d = db.shape
    ng = pool_w.shape[0]
    dg = d // ng
    tm = _tile(t, 256)
    per = tm // HALO
    steps = t // tm

    def body(db_ref, next_ref, diff_ref, w_ref, s_ref, _, dc_ref, dw_ref, ds_ref):
        i = pl.program_id(0)
        dbc = db_ref[...].astype(F32)
        nxt = jnp.where(i < steps - 1, next_ref[...].astype(F32), 0.0)
        ext = jnp.concatenate([dbc, nxt], axis=0)
        rows = tm + HALO
        tok = i * tm + lax.broadcasted_iota(jnp.int32, (rows, 1), 0)

        @pl.when(i == 0)
        def _():
            dw_ref[...] = jnp.zeros_like(dw_ref)
            ds_ref[...] = jnp.zeros_like(ds_ref)

        for g, win in enumerate(POOL_WINDOWS):
            cs = slice(g * dg, (g + 1) * dg)
            dp = (ext[:, cs] * s_ref[:, cs]).astype(BF16)
            dd = _dot_nt(dp, w_ref[g])
            s = dd / jnp.minimum(tok + 1, win).astype(F32)
            sh = 1
            while sh < win:
                s = s + pltpu.roll(s, rows - sh, 0)
                sh *= 2
            dc_ref[0, :, cs] = (s[:tm] - dd[:tm]).astype(BF16)
            dfg = diff_ref[:, cs]
            ds_ref[:, cs] += jnp.sum(dbc[:, cs] * _dot(dfg, w_ref[g]), axis=0, keepdims=True)
            dw_ref[g] += _dot_tn(dfg, dp[:tm])

    return pl.pallas_call(
        body, name=name, grid=(steps,),
        in_specs=[_rows(tm, d), pl.BlockSpec((HALO, d), lambda i: (jnp.minimum((i + 1) * per, t // HALO - 1), 0)),
                  _rows(tm, d), _const((ng, dg, dg)), _const((1, d)), ANY_SPEC],
        out_specs=[pl.BlockSpec((1, tm, d), lambda i: (DZ_SLOT[2], i, 0)), _const((ng, dg, dg)), _const((1, d))],
        out_shape=[jax.ShapeDtypeStruct(dz.shape, BF16), jax.ShapeDtypeStruct((ng, dg, dg), F32), jax.ShapeDtypeStruct((1, d), F32)],
        input_output_aliases={5: 0},
        compiler_params=_params("arbitrary"))(db, db, diff, pool_w, pool_scale, dz)


def dw_tn(x, dy, nk, name, x_kmajor=False, dy_mode="same"):
    t = x.shape[-2]
    kx = x.shape[-1]
    n = dy.shape[-1] // nk if dy_mode == "cols" else dy.shape[-1]
    tt = _tile(t, DW_TOKENS)
    steps = t // tt

    def body(x_ref, dy_ref, o_ref, acc_ref):
        s = pl.program_id(1)
        part = _dot_tn(x_ref[0] if x_kmajor else x_ref[...], dy_ref[0] if dy_mode == "kmajor" else dy_ref[...])
        if steps == 1:
            o_ref[0] = part.astype(BF16)
            return

        @pl.when(s == 0)
        def _():
            acc_ref[...] = jnp.zeros_like(acc_ref)

        acc_ref[...] += part

        @pl.when(s == steps - 1)
        def _():
            o_ref[0] = acc_ref[...].astype(BF16)

    x_spec = pl.BlockSpec((1, tt, kx), lambda k, s: (k, s, 0)) if x_kmajor else pl.BlockSpec((tt, kx), lambda k, s: (s, 0))
    dy_spec = {"kmajor": pl.BlockSpec((1, tt, n), lambda k, s: (k, s, 0)), "cols": pl.BlockSpec((tt, n), lambda k, s: (s, k)),
               "same": pl.BlockSpec((tt, n), lambda k, s: (s, 0))}[dy_mode]
    return pl.pallas_call(
        body, name=name, grid=(nk, steps), in_specs=[x_spec, dy_spec],
        out_specs=pl.BlockSpec((1, kx, n), lambda k, s: (k, 0, 0)), out_shape=jax.ShapeDtypeStruct((nk, kx, n), BF16),
        scratch_shapes=[pltpu.VMEM((kx, n) if steps > 1 else (8, 128), F32)],
        compiler_params=_params("parallel", "arbitrary"))(x, dy)


def dw_in_tiles(xn, dz, nk, name):
    t, d = xn.shape
    sections = len(DZ_SLOT)
    per_section = d // DW_IN_TILE
    per_shard = sections * per_section // nk

    def body(x_ref, dy_ref, o_ref):
        o_ref[0] = _dot_tn(x_ref[...], dy_ref[0]).astype(BF16)

    def slot(j):
        return (j // per_section + DZ_SLOT[0]) % sections

    return pl.pallas_call(
        body, name=name, grid=(sections * per_section,),
        in_specs=[pl.BlockSpec((t, d), lambda j: (0, 0)), pl.BlockSpec((1, t, DW_IN_TILE), lambda j: (slot(j), 0, j % per_section))],
        out_specs=pl.BlockSpec((1, d, DW_IN_TILE), lambda j: (j // per_shard, 0, j % per_shard)),
        out_shape=jax.ShapeDtypeStruct((nk, d, sections * d // nk), BF16), compiler_params=_params("parallel"))(xn, dz)


def _place():
    x, y, c = lax.axis_index("x"), lax.axis_index("y"), lax.axis_index("c")
    chips = [((1 - x) if fx else x, (1 - y) if fy else y) for fx, fy in OTHER_CHIPS]
    return x, y, c, chips


def _half(rows, which):
    return pl.ds(pl.multiple_of(which * (rows // 2), 16), rows // 2)


def _hbm(a):
    return pltpu.with_memory_space_constraint(a, pltpu.HBM)


def _by_shape(arrays):
    buckets = {}
    for i, a in enumerate(arrays):
        buckets.setdefault(a.shape, []).append(i)
    return list(buckets.values())


def cast_place(shards, chip, name, deps=()):
    n = len(shards)
    r, cdim = shards[0].shape
    tr = _tile(r, 256)

    def body(chip_ref, *refs):
        for w_ref, o_ref in zip(refs[:n], refs[n + len(deps):]):
            o_ref[0] = w_ref[...].astype(BF16)

    return pl.pallas_call(
        body, name=name,
        grid_spec=pltpu.PrefetchScalarGridSpec(
            num_scalar_prefetch=1, grid=(r // tr,),
            in_specs=[pl.BlockSpec((tr, cdim), lambda i, chip_ref: (i, 0))] * n + [ANY_SPEC] * len(deps),
            out_specs=[pl.BlockSpec((1, tr, cdim), lambda i, chip_ref: (chip_ref[0], i, 0))] * n),
        out_shape=[jax.ShapeDtypeStruct((N_CHIPS, r, cdim), BF16)] * n, compiler_params=_params("parallel"))(chip, *shards, *deps)


def _gather_copy(buf, sends, recvs, i, j, me, chip_xy, c):
    cx, cy = chip_xy
    mine = _half(buf.shape[1], c)
    return pltpu.make_async_remote_copy(
        src_ref=buf.at[me, mine], dst_ref=buf.at[me, mine], send_sem=sends.at[3 * i + j], recv_sem=recvs.at[3 * i + j],
        device_id=(cx, cy, c), device_id_type=MESH_IDS)


def allgather_start(bufs, name):
    n = len(bufs)

    def body(*refs):
        ins = refs[:n]
        sends, recvs = refs[n], refs[n + 1]
        token = refs[2 * n + 2]
        x, y, c, chips = _place()
        for i in range(n):
            for j, chip_xy in enumerate(chips):
                _gather_copy(ins[i], sends, recvs, i, j, 2 * x + y, chip_xy, c).start()
        token[...] = jnp.zeros_like(token)

    out = pl.pallas_call(
        body, name=name, in_specs=[HBM_SPEC] * n,
        out_specs=[SEM_SPEC, SEM_SPEC] + [HBM_SPEC] * n + [VMEM_SPEC],
        out_shape=[pltpu.SemaphoreType.DMA((3 * n,)), pltpu.SemaphoreType.DMA((3 * n,))]
        + [pltpu.HBM(b.shape, b.dtype) for b in bufs] + [jax.ShapeDtypeStruct((8, 128), F32)],
        input_output_aliases={i: i + 2 for i in range(n)},
        compiler_params=pltpu.CompilerParams(has_side_effects=DATAFLOW))(*[_hbm(b) for b in bufs])
    return out[0], out[1], list(out[2:2 + n]), out[2 + n]


def allgather_wait(sends, recvs, bufs, after, name):
    n = len(bufs)

    def body(*refs):
        ins = refs[:n]
        send_sems, recv_sems = refs[n], refs[n + 1]
        x, y, c, chips = _place()
        for i in range(n):
            for j, (cx, cy) in enumerate(chips):
                mine = _half(ins[i].shape[1], c)
                cp = pltpu.make_async_remote_copy(
                    src_ref=ins[i].at[2 * x + y, mine], dst_ref=ins[i].at[2 * cx + cy, mine], send_sem=send_sems.at[3 * i + j],
                    recv_sem=recv_sems.at[3 * i + j], device_id=(cx, cy, c), device_id_type=MESH_IDS)
                cp.wait_send()
                cp.wait_recv()

    return pl.pallas_call(
        body, name=name, in_specs=[HBM_SPEC] * n + [SEM_SPEC, SEM_SPEC, ANY_SPEC], out_specs=[HBM_SPEC] * n,
        out_shape=[pltpu.HBM(b.shape, b.dtype) for b in bufs], input_output_aliases={i: i for i in range(n)},
        compiler_params=pltpu.CompilerParams(has_side_effects=DATAFLOW))(*bufs, sends, recvs, after)


def d2d_forward(bufs, name):
    n = len(bufs)

    def body(*refs):
        ins = refs[:n]
        send_sems, recv_sems = refs[2 * n:]
        x, y, c, chips = _place()
        copies = []
        for i in range(n):
            mine = _half(ins[i].shape[1], c)
            for j, (cx, cy) in enumerate(chips):
                landed = ins[i].at[2 * cx + cy, mine]
                cp = pltpu.make_async_remote_copy(
                    src_ref=landed, dst_ref=landed, send_sem=send_sems.at[i, j], recv_sem=recv_sems.at[i, j],
                    device_id=(x, y, 1 - c), device_id_type=MESH_IDS)
                cp.start()
                copies.append(cp)
        for i in range(n):
            theirs = _half(ins[i].shape[1], 1 - c)
            for j, (cx, cy) in enumerate(chips):
                passed = ins[i].at[2 * cx + cy, theirs]
                pltpu.make_async_remote_copy(
                    src_ref=passed, dst_ref=passed, send_sem=send_sems.at[i, j], recv_sem=recv_sems.at[i, j],
                    device_id=(x, y, 1 - c), device_id_type=MESH_IDS).wait_recv()
        for cp in copies:
            cp.wait_send()

    return pl.pallas_call(
        body, name=name, in_specs=[HBM_SPEC] * n, out_specs=[HBM_SPEC] * n,
        out_shape=[jax.ShapeDtypeStruct(b.shape, b.dtype) for b in bufs], input_output_aliases={i: i for i in range(n)},
        scratch_shapes=[pltpu.SemaphoreType.DMA((n, 3))] * 2,
        compiler_params=pltpu.CompilerParams(has_side_effects=True))(*bufs)


def _forward_copy(buf, sends, recvs, i, j, chip_xy, x, y, c):
    cx, cy = chip_xy
    rows = buf.shape[1]
    return pltpu.make_async_remote_copy(
        src_ref=buf.at[2 * cx + cy, _half(rows, c)], dst_ref=buf.at[2 * cx + cy, _half(rows, 1 - c)],
        send_sem=sends.at[3 * i + j], recv_sem=recvs.at[3 * i + j], device_id=(x, y, 1 - c), device_id_type=MESH_IDS)


def d2d_forward_start(bufs, name):
    n = len(bufs)

    def body(*refs):
        ins = refs[:n]
        sends, recvs = refs[n], refs[n + 1]
        token = refs[2 * n + 2]
        x, y, c, chips = _place()
        for i in range(n):
            mine = _half(ins[i].shape[1], c)
            for j, (cx, cy) in enumerate(chips):
                landed = ins[i].at[2 * cx + cy, mine]
                pltpu.make_async_remote_copy(
                    src_ref=landed, dst_ref=landed, send_sem=sends.at[3 * i + j], recv_sem=recvs.at[3 * i + j],
                    device_id=(x, y, 1 - c), device_id_type=MESH_IDS).start()
        token[...] = jnp.zeros_like(token)

    out = pl.pallas_call(
        body, name=name, in_specs=[HBM_SPEC] * n,
        out_specs=[SEM_SPEC, SEM_SPEC] + [HBM_SPEC] * n + [VMEM_SPEC],
        out_shape=[pltpu.SemaphoreType.DMA((3 * n,)), pltpu.SemaphoreType.DMA((3 * n,))]
        + [pltpu.HBM(b.shape, b.dtype) for b in bufs] + [jax.ShapeDtypeStruct((8, 128), F32)],
        input_output_aliases={i: i + 2 for i in range(n)},
        compiler_params=pltpu.CompilerParams(has_side_effects=DATAFLOW))(*[_hbm(b) for b in bufs])
    return (out[0], out[1], list(out[2:2 + n])), out[2 + n]


def d2d_forward_wait(started, after, name):
    sends, recvs, bufs = started
    n = len(bufs)

    def body(*refs):
        ins = refs[:n]
        send_sems, recv_sems = refs[n], refs[n + 1]
        x, y, c, chips = _place()
        for i in range(n):
            for j, chip_xy in enumerate(chips):
                cp = _forward_copy(ins[i], send_sems, recv_sems, i, j, chip_xy, x, y, c)
                cp.wait_send()
                cp.wait_recv()

    return pl.pallas_call(
        body, name=name, in_specs=[HBM_SPEC] * n + [SEM_SPEC, SEM_SPEC, ANY_SPEC], out_specs=[HBM_SPEC] * n,
        out_shape=[pltpu.HBM(b.shape, b.dtype) for b in bufs], input_output_aliases={i: i for i in range(n)},
        compiler_params=pltpu.CompilerParams(has_side_effects=DATAFLOW))(*bufs, sends, recvs, after)


def _sibling_copy(src, land, sends, recvs, i, x, y, c, halves):
    part = src.at[:, _half(src.shape[1], 1 - c)] if halves else src
    return pltpu.make_async_remote_copy(
        src_ref=part, dst_ref=land, send_sem=sends.at[i], recv_sem=recvs.at[i], device_id=(x, y, 1 - c),
        device_id_type=MESH_IDS)


def sibling_start(arrays, halves, name):
    n = len(arrays)
    lands = [lax.empty((a.shape[0], a.shape[1] // 2, a.shape[2]) if halves else a.shape, a.dtype) for a in arrays]

    def body(*refs):
        srcs, zones = refs[:n], refs[n:2 * n]
        sends, recvs = refs[2 * n], refs[2 * n + 1]
        token = refs[4 * n + 2]
        x, y, c, _ = _place()
        for i in range(n):
            _sibling_copy(srcs[i], zones[i], sends, recvs, i, x, y, c, halves).start()
        token[...] = jnp.zeros_like(token)

    out = pl.pallas_call(
        body, name=name, in_specs=[HBM_SPEC] * (2 * n),
        out_specs=[SEM_SPEC, SEM_SPEC] + [HBM_SPEC] * (2 * n) + [VMEM_SPEC],
        out_shape=[pltpu.SemaphoreType.DMA((n,)), pltpu.SemaphoreType.DMA((n,))]
        + [pltpu.HBM(a.shape, a.dtype) for a in arrays + lands] + [jax.ShapeDtypeStruct((8, 128), F32)],
        input_output_aliases={i: i + 2 for i in range(2 * n)},
        compiler_params=pltpu.CompilerParams(has_side_effects=DATAFLOW))(*[_hbm(a) for a in arrays + lands])
    return (out[0], out[1], list(out[2:2 + n]), list(out[2 + n:2 + 2 * n])), out[2 + 2 * n]


def sibling_wait(started, halves, after, name):
    sends, recvs, arrays, lands = started
    n = len(arrays)
    after = tuple(after) if isinstance(after, (tuple, list)) else (after,)

    def body(*refs):
        srcs, zones = refs[:n], refs[n:2 * n]
        send_sems, recv_sems = refs[2 * n], refs[2 * n + 1]
        x, y, c, _ = _place()
        for i in range(n):
            cp = _sibling_copy(srcs[i], zones[i], send_sems, recv_sems, i, x, y, c, halves)
            cp.wait_send()
            cp.wait_recv()

    out = pl.pallas_call(
        body, name=name, in_specs=[HBM_SPEC] * (2 * n) + [SEM_SPEC, SEM_SPEC] + [ANY_SPEC] * len(after),
        out_specs=[HBM_SPEC] * (2 * n),
        out_shape=[pltpu.HBM(a.shape, a.dtype) for a in arrays + lands], input_output_aliases={i: i for i in range(2 * n)},
        compiler_params=pltpu.CompilerParams(has_side_effects=DATAFLOW))(*arrays, *lands, sends, recvs, *after)
    return list(out[:n]), list(out[n:])


def _scatter_copy(src, land, sends, recvs, i, j, chip_xy, c):
    cx, cy = chip_xy
    return pltpu.make_async_remote_copy(
        src_ref=src.at[2 * cx + cy], dst_ref=land.at[j], send_sem=sends.at[3 * i + j], recv_sem=recvs.at[3 * i + j],
        device_id=(cx, cy, c), device_id_type=MESH_IDS)


def scatter_start(sums, name):
    n = len(sums)
    lands = [lax.empty((3,) + s.shape[1:], s.dtype) for s in sums]

    def body(*refs):
        srcs, zones = refs[:n], refs[n:2 * n]
        sends, recvs = refs[2 * n], refs[2 * n + 1]
        token = refs[4 * n + 2]
        _, _, c, chips = _place()
        for i in range(n):
            for j, chip_xy in enumerate(chips):
                _scatter_copy(srcs[i], zones[i], sends, recvs, i, j, chip_xy, c).start()
        token[...] = jnp.zeros_like(token)

    out = pl.pallas_call(
        body, name=name, in_specs=[HBM_SPEC] * (2 * n),
        out_specs=[SEM_SPEC, SEM_SPEC] + [HBM_SPEC] * (2 * n) + [VMEM_SPEC],
        out_shape=[pltpu.SemaphoreType.DMA((3 * n,)), pltpu.SemaphoreType.DMA((3 * n,))]
        + [pltpu.HBM(a.shape, a.dtype) for a in sums + lands] + [jax.ShapeDtypeStruct((8, 128), F32)],
        input_output_aliases={i: i + 2 for i in range(2 * n)},
        compiler_params=pltpu.CompilerParams(has_side_effects=DATAFLOW))(*[_hbm(a) for a in sums + lands])
    return out[0], out[1], list(out[2:2 + n]), list(out[2 + n:2 + 2 * n]), out[2 + 2 * n]


def scatter_wait(sends, recvs, sums, lands, after, name):
    n = len(sums)

    def body(*refs):
        srcs, zones = refs[:n], refs[n:2 * n]
        send_sems, recv_sems = refs[2 * n], refs[2 * n + 1]
        _, _, c, chips = _place()
        for i in range(n):
            for j, chip_xy in enumerate(chips):
                cp = _scatter_copy(srcs[i], zones[i], send_sems, recv_sems, i, j, chip_xy, c)
                cp.wait_send()
                cp.wait_recv()

    out = pl.pallas_call(
        body, name=name, in_specs=[HBM_SPEC] * (2 * n) + [SEM_SPEC, SEM_SPEC, ANY_SPEC], out_specs=[HBM_SPEC] * (2 * n),
        out_shape=[pltpu.HBM(a.shape, a.dtype) for a in sums + lands], input_output_aliases={i: i for i in range(2 * n)},
        compiler_params=pltpu.CompilerParams(has_side_effects=DATAFLOW))(*sums, *lands, sends, recvs, after)
    return list(out[:n]), list(out[n:])


def add_halves(grads, recvs, core, name):
    n = len(grads)
    nk, r, cdim = grads[0].shape
    half = r // 2
    views = [g.reshape(nk, 2, half, cdim) for g in grads]

    def body(core_ref, *refs):
        for g_ref, r_ref, o_ref in zip(refs[:n], refs[n:2 * n], refs[2 * n:]):
            o_ref[0] = (g_ref[0, 0].astype(F32) + r_ref[0].astype(F32)).astype(BF16)

    return pl.pallas_call(
        body, name=name,
        grid_spec=pltpu.PrefetchScalarGridSpec(
            num_scalar_prefetch=1, grid=(nk,),
            in_specs=[pl.BlockSpec((1, 1, half, cdim), lambda k, core_ref: (k, core_ref[0], 0, 0))] * n
            + [pl.BlockSpec((1, half, cdim), lambda k, core_ref: (k, 0, 0))] * n,
            out_specs=[pl.BlockSpec((1, half, cdim), lambda k, core_ref: (k, 0, 0))] * n),
        out_shape=[jax.ShapeDtypeStruct((nk, half, cdim), BF16)] * n, compiler_params=_params("parallel"))(core, *views, *recvs)


def add_chips(sums, lands, chip, name):
    n = len(sums)
    _, half, cdim = sums[0].shape
    tr = _tile(half, 128)

    def body(chip_ref, *refs):
        for s_ref, r_ref, o_ref in zip(refs[:n], refs[n:2 * n], refs[2 * n:]):
            o_ref[...] = ((s_ref[0].astype(F32) + r_ref[0].astype(F32)) + r_ref[1].astype(F32)) + r_ref[2].astype(F32)

    return pl.pallas_call(
        body, name=name,
        grid_spec=pltpu.PrefetchScalarGridSpec(
            num_scalar_prefetch=1, grid=(half // tr,),
            in_specs=[pl.BlockSpec((1, tr, cdim), lambda i, chip_ref: (chip_ref[0], i, 0))] * n
            + [pl.BlockSpec((3, tr, cdim), lambda i, chip_ref: (0, i, 0))] * n,
            out_specs=[pl.BlockSpec((tr, cdim), lambda i, chip_ref: (i, 0))] * n),
        out_shape=[jax.ShapeDtypeStruct((half, cdim), F32)] * n, compiler_params=_params("parallel"))(chip, *sums, *lands)


def _adamw_math(w, g, m, v):
    m = ADAM_B1 * m + (1.0 - ADAM_B1) * g
    v = ADAM_B2 * v + (1.0 - ADAM_B2) * (g * g)
    m_hat = m / (1.0 - ADAM_B1 ** ADAM_STEP)
    v_hat = v / (1.0 - ADAM_B2 ** ADAM_STEP)
    return -ADAM_LR * (m_hat / (jnp.sqrt(v_hat) + ADAM_EPS) + ADAM_WD * w), m, v


def adamw_halves(ws, owns, others, ms, vs, core, name, deps=()):
    n = len(ws)
    r, cdim = ws[0].shape
    half = r // 2
    tr = _tile(half, max(8, 256 // n))
    steps = half // tr

    def body(core_ref, *refs):
        ins, outs = refs[:5 * n], refs[5 * n + len(deps):]
        for a in range(n):
            w_ref, own_ref, other_ref, m_ref, v_ref = ins[a::n]
            g = jnp.where(pl.program_id(0) == core_ref[0], own_ref[...], other_ref[...])
            outs[a][...] = g
            outs[n + a][...], outs[2 * n + a][...], outs[3 * n + a][...] = _adamw_math(w_ref[...], g, m_ref[...], v_ref[...])

    whole = pl.BlockSpec((tr, cdim), lambda hh, i, core_ref: (hh * steps + i, 0))
    own = pl.BlockSpec((tr, cdim), lambda hh, i, core_ref: (jnp.where(hh == core_ref[0], i, 0), 0))
    other = pl.BlockSpec((tr, cdim), lambda hh, i, core_ref: (jnp.where(hh == core_ref[0], 0, i), 0))
    out = pl.pallas_call(
        body, name=name,
        grid_spec=pltpu.PrefetchScalarGridSpec(
            num_scalar_prefetch=1, grid=(2, steps),
            in_specs=[whole] * n + [own] * n + [other] * n + [whole] * (2 * n) + [ANY_SPEC] * len(deps),
            out_specs=[whole] * (4 * n)),
        out_shape=[jax.ShapeDtypeStruct((r, cdim), F32)] * (4 * n),
        compiler_params=_params("parallel", "parallel"))(core, *ws, *owns, *others, *ms, *vs, *deps)
    return [out[a::n] for a in range(n)]


def adamw_halves_sparsecore(ws, owns, others, ms, vs, name):
    n = len(ws)
    r, cdim = ws[0].shape
    half_groups = r // 2 // SC_ROWS
    per_tile = -(-2 * half_groups // SC_TILES)

    def body(*refs):
        ins, outs, (wb, gb, mb, vb) = refs[:5 * n], refs[5 * n:9 * n], refs[9 * n:]
        tile = lax.axis_index("sc_tile") * 2 + lax.axis_index("sc_core")
        core = lax.axis_index("c")

        @pl.loop(0, per_tile)
        def _(it):
            group = tile + SC_TILES * it

            @pl.when(group < 2 * half_groups)
            def _():
                rows = pl.ds(group * SC_ROWS, SC_ROWS)
                in_own = (group < half_groups) == (core == 0)
                half_rows = pl.ds((group % half_groups) * SC_ROWS, SC_ROWS)
                for a in range(n):
                    w_hbm, own_hbm, other_hbm, m_hbm, v_hbm = ins[a::n]
                    g_out, d_out, m_out, v_out = outs[a::n]
                    pltpu.sync_copy(w_hbm.at[rows], wb)
                    pltpu.sync_copy(m_hbm.at[rows], mb)
                    pltpu.sync_copy(v_hbm.at[rows], vb)

                    @pl.when(in_own)
                    def _():
                        pltpu.sync_copy(own_hbm.at[half_rows], gb)

                    @pl.when(jnp.logical_not(in_own))
                    def _():
                        pltpu.sync_copy(other_hbm.at[half_rows], gb)

                    pltpu.sync_copy(gb, g_out.at[rows])

                    @pl.loop(0, SC_ROWS)
                    def _(row):
                        @pl.loop(0, cdim, step=SC_LANES)
                        def _(col):
                            at = (row, pl.ds(col, SC_LANES))
                            wb[at], mb[at], vb[at] = _adamw_math(wb[at], gb[at], mb[at], vb[at])

                    pltpu.sync_copy(wb, d_out.at[rows])
                    pltpu.sync_copy(mb, m_out.at[rows])
                    pltpu.sync_copy(vb, v_out.at[rows])

    out = pl.kernel(
        body, name=name, out_type=[jax.ShapeDtypeStruct((r, cdim), F32)] * (4 * n),
        mesh=plsc.VectorSubcoreMesh(core_axis_name="sc_core", subcore_axis_name="sc_tile"),
        scratch_types=[pltpu.VMEM((SC_ROWS, cdim), F32)] * 4)(*ws, *owns, *others, *ms, *vs)
    return [out[a::n] for a in range(n)]


def small_allreduce_adamw(g, w, m, v):
    r, cdim = g.shape

    def body(g_ref, w_ref, m_ref, v_ref, go_ref, d_ref, mo_ref, vo_ref, pair, quad, d2d_send, d2d_recv, ici_send, ici_recv):
        x, y, c, chips = _place()
        me = 2 * x + y
        pair[c] = g_ref[...]
        swap = pltpu.make_async_remote_copy(
            src_ref=g_ref, dst_ref=pair.at[c], send_sem=d2d_send, recv_sem=d2d_recv, device_id=(x, y, 1 - c),
            device_id_type=MESH_IDS)
        swap.start()
        swap.wait()
        quad[me] = pair[0] + pair[1]
        copies = []
        for j, (cx, cy) in enumerate(chips):
            cp = pltpu.make_async_remote_copy(
                src_ref=quad.at[me], dst_ref=quad.at[me], send_sem=ici_send.at[j], recv_sem=ici_recv.at[j],
                device_id=(cx, cy, c), device_id_type=MESH_IDS)
            cp.start()
            copies.append(cp)
        for j, (cx, cy) in enumerate(chips):
            slot = quad.at[2 * cx + cy]
            pltpu.make_async_remote_copy(
                src_ref=slot, dst_ref=slot, send_sem=ici_send.at[j], recv_sem=ici_recv.at[j], device_id=(cx, cy, c),
                device_id_type=MESH_IDS).wait_recv()
        for cp in copies:
            cp.wait_send()
        total = (quad[0] + quad[1]) + (quad[2] + quad[3])
        go_ref[...] = total
        d_ref[...], mo_ref[...], vo_ref[...] = _adamw_math(w_ref[...], total, m_ref[...], v_ref[...])

    return pl.pallas_call(
        body, name="small_allreduce_adamw", in_specs=[VMEM_SPEC] * 4, out_specs=[VMEM_SPEC] * 4,
        out_shape=[jax.ShapeDtypeStruct((r, cdim), F32)] * 4,
        scratch_shapes=[pltpu.VMEM((2, r, cdim), F32), pltpu.VMEM((N_CHIPS, r, cdim), F32), pltpu.SemaphoreType.DMA,
                        pltpu.SemaphoreType.DMA, pltpu.SemaphoreType.DMA((3,)), pltpu.SemaphoreType.DMA((3,))],
        compiler_params=pltpu.CompilerParams(has_side_effects=True, vmem_limit_bytes=VMEM_LIMIT_V7X))(g, w, m, v)


GROUPS = (
    ("ffn1", ("ffn1_w_gate", "ffn1_w_up", "ffn1_w_down")),
    ("mixer", ("w_in", "pool_w", "w_out_a", "w_out_b", "w_o")),
    ("ffn2", ("ffn2_w_gate", "ffn2_w_up", "ffn2_w_down")),
    ("ple", ("ple_w_gate", "ple_w_proj")),
)
GATHERS = (
    ("ffn1_in", ("ffn1_w_gate", "ffn1_w_up")),
    ("ffn1_out", ("ffn1_w_down",)),
) + GROUPS[1:]
GAINS = ("ffn1_pre_g", "ffn1_post_g", "mix_pre_g", "sgu_norm_g", "pool_scale", "mix_post_g",
         "ffn2_pre_g", "ffn2_post_g", "ple_pre_g", "ple_post_g")
SMALL = GAINS + ("sgu_b", "sgu_w")
PACKED = SMALL + ("loss",)
WEIGHTS = ("ffn1_pre_g", "ffn1_w_gate", "ffn1_w_up", "ffn1_w_down", "ffn1_post_g", "mix_pre_g", "w_in", "sgu_norm_g",
           "sgu_w", "sgu_b", "pool_w", "pool_scale", "w_out_a", "w_out_b", "w_o", "mix_post_g", "ffn2_pre_g",
           "ffn2_w_gate", "ffn2_w_up", "ffn2_w_down", "ffn2_post_g", "ple_pre_g", "ple_w_gate", "ple_w_proj", "ple_post_g")
PACK_ROWS = 16


TRANSPOSED = ("ffn1_w_gate", "ffn1_w_up", "ffn2_w_gate", "ffn2_w_up")


def _shard2d(name, a):
    a = a[0]
    return a.T if name in TRANSPOSED else a.reshape(-1, a.shape[-1])


def _unshard2d(name, a2d, shape):
    return (a2d.T if name in TRANSPOSED else a2d).reshape(shape)


def pack_rows(gains, sgu_b, loss_tile, name):
    n = len(gains)
    d = gains[0].shape[1]
    g = sgu_b.shape[0]

    def body(*refs):
        o_ref = refs[-1]
        o_ref[...] = jnp.zeros_like(o_ref)
        for i in range(n):
            o_ref[i:i + 1, :] = refs[i][...]
        o_ref[n:n + g, 0:CHUNK] = refs[n][...]
        o_ref[n + g:n + g + 1, 0:CHUNK] = refs[n + 1][0:1, :]

    return pl.pallas_call(
        body, name=name, in_specs=[VMEM_SPEC] * (n + 2), out_specs=VMEM_SPEC,
        out_shape=jax.ShapeDtypeStruct((PACK_ROWS, d), F32))(*gains, sgu_b, loss_tile)


def _pack_small(parts, tag):
    d = parts[GAINS[0]].shape[-1]
    rows = pack_rows([parts[n] for n in GAINS], parts["sgu_b"].reshape(-1, CHUNK), parts["loss"], "pack_" + tag)
    return jnp.concatenate([rows, parts["sgu_w"].reshape(-1, d)], axis=0)


def _unpack_small(packed, like):
    n, g = len(GAINS), like["sgu_b"].size // CHUNK
    out = {name: packed[i:i + 1] for i, name in enumerate(GAINS)}
    out["sgu_b"] = packed[n:n + g, :CHUNK].reshape(like["sgu_b"].shape)
    out["loss"] = packed[n + g, 0]
    out["sgu_w"] = packed[PACK_ROWS:].reshape(like["sgu_w"].shape)
    return out


def _ffn_fwd(xn, h, w, pre, g_post, g_next, tag, between=None):
    a4, s4, t4 = ffn_gu(xn, w[pre + "w_gate"], w[pre + "w_up"], tag + "_gu")
    deps = ()
    if between is not None:
        more, deps = between(a4)
        w.update(more)
    f, h_new, xn_next = mm_norm_res(a4, w[pre + "w_down"], h, g_post, g_next, 0.5, tag + "_down", deps=deps)
    return dict(xn=xn, h=h, a4=a4, s4=s4, t4=t4, f=f), h_new, xn_next


def _ffn_bwd_w(dh, saved, w, pre, g_post, tag, deps):
    df, dg4, du4, d_post = ffn_bwd_a(dh, saved["f"], g_post, w[pre + "w_down"], saved["s4"], saved["t4"], 0.5, tag + "_bwd_a",
                                     deps=deps)
    nk = N_CHIPS
    grads = {
        pre + "w_down": dw_tn(saved["a4"], df, nk, tag + "_dw_down", x_kmajor=True, dy_mode="same"),
        pre + "w_gate": dw_tn(dg4, saved["xn"], nk, tag + "_dw_gate", x_kmajor=True, dy_mode="same"),
        pre + "w_up": dw_tn(du4, saved["xn"], nk, tag + "_dw_up", x_kmajor=True, dy_mode="same"),
    }
    return dg4, du4, grads, d_post


def _ffn_bwd_x(dh, dg4, du4, saved, w, pre, g_pre, tag, deps):
    return dx_norm_bwd([(dg4, w[pre + "w_gate"], False), (du4, w[pre + "w_up"], False)], saved["h"], g_pre, dh,
                       tag + "_bwd_x", deps=deps)


def _gather_group(names, weights, chip, tag, deps):
    shards = [_shard2d(n, weights[n]) for n in names]
    bufs = [None] * len(names)
    for b, idx in enumerate(_by_shape(shards)):
        for i, buf in zip(idx, cast_place([shards[i] for i in idx], chip, f"cast_{tag}_{b}", deps=deps)):
            bufs[i] = buf
    sends, recvs, bufs, token = allgather_start(bufs, "allgather_start_" + tag)
    return (sends, recvs, bufs), token


def _gathered(started, names, after, tag):
    sends, recvs, bufs = started
    landed = allgather_wait(sends, recvs, bufs, after, "allgather_wait_" + tag)
    return dict(zip(names, d2d_forward(landed, "d2d_forward_" + tag)))


def _forward_early(started, after, tag):
    sends, recvs, bufs = started
    landed = allgather_wait(sends, recvs, bufs, after, "allgather_wait_" + tag)
    return d2d_forward_start(landed, "d2d_forward_start_" + tag)


def _forwarded(forwarding, names, after, tag):
    return dict(zip(names, d2d_forward_wait(forwarding, after, "d2d_forward_wait_" + tag)))


def _exchange_start(names, big_g, tag):
    return sibling_start([big_g[n] for n in names], True, "exchange_start_" + tag)


def _scatter_begin(names, exchanging, after, core, tag):
    partial, from_sibling = sibling_wait(exchanging, True, after, "exchange_wait_" + tag)
    chip_sums = [None] * len(names)
    for b, idx in enumerate(_by_shape(partial)):
        for i, s in zip(idx, add_halves([partial[i] for i in idx], [from_sibling[i] for i in idx], core, f"add_halves_{tag}_{b}")):
            chip_sums[i] = s
    sends, recvs, sums, lands, token = scatter_start(chip_sums, "scatter_start_" + tag)
    return (names, sends, recvs, sums, lands), token


def _share_begin(scattering, after, chip, tag):
    names, sends, recvs, sums, lands = scattering
    sums, lands = scatter_wait(sends, recvs, sums, lands, after, "scatter_wait_" + tag)
    reduced = [None] * len(names)
    for b, idx in enumerate(_by_shape(sums)):
        for i, r in zip(idx, add_chips([sums[i] for i in idx], [lands[i] for i in idx], chip, f"add_chips_{tag}_{b}")):
            reduced[i] = r
    sharing, token = sibling_start(reduced, False, "share_start_" + tag)
    return (names, sharing), token


def _update(shared, after, weights, moments_m, moments_v, core, tag, results, held):
    names, sharing = shared
    reduced, others = sibling_wait(sharing, False, after, "share_wait_" + tag)
    buckets = _by_shape(reduced)

    def update_bucket(b, idx, deps):
        in_bucket = [names[i] for i in idx]
        operands = ([_shard2d(n, weights[n]) for n in in_bucket], [reduced[i] for i in idx], [others[i] for i in idx],
                    [_shard2d(n, moments_m[n]) for n in in_bucket], [_shard2d(n, moments_v[n]) for n in in_bucket])
        if held is None:
            outs = adamw_halves_sparsecore(*operands, f"adamw_{tag}_{b}")
        else:
            outs = adamw_halves(*operands, core, f"adamw_{tag}_{b}", deps=deps)
        for n, per_weight in zip(in_bucket, outs):
            for store, value in zip(results, per_weight):
                store[n] = _unshard2d(n, value, weights[n].shape)
        return outs[-1][1]

    if held is None:
        for b, idx in enumerate(buckets):
            update_bucket(b, idx, ())
        return None
    for b, idx in enumerate(buckets[:-1]):
        held.append(functools.partial(update_bucket, b, idx))
    return update_bucket(len(buckets) - 1, buckets[-1], ())


def kernel(x, p, ffn1_pre_g, ffn1_w_gate, ffn1_w_up, ffn1_w_down, ffn1_post_g, mix_pre_g, w_in, sgu_norm_g, sgu_w, sgu_b, pool_w, pool_scale, w_out_a, w_out_b, w_o, mix_post_g, ffn2_pre_g, ffn2_w_gate, ffn2_w_up, ffn2_w_down, ffn2_post_g, ple_pre_g, ple_w_gate, ple_w_proj, ple_post_g, loss_target, m_ffn1_pre_g, m_ffn1_w_gate, m_ffn1_w_up, m_ffn1_w_down, m_ffn1_post_g, m_mix_pre_g, m_w_in, m_sgu_norm_g, m_sgu_w, m_sgu_b, m_pool_w, m_pool_scale, m_w_out_a, m_w_out_b, m_w_o, m_mix_post_g, m_ffn2_pre_g, m_ffn2_w_gate, m_ffn2_w_up, m_ffn2_w_down, m_ffn2_post_g, m_ple_pre_g, m_ple_w_gate, m_ple_w_proj, m_ple_post_g, v_ffn1_pre_g, v_ffn1_w_gate, v_ffn1_w_up, v_ffn1_w_down, v_ffn1_post_g, v_mix_pre_g, v_w_in, v_sgu_norm_g, v_sgu_w, v_sgu_b, v_pool_w, v_pool_scale, v_w_out_a, v_w_out_b, v_w_o, v_mix_post_g, v_ffn2_pre_g, v_ffn2_w_gate, v_ffn2_w_up, v_ffn2_w_down, v_ffn2_post_g, v_ple_pre_g, v_ple_w_gate, v_ple_w_proj, v_ple_post_g):
    given = dict(locals())
    weights = {n: given[n] for n in WEIGHTS}
    moments_m = {n: given["m_" + n] for n in WEIGHTS}
    moments_v = {n: given["v_" + n] for n in WEIGHTS}
    core = lax.axis_index("c").astype(jnp.int32).reshape(1)
    chip = (2 * lax.axis_index("x") + lax.axis_index("y")).astype(jnp.int32).reshape(1)

    d = x.shape[-1]
    token = ()
    gathering = {}
    for tag, names in GATHERS:
        gathering[tag], tok = _gather_group(names, weights, chip, tag, token)
        token = (tok,)
    gain = {n: weights[n] for n in GAINS}
    sgu_w3 = sgu_w[0]
    sgu_b3 = sgu_b[0][:, :, None]
    groups = dict(GROUPS + GATHERS)

    h0 = x[0]
    tgt = loss_target[0]
    p_bf = p[0, 0].astype(BF16)
    w = _gathered(gathering["ffn1_in"], groups["ffn1_in"], token[0], "ffn1_in")
    xn1 = rms_cast(h0, gain["ffn1_pre_g"], "ffn1_pre_norm")
    s1, h1, xn2 = _ffn_fwd(xn1, h0, w, "ffn1_", gain["ffn1_post_g"], gain["mix_pre_g"], "ffn1",
                           lambda a4: (_gathered(gathering["ffn1_out"], groups["ffn1_out"], a4, "ffn1_out"), ()))
    w.update(_gathered(gathering["mixer"], groups["mixer"], h1, "mixer"))
    full = {n: w[n].reshape(-1, d) for n in ("w_out_a", "w_out_b", "w_o")}
    n_groups = pool_w.shape[1]
    rows_per = pool_w.shape[2]
    dgp = pool_w.shape[3]
    pool_full = w["pool_w"].reshape(N_CHIPS, n_groups, rows_per, dgp).transpose(1, 0, 2, 3).reshape(n_groups, N_CHIPS * rows_per, dgp)
    z = mixer_in(xn2, w["w_in"], "mixer_in")
    a = sgu_fwd(z, gain["sgu_norm_g"], sgu_w3, sgu_b3, "sgu_fwd")
    diff, b = pool_fwd(z, pool_full, gain["pool_scale"], "pool_fwd")
    ya, yb, y = mixer_y(a, b, z, full["w_out_a"], full["w_out_b"], "mixer_y")
    forwarding, tok = _forward_early(gathering["ffn2"], y, "ffn2")
    m, h2, xn3 = mm_norm_res(y[None], full["w_o"][None], h1, gain["mix_post_g"], gain["ffn2_pre_g"], 1.0, "mixer_out",
                             deps=(tok,))
    w.update(_forwarded(forwarding, groups["ffn2"], h2, "ffn2"))
    early = {}

    def forward_ple(a4):
        early["ple"], tok_ple = _forward_early(gathering["ple"], a4, "ple")
        return {}, (tok_ple,)

    s2, h3, xn4 = _ffn_fwd(xn3, h2, w, "ffn2_", gain["ffn2_post_g"], gain["ple_pre_g"], "ffn2", forward_ple)
    w.update(_forwarded(early["ple"], groups["ple"], h3, "ple"))
    full["ple_w_gate"] = w["ple_w_gate"].reshape(-1, d)
    proj_full = w["ple_w_proj"].transpose(1, 0, 2).reshape(ple_w_proj.shape[1], -1)

    small_g, big_g = {}, {}
    ds, de, dh3, small_g["ple_post_g"], small_g["ple_pre_g"], loss_part = ple_loss(
        xn4, p_bf, full["ple_w_gate"], proj_full, h3, gain["ple_post_g"], gain["ple_pre_g"], tgt, "ple_loss")
    small_g["loss"] = loss_part
    grad, delta, new_m, new_v = {}, {}, {}, {}
    results = (grad, delta, new_m, new_v)
    update = functools.partial(_update, weights=weights, moments_m=moments_m, moments_v=moments_v, core=core, results=results)

    def reduce_behind(tag, previous):
        exchanging, tok = _exchange_start(groups[tag], big_g, tag)
        if previous is not None:
            shared, tok = _share_begin(previous[1], tok, chip, previous[0])
        scattering, tok_scatter = _scatter_begin(groups[tag], exchanging, tok, core, tag)
        if previous is not None:
            update(shared, tok_scatter, tag=previous[0], held=None)
        return (tag, scattering), (tok_scatter,)

    big_g["ple_w_gate"] = dw_tn(xn4, ds, 1, "dw_ple_gate").reshape(N_CHIPS, -1, d)
    big_g["ple_w_proj"] = dw_tn(p_bf, de, N_CHIPS, "dw_ple_proj", dy_mode="cols")
    reducing, deps = reduce_behind("ple", None)

    dg4, du4, g2, small_g["ffn2_post_g"] = _ffn_bwd_w(dh3, s2, w, "ffn2_", gain["ffn2_post_g"], "ffn2", deps)
    big_g.update(g2)
    dh2, small_g["ffn2_pre_g"] = _ffn_bwd_x(dh3, dg4, du4, s2, w, "ffn2_", gain["ffn2_pre_g"], "ffn2", ())
    reducing, deps = reduce_behind("ffn2", reducing)

    dm, dya, dyb, dz, da, db, small_g["mix_post_g"] = mixer_bwd_y(
        dh2, m, gain["mix_post_g"], full["w_o"], ya, yb, z, full["w_out_a"], full["w_out_b"], "mixer_bwd_y", deps=deps)
    big_g["w_o"] = dw_tn(y, dm, 1, "dw_o").reshape(N_CHIPS, -1, d)
    big_g["w_out_a"] = dw_tn(a, dya, 1, "dw_out_a").reshape(N_CHIPS, -1, d)
    big_g["w_out_b"] = dw_tn(b, dyb, 1, "dw_out_b").reshape(N_CHIPS, -1, d)
    dz, d_sgu_w, d_sgu_b, small_g["sgu_norm_g"] = sgu_bwd(z, da, dz, gain["sgu_norm_g"], sgu_w3, sgu_b3, "sgu_bwd")
    dz, d_pool_w, small_g["pool_scale"] = pool_bwd(db, diff, dz, pool_full, gain["pool_scale"], "pool_bwd")
    big_g["pool_w"] = d_pool_w.astype(BF16).reshape(n_groups, N_CHIPS, rows_per, dgp).transpose(1, 0, 2, 3).reshape(
        N_CHIPS, n_groups * rows_per, dgp)
    big_g["w_in"] = dw_in_tiles(xn2, dz, N_CHIPS, "dw_in")
    dh1, small_g["mix_pre_g"] = dx_norm_bwd([(dz, w["w_in"], True)], h1, gain["mix_pre_g"], dh2, "mixer_bwd_x")
    reducing, deps = reduce_behind("mixer", reducing)

    dg4, du4, g1, small_g["ffn1_post_g"] = _ffn_bwd_w(dh1, s1, w, "ffn1_", gain["ffn1_post_g"], "ffn1", deps)
    big_g.update(g1)
    reducing, deps = reduce_behind("ffn1", reducing)
    dh0, small_g["ffn1_pre_g"] = _ffn_bwd_x(dh1, dg4, du4, s1, w, "ffn1_", gain["ffn1_pre_g"], "ffn1", deps)
    shared, tok = _share_begin(reducing[1], dh0, chip, "ffn1")
    update(shared, tok, tag="ffn1", held=[])
    small_g["sgu_w"] = d_sgu_w
    small_g["sgu_b"] = d_sgu_b[:, :, 0]

    no_state = {"loss": jnp.zeros((8, 128), F32)}
    packed = small_allreduce_adamw(
        _pack_small(small_g, "grads"), _pack_small({n: weights[n] for n in SMALL} | no_state, "weights"),
        _pack_small({n: moments_m[n] for n in SMALL} | no_state, "m"), _pack_small({n: moments_v[n] for n in SMALL} | no_state, "v"))
    like = {n: weights[n] for n in SMALL}
    for store, block in zip((grad, delta, new_m, new_v), packed):
        store.update(_unpack_small(block, like))

    return (grad["loss"], dh0[None], *[grad[n] for n in WEIGHTS], *[delta[n] for n in WEIGHTS],
            *[new_m[n] for n in WEIGHTS], *[new_v[n] for n in WEIGHTS])
```

```python
import functools

import jax
import jax.numpy as jnp
from jax import lax
from jax.experimental import pallas as pl
from jax.experimental.pallas import tpu as pltpu
from jax.experimental.pallas import tpu_sc as plsc

F32 = jnp.float32
BF16 = jnp.bfloat16
EPS = 1e-6
CHUNK = 128
POOL_WINDOWS = (2, 4, 8, 16)
HALO = 16
N_CHIPS = 4
ADAM_LR, ADAM_B1, ADAM_B2, ADAM_EPS, ADAM_WD, ADAM_STEP = 0.001, 0.9, 0.999, 1e-08, 0.01, 10
VMEM_LIMIT_V7X = 58 * 1024 * 1024
MESH_IDS = pl.DeviceIdType.MESH
HBM_SPEC = pl.BlockSpec(memory_space=pltpu.HBM)
VMEM_SPEC = pl.BlockSpec(memory_space=pltpu.VMEM)
SEM_SPEC = pl.BlockSpec(memory_space=pltpu.SEMAPHORE)
ANY_SPEC = pl.BlockSpec(memory_space=pl.ANY)
DATAFLOW = pltpu.SideEffectType.DATAFLOW_SIDE_EFFECTING
OTHER_CHIPS = ((1, 0), (0, 1), (1, 1))
DZ_SLOT = (2, 3, 4, 0, 1)
SC_TILES, SC_LANES, SC_ROWS = 32, 16, 8
DW_TOKENS = 4096
DW_IN_TILE = 256

NT = (((1,), (1,)), ((), ()))
TN = (((0,), (0,)), ((), ()))


def _params(*sem, **more):
    return pltpu.CompilerParams(dimension_semantics=sem or None, vmem_limit_bytes=VMEM_LIMIT_V7X, **more)


def _tile(t, want):
    return max(c for c in range(8, min(t, want) + 1, 8) if t % c == 0)


def _const(shape):
    return pl.BlockSpec(shape, lambda *_: (0,) * len(shape))


def _rows(tm, d, col=0):
    return pl.BlockSpec((tm, d), lambda i: (i, col))


def _kmajor(nk, tm, kb):
    return pl.BlockSpec((nk, tm, kb), lambda i: (0, i, 0))


def _dot(a, b):
    return jnp.dot(a, b, preferred_element_type=F32)


def _dot_nt(a, b):
    return lax.dot_general(a, b, NT, preferred_element_type=F32)


def _dot_tn(a, b):
    return lax.dot_general(a, b, TN, preferred_element_type=F32)


def _gelu(x):
    return 0.5 * x * (1.0 + jnp.tanh(0.7978845608028654 * (x + 0.044715 * x * x * x)))


def _gelu_and_grad(x):
    x2 = x * x
    th = jnp.tanh(0.7978845608028654 * (x + 0.044715 * x * x2))
    cdf = 0.5 * (1.0 + th)
    return x * cdf, cdf + 0.5 * x * (1.0 - th * th) * 0.7978845608028654 * (1.0 + 3.0 * 0.044715 * x2)


def _sigmoid(x):
    return 1.0 / (1.0 + jnp.exp(-x))


def _rstd(h):
    return lax.rsqrt(jnp.mean(h * h, axis=-1, keepdims=True) + EPS)


def _rms_bwd(h, g, dy):
    r = _rstd(h)
    t = dy * g
    dh = r * t - h * (r * r * r) * jnp.mean(h * t, axis=-1, keepdims=True)
    return dh, jnp.sum(dy * h * r, axis=0, keepdims=True)


def _ordered_after(body, n_in, deps):
    if not deps:
        return body
    return lambda *refs: body(*refs[:n_in], *refs[n_in + len(deps):])


def _accumulate(ref, value):
    @pl.when(pl.program_id(0) == 0)
    def _():
        ref[...] = jnp.zeros_like(ref)

    ref[...] += value


def rms_cast(h, g, name):
    t, d = h.shape
    tm = _tile(t, 512)

    def body(h_ref, g_ref, o_ref):
        hv = h_ref[...]
        o_ref[...] = (hv * _rstd(hv) * g_ref[...]).astype(BF16)

    return pl.pallas_call(
        body, name=name, grid=(t // tm,), in_specs=[_rows(tm, d), _const((1, d))], out_specs=_rows(tm, d),
        out_shape=jax.ShapeDtypeStruct((t, d), BF16), compiler_params=_params("parallel"))(h, g)


def ffn_gu(xn, wgt, wut, name, chip=None, into=None):
    t, d = xn.shape
    nkw, fk, _ = wgt.shape
    tm = _tile(t, 512)
    count = N_CHIPS if chip is None else (1 if into is None else N_CHIPS - 1)
    first = 0 if into is None else 1

    def body(*refs):
        x_ref, wg_ref, wu_ref = refs[-6 - (0 if into is None else 3):][:3]
        a_ref, s_ref, t_ref = refs[-3:]
        xv = x_ref[...]
        g = _dot_nt(xv, wg_ref[0])
        u = _dot_nt(xv, wu_ref[0])
        sg = _sigmoid(g)
        s = g * sg
        a_ref[0] = (s * u).astype(BF16)
        s_ref[0] = s.astype(BF16)
        t_ref[0] = (u * sg * (1.0 + g * (1.0 - sg))).astype(BF16)

    def slot(k, *chip_ref):
        return (chip_ref[0][0] + first + k) % N_CHIPS if chip_ref else k

    w_spec = pl.BlockSpec((1, fk, d), lambda k, i, *c: (slot(k, *c) if nkw > 1 else 0, 0, 0))
    o_spec = pl.BlockSpec((1, tm, fk), lambda k, i, *c: (slot(k, *c), i, 0))
    shape = jax.ShapeDtypeStruct((N_CHIPS, t, fk), BF16)
    prior = () if into is None else tuple(into)
    call = pl.pallas_call(
        body, name=name,
        grid_spec=pltpu.PrefetchScalarGridSpec(
            num_scalar_prefetch=0 if chip is None else 1, grid=(count, t // tm),
            in_specs=[pl.BlockSpec((tm, d), lambda k, i, *c: (i, 0)), w_spec, w_spec] + [ANY_SPEC] * len(prior),
            out_specs=[o_spec] * 3),
        out_shape=[shape] * 3, input_output_aliases={4 + j: j for j in range(len(prior))},
        compiler_params=_params("parallel", "parallel"))
    return call(*(() if chip is None else (chip,)), xn, wgt, wut, *prior)


def mm_norm_res(a3, w3, h_old, g_post, g_next, scale, name, deps=()):
    nk, t, kb = a3.shape
    d = w3.shape[2]
    tm = _tile(t, 256)

    def body(a_ref, w_ref, h_ref, gp_ref, gn_ref, f_ref, hn_ref, xn_ref):
        f = _dot(a_ref[0], w_ref[0])
        for k in range(1, nk):
            f += _dot(a_ref[k], w_ref[k])
        f_ref[...] = f
        hn = h_ref[...] + scale * (f * _rstd(f) * gp_ref[...])
        hn_ref[...] = hn
        xn_ref[...] = (hn * _rstd(hn) * gn_ref[...]).astype(BF16)

    return pl.pallas_call(
        _ordered_after(body, 5, deps), name=name, grid=(t // tm,),
        in_specs=[_kmajor(nk, tm, kb), _const((nk, kb, d)), _rows(tm, d), _const((1, d)), _const((1, d))] + [ANY_SPEC] * len(deps),
        out_specs=[_rows(tm, d)] * 3,
        out_shape=[jax.ShapeDtypeStruct((t, d), F32), jax.ShapeDtypeStruct((t, d), F32), jax.ShapeDtypeStruct((t, d), BF16)],
        compiler_params=_params("parallel"))(a3, w3, h_old, g_post, g_next, *deps)


def mixer_in(xn, win4, name):
    t, d = xn.shape
    nk, _, nb = win4.shape
    tm = _tile(t, 512)

    def body(x_ref, w_ref, z_ref):
        z_ref[...] = _dot(x_ref[...], w_ref[0]).astype(BF16)

    return pl.pallas_call(
        body, name=name, grid=(nk, t // tm),
        in_specs=[pl.BlockSpec((tm, d), lambda k, i: (i, 0)), pl.BlockSpec((1, d, nb), lambda k, i: (k, 0, 0))],
        out_specs=pl.BlockSpec((tm, nb), lambda k, i: (i, k)), out_shape=jax.ShapeDtypeStruct((t, nk * nb), BF16),
        compiler_params=_params("parallel", "parallel"))(xn, win4)


def _causal_mask():
    row = lax.broadcasted_iota(jnp.int32, (CHUNK, CHUNK), 0)
    col = lax.broadcasted_iota(jnp.int32, (CHUNK, CHUNK), 1)
    return row >= col


def _layernorm_parts(v):
    mu = jnp.mean(v, axis=-1, keepdims=True)
    vc = v - mu
    r = lax.rsqrt(jnp.mean(vc * vc, axis=-1, keepdims=True) + EPS)
    return vc * r, r


def sgu_fwd(z, norm_g, sgu_w, sgu_b3, name):
    t = z.shape[0]
    d = norm_g.shape[1]
    ng = sgu_w.shape[0]
    dg = d // ng
    tm = _tile(t, 256)

    def body(zu_ref, zv_ref, ng_ref, w_ref, b_ref, a_ref):
        vhat, _ = _layernorm_parts(_gelu(zv_ref[...].astype(F32)))
        vn = (vhat * ng_ref[...]).astype(BF16)
        u = _gelu(zu_ref[...].astype(F32))
        mask = _causal_mask()
        for g in range(ng):
            wg = jnp.where(mask, w_ref[g], 0.0).astype(BF16)
            for ci in range(tm // CHUNK):
                rs, cs = slice(ci * CHUNK, (ci + 1) * CHUNK), slice(g * dg, (g + 1) * dg)
                sv = _dot(wg, vn[rs, cs]) + b_ref[g]
                a_ref[rs, cs] = (u[rs, cs] * sv).astype(BF16)

    return pl.pallas_call(
        body, name=name, grid=(t // tm,),
        in_specs=[_rows(tm, d, 0), _rows(tm, d, 1), _const((1, d)), _const((ng, CHUNK, CHUNK)), _const((ng, CHUNK, 1))],
        out_specs=_rows(tm, d), out_shape=jax.ShapeDtypeStruct((t, d), BF16),
        compiler_params=_params("parallel"))(z, z, norm_g, sgu_w, sgu_b3)


def pool_fwd(z, pool_w, pool_scale, name):
    t = z.shape[0]
    d = pool_scale.shape[1]
    ng = pool_w.shape[0]
    dg = d // ng
    tm = _tile(t, 256)
    per = tm // HALO

    def body(c_ref, prev_ref, w_ref, s_ref, diff_ref, b_ref):
        i = pl.program_id(0)
        cur = c_ref[...].astype(F32)
        prev = jnp.where(i > 0, prev_ref[...].astype(F32), 0.0)
        ext = jnp.concatenate([prev, cur], axis=0)
        tok = i * tm + lax.broadcasted_iota(jnp.int32, (tm, 1), 0)
        for g, win in enumerate(POOL_WINDOWS):
            cs = slice(g * dg, (g + 1) * dg)
            s = ext[:, cs]
            sh = 1
            while sh < win:
                s = s + pltpu.roll(s, sh, 0)
                sh *= 2
            per_count = 1.0 / jnp.minimum(tok + 1, win).astype(F32)
            diff = (s[HALO:] * per_count - cur[:, cs]).astype(BF16)
            diff_ref[:, cs] = diff
            b_ref[:, cs] = (_dot(diff, w_ref[g]) * s_ref[:, cs]).astype(BF16)

    return pl.pallas_call(
        body, name=name, grid=(t // tm,),
        in_specs=[_rows(tm, d, 2), pl.BlockSpec((HALO, d), lambda i: (jnp.maximum(i * per - 1, 0), 2)),
                  _const((ng, dg, dg)), _const((1, d))],
        out_specs=[_rows(tm, d)] * 2, out_shape=[jax.ShapeDtypeStruct((t, d), BF16)] * 2,
        compiler_params=_params("parallel"))(z, z, pool_w, pool_scale)


def mixer_y(a, b, z, woa, wob, name):
    t, d = a.shape
    tm = _tile(t, 256)

    def body(a_ref, b_ref, ga_ref, gb_ref, wa_ref, wb_ref, ya_ref, yb_ref, y_ref):
        ya = _dot(a_ref[...], wa_ref[...])
        yb = _dot(b_ref[...], wb_ref[...])
        ya_ref[...] = ya.astype(BF16)
        yb_ref[...] = yb.astype(BF16)
        y_ref[...] = (_sigmoid(ga_ref[...].astype(F32)) * ya + _sigmoid(gb_ref[...].astype(F32)) * yb).astype(BF16)

    return pl.pallas_call(
        body, name=name, grid=(t // tm,),
        in_specs=[_rows(tm, d), _rows(tm, d), _rows(tm, d, 3), _rows(tm, d, 4), _const((d, d)), _const((d, d))],
        out_specs=[_rows(tm, d)] * 3, out_shape=[jax.ShapeDtypeStruct((t, d), BF16)] * 3,
        compiler_params=_params("parallel"))(a, b, z, z, woa, wob)


def ple_loss(xn, p, wpg, wpp, h, g_post, g_pre, target, name):
    t, d = xn.shape
    dp = p.shape[1]
    tm = _tile(t, 256)

    def body(x_ref, p_ref, wg_ref, wp_ref, h_ref, gp_ref, gn_ref, tg_ref, ds_ref, de_ref, dhp_ref, dgp_ref, dgn_ref, loss_ref):
        gate = _sigmoid(_dot(x_ref[...], wg_ref[...]))
        e = _dot(p_ref[...], wp_ref[...])
        q = gate * e
        hv = h_ref[...]
        err = hv + q * _rstd(q) * gp_ref[...] - tg_ref[...]
        _accumulate(loss_ref, jnp.full(loss_ref.shape, (0.5 / d) * jnp.sum(err * err), F32))
        dhv = err * (1.0 / d)
        dq, dgp = _rms_bwd(q, gp_ref[...], dhv)
        ds = (dq * e * gate * (1.0 - gate)).astype(BF16)
        ds_ref[...] = ds
        de_ref[...] = (dq * gate).astype(BF16)
        dx, dgn = _rms_bwd(hv, gn_ref[...], _dot_nt(ds, wg_ref[...]))
        dhp_ref[...] = dhv + dx
        _accumulate(dgp_ref, dgp)
        _accumulate(dgn_ref, dgn)

    return pl.pallas_call(
        body, name=name, grid=(t // tm,),
        in_specs=[_rows(tm, d), _rows(tm, dp), _const((d, d)), _const((dp, d)), _rows(tm, d), _const((1, d)), _const((1, d)),
                  _rows(tm, d)],
        out_specs=[_rows(tm, d)] * 3 + [_const((1, d))] * 2 + [_const((8, 128))],
        out_shape=[jax.ShapeDtypeStruct((t, d), BF16), jax.ShapeDtypeStruct((t, d), BF16), jax.ShapeDtypeStruct((t, d), F32),
                   jax.ShapeDtypeStruct((1, d), F32), jax.ShapeDtypeStruct((1, d), F32), jax.ShapeDtypeStruct((8, 128), F32)],
        compiler_params=_params("arbitrary"))(xn, p, wpg, wpp, h, g_post, g_pre, target)


def ffn_bwd_a(dh, f, g_post, wd4, s4, t4, scale, name, deps=()):
    t, d = dh.shape
    nk, fk, _ = wd4.shape
    tm = _tile(t, 256)

    def body(dh_ref, f_ref, gp_ref, w_ref, s_ref, t_ref, df_ref, dg_ref, du_ref, dgp_ref):
        df, dgp = _rms_bwd(f_ref[...], gp_ref[...], dh_ref[...])
        df = (scale * df).astype(BF16)
        df_ref[...] = df
        _accumulate(dgp_ref, scale * dgp)
        for k in range(nk):
            da = _dot_nt(df, w_ref[k])
            du_ref[k] = (da * s_ref[k].astype(F32)).astype(BF16)
            dg_ref[k] = (da * t_ref[k].astype(F32)).astype(BF16)

    return pl.pallas_call(
        _ordered_after(body, 6, deps), name=name, grid=(t // tm,),
        in_specs=[_rows(tm, d), _rows(tm, d), _const((1, d)), _const((nk, fk, d)), _kmajor(nk, tm, fk), _kmajor(nk, tm, fk)]
        + [ANY_SPEC] * len(deps),
        out_specs=[_rows(tm, d), _kmajor(nk, tm, fk), _kmajor(nk, tm, fk), _const((1, d))],
        out_shape=[jax.ShapeDtypeStruct((t, d), BF16), jax.ShapeDtypeStruct((nk, t, fk), BF16),
                   jax.ShapeDtypeStruct((nk, t, fk), BF16), jax.ShapeDtypeStruct((1, d), F32)],
        compiler_params=_params("arbitrary"))(dh, f, g_post, wd4, s4, t4, *deps)


def dx_norm_bwd(pairs, h, g_pre, dh_in, name, deps=()):
    t, d = h.shape
    tm = _tile(t, 256)
    n = len(pairs)

    def body(*refs):
        dys, ws = refs[:n], refs[n:2 * n]
        h_ref, g_ref, dhi_ref, dho_ref, dg_ref = refs[2 * n:]
        acc = None
        for (_, w4, sections), dy_ref, w_ref in zip(pairs, dys, ws):
            if sections:
                wide = w4.shape[2]
                edges = sorted(set(range(0, 5 * d + 1, d)) | set(range(0, 5 * d + 1, wide)))
                parts = [_dot_nt(dy_ref[DZ_SLOT[lo // d], :, lo % d:lo % d + hi - lo], w_ref[lo // wide, :, lo % wide:lo % wide + hi - lo])
                         for lo, hi in zip(edges[:-1], edges[1:])]
            else:
                parts = [_dot(dy_ref[k], w_ref[k]) for k in range(w4.shape[0])]
            for part in parts:
                acc = part if acc is None else acc + part
        dx, dg = _rms_bwd(h_ref[...], g_ref[...], acc)
        dho_ref[...] = dhi_ref[...] + dx
        _accumulate(dg_ref, dg)

    dy_specs = [_kmajor(dy.shape[0], tm, dy.shape[2]) for dy, _, _ in pairs]
    return pl.pallas_call(
        _ordered_after(body, 2 * n + 3, deps), name=name, grid=(t // tm,),
        in_specs=dy_specs + [_const(w4.shape) for _, w4, _ in pairs] + [_rows(tm, d), _const((1, d)), _rows(tm, d)]
        + [ANY_SPEC] * len(deps),
        out_specs=[_rows(tm, d), _const((1, d))],
        out_shape=[jax.ShapeDtypeStruct((t, d), F32), jax.ShapeDtypeStruct((1, d), F32)],
        compiler_params=_params("arbitrary"))(*[dy for dy, _, _ in pairs], *[w4 for _, w4, _ in pairs], h, g_pre, dh_in, *deps)


def mixer_bwd_y(dh, m, g_post, w_o, ya, yb, z, woa, wob, name, deps=()):
    t, d = dh.shape
    tm = _tile(t, 256)

    def body(dh_ref, m_ref, gp_ref, wo_ref, ya_ref, yb_ref, ga_ref, gb_ref, wa_ref, wb_ref,
             dm_ref, dya_ref, dyb_ref, dz_ref, da_ref, db_ref, dgp_ref):
        dm, dgp = _rms_bwd(m_ref[...], gp_ref[...], dh_ref[...])
        dm = dm.astype(BF16)
        dm_ref[...] = dm
        _accumulate(dgp_ref, dgp)
        dy = _dot_nt(dm, wo_ref[...])
        sa = _sigmoid(ga_ref[...].astype(F32))
        sb = _sigmoid(gb_ref[...].astype(F32))
        dya = (dy * sa).astype(BF16)
        dyb = (dy * sb).astype(BF16)
        dya_ref[...] = dya
        dyb_ref[...] = dyb
        dz_ref[0] = (dy * ya_ref[...].astype(F32) * sa * (1.0 - sa)).astype(BF16)
        dz_ref[1] = (dy * yb_ref[...].astype(F32) * sb * (1.0 - sb)).astype(BF16)
        da_ref[...] = _dot_nt(dya, wa_ref[...]).astype(BF16)
        db_ref[...] = _dot_nt(dyb, wb_ref[...]).astype(BF16)

    return pl.pallas_call(
        _ordered_after(body, 10, deps), name=name, grid=(t // tm,),
        in_specs=[_rows(tm, d), _rows(tm, d), _const((1, d)), _const((d, d)), _rows(tm, d), _rows(tm, d),
                  _rows(tm, d, 3), _rows(tm, d, 4), _const((d, d)), _const((d, d))] + [ANY_SPEC] * len(deps),
        out_specs=[_rows(tm, d)] * 3 + [pl.BlockSpec((2, tm, d), lambda i: (0, i, 0))] + [_rows(tm, d)] * 2 + [_const((1, d))],
        out_shape=[jax.ShapeDtypeStruct((t, d), BF16)] * 3 + [jax.ShapeDtypeStruct((5, t, d), BF16)]
        + [jax.ShapeDtypeStruct((t, d), BF16)] * 2 + [jax.ShapeDtypeStruct((1, d), F32)],
        compiler_params=_params("arbitrary"))(dh, m, g_post, w_o, ya, yb, z, z, woa, wob, *deps)


def sgu_bwd(z, da, dz, norm_g, sgu_w, sgu_b3, name):
    t, d = da.shape
    ng = sgu_w.shape[0]
    dg = d // ng
    tm = _tile(t, 256)
    steps = t // tm

    def body(zu_ref, zv_ref, da_ref, ng_ref, w_ref, b_ref, _, dz_ref, dw_ref, db_ref, dng_ref, dvn_ref, dsv_ref):
        i = pl.program_id(0)
        zv = zv_ref[...].astype(F32)
        zu = zu_ref[...].astype(F32)
        v, gv = _gelu_and_grad(zv)
        vhat, r = _layernorm_parts(v)
        gain = ng_ref[...]
        vn = (vhat * gain).astype(BF16)
        u, gu = _gelu_and_grad(zu)
        dav = da_ref[...].astype(F32)
        mask = _causal_mask()

        @pl.when(i == 0)
        def _():
            dw_ref[...] = jnp.zeros_like(dw_ref)
            dsv_ref[...] = jnp.zeros_like(dsv_ref)

        for g in range(ng):
            wg = jnp.where(mask, w_ref[g], 0.0).astype(BF16)
            dw = jnp.zeros((CHUNK, CHUNK), F32)
            dsv_sum = jnp.zeros((CHUNK, dg), F32)
            for ci in range(tm // CHUNK):
                rs, cs = slice(ci * CHUNK, (ci + 1) * CHUNK), slice(g * dg, (g + 1) * dg)
                vn_blk = vn[rs, cs]
                sv = _dot(wg, vn_blk) + b_ref[g]
                dz_ref[0, rs, cs] = (dav[rs, cs] * sv * gu[rs, cs]).astype(BF16)
                dsv = dav[rs, cs] * u[rs, cs]
                dsv_sum += dsv
                dsv = dsv.astype(BF16)
                dw += _dot_nt(dsv, vn_blk)
                dvn_ref[rs, cs] = _dot_tn(wg, dsv)
            dw_ref[g] += dw
            dsv_ref[:, cs] += dsv_sum

        dvn = dvn_ref[...]
        _accumulate(dng_ref, jnp.sum(dvn * vhat, axis=0, keepdims=True))
        dvh = dvn * gain
        dv = r * (dvh - jnp.mean(dvh, axis=-1, keepdims=True) - vhat * jnp.mean(dvh * vhat, axis=-1, keepdims=True))
        dz_ref[1] = (dv * gv).astype(BF16)

        @pl.when(i == steps - 1)
        def _():
            for g in range(ng):
                dw_ref[g] = jnp.where(mask, dw_ref[g], 0.0)
                row_sum = jnp.sum(dsv_ref[:, g * dg:(g + 1) * dg], axis=1, keepdims=True)
                db_ref[g] = jnp.broadcast_to(row_sum, (CHUNK, CHUNK))

    return pl.pallas_call(
        body, name=name, grid=(steps,),
        in_specs=[_rows(tm, d, 0), _rows(tm, d, 1), _rows(tm, d), _const((1, d)), _const((ng, CHUNK, CHUNK)), _const((ng, CHUNK, 1)),
                  ANY_SPEC],
        out_specs=[pl.BlockSpec((2, tm, d), lambda i: (DZ_SLOT[0] // 2, i, 0)), _const((ng, CHUNK, CHUNK)),
                   _const((ng, CHUNK, CHUNK)), _const((1, d))],
        out_shape=[jax.ShapeDtypeStruct(dz.shape, BF16), jax.ShapeDtypeStruct((ng, CHUNK, CHUNK), F32),
                   jax.ShapeDtypeStruct((ng, CHUNK, CHUNK), F32), jax.ShapeDtypeStruct((1, d), F32)],
        scratch_shapes=[pltpu.VMEM((tm, d), F32), pltpu.VMEM((CHUNK, d), F32)], input_output_aliases={6: 0},
        compiler_params=_params("arbitrary"))(z, z, da, norm_g, sgu_w, sgu_b3, dz)


def pool_bwd(db, diff, dz, pool_w, pool_scale, name):
    t, d = db.shape
    ng = pool_w.shape[0]
    dg = d // ng
    tm = _tile(t, 256)
    per = tm // HALO
    steps = t // tm

    def body(db_ref, next_ref, diff_ref, w_ref, s_ref, _, dc_ref, dw_ref, ds_ref):
        i = pl.program_id(0)
        dbc = db_ref[...].astype(F32)
        nxt = jnp.where(i < steps - 1, next_ref[...].astype(F32), 0.0)
        ext = jnp.concatenate([dbc, nxt], axis=0)
        rows = tm + HALO
        tok = i * tm + lax.broadcasted_iota(jnp.int32, (rows, 1), 0)

        @pl.when(i == 0)
        def _():
            dw_ref[...] = jnp.zeros_like(dw_ref)
            ds_ref[...] = jnp.zeros_like(ds_ref)

        for g, win in enumerate(POOL_WINDOWS):
            cs = slice(g * dg, (g + 1) * dg)
            dp = (ext[:, cs] * s_ref[:, cs]).astype(BF16)
            dd = _dot_nt(dp, w_ref[g])
            s = dd * (1.0 / jnp.minimum(tok + 1, win).astype(F32))
            sh = 1
            while sh < win:
                s = s + pltpu.roll(s, rows - sh, 0)
                sh *= 2
            dc_ref[0, :, cs] = (s[:tm] - dd[:tm]).astype(BF16)
            dfg = diff_ref[:, cs]
            ds_ref[:, cs] += jnp.sum(dbc[:, cs] * _dot(dfg, w_ref[g]), axis=0, keepdims=True)
            dw_ref[g] += _dot_tn(dfg, dp[:tm])

    return pl.pallas_call(
        body, name=name, grid=(steps,),
        in_specs=[_rows(tm, d), pl.BlockSpec((HALO, d), lambda i: (jnp.minimum((i + 1) * per, t // HALO - 1), 0)),
                  _rows(tm, d), _const((ng, dg, dg)), _const((1, d)), ANY_SPEC],
        out_specs=[pl.BlockSpec((1, tm, d), lambda i: (DZ_SLOT[2], i, 0)), _const((ng, dg, dg)), _const((1, d))],
        out_shape=[jax.ShapeDtypeStruct(dz.shape, BF16), jax.ShapeDtypeStruct((ng, dg, dg), F32), jax.ShapeDtypeStruct((1, d), F32)],
        input_output_aliases={5: 0},
        compiler_params=_params("arbitrary"))(db, db, diff, pool_w, pool_scale, dz)


def dw_tn(x, dy, nk, name, x_kmajor=False, dy_mode="same"):
    t = x.shape[-2]
    kx = x.shape[-1]
    n = dy.shape[-1] // nk if dy_mode == "cols" else dy.shape[-1]
    tt = _tile(t, DW_TOKENS)
    steps = t // tt

    def body(x_ref, dy_ref, o_ref, acc_ref):
        s = pl.program_id(1)
        part = _dot_tn(x_ref[0] if x_kmajor else x_ref[...], dy_ref[0] if dy_mode == "kmajor" else dy_ref[...])
        if steps == 1:
            o_ref[0] = part.astype(BF16)
            return

        @pl.when(s == 0)
        def _():
            acc_ref[...] = jnp.zeros_like(acc_ref)

        acc_ref[...] += part

        @pl.when(s == steps - 1)
        def _():
            o_ref[0] = acc_ref[...].astype(BF16)

    x_spec = pl.BlockSpec((1, tt, kx), lambda k, s: (k, s, 0)) if x_kmajor else pl.BlockSpec((tt, kx), lambda k, s: (s, 0))
    dy_spec = {"kmajor": pl.BlockSpec((1, tt, n), lambda k, s: (k, s, 0)), "cols": pl.BlockSpec((tt, n), lambda k, s: (s, k)),
               "same": pl.BlockSpec((tt, n), lambda k, s: (s, 0))}[dy_mode]
    return pl.pallas_call(
        body, name=name, grid=(nk, steps), in_specs=[x_spec, dy_spec],
        out_specs=pl.BlockSpec((1, kx, n), lambda k, s: (k, 0, 0)), out_shape=jax.ShapeDtypeStruct((nk, kx, n), BF16),
        scratch_shapes=[pltpu.VMEM((kx, n) if steps > 1 else (8, 128), F32)],
        compiler_params=_params("parallel", "arbitrary"))(x, dy)


def dw_in_tiles(xn, dz, nk, name):
    t, d = xn.shape
    sections = len(DZ_SLOT)
    per_section = d // DW_IN_TILE
    per_shard = sections * per_section // nk

    def body(x_ref, dy_ref, o_ref):
        o_ref[0] = _dot_tn(x_ref[...], dy_ref[0]).astype(BF16)

    def slot(j):
        return (j // per_section + DZ_SLOT[0]) % sections

    return pl.pallas_call(
        body, name=name, grid=(sections * per_section,),
        in_specs=[pl.BlockSpec((t, d), lambda j: (0, 0)), pl.BlockSpec((1, t, DW_IN_TILE), lambda j: (slot(j), 0, j % per_section))],
        out_specs=pl.BlockSpec((1, d, DW_IN_TILE), lambda j: (j // per_shard, 0, j % per_shard)),
        out_shape=jax.ShapeDtypeStruct((nk, d, sections * d // nk), BF16), compiler_params=_params("parallel"))(xn, dz)


def _place():
    x, y, c = lax.axis_index("x"), lax.axis_index("y"), lax.axis_index("c")
    chips = [((1 - x) if fx else x, (1 - y) if fy else y) for fx, fy in OTHER_CHIPS]
    return x, y, c, chips


def _half(rows, which):
    return pl.ds(pl.multiple_of(which * (rows // 2), 16), rows // 2)


def _hbm(a):
    return pltpu.with_memory_space_constraint(a, pltpu.HBM)


def _by_shape(arrays):
    buckets = {}
    for i, a in enumerate(arrays):
        buckets.setdefault(a.shape, []).append(i)
    return list(buckets.values())


def cast_place(shards, chip, name, deps=(), plain=False):
    n = len(shards)
    r, cdim = shards[0].shape
    tr = _tile(r, 256)

    def body(chip_ref, *refs):
        outs = refs[n + len(deps):]
        for a, w_ref in enumerate(refs[:n]):
            cast = w_ref[...].astype(BF16)
            outs[a][0] = cast
            if plain:
                outs[n + a][...] = cast

    out = pl.pallas_call(
        body, name=name,
        grid_spec=pltpu.PrefetchScalarGridSpec(
            num_scalar_prefetch=1, grid=(r // tr,),
            in_specs=[pl.BlockSpec((tr, cdim), lambda i, chip_ref: (i, 0))] * n + [ANY_SPEC] * len(deps),
            out_specs=[pl.BlockSpec((1, tr, cdim), lambda i, chip_ref: (chip_ref[0], i, 0))] * n
            + [pl.BlockSpec((tr, cdim), lambda i, chip_ref: (i, 0))] * (n * plain)),
        out_shape=[jax.ShapeDtypeStruct((N_CHIPS, r, cdim), BF16)] * n + [jax.ShapeDtypeStruct((r, cdim), BF16)] * (n * plain),
        compiler_params=_params("parallel"))(chip, *shards, *deps)
    return (out[:n], out[n:]) if plain else out


def _gather_copy(buf, sends, recvs, i, j, me, chip_xy, c):
    cx, cy = chip_xy
    mine = _half(buf.shape[1], c)
    return pltpu.make_async_remote_copy(
        src_ref=buf.at[me, mine], dst_ref=buf.at[me, mine], send_sem=sends.at[3 * i + j], recv_sem=recvs.at[3 * i + j],
        device_id=(cx, cy, c), device_id_type=MESH_IDS)


def allgather_start(bufs, name):
    n = len(bufs)

    def body(*refs):
        ins = refs[:n]
        sends, recvs = refs[n], refs[n + 1]
        token = refs[2 * n + 2]
        x, y, c, chips = _place()
        for i in range(n):
            for j, chip_xy in enumerate(chips):
                _gather_copy(ins[i], sends, recvs, i, j, 2 * x + y, chip_xy, c).start()
        token[...] = jnp.zeros_like(token)

    out = pl.pallas_call(
        body, name=name, in_specs=[HBM_SPEC] * n,
        out_specs=[SEM_SPEC, SEM_SPEC] + [HBM_SPEC] * n + [VMEM_SPEC],
        out_shape=[pltpu.SemaphoreType.DMA((3 * n,)), pltpu.SemaphoreType.DMA((3 * n,))]
        + [pltpu.HBM(b.shape, b.dtype) for b in bufs] + [jax.ShapeDtypeStruct((8, 128), F32)],
        input_output_aliases={i: i + 2 for i in range(n)},
        compiler_params=pltpu.CompilerParams(has_side_effects=DATAFLOW))(*[_hbm(b) for b in bufs])
    return out[0], out[1], list(out[2:2 + n]), out[2 + n]


def allgather_wait(sends, recvs, bufs, after, name):
    n = len(bufs)

    def body(*refs):
        ins = refs[:n]
        send_sems, recv_sems = refs[n], refs[n + 1]
        x, y, c, chips = _place()
        for i in range(n):
            for j, (cx, cy) in enumerate(chips):
                mine = _half(ins[i].shape[1], c)
                cp = pltpu.make_async_remote_copy(
                    src_ref=ins[i].at[2 * x + y, mine], dst_ref=ins[i].at[2 * cx + cy, mine], send_sem=send_sems.at[3 * i + j],
                    recv_sem=recv_sems.at[3 * i + j], device_id=(cx, cy, c), device_id_type=MESH_IDS)
                cp.wait_send()
                cp.wait_recv()

    return pl.pallas_call(
        body, name=name, in_specs=[HBM_SPEC] * n + [SEM_SPEC, SEM_SPEC, ANY_SPEC], out_specs=[HBM_SPEC] * n,
        out_shape=[pltpu.HBM(b.shape, b.dtype) for b in bufs], input_output_aliases={i: i for i in range(n)},
        compiler_params=pltpu.CompilerParams(has_side_effects=DATAFLOW))(*bufs, sends, recvs, after)


def d2d_forward(bufs, name):
    n = len(bufs)

    def body(*refs):
        ins = refs[:n]
        send_sems, recv_sems = refs[2 * n:]
        x, y, c, chips = _place()
        copies = []
        for i in range(n):
            mine = _half(ins[i].shape[1], c)
            for j, (cx, cy) in enumerate(chips):
                landed = ins[i].at[2 * cx + cy, mine]
                cp = pltpu.make_async_remote_copy(
                    src_ref=landed, dst_ref=landed, send_sem=send_sems.at[i, j], recv_sem=recv_sems.at[i, j],
                    device_id=(x, y, 1 - c), device_id_type=MESH_IDS)
                cp.start()
                copies.append(cp)
        for i in range(n):
            theirs = _half(ins[i].shape[1], 1 - c)
            for j, (cx, cy) in enumerate(chips):
                passed = ins[i].at[2 * cx + cy, theirs]
                pltpu.make_async_remote_copy(
                    src_ref=passed, dst_ref=passed, send_sem=send_sems.at[i, j], recv_sem=recv_sems.at[i, j],
                    device_id=(x, y, 1 - c), device_id_type=MESH_IDS).wait_recv()
        for cp in copies:
            cp.wait_send()

    return pl.pallas_call(
        body, name=name, in_specs=[HBM_SPEC] * n, out_specs=[HBM_SPEC] * n,
        out_shape=[jax.ShapeDtypeStruct(b.shape, b.dtype) for b in bufs], input_output_aliases={i: i for i in range(n)},
        scratch_shapes=[pltpu.SemaphoreType.DMA((n, 3))] * 2,
        compiler_params=pltpu.CompilerParams(has_side_effects=True))(*bufs)


def _forward_copy(buf, sends, recvs, i, j, chip_xy, x, y, c):
    cx, cy = chip_xy
    rows = buf.shape[1]
    return pltpu.make_async_remote_copy(
        src_ref=buf.at[2 * cx + cy, _half(rows, c)], dst_ref=buf.at[2 * cx + cy, _half(rows, 1 - c)],
        send_sem=sends.at[3 * i + j], recv_sem=recvs.at[3 * i + j], device_id=(x, y, 1 - c), device_id_type=MESH_IDS)


def d2d_forward_start(bufs, name):
    n = len(bufs)

    def body(*refs):
        ins = refs[:n]
        sends, recvs = refs[n], refs[n + 1]
        token = refs[2 * n + 2]
        x, y, c, chips = _place()
        for i in range(n):
            mine = _half(ins[i].shape[1], c)
            for j, (cx, cy) in enumerate(chips):
                landed = ins[i].at[2 * cx + cy, mine]
                pltpu.make_async_remote_copy(
                    src_ref=landed, dst_ref=landed, send_sem=sends.at[3 * i + j], recv_sem=recvs.at[3 * i + j],
                    device_id=(x, y, 1 - c), device_id_type=MESH_IDS).start()
        token[...] = jnp.zeros_like(token)

    out = pl.pallas_call(
        body, name=name, in_specs=[HBM_SPEC] * n,
        out_specs=[SEM_SPEC, SEM_SPEC] + [HBM_SPEC] * n + [VMEM_SPEC],
        out_shape=[pltpu.SemaphoreType.DMA((3 * n,)), pltpu.SemaphoreType.DMA((3 * n,))]
        + [pltpu.HBM(b.shape, b.dtype) for b in bufs] + [jax.ShapeDtypeStruct((8, 128), F32)],
        input_output_aliases={i: i + 2 for i in range(n)},
        compiler_params=pltpu.CompilerParams(has_side_effects=DATAFLOW))(*[_hbm(b) for b in bufs])
    return (out[0], out[1], list(out[2:2 + n])), out[2 + n]


def d2d_forward_wait(started, after, name):
    sends, recvs, bufs = started
    n = len(bufs)

    def body(*refs):
        ins = refs[:n]
        send_sems, recv_sems = refs[n], refs[n + 1]
        x, y, c, chips = _place()
        for i in range(n):
            for j, chip_xy in enumerate(chips):
                cp = _forward_copy(ins[i], send_sems, recv_sems, i, j, chip_xy, x, y, c)
                cp.wait_send()
                cp.wait_recv()

    return pl.pallas_call(
        body, name=name, in_specs=[HBM_SPEC] * n + [SEM_SPEC, SEM_SPEC, ANY_SPEC], out_specs=[HBM_SPEC] * n,
        out_shape=[pltpu.HBM(b.shape, b.dtype) for b in bufs], input_output_aliases={i: i for i in range(n)},
        compiler_params=pltpu.CompilerParams(has_side_effects=DATAFLOW))(*bufs, sends, recvs, after)


def _sibling_copy(src, land, sends, recvs, i, x, y, c, halves):
    part = src.at[:, _half(src.shape[1], 1 - c)] if halves else src
    return pltpu.make_async_remote_copy(
        src_ref=part, dst_ref=land, send_sem=sends.at[i], recv_sem=recvs.at[i], device_id=(x, y, 1 - c),
        device_id_type=MESH_IDS)


def sibling_start(arrays, halves, name):
    n = len(arrays)
    lands = [lax.empty((a.shape[0], a.shape[1] // 2, a.shape[2]) if halves else a.shape, a.dtype) for a in arrays]

    def body(*refs):
        srcs, zones = refs[:n], refs[n:2 * n]
        sends, recvs = refs[2 * n], refs[2 * n + 1]
        token = refs[4 * n + 2]
        x, y, c, _ = _place()
        for i in range(n):
            _sibling_copy(srcs[i], zones[i], sends, recvs, i, x, y, c, halves).start()
        token[...] = jnp.zeros_like(token)

    out = pl.pallas_call(
        body, name=name, in_specs=[HBM_SPEC] * (2 * n),
        out_specs=[SEM_SPEC, SEM_SPEC] + [HBM_SPEC] * (2 * n) + [VMEM_SPEC],
        out_shape=[pltpu.SemaphoreType.DMA((n,)), pltpu.SemaphoreType.DMA((n,))]
        + [pltpu.HBM(a.shape, a.dtype) for a in arrays + lands] + [jax.ShapeDtypeStruct((8, 128), F32)],
        input_output_aliases={i: i + 2 for i in range(2 * n)},
        compiler_params=pltpu.CompilerParams(has_side_effects=DATAFLOW))(*[_hbm(a) for a in arrays + lands])
    return (out[0], out[1], list(out[2:2 + n]), list(out[2 + n:2 + 2 * n])), out[2 + 2 * n]


def sibling_wait(started, halves, after, name):
    sends, recvs, arrays, lands = started
    n = len(arrays)
    after = tuple(after) if isinstance(after, (tuple, list)) else (after,)

    def body(*refs):
        srcs, zones = refs[:n], refs[n:2 * n]
        send_sems, recv_sems = refs[2 * n], refs[2 * n + 1]
        x, y, c, _ = _place()
        for i in range(n):
            cp = _sibling_copy(srcs[i], zones[i], send_sems, recv_sems, i, x, y, c, halves)
            cp.wait_send()
            cp.wait_recv()

    out = pl.pallas_call(
        body, name=name, in_specs=[HBM_SPEC] * (2 * n) + [SEM_SPEC, SEM_SPEC] + [ANY_SPEC] * len(after),
        out_specs=[HBM_SPEC] * (2 * n),
        out_shape=[pltpu.HBM(a.shape, a.dtype) for a in arrays + lands], input_output_aliases={i: i for i in range(2 * n)},
        compiler_params=pltpu.CompilerParams(has_side_effects=DATAFLOW))(*arrays, *lands, sends, recvs, *after)
    return list(out[:n]), list(out[n:])


def _scatter_copy(src, land, sends, recvs, i, j, chip_xy, c):
    cx, cy = chip_xy
    return pltpu.make_async_remote_copy(
        src_ref=src.at[2 * cx + cy], dst_ref=land.at[j], send_sem=sends.at[3 * i + j], recv_sem=recvs.at[3 * i + j],
        device_id=(cx, cy, c), device_id_type=MESH_IDS)


def scatter_start(sums, name):
    n = len(sums)
    lands = [lax.empty((3,) + s.shape[1:], s.dtype) for s in sums]

    def body(*refs):
        srcs, zones = refs[:n], refs[n:2 * n]
        sends, recvs = refs[2 * n], refs[2 * n + 1]
        token = refs[4 * n + 2]
        _, _, c, chips = _place()
        for i in range(n):
            for j, chip_xy in enumerate(chips):
                _scatter_copy(srcs[i], zones[i], sends, recvs, i, j, chip_xy, c).start()
        token[...] = jnp.zeros_like(token)

    out = pl.pallas_call(
        body, name=name, in_specs=[HBM_SPEC] * (2 * n),
        out_specs=[SEM_SPEC, SEM_SPEC] + [HBM_SPEC] * (2 * n) + [VMEM_SPEC],
        out_shape=[pltpu.SemaphoreType.DMA((3 * n,)), pltpu.SemaphoreType.DMA((3 * n,))]
        + [pltpu.HBM(a.shape, a.dtype) for a in sums + lands] + [jax.ShapeDtypeStruct((8, 128), F32)],
        input_output_aliases={i: i + 2 for i in range(2 * n)},
        compiler_params=pltpu.CompilerParams(has_side_effects=DATAFLOW))(*[_hbm(a) for a in sums + lands])
    return out[0], out[1], list(out[2:2 + n]), list(out[2 + n:2 + 2 * n]), out[2 + 2 * n]


def scatter_wait(sends, recvs, sums, lands, after, name):
    n = len(sums)

    def body(*refs):
        srcs, zones = refs[:n], refs[n:2 * n]
        send_sems, recv_sems = refs[2 * n], refs[2 * n + 1]
        _, _, c, chips = _place()
        for i in range(n):
            for j, chip_xy in enumerate(chips):
                cp = _scatter_copy(srcs[i], zones[i], send_sems, recv_sems, i, j, chip_xy, c)
                cp.wait_send()
                cp.wait_recv()

    out = pl.pallas_call(
        body, name=name, in_specs=[HBM_SPEC] * (2 * n) + [SEM_SPEC, SEM_SPEC, ANY_SPEC], out_specs=[HBM_SPEC] * (2 * n),
        out_shape=[pltpu.HBM(a.shape, a.dtype) for a in sums + lands], input_output_aliases={i: i for i in range(2 * n)},
        compiler_params=pltpu.CompilerParams(has_side_effects=DATAFLOW))(*sums, *lands, sends, recvs, after)
    return list(out[:n]), list(out[n:])


def add_halves(grads, recvs, core, name):
    n = len(grads)
    nk, r, cdim = grads[0].shape
    half = r // 2
    views = [g.reshape(nk, 2, half, cdim) for g in grads]

    def body(core_ref, *refs):
        for g_ref, r_ref, o_ref in zip(refs[:n], refs[n:2 * n], refs[2 * n:]):
            o_ref[0] = (g_ref[0, 0].astype(F32) + r_ref[0].astype(F32)).astype(BF16)

    return pl.pallas_call(
        body, name=name,
        grid_spec=pltpu.PrefetchScalarGridSpec(
            num_scalar_prefetch=1, grid=(nk,),
            in_specs=[pl.BlockSpec((1, 1, half, cdim), lambda k, core_ref: (k, core_ref[0], 0, 0))] * n
            + [pl.BlockSpec((1, half, cdim), lambda k, core_ref: (k, 0, 0))] * n,
            out_specs=[pl.BlockSpec((1, half, cdim), lambda k, core_ref: (k, 0, 0))] * n),
        out_shape=[jax.ShapeDtypeStruct((nk, half, cdim), BF16)] * n, compiler_params=_params("parallel"))(core, *views, *recvs)


def add_chips(sums, lands, chip, name):
    n = len(sums)
    _, half, cdim = sums[0].shape
    tr = _tile(half, 128)

    def body(chip_ref, *refs):
        for s_ref, r_ref, o_ref in zip(refs[:n], refs[n:2 * n], refs[2 * n:]):
            o_ref[...] = ((s_ref[0].astype(F32) + r_ref[0].astype(F32)) + r_ref[1].astype(F32)) + r_ref[2].astype(F32)

    return pl.pallas_call(
        body, name=name,
        grid_spec=pltpu.PrefetchScalarGridSpec(
            num_scalar_prefetch=1, grid=(half // tr,),
            in_specs=[pl.BlockSpec((1, tr, cdim), lambda i, chip_ref: (chip_ref[0], i, 0))] * n
            + [pl.BlockSpec((3, tr, cdim), lambda i, chip_ref: (0, i, 0))] * n,
            out_specs=[pl.BlockSpec((tr, cdim), lambda i, chip_ref: (i, 0))] * n),
        out_shape=[jax.ShapeDtypeStruct((half, cdim), F32)] * n, compiler_params=_params("parallel"))(chip, *sums, *lands)


def _adamw_math(w, g, m, v):
    m = ADAM_B1 * m + (1.0 - ADAM_B1) * g
    v = ADAM_B2 * v + (1.0 - ADAM_B2) * (g * g)
    m_hat = m / (1.0 - ADAM_B1 ** ADAM_STEP)
    v_hat = v / (1.0 - ADAM_B2 ** ADAM_STEP)
    return -ADAM_LR * (m_hat / (jnp.sqrt(v_hat) + ADAM_EPS) + ADAM_WD * w), m, v


def adamw_halves(ws, owns, others, ms, vs, core, name, deps=()):
    n = len(ws)
    r, cdim = ws[0].shape
    half = r // 2
    tr = _tile(half, max(8, 256 // n))
    steps = half // tr

    def body(core_ref, *refs):
        ins, outs = refs[:5 * n], refs[5 * n + len(deps):]
        for a in range(n):
            w_ref, own_ref, other_ref, m_ref, v_ref = ins[a::n]
            g = jnp.where(pl.program_id(0) == core_ref[0], own_ref[...], other_ref[...])
            outs[a][...] = g
            outs[n + a][...], outs[2 * n + a][...], outs[3 * n + a][...] = _adamw_math(w_ref[...], g, m_ref[...], v_ref[...])

    whole = pl.BlockSpec((tr, cdim), lambda hh, i, core_ref: (hh * steps + i, 0))
    own = pl.BlockSpec((tr, cdim), lambda hh, i, core_ref: (jnp.where(hh == core_ref[0], i, 0), 0))
    other = pl.BlockSpec((tr, cdim), lambda hh, i, core_ref: (jnp.where(hh == core_ref[0], 0, i), 0))
    out = pl.pallas_call(
        body, name=name,
        grid_spec=pltpu.PrefetchScalarGridSpec(
            num_scalar_prefetch=1, grid=(2, steps),
            in_specs=[whole] * n + [own] * n + [other] * n + [whole] * (2 * n) + [ANY_SPEC] * len(deps),
            out_specs=[whole] * (4 * n)),
        out_shape=[jax.ShapeDtypeStruct((r, cdim), F32)] * (4 * n),
        compiler_params=_params("parallel", "parallel"))(core, *ws, *owns, *others, *ms, *vs, *deps)
    return [out[a::n] for a in range(n)]


def adamw_halves_sparsecore(ws, owns, others, ms, vs, name):
    n = len(ws)
    r, cdim = ws[0].shape
    half_groups = r // 2 // SC_ROWS
    per_tile = -(-2 * half_groups // SC_TILES)

    def body(*refs):
        ins, outs, (wb, gb, mb, vb) = refs[:5 * n], refs[5 * n:9 * n], refs[9 * n:]
        tile = lax.axis_index("sc_tile") * 2 + lax.axis_index("sc_core")
        core = lax.axis_index("c")

        @pl.loop(0, per_tile)
        def _(it):
            group = tile + SC_TILES * it

            @pl.when(group < 2 * half_groups)
            def _():
                rows = pl.ds(group * SC_ROWS, SC_ROWS)
                in_own = (group < half_groups) == (core == 0)
                half_rows = pl.ds((group % half_groups) * SC_ROWS, SC_ROWS)
                for a in range(n):
                    w_hbm, own_hbm, other_hbm, m_hbm, v_hbm = ins[a::n]
                    g_out, d_out, m_out, v_out = outs[a::n]
                    pltpu.sync_copy(w_hbm.at[rows], wb)
                    pltpu.sync_copy(m_hbm.at[rows], mb)
                    pltpu.sync_copy(v_hbm.at[rows], vb)

                    @pl.when(in_own)
                    def _():
                        pltpu.sync_copy(own_hbm.at[half_rows], gb)

                    @pl.when(jnp.logical_not(in_own))
                    def _():
                        pltpu.sync_copy(other_hbm.at[half_rows], gb)

                    pltpu.sync_copy(gb, g_out.at[rows])

                    @pl.loop(0, SC_ROWS)
                    def _(row):
                        @pl.loop(0, cdim, step=SC_LANES)
                        def _(col):
                            at = (row, pl.ds(col, SC_LANES))
                            wb[at], mb[at], vb[at] = _adamw_math(wb[at], gb[at], mb[at], vb[at])

                    pltpu.sync_copy(wb, d_out.at[rows])
                    pltpu.sync_copy(mb, m_out.at[rows])
                    pltpu.sync_copy(vb, v_out.at[rows])

    out = pl.kernel(
        body, name=name, out_type=[jax.ShapeDtypeStruct((r, cdim), F32)] * (4 * n),
        mesh=plsc.VectorSubcoreMesh(core_axis_name="sc_core", subcore_axis_name="sc_tile"),
        scratch_types=[pltpu.VMEM((SC_ROWS, cdim), F32)] * 4)(*ws, *owns, *others, *ms, *vs)
    return [out[a::n] for a in range(n)]


def small_allreduce_adamw(g, w, m, v):
    r, cdim = g.shape

    def body(g_ref, w_ref, m_ref, v_ref, go_ref, d_ref, mo_ref, vo_ref, pair, quad, d2d_send, d2d_recv, ici_send, ici_recv):
        x, y, c, chips = _place()
        me = 2 * x + y
        pair[c] = g_ref[...]
        swap = pltpu.make_async_remote_copy(
            src_ref=g_ref, dst_ref=pair.at[c], send_sem=d2d_send, recv_sem=d2d_recv, device_id=(x, y, 1 - c),
            device_id_type=MESH_IDS)
        swap.start()
        swap.wait()
        quad[me] = pair[0] + pair[1]
        copies = []
        for j, (cx, cy) in enumerate(chips):
            cp = pltpu.make_async_remote_copy(
                src_ref=quad.at[me], dst_ref=quad.at[me], send_sem=ici_send.at[j], recv_sem=ici_recv.at[j],
                device_id=(cx, cy, c), device_id_type=MESH_IDS)
            cp.start()
            copies.append(cp)
        for j, (cx, cy) in enumerate(chips):
            slot = quad.at[2 * cx + cy]
            pltpu.make_async_remote_copy(
                src_ref=slot, dst_ref=slot, send_sem=ici_send.at[j], recv_sem=ici_recv.at[j], device_id=(cx, cy, c),
                device_id_type=MESH_IDS).wait_recv()
        for cp in copies:
            cp.wait_send()
        total = (quad[0] + quad[1]) + (quad[2] + quad[3])
        go_ref[...] = total
        d_ref[...], mo_ref[...], vo_ref[...] = _adamw_math(w_ref[...], total, m_ref[...], v_ref[...])

    return pl.pallas_call(
        body, name="small_allreduce_adamw", in_specs=[VMEM_SPEC] * 4, out_specs=[VMEM_SPEC] * 4,
        out_shape=[jax.ShapeDtypeStruct((r, cdim), F32)] * 4,
        scratch_shapes=[pltpu.VMEM((2, r, cdim), F32), pltpu.VMEM((N_CHIPS, r, cdim), F32), pltpu.SemaphoreType.DMA,
                        pltpu.SemaphoreType.DMA, pltpu.SemaphoreType.DMA((3,)), pltpu.SemaphoreType.DMA((3,))],
        compiler_params=pltpu.CompilerParams(has_side_effects=True, vmem_limit_bytes=VMEM_LIMIT_V7X))(g, w, m, v)


GROUPS = (
    ("ffn1", ("ffn1_w_gate", "ffn1_w_up", "ffn1_w_down")),
    ("mixer", ("w_in", "pool_w", "w_out_a", "w_out_b", "w_o")),
    ("ffn2", ("ffn2_w_gate", "ffn2_w_up", "ffn2_w_down")),
    ("ple", ("ple_w_gate", "ple_w_proj")),
)
GATHERS = (
    ("ffn1_in", ("ffn1_w_gate", "ffn1_w_up")),
    ("ffn1_out", ("ffn1_w_down",)),
) + GROUPS[1:]
GAINS = ("ffn1_pre_g", "ffn1_post_g", "mix_pre_g", "sgu_norm_g", "pool_scale", "mix_post_g",
         "ffn2_pre_g", "ffn2_post_g", "ple_pre_g", "ple_post_g")
SMALL = GAINS + ("sgu_b", "sgu_w")
PACKED = SMALL + ("loss",)
WEIGHTS = ("ffn1_pre_g", "ffn1_w_gate", "ffn1_w_up", "ffn1_w_down", "ffn1_post_g", "mix_pre_g", "w_in", "sgu_norm_g",
           "sgu_w", "sgu_b", "pool_w", "pool_scale", "w_out_a", "w_out_b", "w_o", "mix_post_g", "ffn2_pre_g",
           "ffn2_w_gate", "ffn2_w_up", "ffn2_w_down", "ffn2_post_g", "ple_pre_g", "ple_w_gate", "ple_w_proj", "ple_post_g")
PACK_ROWS = 16


TRANSPOSED = ("ffn1_w_gate", "ffn1_w_up", "ffn2_w_gate", "ffn2_w_up")


def _shard2d(name, a):
    a = a[0]
    return a.T if name in TRANSPOSED else a.reshape(-1, a.shape[-1])


def _unshard2d(name, a2d, shape):
    return (a2d.T if name in TRANSPOSED else a2d).reshape(shape)


def pack_rows(gains, sgu_b, loss_tile, name):
    n = len(gains)
    d = gains[0].shape[1]
    g = sgu_b.shape[0]

    def body(*refs):
        o_ref = refs[-1]
        o_ref[...] = jnp.zeros_like(o_ref)
        for i in range(n):
            o_ref[i:i + 1, :] = refs[i][...]
        o_ref[n:n + g, 0:CHUNK] = refs[n][...]
        o_ref[n + g:n + g + 1, 0:CHUNK] = refs[n + 1][0:1, :]

    return pl.pallas_call(
        body, name=name, in_specs=[VMEM_SPEC] * (n + 2), out_specs=VMEM_SPEC,
        out_shape=jax.ShapeDtypeStruct((PACK_ROWS, d), F32))(*gains, sgu_b, loss_tile)


def _pack_small(parts, tag):
    d = parts[GAINS[0]].shape[-1]
    rows = pack_rows([parts[n] for n in GAINS], parts["sgu_b"].reshape(-1, CHUNK), parts["loss"], "pack_" + tag)
    return jnp.concatenate([rows, parts["sgu_w"].reshape(-1, d)], axis=0)


def _unpack_small(packed, like):
    n, g = len(GAINS), like["sgu_b"].size // CHUNK
    out = {name: packed[i:i + 1] for i, name in enumerate(GAINS)}
    out["sgu_b"] = packed[n:n + g, :CHUNK].reshape(like["sgu_b"].shape)
    out["loss"] = packed[n + g, 0]
    out["sgu_w"] = packed[PACK_ROWS:].reshape(like["sgu_w"].shape)
    return out


def _ffn_fwd(xn, h, w, pre, g_post, g_next, tag, between=None, first=None):
    a4, s4, t4 = ffn_gu(xn, w[pre + "w_gate"], w[pre + "w_up"], tag + "_gu") if first is None else first(xn)
    deps = ()
    if between is not None:
        more, deps = between(a4)
        w.update(more)
    f, h_new, xn_next = mm_norm_res(a4, w[pre + "w_down"], h, g_post, g_next, 0.5, tag + "_down", deps=deps)
    return dict(xn=xn, h=h, a4=a4, s4=s4, t4=t4, f=f), h_new, xn_next


def _ffn_bwd_w(dh, saved, w, pre, g_post, tag, deps):
    df, dg4, du4, d_post = ffn_bwd_a(dh, saved["f"], g_post, w[pre + "w_down"], saved["s4"], saved["t4"], 0.5, tag + "_bwd_a",
                                     deps=deps)
    return dg4, du4, _ffn_dw(saved, df, dg4, du4, pre, tag), d_post


def _ffn_dw(saved, df, dg4, du4, pre, tag):
    nk = N_CHIPS
    return {
        pre + "w_down": dw_tn(saved["a4"], df, nk, tag + "_dw_down", x_kmajor=True, dy_mode="same"),
        pre + "w_gate": dw_tn(dg4, saved["xn"], nk, tag + "_dw_gate", x_kmajor=True, dy_mode="same"),
        pre + "w_up": dw_tn(du4, saved["xn"], nk, tag + "_dw_up", x_kmajor=True, dy_mode="same"),
    }


def _ffn_bwd_x(dh, dg4, du4, saved, w, pre, g_pre, tag, deps):
    return dx_norm_bwd([(dg4, w[pre + "w_gate"], False), (du4, w[pre + "w_up"], False)], saved["h"], g_pre, dh,
                       tag + "_bwd_x", deps=deps)


def _gather_group(names, weights, chip, tag, deps, plain=False):
    shards = [_shard2d(n, weights[n]) for n in names]
    bufs = [None] * len(names)
    own = None
    for b, idx in enumerate(_by_shape(shards)):
        made = cast_place([shards[i] for i in idx], chip, f"cast_{tag}_{b}", deps=deps, plain=plain)
        if plain:
            made, own = made
        for i, buf in zip(idx, made):
            bufs[i] = buf
    sends, recvs, bufs, token = allgather_start(bufs, "allgather_start_" + tag)
    return ((sends, recvs, bufs), token, dict(zip(names, own))) if plain else ((sends, recvs, bufs), token)


def _gathered(started, names, after, tag):
    sends, recvs, bufs = started
    landed = allgather_wait(sends, recvs, bufs, after, "allgather_wait_" + tag)
    return dict(zip(names, d2d_forward(landed, "d2d_forward_" + tag)))


def _forward_early(started, after, tag):
    sends, recvs, bufs = started
    landed = allgather_wait(sends, recvs, bufs, after, "allgather_wait_" + tag)
    return d2d_forward_start(landed, "d2d_forward_start_" + tag)


def _forwarded(forwarding, names, after, tag):
    return dict(zip(names, d2d_forward_wait(forwarding, after, "d2d_forward_wait_" + tag)))


def _exchange_start(names, big_g, tag):
    return sibling_start([big_g[n] for n in names], True, "exchange_start_" + tag)


def _scatter_begin(names, exchanging, after, core, tag):
    partial, from_sibling = sibling_wait(exchanging, True, after, "exchange_wait_" + tag)
    chip_sums = [None] * len(names)
    for b, idx in enumerate(_by_shape(partial)):
        for i, s in zip(idx, add_halves([partial[i] for i in idx], [from_sibling[i] for i in idx], core, f"add_halves_{tag}_{b}")):
            chip_sums[i] = s
    sends, recvs, sums, lands, token = scatter_start(chip_sums, "scatter_start_" + tag)
    return (names, sends, recvs, sums, lands), token


def _share_begin(scattering, after, chip, tag):
    names, sends, recvs, sums, lands = scattering
    sums, lands = scatter_wait(sends, recvs, sums, lands, after, "scatter_wait_" + tag)
    reduced = [None] * len(names)
    for b, idx in enumerate(_by_shape(sums)):
        for i, r in zip(idx, add_chips([sums[i] for i in idx], [lands[i] for i in idx], chip, f"add_chips_{tag}_{b}")):
            reduced[i] = r
    sharing, token = sibling_start(reduced, False, "share_start_" + tag)
    return (names, sharing), token


def _update(shared, after, weights, moments_m, moments_v, core, tag, results, held):
    names, sharing = shared
    reduced, others = sibling_wait(sharing, False, after, "share_wait_" + tag)
    buckets = _by_shape(reduced)

    def update_bucket(b, idx, deps):
        in_bucket = [names[i] for i in idx]
        operands = ([_shard2d(n, weights[n]) for n in in_bucket], [reduced[i] for i in idx], [others[i] for i in idx],
                    [_shard2d(n, moments_m[n]) for n in in_bucket], [_shard2d(n, moments_v[n]) for n in in_bucket])
        if held is None:
            outs = adamw_halves_sparsecore(*operands, f"adamw_{tag}_{b}")
        else:
            outs = adamw_halves(*operands, core, f"adamw_{tag}_{b}", deps=deps)
        for n, per_weight in zip(in_bucket, outs):
            for store, value in zip(results, per_weight):
                store[n] = _unshard2d(n, value, weights[n].shape)
        return outs[-1][1]

    if held is None:
        for b, idx in enumerate(buckets):
            update_bucket(b, idx, ())
        return None
    for b, idx in enumerate(buckets[:-1]):
        held.append(functools.partial(update_bucket, b, idx))
    return update_bucket(len(buckets) - 1, buckets[-1], ())


def kernel(x, p, ffn1_pre_g, ffn1_w_gate, ffn1_w_up, ffn1_w_down, ffn1_post_g, mix_pre_g, w_in, sgu_norm_g, sgu_w, sgu_b, pool_w, pool_scale, w_out_a, w_out_b, w_o, mix_post_g, ffn2_pre_g, ffn2_w_gate, ffn2_w_up, ffn2_w_down, ffn2_post_g, ple_pre_g, ple_w_gate, ple_w_proj, ple_post_g, loss_target, m_ffn1_pre_g, m_ffn1_w_gate, m_ffn1_w_up, m_ffn1_w_down, m_ffn1_post_g, m_mix_pre_g, m_w_in, m_sgu_norm_g, m_sgu_w, m_sgu_b, m_pool_w, m_pool_scale, m_w_out_a, m_w_out_b, m_w_o, m_mix_post_g, m_ffn2_pre_g, m_ffn2_w_gate, m_ffn2_w_up, m_ffn2_w_down, m_ffn2_post_g, m_ple_pre_g, m_ple_w_gate, m_ple_w_proj, m_ple_post_g, v_ffn1_pre_g, v_ffn1_w_gate, v_ffn1_w_up, v_ffn1_w_down, v_ffn1_post_g, v_mix_pre_g, v_w_in, v_sgu_norm_g, v_sgu_w, v_sgu_b, v_pool_w, v_pool_scale, v_w_out_a, v_w_out_b, v_w_o, v_mix_post_g, v_ffn2_pre_g, v_ffn2_w_gate, v_ffn2_w_up, v_ffn2_w_down, v_ffn2_post_g, v_ple_pre_g, v_ple_w_gate, v_ple_w_proj, v_ple_post_g):
    given = dict(locals())
    weights = {n: given[n] for n in WEIGHTS}
    moments_m = {n: given["m_" + n] for n in WEIGHTS}
    moments_v = {n: given["v_" + n] for n in WEIGHTS}
    core = lax.axis_index("c").astype(jnp.int32).reshape(1)
    chip = (2 * lax.axis_index("x") + lax.axis_index("y")).astype(jnp.int32).reshape(1)

    d = x.shape[-1]
    token = ()
    gathering = {}
    for tag, names in GATHERS:
        if tag == GATHERS[0][0]:
            gathering[tag], tok, own_first = _gather_group(names, weights, chip, tag, token, plain=True)
        else:
            gathering[tag], tok = _gather_group(names, weights, chip, tag, token)
        token = (tok,)
    gain = {n: weights[n] for n in GAINS}
    sgu_w3 = sgu_w[0]
    sgu_b3 = sgu_b[0][:, :, None]
    groups = dict(GROUPS + GATHERS)

    h0 = x[0]
    tgt = loss_target[0]
    p_bf = p[0, 0].astype(BF16)
    xn1 = rms_cast(h0, gain["ffn1_pre_g"], "ffn1_pre_norm")
    w = {}

    def ffn1_first(xn):
        own = ffn_gu(xn, own_first["ffn1_w_gate"][None], own_first["ffn1_w_up"][None], "ffn1_gu_own", chip=chip)
        w.update(_gathered(gathering["ffn1_in"], groups["ffn1_in"], own[0], "ffn1_in"))
        return ffn_gu(xn, w["ffn1_w_gate"], w["ffn1_w_up"], "ffn1_gu", chip=chip, into=own)

    s1, h1, xn2 = _ffn_fwd(xn1, h0, w, "ffn1_", gain["ffn1_post_g"], gain["mix_pre_g"], "ffn1",
                           lambda a4: (_gathered(gathering["ffn1_out"], groups["ffn1_out"], a4, "ffn1_out"), ()), ffn1_first)
    w.update(_gathered(gathering["mixer"], groups["mixer"], h1, "mixer"))
    full = {n: w[n].reshape(-1, d) for n in ("w_out_a", "w_out_b", "w_o")}
    n_groups = pool_w.shape[1]
    rows_per = pool_w.shape[2]
    dgp = pool_w.shape[3]
    pool_full = w["pool_w"].reshape(N_CHIPS, n_groups, rows_per, dgp).transpose(1, 0, 2, 3).reshape(n_groups, N_CHIPS * rows_per, dgp)
    z = mixer_in(xn2, w["w_in"], "mixer_in")
    a = sgu_fwd(z, gain["sgu_norm_g"], sgu_w3, sgu_b3, "sgu_fwd")
    diff, b = pool_fwd(z, pool_full, gain["pool_scale"], "pool_fwd")
    ya, yb, y = mixer_y(a, b, z, full["w_out_a"], full["w_out_b"], "mixer_y")
    forwarding, tok = _forward_early(gathering["ffn2"], y, "ffn2")
    m, h2, xn3 = mm_norm_res(y[None], full["w_o"][None], h1, gain["mix_post_g"], gain["ffn2_pre_g"], 1.0, "mixer_out",
                             deps=(tok,))
    w.update(_forwarded(forwarding, groups["ffn2"], h2, "ffn2"))
    early = {}

    def forward_ple(a4):
        early["ple"], tok_ple = _forward_early(gathering["ple"], a4, "ple")
        return {}, (tok_ple,)

    s2, h3, xn4 = _ffn_fwd(xn3, h2, w, "ffn2_", gain["ffn2_post_g"], gain["ple_pre_g"], "ffn2", forward_ple)
    w.update(_forwarded(early["ple"], groups["ple"], h3, "ple"))
    full["ple_w_gate"] = w["ple_w_gate"].reshape(-1, d)
    proj_full = w["ple_w_proj"].transpose(1, 0, 2).reshape(ple_w_proj.shape[1], -1)

    small_g, big_g = {}, {}
    ds, de, dh3, small_g["ple_post_g"], small_g["ple_pre_g"], loss_part = ple_loss(
        xn4, p_bf, full["ple_w_gate"], proj_full, h3, gain["ple_post_g"], gain["ple_pre_g"], tgt, "ple_loss")
    small_g["loss"] = loss_part
    grad, delta, new_m, new_v = {}, {}, {}, {}
    results = (grad, delta, new_m, new_v)
    update = functools.partial(_update, weights=weights, moments_m=moments_m, moments_v=moments_v, core=core, results=results)

    def reduce_behind(tag, previous):
        exchanging, tok = _exchange_start(groups[tag], big_g, tag)
        if previous is not None:
            shared, tok = _share_begin(previous[1], tok, chip, previous[0])
        scattering, tok_scatter = _scatter_begin(groups[tag], exchanging, tok, core, tag)
        if previous is not None:
            update(shared, tok_scatter, tag=previous[0], held=None)
        return (tag, scattering), (tok_scatter,)

    big_g["ple_w_gate"] = dw_tn(xn4, ds, 1, "dw_ple_gate").reshape(N_CHIPS, -1, d)
    big_g["ple_w_proj"] = dw_tn(p_bf, de, N_CHIPS, "dw_ple_proj", dy_mode="cols")
    reducing, deps = reduce_behind("ple", None)

    dg4, du4, g2, small_g["ffn2_post_g"] = _ffn_bwd_w(dh3, s2, w, "ffn2_", gain["ffn2_post_g"], "ffn2", deps)
    big_g.update(g2)
    dh2, small_g["ffn2_pre_g"] = _ffn_bwd_x(dh3, dg4, du4, s2, w, "ffn2_", gain["ffn2_pre_g"], "ffn2", ())
    reducing, deps = reduce_behind("ffn2", reducing)

    dm, dya, dyb, dz, da, db, small_g["mix_post_g"] = mixer_bwd_y(
        dh2, m, gain["mix_post_g"], full["w_o"], ya, yb, z, full["w_out_a"], full["w_out_b"], "mixer_bwd_y", deps=deps)
    big_g["w_o"] = dw_tn(y, dm, 1, "dw_o").reshape(N_CHIPS, -1, d)
    big_g["w_out_a"] = dw_tn(a, dya, 1, "dw_out_a").reshape(N_CHIPS, -1, d)
    big_g["w_out_b"] = dw_tn(b, dyb, 1, "dw_out_b").reshape(N_CHIPS, -1, d)
    dz, d_sgu_w, d_sgu_b, small_g["sgu_norm_g"] = sgu_bwd(z, da, dz, gain["sgu_norm_g"], sgu_w3, sgu_b3, "sgu_bwd")
    dz, d_pool_w, small_g["pool_scale"] = pool_bwd(db, diff, dz, pool_full, gain["pool_scale"], "pool_bwd")
    big_g["pool_w"] = d_pool_w.astype(BF16).reshape(n_groups, N_CHIPS, rows_per, dgp).transpose(1, 0, 2, 3).reshape(
        N_CHIPS, n_groups * rows_per, dgp)
    big_g["w_in"] = dw_in_tiles(xn2, dz, N_CHIPS, "dw_in")
    dh1, small_g["mix_pre_g"] = dx_norm_bwd([(dz, w["w_in"], True)], h1, gain["mix_pre_g"], dh2, "mixer_bwd_x")
    reducing, deps = reduce_behind("mixer", reducing)

    dg4, du4, g1, small_g["ffn1_post_g"] = _ffn_bwd_w(dh1, s1, w, "ffn1_", gain["ffn1_post_g"], "ffn1", deps)
    big_g.update(g1)
    reducing, deps = reduce_behind("ffn1", reducing)
    dh0, small_g["ffn1_pre_g"] = _ffn_bwd_x(dh1, dg4, du4, s1, w, "ffn1_", gain["ffn1_pre_g"], "ffn1", deps)
    shared, tok = _share_begin(reducing[1], dh0, chip, "ffn1")
    update(shared, tok, tag="ffn1", held=[])
    small_g["sgu_w"] = d_sgu_w
    small_g["sgu_b"] = d_sgu_b[:, :, 0]

    no_state = {"loss": jnp.zeros((8, 128), F32)}
    packed = small_allreduce_adamw(
        _pack_small(small_g, "grads"), _pack_small({n: weights[n] for n in SMALL} | no_state, "weights"),
        _pack_small({n: moments_m[n] for n in SMALL} | no_state, "m"), _pack_small({n: moments_v[n] for n in SMALL} | no_state, "v"))
    like = {n: weights[n] for n in SMALL}
    for store, block in zip((grad, delta, new_m, new_v), packed):
        store.update(_unpack_small(block, like))

    return (grad["loss"], dh0[None], *[grad[n] for n in WEIGHTS], *[delta[n] for n in WEIGHTS],
            *[new_m[n] for n in WEIGHTS], *[new_v[n] for n in WEIGHTS])
```

```python
import functools

import jax
import jax.numpy as jnp
from jax import lax
from jax.experimental import pallas as pl
from jax.experimental.pallas import tpu as pltpu
from jax.experimental.pallas import tpu_sc as plsc

F32 = jnp.float32
BF16 = jnp.bfloat16
EPS = 1e-6
CHUNK = 128
POOL_WINDOWS = (2, 4, 8, 16)
HALO = 16
N_CHIPS = 4
ADAM_LR, ADAM_B1, ADAM_B2, ADAM_EPS, ADAM_WD, ADAM_STEP = 0.001, 0.9, 0.999, 1e-08, 0.01, 10
VMEM_LIMIT_V7X = 58 * 1024 * 1024
MESH_IDS = pl.DeviceIdType.MESH
HBM_SPEC = pl.BlockSpec(memory_space=pltpu.HBM)
VMEM_SPEC = pl.BlockSpec(memory_space=pltpu.VMEM)
SEM_SPEC = pl.BlockSpec(memory_space=pltpu.SEMAPHORE)
ANY_SPEC = pl.BlockSpec(memory_space=pl.ANY)
DATAFLOW = pltpu.SideEffectType.DATAFLOW_SIDE_EFFECTING
OTHER_CHIPS = ((1, 0), (0, 1), (1, 1))
DZ_SLOT = (2, 3, 4, 0, 1)
SC_TILES, SC_LANES, SC_ROWS = 32, 16, 8
DW_TOKENS = 4096
DW_IN_TILE = 256

NT = (((1,), (1,)), ((), ()))
TN = (((0,), (0,)), ((), ()))


def _params(*sem, **more):
    return pltpu.CompilerParams(dimension_semantics=sem or None, vmem_limit_bytes=VMEM_LIMIT_V7X, **more)


def _tile(t, want):
    return max(c for c in range(8, min(t, want) + 1, 8) if t % c == 0)


def _const(shape):
    return pl.BlockSpec(shape, lambda *_: (0,) * len(shape))


def _rows(tm, d, col=0):
    return pl.BlockSpec((tm, d), lambda i: (i, col))


def _kmajor(nk, tm, kb):
    return pl.BlockSpec((nk, tm, kb), lambda i: (0, i, 0))


def _dot(a, b):
    return jnp.dot(a, b, preferred_element_type=F32)


def _dot_nt(a, b):
    return lax.dot_general(a, b, NT, preferred_element_type=F32)


def _dot_tn(a, b):
    return lax.dot_general(a, b, TN, preferred_element_type=F32)


def _gelu(x):
    return 0.5 * x * (1.0 + jnp.tanh(0.7978845608028654 * (x + 0.044715 * x * x * x)))


def _gelu_and_grad(x):
    k, kc = 0.7978845608028654, 0.7978845608028654 * 0.044715
    x2 = x * x
    th = jnp.tanh(x * (k + kc * x2))
    cdf = 0.5 + 0.5 * th
    return x * cdf, cdf + x * (0.5 - 0.5 * th * th) * (k + 3.0 * kc * x2)


def _sigmoid(x):
    return 1.0 / (1.0 + jnp.exp(-x))


def _rstd(h):
    return lax.rsqrt(jnp.mean(h * h, axis=-1, keepdims=True) + EPS)


def _rms_bwd(h, g, dy):
    r = _rstd(h)
    t = dy * g
    dh = r * t - h * (r * r * r) * jnp.mean(h * t, axis=-1, keepdims=True)
    return dh, jnp.sum(dy * h * r, axis=0, keepdims=True)


def _ordered_after(body, n_in, deps):
    if not deps:
        return body
    return lambda *refs: body(*refs[:n_in], *refs[n_in + len(deps):])


def _accumulate(ref, value):
    @pl.when(pl.program_id(0) == 0)
    def _():
        ref[...] = jnp.zeros_like(ref)

    ref[...] += value


def rms_cast(h, g, name):
    t, d = h.shape
    tm = _tile(t, 512)

    def body(h_ref, g_ref, o_ref):
        hv = h_ref[...]
        o_ref[...] = (hv * _rstd(hv) * g_ref[...]).astype(BF16)

    return pl.pallas_call(
        body, name=name, grid=(t // tm,), in_specs=[_rows(tm, d), _const((1, d))], out_specs=_rows(tm, d),
        out_shape=jax.ShapeDtypeStruct((t, d), BF16), compiler_params=_params("parallel"))(h, g)


def ffn_gu(xn, wgt, wut, name, chip=None, into=None):
    t, d = xn.shape
    nkw, fk, _ = wgt.shape
    tm = _tile(t, 512)
    count = N_CHIPS if chip is None else (1 if into is None else N_CHIPS - 1)
    first = 0 if into is None else 1

    def body(*refs):
        x_ref, wg_ref, wu_ref = refs[-6 - (0 if into is None else 3):][:3]
        a_ref, s_ref, t_ref = refs[-3:]
        xv = x_ref[...]
        g = _dot_nt(xv, wg_ref[0])
        sg = _sigmoid(g)
        s = g * sg
        s_ref[0] = s.astype(BF16)
        ds = sg * (1.0 + g * (1.0 - sg))
        u = _dot_nt(xv, wu_ref[0])
        a_ref[0] = (s * u).astype(BF16)
        t_ref[0] = (u * ds).astype(BF16)

    def slot(k, *chip_ref):
        return (chip_ref[0][0] + first + k) % N_CHIPS if chip_ref else k

    w_spec = pl.BlockSpec((1, fk, d), lambda k, i, *c: (slot(k, *c) if nkw > 1 else 0, 0, 0))
    o_spec = pl.BlockSpec((1, tm, fk), lambda k, i, *c: (slot(k, *c), i, 0))
    shape = jax.ShapeDtypeStruct((N_CHIPS, t, fk), BF16)
    prior = () if into is None else tuple(into)
    call = pl.pallas_call(
        body, name=name,
        grid_spec=pltpu.PrefetchScalarGridSpec(
            num_scalar_prefetch=0 if chip is None else 1, grid=(count, t // tm),
            in_specs=[pl.BlockSpec((tm, d), lambda k, i, *c: (i, 0)), w_spec, w_spec] + [ANY_SPEC] * len(prior),
            out_specs=[o_spec] * 3),
        out_shape=[shape] * 3, input_output_aliases={4 + j: j for j in range(len(prior))},
        compiler_params=_params("parallel", "parallel"))
    return call(*(() if chip is None else (chip,)), xn, wgt, wut, *prior)


def mm_norm_res(a3, w3, h_old, g_post, g_next, scale, name, deps=()):
    nk, t, kb = a3.shape
    d = w3.shape[2]
    tm = _tile(t, 256)

    def body(a_ref, w_ref, h_ref, gp_ref, gn_ref, f_ref, hn_ref, xn_ref):
        f = _dot(a_ref[0], w_ref[0])
        for k in range(1, nk):
            f += _dot(a_ref[k], w_ref[k])
        f_ref[...] = f.astype(BF16)
        hn = h_ref[...] + scale * (f * _rstd(f) * gp_ref[...])
        hn_ref[...] = hn
        xn_ref[...] = (hn * _rstd(hn) * gn_ref[...]).astype(BF16)

    return pl.pallas_call(
        _ordered_after(body, 5, deps), name=name, grid=(t // tm,),
        in_specs=[_kmajor(nk, tm, kb), _const((nk, kb, d)), _rows(tm, d), _const((1, d)), _const((1, d))] + [ANY_SPEC] * len(deps),
        out_specs=[_rows(tm, d)] * 3,
        out_shape=[jax.ShapeDtypeStruct((t, d), BF16), jax.ShapeDtypeStruct((t, d), F32), jax.ShapeDtypeStruct((t, d), BF16)],
        compiler_params=_params("parallel"))(a3, w3, h_old, g_post, g_next, *deps)


def mixer_in(xn, win4, name):
    t, d = xn.shape
    nk, _, nb = win4.shape
    tm = _tile(t, 512)

    def body(x_ref, w_ref, z_ref):
        z_ref[...] = _dot(x_ref[...], w_ref[0]).astype(BF16)

    return pl.pallas_call(
        body, name=name, grid=(nk, t // tm),
        in_specs=[pl.BlockSpec((tm, d), lambda k, i: (i, 0)), pl.BlockSpec((1, d, nb), lambda k, i: (k, 0, 0))],
        out_specs=pl.BlockSpec((tm, nb), lambda k, i: (i, k)), out_shape=jax.ShapeDtypeStruct((t, nk * nb), BF16),
        compiler_params=_params("parallel", "parallel"))(xn, win4)


def _causal_mask():
    row = lax.broadcasted_iota(jnp.int32, (CHUNK, CHUNK), 0)
    col = lax.broadcasted_iota(jnp.int32, (CHUNK, CHUNK), 1)
    return row >= col


def _layernorm_parts(v):
    mu = jnp.mean(v, axis=-1, keepdims=True)
    vc = v - mu
    r = lax.rsqrt(jnp.mean(vc * vc, axis=-1, keepdims=True) + EPS)
    return vc * r, r


def sgu_fwd(z, norm_g, sgu_w, sgu_b3, name):
    t = z.shape[0]
    d = norm_g.shape[1]
    ng = sgu_w.shape[0]
    dg = d // ng
    tm = _tile(t, 256)

    def body(zu_ref, zv_ref, ng_ref, w_ref, b_ref, a_ref):
        vhat, _ = _layernorm_parts(_gelu(zv_ref[...].astype(F32)))
        vn = (vhat * ng_ref[...]).astype(BF16)
        u = _gelu(zu_ref[...].astype(F32))
        mask = _causal_mask()
        for g in range(ng):
            wg = jnp.where(mask, w_ref[g], 0.0).astype(BF16)
            for ci in range(tm // CHUNK):
                rs, cs = slice(ci * CHUNK, (ci + 1) * CHUNK), slice(g * dg, (g + 1) * dg)
                sv = _dot(wg, vn[rs, cs]) + b_ref[g]
                a_ref[rs, cs] = (u[rs, cs] * sv).astype(BF16)

    return pl.pallas_call(
        body, name=name, grid=(t // tm,),
        in_specs=[_rows(tm, d, 0), _rows(tm, d, 1), _const((1, d)), _const((ng, CHUNK, CHUNK)), _const((ng, CHUNK, 1))],
        out_specs=_rows(tm, d), out_shape=jax.ShapeDtypeStruct((t, d), BF16),
        compiler_params=_params("parallel"))(z, z, norm_g, sgu_w, sgu_b3)


def pool_fwd(z, pool_w, pool_scale, name):
    t = z.shape[0]
    d = pool_scale.shape[1]
    ng = pool_w.shape[0]
    dg = d // ng
    tm = _tile(t, 256)
    per = tm // HALO

    def body(c_ref, prev_ref, w_ref, s_ref, diff_ref, b_ref):
        i = pl.program_id(0)
        cur = c_ref[...].astype(F32)
        prev = jnp.where(i > 0, prev_ref[...].astype(F32), 0.0)
        ext = jnp.concatenate([prev, cur], axis=0)
        tok = i * tm + lax.broadcasted_iota(jnp.int32, (tm, 1), 0)
        for g, win in enumerate(POOL_WINDOWS):
            cs = slice(g * dg, (g + 1) * dg)
            s = ext[:, cs]
            sh = 1
            while sh < win:
                s = s + pltpu.roll(s, sh, 0)
                sh *= 2
            per_count = 1.0 / jnp.minimum(tok + 1, win).astype(F32)
            diff = (s[HALO:] * per_count - cur[:, cs]).astype(BF16)
            diff_ref[:, cs] = diff
            b_ref[:, cs] = (_dot(diff, w_ref[g]) * s_ref[:, cs]).astype(BF16)

    return pl.pallas_call(
        body, name=name, grid=(t // tm,),
        in_specs=[_rows(tm, d, 2), pl.BlockSpec((HALO, d), lambda i: (jnp.maximum(i * per - 1, 0), 2)),
                  _const((ng, dg, dg)), _const((1, d))],
        out_specs=[_rows(tm, d)] * 2, out_shape=[jax.ShapeDtypeStruct((t, d), BF16)] * 2,
        compiler_params=_params("parallel"))(z, z, pool_w, pool_scale)


def mixer_y(a, b, z, woa, wob, name):
    t, d = a.shape
    tm = _tile(t, 256)

    def body(a_ref, b_ref, ga_ref, gb_ref, wa_ref, wb_ref, ya_ref, yb_ref, y_ref):
        ya = _dot(a_ref[...], wa_ref[...])
        yb = _dot(b_ref[...], wb_ref[...])
        ya_ref[...] = ya.astype(BF16)
        yb_ref[...] = yb.astype(BF16)
        y_ref[...] = (_sigmoid(ga_ref[...].astype(F32)) * ya + _sigmoid(gb_ref[...].astype(F32)) * yb).astype(BF16)

    return pl.pallas_call(
        body, name=name, grid=(t // tm,),
        in_specs=[_rows(tm, d), _rows(tm, d), _rows(tm, d, 3), _rows(tm, d, 4), _const((d, d)), _const((d, d))],
        out_specs=[_rows(tm, d)] * 3, out_shape=[jax.ShapeDtypeStruct((t, d), BF16)] * 3,
        compiler_params=_params("parallel"))(a, b, z, z, woa, wob)


def ple_loss(xn, p, wpg, wpp, h, g_post, g_pre, target, name):
    t, d = xn.shape
    dp = p.shape[1]
    tm = _tile(t, 256)

    def body(x_ref, p_ref, wg_ref, wp_ref, h_ref, gp_ref, gn_ref, tg_ref, ds_ref, de_ref, dhp_ref, dgp_ref, dgn_ref, loss_ref):
        gate = _sigmoid(_dot(x_ref[...], wg_ref[...]))
        e = _dot(p_ref[...], wp_ref[...])
        q = gate * e
        hv = h_ref[...]
        err = hv + q * _rstd(q) * gp_ref[...] - tg_ref[...]
        _accumulate(loss_ref, jnp.full(loss_ref.shape, (0.5 / d) * jnp.sum(err * err), F32))
        dhv = err * (1.0 / d)
        dq, dgp = _rms_bwd(q, gp_ref[...], dhv)
        ds = (dq * e * gate * (1.0 - gate)).astype(BF16)
        ds_ref[...] = ds
        de_ref[...] = (dq * gate).astype(BF16)
        dx, dgn = _rms_bwd(hv, gn_ref[...], _dot_nt(ds, wg_ref[...]))
        dhp_ref[...] = dhv + dx
        _accumulate(dgp_ref, dgp)
        _accumulate(dgn_ref, dgn)

    return pl.pallas_call(
        body, name=name, grid=(t // tm,),
        in_specs=[_rows(tm, d), _rows(tm, dp), _const((d, d)), _const((dp, d)), _rows(tm, d), _const((1, d)), _const((1, d)),
                  _rows(tm, d)],
        out_specs=[_rows(tm, d)] * 3 + [_const((1, d))] * 2 + [_const((8, 128))],
        out_shape=[jax.ShapeDtypeStruct((t, d), BF16), jax.ShapeDtypeStruct((t, d), BF16), jax.ShapeDtypeStruct((t, d), F32),
                   jax.ShapeDtypeStruct((1, d), F32), jax.ShapeDtypeStruct((1, d), F32), jax.ShapeDtypeStruct((8, 128), F32)],
        compiler_params=_params("arbitrary"))(xn, p, wpg, wpp, h, g_post, g_pre, target)


def ffn_bwd_a(dh, f, g_post, wd4, s4, t4, scale, name, deps=()):
    t, d = dh.shape
    nk, fk, _ = wd4.shape
    tm = _tile(t, 256)

    def body(dh_ref, f_ref, gp_ref, w_ref, s_ref, t_ref, df_ref, dg_ref, du_ref, dgp_ref):
        df, dgp = _rms_bwd(f_ref[...].astype(F32), gp_ref[...], dh_ref[...])
        df = (scale * df).astype(BF16)
        df_ref[...] = df
        _accumulate(dgp_ref, scale * dgp)
        for k in range(nk):
            da = _dot_nt(df, w_ref[k])
            du_ref[k] = (da * s_ref[k].astype(F32)).astype(BF16)
            dg_ref[k] = (da * t_ref[k].astype(F32)).astype(BF16)

    return pl.pallas_call(
        _ordered_after(body, 6, deps), name=name, grid=(t // tm,),
        in_specs=[_rows(tm, d), _rows(tm, d), _const((1, d)), _const((nk, fk, d)), _kmajor(nk, tm, fk), _kmajor(nk, tm, fk)]
        + [ANY_SPEC] * len(deps),
        out_specs=[_rows(tm, d), _kmajor(nk, tm, fk), _kmajor(nk, tm, fk), _const((1, d))],
        out_shape=[jax.ShapeDtypeStruct((t, d), BF16), jax.ShapeDtypeStruct((nk, t, fk), BF16),
                   jax.ShapeDtypeStruct((nk, t, fk), BF16), jax.ShapeDtypeStruct((1, d), F32)],
        compiler_params=_params("arbitrary"))(dh, f, g_post, wd4, s4, t4, *deps)


def dx_norm_bwd(pairs, h, g_pre, dh_in, name, deps=()):
    t, d = h.shape
    tm = _tile(t, 256)
    n = len(pairs)

    def body(*refs):
        dys, ws = refs[:n], refs[n:2 * n]
        h_ref, g_ref, dhi_ref, dho_ref, dg_ref = refs[2 * n:]
        acc = None
        for (_, w4, sections), dy_ref, w_ref in zip(pairs, dys, ws):
            if sections:
                wide = w4.shape[2]
                edges = sorted(set(range(0, 5 * d + 1, d)) | set(range(0, 5 * d + 1, wide)))
                parts = [_dot_nt(dy_ref[DZ_SLOT[lo // d], :, lo % d:lo % d + hi - lo], w_ref[lo // wide, :, lo % wide:lo % wide + hi - lo])
                         for lo, hi in zip(edges[:-1], edges[1:])]
            else:
                parts = [_dot(dy_ref[k], w_ref[k]) for k in range(w4.shape[0])]
            for part in parts:
                acc = part if acc is None else acc + part
        dx, dg = _rms_bwd(h_ref[...], g_ref[...], acc)
        dho_ref[...] = dhi_ref[...] + dx
        _accumulate(dg_ref, dg)

    dy_specs = [_kmajor(dy.shape[0], tm, dy.shape[2]) for dy, _, _ in pairs]
    return pl.pallas_call(
        _ordered_after(body, 2 * n + 3, deps), name=name, grid=(t // tm,),
        in_specs=dy_specs + [_const(w4.shape) for _, w4, _ in pairs] + [_rows(tm, d), _const((1, d)), _rows(tm, d)]
        + [ANY_SPEC] * len(deps),
        out_specs=[_rows(tm, d), _const((1, d))],
        out_shape=[jax.ShapeDtypeStruct((t, d), F32), jax.ShapeDtypeStruct((1, d), F32)],
        compiler_params=_params("arbitrary"))(*[dy for dy, _, _ in pairs], *[w4 for _, w4, _ in pairs], h, g_pre, dh_in, *deps)


def mixer_bwd_y(dh, m, g_post, w_o, ya, yb, z, woa, wob, name, deps=()):
    t, d = dh.shape
    tm = _tile(t, 256)

    def body(dh_ref, m_ref, gp_ref, wo_ref, ya_ref, yb_ref, ga_ref, gb_ref, wa_ref, wb_ref,
             dm_ref, dya_ref, dyb_ref, dz_ref, da_ref, db_ref, dgp_ref):
        dm, dgp = _rms_bwd(m_ref[...].astype(F32), gp_ref[...], dh_ref[...])
        dm = dm.astype(BF16)
        dm_ref[...] = dm
        _accumulate(dgp_ref, dgp)
        dy = _dot_nt(dm, wo_ref[...])
        sa = _sigmoid(ga_ref[...].astype(F32))
        sb = _sigmoid(gb_ref[...].astype(F32))
        dya = (dy * sa).astype(BF16)
        dyb = (dy * sb).astype(BF16)
        dya_ref[...] = dya
        dyb_ref[...] = dyb
        dz_ref[0] = (dy * ya_ref[...].astype(F32) * sa * (1.0 - sa)).astype(BF16)
        dz_ref[1] = (dy * yb_ref[...].astype(F32) * sb * (1.0 - sb)).astype(BF16)
        da_ref[...] = _dot_nt(dya, wa_ref[...]).astype(BF16)
        db_ref[...] = _dot_nt(dyb, wb_ref[...]).astype(BF16)

    return pl.pallas_call(
        _ordered_after(body, 10, deps), name=name, grid=(t // tm,),
        in_specs=[_rows(tm, d), _rows(tm, d), _const((1, d)), _const((d, d)), _rows(tm, d), _rows(tm, d),
                  _rows(tm, d, 3), _rows(tm, d, 4), _const((d, d)), _const((d, d))] + [ANY_SPEC] * len(deps),
        out_specs=[_rows(tm, d)] * 3 + [pl.BlockSpec((2, tm, d), lambda i: (0, i, 0))] + [_rows(tm, d)] * 2 + [_const((1, d))],
        out_shape=[jax.ShapeDtypeStruct((t, d), BF16)] * 3 + [jax.ShapeDtypeStruct((5, t, d), BF16)]
        + [jax.ShapeDtypeStruct((t, d), BF16)] * 2 + [jax.ShapeDtypeStruct((1, d), F32)],
        compiler_params=_params("arbitrary"))(dh, m, g_post, w_o, ya, yb, z, z, woa, wob, *deps)


def sgu_bwd(z, da, dz, norm_g, sgu_w, sgu_b3, name):
    t, d = da.shape
    ng = sgu_w.shape[0]
    dg = d // ng
    tm = _tile(t, 256)
    steps = t // tm

    def body(zu_ref, zv_ref, da_ref, ng_ref, w_ref, b_ref, _, dz_ref, dw_ref, db_ref, dng_ref, dvn_ref, dsv_ref):
        i = pl.program_id(0)
        zv = zv_ref[...].astype(F32)
        zu = zu_ref[...].astype(F32)
        v, gv = _gelu_and_grad(zv)
        vhat, r = _layernorm_parts(v)
        gain = ng_ref[...]
        vn = (vhat * gain).astype(BF16)
        u, gu = _gelu_and_grad(zu)
        dav = da_ref[...].astype(F32)
        mask = _causal_mask()

        @pl.when(i == 0)
        def _():
            dw_ref[...] = jnp.zeros_like(dw_ref)
            dsv_ref[...] = jnp.zeros_like(dsv_ref)

        for g in range(ng):
            wg = jnp.where(mask, w_ref[g], 0.0).astype(BF16)
            dw = jnp.zeros((CHUNK, CHUNK), F32)
            dsv_sum = jnp.zeros((CHUNK, dg), F32)
            for ci in range(tm // CHUNK):
                rs, cs = slice(ci * CHUNK, (ci + 1) * CHUNK), slice(g * dg, (g + 1) * dg)
                vn_blk = vn[rs, cs]
                sv = _dot(wg, vn_blk) + b_ref[g]
                dz_ref[0, rs, cs] = (dav[rs, cs] * sv * gu[rs, cs]).astype(BF16)
                dsv = dav[rs, cs] * u[rs, cs]
                dsv_sum += dsv
                dsv = dsv.astype(BF16)
                dw += _dot_nt(dsv, vn_blk)
                dvn_ref[rs, cs] = _dot_tn(wg, dsv)
            dw_ref[g] += dw
            dsv_ref[:, cs] += dsv_sum

        dvn = dvn_ref[...]
        _accumulate(dng_ref, jnp.sum(dvn * vhat, axis=0, keepdims=True))
        dvh = dvn * gain
        dv = r * (dvh - jnp.mean(dvh, axis=-1, keepdims=True) - vhat * jnp.mean(dvh * vhat, axis=-1, keepdims=True))
        dz_ref[1] = (dv * gv).astype(BF16)

        @pl.when(i == steps - 1)
        def _():
            for g in range(ng):
                dw_ref[g] = jnp.where(mask, dw_ref[g], 0.0)
                row_sum = jnp.sum(dsv_ref[:, g * dg:(g + 1) * dg], axis=1, keepdims=True)
                db_ref[g] = jnp.broadcast_to(row_sum, (CHUNK, CHUNK))

    return pl.pallas_call(
        body, name=name, grid=(steps,),
        in_specs=[_rows(tm, d, 0), _rows(tm, d, 1), _rows(tm, d), _const((1, d)), _const((ng, CHUNK, CHUNK)), _const((ng, CHUNK, 1)),
                  ANY_SPEC],
        out_specs=[pl.BlockSpec((2, tm, d), lambda i: (DZ_SLOT[0] // 2, i, 0)), _const((ng, CHUNK, CHUNK)),
                   _const((ng, CHUNK, CHUNK)), _const((1, d))],
        out_shape=[jax.ShapeDtypeStruct(dz.shape, BF16), jax.ShapeDtypeStruct((ng, CHUNK, CHUNK), F32),
                   jax.ShapeDtypeStruct((ng, CHUNK, CHUNK), F32), jax.ShapeDtypeStruct((1, d), F32)],
        scratch_shapes=[pltpu.VMEM((tm, d), F32), pltpu.VMEM((CHUNK, d), F32)], input_output_aliases={6: 0},
        compiler_params=_params("arbitrary"))(z, z, da, norm_g, sgu_w, sgu_b3, dz)


def pool_bwd(db, diff, dz, pool_w, pool_scale, name):
    t, d = db.shape
    ng = pool_w.shape[0]
    dg = d // ng
    tm = _tile(t, 256)
    per = tm // HALO
    steps = t // tm

    def body(db_ref, next_ref, diff_ref, w_ref, s_ref, _, dc_ref, dw_ref, ds_ref):
        i = pl.program_id(0)
        dbc = db_ref[...].astype(F32)
        nxt = jnp.where(i < steps - 1, next_ref[...].astype(F32), 0.0)
        ext = jnp.concatenate([dbc, nxt], axis=0)
        rows = tm + HALO
        tok = i * tm + lax.broadcasted_iota(jnp.int32, (rows, 1), 0)

        @pl.when(i == 0)
        def _():
            dw_ref[...] = jnp.zeros_like(dw_ref)
            ds_ref[...] = jnp.zeros_like(ds_ref)

        for g, win in enumerate(POOL_WINDOWS):
            cs = slice(g * dg, (g + 1) * dg)
            dp = (ext[:, cs] * s_ref[:, cs]).astype(BF16)
            dd = _dot_nt(dp, w_ref[g])
            s = dd * (1.0 / jnp.minimum(tok + 1, win).astype(F32))
            sh = 1
            while sh < win:
                s = s + pltpu.roll(s, rows - sh, 0)
                sh *= 2
            dc_ref[0, :, cs] = (s[:tm] - dd[:tm]).astype(BF16)
            dfg = diff_ref[:, cs]
            ds_ref[:, cs] += jnp.sum(dbc[:, cs] * _dot(dfg, w_ref[g]), axis=0, keepdims=True)
            dw_ref[g] += _dot_tn(dfg, dp[:tm])

    return pl.pallas_call(
        body, name=name, grid=(steps,),
        in_specs=[_rows(tm, d), pl.BlockSpec((HALO, d), lambda i: (jnp.minimum((i + 1) * per, t // HALO - 1), 0)),
                  _rows(tm, d), _const((ng, dg, dg)), _const((1, d)), ANY_SPEC],
        out_specs=[pl.BlockSpec((1, tm, d), lambda i: (DZ_SLOT[2], i, 0)), _const((ng, dg, dg)), _const((1, d))],
        out_shape=[jax.ShapeDtypeStruct(dz.shape, BF16), jax.ShapeDtypeStruct((ng, dg, dg), F32), jax.ShapeDtypeStruct((1, d), F32)],
        input_output_aliases={5: 0},
        compiler_params=_params("arbitrary"))(db, db, diff, pool_w, pool_scale, dz)


def dw_tn(x, dy, nk, name, x_kmajor=False, dy_mode="same"):
    t = x.shape[-2]
    kx = x.shape[-1]
    n = dy.shape[-1] // nk if dy_mode == "cols" else dy.shape[-1]
    tt = _tile(t, DW_TOKENS)
    steps = t // tt

    def body(x_ref, dy_ref, o_ref, acc_ref):
        s = pl.program_id(1)
        part = _dot_tn(x_ref[0] if x_kmajor else x_ref[...], dy_ref[0] if dy_mode == "kmajor" else dy_ref[...])
        if steps == 1:
            o_ref[0] = part.astype(BF16)
            return

        @pl.when(s == 0)
        def _():
            acc_ref[...] = jnp.zeros_like(acc_ref)

        acc_ref[...] += part

        @pl.when(s == steps - 1)
        def _():
            o_ref[0] = acc_ref[...].astype(BF16)

    x_spec = pl.BlockSpec((1, tt, kx), lambda k, s: (k, s, 0)) if x_kmajor else pl.BlockSpec((tt, kx), lambda k, s: (s, 0))
    dy_spec = {"kmajor": pl.BlockSpec((1, tt, n), lambda k, s: (k, s, 0)), "cols": pl.BlockSpec((tt, n), lambda k, s: (s, k)),
               "same": pl.BlockSpec((tt, n), lambda k, s: (s, 0))}[dy_mode]
    return pl.pallas_call(
        body, name=name, grid=(nk, steps), in_specs=[x_spec, dy_spec],
        out_specs=pl.BlockSpec((1, kx, n), lambda k, s: (k, 0, 0)), out_shape=jax.ShapeDtypeStruct((nk, kx, n), BF16),
        scratch_shapes=[pltpu.VMEM((kx, n) if steps > 1 else (8, 128), F32)],
        compiler_params=_params("parallel", "arbitrary"))(x, dy)


def dw_in_tiles(xn, dz, nk, name):
    t, d = xn.shape
    sections = len(DZ_SLOT)
    per_section = d // DW_IN_TILE
    per_shard = sections * per_section // nk

    def body(x_ref, dy_ref, o_ref):
        o_ref[0] = _dot_tn(x_ref[...], dy_ref[0]).astype(BF16)

    def slot(j):
        return (j // per_section + DZ_SLOT[0]) % sections

    return pl.pallas_call(
        body, name=name, grid=(sections * per_section,),
        in_specs=[pl.BlockSpec((t, d), lambda j: (0, 0)), pl.BlockSpec((1, t, DW_IN_TILE), lambda j: (slot(j), 0, j % per_section))],
        out_specs=pl.BlockSpec((1, d, DW_IN_TILE), lambda j: (j // per_shard, 0, j % per_shard)),
        out_shape=jax.ShapeDtypeStruct((nk, d, sections * d // nk), BF16), compiler_params=_params("parallel"))(xn, dz)


def _place():
    x, y, c = lax.axis_index("x"), lax.axis_index("y"), lax.axis_index("c")
    chips = [((1 - x) if fx else x, (1 - y) if fy else y) for fx, fy in OTHER_CHIPS]
    return x, y, c, chips


def _half(rows, which):
    return pl.ds(pl.multiple_of(which * (rows // 2), 16), rows // 2)


def _hbm(a):
    return pltpu.with_memory_space_constraint(a, pltpu.HBM)


def _by_shape(arrays):
    buckets = {}
    for i, a in enumerate(arrays):
        buckets.setdefault(a.shape, []).append(i)
    return list(buckets.values())


def cast_place(shards, chip, name, deps=(), plain=False):
    n = len(shards)
    r, cdim = shards[0].shape
    tr = _tile(r, 256)

    def body(chip_ref, *refs):
        outs = refs[n + len(deps):]
        for a, w_ref in enumerate(refs[:n]):
            cast = w_ref[...].astype(BF16)
            outs[a][0] = cast
            if plain:
                outs[n + a][...] = cast

    out = pl.pallas_call(
        body, name=name,
        grid_spec=pltpu.PrefetchScalarGridSpec(
            num_scalar_prefetch=1, grid=(r // tr,),
            in_specs=[pl.BlockSpec((tr, cdim), lambda i, chip_ref: (i, 0))] * n + [ANY_SPEC] * len(deps),
            out_specs=[pl.BlockSpec((1, tr, cdim), lambda i, chip_ref: (chip_ref[0], i, 0))] * n
            + [pl.BlockSpec((tr, cdim), lambda i, chip_ref: (i, 0))] * (n * plain)),
        out_shape=[jax.ShapeDtypeStruct((N_CHIPS, r, cdim), BF16)] * n + [jax.ShapeDtypeStruct((r, cdim), BF16)] * (n * plain),
        compiler_params=_params("parallel"))(chip, *shards, *deps)
    return (out[:n], out[n:]) if plain else out


def _gather_copy(buf, sends, recvs, i, j, me, chip_xy, c):
    cx, cy = chip_xy
    mine = _half(buf.shape[1], c)
    return pltpu.make_async_remote_copy(
        src_ref=buf.at[me, mine], dst_ref=buf.at[me, mine], send_sem=sends.at[3 * i + j], recv_sem=recvs.at[3 * i + j],
        device_id=(cx, cy, c), device_id_type=MESH_IDS)


def allgather_start(bufs, name):
    n = len(bufs)

    def body(*refs):
        ins = refs[:n]
        sends, recvs = refs[n], refs[n + 1]
        token = refs[2 * n + 2]
        x, y, c, chips = _place()
        for i in range(n):
            for j, chip_xy in enumerate(chips):
                _gather_copy(ins[i], sends, recvs, i, j, 2 * x + y, chip_xy, c).start()
        token[...] = jnp.zeros_like(token)

    out = pl.pallas_call(
        body, name=name, in_specs=[HBM_SPEC] * n,
        out_specs=[SEM_SPEC, SEM_SPEC] + [HBM_SPEC] * n + [VMEM_SPEC],
        out_shape=[pltpu.SemaphoreType.DMA((3 * n,)), pltpu.SemaphoreType.DMA((3 * n,))]
        + [pltpu.HBM(b.shape, b.dtype) for b in bufs] + [jax.ShapeDtypeStruct((8, 128), F32)],
        input_output_aliases={i: i + 2 for i in range(n)},
        compiler_params=pltpu.CompilerParams(has_side_effects=DATAFLOW))(*[_hbm(b) for b in bufs])
    return out[0], out[1], list(out[2:2 + n]), out[2 + n]


def allgather_wait(sends, recvs, bufs, after, name):
    n = len(bufs)

    def body(*refs):
        ins = refs[:n]
        send_sems, recv_sems = refs[n], refs[n + 1]
        x, y, c, chips = _place()
        for i in range(n):
            for j, (cx, cy) in enumerate(chips):
                mine = _half(ins[i].shape[1], c)
                cp = pltpu.make_async_remote_copy(
                    src_ref=ins[i].at[2 * x + y, mine], dst_ref=ins[i].at[2 * cx + cy, mine], send_sem=send_sems.at[3 * i + j],
                    recv_sem=recv_sems.at[3 * i + j], device_id=(cx, cy, c), device_id_type=MESH_IDS)
                cp.wait_send()
                cp.wait_recv()

    return pl.pallas_call(
        body, name=name, in_specs=[HBM_SPEC] * n + [SEM_SPEC, SEM_SPEC, ANY_SPEC], out_specs=[HBM_SPEC] * n,
        out_shape=[pltpu.HBM(b.shape, b.dtype) for b in bufs], input_output_aliases={i: i for i in range(n)},
        compiler_params=pltpu.CompilerParams(has_side_effects=DATAFLOW))(*bufs, sends, recvs, after)


def d2d_forward(bufs, name):
    n = len(bufs)

    def body(*refs):
        ins = refs[:n]
        send_sems, recv_sems = refs[2 * n:]
        x, y, c, chips = _place()
        copies = []
        for i in range(n):
            mine = _half(ins[i].shape[1], c)
            for j, (cx, cy) in enumerate(chips):
                landed = ins[i].at[2 * cx + cy, mine]
                cp = pltpu.make_async_remote_copy(
                    src_ref=landed, dst_ref=landed, send_sem=send_sems.at[i, j], recv_sem=recv_sems.at[i, j],
                    device_id=(x, y, 1 - c), device_id_type=MESH_IDS)
                cp.start()
                copies.append(cp)
        for i in range(n):
            theirs = _half(ins[i].shape[1], 1 - c)
            for j, (cx, cy) in enumerate(chips):
                passed = ins[i].at[2 * cx + cy, theirs]
                pltpu.make_async_remote_copy(
                    src_ref=passed, dst_ref=passed, send_sem=send_sems.at[i, j], recv_sem=recv_sems.at[i, j],
                    device_id=(x, y, 1 - c), device_id_type=MESH_IDS).wait_recv()
        for cp in copies:
            cp.wait_send()

    return pl.pallas_call(
        body, name=name, in_specs=[HBM_SPEC] * n, out_specs=[HBM_SPEC] * n,
        out_shape=[jax.ShapeDtypeStruct(b.shape, b.dtype) for b in bufs], input_output_aliases={i: i for i in range(n)},
        scratch_shapes=[pltpu.SemaphoreType.DMA((n, 3))] * 2,
        compiler_params=pltpu.CompilerParams(has_side_effects=True))(*bufs)


def _forward_copy(buf, sends, recvs, i, j, chip_xy, x, y, c):
    cx, cy = chip_xy
    rows = buf.shape[1]
    return pltpu.make_async_remote_copy(
        src_ref=buf.at[2 * cx + cy, _half(rows, c)], dst_ref=buf.at[2 * cx + cy, _half(rows, 1 - c)],
        send_sem=sends.at[3 * i + j], recv_sem=recvs.at[3 * i + j], device_id=(x, y, 1 - c), device_id_type=MESH_IDS)


def d2d_forward_start(bufs, name):
    n = len(bufs)

    def body(*refs):
        ins = refs[:n]
        sends, recvs = refs[n], refs[n + 1]
        token = refs[2 * n + 2]
        x, y, c, chips = _place()
        for i in range(n):
            mine = _half(ins[i].shape[1], c)
            for j, (cx, cy) in enumerate(chips):
                landed = ins[i].at[2 * cx + cy, mine]
                pltpu.make_async_remote_copy(
                    src_ref=landed, dst_ref=landed, send_sem=sends.at[3 * i + j], recv_sem=recvs.at[3 * i + j],
                    device_id=(x, y, 1 - c), device_id_type=MESH_IDS).start()
        token[...] = jnp.zeros_like(token)

    out = pl.pallas_call(
        body, name=name, in_specs=[HBM_SPEC] * n,
        out_specs=[SEM_SPEC, SEM_SPEC] + [HBM_SPEC] * n + [VMEM_SPEC],
        out_shape=[pltpu.SemaphoreType.DMA((3 * n,)), pltpu.SemaphoreType.DMA((3 * n,))]
        + [pltpu.HBM(b.shape, b.dtype) for b in bufs] + [jax.ShapeDtypeStruct((8, 128), F32)],
        input_output_aliases={i: i + 2 for i in range(n)},
        compiler_params=pltpu.CompilerParams(has_side_effects=DATAFLOW))(*[_hbm(b) for b in bufs])
    return (out[0], out[1], list(out[2:2 + n])), out[2 + n]


def d2d_forward_wait(started, after, name):
    sends, recvs, bufs = started
    n = len(bufs)

    def body(*refs):
        ins = refs[:n]
        send_sems, recv_sems = refs[n], refs[n + 1]
        x, y, c, chips = _place()
        for i in range(n):
            for j, chip_xy in enumerate(chips):
                cp = _forward_copy(ins[i], send_sems, recv_sems, i, j, chip_xy, x, y, c)
                cp.wait_send()
                cp.wait_recv()

    return pl.pallas_call(
        body, name=name, in_specs=[HBM_SPEC] * n + [SEM_SPEC, SEM_SPEC, ANY_SPEC], out_specs=[HBM_SPEC] * n,
        out_shape=[pltpu.HBM(b.shape, b.dtype) for b in bufs], input_output_aliases={i: i for i in range(n)},
        compiler_params=pltpu.CompilerParams(has_side_effects=DATAFLOW))(*bufs, sends, recvs, after)


def _sibling_copy(src, land, sends, recvs, i, x, y, c, halves):
    part = src.at[:, _half(src.shape[1], 1 - c)] if halves else src
    return pltpu.make_async_remote_copy(
        src_ref=part, dst_ref=land, send_sem=sends.at[i], recv_sem=recvs.at[i], device_id=(x, y, 1 - c),
        device_id_type=MESH_IDS)


def sibling_start(arrays, halves, name):
    n = len(arrays)
    lands = [lax.empty((a.shape[0], a.shape[1] // 2, a.shape[2]) if halves else a.shape, a.dtype) for a in arrays]

    def body(*refs):
        srcs, zones = refs[:n], refs[n:2 * n]
        sends, recvs = refs[2 * n], refs[2 * n + 1]
        token = refs[4 * n + 2]
        x, y, c, _ = _place()
        for i in range(n):
            _sibling_copy(srcs[i], zones[i], sends, recvs, i, x, y, c, halves).start()
        token[...] = jnp.zeros_like(token)

    out = pl.pallas_call(
        body, name=name, in_specs=[HBM_SPEC] * (2 * n),
        out_specs=[SEM_SPEC, SEM_SPEC] + [HBM_SPEC] * (2 * n) + [VMEM_SPEC],
        out_shape=[pltpu.SemaphoreType.DMA((n,)), pltpu.SemaphoreType.DMA((n,))]
        + [pltpu.HBM(a.shape, a.dtype) for a in arrays + lands] + [jax.ShapeDtypeStruct((8, 128), F32)],
        input_output_aliases={i: i + 2 for i in range(2 * n)},
        compiler_params=pltpu.CompilerParams(has_side_effects=DATAFLOW))(*[_hbm(a) for a in arrays + lands])
    return (out[0], out[1], list(out[2:2 + n]), list(out[2 + n:2 + 2 * n])), out[2 + 2 * n]


def sibling_wait(started, halves, after, name):
    sends, recvs, arrays, lands = started
    n = len(arrays)
    after = tuple(after) if isinstance(after, (tuple, list)) else (after,)

    def body(*refs):
        srcs, zones = refs[:n], refs[n:2 * n]
        send_sems, recv_sems = refs[2 * n], refs[2 * n + 1]
        x, y, c, _ = _place()
        for i in range(n):
            cp = _sibling_copy(srcs[i], zones[i], send_sems, recv_sems, i, x, y, c, halves)
            cp.wait_send()
            cp.wait_recv()

    out = pl.pallas_call(
        body, name=name, in_specs=[HBM_SPEC] * (2 * n) + [SEM_SPEC, SEM_SPEC] + [ANY_SPEC] * len(after),
        out_specs=[HBM_SPEC] * (2 * n),
        out_shape=[pltpu.HBM(a.shape, a.dtype) for a in arrays + lands], input_output_aliases={i: i for i in range(2 * n)},
        compiler_params=pltpu.CompilerParams(has_side_effects=DATAFLOW))(*arrays, *lands, sends, recvs, *after)
    return list(out[:n]), list(out[n:])


def _scatter_copy(src, land, sends, recvs, i, j, chip_xy, c):
    cx, cy = chip_xy
    return pltpu.make_async_remote_copy(
        src_ref=src.at[2 * cx + cy], dst_ref=land.at[j], send_sem=sends.at[3 * i + j], recv_sem=recvs.at[3 * i + j],
        device_id=(cx, cy, c), device_id_type=MESH_IDS)


def scatter_start(sums, name):
    n = len(sums)
    lands = [lax.empty((3,) + s.shape[1:], s.dtype) for s in sums]

    def body(*refs):
        srcs, zones = refs[:n], refs[n:2 * n]
        sends, recvs = refs[2 * n], refs[2 * n + 1]
        token = refs[4 * n + 2]
        _, _, c, chips = _place()
        for i in range(n):
            for j, chip_xy in enumerate(chips):
                _scatter_copy(srcs[i], zones[i], sends, recvs, i, j, chip_xy, c).start()
        token[...] = jnp.zeros_like(token)

    out = pl.pallas_call(
        body, name=name, in_specs=[HBM_SPEC] * (2 * n),
        out_specs=[SEM_SPEC, SEM_SPEC] + [HBM_SPEC] * (2 * n) + [VMEM_SPEC],
        out_shape=[pltpu.SemaphoreType.DMA((3 * n,)), pltpu.SemaphoreType.DMA((3 * n,))]
        + [pltpu.HBM(a.shape, a.dtype) for a in sums + lands] + [jax.ShapeDtypeStruct((8, 128), F32)],
        input_output_aliases={i: i + 2 for i in range(2 * n)},
        compiler_params=pltpu.CompilerParams(has_side_effects=DATAFLOW))(*[_hbm(a) for a in sums + lands])
    return out[0], out[1], list(out[2:2 + n]), list(out[2 + n:2 + 2 * n]), out[2 + 2 * n]


def scatter_wait(sends, recvs, sums, lands, after, name):
    n = len(sums)

    def body(*refs):
        srcs, zones = refs[:n], refs[n:2 * n]
        send_sems, recv_sems = refs[2 * n], refs[2 * n + 1]
        _, _, c, chips = _place()
        for i in range(n):
            for j, chip_xy in enumerate(chips):
                cp = _scatter_copy(srcs[i], zones[i], send_sems, recv_sems, i, j, chip_xy, c)
                cp.wait_send()
                cp.wait_recv()

    out = pl.pallas_call(
        body, name=name, in_specs=[HBM_SPEC] * (2 * n) + [SEM_SPEC, SEM_SPEC, ANY_SPEC], out_specs=[HBM_SPEC] * (2 * n),
        out_shape=[pltpu.HBM(a.shape, a.dtype) for a in sums + lands], input_output_aliases={i: i for i in range(2 * n)},
        compiler_params=pltpu.CompilerParams(has_side_effects=DATAFLOW))(*sums, *lands, sends, recvs, after)
    return list(out[:n]), list(out[n:])


def add_halves(grads, recvs, core, name):
    n = len(grads)
    nk, r, cdim = grads[0].shape
    half = r // 2
    views = [g.reshape(nk, 2, half, cdim) for g in grads]

    def body(core_ref, *refs):
        for g_ref, r_ref, o_ref in zip(refs[:n], refs[n:2 * n], refs[2 * n:]):
            o_ref[0] = (g_ref[0, 0].astype(F32) + r_ref[0].astype(F32)).astype(BF16)

    return pl.pallas_call(
        body, name=name,
        grid_spec=pltpu.PrefetchScalarGridSpec(
            num_scalar_prefetch=1, grid=(nk,),
            in_specs=[pl.BlockSpec((1, 1, half, cdim), lambda k, core_ref: (k, core_ref[0], 0, 0))] * n
            + [pl.BlockSpec((1, half, cdim), lambda k, core_ref: (k, 0, 0))] * n,
            out_specs=[pl.BlockSpec((1, half, cdim), lambda k, core_ref: (k, 0, 0))] * n),
        out_shape=[jax.ShapeDtypeStruct((nk, half, cdim), BF16)] * n, compiler_params=_params("parallel"))(core, *views, *recvs)


def add_chips(sums, lands, chip, name):
    n = len(sums)
    _, half, cdim = sums[0].shape
    tr = _tile(half, 128)

    def body(chip_ref, *refs):
        for s_ref, r_ref, o_ref in zip(refs[:n], refs[n:2 * n], refs[2 * n:]):
            o_ref[...] = ((s_ref[0].astype(F32) + r_ref[0].astype(F32)) + r_ref[1].astype(F32)) + r_ref[2].astype(F32)

    return pl.pallas_call(
        body, name=name,
        grid_spec=pltpu.PrefetchScalarGridSpec(
            num_scalar_prefetch=1, grid=(half // tr,),
            in_specs=[pl.BlockSpec((1, tr, cdim), lambda i, chip_ref: (chip_ref[0], i, 0))] * n
            + [pl.BlockSpec((3, tr, cdim), lambda i, chip_ref: (0, i, 0))] * n,
            out_specs=[pl.BlockSpec((tr, cdim), lambda i, chip_ref: (i, 0))] * n),
        out_shape=[jax.ShapeDtypeStruct((half, cdim), F32)] * n, compiler_params=_params("parallel"))(chip, *sums, *lands)


def _adamw_math(w, g, m, v):
    m = ADAM_B1 * m + (1.0 - ADAM_B1) * g
    v = ADAM_B2 * v + (1.0 - ADAM_B2) * (g * g)
    m_hat = m / (1.0 - ADAM_B1 ** ADAM_STEP)
    v_hat = v / (1.0 - ADAM_B2 ** ADAM_STEP)
    return -ADAM_LR * (m_hat / (jnp.sqrt(v_hat) + ADAM_EPS) + ADAM_WD * w), m, v


def adamw_halves(ws, owns, others, ms, vs, core, name, deps=()):
    n = len(ws)
    r, cdim = ws[0].shape
    half = r // 2
    tr = _tile(half, max(8, 256 // n))
    steps = half // tr

    def body(core_ref, *refs):
        ins, outs = refs[:5 * n], refs[5 * n + len(deps):]
        for a in range(n):
            w_ref, own_ref, other_ref, m_ref, v_ref = ins[a::n]
            g = jnp.where(pl.program_id(0) == core_ref[0], own_ref[...], other_ref[...])
            outs[a][...] = g
            outs[n + a][...], outs[2 * n + a][...], outs[3 * n + a][...] = _adamw_math(w_ref[...], g, m_ref[...], v_ref[...])

    whole = pl.BlockSpec((tr, cdim), lambda hh, i, core_ref: (hh * steps + i, 0))
    own = pl.BlockSpec((tr, cdim), lambda hh, i, core_ref: (jnp.where(hh == core_ref[0], i, 0), 0))
    other = pl.BlockSpec((tr, cdim), lambda hh, i, core_ref: (jnp.where(hh == core_ref[0], 0, i), 0))
    out = pl.pallas_call(
        body, name=name,
        grid_spec=pltpu.PrefetchScalarGridSpec(
            num_scalar_prefetch=1, grid=(2, steps),
            in_specs=[whole] * n + [own] * n + [other] * n + [whole] * (2 * n) + [ANY_SPEC] * len(deps),
            out_specs=[whole] * (4 * n)),
        out_shape=[jax.ShapeDtypeStruct((r, cdim), F32)] * (4 * n),
        compiler_params=_params("parallel", "parallel"))(core, *ws, *owns, *others, *ms, *vs, *deps)
    return [out[a::n] for a in range(n)]


def adamw_halves_sparsecore(ws, owns, others, ms, vs, name):
    n = len(ws)
    r, cdim = ws[0].shape
    half_groups = r // 2 // SC_ROWS
    per_tile = -(-2 * half_groups // SC_TILES)

    def body(*refs):
        ins, outs, (wb, gb, mb, vb) = refs[:5 * n], refs[5 * n:9 * n], refs[9 * n:]
        tile = lax.axis_index("sc_tile") * 2 + lax.axis_index("sc_core")
        core = lax.axis_index("c")

        @pl.loop(0, per_tile)
        def _(it):
            group = tile + SC_TILES * it

            @pl.when(group < 2 * half_groups)
            def _():
                rows = pl.ds(group * SC_ROWS, SC_ROWS)
                in_own = (group < half_groups) == (core == 0)
                half_rows = pl.ds((group % half_groups) * SC_ROWS, SC_ROWS)
                for a in range(n):
                    w_hbm, own_hbm, other_hbm, m_hbm, v_hbm = ins[a::n]
                    g_out, d_out, m_out, v_out = outs[a::n]
                    pltpu.sync_copy(w_hbm.at[rows], wb)
                    pltpu.sync_copy(m_hbm.at[rows], mb)
                    pltpu.sync_copy(v_hbm.at[rows], vb)

                    @pl.when(in_own)
                    def _():
                        pltpu.sync_copy(own_hbm.at[half_rows], gb)

                    @pl.when(jnp.logical_not(in_own))
                    def _():
                        pltpu.sync_copy(other_hbm.at[half_rows], gb)

                    pltpu.sync_copy(gb, g_out.at[rows])

                    @pl.loop(0, SC_ROWS)
                    def _(row):
                        @pl.loop(0, cdim, step=SC_LANES)
                        def _(col):
                            at = (row, pl.ds(col, SC_LANES))
                            wb[at], mb[at], vb[at] = _adamw_math(wb[at], gb[at], mb[at], vb[at])

                    pltpu.sync_copy(wb, d_out.at[rows])
                    pltpu.sync_copy(mb, m_out.at[rows])
                    pltpu.sync_copy(vb, v_out.at[rows])

    out = pl.kernel(
        body, name=name, out_type=[jax.ShapeDtypeStruct((r, cdim), F32)] * (4 * n),
        mesh=plsc.VectorSubcoreMesh(core_axis_name="sc_core", subcore_axis_name="sc_tile"),
        scratch_types=[pltpu.VMEM((SC_ROWS, cdim), F32)] * 4)(*ws, *owns, *others, *ms, *vs)
    return [out[a::n] for a in range(n)]


def small_allreduce_adamw(g, w, m, v, deps=()):
    r, cdim = g.shape

    def body(g_ref, w_ref, m_ref, v_ref, go_ref, d_ref, mo_ref, vo_ref, pair, quad, d2d_send, d2d_recv, ici_send, ici_recv):
        x, y, c, chips = _place()
        me = 2 * x + y
        pair[c] = g_ref[...]
        swap = pltpu.make_async_remote_copy(
            src_ref=g_ref, dst_ref=pair.at[c], send_sem=d2d_send, recv_sem=d2d_recv, device_id=(x, y, 1 - c),
            device_id_type=MESH_IDS)
        swap.start()
        swap.wait()
        quad[me] = pair[0] + pair[1]
        copies = []
        for j, (cx, cy) in enumerate(chips):
            cp = pltpu.make_async_remote_copy(
                src_ref=quad.at[me], dst_ref=quad.at[me], send_sem=ici_send.at[j], recv_sem=ici_recv.at[j],
                device_id=(cx, cy, c), device_id_type=MESH_IDS)
            cp.start()
            copies.append(cp)
        for j, (cx, cy) in enumerate(chips):
            slot = quad.at[2 * cx + cy]
            pltpu.make_async_remote_copy(
                src_ref=slot, dst_ref=slot, send_sem=ici_send.at[j], recv_sem=ici_recv.at[j], device_id=(cx, cy, c),
                device_id_type=MESH_IDS).wait_recv()
        for cp in copies:
            cp.wait_send()
        total = (quad[0] + quad[1]) + (quad[2] + quad[3])
        go_ref[...] = total
        d_ref[...], mo_ref[...], vo_ref[...] = _adamw_math(w_ref[...], total, m_ref[...], v_ref[...])

    return pl.pallas_call(
        _ordered_after(body, 4, deps), name="small_allreduce_adamw", in_specs=[VMEM_SPEC] * 4 + [ANY_SPEC] * len(deps),
        out_specs=[VMEM_SPEC] * 4,
        out_shape=[jax.ShapeDtypeStruct((r, cdim), F32)] * 4,
        scratch_shapes=[pltpu.VMEM((2, r, cdim), F32), pltpu.VMEM((N_CHIPS, r, cdim), F32), pltpu.SemaphoreType.DMA,
                        pltpu.SemaphoreType.DMA, pltpu.SemaphoreType.DMA((3,)), pltpu.SemaphoreType.DMA((3,))],
        compiler_params=pltpu.CompilerParams(has_side_effects=True, vmem_limit_bytes=VMEM_LIMIT_V7X))(g, w, m, v, *deps)


GROUPS = (
    ("ffn1", ("ffn1_w_gate", "ffn1_w_up", "ffn1_w_down")),
    ("mixer", ("w_in", "pool_w", "w_out_a", "w_out_b", "w_o")),
    ("ffn2", ("ffn2_w_gate", "ffn2_w_up", "ffn2_w_down")),
    ("ple", ("ple_w_gate", "ple_w_proj")),
)
GATHERS = (
    ("ffn1_in", ("ffn1_w_gate", "ffn1_w_up")),
    ("ffn1_out", ("ffn1_w_down",)),
) + GROUPS[1:]
GAINS = ("ffn1_pre_g", "ffn1_post_g", "mix_pre_g", "sgu_norm_g", "pool_scale", "mix_post_g",
         "ffn2_pre_g", "ffn2_post_g", "ple_pre_g", "ple_post_g")
SMALL = GAINS + ("sgu_b", "sgu_w")
PACKED = SMALL + ("loss",)
WEIGHTS = ("ffn1_pre_g", "ffn1_w_gate", "ffn1_w_up", "ffn1_w_down", "ffn1_post_g", "mix_pre_g", "w_in", "sgu_norm_g",
           "sgu_w", "sgu_b", "pool_w", "pool_scale", "w_out_a", "w_out_b", "w_o", "mix_post_g", "ffn2_pre_g",
           "ffn2_w_gate", "ffn2_w_up", "ffn2_w_down", "ffn2_post_g", "ple_pre_g", "ple_w_gate", "ple_w_proj", "ple_post_g")
PACK_ROWS = 16


TRANSPOSED = ("ffn1_w_gate", "ffn1_w_up", "ffn2_w_gate", "ffn2_w_up")


def _shard2d(name, a):
    a = a[0]
    return a.T if name in TRANSPOSED else a.reshape(-1, a.shape[-1])


def _unshard2d(name, a2d, shape):
    return (a2d.T if name in TRANSPOSED else a2d).reshape(shape)


def pack_rows(gains, sgu_b, loss_tile, name):
    n = len(gains)
    d = gains[0].shape[1]
    g = sgu_b.shape[0]

    def body(*refs):
        o_ref = refs[-1]
        o_ref[...] = jnp.zeros_like(o_ref)
        for i in range(n):
            o_ref[i:i + 1, :] = refs[i][...]
        o_ref[n:n + g, 0:CHUNK] = refs[n][...]
        o_ref[n + g:n + g + 1, 0:CHUNK] = refs[n + 1][0:1, :]

    return pl.pallas_call(
        body, name=name, in_specs=[VMEM_SPEC] * (n + 2), out_specs=VMEM_SPEC,
        out_shape=jax.ShapeDtypeStruct((PACK_ROWS, d), F32))(*gains, sgu_b, loss_tile)


def _pack_small(parts, tag):
    d = parts[GAINS[0]].shape[-1]
    rows = pack_rows([parts[n] for n in GAINS], parts["sgu_b"].reshape(-1, CHUNK), parts["loss"], "pack_" + tag)
    return jnp.concatenate([rows, parts["sgu_w"].reshape(-1, d)], axis=0)


def _unpack_small(packed, like):
    n, g = len(GAINS), like["sgu_b"].size // CHUNK
    out = {name: packed[i:i + 1] for i, name in enumerate(GAINS)}
    out["sgu_b"] = packed[n:n + g, :CHUNK].reshape(like["sgu_b"].shape)
    out["loss"] = packed[n + g, 0]
    out["sgu_w"] = packed[PACK_ROWS:].reshape(like["sgu_w"].shape)
    return out


def _ffn_fwd(xn, h, w, pre, g_post, g_next, tag, between=None, first=None):
    a4, s4, t4 = ffn_gu(xn, w[pre + "w_gate"], w[pre + "w_up"], tag + "_gu") if first is None else first(xn)
    deps = ()
    if between is not None:
        more, deps = between(a4)
        w.update(more)
    f, h_new, xn_next = mm_norm_res(a4, w[pre + "w_down"], h, g_post, g_next, 0.5, tag + "_down", deps=deps)
    return dict(xn=xn, h=h, a4=a4, s4=s4, t4=t4, f=f), h_new, xn_next


def _ffn_bwd_w(dh, saved, w, pre, g_post, tag, deps):
    df, dg4, du4, d_post = ffn_bwd_a(dh, saved["f"], g_post, w[pre + "w_down"], saved["s4"], saved["t4"], 0.5, tag + "_bwd_a",
                                     deps=deps)
    return dg4, du4, _ffn_dw(saved, df, dg4, du4, pre, tag), d_post


def _ffn_dw(saved, df, dg4, du4, pre, tag):
    nk = N_CHIPS
    return {
        pre + "w_down": dw_tn(saved["a4"], df, nk, tag + "_dw_down", x_kmajor=True, dy_mode="same"),
        pre + "w_gate": dw_tn(dg4, saved["xn"], nk, tag + "_dw_gate", x_kmajor=True, dy_mode="same"),
        pre + "w_up": dw_tn(du4, saved["xn"], nk, tag + "_dw_up", x_kmajor=True, dy_mode="same"),
    }


def _ffn_bwd_x(dh, dg4, du4, saved, w, pre, g_pre, tag, deps):
    return dx_norm_bwd([(dg4, w[pre + "w_gate"], False), (du4, w[pre + "w_up"], False)], saved["h"], g_pre, dh,
                       tag + "_bwd_x", deps=deps)


def _gather_group(names, weights, chip, tag, deps, plain=False):
    shards = [_shard2d(n, weights[n]) for n in names]
    bufs = [None] * len(names)
    own = None
    for b, idx in enumerate(_by_shape(shards)):
        made = cast_place([shards[i] for i in idx], chip, f"cast_{tag}_{b}", deps=deps, plain=plain)
        if plain:
            made, own = made
        for i, buf in zip(idx, made):
            bufs[i] = buf
    sends, recvs, bufs, token = allgather_start(bufs, "allgather_start_" + tag)
    return ((sends, recvs, bufs), token, dict(zip(names, own))) if plain else ((sends, recvs, bufs), token)


def _gathered(started, names, after, tag):
    sends, recvs, bufs = started
    landed = allgather_wait(sends, recvs, bufs, after, "allgather_wait_" + tag)
    return dict(zip(names, d2d_forward(landed, "d2d_forward_" + tag)))


def _forward_early(started, after, tag):
    sends, recvs, bufs = started
    landed = allgather_wait(sends, recvs, bufs, after, "allgather_wait_" + tag)
    return d2d_forward_start(landed, "d2d_forward_start_" + tag)


def _forwarded(forwarding, names, after, tag):
    return dict(zip(names, d2d_forward_wait(forwarding, after, "d2d_forward_wait_" + tag)))


def _exchange_start(names, big_g, tag):
    return sibling_start([big_g[n] for n in names], True, "exchange_start_" + tag)


def _scatter_begin(names, exchanging, after, core, tag):
    partial, from_sibling = sibling_wait(exchanging, True, after, "exchange_wait_" + tag)
    chip_sums = [None] * len(names)
    for b, idx in enumerate(_by_shape(partial)):
        for i, s in zip(idx, add_halves([partial[i] for i in idx], [from_sibling[i] for i in idx], core, f"add_halves_{tag}_{b}")):
            chip_sums[i] = s
    sends, recvs, sums, lands, token = scatter_start(chip_sums, "scatter_start_" + tag)
    return (names, sends, recvs, sums, lands), token


def _share_begin(scattering, after, chip, tag):
    names, sends, recvs, sums, lands = scattering
    sums, lands = scatter_wait(sends, recvs, sums, lands, after, "scatter_wait_" + tag)
    reduced = [None] * len(names)
    for b, idx in enumerate(_by_shape(sums)):
        for i, r in zip(idx, add_chips([sums[i] for i in idx], [lands[i] for i in idx], chip, f"add_chips_{tag}_{b}")):
            reduced[i] = r
    sharing, token = sibling_start(reduced, False, "share_start_" + tag)
    return (names, sharing), token


def _update(shared, after, weights, moments_m, moments_v, core, tag, results, held):
    names, sharing = shared
    reduced, others = sibling_wait(sharing, False, after, "share_wait_" + tag)
    buckets = _by_shape(reduced)

    def update_bucket(b, idx, deps):
        in_bucket = [names[i] for i in idx]
        operands = ([_shard2d(n, weights[n]) for n in in_bucket], [reduced[i] for i in idx], [others[i] for i in idx],
                    [_shard2d(n, moments_m[n]) for n in in_bucket], [_shard2d(n, moments_v[n]) for n in in_bucket])
        if held is None:
            outs = adamw_halves_sparsecore(*operands, f"adamw_{tag}_{b}")
        else:
            outs = adamw_halves(*operands, core, f"adamw_{tag}_{b}", deps=deps)
        for n, per_weight in zip(in_bucket, outs):
            for store, value in zip(results, per_weight):
                store[n] = _unshard2d(n, value, weights[n].shape)
        return outs[-1][1]

    if held is None:
        for b, idx in enumerate(buckets):
            update_bucket(b, idx, ())
        return None
    for b, idx in enumerate(buckets[:-1]):
        held.append(functools.partial(update_bucket, b, idx))
    return update_bucket(len(buckets) - 1, buckets[-1], ())


def kernel(x, p, ffn1_pre_g, ffn1_w_gate, ffn1_w_up, ffn1_w_down, ffn1_post_g, mix_pre_g, w_in, sgu_norm_g, sgu_w, sgu_b, pool_w, pool_scale, w_out_a, w_out_b, w_o, mix_post_g, ffn2_pre_g, ffn2_w_gate, ffn2_w_up, ffn2_w_down, ffn2_post_g, ple_pre_g, ple_w_gate, ple_w_proj, ple_post_g, loss_target, m_ffn1_pre_g, m_ffn1_w_gate, m_ffn1_w_up, m_ffn1_w_down, m_ffn1_post_g, m_mix_pre_g, m_w_in, m_sgu_norm_g, m_sgu_w, m_sgu_b, m_pool_w, m_pool_scale, m_w_out_a, m_w_out_b, m_w_o, m_mix_post_g, m_ffn2_pre_g, m_ffn2_w_gate, m_ffn2_w_up, m_ffn2_w_down, m_ffn2_post_g, m_ple_pre_g, m_ple_w_gate, m_ple_w_proj, m_ple_post_g, v_ffn1_pre_g, v_ffn1_w_gate, v_ffn1_w_up, v_ffn1_w_down, v_ffn1_post_g, v_mix_pre_g, v_w_in, v_sgu_norm_g, v_sgu_w, v_sgu_b, v_pool_w, v_pool_scale, v_w_out_a, v_w_out_b, v_w_o, v_mix_post_g, v_ffn2_pre_g, v_ffn2_w_gate, v_ffn2_w_up, v_ffn2_w_down, v_ffn2_post_g, v_ple_pre_g, v_ple_w_gate, v_ple_w_proj, v_ple_post_g):
    given = dict(locals())
    weights = {n: given[n] for n in WEIGHTS}
    moments_m = {n: given["m_" + n] for n in WEIGHTS}
    moments_v = {n: given["v_" + n] for n in WEIGHTS}
    core = lax.axis_index("c").astype(jnp.int32).reshape(1)
    chip = (2 * lax.axis_index("x") + lax.axis_index("y")).astype(jnp.int32).reshape(1)

    d = x.shape[-1]
    token = ()
    gathering = {}
    for tag, names in GATHERS:
        if tag == GATHERS[0][0]:
            gathering[tag], tok, own_first = _gather_group(names, weights, chip, tag, token, plain=True)
        else:
            gathering[tag], tok = _gather_group(names, weights, chip, tag, token)
        token = (tok,)
    gain = {n: weights[n] for n in GAINS}
    sgu_w3 = sgu_w[0]
    sgu_b3 = sgu_b[0][:, :, None]
    groups = dict(GROUPS + GATHERS)

    h0 = x[0]
    tgt = loss_target[0]
    p_bf = p[0, 0].astype(BF16)
    xn1 = rms_cast(h0, gain["ffn1_pre_g"], "ffn1_pre_norm")
    w = {}

    def ffn1_first(xn):
        own = ffn_gu(xn, own_first["ffn1_w_gate"][None], own_first["ffn1_w_up"][None], "ffn1_gu_own", chip=chip)
        w.update(_gathered(gathering["ffn1_in"], groups["ffn1_in"], own[0], "ffn1_in"))
        return ffn_gu(xn, w["ffn1_w_gate"], w["ffn1_w_up"], "ffn1_gu", chip=chip, into=own)

    s1, h1, xn2 = _ffn_fwd(xn1, h0, w, "ffn1_", gain["ffn1_post_g"], gain["mix_pre_g"], "ffn1",
                           lambda a4: (_gathered(gathering["ffn1_out"], groups["ffn1_out"], a4, "ffn1_out"), ()), ffn1_first)
    w.update(_gathered(gathering["mixer"], groups["mixer"], h1, "mixer"))
    full = {n: w[n].reshape(-1, d) for n in ("w_out_a", "w_out_b", "w_o")}
    n_groups = pool_w.shape[1]
    rows_per = pool_w.shape[2]
    dgp = pool_w.shape[3]
    pool_full = w["pool_w"].reshape(N_CHIPS, n_groups, rows_per, dgp).transpose(1, 0, 2, 3).reshape(n_groups, N_CHIPS * rows_per, dgp)
    z = mixer_in(xn2, w["w_in"], "mixer_in")
    a = sgu_fwd(z, gain["sgu_norm_g"], sgu_w3, sgu_b3, "sgu_fwd")
    diff, b = pool_fwd(z, pool_full, gain["pool_scale"], "pool_fwd")
    ya, yb, y = mixer_y(a, b, z, full["w_out_a"], full["w_out_b"], "mixer_y")
    forwarding, tok = _forward_early(gathering["ffn2"], y, "ffn2")
    m, h2, xn3 = mm_norm_res(y[None], full["w_o"][None], h1, gain["mix_post_g"], gain["ffn2_pre_g"], 1.0, "mixer_out",
                             deps=(tok,))
    w.update(_forwarded(forwarding, groups["ffn2"], h2, "ffn2"))
    early = {}

    def forward_ple(a4):
        early["ple"], tok_ple = _forward_early(gathering["ple"], a4, "ple")
        return {}, (tok_ple,)

    s2, h3, xn4 = _ffn_fwd(xn3, h2, w, "ffn2_", gain["ffn2_post_g"], gain["ple_pre_g"], "ffn2", forward_ple)
    w.update(_forwarded(early["ple"], groups["ple"], h3, "ple"))
    full["ple_w_gate"] = w["ple_w_gate"].reshape(-1, d)
    proj_full = w["ple_w_proj"].transpose(1, 0, 2).reshape(ple_w_proj.shape[1], -1)

    small_g, big_g = {}, {}
    ds, de, dh3, small_g["ple_post_g"], small_g["ple_pre_g"], loss_part = ple_loss(
        xn4, p_bf, full["ple_w_gate"], proj_full, h3, gain["ple_post_g"], gain["ple_pre_g"], tgt, "ple_loss")
    small_g["loss"] = loss_part
    grad, delta, new_m, new_v = {}, {}, {}, {}
    results = (grad, delta, new_m, new_v)
    update = functools.partial(_update, weights=weights, moments_m=moments_m, moments_v=moments_v, core=core, results=results)

    def reduce_behind(tag, previous):
        exchanging, tok = _exchange_start(groups[tag], big_g, tag)
        if previous is not None:
            shared, tok = _share_begin(previous[1], tok, chip, previous[0])
        scattering, tok_scatter = _scatter_begin(groups[tag], exchanging, tok, core, tag)
        if previous is not None:
            update(shared, tok_scatter, tag=previous[0], held=None)
        return (tag, scattering), (tok_scatter,)

    big_g["ple_w_gate"] = dw_tn(xn4, ds, 1, "dw_ple_gate").reshape(N_CHIPS, -1, d)
    big_g["ple_w_proj"] = dw_tn(p_bf, de, N_CHIPS, "dw_ple_proj", dy_mode="cols")
    reducing, deps = reduce_behind("ple", None)

    dg4, du4, g2, small_g["ffn2_post_g"] = _ffn_bwd_w(dh3, s2, w, "ffn2_", gain["ffn2_post_g"], "ffn2", deps)
    big_g.update(g2)
    dh2, small_g["ffn2_pre_g"] = _ffn_bwd_x(dh3, dg4, du4, s2, w, "ffn2_", gain["ffn2_pre_g"], "ffn2", ())
    reducing, deps = reduce_behind("ffn2", reducing)

    dm, dya, dyb, dz, da, db, small_g["mix_post_g"] = mixer_bwd_y(
        dh2, m, gain["mix_post_g"], full["w_o"], ya, yb, z, full["w_out_a"], full["w_out_b"], "mixer_bwd_y", deps=deps)
    big_g["w_o"] = dw_tn(y, dm, 1, "dw_o").reshape(N_CHIPS, -1, d)
    big_g["w_out_a"] = dw_tn(a, dya, 1, "dw_out_a").reshape(N_CHIPS, -1, d)
    big_g["w_out_b"] = dw_tn(b, dyb, 1, "dw_out_b").reshape(N_CHIPS, -1, d)
    dz, d_sgu_w, d_sgu_b, small_g["sgu_norm_g"] = sgu_bwd(z, da, dz, gain["sgu_norm_g"], sgu_w3, sgu_b3, "sgu_bwd")
    dz, d_pool_w, small_g["pool_scale"] = pool_bwd(db, diff, dz, pool_full, gain["pool_scale"], "pool_bwd")
    big_g["pool_w"] = d_pool_w.astype(BF16).reshape(n_groups, N_CHIPS, rows_per, dgp).transpose(1, 0, 2, 3).reshape(
        N_CHIPS, n_groups * rows_per, dgp)
    big_g["w_in"] = dw_in_tiles(xn2, dz, N_CHIPS, "dw_in")
    dh1, small_g["mix_pre_g"] = dx_norm_bwd([(dz, w["w_in"], True)], h1, gain["mix_pre_g"], dh2, "mixer_bwd_x")
    reducing, deps = reduce_behind("mixer", reducing)

    dg4, du4, g1, small_g["ffn1_post_g"] = _ffn_bwd_w(dh1, s1, w, "ffn1_", gain["ffn1_post_g"], "ffn1", deps)
    big_g.update(g1)
    reducing, deps = reduce_behind("ffn1", reducing)
    dh0, small_g["ffn1_pre_g"] = _ffn_bwd_x(dh1, dg4, du4, s1, w, "ffn1_", gain["ffn1_pre_g"], "ffn1", deps)
    shared, tok = _share_begin(reducing[1], dh0, chip, "ffn1")
    small_g["sgu_w"] = d_sgu_w
    small_g["sgu_b"] = d_sgu_b[:, :, 0]

    no_state = {"loss": jnp.zeros((8, 128), F32)}
    packed = small_allreduce_adamw(
        _pack_small(small_g, "grads"), _pack_small({n: weights[n] for n in SMALL} | no_state, "weights"),
        _pack_small({n: moments_m[n] for n in SMALL} | no_state, "m"), _pack_small({n: moments_v[n] for n in SMALL} | no_state, "v"),
        deps=(tok,))
    like = {n: weights[n] for n in SMALL}
    for store, block in zip((grad, delta, new_m, new_v), packed):
        store.update(_unpack_small(block, like))
    update(shared, packed[0], tag="ffn1", held=[])

    return (grad["loss"], dh0[None], *[grad[n] for n in WEIGHTS], *[delta[n] for n in WEIGHTS],
            *[new_m[n] for n in WEIGHTS], *[new_v[n] for n in WEIGHTS])
```

```python
import functools

import jax
import jax.numpy as jnp
from jax import lax
from jax.experimental import pallas as pl
from jax.experimental.pallas import tpu as pltpu
from jax.experimental.pallas import tpu_sc as plsc

F32 = jnp.float32
BF16 = jnp.bfloat16
EPS = 1e-6
CHUNK = 128
POOL_WINDOWS = (2, 4, 8, 16)
HALO = 16
N_CHIPS = 4
ADAM_LR, ADAM_B1, ADAM_B2, ADAM_EPS, ADAM_WD, ADAM_STEP = 0.001, 0.9, 0.999, 1e-08, 0.01, 10
VMEM_LIMIT_V7X = 58 * 1024 * 1024
MESH_IDS = pl.DeviceIdType.MESH
HBM_SPEC = pl.BlockSpec(memory_space=pltpu.HBM)
VMEM_SPEC = pl.BlockSpec(memory_space=pltpu.VMEM)
SEM_SPEC = pl.BlockSpec(memory_space=pltpu.SEMAPHORE)
ANY_SPEC = pl.BlockSpec(memory_space=pl.ANY)
DATAFLOW = pltpu.SideEffectType.DATAFLOW_SIDE_EFFECTING
OTHER_CHIPS = ((1, 0), (0, 1), (1, 1))
DZ_SLOT = (2, 3, 4, 0, 1)
SC_TILES, SC_LANES, SC_ROWS = 32, 16, 8
DW_TOKENS = 4096
DW_IN_TILE = 256

NT = (((1,), (1,)), ((), ()))
TN = (((0,), (0,)), ((), ()))


def _params(*sem, **more):
    return pltpu.CompilerParams(dimension_semantics=sem or None, vmem_limit_bytes=VMEM_LIMIT_V7X, **more)


def _tile(t, want):
    return max(c for c in range(8, min(t, want) + 1, 8) if t % c == 0)


def _const(shape):
    return pl.BlockSpec(shape, lambda *_: (0,) * len(shape))


def _rows(tm, d, col=0):
    return pl.BlockSpec((tm, d), lambda i: (i, col))


def _kmajor(nk, tm, kb):
    return pl.BlockSpec((nk, tm, kb), lambda i: (0, i, 0))


def _dot(a, b):
    return jnp.dot(a, b, preferred_element_type=F32)


def _dot_nt(a, b):
    return lax.dot_general(a, b, NT, preferred_element_type=F32)


def _dot_tn(a, b):
    return lax.dot_general(a, b, TN, preferred_element_type=F32)


def _gelu(x):
    return 0.5 * x * (1.0 + jnp.tanh(0.7978845608028654 * (x + 0.044715 * x * x * x)))


def _gelu_and_grad(x):
    k, kc = 0.7978845608028654, 0.7978845608028654 * 0.044715
    x2 = x * x
    th = jnp.tanh(x * (k + kc * x2))
    cdf = 0.5 + 0.5 * th
    return x * cdf, cdf + x * (0.5 - 0.5 * th * th) * (k + 3.0 * kc * x2)


def _sigmoid(x):
    return 1.0 / (1.0 + jnp.exp(-x))


def _rstd(h):
    return lax.rsqrt(jnp.mean(h * h, axis=-1, keepdims=True) + EPS)


def _rms_bwd(h, g, dy):
    r = _rstd(h)
    t = dy * g
    dh = r * t - h * (r * r * r) * jnp.mean(h * t, axis=-1, keepdims=True)
    return dh, jnp.sum(dy * h * r, axis=0, keepdims=True)


def _ordered_after(body, n_in, deps):
    if not deps:
        return body
    return lambda *refs: body(*refs[:n_in], *refs[n_in + len(deps):])


def _accumulate(ref, value):
    @pl.when(pl.program_id(0) == 0)
    def _():
        ref[...] = jnp.zeros_like(ref)

    ref[...] += value


def rms_cast(h, g, name):
    t, d = h.shape
    tm = _tile(t, 512)

    def body(h_ref, g_ref, o_ref):
        hv = h_ref[...]
        o_ref[...] = (hv * _rstd(hv) * g_ref[...]).astype(BF16)

    return pl.pallas_call(
        body, name=name, grid=(t // tm,), in_specs=[_rows(tm, d), _const((1, d))], out_specs=_rows(tm, d),
        out_shape=jax.ShapeDtypeStruct((t, d), BF16), compiler_params=_params("parallel"))(h, g)


def ffn_gu(xn, wgt, wut, name, chip=None, into=None):
    t, d = xn.shape
    nkw, fk, _ = wgt.shape
    tm = _tile(t, 512)
    count = N_CHIPS if chip is None else (1 if into is None else N_CHIPS - 1)
    first = 0 if into is None else 1

    def body(*refs):
        x_ref, wg_ref, wu_ref = refs[-6 - (0 if into is None else 3):][:3]
        a_ref, s_ref, t_ref = refs[-3:]
        xv = x_ref[...]
        g = _dot_nt(xv, wg_ref[0])
        sg = _sigmoid(g)
        s = g * sg
        s_ref[0] = s.astype(BF16)
        ds = sg * (1.0 + g * (1.0 - sg))
        u = _dot_nt(xv, wu_ref[0])
        a_ref[0] = (s * u).astype(BF16)
        t_ref[0] = (u * ds).astype(BF16)

    def slot(k, *chip_ref):
        return (chip_ref[0][0] + first + k) % N_CHIPS if chip_ref else k

    w_spec = pl.BlockSpec((1, fk, d), lambda k, i, *c: (slot(k, *c) if nkw > 1 else 0, 0, 0))
    o_spec = pl.BlockSpec((1, tm, fk), lambda k, i, *c: (slot(k, *c), i, 0))
    shape = jax.ShapeDtypeStruct((N_CHIPS, t, fk), BF16)
    prior = () if into is None else tuple(into)
    call = pl.pallas_call(
        body, name=name,
        grid_spec=pltpu.PrefetchScalarGridSpec(
            num_scalar_prefetch=0 if chip is None else 1, grid=(count, t // tm),
            in_specs=[pl.BlockSpec((tm, d), lambda k, i, *c: (i, 0)), w_spec, w_spec] + [ANY_SPEC] * len(prior),
            out_specs=[o_spec] * 3),
        out_shape=[shape] * 3, input_output_aliases={4 + j: j for j in range(len(prior))},
        compiler_params=_params("parallel", "parallel"))
    return call(*(() if chip is None else (chip,)), xn, wgt, wut, *prior)


def mm_norm_res(a3, w3, h_old, g_post, g_next, scale, name, deps=()):
    nk, t, kb = a3.shape
    d = w3.shape[2]
    tm = _tile(t, 256)

    def body(a_ref, w_ref, h_ref, gp_ref, gn_ref, f_ref, hn_ref, xn_ref):
        f = _dot(a_ref[0], w_ref[0])
        for k in range(1, nk):
            f += _dot(a_ref[k], w_ref[k])
        f_ref[...] = f.astype(BF16)
        hn = h_ref[...] + scale * (f * _rstd(f) * gp_ref[...])
        hn_ref[...] = hn
        xn_ref[...] = (hn * _rstd(hn) * gn_ref[...]).astype(BF16)

    return pl.pallas_call(
        _ordered_after(body, 5, deps), name=name, grid=(t // tm,),
        in_specs=[_kmajor(nk, tm, kb), _const((nk, kb, d)), _rows(tm, d), _const((1, d)), _const((1, d))] + [ANY_SPEC] * len(deps),
        out_specs=[_rows(tm, d)] * 3,
        out_shape=[jax.ShapeDtypeStruct((t, d), BF16), jax.ShapeDtypeStruct((t, d), F32), jax.ShapeDtypeStruct((t, d), BF16)],
        compiler_params=_params("parallel"))(a3, w3, h_old, g_post, g_next, *deps)


def mixer_in(xn, win4, name):
    t, d = xn.shape
    nk, _, nb = win4.shape
    tm = _tile(t, 512)

    def body(x_ref, w_ref, z_ref):
        z_ref[...] = _dot(x_ref[...], w_ref[0]).astype(BF16)

    return pl.pallas_call(
        body, name=name, grid=(nk, t // tm),
        in_specs=[pl.BlockSpec((tm, d), lambda k, i: (i, 0)), pl.BlockSpec((1, d, nb), lambda k, i: (k, 0, 0))],
        out_specs=pl.BlockSpec((tm, nb), lambda k, i: (i, k)), out_shape=jax.ShapeDtypeStruct((t, nk * nb), BF16),
        compiler_params=_params("parallel", "parallel"))(xn, win4)


def _causal_mask():
    row = lax.broadcasted_iota(jnp.int32, (CHUNK, CHUNK), 0)
    col = lax.broadcasted_iota(jnp.int32, (CHUNK, CHUNK), 1)
    return row >= col


def _layernorm_parts(v):
    mu = jnp.mean(v, axis=-1, keepdims=True)
    vc = v - mu
    r = lax.rsqrt(jnp.mean(vc * vc, axis=-1, keepdims=True) + EPS)
    return vc * r, r


def sgu_fwd(z, norm_g, sgu_w, sgu_b3, name):
    t = z.shape[0]
    d = norm_g.shape[1]
    ng = sgu_w.shape[0]
    dg = d // ng
    tm = _tile(t, 256)

    def body(zu_ref, zv_ref, ng_ref, w_ref, b_ref, a_ref):
        vhat, _ = _layernorm_parts(_gelu(zv_ref[...].astype(F32)))
        vn = (vhat * ng_ref[...]).astype(BF16)
        u = _gelu(zu_ref[...].astype(F32))
        mask = _causal_mask()
        for g in range(ng):
            wg = jnp.where(mask, w_ref[g], 0.0).astype(BF16)
            for ci in range(tm // CHUNK):
                rs, cs = slice(ci * CHUNK, (ci + 1) * CHUNK), slice(g * dg, (g + 1) * dg)
                sv = _dot(wg, vn[rs, cs]) + b_ref[g]
                a_ref[rs, cs] = (u[rs, cs] * sv).astype(BF16)

    return pl.pallas_call(
        body, name=name, grid=(t // tm,),
        in_specs=[_rows(tm, d, 0), _rows(tm, d, 1), _const((1, d)), _const((ng, CHUNK, CHUNK)), _const((ng, CHUNK, 1))],
        out_specs=_rows(tm, d), out_shape=jax.ShapeDtypeStruct((t, d), BF16),
        compiler_params=_params("parallel"))(z, z, norm_g, sgu_w, sgu_b3)


def pool_fwd(z, pool_w, pool_scale, name):
    t = z.shape[0]
    d = pool_scale.shape[1]
    ng = pool_w.shape[0]
    dg = d // ng
    tm = _tile(t, 256)
    per = tm // HALO

    def body(c_ref, prev_ref, w_ref, s_ref, diff_ref, b_ref):
        i = pl.program_id(0)
        cur = c_ref[...].astype(F32)
        prev = jnp.where(i > 0, prev_ref[...].astype(F32), 0.0)
        ext = jnp.concatenate([prev, cur], axis=0)
        tok = i * tm + lax.broadcasted_iota(jnp.int32, (tm, 1), 0)
        for g, win in enumerate(POOL_WINDOWS):
            cs = slice(g * dg, (g + 1) * dg)
            s = ext[:, cs]
            sh = 1
            while sh < win:
                s = s + pltpu.roll(s, sh, 0)
                sh *= 2
            per_count = 1.0 / jnp.minimum(tok + 1, win).astype(F32)
            diff = (s[HALO:] * per_count - cur[:, cs]).astype(BF16)
            diff_ref[:, cs] = diff
            b_ref[:, cs] = (_dot(diff, w_ref[g]) * s_ref[:, cs]).astype(BF16)

    return pl.pallas_call(
        body, name=name, grid=(t // tm,),
        in_specs=[_rows(tm, d, 2), pl.BlockSpec((HALO, d), lambda i: (jnp.maximum(i * per - 1, 0), 2)),
                  _const((ng, dg, dg)), _const((1, d))],
        out_specs=[_rows(tm, d)] * 2, out_shape=[jax.ShapeDtypeStruct((t, d), BF16)] * 2,
        compiler_params=_params("parallel"))(z, z, pool_w, pool_scale)


def mixer_y(a, b, z, woa, wob, name):
    t, d = a.shape
    tm = _tile(t, 256)

    def body(a_ref, b_ref, ga_ref, gb_ref, wa_ref, wb_ref, ya_ref, yb_ref, y_ref):
        ya = _dot(a_ref[...], wa_ref[...])
        yb = _dot(b_ref[...], wb_ref[...])
        ya_ref[...] = ya.astype(BF16)
        yb_ref[...] = yb.astype(BF16)
        y_ref[...] = (_sigmoid(ga_ref[...].astype(F32)) * ya + _sigmoid(gb_ref[...].astype(F32)) * yb).astype(BF16)

    return pl.pallas_call(
        body, name=name, grid=(t // tm,),
        in_specs=[_rows(tm, d), _rows(tm, d), _rows(tm, d, 3), _rows(tm, d, 4), _const((d, d)), _const((d, d))],
        out_specs=[_rows(tm, d)] * 3, out_shape=[jax.ShapeDtypeStruct((t, d), BF16)] * 3,
        compiler_params=_params("parallel"))(a, b, z, z, woa, wob)


def ple_loss(xn, p, wpg, wpp, h, g_post, g_pre, target, name):
    t, d = xn.shape
    dp = p.shape[1]
    tm = _tile(t, 256)

    def body(x_ref, p_ref, wg_ref, wp_ref, h_ref, gp_ref, gn_ref, tg_ref, ds_ref, de_ref, dhp_ref, dgp_ref, dgn_ref, loss_ref):
        gate = _sigmoid(_dot(x_ref[...], wg_ref[...]))
        e = _dot(p_ref[...], wp_ref[...])
        q = gate * e
        hv = h_ref[...]
        err = hv + q * _rstd(q) * gp_ref[...] - tg_ref[...]
        _accumulate(loss_ref, jnp.full(loss_ref.shape, (0.5 / d) * jnp.sum(err * err), F32))
        dhv = err * (1.0 / d)
        dq, dgp = _rms_bwd(q, gp_ref[...], dhv)
        ds = (dq * e * gate * (1.0 - gate)).astype(BF16)
        ds_ref[...] = ds
        de_ref[...] = (dq * gate).astype(BF16)
        dx, dgn = _rms_bwd(hv, gn_ref[...], _dot_nt(ds, wg_ref[...]))
        dhp_ref[...] = dhv + dx
        _accumulate(dgp_ref, dgp)
        _accumulate(dgn_ref, dgn)

    return pl.pallas_call(
        body, name=name, grid=(t // tm,),
        in_specs=[_rows(tm, d), _rows(tm, dp), _const((d, d)), _const((dp, d)), _rows(tm, d), _const((1, d)), _const((1, d)),
                  _rows(tm, d)],
        out_specs=[_rows(tm, d)] * 3 + [_const((1, d))] * 2 + [_const((8, 128))],
        out_shape=[jax.ShapeDtypeStruct((t, d), BF16), jax.ShapeDtypeStruct((t, d), BF16), jax.ShapeDtypeStruct((t, d), F32),
                   jax.ShapeDtypeStruct((1, d), F32), jax.ShapeDtypeStruct((1, d), F32), jax.ShapeDtypeStruct((8, 128), F32)],
        compiler_params=_params("arbitrary"))(xn, p, wpg, wpp, h, g_post, g_pre, target)


def ffn_bwd_a(dh, f, g_post, wd4, s4, t4, scale, name, deps=()):
    t, d = dh.shape
    nk, fk, _ = wd4.shape
    tm = _tile(t, 256)

    def body(dh_ref, f_ref, gp_ref, w_ref, s_ref, t_ref, df_ref, dg_ref, du_ref, dgp_ref):
        df, dgp = _rms_bwd(f_ref[...].astype(F32), gp_ref[...], dh_ref[...])
        df = (scale * df).astype(BF16)
        df_ref[...] = df
        _accumulate(dgp_ref, scale * dgp)
        for k in range(nk):
            da = _dot_nt(df, w_ref[k])
            du_ref[k] = (da * s_ref[k].astype(F32)).astype(BF16)
            dg_ref[k] = (da * t_ref[k].astype(F32)).astype(BF16)

    return pl.pallas_call(
        _ordered_after(body, 6, deps), name=name, grid=(t // tm,),
        in_specs=[_rows(tm, d), _rows(tm, d), _const((1, d)), _const((nk, fk, d)), _kmajor(nk, tm, fk), _kmajor(nk, tm, fk)]
        + [ANY_SPEC] * len(deps),
        out_specs=[_rows(tm, d), _kmajor(nk, tm, fk), _kmajor(nk, tm, fk), _const((1, d))],
        out_shape=[jax.ShapeDtypeStruct((t, d), BF16), jax.ShapeDtypeStruct((nk, t, fk), BF16),
                   jax.ShapeDtypeStruct((nk, t, fk), BF16), jax.ShapeDtypeStruct((1, d), F32)],
        compiler_params=_params("arbitrary"))(dh, f, g_post, wd4, s4, t4, *deps)


def dx_norm_bwd(pairs, h, g_pre, dh_in, name, deps=()):
    t, d = h.shape
    tm = _tile(t, 256)
    n = len(pairs)

    def body(*refs):
        dys, ws = refs[:n], refs[n:2 * n]
        h_ref, g_ref, dhi_ref, dho_ref, dg_ref = refs[2 * n:]
        acc = None
        for (_, w4, sections), dy_ref, w_ref in zip(pairs, dys, ws):
            if sections:
                wide = w4.shape[2]
                edges = sorted(set(range(0, 5 * d + 1, d)) | set(range(0, 5 * d + 1, wide)))
                parts = [_dot_nt(dy_ref[DZ_SLOT[lo // d], :, lo % d:lo % d + hi - lo], w_ref[lo // wide, :, lo % wide:lo % wide + hi - lo])
                         for lo, hi in zip(edges[:-1], edges[1:])]
            else:
                parts = [_dot(dy_ref[k], w_ref[k]) for k in range(w4.shape[0])]
            for part in parts:
                acc = part if acc is None else acc + part
        dx, dg = _rms_bwd(h_ref[...], g_ref[...], acc)
        dho_ref[...] = dhi_ref[...] + dx
        _accumulate(dg_ref, dg)

    dy_specs = [_kmajor(dy.shape[0], tm, dy.shape[2]) for dy, _, _ in pairs]
    return pl.pallas_call(
        _ordered_after(body, 2 * n + 3, deps), name=name, grid=(t // tm,),
        in_specs=dy_specs + [_const(w4.shape) for _, w4, _ in pairs] + [_rows(tm, d), _const((1, d)), _rows(tm, d)]
        + [ANY_SPEC] * len(deps),
        out_specs=[_rows(tm, d), _const((1, d))],
        out_shape=[jax.ShapeDtypeStruct((t, d), F32), jax.ShapeDtypeStruct((1, d), F32)],
        compiler_params=_params("arbitrary"))(*[dy for dy, _, _ in pairs], *[w4 for _, w4, _ in pairs], h, g_pre, dh_in, *deps)


def mixer_bwd_y(dh, m, g_post, w_o, ya, yb, z, woa, wob, name, deps=()):
    t, d = dh.shape
    tm = _tile(t, 256)

    def body(dh_ref, m_ref, gp_ref, wo_ref, ya_ref, yb_ref, ga_ref, gb_ref, wa_ref, wb_ref,
             dm_ref, dya_ref, dyb_ref, dz_ref, da_ref, db_ref, dgp_ref):
        dm, dgp = _rms_bwd(m_ref[...].astype(F32), gp_ref[...], dh_ref[...])
        dm = dm.astype(BF16)
        dm_ref[...] = dm
        _accumulate(dgp_ref, dgp)
        dy = _dot_nt(dm, wo_ref[...])
        sa = _sigmoid(ga_ref[...].astype(F32))
        sb = _sigmoid(gb_ref[...].astype(F32))
        dya = (dy * sa).astype(BF16)
        dyb = (dy * sb).astype(BF16)
        dya_ref[...] = dya
        dyb_ref[...] = dyb
        dz_ref[0] = (dy * ya_ref[...].astype(F32) * sa * (1.0 - sa)).astype(BF16)
        dz_ref[1] = (dy * yb_ref[...].astype(F32) * sb * (1.0 - sb)).astype(BF16)
        da_ref[...] = _dot_nt(dya, wa_ref[...]).astype(BF16)
        db_ref[...] = _dot_nt(dyb, wb_ref[...]).astype(BF16)

    return pl.pallas_call(
        _ordered_after(body, 10, deps), name=name, grid=(t // tm,),
        in_specs=[_rows(tm, d), _rows(tm, d), _const((1, d)), _const((d, d)), _rows(tm, d), _rows(tm, d),
                  _rows(tm, d, 3), _rows(tm, d, 4), _const((d, d)), _const((d, d))] + [ANY_SPEC] * len(deps),
        out_specs=[_rows(tm, d)] * 3 + [pl.BlockSpec((2, tm, d), lambda i: (0, i, 0))] + [_rows(tm, d)] * 2 + [_const((1, d))],
        out_shape=[jax.ShapeDtypeStruct((t, d), BF16)] * 3 + [jax.ShapeDtypeStruct((5, t, d), BF16)]
        + [jax.ShapeDtypeStruct((t, d), BF16)] * 2 + [jax.ShapeDtypeStruct((1, d), F32)],
        compiler_params=_params("arbitrary"))(dh, m, g_post, w_o, ya, yb, z, z, woa, wob, *deps)


def sgu_bwd(z, da, dz, norm_g, sgu_w, sgu_b3, name):
    t, d = da.shape
    ng = sgu_w.shape[0]
    dg = d // ng
    tm = _tile(t, 256)
    steps = t // tm

    def body(zu_ref, zv_ref, da_ref, ng_ref, w_ref, b_ref, _, dz_ref, dw_ref, db_ref, dng_ref, dvn_ref, dsv_ref):
        i = pl.program_id(0)
        zv = zv_ref[...].astype(F32)
        zu = zu_ref[...].astype(F32)
        v, gv = _gelu_and_grad(zv)
        vhat, r = _layernorm_parts(v)
        gain = ng_ref[...]
        vn = (vhat * gain).astype(BF16)
        u, gu = _gelu_and_grad(zu)
        dav = da_ref[...].astype(F32)
        mask = _causal_mask()

        @pl.when(i == 0)
        def _():
            dw_ref[...] = jnp.zeros_like(dw_ref)
            dsv_ref[...] = jnp.zeros_like(dsv_ref)

        for g in range(ng):
            wg = jnp.where(mask, w_ref[g], 0.0).astype(BF16)
            dw = jnp.zeros((CHUNK, CHUNK), F32)
            dsv_sum = jnp.zeros((CHUNK, dg), F32)
            for ci in range(tm // CHUNK):
                rs, cs = slice(ci * CHUNK, (ci + 1) * CHUNK), slice(g * dg, (g + 1) * dg)
                vn_blk = vn[rs, cs]
                sv = _dot(wg, vn_blk) + b_ref[g]
                dz_ref[0, rs, cs] = (dav[rs, cs] * sv * gu[rs, cs]).astype(BF16)
                dsv = dav[rs, cs] * u[rs, cs]
                dsv_sum += dsv
                dsv = dsv.astype(BF16)
                dw += _dot_nt(dsv, vn_blk)
                dvn_ref[rs, cs] = _dot_tn(wg, dsv)
            dw_ref[g] += dw
            dsv_ref[:, cs] += dsv_sum

        dvn = dvn_ref[...]
        _accumulate(dng_ref, jnp.sum(dvn * vhat, axis=0, keepdims=True))
        dvh = dvn * gain
        dv = r * (dvh - jnp.mean(dvh, axis=-1, keepdims=True) - vhat * jnp.mean(dvh * vhat, axis=-1, keepdims=True))
        dz_ref[1] = (dv * gv).astype(BF16)

        @pl.when(i == steps - 1)
        def _():
            for g in range(ng):
                dw_ref[g] = jnp.where(mask, dw_ref[g], 0.0)
                row_sum = jnp.sum(dsv_ref[:, g * dg:(g + 1) * dg], axis=1, keepdims=True)
                db_ref[g] = jnp.broadcast_to(row_sum, (CHUNK, CHUNK))

    return pl.pallas_call(
        body, name=name, grid=(steps,),
        in_specs=[_rows(tm, d, 0), _rows(tm, d, 1), _rows(tm, d), _const((1, d)), _const((ng, CHUNK, CHUNK)), _const((ng, CHUNK, 1)),
                  ANY_SPEC],
        out_specs=[pl.BlockSpec((2, tm, d), lambda i: (DZ_SLOT[0] // 2, i, 0)), _const((ng, CHUNK, CHUNK)),
                   _const((ng, CHUNK, CHUNK)), _const((1, d))],
        out_shape=[jax.ShapeDtypeStruct(dz.shape, BF16), jax.ShapeDtypeStruct((ng, CHUNK, CHUNK), F32),
                   jax.ShapeDtypeStruct((ng, CHUNK, CHUNK), F32), jax.ShapeDtypeStruct((1, d), F32)],
        scratch_shapes=[pltpu.VMEM((tm, d), F32), pltpu.VMEM((CHUNK, d), F32)], input_output_aliases={6: 0},
        compiler_params=_params("arbitrary"))(z, z, da, norm_g, sgu_w, sgu_b3, dz)


def pool_bwd(db, diff, dz, pool_w, pool_scale, name):
    t, d = db.shape
    ng = pool_w.shape[0]
    dg = d // ng
    tm = _tile(t, 256)
    per = tm // HALO
    steps = t // tm

    def body(db_ref, next_ref, diff_ref, w_ref, s_ref, _, dc_ref, dw_ref, ds_ref):
        i = pl.program_id(0)
        dbc = db_ref[...].astype(F32)
        nxt = jnp.where(i < steps - 1, next_ref[...].astype(F32), 0.0)
        ext = jnp.concatenate([dbc, nxt], axis=0)
        rows = tm + HALO
        tok = i * tm + lax.broadcasted_iota(jnp.int32, (rows, 1), 0)

        @pl.when(i == 0)
        def _():
            dw_ref[...] = jnp.zeros_like(dw_ref)
            ds_ref[...] = jnp.zeros_like(ds_ref)

        for g, win in enumerate(POOL_WINDOWS):
            cs = slice(g * dg, (g + 1) * dg)
            dp = (ext[:, cs] * s_ref[:, cs]).astype(BF16)
            dd = _dot_nt(dp, w_ref[g])
            s = dd * (1.0 / jnp.minimum(tok + 1, win).astype(F32))
            sh = 1
            while sh < win:
                s = s + pltpu.roll(s, rows - sh, 0)
                sh *= 2
            dc_ref[0, :, cs] = (s[:tm] - dd[:tm]).astype(BF16)
            dfg = diff_ref[:, cs]
            ds_ref[:, cs] += jnp.sum(dbc[:, cs] * _dot(dfg, w_ref[g]), axis=0, keepdims=True)
            dw_ref[g] += _dot_tn(dfg, dp[:tm])

    return pl.pallas_call(
        body, name=name, grid=(steps,),
        in_specs=[_rows(tm, d), pl.BlockSpec((HALO, d), lambda i: (jnp.minimum((i + 1) * per, t // HALO - 1), 0)),
                  _rows(tm, d), _const((ng, dg, dg)), _const((1, d)), ANY_SPEC],
        out_specs=[pl.BlockSpec((1, tm, d), lambda i: (DZ_SLOT[2], i, 0)), _const((ng, dg, dg)), _const((1, d))],
        out_shape=[jax.ShapeDtypeStruct(dz.shape, BF16), jax.ShapeDtypeStruct((ng, dg, dg), F32), jax.ShapeDtypeStruct((1, d), F32)],
        input_output_aliases={5: 0},
        compiler_params=_params("arbitrary"))(db, db, diff, pool_w, pool_scale, dz)


def dw_tn(x, dy, nk, name, x_kmajor=False, dy_mode="same"):
    t = x.shape[-2]
    kx = x.shape[-1]
    n = dy.shape[-1] // nk if dy_mode == "cols" else dy.shape[-1]
    tt = _tile(t, DW_TOKENS)
    steps = t // tt

    def body(x_ref, dy_ref, o_ref, acc_ref):
        s = pl.program_id(1)
        part = _dot_tn(x_ref[0] if x_kmajor else x_ref[...], dy_ref[0] if dy_mode == "kmajor" else dy_ref[...])
        if steps == 1:
            o_ref[0] = part.astype(BF16)
            return

        @pl.when(s == 0)
        def _():
            acc_ref[...] = jnp.zeros_like(acc_ref)

        acc_ref[...] += part

        @pl.when(s == steps - 1)
        def _():
            o_ref[0] = acc_ref[...].astype(BF16)

    x_spec = pl.BlockSpec((1, tt, kx), lambda k, s: (k, s, 0)) if x_kmajor else pl.BlockSpec((tt, kx), lambda k, s: (s, 0))
    dy_spec = {"kmajor": pl.BlockSpec((1, tt, n), lambda k, s: (k, s, 0)), "cols": pl.BlockSpec((tt, n), lambda k, s: (s, k)),
               "same": pl.BlockSpec((tt, n), lambda k, s: (s, 0))}[dy_mode]
    return pl.pallas_call(
        body, name=name, grid=(nk, steps), in_specs=[x_spec, dy_spec],
        out_specs=pl.BlockSpec((1, kx, n), lambda k, s: (k, 0, 0)), out_shape=jax.ShapeDtypeStruct((nk, kx, n), BF16),
        scratch_shapes=[pltpu.VMEM((kx, n) if steps > 1 else (8, 128), F32)],
        compiler_params=_params("parallel", "arbitrary"))(x, dy)


def dw_in_tiles(xn, dz, nk, name):
    t, d = xn.shape
    sections = len(DZ_SLOT)
    per_section = d // DW_IN_TILE
    per_shard = sections * per_section // nk

    def body(x_ref, dy_ref, o_ref):
        o_ref[0] = _dot_tn(x_ref[...], dy_ref[0]).astype(BF16)

    def slot(j):
        return (j // per_section + DZ_SLOT[0]) % sections

    return pl.pallas_call(
        body, name=name, grid=(sections * per_section,),
        in_specs=[pl.BlockSpec((t, d), lambda j: (0, 0)), pl.BlockSpec((1, t, DW_IN_TILE), lambda j: (slot(j), 0, j % per_section))],
        out_specs=pl.BlockSpec((1, d, DW_IN_TILE), lambda j: (j // per_shard, 0, j % per_shard)),
        out_shape=jax.ShapeDtypeStruct((nk, d, sections * d // nk), BF16), compiler_params=_params("parallel"))(xn, dz)


def _place():
    x, y, c = lax.axis_index("x"), lax.axis_index("y"), lax.axis_index("c")
    chips = [((1 - x) if fx else x, (1 - y) if fy else y) for fx, fy in OTHER_CHIPS]
    return x, y, c, chips


def _half(rows, which):
    return pl.ds(pl.multiple_of(which * (rows // 2), 16), rows // 2)


def _hbm(a):
    return pltpu.with_memory_space_constraint(a, pltpu.HBM)


def _by_shape(arrays):
    buckets = {}
    for i, a in enumerate(arrays):
        buckets.setdefault(a.shape, []).append(i)
    return list(buckets.values())


def cast_place(shards, chip, name, deps=(), plain=False):
    n = len(shards)
    r, cdim = shards[0].shape
    tr = _tile(r, 256)

    def body(chip_ref, *refs):
        outs = refs[n + len(deps):]
        for a, w_ref in enumerate(refs[:n]):
            cast = w_ref[...].astype(BF16)
            outs[a][0] = cast
            if plain:
                outs[n + a][...] = cast

    out = pl.pallas_call(
        body, name=name,
        grid_spec=pltpu.PrefetchScalarGridSpec(
            num_scalar_prefetch=1, grid=(r // tr,),
            in_specs=[pl.BlockSpec((tr, cdim), lambda i, chip_ref: (i, 0))] * n + [ANY_SPEC] * len(deps),
            out_specs=[pl.BlockSpec((1, tr, cdim), lambda i, chip_ref: (chip_ref[0], i, 0))] * n
            + [pl.BlockSpec((tr, cdim), lambda i, chip_ref: (i, 0))] * (n * plain)),
        out_shape=[jax.ShapeDtypeStruct((N_CHIPS, r, cdim), BF16)] * n + [jax.ShapeDtypeStruct((r, cdim), BF16)] * (n * plain),
        compiler_params=_params("parallel"))(chip, *shards, *deps)
    return (out[:n], out[n:]) if plain else out


def _gather_copy(buf, sends, recvs, i, j, me, chip_xy, c):
    cx, cy = chip_xy
    mine = _half(buf.shape[1], c)
    return pltpu.make_async_remote_copy(
        src_ref=buf.at[me, mine], dst_ref=buf.at[me, mine], send_sem=sends.at[3 * i + j], recv_sem=recvs.at[3 * i + j],
        device_id=(cx, cy, c), device_id_type=MESH_IDS)


def allgather_start(bufs, name):
    n = len(bufs)

    def body(*refs):
        ins = refs[:n]
        sends, recvs = refs[n], refs[n + 1]
        token = refs[2 * n + 2]
        x, y, c, chips = _place()
        for i in range(n):
            for j, chip_xy in enumerate(chips):
                _gather_copy(ins[i], sends, recvs, i, j, 2 * x + y, chip_xy, c).start()
        token[...] = jnp.zeros_like(token)

    out = pl.pallas_call(
        body, name=name, in_specs=[HBM_SPEC] * n,
        out_specs=[SEM_SPEC, SEM_SPEC] + [HBM_SPEC] * n + [VMEM_SPEC],
        out_shape=[pltpu.SemaphoreType.DMA((3 * n,)), pltpu.SemaphoreType.DMA((3 * n,))]
        + [pltpu.HBM(b.shape, b.dtype) for b in bufs] + [jax.ShapeDtypeStruct((8, 128), F32)],
        input_output_aliases={i: i + 2 for i in range(n)},
        compiler_params=pltpu.CompilerParams(has_side_effects=DATAFLOW))(*[_hbm(b) for b in bufs])
    return out[0], out[1], list(out[2:2 + n]), out[2 + n]


def allgather_wait(sends, recvs, bufs, after, name):
    n = len(bufs)

    def body(*refs):
        ins = refs[:n]
        send_sems, recv_sems = refs[n], refs[n + 1]
        x, y, c, chips = _place()
        for i in range(n):
            for j, (cx, cy) in enumerate(chips):
                mine = _half(ins[i].shape[1], c)
                cp = pltpu.make_async_remote_copy(
                    src_ref=ins[i].at[2 * x + y, mine], dst_ref=ins[i].at[2 * cx + cy, mine], send_sem=send_sems.at[3 * i + j],
                    recv_sem=recv_sems.at[3 * i + j], device_id=(cx, cy, c), device_id_type=MESH_IDS)
                cp.wait_send()
                cp.wait_recv()

    return pl.pallas_call(
        body, name=name, in_specs=[HBM_SPEC] * n + [SEM_SPEC, SEM_SPEC, ANY_SPEC], out_specs=[HBM_SPEC] * n,
        out_shape=[pltpu.HBM(b.shape, b.dtype) for b in bufs], input_output_aliases={i: i for i in range(n)},
        compiler_params=pltpu.CompilerParams(has_side_effects=DATAFLOW))(*bufs, sends, recvs, after)


def d2d_forward(bufs, name):
    n = len(bufs)

    def body(*refs):
        ins = refs[:n]
        send_sems, recv_sems = refs[2 * n:]
        x, y, c, chips = _place()
        copies = []
        for i in range(n):
            mine = _half(ins[i].shape[1], c)
            for j, (cx, cy) in enumerate(chips):
                landed = ins[i].at[2 * cx + cy, mine]
                cp = pltpu.make_async_remote_copy(
                    src_ref=landed, dst_ref=landed, send_sem=send_sems.at[i, j], recv_sem=recv_sems.at[i, j],
                    device_id=(x, y, 1 - c), device_id_type=MESH_IDS)
                cp.start()
                copies.append(cp)
        for i in range(n):
            theirs = _half(ins[i].shape[1], 1 - c)
            for j, (cx, cy) in enumerate(chips):
                passed = ins[i].at[2 * cx + cy, theirs]
                pltpu.make_async_remote_copy(
                    src_ref=passed, dst_ref=passed, send_sem=send_sems.at[i, j], recv_sem=recv_sems.at[i, j],
                    device_id=(x, y, 1 - c), device_id_type=MESH_IDS).wait_recv()
        for cp in copies:
            cp.wait_send()

    return pl.pallas_call(
        body, name=name, in_specs=[HBM_SPEC] * n, out_specs=[HBM_SPEC] * n,
        out_shape=[jax.ShapeDtypeStruct(b.shape, b.dtype) for b in bufs], input_output_aliases={i: i for i in range(n)},
        scratch_shapes=[pltpu.SemaphoreType.DMA((n, 3))] * 2,
        compiler_params=pltpu.CompilerParams(has_side_effects=True))(*bufs)


def _forward_copy(buf, sends, recvs, i, j, chip_xy, x, y, c):
    cx, cy = chip_xy
    rows = buf.shape[1]
    return pltpu.make_async_remote_copy(
        src_ref=buf.at[2 * cx + cy, _half(rows, c)], dst_ref=buf.at[2 * cx + cy, _half(rows, 1 - c)],
        send_sem=sends.at[3 * i + j], recv_sem=recvs.at[3 * i + j], device_id=(x, y, 1 - c), device_id_type=MESH_IDS)


def d2d_forward_start(bufs, name):
    n = len(bufs)

    def body(*refs):
        ins = refs[:n]
        sends, recvs = refs[n], refs[n + 1]
        token = refs[2 * n + 2]
        x, y, c, chips = _place()
        for i in range(n):
            mine = _half(ins[i].shape[1], c)
            for j, (cx, cy) in enumerate(chips):
                landed = ins[i].at[2 * cx + cy, mine]
                pltpu.make_async_remote_copy(
                    src_ref=landed, dst_ref=landed, send_sem=sends.at[3 * i + j], recv_sem=recvs.at[3 * i + j],
                    device_id=(x, y, 1 - c), device_id_type=MESH_IDS).start()
        token[...] = jnp.zeros_like(token)

    out = pl.pallas_call(
        body, name=name, in_specs=[HBM_SPEC] * n,
        out_specs=[SEM_SPEC, SEM_SPEC] + [HBM_SPEC] * n + [VMEM_SPEC],
        out_shape=[pltpu.SemaphoreType.DMA((3 * n,)), pltpu.SemaphoreType.DMA((3 * n,))]
        + [pltpu.HBM(b.shape, b.dtype) for b in bufs] + [jax.ShapeDtypeStruct((8, 128), F32)],
        input_output_aliases={i: i + 2 for i in range(n)},
        compiler_params=pltpu.CompilerParams(has_side_effects=DATAFLOW))(*[_hbm(b) for b in bufs])
    return (out[0], out[1], list(out[2:2 + n])), out[2 + n]


def d2d_forward_wait(started, after, name):
    sends, recvs, bufs = started
    n = len(bufs)

    def body(*refs):
        ins = refs[:n]
        send_sems, recv_sems = refs[n], refs[n + 1]
        x, y, c, chips = _place()
        for i in range(n):
            for j, chip_xy in enumerate(chips):
                cp = _forward_copy(ins[i], send_sems, recv_sems, i, j, chip_xy, x, y, c)
                cp.wait_send()
                cp.wait_recv()

    return pl.pallas_call(
        body, name=name, in_specs=[HBM_SPEC] * n + [SEM_SPEC, SEM_SPEC, ANY_SPEC], out_specs=[HBM_SPEC] * n,
        out_shape=[pltpu.HBM(b.shape, b.dtype) for b in bufs], input_output_aliases={i: i for i in range(n)},
        compiler_params=pltpu.CompilerParams(has_side_effects=DATAFLOW))(*bufs, sends, recvs, after)


def _sibling_copy(src, land, sends, recvs, i, x, y, c, halves):
    part = src.at[:, _half(src.shape[1], 1 - c)] if halves else src
    return pltpu.make_async_remote_copy(
        src_ref=part, dst_ref=land, send_sem=sends.at[i], recv_sem=recvs.at[i], device_id=(x, y, 1 - c),
        device_id_type=MESH_IDS)


def sibling_start(arrays, halves, name):
    n = len(arrays)
    lands = [lax.empty((a.shape[0], a.shape[1] // 2, a.shape[2]) if halves else a.shape, a.dtype) for a in arrays]

    def body(*refs):
        srcs, zones = refs[:n], refs[n:2 * n]
        sends, recvs = refs[2 * n], refs[2 * n + 1]
        token = refs[4 * n + 2]
        x, y, c, _ = _place()
        for i in range(n):
            _sibling_copy(srcs[i], zones[i], sends, recvs, i, x, y, c, halves).start()
        token[...] = jnp.zeros_like(token)

    out = pl.pallas_call(
        body, name=name, in_specs=[HBM_SPEC] * (2 * n),
        out_specs=[SEM_SPEC, SEM_SPEC] + [HBM_SPEC] * (2 * n) + [VMEM_SPEC],
        out_shape=[pltpu.SemaphoreType.DMA((n,)), pltpu.SemaphoreType.DMA((n,))]
        + [pltpu.HBM(a.shape, a.dtype) for a in arrays + lands] + [jax.ShapeDtypeStruct((8, 128), F32)],
        input_output_aliases={i: i + 2 for i in range(2 * n)},
        compiler_params=pltpu.CompilerParams(has_side_effects=DATAFLOW))(*[_hbm(a) for a in arrays + lands])
    return (out[0], out[1], list(out[2:2 + n]), list(out[2 + n:2 + 2 * n])), out[2 + 2 * n]


def sibling_wait(started, halves, after, name):
    sends, recvs, arrays, lands = started
    n = len(arrays)
    after = tuple(after) if isinstance(after, (tuple, list)) else (after,)

    def body(*refs):
        srcs, zones = refs[:n], refs[n:2 * n]
        send_sems, recv_sems = refs[2 * n], refs[2 * n + 1]
        x, y, c, _ = _place()
        for i in range(n):
            cp = _sibling_copy(srcs[i], zones[i], send_sems, recv_sems, i, x, y, c, halves)
            cp.wait_send()
            cp.wait_recv()

    out = pl.pallas_call(
        body, name=name, in_specs=[HBM_SPEC] * (2 * n) + [SEM_SPEC, SEM_SPEC] + [ANY_SPEC] * len(after),
        out_specs=[HBM_SPEC] * (2 * n),
        out_shape=[pltpu.HBM(a.shape, a.dtype) for a in arrays + lands], input_output_aliases={i: i for i in range(2 * n)},
        compiler_params=pltpu.CompilerParams(has_side_effects=DATAFLOW))(*arrays, *lands, sends, recvs, *after)
    return list(out[:n]), list(out[n:])


def _scatter_copy(src, land, sends, recvs, i, j, chip_xy, c):
    cx, cy = chip_xy
    return pltpu.make_async_remote_copy(
        src_ref=src.at[2 * cx + cy], dst_ref=land.at[j], send_sem=sends.at[3 * i + j], recv_sem=recvs.at[3 * i + j],
        device_id=(cx, cy, c), device_id_type=MESH_IDS)


def scatter_start(sums, name):
    n = len(sums)
    lands = [lax.empty((3,) + s.shape[1:], s.dtype) for s in sums]

    def body(*refs):
        srcs, zones = refs[:n], refs[n:2 * n]
        sends, recvs = refs[2 * n], refs[2 * n + 1]
        token = refs[4 * n + 2]
        _, _, c, chips = _place()
        for i in range(n):
            for j, chip_xy in enumerate(chips):
                _scatter_copy(srcs[i], zones[i], sends, recvs, i, j, chip_xy, c).start()
        token[...] = jnp.zeros_like(token)

    out = pl.pallas_call(
        body, name=name, in_specs=[HBM_SPEC] * (2 * n),
        out_specs=[SEM_SPEC, SEM_SPEC] + [HBM_SPEC] * (2 * n) + [VMEM_SPEC],
        out_shape=[pltpu.SemaphoreType.DMA((3 * n,)), pltpu.SemaphoreType.DMA((3 * n,))]
        + [pltpu.HBM(a.shape, a.dtype) for a in sums + lands] + [jax.ShapeDtypeStruct((8, 128), F32)],
        input_output_aliases={i: i + 2 for i in range(2 * n)},
        compiler_params=pltpu.CompilerParams(has_side_effects=DATAFLOW))(*[_hbm(a) for a in sums + lands])
    return out[0], out[1], list(out[2:2 + n]), list(out[2 + n:2 + 2 * n]), out[2 + 2 * n]


def scatter_wait(sends, recvs, sums, lands, after, name):
    n = len(sums)

    def body(*refs):
        srcs, zones = refs[:n], refs[n:2 * n]
        send_sems, recv_sems = refs[2 * n], refs[2 * n + 1]
        _, _, c, chips = _place()
        for i in range(n):
            for j, chip_xy in enumerate(chips):
                cp = _scatter_copy(srcs[i], zones[i], send_sems, recv_sems, i, j, chip_xy, c)
                cp.wait_send()
                cp.wait_recv()

    out = pl.pallas_call(
        body, name=name, in_specs=[HBM_SPEC] * (2 * n) + [SEM_SPEC, SEM_SPEC, ANY_SPEC], out_specs=[HBM_SPEC] * (2 * n),
        out_shape=[pltpu.HBM(a.shape, a.dtype) for a in sums + lands], input_output_aliases={i: i for i in range(2 * n)},
        compiler_params=pltpu.CompilerParams(has_side_effects=DATAFLOW))(*sums, *lands, sends, recvs, after)
    return list(out[:n]), list(out[n:])


def add_halves(grads, recvs, core, name):
    n = len(grads)
    nk, r, cdim = grads[0].shape
    half = r // 2
    views = [g.reshape(nk, 2, half, cdim) for g in grads]

    def body(core_ref, *refs):
        for g_ref, r_ref, o_ref in zip(refs[:n], refs[n:2 * n], refs[2 * n:]):
            o_ref[0] = (g_ref[0, 0].astype(F32) + r_ref[0].astype(F32)).astype(BF16)

    return pl.pallas_call(
        body, name=name,
        grid_spec=pltpu.PrefetchScalarGridSpec(
            num_scalar_prefetch=1, grid=(nk,),
            in_specs=[pl.BlockSpec((1, 1, half, cdim), lambda k, core_ref: (k, core_ref[0], 0, 0))] * n
            + [pl.BlockSpec((1, half, cdim), lambda k, core_ref: (k, 0, 0))] * n,
            out_specs=[pl.BlockSpec((1, half, cdim), lambda k, core_ref: (k, 0, 0))] * n),
        out_shape=[jax.ShapeDtypeStruct((nk, half, cdim), BF16)] * n, compiler_params=_params("parallel"))(core, *views, *recvs)


def add_chips(sums, lands, chip, name):
    n = len(sums)
    _, half, cdim = sums[0].shape
    tr = _tile(half, 128)

    def body(chip_ref, *refs):
        for s_ref, r_ref, o_ref in zip(refs[:n], refs[n:2 * n], refs[2 * n:]):
            o_ref[...] = ((s_ref[0].astype(F32) + r_ref[0].astype(F32)) + r_ref[1].astype(F32)) + r_ref[2].astype(F32)

    return pl.pallas_call(
        body, name=name,
        grid_spec=pltpu.PrefetchScalarGridSpec(
            num_scalar_prefetch=1, grid=(half // tr,),
            in_specs=[pl.BlockSpec((1, tr, cdim), lambda i, chip_ref: (chip_ref[0], i, 0))] * n
            + [pl.BlockSpec((3, tr, cdim), lambda i, chip_ref: (0, i, 0))] * n,
            out_specs=[pl.BlockSpec((tr, cdim), lambda i, chip_ref: (i, 0))] * n),
        out_shape=[jax.ShapeDtypeStruct((half, cdim), F32)] * n, compiler_params=_params("parallel"))(chip, *sums, *lands)


def _adamw_math(w, g, m, v):
    m = ADAM_B1 * m + (1.0 - ADAM_B1) * g
    v = ADAM_B2 * v + (1.0 - ADAM_B2) * (g * g)
    m_hat = m / (1.0 - ADAM_B1 ** ADAM_STEP)
    v_hat = v / (1.0 - ADAM_B2 ** ADAM_STEP)
    return -ADAM_LR * (m_hat / (jnp.sqrt(v_hat) + ADAM_EPS) + ADAM_WD * w), m, v


def adamw_one_half(ws, gs, ms, vs, core, name, own, into=None):
    n = len(ws)
    r, cdim = ws[0].shape
    half = r // 2
    tr = _tile(half, max(8, 256 // n))
    steps = half // tr
    prior = () if into is None else tuple(a for group in into for a in group)

    def body(core_ref, *refs):
        ins, outs = refs[:4 * n], refs[4 * n + len(prior):]
        for a in range(n):
            w_ref, g_ref, m_ref, v_ref = ins[a::n]
            g = g_ref[...]
            outs[a][...] = g
            outs[n + a][...], outs[2 * n + a][...], outs[3 * n + a][...] = _adamw_math(w_ref[...], g, m_ref[...], v_ref[...])

    def rows(i, core_ref):
        return ((core_ref[0] if own else 1 - core_ref[0]) * steps + i, 0)

    whole = pl.BlockSpec((tr, cdim), rows)
    part = pl.BlockSpec((tr, cdim), lambda i, core_ref: (i, 0))
    out = pl.pallas_call(
        body, name=name,
        grid_spec=pltpu.PrefetchScalarGridSpec(
            num_scalar_prefetch=1, grid=(steps,),
            in_specs=[whole] * n + [part] * n + [whole] * (2 * n) + [ANY_SPEC] * len(prior), out_specs=[whole] * (4 * n)),
        out_shape=[jax.ShapeDtypeStruct((r, cdim), F32)] * (4 * n),
        input_output_aliases={1 + 4 * n + j: j for j in range(len(prior))},
        compiler_params=_params("parallel"))(core, *ws, *gs, *ms, *vs, *prior)
    return [out[k * n:(k + 1) * n] for k in range(4)]


def adamw_halves_sparsecore(ws, owns, others, ms, vs, name):
    n = len(ws)
    r, cdim = ws[0].shape
    half_groups = r // 2 // SC_ROWS
    per_tile = -(-2 * half_groups // SC_TILES)

    def body(*refs):
        ins, outs, (wb, gb, mb, vb) = refs[:5 * n], refs[5 * n:9 * n], refs[9 * n:]
        tile = lax.axis_index("sc_tile") * 2 + lax.axis_index("sc_core")
        core = lax.axis_index("c")

        @pl.loop(0, per_tile)
        def _(it):
            group = tile + SC_TILES * it

            @pl.when(group < 2 * half_groups)
            def _():
                rows = pl.ds(group * SC_ROWS, SC_ROWS)
                in_own = (group < half_groups) == (core == 0)
                half_rows = pl.ds((group % half_groups) * SC_ROWS, SC_ROWS)
                for a in range(n):
                    w_hbm, own_hbm, other_hbm, m_hbm, v_hbm = ins[a::n]
                    g_out, d_out, m_out, v_out = outs[a::n]
                    pltpu.sync_copy(w_hbm.at[rows], wb)
                    pltpu.sync_copy(m_hbm.at[rows], mb)
                    pltpu.sync_copy(v_hbm.at[rows], vb)

                    @pl.when(in_own)
                    def _():
                        pltpu.sync_copy(own_hbm.at[half_rows], gb)

                    @pl.when(jnp.logical_not(in_own))
                    def _():
                        pltpu.sync_copy(other_hbm.at[half_rows], gb)

                    pltpu.sync_copy(gb, g_out.at[rows])

                    @pl.loop(0, SC_ROWS)
                    def _(row):
                        @pl.loop(0, cdim, step=SC_LANES)
                        def _(col):
                            at = (row, pl.ds(col, SC_LANES))
                            wb[at], mb[at], vb[at] = _adamw_math(wb[at], gb[at], mb[at], vb[at])

                    pltpu.sync_copy(wb, d_out.at[rows])
                    pltpu.sync_copy(mb, m_out.at[rows])
                    pltpu.sync_copy(vb, v_out.at[rows])

    out = pl.kernel(
        body, name=name, out_type=[jax.ShapeDtypeStruct((r, cdim), F32)] * (4 * n),
        mesh=plsc.VectorSubcoreMesh(core_axis_name="sc_core", subcore_axis_name="sc_tile"),
        scratch_types=[pltpu.VMEM((SC_ROWS, cdim), F32)] * 4)(*ws, *owns, *others, *ms, *vs)
    return [out[a::n] for a in range(n)]


def small_allreduce_adamw(g, w, m, v, deps=()):
    r, cdim = g.shape

    def body(g_ref, w_ref, m_ref, v_ref, go_ref, d_ref, mo_ref, vo_ref, pair, quad, d2d_send, d2d_recv, ici_send, ici_recv):
        x, y, c, chips = _place()
        me = 2 * x + y
        pair[c] = g_ref[...]
        swap = pltpu.make_async_remote_copy(
            src_ref=g_ref, dst_ref=pair.at[c], send_sem=d2d_send, recv_sem=d2d_recv, device_id=(x, y, 1 - c),
            device_id_type=MESH_IDS)
        swap.start()
        swap.wait()
        quad[me] = pair[0] + pair[1]
        copies = []
        for j, (cx, cy) in enumerate(chips):
            cp = pltpu.make_async_remote_copy(
                src_ref=quad.at[me], dst_ref=quad.at[me], send_sem=ici_send.at[j], recv_sem=ici_recv.at[j],
                device_id=(cx, cy, c), device_id_type=MESH_IDS)
            cp.start()
            copies.append(cp)
        for j, (cx, cy) in enumerate(chips):
            slot = quad.at[2 * cx + cy]
            pltpu.make_async_remote_copy(
                src_ref=slot, dst_ref=slot, send_sem=ici_send.at[j], recv_sem=ici_recv.at[j], device_id=(cx, cy, c),
                device_id_type=MESH_IDS).wait_recv()
        for cp in copies:
            cp.wait_send()
        total = (quad[0] + quad[1]) + (quad[2] + quad[3])
        go_ref[...] = total
        d_ref[...], mo_ref[...], vo_ref[...] = _adamw_math(w_ref[...], total, m_ref[...], v_ref[...])

    return pl.pallas_call(
        _ordered_after(body, 4, deps), name="small_allreduce_adamw", in_specs=[VMEM_SPEC] * 4 + [ANY_SPEC] * len(deps),
        out_specs=[VMEM_SPEC] * 4,
        out_shape=[jax.ShapeDtypeStruct((r, cdim), F32)] * 4,
        scratch_shapes=[pltpu.VMEM((2, r, cdim), F32), pltpu.VMEM((N_CHIPS, r, cdim), F32), pltpu.SemaphoreType.DMA,
                        pltpu.SemaphoreType.DMA, pltpu.SemaphoreType.DMA((3,)), pltpu.SemaphoreType.DMA((3,))],
        compiler_params=pltpu.CompilerParams(has_side_effects=True, vmem_limit_bytes=VMEM_LIMIT_V7X))(g, w, m, v, *deps)


GROUPS = (
    ("ffn1", ("ffn1_w_gate", "ffn1_w_up", "ffn1_w_down")),
    ("mixer", ("w_in", "pool_w", "w_out_a", "w_out_b", "w_o")),
    ("ffn2", ("ffn2_w_gate", "ffn2_w_up", "ffn2_w_down")),
    ("ple", ("ple_w_gate", "ple_w_proj")),
)
GATHERS = (
    ("ffn1_in", ("ffn1_w_gate", "ffn1_w_up")),
    ("ffn1_out", ("ffn1_w_down",)),
) + GROUPS[1:]
GAINS = ("ffn1_pre_g", "ffn1_post_g", "mix_pre_g", "sgu_norm_g", "pool_scale", "mix_post_g",
         "ffn2_pre_g", "ffn2_post_g", "ple_pre_g", "ple_post_g")
SMALL = GAINS + ("sgu_b", "sgu_w")
PACKED = SMALL + ("loss",)
WEIGHTS = ("ffn1_pre_g", "ffn1_w_gate", "ffn1_w_up", "ffn1_w_down", "ffn1_post_g", "mix_pre_g", "w_in", "sgu_norm_g",
           "sgu_w", "sgu_b", "pool_w", "pool_scale", "w_out_a", "w_out_b", "w_o", "mix_post_g", "ffn2_pre_g",
           "ffn2_w_gate", "ffn2_w_up", "ffn2_w_down", "ffn2_post_g", "ple_pre_g", "ple_w_gate", "ple_w_proj", "ple_post_g")
PACK_ROWS = 16


TRANSPOSED = ("ffn1_w_gate", "ffn1_w_up", "ffn2_w_gate", "ffn2_w_up")


def _shard2d(name, a):
    a = a[0]
    return a.T if name in TRANSPOSED else a.reshape(-1, a.shape[-1])


def _unshard2d(name, a2d, shape):
    return (a2d.T if name in TRANSPOSED else a2d).reshape(shape)


def pack_rows(gains, sgu_b, loss_tile, name):
    n = len(gains)
    d = gains[0].shape[1]
    g = sgu_b.shape[0]

    def body(*refs):
        o_ref = refs[-1]
        o_ref[...] = jnp.zeros_like(o_ref)
        for i in range(n):
            o_ref[i:i + 1, :] = refs[i][...]
        o_ref[n:n + g, 0:CHUNK] = refs[n][...]
        o_ref[n + g:n + g + 1, 0:CHUNK] = refs[n + 1][0:1, :]

    return pl.pallas_call(
        body, name=name, in_specs=[VMEM_SPEC] * (n + 2), out_specs=VMEM_SPEC,
        out_shape=jax.ShapeDtypeStruct((PACK_ROWS, d), F32))(*gains, sgu_b, loss_tile)


def _pack_small(parts, tag):
    d = parts[GAINS[0]].shape[-1]
    rows = pack_rows([parts[n] for n in GAINS], parts["sgu_b"].reshape(-1, CHUNK), parts["loss"], "pack_" + tag)
    return jnp.concatenate([rows, parts["sgu_w"].reshape(-1, d)], axis=0)


def _unpack_small(packed, like):
    n, g = len(GAINS), like["sgu_b"].size // CHUNK
    out = {name: packed[i:i + 1] for i, name in enumerate(GAINS)}
    out["sgu_b"] = packed[n:n + g, :CHUNK].reshape(like["sgu_b"].shape)
    out["loss"] = packed[n + g, 0]
    out["sgu_w"] = packed[PACK_ROWS:].reshape(like["sgu_w"].shape)
    return out


def _ffn_fwd(xn, h, w, pre, g_post, g_next, tag, between=None, first=None):
    a4, s4, t4 = ffn_gu(xn, w[pre + "w_gate"], w[pre + "w_up"], tag + "_gu") if first is None else first(xn)
    deps = ()
    if between is not None:
        more, deps = between(a4)
        w.update(more)
    f, h_new, xn_next = mm_norm_res(a4, w[pre + "w_down"], h, g_post, g_next, 0.5, tag + "_down", deps=deps)
    return dict(xn=xn, h=h, a4=a4, s4=s4, t4=t4, f=f), h_new, xn_next


def _ffn_bwd_w(dh, saved, w, pre, g_post, tag, deps):
    df, dg4, du4, d_post = ffn_bwd_a(dh, saved["f"], g_post, w[pre + "w_down"], saved["s4"], saved["t4"], 0.5, tag + "_bwd_a",
                                     deps=deps)
    return dg4, du4, _ffn_dw(saved, df, dg4, du4, pre, tag), d_post


def _ffn_dw(saved, df, dg4, du4, pre, tag):
    nk = N_CHIPS
    return {
        pre + "w_down": dw_tn(saved["a4"], df, nk, tag + "_dw_down", x_kmajor=True, dy_mode="same"),
        pre + "w_gate": dw_tn(dg4, saved["xn"], nk, tag + "_dw_gate", x_kmajor=True, dy_mode="same"),
        pre + "w_up": dw_tn(du4, saved["xn"], nk, tag + "_dw_up", x_kmajor=True, dy_mode="same"),
    }


def _ffn_bwd_x(dh, dg4, du4, saved, w, pre, g_pre, tag, deps):
    return dx_norm_bwd([(dg4, w[pre + "w_gate"], False), (du4, w[pre + "w_up"], False)], saved["h"], g_pre, dh,
                       tag + "_bwd_x", deps=deps)


def _gather_group(names, weights, chip, tag, deps, plain=False):
    shards = [_shard2d(n, weights[n]) for n in names]
    bufs = [None] * len(names)
    own = None
    for b, idx in enumerate(_by_shape(shards)):
        made = cast_place([shards[i] for i in idx], chip, f"cast_{tag}_{b}", deps=deps, plain=plain)
        if plain:
            made, own = made
        for i, buf in zip(idx, made):
            bufs[i] = buf
    sends, recvs, bufs, token = allgather_start(bufs, "allgather_start_" + tag)
    return ((sends, recvs, bufs), token, dict(zip(names, own))) if plain else ((sends, recvs, bufs), token)


def _gathered(started, names, after, tag):
    sends, recvs, bufs = started
    landed = allgather_wait(sends, recvs, bufs, after, "allgather_wait_" + tag)
    return dict(zip(names, d2d_forward(landed, "d2d_forward_" + tag)))


def _forward_early(started, after, tag):
    sends, recvs, bufs = started
    landed = allgather_wait(sends, recvs, bufs, after, "allgather_wait_" + tag)
    return d2d_forward_start(landed, "d2d_forward_start_" + tag)


def _forwarded(forwarding, names, after, tag):
    return dict(zip(names, d2d_forward_wait(forwarding, after, "d2d_forward_wait_" + tag)))


def _exchange_start(names, big_g, tag):
    return sibling_start([big_g[n] for n in names], True, "exchange_start_" + tag)


def _scatter_begin(names, exchanging, after, core, tag):
    partial, from_sibling = sibling_wait(exchanging, True, after, "exchange_wait_" + tag)
    chip_sums = [None] * len(names)
    for b, idx in enumerate(_by_shape(partial)):
        for i, s in zip(idx, add_halves([partial[i] for i in idx], [from_sibling[i] for i in idx], core, f"add_halves_{tag}_{b}")):
            chip_sums[i] = s
    sends, recvs, sums, lands, token = scatter_start(chip_sums, "scatter_start_" + tag)
    return (names, sends, recvs, sums, lands), token


def _share_begin(scattering, after, chip, tag):
    names, sends, recvs, sums, lands = scattering
    sums, lands = scatter_wait(sends, recvs, sums, lands, after, "scatter_wait_" + tag)
    reduced = [None] * len(names)
    for b, idx in enumerate(_by_shape(sums)):
        for i, r in zip(idx, add_chips([sums[i] for i in idx], [lands[i] for i in idx], chip, f"add_chips_{tag}_{b}")):
            reduced[i] = r
    sharing, token = sibling_start(reduced, False, "share_start_" + tag)
    return (names, sharing), token


def _update(shared, after, weights, moments_m, moments_v, tag, results):
    names, sharing = shared
    reduced, others = sibling_wait(sharing, False, after, "share_wait_" + tag)
    for b, idx in enumerate(_by_shape(reduced)):
        in_bucket = [names[i] for i in idx]
        outs = adamw_halves_sparsecore(
            [_shard2d(n, weights[n]) for n in in_bucket], [reduced[i] for i in idx], [others[i] for i in idx],
            [_shard2d(n, moments_m[n]) for n in in_bucket], [_shard2d(n, moments_v[n]) for n in in_bucket], f"adamw_{tag}_{b}")
        for n, per_weight in zip(in_bucket, outs):
            for store, value in zip(results, per_weight):
                store[n] = _unshard2d(n, value, weights[n].shape)


def _update_last(scattering, after, weights, moments_m, moments_v, core, chip, tag, results):
    names, sends, recvs, sums, lands = scattering
    sums, lands = scatter_wait(sends, recvs, sums, lands, after, "scatter_wait_" + tag)
    reduced = list(add_chips(sums, lands, chip, f"add_chips_{tag}_0"))
    (sends, recvs, reduced, zones), _ = sibling_start(reduced, False, "share_start_" + tag)
    state = [[_shard2d(n, held[n]) for n in names] for held in (weights, moments_m, moments_v)]
    first = adamw_one_half(state[0], reduced, state[1], state[2], core, f"adamw_{tag}_own", own=True)
    _, others = sibling_wait((sends, recvs, reduced, zones), False, first[1][-1], "share_wait_" + tag)
    outs = adamw_one_half(state[0], others, state[1], state[2], core, f"adamw_{tag}_other", own=False, into=first)
    for store, values in zip(results, outs):
        for n, value in zip(names, values):
            store[n] = _unshard2d(n, value, weights[n].shape)


def kernel(x, p, ffn1_pre_g, ffn1_w_gate, ffn1_w_up, ffn1_w_down, ffn1_post_g, mix_pre_g, w_in, sgu_norm_g, sgu_w, sgu_b, pool_w, pool_scale, w_out_a, w_out_b, w_o, mix_post_g, ffn2_pre_g, ffn2_w_gate, ffn2_w_up, ffn2_w_down, ffn2_post_g, ple_pre_g, ple_w_gate, ple_w_proj, ple_post_g, loss_target, m_ffn1_pre_g, m_ffn1_w_gate, m_ffn1_w_up, m_ffn1_w_down, m_ffn1_post_g, m_mix_pre_g, m_w_in, m_sgu_norm_g, m_sgu_w, m_sgu_b, m_pool_w, m_pool_scale, m_w_out_a, m_w_out_b, m_w_o, m_mix_post_g, m_ffn2_pre_g, m_ffn2_w_gate, m_ffn2_w_up, m_ffn2_w_down, m_ffn2_post_g, m_ple_pre_g, m_ple_w_gate, m_ple_w_proj, m_ple_post_g, v_ffn1_pre_g, v_ffn1_w_gate, v_ffn1_w_up, v_ffn1_w_down, v_ffn1_post_g, v_mix_pre_g, v_w_in, v_sgu_norm_g, v_sgu_w, v_sgu_b, v_pool_w, v_pool_scale, v_w_out_a, v_w_out_b, v_w_o, v_mix_post_g, v_ffn2_pre_g, v_ffn2_w_gate, v_ffn2_w_up, v_ffn2_w_down, v_ffn2_post_g, v_ple_pre_g, v_ple_w_gate, v_ple_w_proj, v_ple_post_g):
    given = dict(locals())
    weights = {n: given[n] for n in WEIGHTS}
    moments_m = {n: given["m_" + n] for n in WEIGHTS}
    moments_v = {n: given["v_" + n] for n in WEIGHTS}
    core = lax.axis_index("c").astype(jnp.int32).reshape(1)
    chip = (2 * lax.axis_index("x") + lax.axis_index("y")).astype(jnp.int32).reshape(1)

    d = x.shape[-1]
    token = ()
    gathering = {}
    for tag, names in GATHERS:
        if tag == GATHERS[0][0]:
            gathering[tag], tok, own_first = _gather_group(names, weights, chip, tag, token, plain=True)
        else:
            gathering[tag], tok = _gather_group(names, weights, chip, tag, token)
        token = (tok,)
    gain = {n: weights[n] for n in GAINS}
    sgu_w3 = sgu_w[0]
    sgu_b3 = sgu_b[0][:, :, None]
    groups = dict(GROUPS + GATHERS)

    h0 = x[0]
    tgt = loss_target[0]
    p_bf = p[0, 0].astype(BF16)
    xn1 = rms_cast(h0, gain["ffn1_pre_g"], "ffn1_pre_norm")
    w = {}

    def ffn1_first(xn):
        own = ffn_gu(xn, own_first["ffn1_w_gate"][None], own_first["ffn1_w_up"][None], "ffn1_gu_own", chip=chip)
        w.update(_gathered(gathering["ffn1_in"], groups["ffn1_in"], own[0], "ffn1_in"))
        return ffn_gu(xn, w["ffn1_w_gate"], w["ffn1_w_up"], "ffn1_gu", chip=chip, into=own)

    s1, h1, xn2 = _ffn_fwd(xn1, h0, w, "ffn1_", gain["ffn1_post_g"], gain["mix_pre_g"], "ffn1",
                           lambda a4: (_gathered(gathering["ffn1_out"], groups["ffn1_out"], a4, "ffn1_out"), ()), ffn1_first)
    w.update(_gathered(gathering["mixer"], groups["mixer"], h1, "mixer"))
    full = {n: w[n].reshape(-1, d) for n in ("w_out_a", "w_out_b", "w_o")}
    n_groups = pool_w.shape[1]
    rows_per = pool_w.shape[2]
    dgp = pool_w.shape[3]
    pool_full = w["pool_w"].reshape(N_CHIPS, n_groups, rows_per, dgp).transpose(1, 0, 2, 3).reshape(n_groups, N_CHIPS * rows_per, dgp)
    z = mixer_in(xn2, w["w_in"], "mixer_in")
    a = sgu_fwd(z, gain["sgu_norm_g"], sgu_w3, sgu_b3, "sgu_fwd")
    diff, b = pool_fwd(z, pool_full, gain["pool_scale"], "pool_fwd")
    ya, yb, y = mixer_y(a, b, z, full["w_out_a"], full["w_out_b"], "mixer_y")
    forwarding, tok = _forward_early(gathering["ffn2"], y, "ffn2")
    m, h2, xn3 = mm_norm_res(y[None], full["w_o"][None], h1, gain["mix_post_g"], gain["ffn2_pre_g"], 1.0, "mixer_out",
                             deps=(tok,))
    w.update(_forwarded(forwarding, groups["ffn2"], h2, "ffn2"))
    early = {}

    def forward_ple(a4):
        early["ple"], tok_ple = _forward_early(gathering["ple"], a4, "ple")
        return {}, (tok_ple,)

    s2, h3, xn4 = _ffn_fwd(xn3, h2, w, "ffn2_", gain["ffn2_post_g"], gain["ple_pre_g"], "ffn2", forward_ple)
    w.update(_forwarded(early["ple"], groups["ple"], h3, "ple"))
    full["ple_w_gate"] = w["ple_w_gate"].reshape(-1, d)
    proj_full = w["ple_w_proj"].transpose(1, 0, 2).reshape(ple_w_proj.shape[1], -1)

    small_g, big_g = {}, {}
    ds, de, dh3, small_g["ple_post_g"], small_g["ple_pre_g"], loss_part = ple_loss(
        xn4, p_bf, full["ple_w_gate"], proj_full, h3, gain["ple_post_g"], gain["ple_pre_g"], tgt, "ple_loss")
    small_g["loss"] = loss_part
    grad, delta, new_m, new_v = {}, {}, {}, {}
    results = (grad, delta, new_m, new_v)
    update = functools.partial(_update, weights=weights, moments_m=moments_m, moments_v=moments_v, results=results)

    def reduce_behind(tag, previous):
        exchanging, tok = _exchange_start(groups[tag], big_g, tag)
        if previous is not None:
            shared, tok = _share_begin(previous[1], tok, chip, previous[0])
        scattering, tok_scatter = _scatter_begin(groups[tag], exchanging, tok, core, tag)
        if previous is not None:
            update(shared, tok_scatter, tag=previous[0])
        return (tag, scattering), (tok_scatter,)

    big_g["ple_w_gate"] = dw_tn(xn4, ds, 1, "dw_ple_gate").reshape(N_CHIPS, -1, d)
    big_g["ple_w_proj"] = dw_tn(p_bf, de, N_CHIPS, "dw_ple_proj", dy_mode="cols")
    reducing, deps = reduce_behind("ple", None)

    dg4, du4, g2, small_g["ffn2_post_g"] = _ffn_bwd_w(dh3, s2, w, "ffn2_", gain["ffn2_post_g"], "ffn2", deps)
    big_g.update(g2)
    dh2, small_g["ffn2_pre_g"] = _ffn_bwd_x(dh3, dg4, du4, s2, w, "ffn2_", gain["ffn2_pre_g"], "ffn2", ())
    reducing, deps = reduce_behind("ffn2", reducing)

    dm, dya, dyb, dz, da, db, small_g["mix_post_g"] = mixer_bwd_y(
        dh2, m, gain["mix_post_g"], full["w_o"], ya, yb, z, full["w_out_a"], full["w_out_b"], "mixer_bwd_y", deps=deps)
    big_g["w_o"] = dw_tn(y, dm, 1, "dw_o").reshape(N_CHIPS, -1, d)
    big_g["w_out_a"] = dw_tn(a, dya, 1, "dw_out_a").reshape(N_CHIPS, -1, d)
    big_g["w_out_b"] = dw_tn(b, dyb, 1, "dw_out_b").reshape(N_CHIPS, -1, d)
    dz, d_sgu_w, d_sgu_b, small_g["sgu_norm_g"] = sgu_bwd(z, da, dz, gain["sgu_norm_g"], sgu_w3, sgu_b3, "sgu_bwd")
    dz, d_pool_w, small_g["pool_scale"] = pool_bwd(db, diff, dz, pool_full, gain["pool_scale"], "pool_bwd")
    big_g["pool_w"] = d_pool_w.astype(BF16).reshape(n_groups, N_CHIPS, rows_per, dgp).transpose(1, 0, 2, 3).reshape(
        N_CHIPS, n_groups * rows_per, dgp)
    big_g["w_in"] = dw_in_tiles(xn2, dz, N_CHIPS, "dw_in")
    dh1, small_g["mix_pre_g"] = dx_norm_bwd([(dz, w["w_in"], True)], h1, gain["mix_pre_g"], dh2, "mixer_bwd_x")
    reducing, deps = reduce_behind("mixer", reducing)

    dg4, du4, g1, small_g["ffn1_post_g"] = _ffn_bwd_w(dh1, s1, w, "ffn1_", gain["ffn1_post_g"], "ffn1", deps)
    big_g.update(g1)
    reducing, deps = reduce_behind("ffn1", reducing)
    dh0, small_g["ffn1_pre_g"] = _ffn_bwd_x(dh1, dg4, du4, s1, w, "ffn1_", gain["ffn1_pre_g"], "ffn1", deps)
    small_g["sgu_w"] = d_sgu_w
    small_g["sgu_b"] = d_sgu_b[:, :, 0]

    no_state = {"loss": jnp.zeros((8, 128), F32)}
    packed = small_allreduce_adamw(
        _pack_small(small_g, "grads"), _pack_small({n: weights[n] for n in SMALL} | no_state, "weights"),
        _pack_small({n: moments_m[n] for n in SMALL} | no_state, "m"), _pack_small({n: moments_v[n] for n in SMALL} | no_state, "v"))
    like = {n: weights[n] for n in SMALL}
    for store, block in zip((grad, delta, new_m, new_v), packed):
        store.update(_unpack_small(block, like))
    _update_last(reducing[1], packed[0], weights, moments_m, moments_v, core, chip, "ffn1", results)

    return (grad["loss"], dh0[None], *[grad[n] for n in WEIGHTS], *[delta[n] for n in WEIGHTS],
            *[new_m[n] for n in WEIGHTS], *[new_v[n] for n in WEIGHTS])
```

```python
import functools

import jax
import jax.numpy as jnp
from jax import lax
from jax.experimental import pallas as pl
from jax.experimental.pallas import tpu as pltpu
from jax.experimental.pallas import tpu_sc as plsc

F32 = jnp.float32
BF16 = jnp.bfloat16
EPS = 1e-6
CHUNK = 128
POOL_WINDOWS = (2, 4, 8, 16)
HALO = 16
N_CHIPS = 4
ADAM_LR, ADAM_B1, ADAM_B2, ADAM_EPS, ADAM_WD, ADAM_STEP = 0.001, 0.9, 0.999, 1e-08, 0.01, 10
VMEM_LIMIT_V7X = 58 * 1024 * 1024
MESH_IDS = pl.DeviceIdType.MESH
HBM_SPEC = pl.BlockSpec(memory_space=pltpu.HBM)
VMEM_SPEC = pl.BlockSpec(memory_space=pltpu.VMEM)
SEM_SPEC = pl.BlockSpec(memory_space=pltpu.SEMAPHORE)
ANY_SPEC = pl.BlockSpec(memory_space=pl.ANY)
DATAFLOW = pltpu.SideEffectType.DATAFLOW_SIDE_EFFECTING
OTHER_CHIPS = ((1, 0), (0, 1), (1, 1))
DZ_SLOT = (2, 3, 4, 0, 1)
SC_TILES, SC_LANES, SC_ROWS = 32, 16, 8
SIBLING_BARRIER = 1
DW_TOKENS = 4096
DW_IN_TILE = 256

NT = (((1,), (1,)), ((), ()))
TN = (((0,), (0,)), ((), ()))


def _params(*sem, **more):
    return pltpu.CompilerParams(dimension_semantics=sem or None, vmem_limit_bytes=VMEM_LIMIT_V7X, **more)


def _tile(t, want):
    return max(c for c in range(8, min(t, want) + 1, 8) if t % c == 0)


def _const(shape):
    return pl.BlockSpec(shape, lambda *_: (0,) * len(shape))


def _rows(tm, d, col=0):
    return pl.BlockSpec((tm, d), lambda i: (i, col))


def _kmajor(nk, tm, kb):
    return pl.BlockSpec((nk, tm, kb), lambda i: (0, i, 0))


def _dot(a, b):
    return jnp.dot(a, b, preferred_element_type=F32)


def _dot_nt(a, b):
    return lax.dot_general(a, b, NT, preferred_element_type=F32)


def _dot_tn(a, b):
    return lax.dot_general(a, b, TN, preferred_element_type=F32)


def _gelu(x):
    return 0.5 * x * (1.0 + jnp.tanh(0.7978845608028654 * (x + 0.044715 * x * x * x)))


def _gelu_and_grad(x):
    k, kc = 0.7978845608028654, 0.7978845608028654 * 0.044715
    x2 = x * x
    th = jnp.tanh(x * (k + kc * x2))
    cdf = 0.5 + 0.5 * th
    return x * cdf, cdf + x * (0.5 - 0.5 * th * th) * (k + 3.0 * kc * x2)


def _sigmoid(x):
    return 1.0 / (1.0 + jnp.exp(-x))


def _rstd(h):
    return lax.rsqrt(jnp.mean(h * h, axis=-1, keepdims=True) + EPS)


def _rms_bwd(h, g, dy):
    r = _rstd(h)
    t = dy * g
    dh = r * t - h * (r * r * r) * jnp.mean(h * t, axis=-1, keepdims=True)
    return dh, jnp.sum(dy * h * r, axis=0, keepdims=True)


def _ordered_after(body, n_in, deps):
    if not deps:
        return body
    return lambda *refs: body(*refs[:n_in], *refs[n_in + len(deps):])


def _accumulate(ref, value):
    @pl.when(pl.program_id(0) == 0)
    def _():
        ref[...] = jnp.zeros_like(ref)

    ref[...] += value


def rms_cast(h, g, name):
    t, d = h.shape
    tm = _tile(t, 512)

    def body(h_ref, g_ref, o_ref):
        hv = h_ref[...]
        o_ref[...] = (hv * _rstd(hv) * g_ref[...]).astype(BF16)

    return pl.pallas_call(
        body, name=name, grid=(t // tm,), in_specs=[_rows(tm, d), _const((1, d))], out_specs=_rows(tm, d),
        out_shape=jax.ShapeDtypeStruct((t, d), BF16), compiler_params=_params("parallel"))(h, g)


def ffn_gu(xn, wgt, wut, name, chip=None, into=None):
    t, d = xn.shape
    nkw, fk, _ = wgt.shape
    tm = _tile(t, 512)
    count = N_CHIPS if chip is None else (1 if into is None else N_CHIPS - 1)
    first = 0 if into is None else 1

    def body(*refs):
        x_ref, wg_ref, wu_ref = refs[-6 - (0 if into is None else 3):][:3]
        a_ref, s_ref, t_ref = refs[-3:]
        xv = x_ref[...]
        g = _dot_nt(xv, wg_ref[0])
        sg = _sigmoid(g)
        s = g * sg
        s_ref[0] = s.astype(BF16)
        ds = sg * (1.0 + g * (1.0 - sg))
        u = _dot_nt(xv, wu_ref[0])
        a_ref[0] = (s * u).astype(BF16)
        t_ref[0] = (u * ds).astype(BF16)

    def slot(k, *chip_ref):
        return (chip_ref[0][0] + first + k) % N_CHIPS if chip_ref else k

    w_spec = pl.BlockSpec((1, fk, d), lambda k, i, *c: (slot(k, *c) if nkw > 1 else 0, 0, 0))
    o_spec = pl.BlockSpec((1, tm, fk), lambda k, i, *c: (slot(k, *c), i, 0))
    shape = jax.ShapeDtypeStruct((N_CHIPS, t, fk), BF16)
    prior = () if into is None else tuple(into)
    call = pl.pallas_call(
        body, name=name,
        grid_spec=pltpu.PrefetchScalarGridSpec(
            num_scalar_prefetch=0 if chip is None else 1, grid=(count, t // tm),
            in_specs=[pl.BlockSpec((tm, d), lambda k, i, *c: (i, 0)), w_spec, w_spec] + [ANY_SPEC] * len(prior),
            out_specs=[o_spec] * 3),
        out_shape=[shape] * 3, input_output_aliases={4 + j: j for j in range(len(prior))},
        compiler_params=_params("parallel", "parallel"))
    return call(*(() if chip is None else (chip,)), xn, wgt, wut, *prior)


def mm_norm_res(a3, w3, h_old, g_post, g_next, scale, name, deps=()):
    nk, t, kb = a3.shape
    d = w3.shape[2]
    tm = _tile(t, 256)

    def body(a_ref, w_ref, h_ref, gp_ref, gn_ref, f_ref, hn_ref, xn_ref):
        f = _dot(a_ref[0], w_ref[0])
        for k in range(1, nk):
            f += _dot(a_ref[k], w_ref[k])
        f_ref[...] = f.astype(BF16)
        hn = h_ref[...] + scale * (f * _rstd(f) * gp_ref[...])
        hn_ref[...] = hn
        xn_ref[...] = (hn * _rstd(hn) * gn_ref[...]).astype(BF16)

    return pl.pallas_call(
        _ordered_after(body, 5, deps), name=name, grid=(t // tm,),
        in_specs=[_kmajor(nk, tm, kb), _const((nk, kb, d)), _rows(tm, d), _const((1, d)), _const((1, d))] + [ANY_SPEC] * len(deps),
        out_specs=[_rows(tm, d)] * 3,
        out_shape=[jax.ShapeDtypeStruct((t, d), BF16), jax.ShapeDtypeStruct((t, d), F32), jax.ShapeDtypeStruct((t, d), BF16)],
        compiler_params=_params("parallel"))(a3, w3, h_old, g_post, g_next, *deps)


def mixer_in(xn, win4, name):
    t, d = xn.shape
    nk, _, nb = win4.shape
    tm = _tile(t, 512)

    def body(x_ref, w_ref, z_ref):
        z_ref[...] = _dot(x_ref[...], w_ref[0]).astype(BF16)

    return pl.pallas_call(
        body, name=name, grid=(nk, t // tm),
        in_specs=[pl.BlockSpec((tm, d), lambda k, i: (i, 0)), pl.BlockSpec((1, d, nb), lambda k, i: (k, 0, 0))],
        out_specs=pl.BlockSpec((tm, nb), lambda k, i: (i, k)), out_shape=jax.ShapeDtypeStruct((t, nk * nb), BF16),
        compiler_params=_params("parallel", "parallel"))(xn, win4)


def _causal_mask():
    row = lax.broadcasted_iota(jnp.int32, (CHUNK, CHUNK), 0)
    col = lax.broadcasted_iota(jnp.int32, (CHUNK, CHUNK), 1)
    return row >= col


def _layernorm_parts(v):
    mu = jnp.mean(v, axis=-1, keepdims=True)
    vc = v - mu
    r = lax.rsqrt(jnp.mean(vc * vc, axis=-1, keepdims=True) + EPS)
    return vc * r, r


def sgu_fwd(z, norm_g, sgu_w, sgu_b3, name):
    t = z.shape[0]
    d = norm_g.shape[1]
    ng = sgu_w.shape[0]
    dg = d // ng
    tm = _tile(t, 256)

    def body(zu_ref, zv_ref, ng_ref, w_ref, b_ref, a_ref):
        vhat, _ = _layernorm_parts(_gelu(zv_ref[...].astype(F32)))
        vn = (vhat * ng_ref[...]).astype(BF16)
        u = _gelu(zu_ref[...].astype(F32))
        mask = _causal_mask()
        for g in range(ng):
            wg = jnp.where(mask, w_ref[g], 0.0).astype(BF16)
            for ci in range(tm // CHUNK):
                rs, cs = slice(ci * CHUNK, (ci + 1) * CHUNK), slice(g * dg, (g + 1) * dg)
                sv = _dot(wg, vn[rs, cs]) + b_ref[g]
                a_ref[rs, cs] = (u[rs, cs] * sv).astype(BF16)

    return pl.pallas_call(
        body, name=name, grid=(t // tm,),
        in_specs=[_rows(tm, d, 0), _rows(tm, d, 1), _const((1, d)), _const((ng, CHUNK, CHUNK)), _const((ng, CHUNK, 1))],
        out_specs=_rows(tm, d), out_shape=jax.ShapeDtypeStruct((t, d), BF16),
        compiler_params=_params("parallel"))(z, z, norm_g, sgu_w, sgu_b3)


def pool_fwd(z, pool_w, pool_scale, name):
    t = z.shape[0]
    d = pool_scale.shape[1]
    ng = pool_w.shape[0]
    dg = d // ng
    tm = _tile(t, 256)
    per = tm // HALO

    def body(c_ref, prev_ref, w_ref, s_ref, diff_ref, b_ref):
        i = pl.program_id(0)
        cur = c_ref[...].astype(F32)
        prev = jnp.where(i > 0, prev_ref[...].astype(F32), 0.0)
        ext = jnp.concatenate([prev, cur], axis=0)
        tok = i * tm + lax.broadcasted_iota(jnp.int32, (tm, 1), 0)
        for g, win in enumerate(POOL_WINDOWS):
            cs = slice(g * dg, (g + 1) * dg)
            s = ext[:, cs]
            sh = 1
            while sh < win:
                s = s + pltpu.roll(s, sh, 0)
                sh *= 2
            per_count = 1.0 / jnp.minimum(tok + 1, win).astype(F32)
            diff = (s[HALO:] * per_count - cur[:, cs]).astype(BF16)
            diff_ref[:, cs] = diff
            b_ref[:, cs] = (_dot(diff, w_ref[g]) * s_ref[:, cs]).astype(BF16)

    return pl.pallas_call(
        body, name=name, grid=(t // tm,),
        in_specs=[_rows(tm, d, 2), pl.BlockSpec((HALO, d), lambda i: (jnp.maximum(i * per - 1, 0), 2)),
                  _const((ng, dg, dg)), _const((1, d))],
        out_specs=[_rows(tm, d)] * 2, out_shape=[jax.ShapeDtypeStruct((t, d), BF16)] * 2,
        compiler_params=_params("parallel"))(z, z, pool_w, pool_scale)


def mixer_y(a, b, z, woa, wob, name):
    t, d = a.shape
    tm = _tile(t, 256)

    def body(a_ref, b_ref, ga_ref, gb_ref, wa_ref, wb_ref, ya_ref, yb_ref, y_ref):
        ya = _dot(a_ref[...], wa_ref[...])
        yb = _dot(b_ref[...], wb_ref[...])
        ya_ref[...] = ya.astype(BF16)
        yb_ref[...] = yb.astype(BF16)
        y_ref[...] = (_sigmoid(ga_ref[...].astype(F32)) * ya + _sigmoid(gb_ref[...].astype(F32)) * yb).astype(BF16)

    return pl.pallas_call(
        body, name=name, grid=(t // tm,),
        in_specs=[_rows(tm, d), _rows(tm, d), _rows(tm, d, 3), _rows(tm, d, 4), _const((d, d)), _const((d, d))],
        out_specs=[_rows(tm, d)] * 3, out_shape=[jax.ShapeDtypeStruct((t, d), BF16)] * 3,
        compiler_params=_params("parallel"))(a, b, z, z, woa, wob)


def ple_loss(xn, p, wpg, wpp, h, g_post, g_pre, target, name):
    t, d = xn.shape
    dp = p.shape[1]
    tm = _tile(t, 256)

    def body(x_ref, p_ref, wg_ref, wp_ref, h_ref, gp_ref, gn_ref, tg_ref, ds_ref, de_ref, dhp_ref, dgp_ref, dgn_ref, loss_ref):
        gate = _sigmoid(_dot(x_ref[...], wg_ref[...]))
        e = _dot(p_ref[...], wp_ref[...])
        q = gate * e
        hv = h_ref[...]
        err = hv + q * _rstd(q) * gp_ref[...] - tg_ref[...]
        _accumulate(loss_ref, jnp.full(loss_ref.shape, (0.5 / d) * jnp.sum(err * err), F32))
        dhv = err * (1.0 / d)
        dq, dgp = _rms_bwd(q, gp_ref[...], dhv)
        ds = (dq * e * gate * (1.0 - gate)).astype(BF16)
        ds_ref[...] = ds
        de_ref[...] = (dq * gate).astype(BF16)
        dx, dgn = _rms_bwd(hv, gn_ref[...], _dot_nt(ds, wg_ref[...]))
        dhp_ref[...] = dhv + dx
        _accumulate(dgp_ref, dgp)
        _accumulate(dgn_ref, dgn)

    return pl.pallas_call(
        body, name=name, grid=(t // tm,),
        in_specs=[_rows(tm, d), _rows(tm, dp), _const((d, d)), _const((dp, d)), _rows(tm, d), _const((1, d)), _const((1, d)),
                  _rows(tm, d)],
        out_specs=[_rows(tm, d)] * 3 + [_const((1, d))] * 2 + [_const((8, 128))],
        out_shape=[jax.ShapeDtypeStruct((t, d), BF16), jax.ShapeDtypeStruct((t, d), BF16), jax.ShapeDtypeStruct((t, d), F32),
                   jax.ShapeDtypeStruct((1, d), F32), jax.ShapeDtypeStruct((1, d), F32), jax.ShapeDtypeStruct((8, 128), F32)],
        compiler_params=_params("arbitrary"))(xn, p, wpg, wpp, h, g_post, g_pre, target)


def ffn_bwd_a(dh, f, g_post, wd4, s4, t4, scale, name, deps=()):
    t, d = dh.shape
    nk, fk, _ = wd4.shape
    tm = _tile(t, 256)

    def body(dh_ref, f_ref, gp_ref, w_ref, s_ref, t_ref, df_ref, dg_ref, du_ref, dgp_ref):
        df, dgp = _rms_bwd(f_ref[...].astype(F32), gp_ref[...], dh_ref[...])
        df = (scale * df).astype(BF16)
        df_ref[...] = df
        _accumulate(dgp_ref, scale * dgp)
        for k in range(nk):
            da = _dot_nt(df, w_ref[k])
            du_ref[k] = (da * s_ref[k].astype(F32)).astype(BF16)
            dg_ref[k] = (da * t_ref[k].astype(F32)).astype(BF16)

    return pl.pallas_call(
        _ordered_after(body, 6, deps), name=name, grid=(t // tm,),
        in_specs=[_rows(tm, d), _rows(tm, d), _const((1, d)), _const((nk, fk, d)), _kmajor(nk, tm, fk), _kmajor(nk, tm, fk)]
        + [ANY_SPEC] * len(deps),
        out_specs=[_rows(tm, d), _kmajor(nk, tm, fk), _kmajor(nk, tm, fk), _const((1, d))],
        out_shape=[jax.ShapeDtypeStruct((t, d), BF16), jax.ShapeDtypeStruct((nk, t, fk), BF16),
                   jax.ShapeDtypeStruct((nk, t, fk), BF16), jax.ShapeDtypeStruct((1, d), F32)],
        compiler_params=_params("arbitrary"))(dh, f, g_post, wd4, s4, t4, *deps)


def dx_norm_bwd(pairs, h, g_pre, dh_in, name, deps=()):
    t, d = h.shape
    tm = _tile(t, 256)
    n = len(pairs)

    def body(*refs):
        dys, ws = refs[:n], refs[n:2 * n]
        h_ref, g_ref, dhi_ref, dho_ref, dg_ref = refs[2 * n:]
        acc = None
        for (_, w4, sections), dy_ref, w_ref in zip(pairs, dys, ws):
            if sections:
                wide = w4.shape[2]
                edges = sorted(set(range(0, 5 * d + 1, d)) | set(range(0, 5 * d + 1, wide)))
                parts = [_dot_nt(dy_ref[DZ_SLOT[lo // d], :, lo % d:lo % d + hi - lo], w_ref[lo // wide, :, lo % wide:lo % wide + hi - lo])
                         for lo, hi in zip(edges[:-1], edges[1:])]
            else:
                parts = [_dot(dy_ref[k], w_ref[k]) for k in range(w4.shape[0])]
            for part in parts:
                acc = part if acc is None else acc + part
        dx, dg = _rms_bwd(h_ref[...], g_ref[...], acc)
        dho_ref[...] = dhi_ref[...] + dx
        _accumulate(dg_ref, dg)

    dy_specs = [_kmajor(dy.shape[0], tm, dy.shape[2]) for dy, _, _ in pairs]
    return pl.pallas_call(
        _ordered_after(body, 2 * n + 3, deps), name=name, grid=(t // tm,),
        in_specs=dy_specs + [_const(w4.shape) for _, w4, _ in pairs] + [_rows(tm, d), _const((1, d)), _rows(tm, d)]
        + [ANY_SPEC] * len(deps),
        out_specs=[_rows(tm, d), _const((1, d))],
        out_shape=[jax.ShapeDtypeStruct((t, d), F32), jax.ShapeDtypeStruct((1, d), F32)],
        compiler_params=_params("arbitrary"))(*[dy for dy, _, _ in pairs], *[w4 for _, w4, _ in pairs], h, g_pre, dh_in, *deps)


def mixer_bwd_y(dh, m, g_post, w_o, ya, yb, z, woa, wob, name, deps=()):
    t, d = dh.shape
    tm = _tile(t, 256)

    def body(dh_ref, m_ref, gp_ref, wo_ref, ya_ref, yb_ref, ga_ref, gb_ref, wa_ref, wb_ref,
             dm_ref, dya_ref, dyb_ref, dz_ref, da_ref, db_ref, dgp_ref):
        dm, dgp = _rms_bwd(m_ref[...].astype(F32), gp_ref[...], dh_ref[...])
        dm = dm.astype(BF16)
        dm_ref[...] = dm
        _accumulate(dgp_ref, dgp)
        dy = _dot_nt(dm, wo_ref[...])
        sa = _sigmoid(ga_ref[...].astype(F32))
        sb = _sigmoid(gb_ref[...].astype(F32))
        dya = (dy * sa).astype(BF16)
        dyb = (dy * sb).astype(BF16)
        dya_ref[...] = dya
        dyb_ref[...] = dyb
        dz_ref[0] = (dy * ya_ref[...].astype(F32) * sa * (1.0 - sa)).astype(BF16)
        dz_ref[1] = (dy * yb_ref[...].astype(F32) * sb * (1.0 - sb)).astype(BF16)
        da_ref[...] = _dot_nt(dya, wa_ref[...]).astype(BF16)
        db_ref[...] = _dot_nt(dyb, wb_ref[...]).astype(BF16)

    return pl.pallas_call(
        _ordered_after(body, 10, deps), name=name, grid=(t // tm,),
        in_specs=[_rows(tm, d), _rows(tm, d), _const((1, d)), _const((d, d)), _rows(tm, d), _rows(tm, d),
                  _rows(tm, d, 3), _rows(tm, d, 4), _const((d, d)), _const((d, d))] + [ANY_SPEC] * len(deps),
        out_specs=[_rows(tm, d)] * 3 + [pl.BlockSpec((2, tm, d), lambda i: (0, i, 0))] + [_rows(tm, d)] * 2 + [_const((1, d))],
        out_shape=[jax.ShapeDtypeStruct((t, d), BF16)] * 3 + [jax.ShapeDtypeStruct((5, t, d), BF16)]
        + [jax.ShapeDtypeStruct((t, d), BF16)] * 2 + [jax.ShapeDtypeStruct((1, d), F32)],
        compiler_params=_params("arbitrary"))(dh, m, g_post, w_o, ya, yb, z, z, woa, wob, *deps)


def sgu_bwd(z, da, dz, norm_g, sgu_w, sgu_b3, name):
    t, d = da.shape
    ng = sgu_w.shape[0]
    dg = d // ng
    tm = _tile(t, 256)
    steps = t // tm

    def body(zu_ref, zv_ref, da_ref, ng_ref, w_ref, b_ref, _, dz_ref, dw_ref, db_ref, dng_ref, dvn_ref, dsv_ref):
        i = pl.program_id(0)
        zv = zv_ref[...].astype(F32)
        zu = zu_ref[...].astype(F32)
        v, gv = _gelu_and_grad(zv)
        vhat, r = _layernorm_parts(v)
        gain = ng_ref[...]
        vn = (vhat * gain).astype(BF16)
        u, gu = _gelu_and_grad(zu)
        dav = da_ref[...].astype(F32)
        mask = _causal_mask()

        @pl.when(i == 0)
        def _():
            dw_ref[...] = jnp.zeros_like(dw_ref)
            dsv_ref[...] = jnp.zeros_like(dsv_ref)

        for g in range(ng):
            wg = jnp.where(mask, w_ref[g], 0.0).astype(BF16)
            dw = jnp.zeros((CHUNK, CHUNK), F32)
            dsv_sum = jnp.zeros((CHUNK, dg), F32)
            for ci in range(tm // CHUNK):
                rs, cs = slice(ci * CHUNK, (ci + 1) * CHUNK), slice(g * dg, (g + 1) * dg)
                vn_blk = vn[rs, cs]
                sv = _dot(wg, vn_blk) + b_ref[g]
                dz_ref[0, rs, cs] = (dav[rs, cs] * sv * gu[rs, cs]).astype(BF16)
                dsv = dav[rs, cs] * u[rs, cs]
                dsv_sum += dsv
                dsv = dsv.astype(BF16)
                dw += _dot_nt(dsv, vn_blk)
                dvn_ref[rs, cs] = _dot_tn(wg, dsv)
            dw_ref[g] += dw
            dsv_ref[:, cs] += dsv_sum

        dvn = dvn_ref[...]
        _accumulate(dng_ref, jnp.sum(dvn * vhat, axis=0, keepdims=True))
        dvh = dvn * gain
        dv = r * (dvh - jnp.mean(dvh, axis=-1, keepdims=True) - vhat * jnp.mean(dvh * vhat, axis=-1, keepdims=True))
        dz_ref[1] = (dv * gv).astype(BF16)

        @pl.when(i == steps - 1)
        def _():
            for g in range(ng):
                dw_ref[g] = jnp.where(mask, dw_ref[g], 0.0)
                row_sum = jnp.sum(dsv_ref[:, g * dg:(g + 1) * dg], axis=1, keepdims=True)
                db_ref[g] = jnp.broadcast_to(row_sum, (CHUNK, CHUNK))

    return pl.pallas_call(
        body, name=name, grid=(steps,),
        in_specs=[_rows(tm, d, 0), _rows(tm, d, 1), _rows(tm, d), _const((1, d)), _const((ng, CHUNK, CHUNK)), _const((ng, CHUNK, 1)),
                  ANY_SPEC],
        out_specs=[pl.BlockSpec((2, tm, d), lambda i: (DZ_SLOT[0] // 2, i, 0)), _const((ng, CHUNK, CHUNK)),
                   _const((ng, CHUNK, CHUNK)), _const((1, d))],
        out_shape=[jax.ShapeDtypeStruct(dz.shape, BF16), jax.ShapeDtypeStruct((ng, CHUNK, CHUNK), F32),
                   jax.ShapeDtypeStruct((ng, CHUNK, CHUNK), F32), jax.ShapeDtypeStruct((1, d), F32)],
        scratch_shapes=[pltpu.VMEM((tm, d), F32), pltpu.VMEM((CHUNK, d), F32)], input_output_aliases={6: 0},
        compiler_params=_params("arbitrary"))(z, z, da, norm_g, sgu_w, sgu_b3, dz)


def pool_bwd(db, diff, dz, pool_w, pool_scale, name):
    t, d = db.shape
    ng = pool_w.shape[0]
    dg = d // ng
    tm = _tile(t, 256)
    per = tm // HALO
    steps = t // tm

    def body(db_ref, next_ref, diff_ref, w_ref, s_ref, _, dc_ref, dw_ref, ds_ref):
        i = pl.program_id(0)
        dbc = db_ref[...].astype(F32)
        nxt = jnp.where(i < steps - 1, next_ref[...].astype(F32), 0.0)
        ext = jnp.concatenate([dbc, nxt], axis=0)
        rows = tm + HALO
        tok = i * tm + lax.broadcasted_iota(jnp.int32, (rows, 1), 0)

        @pl.when(i == 0)
        def _():
            dw_ref[...] = jnp.zeros_like(dw_ref)
            ds_ref[...] = jnp.zeros_like(ds_ref)

        for g, win in enumerate(POOL_WINDOWS):
            cs = slice(g * dg, (g + 1) * dg)
            dp = (ext[:, cs] * s_ref[:, cs]).astype(BF16)
            dd = _dot_nt(dp, w_ref[g])
            s = dd * (1.0 / jnp.minimum(tok + 1, win).astype(F32))
            sh = 1
            while sh < win:
                s = s + pltpu.roll(s, rows - sh, 0)
                sh *= 2
            dc_ref[0, :, cs] = (s[:tm] - dd[:tm]).astype(BF16)
            dfg = diff_ref[:, cs]
            ds_ref[:, cs] += jnp.sum(dbc[:, cs] * _dot(dfg, w_ref[g]), axis=0, keepdims=True)
            dw_ref[g] += _dot_tn(dfg, dp[:tm])

    return pl.pallas_call(
        body, name=name, grid=(steps,),
        in_specs=[_rows(tm, d), pl.BlockSpec((HALO, d), lambda i: (jnp.minimum((i + 1) * per, t // HALO - 1), 0)),
                  _rows(tm, d), _const((ng, dg, dg)), _const((1, d)), ANY_SPEC],
        out_specs=[pl.BlockSpec((1, tm, d), lambda i: (DZ_SLOT[2], i, 0)), _const((ng, dg, dg)), _const((1, d))],
        out_shape=[jax.ShapeDtypeStruct(dz.shape, BF16), jax.ShapeDtypeStruct((ng, dg, dg), F32), jax.ShapeDtypeStruct((1, d), F32)],
        input_output_aliases={5: 0},
        compiler_params=_params("arbitrary"))(db, db, diff, pool_w, pool_scale, dz)


def dw_tn(x, dy, nk, name, x_kmajor=False, dy_mode="same"):
    t = x.shape[-2]
    kx = x.shape[-1]
    n = dy.shape[-1] // nk if dy_mode == "cols" else dy.shape[-1]
    tt = _tile(t, DW_TOKENS)
    steps = t // tt

    def body(x_ref, dy_ref, o_ref, acc_ref):
        s = pl.program_id(1)
        part = _dot_tn(x_ref[0] if x_kmajor else x_ref[...], dy_ref[0] if dy_mode == "kmajor" else dy_ref[...])
        if steps == 1:
            o_ref[0] = part.astype(BF16)
            return

        @pl.when(s == 0)
        def _():
            acc_ref[...] = jnp.zeros_like(acc_ref)

        acc_ref[...] += part

        @pl.when(s == steps - 1)
        def _():
            o_ref[0] = acc_ref[...].astype(BF16)

    x_spec = pl.BlockSpec((1, tt, kx), lambda k, s: (k, s, 0)) if x_kmajor else pl.BlockSpec((tt, kx), lambda k, s: (s, 0))
    dy_spec = {"kmajor": pl.BlockSpec((1, tt, n), lambda k, s: (k, s, 0)), "cols": pl.BlockSpec((tt, n), lambda k, s: (s, k)),
               "same": pl.BlockSpec((tt, n), lambda k, s: (s, 0))}[dy_mode]
    return pl.pallas_call(
        body, name=name, grid=(nk, steps), in_specs=[x_spec, dy_spec],
        out_specs=pl.BlockSpec((1, kx, n), lambda k, s: (k, 0, 0)), out_shape=jax.ShapeDtypeStruct((nk, kx, n), BF16),
        scratch_shapes=[pltpu.VMEM((kx, n) if steps > 1 else (8, 128), F32)],
        compiler_params=_params("parallel", "arbitrary"))(x, dy)


def dw_in_tiles(xn, dz, nk, name):
    t, d = xn.shape
    sections = len(DZ_SLOT)
    per_section = d // DW_IN_TILE
    per_shard = sections * per_section // nk

    def body(x_ref, dy_ref, o_ref):
        o_ref[0] = _dot_tn(x_ref[...], dy_ref[0]).astype(BF16)

    def slot(j):
        return (j // per_section + DZ_SLOT[0]) % sections

    return pl.pallas_call(
        body, name=name, grid=(sections * per_section,),
        in_specs=[pl.BlockSpec((t, d), lambda j: (0, 0)), pl.BlockSpec((1, t, DW_IN_TILE), lambda j: (slot(j), 0, j % per_section))],
        out_specs=pl.BlockSpec((1, d, DW_IN_TILE), lambda j: (j // per_shard, 0, j % per_shard)),
        out_shape=jax.ShapeDtypeStruct((nk, d, sections * d // nk), BF16), compiler_params=_params("parallel"))(xn, dz)


def _place():
    x, y, c = lax.axis_index("x"), lax.axis_index("y"), lax.axis_index("c")
    chips = [((1 - x) if fx else x, (1 - y) if fy else y) for fx, fy in OTHER_CHIPS]
    return x, y, c, chips


def _sibling_handshake(x, y, c):
    barrier = pltpu.get_barrier_semaphore()
    pl.semaphore_signal(barrier, inc=1, device_id=(x, y, 1 - c), device_id_type=MESH_IDS)
    pl.semaphore_wait(barrier, 1)


def _half(rows, which):
    return pl.ds(pl.multiple_of(which * (rows // 2), 16), rows // 2)


def _hbm(a):
    return pltpu.with_memory_space_constraint(a, pltpu.HBM)


def _by_shape(arrays):
    buckets = {}
    for i, a in enumerate(arrays):
        buckets.setdefault(a.shape, []).append(i)
    return list(buckets.values())


def cast_place(shards, chip, name, deps=(), plain=False):
    n = len(shards)
    r, cdim = shards[0].shape
    tr = _tile(r, 256)

    def body(chip_ref, *refs):
        outs = refs[n + len(deps):]
        for a, w_ref in enumerate(refs[:n]):
            cast = w_ref[...].astype(BF16)
            outs[a][0] = cast
            if plain:
                outs[n + a][...] = cast

    out = pl.pallas_call(
        body, name=name,
        grid_spec=pltpu.PrefetchScalarGridSpec(
            num_scalar_prefetch=1, grid=(r // tr,),
            in_specs=[pl.BlockSpec((tr, cdim), lambda i, chip_ref: (i, 0))] * n + [ANY_SPEC] * len(deps),
            out_specs=[pl.BlockSpec((1, tr, cdim), lambda i, chip_ref: (chip_ref[0], i, 0))] * n
            + [pl.BlockSpec((tr, cdim), lambda i, chip_ref: (i, 0))] * (n * plain)),
        out_shape=[jax.ShapeDtypeStruct((N_CHIPS, r, cdim), BF16)] * n + [jax.ShapeDtypeStruct((r, cdim), BF16)] * (n * plain),
        compiler_params=_params("parallel"))(chip, *shards, *deps)
    return (out[:n], out[n:]) if plain else out


def _gather_copy(buf, sends, recvs, i, j, me, chip_xy, c):
    cx, cy = chip_xy
    mine = _half(buf.shape[1], c)
    return pltpu.make_async_remote_copy(
        src_ref=buf.at[me, mine], dst_ref=buf.at[me, mine], send_sem=sends.at[3 * i + j], recv_sem=recvs.at[3 * i + j],
        device_id=(cx, cy, c), device_id_type=MESH_IDS)


def allgather_start(bufs, name):
    n = len(bufs)

    def body(*refs):
        ins = refs[:n]
        sends, recvs = refs[n], refs[n + 1]
        token = refs[2 * n + 2]
        x, y, c, chips = _place()
        for i in range(n):
            for j, chip_xy in enumerate(chips):
                _gather_copy(ins[i], sends, recvs, i, j, 2 * x + y, chip_xy, c).start()
        token[...] = jnp.zeros_like(token)

    out = pl.pallas_call(
        body, name=name, in_specs=[HBM_SPEC] * n,
        out_specs=[SEM_SPEC, SEM_SPEC] + [HBM_SPEC] * n + [VMEM_SPEC],
        out_shape=[pltpu.SemaphoreType.DMA((3 * n,)), pltpu.SemaphoreType.DMA((3 * n,))]
        + [pltpu.HBM(b.shape, b.dtype) for b in bufs] + [jax.ShapeDtypeStruct((8, 128), F32)],
        input_output_aliases={i: i + 2 for i in range(n)},
        compiler_params=pltpu.CompilerParams(has_side_effects=DATAFLOW))(*[_hbm(b) for b in bufs])
    return out[0], out[1], list(out[2:2 + n]), out[2 + n]


def allgather_wait(sends, recvs, bufs, after, name):
    n = len(bufs)

    def body(*refs):
        ins = refs[:n]
        send_sems, recv_sems = refs[n], refs[n + 1]
        x, y, c, chips = _place()
        for i in range(n):
            for j, (cx, cy) in enumerate(chips):
                mine = _half(ins[i].shape[1], c)
                cp = pltpu.make_async_remote_copy(
                    src_ref=ins[i].at[2 * x + y, mine], dst_ref=ins[i].at[2 * cx + cy, mine], send_sem=send_sems.at[3 * i + j],
                    recv_sem=recv_sems.at[3 * i + j], device_id=(cx, cy, c), device_id_type=MESH_IDS)
                cp.wait_send()
                cp.wait_recv()

    return pl.pallas_call(
        body, name=name, in_specs=[HBM_SPEC] * n + [SEM_SPEC, SEM_SPEC, ANY_SPEC], out_specs=[HBM_SPEC] * n,
        out_shape=[pltpu.HBM(b.shape, b.dtype) for b in bufs], input_output_aliases={i: i for i in range(n)},
        compiler_params=pltpu.CompilerParams(has_side_effects=DATAFLOW))(*bufs, sends, recvs, after)


def d2d_forward(bufs, name):
    n = len(bufs)

    def body(*refs):
        ins = refs[:n]
        send_sems, recv_sems = refs[2 * n:]
        x, y, c, chips = _place()
        copies = []
        for i in range(n):
            mine = _half(ins[i].shape[1], c)
            for j, (cx, cy) in enumerate(chips):
                landed = ins[i].at[2 * cx + cy, mine]
                cp = pltpu.make_async_remote_copy(
                    src_ref=landed, dst_ref=landed, send_sem=send_sems.at[i, j], recv_sem=recv_sems.at[i, j],
                    device_id=(x, y, 1 - c), device_id_type=MESH_IDS)
                cp.start()
                copies.append(cp)
        for i in range(n):
            theirs = _half(ins[i].shape[1], 1 - c)
            for j, (cx, cy) in enumerate(chips):
                passed = ins[i].at[2 * cx + cy, theirs]
                pltpu.make_async_remote_copy(
                    src_ref=passed, dst_ref=passed, send_sem=send_sems.at[i, j], recv_sem=recv_sems.at[i, j],
                    device_id=(x, y, 1 - c), device_id_type=MESH_IDS).wait_recv()
        for cp in copies:
            cp.wait_send()

    return pl.pallas_call(
        body, name=name, in_specs=[HBM_SPEC] * n, out_specs=[HBM_SPEC] * n,
        out_shape=[jax.ShapeDtypeStruct(b.shape, b.dtype) for b in bufs], input_output_aliases={i: i for i in range(n)},
        scratch_shapes=[pltpu.SemaphoreType.DMA((n, 3))] * 2,
        compiler_params=pltpu.CompilerParams(has_side_effects=True))(*bufs)


def _forward_copy(buf, sends, recvs, i, j, chip_xy, x, y, c):
    cx, cy = chip_xy
    rows = buf.shape[1]
    return pltpu.make_async_remote_copy(
        src_ref=buf.at[2 * cx + cy, _half(rows, c)], dst_ref=buf.at[2 * cx + cy, _half(rows, 1 - c)],
        send_sem=sends.at[3 * i + j], recv_sem=recvs.at[3 * i + j], device_id=(x, y, 1 - c), device_id_type=MESH_IDS)


def d2d_forward_start(bufs, name):
    n = len(bufs)

    def body(*refs):
        ins = refs[:n]
        sends, recvs = refs[n], refs[n + 1]
        token = refs[2 * n + 2]
        x, y, c, chips = _place()
        _sibling_handshake(x, y, c)
        for i in range(n):
            mine = _half(ins[i].shape[1], c)
            for j, (cx, cy) in enumerate(chips):
                landed = ins[i].at[2 * cx + cy, mine]
                pltpu.make_async_remote_copy(
                    src_ref=landed, dst_ref=landed, send_sem=sends.at[3 * i + j], recv_sem=recvs.at[3 * i + j],
                    device_id=(x, y, 1 - c), device_id_type=MESH_IDS).start()
        token[...] = jnp.zeros_like(token)

    out = pl.pallas_call(
        body, name=name, in_specs=[HBM_SPEC] * n,
        out_specs=[SEM_SPEC, SEM_SPEC] + [HBM_SPEC] * n + [VMEM_SPEC],
        out_shape=[pltpu.SemaphoreType.DMA((3 * n,)), pltpu.SemaphoreType.DMA((3 * n,))]
        + [pltpu.HBM(b.shape, b.dtype) for b in bufs] + [jax.ShapeDtypeStruct((8, 128), F32)],
        input_output_aliases={i: i + 2 for i in range(n)},
        compiler_params=pltpu.CompilerParams(has_side_effects=DATAFLOW, collective_id=SIBLING_BARRIER))(*[_hbm(b) for b in bufs])
    return (out[0], out[1], list(out[2:2 + n])), out[2 + n]


def d2d_forward_wait(started, after, name):
    sends, recvs, bufs = started
    n = len(bufs)

    def body(*refs):
        ins = refs[:n]
        send_sems, recv_sems = refs[n], refs[n + 1]
        x, y, c, chips = _place()
        for i in range(n):
            for j, chip_xy in enumerate(chips):
                cp = _forward_copy(ins[i], send_sems, recv_sems, i, j, chip_xy, x, y, c)
                cp.wait_send()
                cp.wait_recv()

    return pl.pallas_call(
        body, name=name, in_specs=[HBM_SPEC] * n + [SEM_SPEC, SEM_SPEC, ANY_SPEC], out_specs=[HBM_SPEC] * n,
        out_shape=[pltpu.HBM(b.shape, b.dtype) for b in bufs], input_output_aliases={i: i for i in range(n)},
        compiler_params=pltpu.CompilerParams(has_side_effects=DATAFLOW))(*bufs, sends, recvs, after)


def _sibling_copy(src, land, sends, recvs, i, x, y, c, halves):
    part = src.at[:, _half(src.shape[1], 1 - c)] if halves else src
    return pltpu.make_async_remote_copy(
        src_ref=part, dst_ref=land, send_sem=sends.at[i], recv_sem=recvs.at[i], device_id=(x, y, 1 - c),
        device_id_type=MESH_IDS)


def sibling_start(arrays, halves, name):
    n = len(arrays)
    lands = [lax.empty((a.shape[0], a.shape[1] // 2, a.shape[2]) if halves else a.shape, a.dtype) for a in arrays]

    def body(*refs):
        srcs, zones = refs[:n], refs[n:2 * n]
        sends, recvs = refs[2 * n], refs[2 * n + 1]
        token = refs[4 * n + 2]
        x, y, c, _ = _place()
        _sibling_handshake(x, y, c)
        for i in range(n):
            _sibling_copy(srcs[i], zones[i], sends, recvs, i, x, y, c, halves).start()
        token[...] = jnp.zeros_like(token)

    out = pl.pallas_call(
        body, name=name, in_specs=[HBM_SPEC] * (2 * n),
        out_specs=[SEM_SPEC, SEM_SPEC] + [HBM_SPEC] * (2 * n) + [VMEM_SPEC],
        out_shape=[pltpu.SemaphoreType.DMA((n,)), pltpu.SemaphoreType.DMA((n,))]
        + [pltpu.HBM(a.shape, a.dtype) for a in arrays + lands] + [jax.ShapeDtypeStruct((8, 128), F32)],
        input_output_aliases={i: i + 2 for i in range(2 * n)},
        compiler_params=pltpu.CompilerParams(has_side_effects=DATAFLOW, collective_id=SIBLING_BARRIER))(
            *[_hbm(a) for a in arrays + lands])
    return (out[0], out[1], list(out[2:2 + n]), list(out[2 + n:2 + 2 * n])), out[2 + 2 * n]


def sibling_wait(started, halves, after, name):
    sends, recvs, arrays, lands = started
    n = len(arrays)
    after = tuple(after) if isinstance(after, (tuple, list)) else (after,)

    def body(*refs):
        srcs, zones = refs[:n], refs[n:2 * n]
        send_sems, recv_sems = refs[2 * n], refs[2 * n + 1]
        x, y, c, _ = _place()
        for i in range(n):
            cp = _sibling_copy(srcs[i], zones[i], send_sems, recv_sems, i, x, y, c, halves)
            cp.wait_send()
            cp.wait_recv()

    out = pl.pallas_call(
        body, name=name, in_specs=[HBM_SPEC] * (2 * n) + [SEM_SPEC, SEM_SPEC] + [ANY_SPEC] * len(after),
        out_specs=[HBM_SPEC] * (2 * n),
        out_shape=[pltpu.HBM(a.shape, a.dtype) for a in arrays + lands], input_output_aliases={i: i for i in range(2 * n)},
        compiler_params=pltpu.CompilerParams(has_side_effects=DATAFLOW))(*arrays, *lands, sends, recvs, *after)
    return list(out[:n]), list(out[n:])


def _scatter_copy(src, land, sends, recvs, i, j, chip_xy, c):
    cx, cy = chip_xy
    return pltpu.make_async_remote_copy(
        src_ref=src.at[2 * cx + cy], dst_ref=land.at[j], send_sem=sends.at[3 * i + j], recv_sem=recvs.at[3 * i + j],
        device_id=(cx, cy, c), device_id_type=MESH_IDS)


def scatter_start(sums, name):
    n = len(sums)
    lands = [lax.empty((3,) + s.shape[1:], s.dtype) for s in sums]

    def body(*refs):
        srcs, zones = refs[:n], refs[n:2 * n]
        sends, recvs = refs[2 * n], refs[2 * n + 1]
        token = refs[4 * n + 2]
        _, _, c, chips = _place()
        for i in range(n):
            for j, chip_xy in enumerate(chips):
                _scatter_copy(srcs[i], zones[i], sends, recvs, i, j, chip_xy, c).start()
        token[...] = jnp.zeros_like(token)

    out = pl.pallas_call(
        body, name=name, in_specs=[HBM_SPEC] * (2 * n),
        out_specs=[SEM_SPEC, SEM_SPEC] + [HBM_SPEC] * (2 * n) + [VMEM_SPEC],
        out_shape=[pltpu.SemaphoreType.DMA((3 * n,)), pltpu.SemaphoreType.DMA((3 * n,))]
        + [pltpu.HBM(a.shape, a.dtype) for a in sums + lands] + [jax.ShapeDtypeStruct((8, 128), F32)],
        input_output_aliases={i: i + 2 for i in range(2 * n)},
        compiler_params=pltpu.CompilerParams(has_side_effects=DATAFLOW))(*[_hbm(a) for a in sums + lands])
    return out[0], out[1], list(out[2:2 + n]), list(out[2 + n:2 + 2 * n]), out[2 + 2 * n]


def scatter_wait(sends, recvs, sums, lands, after, name):
    n = len(sums)

    def body(*refs):
        srcs, zones = refs[:n], refs[n:2 * n]
        send_sems, recv_sems = refs[2 * n], refs[2 * n + 1]
        _, _, c, chips = _place()
        for i in range(n):
            for j, chip_xy in enumerate(chips):
                cp = _scatter_copy(srcs[i], zones[i], send_sems, recv_sems, i, j, chip_xy, c)
                cp.wait_send()
                cp.wait_recv()

    out = pl.pallas_call(
        body, name=name, in_specs=[HBM_SPEC] * (2 * n) + [SEM_SPEC, SEM_SPEC, ANY_SPEC], out_specs=[HBM_SPEC] * (2 * n),
        out_shape=[pltpu.HBM(a.shape, a.dtype) for a in sums + lands], input_output_aliases={i: i for i in range(2 * n)},
        compiler_params=pltpu.CompilerParams(has_side_effects=DATAFLOW))(*sums, *lands, sends, recvs, after)
    return list(out[:n]), list(out[n:])


def add_halves(grads, recvs, core, name):
    n = len(grads)
    nk, r, cdim = grads[0].shape
    half = r // 2
    views = [g.reshape(nk, 2, half, cdim) for g in grads]

    def body(core_ref, *refs):
        for g_ref, r_ref, o_ref in zip(refs[:n], refs[n:2 * n], refs[2 * n:]):
            o_ref[0] = (g_ref[0, 0].astype(F32) + r_ref[0].astype(F32)).astype(BF16)

    return pl.pallas_call(
        body, name=name,
        grid_spec=pltpu.PrefetchScalarGridSpec(
            num_scalar_prefetch=1, grid=(nk,),
            in_specs=[pl.BlockSpec((1, 1, half, cdim), lambda k, core_ref: (k, core_ref[0], 0, 0))] * n
            + [pl.BlockSpec((1, half, cdim), lambda k, core_ref: (k, 0, 0))] * n,
            out_specs=[pl.BlockSpec((1, half, cdim), lambda k, core_ref: (k, 0, 0))] * n),
        out_shape=[jax.ShapeDtypeStruct((nk, half, cdim), BF16)] * n, compiler_params=_params("parallel"))(core, *views, *recvs)


def add_chips(sums, lands, chip, name):
    n = len(sums)
    _, half, cdim = sums[0].shape
    tr = _tile(half, 128)

    def body(chip_ref, *refs):
        for s_ref, r_ref, o_ref in zip(refs[:n], refs[n:2 * n], refs[2 * n:]):
            o_ref[...] = ((s_ref[0].astype(F32) + r_ref[0].astype(F32)) + r_ref[1].astype(F32)) + r_ref[2].astype(F32)

    return pl.pallas_call(
        body, name=name,
        grid_spec=pltpu.PrefetchScalarGridSpec(
            num_scalar_prefetch=1, grid=(half // tr,),
            in_specs=[pl.BlockSpec((1, tr, cdim), lambda i, chip_ref: (chip_ref[0], i, 0))] * n
            + [pl.BlockSpec((3, tr, cdim), lambda i, chip_ref: (0, i, 0))] * n,
            out_specs=[pl.BlockSpec((tr, cdim), lambda i, chip_ref: (i, 0))] * n),
        out_shape=[jax.ShapeDtypeStruct((half, cdim), F32)] * n, compiler_params=_params("parallel"))(chip, *sums, *lands)


def _adamw_math(w, g, m, v):
    m = ADAM_B1 * m + (1.0 - ADAM_B1) * g
    v = ADAM_B2 * v + (1.0 - ADAM_B2) * (g * g)
    m_hat = m / (1.0 - ADAM_B1 ** ADAM_STEP)
    v_hat = v / (1.0 - ADAM_B2 ** ADAM_STEP)
    return -ADAM_LR * (m_hat / (jnp.sqrt(v_hat) + ADAM_EPS) + ADAM_WD * w), m, v


def adamw_one_half(ws, gs, ms, vs, core, name, own, into=None):
    n = len(ws)
    r, cdim = ws[0].shape
    half = r // 2
    tr = _tile(half, max(8, 256 // n))
    steps = half // tr
    prior = () if into is None else tuple(a for group in into for a in group)

    def body(core_ref, *refs):
        ins, outs = refs[:4 * n], refs[4 * n + len(prior):]
        for a in range(n):
            w_ref, g_ref, m_ref, v_ref = ins[a::n]
            g = g_ref[...]
            outs[a][...] = g
            outs[n + a][...], outs[2 * n + a][...], outs[3 * n + a][...] = _adamw_math(w_ref[...], g, m_ref[...], v_ref[...])

    def rows(i, core_ref):
        return ((core_ref[0] if own else 1 - core_ref[0]) * steps + i, 0)

    whole = pl.BlockSpec((tr, cdim), rows)
    part = pl.BlockSpec((tr, cdim), lambda i, core_ref: (i, 0))
    out = pl.pallas_call(
        body, name=name,
        grid_spec=pltpu.PrefetchScalarGridSpec(
            num_scalar_prefetch=1, grid=(steps,),
            in_specs=[whole] * n + [part] * n + [whole] * (2 * n) + [ANY_SPEC] * len(prior), out_specs=[whole] * (4 * n)),
        out_shape=[jax.ShapeDtypeStruct((r, cdim), F32)] * (4 * n),
        input_output_aliases={1 + 4 * n + j: j for j in range(len(prior))},
        compiler_params=_params("parallel"))(core, *ws, *gs, *ms, *vs, *prior)
    return [out[k * n:(k + 1) * n] for k in range(4)]


def adamw_halves_sparsecore(ws, owns, others, ms, vs, name):
    n = len(ws)
    r, cdim = ws[0].shape
    half_groups = r // 2 // SC_ROWS
    per_tile = -(-2 * half_groups // SC_TILES)

    def body(*refs):
        ins, outs, (wb, gb, mb, vb) = refs[:5 * n], refs[5 * n:9 * n], refs[9 * n:]
        tile = lax.axis_index("sc_tile") * 2 + lax.axis_index("sc_core")
        core = lax.axis_index("c")

        @pl.loop(0, per_tile)
        def _(it):
            group = tile + SC_TILES * it

            @pl.when(group < 2 * half_groups)
            def _():
                rows = pl.ds(group * SC_ROWS, SC_ROWS)
                in_own = (group < half_groups) == (core == 0)
                half_rows = pl.ds((group % half_groups) * SC_ROWS, SC_ROWS)
                for a in range(n):
                    w_hbm, own_hbm, other_hbm, m_hbm, v_hbm = ins[a::n]
                    g_out, d_out, m_out, v_out = outs[a::n]
                    pltpu.sync_copy(w_hbm.at[rows], wb)
                    pltpu.sync_copy(m_hbm.at[rows], mb)
                    pltpu.sync_copy(v_hbm.at[rows], vb)

                    @pl.when(in_own)
                    def _():
                        pltpu.sync_copy(own_hbm.at[half_rows], gb)

                    @pl.when(jnp.logical_not(in_own))
                    def _():
                        pltpu.sync_copy(other_hbm.at[half_rows], gb)

                    pltpu.sync_copy(gb, g_out.at[rows])

                    @pl.loop(0, SC_ROWS)
                    def _(row):
                        @pl.loop(0, cdim, step=SC_LANES)
                        def _(col):
                            at = (row, pl.ds(col, SC_LANES))
                            wb[at], mb[at], vb[at] = _adamw_math(wb[at], gb[at], mb[at], vb[at])

                    pltpu.sync_copy(wb, d_out.at[rows])
                    pltpu.sync_copy(mb, m_out.at[rows])
                    pltpu.sync_copy(vb, v_out.at[rows])

    out = pl.kernel(
        body, name=name, out_type=[jax.ShapeDtypeStruct((r, cdim), F32)] * (4 * n),
        mesh=plsc.VectorSubcoreMesh(core_axis_name="sc_core", subcore_axis_name="sc_tile"),
        scratch_types=[pltpu.VMEM((SC_ROWS, cdim), F32)] * 4)(*ws, *owns, *others, *ms, *vs)
    return [out[a::n] for a in range(n)]


def small_allreduce_adamw(g, w, m, v, deps=()):
    r, cdim = g.shape

    def body(g_ref, w_ref, m_ref, v_ref, go_ref, d_ref, mo_ref, vo_ref, pair, quad, d2d_send, d2d_recv, ici_send, ici_recv):
        x, y, c, chips = _place()
        me = 2 * x + y
        pair[c] = g_ref[...]
        swap = pltpu.make_async_remote_copy(
            src_ref=g_ref, dst_ref=pair.at[c], send_sem=d2d_send, recv_sem=d2d_recv, device_id=(x, y, 1 - c),
            device_id_type=MESH_IDS)
        swap.start()
        swap.wait()
        quad[me] = pair[0] + pair[1]
        copies = []
        for j, (cx, cy) in enumerate(chips):
            cp = pltpu.make_async_remote_copy(
                src_ref=quad.at[me], dst_ref=quad.at[me], send_sem=ici_send.at[j], recv_sem=ici_recv.at[j],
                device_id=(cx, cy, c), device_id_type=MESH_IDS)
            cp.start()
            copies.append(cp)
        for j, (cx, cy) in enumerate(chips):
            slot = quad.at[2 * cx + cy]
            pltpu.make_async_remote_copy(
                src_ref=slot, dst_ref=slot, send_sem=ici_send.at[j], recv_sem=ici_recv.at[j], device_id=(cx, cy, c),
                device_id_type=MESH_IDS).wait_recv()
        for cp in copies:
            cp.wait_send()
        total = (quad[0] + quad[1]) + (quad[2] + quad[3])
        go_ref[...] = total
        d_ref[...], mo_ref[...], vo_ref[...] = _adamw_math(w_ref[...], total, m_ref[...], v_ref[...])

    return pl.pallas_call(
        _ordered_after(body, 4, deps), name="small_allreduce_adamw", in_specs=[VMEM_SPEC] * 4 + [ANY_SPEC] * len(deps),
        out_specs=[VMEM_SPEC] * 4,
        out_shape=[jax.ShapeDtypeStruct((r, cdim), F32)] * 4,
        scratch_shapes=[pltpu.VMEM((2, r, cdim), F32), pltpu.VMEM((N_CHIPS, r, cdim), F32), pltpu.SemaphoreType.DMA,
                        pltpu.SemaphoreType.DMA, pltpu.SemaphoreType.DMA((3,)), pltpu.SemaphoreType.DMA((3,))],
        compiler_params=pltpu.CompilerParams(has_side_effects=True, vmem_limit_bytes=VMEM_LIMIT_V7X))(g, w, m, v, *deps)


GROUPS = (
    ("ffn1", ("ffn1_w_gate", "ffn1_w_up", "ffn1_w_down")),
    ("mixer", ("w_in", "pool_w", "w_out_a", "w_out_b", "w_o")),
    ("ffn2", ("ffn2_w_gate", "ffn2_w_up", "ffn2_w_down")),
    ("ple", ("ple_w_gate", "ple_w_proj")),
)
GATHERS = (
    ("ffn1_in", ("ffn1_w_gate", "ffn1_w_up")),
    ("ffn1_out", ("ffn1_w_down",)),
) + GROUPS[1:]
GAINS = ("ffn1_pre_g", "ffn1_post_g", "mix_pre_g", "sgu_norm_g", "pool_scale", "mix_post_g",
         "ffn2_pre_g", "ffn2_post_g", "ple_pre_g", "ple_post_g")
SMALL = GAINS + ("sgu_b", "sgu_w")
PACKED = SMALL + ("loss",)
WEIGHTS = ("ffn1_pre_g", "ffn1_w_gate", "ffn1_w_up", "ffn1_w_down", "ffn1_post_g", "mix_pre_g", "w_in", "sgu_norm_g",
           "sgu_w", "sgu_b", "pool_w", "pool_scale", "w_out_a", "w_out_b", "w_o", "mix_post_g", "ffn2_pre_g",
           "ffn2_w_gate", "ffn2_w_up", "ffn2_w_down", "ffn2_post_g", "ple_pre_g", "ple_w_gate", "ple_w_proj", "ple_post_g")
PACK_ROWS = 16


TRANSPOSED = ("ffn1_w_gate", "ffn1_w_up", "ffn2_w_gate", "ffn2_w_up")


def _shard2d(name, a):
    a = a[0]
    return a.T if name in TRANSPOSED else a.reshape(-1, a.shape[-1])


def _unshard2d(name, a2d, shape):
    return (a2d.T if name in TRANSPOSED else a2d).reshape(shape)


def pack_rows(gains, sgu_b, loss_tile, name):
    n = len(gains)
    d = gains[0].shape[1]
    g = sgu_b.shape[0]

    def body(*refs):
        o_ref = refs[-1]
        o_ref[...] = jnp.zeros_like(o_ref)
        for i in range(n):
            o_ref[i:i + 1, :] = refs[i][...]
        o_ref[n:n + g, 0:CHUNK] = refs[n][...]
        o_ref[n + g:n + g + 1, 0:CHUNK] = refs[n + 1][0:1, :]

    return pl.pallas_call(
        body, name=name, in_specs=[VMEM_SPEC] * (n + 2), out_specs=VMEM_SPEC,
        out_shape=jax.ShapeDtypeStruct((PACK_ROWS, d), F32))(*gains, sgu_b, loss_tile)


def _pack_small(parts, tag):
    d = parts[GAINS[0]].shape[-1]
    rows = pack_rows([parts[n] for n in GAINS], parts["sgu_b"].reshape(-1, CHUNK), parts["loss"], "pack_" + tag)
    return jnp.concatenate([rows, parts["sgu_w"].reshape(-1, d)], axis=0)


def _unpack_small(packed, like):
    n, g = len(GAINS), like["sgu_b"].size // CHUNK
    out = {name: packed[i:i + 1] for i, name in enumerate(GAINS)}
    out["sgu_b"] = packed[n:n + g, :CHUNK].reshape(like["sgu_b"].shape)
    out["loss"] = packed[n + g, 0]
    out["sgu_w"] = packed[PACK_ROWS:].reshape(like["sgu_w"].shape)
    return out


def _ffn_fwd(xn, h, w, pre, g_post, g_next, tag, between=None, first=None):
    a4, s4, t4 = ffn_gu(xn, w[pre + "w_gate"], w[pre + "w_up"], tag + "_gu") if first is None else first(xn)
    deps = ()
    if between is not None:
        more, deps = between(a4)
        w.update(more)
    f, h_new, xn_next = mm_norm_res(a4, w[pre + "w_down"], h, g_post, g_next, 0.5, tag + "_down", deps=deps)
    return dict(xn=xn, h=h, a4=a4, s4=s4, t4=t4, f=f), h_new, xn_next


def _ffn_bwd_w(dh, saved, w, pre, g_post, tag, deps):
    df, dg4, du4, d_post = ffn_bwd_a(dh, saved["f"], g_post, w[pre + "w_down"], saved["s4"], saved["t4"], 0.5, tag + "_bwd_a",
                                     deps=deps)
    return dg4, du4, _ffn_dw(saved, df, dg4, du4, pre, tag), d_post


def _ffn_dw(saved, df, dg4, du4, pre, tag):
    nk = N_CHIPS
    return {
        pre + "w_down": dw_tn(saved["a4"], df, nk, tag + "_dw_down", x_kmajor=True, dy_mode="same"),
        pre + "w_gate": dw_tn(dg4, saved["xn"], nk, tag + "_dw_gate", x_kmajor=True, dy_mode="same"),
        pre + "w_up": dw_tn(du4, saved["xn"], nk, tag + "_dw_up", x_kmajor=True, dy_mode="same"),
    }


def _ffn_bwd_x(dh, dg4, du4, saved, w, pre, g_pre, tag, deps):
    return dx_norm_bwd([(dg4, w[pre + "w_gate"], False), (du4, w[pre + "w_up"], False)], saved["h"], g_pre, dh,
                       tag + "_bwd_x", deps=deps)


def _gather_group(names, weights, chip, tag, deps, plain=False):
    shards = [_shard2d(n, weights[n]) for n in names]
    bufs = [None] * len(names)
    own = None
    for b, idx in enumerate(_by_shape(shards)):
        made = cast_place([shards[i] for i in idx], chip, f"cast_{tag}_{b}", deps=deps, plain=plain)
        if plain:
            made, own = made
        for i, buf in zip(idx, made):
            bufs[i] = buf
    sends, recvs, bufs, token = allgather_start(bufs, "allgather_start_" + tag)
    return ((sends, recvs, bufs), token, dict(zip(names, own))) if plain else ((sends, recvs, bufs), token)


def _gathered(started, names, after, tag):
    sends, recvs, bufs = started
    landed = allgather_wait(sends, recvs, bufs, after, "allgather_wait_" + tag)
    return dict(zip(names, d2d_forward(landed, "d2d_forward_" + tag)))


def _forward_early(started, after, tag):
    sends, recvs, bufs = started
    landed = allgather_wait(sends, recvs, bufs, after, "allgather_wait_" + tag)
    return d2d_forward_start(landed, "d2d_forward_start_" + tag)


def _forwarded(forwarding, names, after, tag):
    return dict(zip(names, d2d_forward_wait(forwarding, after, "d2d_forward_wait_" + tag)))


def _exchange_start(names, big_g, tag):
    return sibling_start([big_g[n] for n in names], True, "exchange_start_" + tag)


def _scatter_begin(names, exchanging, after, core, tag):
    partial, from_sibling = sibling_wait(exchanging, True, after, "exchange_wait_" + tag)
    chip_sums = [None] * len(names)
    for b, idx in enumerate(_by_shape(partial)):
        for i, s in zip(idx, add_halves([partial[i] for i in idx], [from_sibling[i] for i in idx], core, f"add_halves_{tag}_{b}")):
            chip_sums[i] = s
    sends, recvs, sums, lands, token = scatter_start(chip_sums, "scatter_start_" + tag)
    return (names, sends, recvs, sums, lands), token


def _share_begin(scattering, after, chip, tag):
    names, sends, recvs, sums, lands = scattering
    sums, lands = scatter_wait(sends, recvs, sums, lands, after, "scatter_wait_" + tag)
    reduced = [None] * len(names)
    for b, idx in enumerate(_by_shape(sums)):
        for i, r in zip(idx, add_chips([sums[i] for i in idx], [lands[i] for i in idx], chip, f"add_chips_{tag}_{b}")):
            reduced[i] = r
    sharing, token = sibling_start(reduced, False, "share_start_" + tag)
    return (names, sharing), token


def _update(shared, after, weights, moments_m, moments_v, tag, results):
    names, sharing = shared
    reduced, others = sibling_wait(sharing, False, after, "share_wait_" + tag)
    for b, idx in enumerate(_by_shape(reduced)):
        in_bucket = [names[i] for i in idx]
        outs = adamw_halves_sparsecore(
            [_shard2d(n, weights[n]) for n in in_bucket], [reduced[i] for i in idx], [others[i] for i in idx],
            [_shard2d(n, moments_m[n]) for n in in_bucket], [_shard2d(n, moments_v[n]) for n in in_bucket], f"adamw_{tag}_{b}")
        for n, per_weight in zip(in_bucket, outs):
            for store, value in zip(results, per_weight):
                store[n] = _unshard2d(n, value, weights[n].shape)


def _update_last(scattering, after, weights, moments_m, moments_v, core, chip, tag, results):
    names, sends, recvs, sums, lands = scattering
    sums, lands = scatter_wait(sends, recvs, sums, lands, after, "scatter_wait_" + tag)
    reduced = list(add_chips(sums, lands, chip, f"add_chips_{tag}_0"))
    (sends, recvs, reduced, zones), _ = sibling_start(reduced, False, "share_start_" + tag)
    state = [[_shard2d(n, held[n]) for n in names] for held in (weights, moments_m, moments_v)]
    first = adamw_one_half(state[0], reduced, state[1], state[2], core, f"adamw_{tag}_own", own=True)
    _, others = sibling_wait((sends, recvs, reduced, zones), False, first[1][-1], "share_wait_" + tag)
    outs = adamw_one_half(state[0], others, state[1], state[2], core, f"adamw_{tag}_other", own=False, into=first)
    for store, values in zip(results, outs):
        for n, value in zip(names, values):
            store[n] = _unshard2d(n, value, weights[n].shape)


def kernel(x, p, ffn1_pre_g, ffn1_w_gate, ffn1_w_up, ffn1_w_down, ffn1_post_g, mix_pre_g, w_in, sgu_norm_g, sgu_w, sgu_b, pool_w, pool_scale, w_out_a, w_out_b, w_o, mix_post_g, ffn2_pre_g, ffn2_w_gate, ffn2_w_up, ffn2_w_down, ffn2_post_g, ple_pre_g, ple_w_gate, ple_w_proj, ple_post_g, loss_target, m_ffn1_pre_g, m_ffn1_w_gate, m_ffn1_w_up, m_ffn1_w_down, m_ffn1_post_g, m_mix_pre_g, m_w_in, m_sgu_norm_g, m_sgu_w, m_sgu_b, m_pool_w, m_pool_scale, m_w_out_a, m_w_out_b, m_w_o, m_mix_post_g, m_ffn2_pre_g, m_ffn2_w_gate, m_ffn2_w_up, m_ffn2_w_down, m_ffn2_post_g, m_ple_pre_g, m_ple_w_gate, m_ple_w_proj, m_ple_post_g, v_ffn1_pre_g, v_ffn1_w_gate, v_ffn1_w_up, v_ffn1_w_down, v_ffn1_post_g, v_mix_pre_g, v_w_in, v_sgu_norm_g, v_sgu_w, v_sgu_b, v_pool_w, v_pool_scale, v_w_out_a, v_w_out_b, v_w_o, v_mix_post_g, v_ffn2_pre_g, v_ffn2_w_gate, v_ffn2_w_up, v_ffn2_w_down, v_ffn2_post_g, v_ple_pre_g, v_ple_w_gate, v_ple_w_proj, v_ple_post_g):
    given = dict(locals())
    weights = {n: given[n] for n in WEIGHTS}
    moments_m = {n: given["m_" + n] for n in WEIGHTS}
    moments_v = {n: given["v_" + n] for n in WEIGHTS}
    core = lax.axis_index("c").astype(jnp.int32).reshape(1)
    chip = (2 * lax.axis_index("x") + lax.axis_index("y")).astype(jnp.int32).reshape(1)

    d = x.shape[-1]
    token = ()
    gathering = {}
    for tag, names in GATHERS:
        if tag == GATHERS[0][0]:
            gathering[tag], tok, own_first = _gather_group(names, weights, chip, tag, token, plain=True)
        else:
            gathering[tag], tok = _gather_group(names, weights, chip, tag, token)
        token = (tok,)
    gain = {n: weights[n] for n in GAINS}
    sgu_w3 = sgu_w[0]
    sgu_b3 = sgu_b[0][:, :, None]
    groups = dict(GROUPS + GATHERS)

    h0 = x[0]
    tgt = loss_target[0]
    p_bf = p[0, 0].astype(BF16)
    xn1 = rms_cast(h0, gain["ffn1_pre_g"], "ffn1_pre_norm")
    w = {}

    def ffn1_first(xn):
        own = ffn_gu(xn, own_first["ffn1_w_gate"][None], own_first["ffn1_w_up"][None], "ffn1_gu_own", chip=chip)
        w.update(_gathered(gathering["ffn1_in"], groups["ffn1_in"], own[0], "ffn1_in"))
        return ffn_gu(xn, w["ffn1_w_gate"], w["ffn1_w_up"], "ffn1_gu", chip=chip, into=own)

    s1, h1, xn2 = _ffn_fwd(xn1, h0, w, "ffn1_", gain["ffn1_post_g"], gain["mix_pre_g"], "ffn1",
                           lambda a4: (_gathered(gathering["ffn1_out"], groups["ffn1_out"], a4, "ffn1_out"), ()), ffn1_first)
    w.update(_gathered(gathering["mixer"], groups["mixer"], h1, "mixer"))
    full = {n: w[n].reshape(-1, d) for n in ("w_out_a", "w_out_b", "w_o")}
    n_groups = pool_w.shape[1]
    rows_per = pool_w.shape[2]
    dgp = pool_w.shape[3]
    pool_full = w["pool_w"].reshape(N_CHIPS, n_groups, rows_per, dgp).transpose(1, 0, 2, 3).reshape(n_groups, N_CHIPS * rows_per, dgp)
    z = mixer_in(xn2, w["w_in"], "mixer_in")
    a = sgu_fwd(z, gain["sgu_norm_g"], sgu_w3, sgu_b3, "sgu_fwd")
    diff, b = pool_fwd(z, pool_full, gain["pool_scale"], "pool_fwd")
    ya, yb, y = mixer_y(a, b, z, full["w_out_a"], full["w_out_b"], "mixer_y")
    forwarding, tok = _forward_early(gathering["ffn2"], y, "ffn2")
    m, h2, xn3 = mm_norm_res(y[None], full["w_o"][None], h1, gain["mix_post_g"], gain["ffn2_pre_g"], 1.0, "mixer_out",
                             deps=(tok,))
    w.update(_forwarded(forwarding, groups["ffn2"], h2, "ffn2"))
    early = {}

    def forward_ple(a4):
        early["ple"], tok_ple = _forward_early(gathering["ple"], a4, "ple")
        return {}, (tok_ple,)

    s2, h3, xn4 = _ffn_fwd(xn3, h2, w, "ffn2_", gain["ffn2_post_g"], gain["ple_pre_g"], "ffn2", forward_ple)
    w.update(_forwarded(early["ple"], groups["ple"], h3, "ple"))
    full["ple_w_gate"] = w["ple_w_gate"].reshape(-1, d)
    proj_full = w["ple_w_proj"].transpose(1, 0, 2).reshape(ple_w_proj.shape[1], -1)

    small_g, big_g = {}, {}
    ds, de, dh3, small_g["ple_post_g"], small_g["ple_pre_g"], loss_part = ple_loss(
        xn4, p_bf, full["ple_w_gate"], proj_full, h3, gain["ple_post_g"], gain["ple_pre_g"], tgt, "ple_loss")
    small_g["loss"] = loss_part
    grad, delta, new_m, new_v = {}, {}, {}, {}
    results = (grad, delta, new_m, new_v)
    update = functools.partial(_update, weights=weights, moments_m=moments_m, moments_v=moments_v, results=results)

    def reduce_behind(tag, previous):
        exchanging, tok = _exchange_start(groups[tag], big_g, tag)
        if previous is not None:
            shared, tok = _share_begin(previous[1], tok, chip, previous[0])
        scattering, tok_scatter = _scatter_begin(groups[tag], exchanging, tok, core, tag)
        if previous is not None:
            update(shared, tok_scatter, tag=previous[0])
        return (tag, scattering), (tok_scatter,)

    big_g["ple_w_gate"] = dw_tn(xn4, ds, 1, "dw_ple_gate").reshape(N_CHIPS, -1, d)
    big_g["ple_w_proj"] = dw_tn(p_bf, de, N_CHIPS, "dw_ple_proj", dy_mode="cols")
    reducing, deps = reduce_behind("ple", None)

    dg4, du4, g2, small_g["ffn2_post_g"] = _ffn_bwd_w(dh3, s2, w, "ffn2_", gain["ffn2_post_g"], "ffn2", deps)
    big_g.update(g2)
    dh2, small_g["ffn2_pre_g"] = _ffn_bwd_x(dh3, dg4, du4, s2, w, "ffn2_", gain["ffn2_pre_g"], "ffn2", ())
    reducing, deps = reduce_behind("ffn2", reducing)

    dm, dya, dyb, dz, da, db, small_g["mix_post_g"] = mixer_bwd_y(
        dh2, m, gain["mix_post_g"], full["w_o"], ya, yb, z, full["w_out_a"], full["w_out_b"], "mixer_bwd_y", deps=deps)
    big_g["w_o"] = dw_tn(y, dm, 1, "dw_o").reshape(N_CHIPS, -1, d)
    big_g["w_out_a"] = dw_tn(a, dya, 1, "dw_out_a").reshape(N_CHIPS, -1, d)
    big_g["w_out_b"] = dw_tn(b, dyb, 1, "dw_out_b").reshape(N_CHIPS, -1, d)
    dz, d_sgu_w, d_sgu_b, small_g["sgu_norm_g"] = sgu_bwd(z, da, dz, gain["sgu_norm_g"], sgu_w3, sgu_b3, "sgu_bwd")
    dz, d_pool_w, small_g["pool_scale"] = pool_bwd(db, diff, dz, pool_full, gain["pool_scale"], "pool_bwd")
    big_g["pool_w"] = d_pool_w.astype(BF16).reshape(n_groups, N_CHIPS, rows_per, dgp).transpose(1, 0, 2, 3).reshape(
        N_CHIPS, n_groups * rows_per, dgp)
    big_g["w_in"] = dw_in_tiles(xn2, dz, N_CHIPS, "dw_in")
    dh1, small_g["mix_pre_g"] = dx_norm_bwd([(dz, w["w_in"], True)], h1, gain["mix_pre_g"], dh2, "mixer_bwd_x")
    reducing, deps = reduce_behind("mixer", reducing)

    dg4, du4, g1, small_g["ffn1_post_g"] = _ffn_bwd_w(dh1, s1, w, "ffn1_", gain["ffn1_post_g"], "ffn1", deps)
    big_g.update(g1)
    reducing, deps = reduce_behind("ffn1", reducing)
    dh0, small_g["ffn1_pre_g"] = _ffn_bwd_x(dh1, dg4, du4, s1, w, "ffn1_", gain["ffn1_pre_g"], "ffn1", deps)
    small_g["sgu_w"] = d_sgu_w
    small_g["sgu_b"] = d_sgu_b[:, :, 0]

    no_state = {"loss": jnp.zeros((8, 128), F32)}
    packed = small_allreduce_adamw(
        _pack_small(small_g, "grads"), _pack_small({n: weights[n] for n in SMALL} | no_state, "weights"),
        _pack_small({n: moments_m[n] for n in SMALL} | no_state, "m"), _pack_small({n: moments_v[n] for n in SMALL} | no_state, "v"))
    like = {n: weights[n] for n in SMALL}
    for store, block in zip((grad, delta, new_m, new_v), packed):
        store.update(_unpack_small(block, like))
    _update_last(reducing[1], packed[0], weights, moments_m, moments_v, core, chip, "ffn1", results)

    return (grad["loss"], dh0[None], *[grad[n] for n in WEIGHTS], *[delta[n] for n in WEIGHTS],
            *[new_m[n] for n in WEIGHTS], *[new_v[n] for n in WEIGHTS])
```

```python
import functools

import jax
import jax.numpy as jnp
from jax import lax
from jax.experimental import pallas as pl
from jax.experimental.pallas import tpu as pltpu
from jax.experimental.pallas import tpu_sc as plsc

F32 = jnp.float32
BF16 = jnp.bfloat16
EPS = 1e-6
CHUNK = 128
POOL_WINDOWS = (2, 4, 8, 16)
HALO = 16
N_CHIPS = 4
ADAM_LR, ADAM_B1, ADAM_B2, ADAM_EPS, ADAM_WD, ADAM_STEP = 0.001, 0.9, 0.999, 1e-08, 0.01, 10
VMEM_LIMIT_V7X = 58 * 1024 * 1024
MESH_IDS = pl.DeviceIdType.MESH
HBM_SPEC = pl.BlockSpec(memory_space=pltpu.HBM)
VMEM_SPEC = pl.BlockSpec(memory_space=pltpu.VMEM)
SEM_SPEC = pl.BlockSpec(memory_space=pltpu.SEMAPHORE)
ANY_SPEC = pl.BlockSpec(memory_space=pl.ANY)
DATAFLOW = pltpu.SideEffectType.DATAFLOW_SIDE_EFFECTING
OTHER_CHIPS = ((1, 0), (0, 1), (1, 1))
DZ_SLOT = (2, 3, 4, 0, 1)
SC_TILES, SC_LANES, SC_ROWS = 32, 16, 8
ROW_STEPS = 4
SIBLING_BARRIER = 1
DW_TOKENS = 4096
DW_IN_TILE = 256

NT = (((1,), (1,)), ((), ()))
TN = (((0,), (0,)), ((), ()))


def _params(*sem, **more):
    return pltpu.CompilerParams(dimension_semantics=sem or None, vmem_limit_bytes=VMEM_LIMIT_V7X, **more)


def _tile(t, want):
    return max(c for c in range(8, min(t, want) + 1, 8) if t % c == 0)


def _const(shape):
    return pl.BlockSpec(shape, lambda *_: (0,) * len(shape))


def _rows(tm, d, col=0):
    return pl.BlockSpec((tm, d), lambda i: (i, col))


def _kmajor(nk, tm, kb):
    return pl.BlockSpec((nk, tm, kb), lambda i: (0, i, 0))


def _dot(a, b):
    return jnp.dot(a, b, preferred_element_type=F32)


def _dot_nt(a, b):
    return lax.dot_general(a, b, NT, preferred_element_type=F32)


def _dot_tn(a, b):
    return lax.dot_general(a, b, TN, preferred_element_type=F32)


def _gelu(x):
    return 0.5 * x * (1.0 + jnp.tanh(0.7978845608028654 * (x + 0.044715 * x * x * x)))


def _gelu_and_grad(x):
    k, kc = 0.7978845608028654, 0.7978845608028654 * 0.044715
    x2 = x * x
    th = jnp.tanh(x * (k + kc * x2))
    cdf = 0.5 + 0.5 * th
    return x * cdf, cdf + x * (0.5 - 0.5 * th * th) * (k + 3.0 * kc * x2)


def _sigmoid(x):
    return 1.0 / (1.0 + jnp.exp(-x))


def _rstd(h):
    return lax.rsqrt(jnp.mean(h * h, axis=-1, keepdims=True) + EPS)


def _rms_bwd(h, g, dy):
    r = _rstd(h)
    t = dy * g
    dh = r * t - h * (r * r * r) * jnp.mean(h * t, axis=-1, keepdims=True)
    return dh, jnp.sum(dy * h * r, axis=0, keepdims=True)


def _ordered_after(body, n_in, deps):
    if not deps:
        return body
    return lambda *refs: body(*refs[:n_in], *refs[n_in + len(deps):])


def _accumulate(ref, value):
    @pl.when(pl.program_id(0) == 0)
    def _():
        ref[...] = jnp.zeros_like(ref)

    ref[...] += value


def rms_cast(h, g, name):
    t, d = h.shape
    tm = _tile(t, 512)

    def body(h_ref, g_ref, o_ref):
        hv = h_ref[...]
        o_ref[...] = (hv * _rstd(hv) * g_ref[...]).astype(BF16)

    return pl.pallas_call(
        body, name=name, grid=(t // tm,), in_specs=[_rows(tm, d), _const((1, d))], out_specs=_rows(tm, d),
        out_shape=jax.ShapeDtypeStruct((t, d), BF16), compiler_params=_params("parallel"))(h, g)


def ffn_gu(xn, wgt, wut, name, chip=None, into=None):
    t, d = xn.shape
    nkw, fk, _ = wgt.shape
    tm = _tile(t, 512)
    count = N_CHIPS if chip is None else (1 if into is None else N_CHIPS - 1)
    first = 0 if into is None else 1

    def body(*refs):
        x_ref, wg_ref, wu_ref = refs[-6 - (0 if into is None else 3):][:3]
        a_ref, s_ref, t_ref = refs[-3:]
        xv = x_ref[...]
        g = _dot_nt(xv, wg_ref[0])
        sg = _sigmoid(g)
        s = g * sg
        s_ref[0] = s.astype(BF16)
        ds = sg * (1.0 + g * (1.0 - sg))
        u = _dot_nt(xv, wu_ref[0])
        a_ref[0] = (s * u).astype(BF16)
        t_ref[0] = (u * ds).astype(BF16)

    def slot(k, *chip_ref):
        return (chip_ref[0][0] + first + k) % N_CHIPS if chip_ref else k

    w_spec = pl.BlockSpec((1, fk, d), lambda k, i, *c: (slot(k, *c) if nkw > 1 else 0, 0, 0))
    o_spec = pl.BlockSpec((1, tm, fk), lambda k, i, *c: (slot(k, *c), i, 0))
    shape = jax.ShapeDtypeStruct((N_CHIPS, t, fk), BF16)
    prior = () if into is None else tuple(into)
    call = pl.pallas_call(
        body, name=name,
        grid_spec=pltpu.PrefetchScalarGridSpec(
            num_scalar_prefetch=0 if chip is None else 1, grid=(count, t // tm),
            in_specs=[pl.BlockSpec((tm, d), lambda k, i, *c: (i, 0)), w_spec, w_spec] + [ANY_SPEC] * len(prior),
            out_specs=[o_spec] * 3),
        out_shape=[shape] * 3, input_output_aliases={4 + j: j for j in range(len(prior))},
        compiler_params=_params("parallel", "parallel"))
    return call(*(() if chip is None else (chip,)), xn, wgt, wut, *prior)


def mm_norm_res(a3, w3, h_old, g_post, g_next, scale, name, deps=()):
    nk, t, kb = a3.shape
    d = w3.shape[2]
    tm = _tile(t, 256)

    def body(a_ref, w_ref, h_ref, gp_ref, gn_ref, f_ref, hn_ref, xn_ref):
        f = _dot(a_ref[0], w_ref[0])
        for k in range(1, nk):
            f += _dot(a_ref[k], w_ref[k])
        f_ref[...] = f.astype(BF16)
        hn = h_ref[...] + scale * (f * _rstd(f) * gp_ref[...])
        hn_ref[...] = hn
        xn_ref[...] = (hn * _rstd(hn) * gn_ref[...]).astype(BF16)

    return pl.pallas_call(
        _ordered_after(body, 5, deps), name=name, grid=(t // tm,),
        in_specs=[_kmajor(nk, tm, kb), _const((nk, kb, d)), _rows(tm, d), _const((1, d)), _const((1, d))] + [ANY_SPEC] * len(deps),
        out_specs=[_rows(tm, d)] * 3,
        out_shape=[jax.ShapeDtypeStruct((t, d), BF16), jax.ShapeDtypeStruct((t, d), F32), jax.ShapeDtypeStruct((t, d), BF16)],
        compiler_params=_params("parallel"))(a3, w3, h_old, g_post, g_next, *deps)


def mixer_in(xn, win4, name):
    t, d = xn.shape
    nk, _, nb = win4.shape
    tm = _tile(t, 512)

    def body(x_ref, w_ref, z_ref):
        z_ref[...] = _dot(x_ref[...], w_ref[0]).astype(BF16)

    return pl.pallas_call(
        body, name=name, grid=(nk, t // tm),
        in_specs=[pl.BlockSpec((tm, d), lambda k, i: (i, 0)), pl.BlockSpec((1, d, nb), lambda k, i: (k, 0, 0))],
        out_specs=pl.BlockSpec((tm, nb), lambda k, i: (i, k)), out_shape=jax.ShapeDtypeStruct((t, nk * nb), BF16),
        compiler_params=_params("parallel", "parallel"))(xn, win4)


def _causal_mask():
    row = lax.broadcasted_iota(jnp.int32, (CHUNK, CHUNK), 0)
    col = lax.broadcasted_iota(jnp.int32, (CHUNK, CHUNK), 1)
    return row >= col


def _layernorm_parts(v):
    mu = jnp.mean(v, axis=-1, keepdims=True)
    vc = v - mu
    r = lax.rsqrt(jnp.mean(vc * vc, axis=-1, keepdims=True) + EPS)
    return vc * r, r


def sgu_fwd(z, norm_g, sgu_w, sgu_b3, name):
    t = z.shape[0]
    d = norm_g.shape[1]
    ng = sgu_w.shape[0]
    dg = d // ng
    tm = _tile(t, 256)

    def body(zu_ref, zv_ref, ng_ref, w_ref, b_ref, a_ref):
        vhat, _ = _layernorm_parts(_gelu(zv_ref[...].astype(F32)))
        vn = (vhat * ng_ref[...]).astype(BF16)
        u = _gelu(zu_ref[...].astype(F32))
        mask = _causal_mask()
        for g in range(ng):
            wg = jnp.where(mask, w_ref[g], 0.0).astype(BF16)
            for ci in range(tm // CHUNK):
                rs, cs = slice(ci * CHUNK, (ci + 1) * CHUNK), slice(g * dg, (g + 1) * dg)
                sv = _dot(wg, vn[rs, cs]) + b_ref[g]
                a_ref[rs, cs] = (u[rs, cs] * sv).astype(BF16)

    return pl.pallas_call(
        body, name=name, grid=(t // tm,),
        in_specs=[_rows(tm, d, 0), _rows(tm, d, 1), _const((1, d)), _const((ng, CHUNK, CHUNK)), _const((ng, CHUNK, 1))],
        out_specs=_rows(tm, d), out_shape=jax.ShapeDtypeStruct((t, d), BF16),
        compiler_params=_params("parallel"))(z, z, norm_g, sgu_w, sgu_b3)


def pool_fwd(z, pool_w, pool_scale, name):
    t = z.shape[0]
    d = pool_scale.shape[1]
    ng = pool_w.shape[0]
    dg = d // ng
    tm = _tile(t, 256)
    per = tm // HALO

    def body(c_ref, prev_ref, w_ref, s_ref, diff_ref, b_ref):
        i = pl.program_id(0)
        cur = c_ref[...].astype(F32)
        prev = jnp.where(i > 0, prev_ref[...].astype(F32), 0.0)
        ext = jnp.concatenate([prev, cur], axis=0)
        tok = i * tm + lax.broadcasted_iota(jnp.int32, (tm, 1), 0)
        for g, win in enumerate(POOL_WINDOWS):
            cs = slice(g * dg, (g + 1) * dg)
            s = ext[:, cs]
            sh = 1
            while sh < win:
                s = s + pltpu.roll(s, sh, 0)
                sh *= 2
            per_count = 1.0 / jnp.minimum(tok + 1, win).astype(F32)
            diff = (s[HALO:] * per_count - cur[:, cs]).astype(BF16)
            diff_ref[:, cs] = diff
            b_ref[:, cs] = (_dot(diff, w_ref[g]) * s_ref[:, cs]).astype(BF16)

    return pl.pallas_call(
        body, name=name, grid=(t // tm,),
        in_specs=[_rows(tm, d, 2), pl.BlockSpec((HALO, d), lambda i: (jnp.maximum(i * per - 1, 0), 2)),
                  _const((ng, dg, dg)), _const((1, d))],
        out_specs=[_rows(tm, d)] * 2, out_shape=[jax.ShapeDtypeStruct((t, d), BF16)] * 2,
        compiler_params=_params("parallel"))(z, z, pool_w, pool_scale)


def mixer_y(a, b, z, woa, wob, name):
    t, d = a.shape
    tm = _tile(t, 256)

    def body(a_ref, b_ref, ga_ref, gb_ref, wa_ref, wb_ref, ya_ref, yb_ref, y_ref):
        ya = _dot(a_ref[...], wa_ref[...])
        yb = _dot(b_ref[...], wb_ref[...])
        ya_ref[...] = ya.astype(BF16)
        yb_ref[...] = yb.astype(BF16)
        y_ref[...] = (_sigmoid(ga_ref[...].astype(F32)) * ya + _sigmoid(gb_ref[...].astype(F32)) * yb).astype(BF16)

    return pl.pallas_call(
        body, name=name, grid=(t // tm,),
        in_specs=[_rows(tm, d), _rows(tm, d), _rows(tm, d, 3), _rows(tm, d, 4), _const((d, d)), _const((d, d))],
        out_specs=[_rows(tm, d)] * 3, out_shape=[jax.ShapeDtypeStruct((t, d), BF16)] * 3,
        compiler_params=_params("parallel"))(a, b, z, z, woa, wob)


def ple_loss(xn, p, wpg, wpp, h, g_post, g_pre, target, name):
    t, d = xn.shape
    dp = p.shape[1]
    tm = _tile(t, 256)

    def body(x_ref, p_ref, wg_ref, wp_ref, h_ref, gp_ref, gn_ref, tg_ref, ds_ref, de_ref, dhp_ref, dgp_ref, dgn_ref, loss_ref):
        gate = _sigmoid(_dot(x_ref[...], wg_ref[...]))
        e = _dot(p_ref[...], wp_ref[...])
        q = gate * e
        hv = h_ref[...]
        err = hv + q * _rstd(q) * gp_ref[...] - tg_ref[...]
        _accumulate(loss_ref, jnp.full(loss_ref.shape, (0.5 / d) * jnp.sum(err * err), F32))
        dhv = err * (1.0 / d)
        dq, dgp = _rms_bwd(q, gp_ref[...], dhv)
        ds = (dq * e * gate * (1.0 - gate)).astype(BF16)
        ds_ref[...] = ds
        de_ref[...] = (dq * gate).astype(BF16)
        dx, dgn = _rms_bwd(hv, gn_ref[...], _dot_nt(ds, wg_ref[...]))
        dhp_ref[...] = dhv + dx
        _accumulate(dgp_ref, dgp)
        _accumulate(dgn_ref, dgn)

    return pl.pallas_call(
        body, name=name, grid=(t // tm,),
        in_specs=[_rows(tm, d), _rows(tm, dp), _const((d, d)), _const((dp, d)), _rows(tm, d), _const((1, d)), _const((1, d)),
                  _rows(tm, d)],
        out_specs=[_rows(tm, d)] * 3 + [_const((1, d))] * 2 + [_const((8, 128))],
        out_shape=[jax.ShapeDtypeStruct((t, d), BF16), jax.ShapeDtypeStruct((t, d), BF16), jax.ShapeDtypeStruct((t, d), F32),
                   jax.ShapeDtypeStruct((1, d), F32), jax.ShapeDtypeStruct((1, d), F32), jax.ShapeDtypeStruct((8, 128), F32)],
        compiler_params=_params("arbitrary"))(xn, p, wpg, wpp, h, g_post, g_pre, target)


def ffn_bwd_a(dh, f, g_post, wd4, s4, t4, scale, name, deps=()):
    t, d = dh.shape
    nk, fk, _ = wd4.shape
    tm = _tile(t, 256)

    def body(dh_ref, f_ref, gp_ref, w_ref, s_ref, t_ref, df_ref, dg_ref, du_ref, dgp_ref):
        df, dgp = _rms_bwd(f_ref[...].astype(F32), gp_ref[...], dh_ref[...])
        df = (scale * df).astype(BF16)
        df_ref[...] = df
        _accumulate(dgp_ref, scale * dgp)
        for k in range(nk):
            da = _dot_nt(df, w_ref[k])
            du_ref[k] = (da * s_ref[k].astype(F32)).astype(BF16)
            dg_ref[k] = (da * t_ref[k].astype(F32)).astype(BF16)

    return pl.pallas_call(
        _ordered_after(body, 6, deps), name=name, grid=(t // tm,),
        in_specs=[_rows(tm, d), _rows(tm, d), _const((1, d)), _const((nk, fk, d)), _kmajor(nk, tm, fk), _kmajor(nk, tm, fk)]
        + [ANY_SPEC] * len(deps),
        out_specs=[_rows(tm, d), _kmajor(nk, tm, fk), _kmajor(nk, tm, fk), _const((1, d))],
        out_shape=[jax.ShapeDtypeStruct((t, d), BF16), jax.ShapeDtypeStruct((nk, t, fk), BF16),
                   jax.ShapeDtypeStruct((nk, t, fk), BF16), jax.ShapeDtypeStruct((1, d), F32)],
        compiler_params=_params("arbitrary"))(dh, f, g_post, wd4, s4, t4, *deps)


def dx_norm_bwd(pairs, h, g_pre, dh_in, name, deps=()):
    t, d = h.shape
    tm = _tile(t, 256)
    n = len(pairs)

    def body(*refs):
        dys, ws = refs[:n], refs[n:2 * n]
        h_ref, g_ref, dhi_ref, dho_ref, dg_ref = refs[2 * n:]
        acc = None
        for (_, w4, sections), dy_ref, w_ref in zip(pairs, dys, ws):
            if sections:
                wide = w4.shape[2]
                edges = sorted(set(range(0, 5 * d + 1, d)) | set(range(0, 5 * d + 1, wide)))
                parts = [_dot_nt(dy_ref[DZ_SLOT[lo // d], :, lo % d:lo % d + hi - lo], w_ref[lo // wide, :, lo % wide:lo % wide + hi - lo])
                         for lo, hi in zip(edges[:-1], edges[1:])]
            else:
                parts = [_dot(dy_ref[k], w_ref[k]) for k in range(w4.shape[0])]
            for part in parts:
                acc = part if acc is None else acc + part
        dx, dg = _rms_bwd(h_ref[...], g_ref[...], acc)
        dho_ref[...] = dhi_ref[...] + dx
        _accumulate(dg_ref, dg)

    dy_specs = [_kmajor(dy.shape[0], tm, dy.shape[2]) for dy, _, _ in pairs]
    return pl.pallas_call(
        _ordered_after(body, 2 * n + 3, deps), name=name, grid=(t // tm,),
        in_specs=dy_specs + [_const(w4.shape) for _, w4, _ in pairs] + [_rows(tm, d), _const((1, d)), _rows(tm, d)]
        + [ANY_SPEC] * len(deps),
        out_specs=[_rows(tm, d), _const((1, d))],
        out_shape=[jax.ShapeDtypeStruct((t, d), F32), jax.ShapeDtypeStruct((1, d), F32)],
        compiler_params=_params("arbitrary"))(*[dy for dy, _, _ in pairs], *[w4 for _, w4, _ in pairs], h, g_pre, dh_in, *deps)


def mixer_bwd_y(dh, m, g_post, w_o, ya, yb, z, woa, wob, name, deps=()):
    t, d = dh.shape
    tm = _tile(t, 256)

    def body(dh_ref, m_ref, gp_ref, wo_ref, ya_ref, yb_ref, ga_ref, gb_ref, wa_ref, wb_ref,
             dm_ref, dya_ref, dyb_ref, dz_ref, da_ref, db_ref, dgp_ref):
        dm, dgp = _rms_bwd(m_ref[...].astype(F32), gp_ref[...], dh_ref[...])
        dm = dm.astype(BF16)
        dm_ref[...] = dm
        _accumulate(dgp_ref, dgp)
        dy = _dot_nt(dm, wo_ref[...])
        sa = _sigmoid(ga_ref[...].astype(F32))
        sb = _sigmoid(gb_ref[...].astype(F32))
        dya = (dy * sa).astype(BF16)
        dyb = (dy * sb).astype(BF16)
        dya_ref[...] = dya
        dyb_ref[...] = dyb
        dz_ref[0] = (dy * ya_ref[...].astype(F32) * sa * (1.0 - sa)).astype(BF16)
        dz_ref[1] = (dy * yb_ref[...].astype(F32) * sb * (1.0 - sb)).astype(BF16)
        da_ref[...] = _dot_nt(dya, wa_ref[...]).astype(BF16)
        db_ref[...] = _dot_nt(dyb, wb_ref[...]).astype(BF16)

    return pl.pallas_call(
        _ordered_after(body, 10, deps), name=name, grid=(t // tm,),
        in_specs=[_rows(tm, d), _rows(tm, d), _const((1, d)), _const((d, d)), _rows(tm, d), _rows(tm, d),
                  _rows(tm, d, 3), _rows(tm, d, 4), _const((d, d)), _const((d, d))] + [ANY_SPEC] * len(deps),
        out_specs=[_rows(tm, d)] * 3 + [pl.BlockSpec((2, tm, d), lambda i: (0, i, 0))] + [_rows(tm, d)] * 2 + [_const((1, d))],
        out_shape=[jax.ShapeDtypeStruct((t, d), BF16)] * 3 + [jax.ShapeDtypeStruct((5, t, d), BF16)]
        + [jax.ShapeDtypeStruct((t, d), BF16)] * 2 + [jax.ShapeDtypeStruct((1, d), F32)],
        compiler_params=_params("arbitrary"))(dh, m, g_post, w_o, ya, yb, z, z, woa, wob, *deps)


def sgu_bwd(z, da, dz, norm_g, sgu_w, sgu_b3, name):
    t, d = da.shape
    ng = sgu_w.shape[0]
    dg = d // ng
    tm = _tile(t, 256)
    steps = t // tm

    def body(zu_ref, zv_ref, da_ref, ng_ref, w_ref, b_ref, _, dz_ref, dw_ref, db_ref, dng_ref, dvn_ref, dsv_ref):
        i = pl.program_id(0)
        zv = zv_ref[...].astype(F32)
        zu = zu_ref[...].astype(F32)
        v, gv = _gelu_and_grad(zv)
        vhat, r = _layernorm_parts(v)
        gain = ng_ref[...]
        vn = (vhat * gain).astype(BF16)
        u, gu = _gelu_and_grad(zu)
        dav = da_ref[...].astype(F32)
        mask = _causal_mask()

        @pl.when(i == 0)
        def _():
            dw_ref[...] = jnp.zeros_like(dw_ref)
            dsv_ref[...] = jnp.zeros_like(dsv_ref)

        for g in range(ng):
            wg = jnp.where(mask, w_ref[g], 0.0).astype(BF16)
            dw = jnp.zeros((CHUNK, CHUNK), F32)
            dsv_sum = jnp.zeros((CHUNK, dg), F32)
            for ci in range(tm // CHUNK):
                rs, cs = slice(ci * CHUNK, (ci + 1) * CHUNK), slice(g * dg, (g + 1) * dg)
                vn_blk = vn[rs, cs]
                sv = _dot(wg, vn_blk) + b_ref[g]
                dz_ref[0, rs, cs] = (dav[rs, cs] * sv * gu[rs, cs]).astype(BF16)
                dsv = dav[rs, cs] * u[rs, cs]
                dsv_sum += dsv
                dsv = dsv.astype(BF16)
                dw += _dot_nt(dsv, vn_blk)
                dvn_ref[rs, cs] = _dot_tn(wg, dsv)
            dw_ref[g] += dw
            dsv_ref[:, cs] += dsv_sum

        dvn = dvn_ref[...]
        _accumulate(dng_ref, jnp.sum(dvn * vhat, axis=0, keepdims=True))
        dvh = dvn * gain
        dv = r * (dvh - jnp.mean(dvh, axis=-1, keepdims=True) - vhat * jnp.mean(dvh * vhat, axis=-1, keepdims=True))
        dz_ref[1] = (dv * gv).astype(BF16)

        @pl.when(i == steps - 1)
        def _():
            for g in range(ng):
                dw_ref[g] = jnp.where(mask, dw_ref[g], 0.0)
                row_sum = jnp.sum(dsv_ref[:, g * dg:(g + 1) * dg], axis=1, keepdims=True)
                db_ref[g] = jnp.broadcast_to(row_sum, (CHUNK, CHUNK))

    return pl.pallas_call(
        body, name=name, grid=(steps,),
        in_specs=[_rows(tm, d, 0), _rows(tm, d, 1), _rows(tm, d), _const((1, d)), _const((ng, CHUNK, CHUNK)), _const((ng, CHUNK, 1)),
                  ANY_SPEC],
        out_specs=[pl.BlockSpec((2, tm, d), lambda i: (DZ_SLOT[0] // 2, i, 0)), _const((ng, CHUNK, CHUNK)),
                   _const((ng, CHUNK, CHUNK)), _const((1, d))],
        out_shape=[jax.ShapeDtypeStruct(dz.shape, BF16), jax.ShapeDtypeStruct((ng, CHUNK, CHUNK), F32),
                   jax.ShapeDtypeStruct((ng, CHUNK, CHUNK), F32), jax.ShapeDtypeStruct((1, d), F32)],
        scratch_shapes=[pltpu.VMEM((tm, d), F32), pltpu.VMEM((CHUNK, d), F32)], input_output_aliases={6: 0},
        compiler_params=_params("arbitrary"))(z, z, da, norm_g, sgu_w, sgu_b3, dz)


def pool_bwd(db, diff, dz, pool_w, pool_scale, name):
    t, d = db.shape
    ng = pool_w.shape[0]
    dg = d // ng
    tm = _tile(t, 256)
    per = tm // HALO
    steps = t // tm

    def body(db_ref, next_ref, diff_ref, w_ref, s_ref, _, dc_ref, dw_ref, ds_ref):
        i = pl.program_id(0)
        dbc = db_ref[...].astype(F32)
        nxt = jnp.where(i < steps - 1, next_ref[...].astype(F32), 0.0)
        ext = jnp.concatenate([dbc, nxt], axis=0)
        rows = tm + HALO
        tok = i * tm + lax.broadcasted_iota(jnp.int32, (rows, 1), 0)

        @pl.when(i == 0)
        def _():
            dw_ref[...] = jnp.zeros_like(dw_ref)
            ds_ref[...] = jnp.zeros_like(ds_ref)

        for g, win in enumerate(POOL_WINDOWS):
            cs = slice(g * dg, (g + 1) * dg)
            dp = (ext[:, cs] * s_ref[:, cs]).astype(BF16)
            dd = _dot_nt(dp, w_ref[g])
            s = dd * (1.0 / jnp.minimum(tok + 1, win).astype(F32))
            sh = 1
            while sh < win:
                s = s + pltpu.roll(s, rows - sh, 0)
                sh *= 2
            dc_ref[0, :, cs] = (s[:tm] - dd[:tm]).astype(BF16)
            dfg = diff_ref[:, cs]
            ds_ref[:, cs] += jnp.sum(dbc[:, cs] * _dot(dfg, w_ref[g]), axis=0, keepdims=True)
            dw_ref[g] += _dot_tn(dfg, dp[:tm])

    return pl.pallas_call(
        body, name=name, grid=(steps,),
        in_specs=[_rows(tm, d), pl.BlockSpec((HALO, d), lambda i: (jnp.minimum((i + 1) * per, t // HALO - 1), 0)),
                  _rows(tm, d), _const((ng, dg, dg)), _const((1, d)), ANY_SPEC],
        out_specs=[pl.BlockSpec((1, tm, d), lambda i: (DZ_SLOT[2], i, 0)), _const((ng, dg, dg)), _const((1, d))],
        out_shape=[jax.ShapeDtypeStruct(dz.shape, BF16), jax.ShapeDtypeStruct((ng, dg, dg), F32), jax.ShapeDtypeStruct((1, d), F32)],
        input_output_aliases={5: 0},
        compiler_params=_params("arbitrary"))(db, db, diff, pool_w, pool_scale, dz)


def dw_tn(x, dy, nk, name, x_kmajor=False, dy_mode="same"):
    t = x.shape[-2]
    kx = x.shape[-1]
    n = dy.shape[-1] // nk if dy_mode == "cols" else dy.shape[-1]
    tt = _tile(t, DW_TOKENS)
    steps = t // tt

    def body(x_ref, dy_ref, o_ref, acc_ref):
        s = pl.program_id(1)
        part = _dot_tn(x_ref[0] if x_kmajor else x_ref[...], dy_ref[0] if dy_mode == "kmajor" else dy_ref[...])
        if steps == 1:
            o_ref[0] = part.astype(BF16)
            return

        @pl.when(s == 0)
        def _():
            acc_ref[...] = jnp.zeros_like(acc_ref)

        acc_ref[...] += part

        @pl.when(s == steps - 1)
        def _():
            o_ref[0] = acc_ref[...].astype(BF16)

    x_spec = pl.BlockSpec((1, tt, kx), lambda k, s: (k, s, 0)) if x_kmajor else pl.BlockSpec((tt, kx), lambda k, s: (s, 0))
    dy_spec = {"kmajor": pl.BlockSpec((1, tt, n), lambda k, s: (k, s, 0)), "cols": pl.BlockSpec((tt, n), lambda k, s: (s, k)),
               "same": pl.BlockSpec((tt, n), lambda k, s: (s, 0))}[dy_mode]
    return pl.pallas_call(
        body, name=name, grid=(nk, steps), in_specs=[x_spec, dy_spec],
        out_specs=pl.BlockSpec((1, kx, n), lambda k, s: (k, 0, 0)), out_shape=jax.ShapeDtypeStruct((nk, kx, n), BF16),
        scratch_shapes=[pltpu.VMEM((kx, n) if steps > 1 else (8, 128), F32)],
        compiler_params=_params("parallel", "arbitrary"))(x, dy)


def dw_in_tiles(xn, dz, nk, name):
    t, d = xn.shape
    sections = len(DZ_SLOT)
    per_section = d // DW_IN_TILE
    per_shard = sections * per_section // nk

    def body(x_ref, dy_ref, o_ref):
        o_ref[0] = _dot_tn(x_ref[...], dy_ref[0]).astype(BF16)

    def slot(j):
        return (j // per_section + DZ_SLOT[0]) % sections

    return pl.pallas_call(
        body, name=name, grid=(sections * per_section,),
        in_specs=[pl.BlockSpec((t, d), lambda j: (0, 0)), pl.BlockSpec((1, t, DW_IN_TILE), lambda j: (slot(j), 0, j % per_section))],
        out_specs=pl.BlockSpec((1, d, DW_IN_TILE), lambda j: (j // per_shard, 0, j % per_shard)),
        out_shape=jax.ShapeDtypeStruct((nk, d, sections * d // nk), BF16), compiler_params=_params("parallel"))(xn, dz)


def _place():
    x, y, c = lax.axis_index("x"), lax.axis_index("y"), lax.axis_index("c")
    chips = [((1 - x) if fx else x, (1 - y) if fy else y) for fx, fy in OTHER_CHIPS]
    return x, y, c, chips


def _sibling_handshake(x, y, c):
    barrier = pltpu.get_barrier_semaphore()
    pl.semaphore_signal(barrier, inc=1, device_id=(x, y, 1 - c), device_id_type=MESH_IDS)
    pl.semaphore_wait(barrier, 1)


def _half(rows, which):
    return pl.ds(pl.multiple_of(which * (rows // 2), 16), rows // 2)


def _hbm(a):
    return pltpu.with_memory_space_constraint(a, pltpu.HBM)


def _by_shape(arrays):
    buckets = {}
    for i, a in enumerate(arrays):
        buckets.setdefault(a.shape, []).append(i)
    return list(buckets.values())


def cast_place(shards, chip, name, deps=(), plain=False):
    n = len(shards)

    def body(chip_ref, *refs):
        outs = refs[n + len(deps):]
        for a, w_ref in enumerate(refs[:n]):
            cast = w_ref[...].astype(BF16)
            outs[a][0] = cast
            if plain:
                outs[n + a][...] = cast

    def rows(s):
        return (s.shape[0] // ROW_STEPS, s.shape[1])

    out = pl.pallas_call(
        body, name=name,
        grid_spec=pltpu.PrefetchScalarGridSpec(
            num_scalar_prefetch=1, grid=(ROW_STEPS,),
            in_specs=[pl.BlockSpec(rows(s), lambda i, chip_ref: (i, 0)) for s in shards] + [ANY_SPEC] * len(deps),
            out_specs=[pl.BlockSpec((1,) + rows(s), lambda i, chip_ref: (chip_ref[0], i, 0)) for s in shards]
            + [pl.BlockSpec(rows(s), lambda i, chip_ref: (i, 0)) for s in shards] * plain),
        out_shape=[jax.ShapeDtypeStruct((N_CHIPS,) + s.shape, BF16) for s in shards]
        + [jax.ShapeDtypeStruct(s.shape, BF16) for s in shards] * plain,
        compiler_params=_params("parallel"))(chip, *shards, *deps)
    return (out[:n], out[n:]) if plain else out


def _gather_copy(buf, sends, recvs, i, j, me, chip_xy, c):
    cx, cy = chip_xy
    mine = _half(buf.shape[1], c)
    return pltpu.make_async_remote_copy(
        src_ref=buf.at[me, mine], dst_ref=buf.at[me, mine], send_sem=sends.at[3 * i + j], recv_sem=recvs.at[3 * i + j],
        device_id=(cx, cy, c), device_id_type=MESH_IDS)


def allgather_start(bufs, name, deps=()):
    n = len(bufs)
    extra = len(deps)

    def body(*refs):
        ins = refs[:n]
        sends, recvs = refs[n + extra], refs[n + extra + 1]
        token = refs[2 * n + extra + 2]
        x, y, c, chips = _place()
        for i in range(n):
            for j, chip_xy in enumerate(chips):
                _gather_copy(ins[i], sends, recvs, i, j, 2 * x + y, chip_xy, c).start()
        token[...] = jnp.zeros_like(token)

    out = pl.pallas_call(
        body, name=name, in_specs=[HBM_SPEC] * n + [ANY_SPEC] * extra,
        out_specs=[SEM_SPEC, SEM_SPEC] + [HBM_SPEC] * n + [VMEM_SPEC],
        out_shape=[pltpu.SemaphoreType.DMA((3 * n,)), pltpu.SemaphoreType.DMA((3 * n,))]
        + [pltpu.HBM(b.shape, b.dtype) for b in bufs] + [jax.ShapeDtypeStruct((8, 128), F32)],
        input_output_aliases={i: i + 2 for i in range(n)},
        compiler_params=pltpu.CompilerParams(has_side_effects=DATAFLOW))(*[_hbm(b) for b in bufs], *deps)
    return out[0], out[1], list(out[2:2 + n]), out[2 + n]


def allgather_wait(sends, recvs, bufs, after, name):
    n = len(bufs)

    def body(*refs):
        ins = refs[:n]
        send_sems, recv_sems = refs[n], refs[n + 1]
        x, y, c, chips = _place()
        for i in range(n):
            for j, (cx, cy) in enumerate(chips):
                mine = _half(ins[i].shape[1], c)
                cp = pltpu.make_async_remote_copy(
                    src_ref=ins[i].at[2 * x + y, mine], dst_ref=ins[i].at[2 * cx + cy, mine], send_sem=send_sems.at[3 * i + j],
                    recv_sem=recv_sems.at[3 * i + j], device_id=(cx, cy, c), device_id_type=MESH_IDS)
                cp.wait_send()
                cp.wait_recv()

    return pl.pallas_call(
        body, name=name, in_specs=[HBM_SPEC] * n + [SEM_SPEC, SEM_SPEC, ANY_SPEC], out_specs=[HBM_SPEC] * n,
        out_shape=[pltpu.HBM(b.shape, b.dtype) for b in bufs], input_output_aliases={i: i for i in range(n)},
        compiler_params=pltpu.CompilerParams(has_side_effects=DATAFLOW))(*bufs, sends, recvs, after)


def d2d_forward(bufs, name):
    n = len(bufs)

    def body(*refs):
        ins = refs[:n]
        send_sems, recv_sems = refs[2 * n:]
        x, y, c, chips = _place()
        _sibling_handshake(x, y, c)
        copies = []
        for i in range(n):
            mine = _half(ins[i].shape[1], c)
            for j, (cx, cy) in enumerate(chips):
                landed = ins[i].at[2 * cx + cy, mine]
                cp = pltpu.make_async_remote_copy(
                    src_ref=landed, dst_ref=landed, send_sem=send_sems.at[i, j], recv_sem=recv_sems.at[i, j],
                    device_id=(x, y, 1 - c), device_id_type=MESH_IDS)
                cp.start()
                copies.append(cp)
        for i in range(n):
            theirs = _half(ins[i].shape[1], 1 - c)
            for j, (cx, cy) in enumerate(chips):
                passed = ins[i].at[2 * cx + cy, theirs]
                pltpu.make_async_remote_copy(
                    src_ref=passed, dst_ref=passed, send_sem=send_sems.at[i, j], recv_sem=recv_sems.at[i, j],
                    device_id=(x, y, 1 - c), device_id_type=MESH_IDS).wait_recv()
        for cp in copies:
            cp.wait_send()

    return pl.pallas_call(
        body, name=name, in_specs=[HBM_SPEC] * n, out_specs=[HBM_SPEC] * n,
        out_shape=[jax.ShapeDtypeStruct(b.shape, b.dtype) for b in bufs], input_output_aliases={i: i for i in range(n)},
        scratch_shapes=[pltpu.SemaphoreType.DMA((n, 3))] * 2,
        compiler_params=pltpu.CompilerParams(has_side_effects=True, collective_id=SIBLING_BARRIER))(*bufs)


def _forward_copy(buf, sends, recvs, i, j, chip_xy, x, y, c):
    cx, cy = chip_xy
    rows = buf.shape[1]
    return pltpu.make_async_remote_copy(
        src_ref=buf.at[2 * cx + cy, _half(rows, c)], dst_ref=buf.at[2 * cx + cy, _half(rows, 1 - c)],
        send_sem=sends.at[3 * i + j], recv_sem=recvs.at[3 * i + j], device_id=(x, y, 1 - c), device_id_type=MESH_IDS)


def d2d_forward_start(bufs, name):
    n = len(bufs)

    def body(*refs):
        ins = refs[:n]
        sends, recvs = refs[n], refs[n + 1]
        token = refs[2 * n + 2]
        x, y, c, chips = _place()
        _sibling_handshake(x, y, c)
        for i in range(n):
            mine = _half(ins[i].shape[1], c)
            for j, (cx, cy) in enumerate(chips):
                landed = ins[i].at[2 * cx + cy, mine]
                pltpu.make_async_remote_copy(
                    src_ref=landed, dst_ref=landed, send_sem=sends.at[3 * i + j], recv_sem=recvs.at[3 * i + j],
                    device_id=(x, y, 1 - c), device_id_type=MESH_IDS).start()
        token[...] = jnp.zeros_like(token)

    out = pl.pallas_call(
        body, name=name, in_specs=[HBM_SPEC] * n,
        out_specs=[SEM_SPEC, SEM_SPEC] + [HBM_SPEC] * n + [VMEM_SPEC],
        out_shape=[pltpu.SemaphoreType.DMA((3 * n,)), pltpu.SemaphoreType.DMA((3 * n,))]
        + [pltpu.HBM(b.shape, b.dtype) for b in bufs] + [jax.ShapeDtypeStruct((8, 128), F32)],
        input_output_aliases={i: i + 2 for i in range(n)},
        compiler_params=pltpu.CompilerParams(has_side_effects=DATAFLOW, collective_id=SIBLING_BARRIER))(*[_hbm(b) for b in bufs])
    return (out[0], out[1], list(out[2:2 + n])), out[2 + n]


def d2d_forward_wait(started, after, name):
    sends, recvs, bufs = started
    n = len(bufs)

    def body(*refs):
        ins = refs[:n]
        send_sems, recv_sems = refs[n], refs[n + 1]
        x, y, c, chips = _place()
        for i in range(n):
            for j, chip_xy in enumerate(chips):
                cp = _forward_copy(ins[i], send_sems, recv_sems, i, j, chip_xy, x, y, c)
                cp.wait_send()
                cp.wait_recv()

    return pl.pallas_call(
        body, name=name, in_specs=[HBM_SPEC] * n + [SEM_SPEC, SEM_SPEC, ANY_SPEC], out_specs=[HBM_SPEC] * n,
        out_shape=[pltpu.HBM(b.shape, b.dtype) for b in bufs], input_output_aliases={i: i for i in range(n)},
        compiler_params=pltpu.CompilerParams(has_side_effects=DATAFLOW))(*bufs, sends, recvs, after)


def _sibling_copy(src, land, sends, recvs, i, x, y, c, halves):
    part = src.at[:, _half(src.shape[1], 1 - c)] if halves else src
    return pltpu.make_async_remote_copy(
        src_ref=part, dst_ref=land, send_sem=sends.at[i], recv_sem=recvs.at[i], device_id=(x, y, 1 - c),
        device_id_type=MESH_IDS)


def sibling_start(arrays, halves, name):
    n = len(arrays)
    lands = [lax.empty((a.shape[0], a.shape[1] // 2, a.shape[2]) if halves else a.shape, a.dtype) for a in arrays]

    def body(*refs):
        srcs, zones = refs[:n], refs[n:2 * n]
        sends, recvs = refs[2 * n], refs[2 * n + 1]
        token = refs[4 * n + 2]
        x, y, c, _ = _place()
        _sibling_handshake(x, y, c)
        for i in range(n):
            _sibling_copy(srcs[i], zones[i], sends, recvs, i, x, y, c, halves).start()
        token[...] = jnp.zeros_like(token)

    out = pl.pallas_call(
        body, name=name, in_specs=[HBM_SPEC] * (2 * n),
        out_specs=[SEM_SPEC, SEM_SPEC] + [HBM_SPEC] * (2 * n) + [VMEM_SPEC],
        out_shape=[pltpu.SemaphoreType.DMA((n,)), pltpu.SemaphoreType.DMA((n,))]
        + [pltpu.HBM(a.shape, a.dtype) for a in arrays + lands] + [jax.ShapeDtypeStruct((8, 128), F32)],
        input_output_aliases={i: i + 2 for i in range(2 * n)},
        compiler_params=pltpu.CompilerParams(has_side_effects=DATAFLOW, collective_id=SIBLING_BARRIER))(
            *[_hbm(a) for a in arrays + lands])
    return (out[0], out[1], list(out[2:2 + n]), list(out[2 + n:2 + 2 * n])), out[2 + 2 * n]


def sibling_wait(started, halves, after, name):
    sends, recvs, arrays, lands = started
    n = len(arrays)
    after = tuple(after) if isinstance(after, (tuple, list)) else (after,)

    def body(*refs):
        srcs, zones = refs[:n], refs[n:2 * n]
        send_sems, recv_sems = refs[2 * n], refs[2 * n + 1]
        x, y, c, _ = _place()
        for i in range(n):
            cp = _sibling_copy(srcs[i], zones[i], send_sems, recv_sems, i, x, y, c, halves)
            cp.wait_send()
            cp.wait_recv()

    out = pl.pallas_call(
        body, name=name, in_specs=[HBM_SPEC] * (2 * n) + [SEM_SPEC, SEM_SPEC] + [ANY_SPEC] * len(after),
        out_specs=[HBM_SPEC] * (2 * n),
        out_shape=[pltpu.HBM(a.shape, a.dtype) for a in arrays + lands], input_output_aliases={i: i for i in range(2 * n)},
        compiler_params=pltpu.CompilerParams(has_side_effects=DATAFLOW))(*arrays, *lands, sends, recvs, *after)
    return list(out[:n]), list(out[n:])


def _scatter_copy(src, land, sends, recvs, i, j, chip_xy, c):
    cx, cy = chip_xy
    return pltpu.make_async_remote_copy(
        src_ref=src.at[2 * cx + cy], dst_ref=land.at[j], send_sem=sends.at[3 * i + j], recv_sem=recvs.at[3 * i + j],
        device_id=(cx, cy, c), device_id_type=MESH_IDS)


def scatter_start(sums, name):
    n = len(sums)
    lands = [lax.empty((3,) + s.shape[1:], s.dtype) for s in sums]

    def body(*refs):
        srcs, zones = refs[:n], refs[n:2 * n]
        sends, recvs = refs[2 * n], refs[2 * n + 1]
        token = refs[4 * n + 2]
        _, _, c, chips = _place()
        for i in range(n):
            for j, chip_xy in enumerate(chips):
                _scatter_copy(srcs[i], zones[i], sends, recvs, i, j, chip_xy, c).start()
        token[...] = jnp.zeros_like(token)

    out = pl.pallas_call(
        body, name=name, in_specs=[HBM_SPEC] * (2 * n),
        out_specs=[SEM_SPEC, SEM_SPEC] + [HBM_SPEC] * (2 * n) + [VMEM_SPEC],
        out_shape=[pltpu.SemaphoreType.DMA((3 * n,)), pltpu.SemaphoreType.DMA((3 * n,))]
        + [pltpu.HBM(a.shape, a.dtype) for a in sums + lands] + [jax.ShapeDtypeStruct((8, 128), F32)],
        input_output_aliases={i: i + 2 for i in range(2 * n)},
        compiler_params=pltpu.CompilerParams(has_side_effects=DATAFLOW))(*[_hbm(a) for a in sums + lands])
    return out[0], out[1], list(out[2:2 + n]), list(out[2 + n:2 + 2 * n]), out[2 + 2 * n]


def scatter_wait(sends, recvs, sums, lands, after, name):
    n = len(sums)

    def body(*refs):
        srcs, zones = refs[:n], refs[n:2 * n]
        send_sems, recv_sems = refs[2 * n], refs[2 * n + 1]
        _, _, c, chips = _place()
        for i in range(n):
            for j, chip_xy in enumerate(chips):
                cp = _scatter_copy(srcs[i], zones[i], send_sems, recv_sems, i, j, chip_xy, c)
                cp.wait_send()
                cp.wait_recv()

    out = pl.pallas_call(
        body, name=name, in_specs=[HBM_SPEC] * (2 * n) + [SEM_SPEC, SEM_SPEC, ANY_SPEC], out_specs=[HBM_SPEC] * (2 * n),
        out_shape=[pltpu.HBM(a.shape, a.dtype) for a in sums + lands], input_output_aliases={i: i for i in range(2 * n)},
        compiler_params=pltpu.CompilerParams(has_side_effects=DATAFLOW))(*sums, *lands, sends, recvs, after)
    return list(out[:n]), list(out[n:])


def add_halves(grads, recvs, core, name):
    n = len(grads)
    nk = grads[0].shape[0]
    views = [g.reshape(nk, 2, g.shape[1] // 2, g.shape[2]) for g in grads]

    def body(core_ref, *refs):
        for g_ref, r_ref, o_ref in zip(refs[:n], refs[n:2 * n], refs[2 * n:]):
            o_ref[0] = (g_ref[0, 0].astype(F32) + r_ref[0].astype(F32)).astype(BF16)

    return pl.pallas_call(
        body, name=name,
        grid_spec=pltpu.PrefetchScalarGridSpec(
            num_scalar_prefetch=1, grid=(nk,),
            in_specs=[pl.BlockSpec((1, 1) + v.shape[2:], lambda k, core_ref: (k, core_ref[0], 0, 0)) for v in views]
            + [pl.BlockSpec((1,) + r.shape[1:], lambda k, core_ref: (k, 0, 0)) for r in recvs],
            out_specs=[pl.BlockSpec((1,) + r.shape[1:], lambda k, core_ref: (k, 0, 0)) for r in recvs]),
        out_shape=[jax.ShapeDtypeStruct(r.shape, BF16) for r in recvs], compiler_params=_params("parallel"))(core, *views, *recvs)


def add_chips(sums, lands, chip, name):
    n = len(sums)

    def body(chip_ref, *refs):
        for s_ref, r_ref, o_ref in zip(refs[:n], refs[n:2 * n], refs[2 * n:]):
            o_ref[...] = ((s_ref[0].astype(F32) + r_ref[0].astype(F32)) + r_ref[1].astype(F32)) + r_ref[2].astype(F32)

    def rows(s):
        return (s.shape[1] // ROW_STEPS, s.shape[2])

    return pl.pallas_call(
        body, name=name,
        grid_spec=pltpu.PrefetchScalarGridSpec(
            num_scalar_prefetch=1, grid=(ROW_STEPS,),
            in_specs=[pl.BlockSpec((1,) + rows(s), lambda i, chip_ref: (chip_ref[0], i, 0)) for s in sums]
            + [pl.BlockSpec((3,) + rows(s), lambda i, chip_ref: (0, i, 0)) for s in sums],
            out_specs=[pl.BlockSpec(rows(s), lambda i, chip_ref: (i, 0)) for s in sums]),
        out_shape=[jax.ShapeDtypeStruct(s.shape[1:], F32) for s in sums], compiler_params=_params("parallel"))(chip, *sums, *lands)


def _adamw_math(w, g, m, v):
    m = ADAM_B1 * m + (1.0 - ADAM_B1) * g
    v = ADAM_B2 * v + (1.0 - ADAM_B2) * (g * g)
    m_hat = m / (1.0 - ADAM_B1 ** ADAM_STEP)
    v_hat = v / (1.0 - ADAM_B2 ** ADAM_STEP)
    return -ADAM_LR * (m_hat / (jnp.sqrt(v_hat) + ADAM_EPS) + ADAM_WD * w), m, v


def adamw_one_half(ws, gs, ms, vs, core, name, own, into=None):
    n = len(ws)
    r, cdim = ws[0].shape
    half = r // 2
    tr = _tile(half, max(8, 256 // n))
    steps = half // tr
    prior = () if into is None else tuple(a for group in into for a in group)

    def body(core_ref, *refs):
        ins, outs = refs[:4 * n], refs[4 * n + len(prior):]
        for a in range(n):
            w_ref, g_ref, m_ref, v_ref = ins[a::n]
            g = g_ref[...]
            outs[a][...] = g
            outs[n + a][...], outs[2 * n + a][...], outs[3 * n + a][...] = _adamw_math(w_ref[...], g, m_ref[...], v_ref[...])

    def rows(i, core_ref):
        return ((core_ref[0] if own else 1 - core_ref[0]) * steps + i, 0)

    whole = pl.BlockSpec((tr, cdim), rows)
    part = pl.BlockSpec((tr, cdim), lambda i, core_ref: (i, 0))
    out = pl.pallas_call(
        body, name=name,
        grid_spec=pltpu.PrefetchScalarGridSpec(
            num_scalar_prefetch=1, grid=(steps,),
            in_specs=[whole] * n + [part] * n + [whole] * (2 * n) + [ANY_SPEC] * len(prior), out_specs=[whole] * (4 * n)),
        out_shape=[jax.ShapeDtypeStruct((r, cdim), F32)] * (4 * n),
        input_output_aliases={1 + 4 * n + j: j for j in range(len(prior))},
        compiler_params=_params("parallel"))(core, *ws, *gs, *ms, *vs, *prior)
    return [out[k * n:(k + 1) * n] for k in range(4)]


def adamw_halves_sparsecore(ws, owns, others, ms, vs, name):
    n = len(ws)
    r, cdim = ws[0].shape
    half_groups = r // 2 // SC_ROWS
    per_tile = -(-2 * half_groups // SC_TILES)

    def body(*refs):
        ins, outs, (wb, gb, mb, vb) = refs[:5 * n], refs[5 * n:9 * n], refs[9 * n:]
        tile = lax.axis_index("sc_tile") * 2 + lax.axis_index("sc_core")
        core = lax.axis_index("c")

        @pl.loop(0, per_tile)
        def _(it):
            group = tile + SC_TILES * it

            @pl.when(group < 2 * half_groups)
            def _():
                rows = pl.ds(group * SC_ROWS, SC_ROWS)
                in_own = (group < half_groups) == (core == 0)
                half_rows = pl.ds((group % half_groups) * SC_ROWS, SC_ROWS)
                for a in range(n):
                    w_hbm, own_hbm, other_hbm, m_hbm, v_hbm = ins[a::n]
                    g_out, d_out, m_out, v_out = outs[a::n]
                    pltpu.sync_copy(w_hbm.at[rows], wb)
                    pltpu.sync_copy(m_hbm.at[rows], mb)
                    pltpu.sync_copy(v_hbm.at[rows], vb)

                    @pl.when(in_own)
                    def _():
                        pltpu.sync_copy(own_hbm.at[half_rows], gb)

                    @pl.when(jnp.logical_not(in_own))
                    def _():
                        pltpu.sync_copy(other_hbm.at[half_rows], gb)

                    pltpu.sync_copy(gb, g_out.at[rows])

                    @pl.loop(0, SC_ROWS)
                    def _(row):
                        @pl.loop(0, cdim, step=SC_LANES)
                        def _(col):
                            at = (row, pl.ds(col, SC_LANES))
                            wb[at], mb[at], vb[at] = _adamw_math(wb[at], gb[at], mb[at], vb[at])

                    pltpu.sync_copy(wb, d_out.at[rows])
                    pltpu.sync_copy(mb, m_out.at[rows])
                    pltpu.sync_copy(vb, v_out.at[rows])

    out = pl.kernel(
        body, name=name, out_type=[jax.ShapeDtypeStruct((r, cdim), F32)] * (4 * n),
        mesh=plsc.VectorSubcoreMesh(core_axis_name="sc_core", subcore_axis_name="sc_tile"),
        scratch_types=[pltpu.VMEM((SC_ROWS, cdim), F32)] * 4)(*ws, *owns, *others, *ms, *vs)
    return [out[a::n] for a in range(n)]


def small_allreduce_adamw(g, w, m, v, deps=()):
    r, cdim = g.shape

    def body(g_ref, w_ref, m_ref, v_ref, go_ref, d_ref, mo_ref, vo_ref, pair, quad, d2d_send, d2d_recv, ici_send, ici_recv):
        x, y, c, chips = _place()
        me = 2 * x + y
        pair[c] = g_ref[...]
        swap = pltpu.make_async_remote_copy(
            src_ref=g_ref, dst_ref=pair.at[c], send_sem=d2d_send, recv_sem=d2d_recv, device_id=(x, y, 1 - c),
            device_id_type=MESH_IDS)
        swap.start()
        swap.wait()
        quad[me] = pair[0] + pair[1]
        copies = []
        for j, (cx, cy) in enumerate(chips):
            cp = pltpu.make_async_remote_copy(
                src_ref=quad.at[me], dst_ref=quad.at[me], send_sem=ici_send.at[j], recv_sem=ici_recv.at[j],
                device_id=(cx, cy, c), device_id_type=MESH_IDS)
            cp.start()
            copies.append(cp)
        for j, (cx, cy) in enumerate(chips):
            slot = quad.at[2 * cx + cy]
            pltpu.make_async_remote_copy(
                src_ref=slot, dst_ref=slot, send_sem=ici_send.at[j], recv_sem=ici_recv.at[j], device_id=(cx, cy, c),
                device_id_type=MESH_IDS).wait_recv()
        for cp in copies:
            cp.wait_send()
        total = (quad[0] + quad[1]) + (quad[2] + quad[3])
        go_ref[...] = total
        d_ref[...], mo_ref[...], vo_ref[...] = _adamw_math(w_ref[...], total, m_ref[...], v_ref[...])

    return pl.pallas_call(
        _ordered_after(body, 4, deps), name="small_allreduce_adamw", in_specs=[VMEM_SPEC] * 4 + [ANY_SPEC] * len(deps),
        out_specs=[VMEM_SPEC] * 4,
        out_shape=[jax.ShapeDtypeStruct((r, cdim), F32)] * 4,
        scratch_shapes=[pltpu.VMEM((2, r, cdim), F32), pltpu.VMEM((N_CHIPS, r, cdim), F32), pltpu.SemaphoreType.DMA,
                        pltpu.SemaphoreType.DMA, pltpu.SemaphoreType.DMA((3,)), pltpu.SemaphoreType.DMA((3,))],
        compiler_params=pltpu.CompilerParams(has_side_effects=True, vmem_limit_bytes=VMEM_LIMIT_V7X))(g, w, m, v, *deps)


GROUPS = (
    ("ffn1", ("ffn1_w_gate", "ffn1_w_up", "ffn1_w_down")),
    ("mixer", ("w_in", "pool_w", "w_out_a", "w_out_b", "w_o")),
    ("ffn2", ("ffn2_w_gate", "ffn2_w_up", "ffn2_w_down")),
    ("ple", ("ple_w_gate", "ple_w_proj")),
)
GATHERS = (
    ("ffn1_in", ("ffn1_w_gate", "ffn1_w_up")),
    ("ffn1_out", ("ffn1_w_down",)),
) + GROUPS[1:]
GAINS = ("ffn1_pre_g", "ffn1_post_g", "mix_pre_g", "sgu_norm_g", "pool_scale", "mix_post_g",
         "ffn2_pre_g", "ffn2_post_g", "ple_pre_g", "ple_post_g")
SMALL = GAINS + ("sgu_b", "sgu_w")
PACKED = SMALL + ("loss",)
WEIGHTS = ("ffn1_pre_g", "ffn1_w_gate", "ffn1_w_up", "ffn1_w_down", "ffn1_post_g", "mix_pre_g", "w_in", "sgu_norm_g",
           "sgu_w", "sgu_b", "pool_w", "pool_scale", "w_out_a", "w_out_b", "w_o", "mix_post_g", "ffn2_pre_g",
           "ffn2_w_gate", "ffn2_w_up", "ffn2_w_down", "ffn2_post_g", "ple_pre_g", "ple_w_gate", "ple_w_proj", "ple_post_g")
PACK_ROWS = 16


TRANSPOSED = ("ffn1_w_gate", "ffn1_w_up", "ffn2_w_gate", "ffn2_w_up")


def _shard2d(name, a):
    a = a[0]
    return a.T if name in TRANSPOSED else a.reshape(-1, a.shape[-1])


def _unshard2d(name, a2d, shape):
    return (a2d.T if name in TRANSPOSED else a2d).reshape(shape)


def pack_rows(gains, sgu_b, loss_tile, name):
    n = len(gains)
    d = gains[0].shape[1]
    g = sgu_b.shape[0]

    def body(*refs):
        o_ref = refs[-1]
        o_ref[...] = jnp.zeros_like(o_ref)
        for i in range(n):
            o_ref[i:i + 1, :] = refs[i][...]
        o_ref[n:n + g, 0:CHUNK] = refs[n][...]
        o_ref[n + g:n + g + 1, 0:CHUNK] = refs[n + 1][0:1, :]

    return pl.pallas_call(
        body, name=name, in_specs=[VMEM_SPEC] * (n + 2), out_specs=VMEM_SPEC,
        out_shape=jax.ShapeDtypeStruct((PACK_ROWS, d), F32))(*gains, sgu_b, loss_tile)


def _pack_small(parts, tag):
    d = parts[GAINS[0]].shape[-1]
    rows = pack_rows([parts[n] for n in GAINS], parts["sgu_b"].reshape(-1, CHUNK), parts["loss"], "pack_" + tag)
    return jnp.concatenate([rows, parts["sgu_w"].reshape(-1, d)], axis=0)


def _unpack_small(packed, like):
    n, g = len(GAINS), like["sgu_b"].size // CHUNK
    out = {name: packed[i:i + 1] for i, name in enumerate(GAINS)}
    out["sgu_b"] = packed[n:n + g, :CHUNK].reshape(like["sgu_b"].shape)
    out["loss"] = packed[n + g, 0]
    out["sgu_w"] = packed[PACK_ROWS:].reshape(like["sgu_w"].shape)
    return out


def _ffn_fwd(xn, h, w, pre, g_post, g_next, tag, between=None, first=None):
    a4, s4, t4 = ffn_gu(xn, w[pre + "w_gate"], w[pre + "w_up"], tag + "_gu") if first is None else first(xn)
    deps = ()
    if between is not None:
        more, deps = between(a4)
        w.update(more)
    f, h_new, xn_next = mm_norm_res(a4, w[pre + "w_down"], h, g_post, g_next, 0.5, tag + "_down", deps=deps)
    return dict(xn=xn, h=h, a4=a4, s4=s4, t4=t4, f=f), h_new, xn_next


def _ffn_bwd_w(dh, saved, w, pre, g_post, tag, deps):
    df, dg4, du4, d_post = ffn_bwd_a(dh, saved["f"], g_post, w[pre + "w_down"], saved["s4"], saved["t4"], 0.5, tag + "_bwd_a",
                                     deps=deps)
    return dg4, du4, _ffn_dw(saved, df, dg4, du4, pre, tag), d_post


def _ffn_dw(saved, df, dg4, du4, pre, tag):
    nk = N_CHIPS
    return {
        pre + "w_down": dw_tn(saved["a4"], df, nk, tag + "_dw_down", x_kmajor=True, dy_mode="same"),
        pre + "w_gate": dw_tn(dg4, saved["xn"], nk, tag + "_dw_gate", x_kmajor=True, dy_mode="same"),
        pre + "w_up": dw_tn(du4, saved["xn"], nk, tag + "_dw_up", x_kmajor=True, dy_mode="same"),
    }


def _ffn_bwd_x(dh, dg4, du4, saved, w, pre, g_pre, tag, deps):
    return dx_norm_bwd([(dg4, w[pre + "w_gate"], False), (du4, w[pre + "w_up"], False)], saved["h"], g_pre, dh,
                       tag + "_bwd_x", deps=deps)


def _start_gathers(weights, chip):
    first_tag, first_names = GATHERS[0]
    made, own = cast_place([_shard2d(n, weights[n]) for n in first_names], chip, "cast_" + first_tag, plain=True)
    sends, recvs, bufs, token = allgather_start(list(made), "allgather_start_" + first_tag)
    gathering = {first_tag: (sends, recvs, bufs)}
    later = [n for _, names in GATHERS[1:] for n in names]
    cast = dict(zip(later, cast_place([_shard2d(n, weights[n]) for n in later], chip, "cast_later", deps=(token,))))
    for tag, names in GATHERS[1:]:
        sends, recvs, bufs, token = allgather_start([cast[n] for n in names], "allgather_start_" + tag, deps=(token,))
        gathering[tag] = (sends, recvs, bufs)
    return gathering, token, dict(zip(first_names, own))


def _gathered(started, names, after, tag):
    sends, recvs, bufs = started
    landed = allgather_wait(sends, recvs, bufs, after, "allgather_wait_" + tag)
    return dict(zip(names, d2d_forward(landed, "d2d_forward_" + tag)))


def _forward_early(started, after, tag):
    sends, recvs, bufs = started
    landed = allgather_wait(sends, recvs, bufs, after, "allgather_wait_" + tag)
    return d2d_forward_start(landed, "d2d_forward_start_" + tag)


def _forwarded(forwarding, names, after, tag):
    return dict(zip(names, d2d_forward_wait(forwarding, after, "d2d_forward_wait_" + tag)))


def _exchange_start(names, big_g, tag):
    return sibling_start([big_g[n] for n in names], True, "exchange_start_" + tag)


def _scatter_begin(names, exchanging, after, core, tag):
    partial, from_sibling = sibling_wait(exchanging, True, after, "exchange_wait_" + tag)
    chip_sums = list(add_halves(partial, from_sibling, core, "add_halves_" + tag))
    sends, recvs, sums, lands, token = scatter_start(chip_sums, "scatter_start_" + tag)
    return (names, sends, recvs, sums, lands), token


def _share_begin(scattering, after, chip, tag):
    names, sends, recvs, sums, lands = scattering
    sums, lands = scatter_wait(sends, recvs, sums, lands, after, "scatter_wait_" + tag)
    reduced = list(add_chips(sums, lands, chip, "add_chips_" + tag))
    sharing, token = sibling_start(reduced, False, "share_start_" + tag)
    return (names, sharing), token


def _update(shared, after, weights, moments_m, moments_v, tag, results):
    names, sharing = shared
    reduced, others = sibling_wait(sharing, False, after, "share_wait_" + tag)
    for b, idx in enumerate(_by_shape(reduced)):
        in_bucket = [names[i] for i in idx]
        outs = adamw_halves_sparsecore(
            [_shard2d(n, weights[n]) for n in in_bucket], [reduced[i] for i in idx], [others[i] for i in idx],
            [_shard2d(n, moments_m[n]) for n in in_bucket], [_shard2d(n, moments_v[n]) for n in in_bucket], f"adamw_{tag}_{b}")
        for n, per_weight in zip(in_bucket, outs):
            for store, value in zip(results, per_weight):
                store[n] = _unshard2d(n, value, weights[n].shape)


def _update_last(scattering, after, weights, moments_m, moments_v, core, chip, tag, results):
    names, sends, recvs, sums, lands = scattering
    sums, lands = scatter_wait(sends, recvs, sums, lands, after, "scatter_wait_" + tag)
    reduced = list(add_chips(sums, lands, chip, "add_chips_" + tag))
    (sends, recvs, reduced, zones), _ = sibling_start(reduced, False, "share_start_" + tag)
    state = [[_shard2d(n, held[n]) for n in names] for held in (weights, moments_m, moments_v)]
    first = adamw_one_half(state[0], reduced, state[1], state[2], core, f"adamw_{tag}_own", own=True)
    _, others = sibling_wait((sends, recvs, reduced, zones), False, first[1][-1], "share_wait_" + tag)
    outs = adamw_one_half(state[0], others, state[1], state[2], core, f"adamw_{tag}_other", own=False, into=first)
    for store, values in zip(results, outs):
        for n, value in zip(names, values):
            store[n] = _unshard2d(n, value, weights[n].shape)


def kernel(x, p, ffn1_pre_g, ffn1_w_gate, ffn1_w_up, ffn1_w_down, ffn1_post_g, mix_pre_g, w_in, sgu_norm_g, sgu_w, sgu_b, pool_w, pool_scale, w_out_a, w_out_b, w_o, mix_post_g, ffn2_pre_g, ffn2_w_gate, ffn2_w_up, ffn2_w_down, ffn2_post_g, ple_pre_g, ple_w_gate, ple_w_proj, ple_post_g, loss_target, m_ffn1_pre_g, m_ffn1_w_gate, m_ffn1_w_up, m_ffn1_w_down, m_ffn1_post_g, m_mix_pre_g, m_w_in, m_sgu_norm_g, m_sgu_w, m_sgu_b, m_pool_w, m_pool_scale, m_w_out_a, m_w_out_b, m_w_o, m_mix_post_g, m_ffn2_pre_g, m_ffn2_w_gate, m_ffn2_w_up, m_ffn2_w_down, m_ffn2_post_g, m_ple_pre_g, m_ple_w_gate, m_ple_w_proj, m_ple_post_g, v_ffn1_pre_g, v_ffn1_w_gate, v_ffn1_w_up, v_ffn1_w_down, v_ffn1_post_g, v_mix_pre_g, v_w_in, v_sgu_norm_g, v_sgu_w, v_sgu_b, v_pool_w, v_pool_scale, v_w_out_a, v_w_out_b, v_w_o, v_mix_post_g, v_ffn2_pre_g, v_ffn2_w_gate, v_ffn2_w_up, v_ffn2_w_down, v_ffn2_post_g, v_ple_pre_g, v_ple_w_gate, v_ple_w_proj, v_ple_post_g):
    given = dict(locals())
    weights = {n: given[n] for n in WEIGHTS}
    moments_m = {n: given["m_" + n] for n in WEIGHTS}
    moments_v = {n: given["v_" + n] for n in WEIGHTS}
    core = lax.axis_index("c").astype(jnp.int32).reshape(1)
    chip = (2 * lax.axis_index("x") + lax.axis_index("y")).astype(jnp.int32).reshape(1)

    d = x.shape[-1]
    gathering, last_start, own_first = _start_gathers(weights, chip)
    token = (last_start,)
    gain = {n: weights[n] for n in GAINS}
    sgu_w3 = sgu_w[0]
    sgu_b3 = sgu_b[0][:, :, None]
    groups = dict(GROUPS + GATHERS)

    h0 = x[0]
    tgt = loss_target[0]
    p_bf = p[0, 0].astype(BF16)
    xn1 = rms_cast(h0, gain["ffn1_pre_g"], "ffn1_pre_norm")
    w = {}

    def ffn1_first(xn):
        own = ffn_gu(xn, own_first["ffn1_w_gate"][None], own_first["ffn1_w_up"][None], "ffn1_gu_own", chip=chip)
        w.update(_gathered(gathering["ffn1_in"], groups["ffn1_in"], own[0], "ffn1_in"))
        return ffn_gu(xn, w["ffn1_w_gate"], w["ffn1_w_up"], "ffn1_gu", chip=chip, into=own)

    s1, h1, xn2 = _ffn_fwd(xn1, h0, w, "ffn1_", gain["ffn1_post_g"], gain["mix_pre_g"], "ffn1",
                           lambda a4: (_gathered(gathering["ffn1_out"], groups["ffn1_out"], a4, "ffn1_out"), ()), ffn1_first)
    w.update(_gathered(gathering["mixer"], groups["mixer"], h1, "mixer"))
    full = {n: w[n].reshape(-1, d) for n in ("w_out_a", "w_out_b", "w_o")}
    n_groups = pool_w.shape[1]
    rows_per = pool_w.shape[2]
    dgp = pool_w.shape[3]
    pool_full = w["pool_w"].reshape(N_CHIPS, n_groups, rows_per, dgp).transpose(1, 0, 2, 3).reshape(n_groups, N_CHIPS * rows_per, dgp)
    z = mixer_in(xn2, w["w_in"], "mixer_in")
    a = sgu_fwd(z, gain["sgu_norm_g"], sgu_w3, sgu_b3, "sgu_fwd")
    diff, b = pool_fwd(z, pool_full, gain["pool_scale"], "pool_fwd")
    ya, yb, y = mixer_y(a, b, z, full["w_out_a"], full["w_out_b"], "mixer_y")
    forwarding, tok = _forward_early(gathering["ffn2"], y, "ffn2")
    m, h2, xn3 = mm_norm_res(y[None], full["w_o"][None], h1, gain["mix_post_g"], gain["ffn2_pre_g"], 1.0, "mixer_out",
                             deps=(tok,))
    w.update(_forwarded(forwarding, groups["ffn2"], h2, "ffn2"))
    early = {}

    def forward_ple(a4):
        early["ple"], tok_ple = _forward_early(gathering["ple"], a4, "ple")
        return {}, (tok_ple,)

    s2, h3, xn4 = _ffn_fwd(xn3, h2, w, "ffn2_", gain["ffn2_post_g"], gain["ple_pre_g"], "ffn2", forward_ple)
    w.update(_forwarded(early["ple"], groups["ple"], h3, "ple"))
    full["ple_w_gate"] = w["ple_w_gate"].reshape(-1, d)
    proj_full = w["ple_w_proj"].transpose(1, 0, 2).reshape(ple_w_proj.shape[1], -1)

    small_g, big_g = {}, {}
    ds, de, dh3, small_g["ple_post_g"], small_g["ple_pre_g"], loss_part = ple_loss(
        xn4, p_bf, full["ple_w_gate"], proj_full, h3, gain["ple_post_g"], gain["ple_pre_g"], tgt, "ple_loss")
    small_g["loss"] = loss_part
    grad, delta, new_m, new_v = {}, {}, {}, {}
    results = (grad, delta, new_m, new_v)
    update = functools.partial(_update, weights=weights, moments_m=moments_m, moments_v=moments_v, results=results)

    def reduce_behind(tag, previous):
        exchanging, tok = _exchange_start(groups[tag], big_g, tag)
        if previous is not None:
            shared, tok = _share_begin(previous[1], tok, chip, previous[0])
        scattering, tok_scatter = _scatter_begin(groups[tag], exchanging, tok, core, tag)
        if previous is not None:
            update(shared, tok_scatter, tag=previous[0])
        return (tag, scattering), (tok_scatter,)

    big_g["ple_w_gate"] = dw_tn(xn4, ds, 1, "dw_ple_gate").reshape(N_CHIPS, -1, d)
    big_g["ple_w_proj"] = dw_tn(p_bf, de, N_CHIPS, "dw_ple_proj", dy_mode="cols")
    reducing, deps = reduce_behind("ple", None)

    dg4, du4, g2, small_g["ffn2_post_g"] = _ffn_bwd_w(dh3, s2, w, "ffn2_", gain["ffn2_post_g"], "ffn2", deps)
    big_g.update(g2)
    dh2, small_g["ffn2_pre_g"] = _ffn_bwd_x(dh3, dg4, du4, s2, w, "ffn2_", gain["ffn2_pre_g"], "ffn2", ())
    reducing, deps = reduce_behind("ffn2", reducing)

    dm, dya, dyb, dz, da, db, small_g["mix_post_g"] = mixer_bwd_y(
        dh2, m, gain["mix_post_g"], full["w_o"], ya, yb, z, full["w_out_a"], full["w_out_b"], "mixer_bwd_y", deps=deps)
    big_g["w_o"] = dw_tn(y, dm, 1, "dw_o").reshape(N_CHIPS, -1, d)
    big_g["w_out_a"] = dw_tn(a, dya, 1, "dw_out_a").reshape(N_CHIPS, -1, d)
    big_g["w_out_b"] = dw_tn(b, dyb, 1, "dw_out_b").reshape(N_CHIPS, -1, d)
    dz, d_sgu_w, d_sgu_b, small_g["sgu_norm_g"] = sgu_bwd(z, da, dz, gain["sgu_norm_g"], sgu_w3, sgu_b3, "sgu_bwd")
    dz, d_pool_w, small_g["pool_scale"] = pool_bwd(db, diff, dz, pool_full, gain["pool_scale"], "pool_bwd")
    big_g["pool_w"] = d_pool_w.astype(BF16).reshape(n_groups, N_CHIPS, rows_per, dgp).transpose(1, 0, 2, 3).reshape(
        N_CHIPS, n_groups * rows_per, dgp)
    big_g["w_in"] = dw_in_tiles(xn2, dz, N_CHIPS, "dw_in")
    dh1, small_g["mix_pre_g"] = dx_norm_bwd([(dz, w["w_in"], True)], h1, gain["mix_pre_g"], dh2, "mixer_bwd_x")
    reducing, deps = reduce_behind("mixer", reducing)

    dg4, du4, g1, small_g["ffn1_post_g"] = _ffn_bwd_w(dh1, s1, w, "ffn1_", gain["ffn1_post_g"], "ffn1", deps)
    big_g.update(g1)
    reducing, deps = reduce_behind("ffn1", reducing)
    dh0, small_g["ffn1_pre_g"] = _ffn_bwd_x(dh1, dg4, du4, s1, w, "ffn1_", gain["ffn1_pre_g"], "ffn1", deps)
    small_g["sgu_w"] = d_sgu_w
    small_g["sgu_b"] = d_sgu_b[:, :, 0]

    no_state = {"loss": jnp.zeros((8, 128), F32)}
    packed = small_allreduce_adamw(
        _pack_small(small_g, "grads"), _pack_small({n: weights[n] for n in SMALL} | no_state, "weights"),
        _pack_small({n: moments_m[n] for n in SMALL} | no_state, "m"), _pack_small({n: moments_v[n] for n in SMALL} | no_state, "v"))
    like = {n: weights[n] for n in SMALL}
    for store, block in zip((grad, delta, new_m, new_v), packed):
        store.update(_unpack_small(block, like))
    _update_last(reducing[1], packed[0], weights, moments_m, moments_v, core, chip, "ffn1", results)

    return (grad["loss"], dh0[None], *[grad[n] for n in WEIGHTS], *[delta[n] for n in WEIGHTS],
            *[new_m[n] for n in WEIGHTS], *[new_v[n] for n in WEIGHTS])
```

```python
import functools

import jax
import jax.numpy as jnp
from jax import lax
from jax.experimental import pallas as pl
from jax.experimental.pallas import tpu as pltpu
from jax.experimental.pallas import tpu_sc as plsc

F32 = jnp.float32
BF16 = jnp.bfloat16
EPS = 1e-6
CHUNK = 128
POOL_WINDOWS = (2, 4, 8, 16)
HALO = 16
N_CHIPS = 4
ADAM_LR, ADAM_B1, ADAM_B2, ADAM_EPS, ADAM_WD, ADAM_STEP = 0.001, 0.9, 0.999, 1e-08, 0.01, 10
VMEM_LIMIT_V7X = 58 * 1024 * 1024
MESH_IDS = pl.DeviceIdType.MESH
HBM_SPEC = pl.BlockSpec(memory_space=pltpu.HBM)
VMEM_SPEC = pl.BlockSpec(memory_space=pltpu.VMEM)
SEM_SPEC = pl.BlockSpec(memory_space=pltpu.SEMAPHORE)
ANY_SPEC = pl.BlockSpec(memory_space=pl.ANY)
DATAFLOW = pltpu.SideEffectType.DATAFLOW_SIDE_EFFECTING
OTHER_CHIPS = ((1, 0), (0, 1), (1, 1))
DZ_SLOT = (2, 3, 4, 0, 1)
SC_TILES, SC_LANES, SC_ROWS = 32, 16, 8
ROW_STEPS = 4
SIBLING_BARRIER = 1
DW_TOKENS = 4096
DW_IN_TILE = 256

NT = (((1,), (1,)), ((), ()))
TN = (((0,), (0,)), ((), ()))


def _params(*sem, **more):
    return pltpu.CompilerParams(dimension_semantics=sem or None, vmem_limit_bytes=VMEM_LIMIT_V7X, **more)


def _tile(t, want):
    return max(c for c in range(8, min(t, want) + 1, 8) if t % c == 0)


def _const(shape):
    return pl.BlockSpec(shape, lambda *_: (0,) * len(shape))


def _rows(tm, d, col=0):
    return pl.BlockSpec((tm, d), lambda i: (i, col))


def _kmajor(nk, tm, kb):
    return pl.BlockSpec((nk, tm, kb), lambda i: (0, i, 0))


def _dot(a, b):
    return jnp.dot(a, b, preferred_element_type=F32)


def _dot_nt(a, b):
    return lax.dot_general(a, b, NT, preferred_element_type=F32)


def _dot_tn(a, b):
    return lax.dot_general(a, b, TN, preferred_element_type=F32)


def _gelu(x):
    return 0.5 * x * (1.0 + jnp.tanh(0.7978845608028654 * (x + 0.044715 * x * x * x)))


def _gelu_and_grad(x):
    k, kc = 0.7978845608028654, 0.7978845608028654 * 0.044715
    x2 = x * x
    th = jnp.tanh(x * (k + kc * x2))
    cdf = 0.5 + 0.5 * th
    return x * cdf, cdf + x * (0.5 - 0.5 * th * th) * (k + 3.0 * kc * x2)


def _sigmoid(x):
    return 1.0 / (1.0 + jnp.exp(-x))


def _rstd(h):
    return lax.rsqrt(jnp.mean(h * h, axis=-1, keepdims=True) + EPS)


def _rms_bwd(h, g, dy):
    r = _rstd(h)
    t = dy * g
    dh = r * t - h * (r * r * r) * jnp.mean(h * t, axis=-1, keepdims=True)
    return dh, jnp.sum(dy * h * r, axis=0, keepdims=True)


def _ordered_after(body, n_in, deps):
    if not deps:
        return body
    return lambda *refs: body(*refs[:n_in], *refs[n_in + len(deps):])


def _accumulate(ref, value):
    @pl.when(pl.program_id(0) == 0)
    def _():
        ref[...] = jnp.zeros_like(ref)

    ref[...] += value


def rms_cast(h, g, name):
    t, d = h.shape
    tm = _tile(t, 512)

    def body(h_ref, g_ref, o_ref):
        hv = h_ref[...]
        o_ref[...] = (hv * _rstd(hv) * g_ref[...]).astype(BF16)

    return pl.pallas_call(
        body, name=name, grid=(t // tm,), in_specs=[_rows(tm, d), _const((1, d))], out_specs=_rows(tm, d),
        out_shape=jax.ShapeDtypeStruct((t, d), BF16), compiler_params=_params("parallel"))(h, g)


def ffn_gu(xn, wgt, wut, name, chip=None, into=None):
    t, d = xn.shape
    nkw, fk, _ = wgt.shape
    tm = _tile(t, 512)
    count = N_CHIPS if chip is None else (1 if into is None else N_CHIPS - 1)
    first = 0 if into is None else 1

    def body(*refs):
        x_ref, wg_ref, wu_ref = refs[-6 - (0 if into is None else 3):][:3]
        a_ref, s_ref, t_ref = refs[-3:]
        xv = x_ref[...]
        g = _dot_nt(xv, wg_ref[0])
        sg = _sigmoid(g)
        s = g * sg
        s_ref[0] = s.astype(BF16)
        ds = sg * (1.0 + g * (1.0 - sg))
        u = _dot_nt(xv, wu_ref[0])
        a_ref[0] = (s * u).astype(BF16)
        t_ref[0] = (u * ds).astype(BF16)

    def slot(k, *chip_ref):
        return (chip_ref[0][0] + first + k) % N_CHIPS if chip_ref else k

    w_spec = pl.BlockSpec((1, fk, d), lambda k, i, *c: (slot(k, *c) if nkw > 1 else 0, 0, 0))
    o_spec = pl.BlockSpec((1, tm, fk), lambda k, i, *c: (slot(k, *c), i, 0))
    shape = jax.ShapeDtypeStruct((N_CHIPS, t, fk), BF16)
    prior = () if into is None else tuple(into)
    call = pl.pallas_call(
        body, name=name,
        grid_spec=pltpu.PrefetchScalarGridSpec(
            num_scalar_prefetch=0 if chip is None else 1, grid=(count, t // tm),
            in_specs=[pl.BlockSpec((tm, d), lambda k, i, *c: (i, 0)), w_spec, w_spec] + [ANY_SPEC] * len(prior),
            out_specs=[o_spec] * 3),
        out_shape=[shape] * 3, input_output_aliases={4 + j: j for j in range(len(prior))},
        compiler_params=_params("parallel", "parallel"))
    return call(*(() if chip is None else (chip,)), xn, wgt, wut, *prior)


def mm_norm_res(a3, w3, h_old, g_post, g_next, scale, name, deps=()):
    nk, t, kb = a3.shape
    d = w3.shape[2]
    tm = _tile(t, 256)

    def body(a_ref, w_ref, h_ref, gp_ref, gn_ref, f_ref, hn_ref, xn_ref):
        f = _dot(a_ref[0], w_ref[0])
        for k in range(1, nk):
            f += _dot(a_ref[k], w_ref[k])
        f_ref[...] = f.astype(BF16)
        hn = h_ref[...] + scale * (f * _rstd(f) * gp_ref[...])
        hn_ref[...] = hn
        xn_ref[...] = (hn * _rstd(hn) * gn_ref[...]).astype(BF16)

    return pl.pallas_call(
        _ordered_after(body, 5, deps), name=name, grid=(t // tm,),
        in_specs=[_kmajor(nk, tm, kb), _const((nk, kb, d)), _rows(tm, d), _const((1, d)), _const((1, d))] + [ANY_SPEC] * len(deps),
        out_specs=[_rows(tm, d)] * 3,
        out_shape=[jax.ShapeDtypeStruct((t, d), BF16), jax.ShapeDtypeStruct((t, d), F32), jax.ShapeDtypeStruct((t, d), BF16)],
        compiler_params=_params("parallel"))(a3, w3, h_old, g_post, g_next, *deps)


def mixer_in(xn, win4, name):
    t, d = xn.shape
    nk, _, nb = win4.shape
    tm = _tile(t, 512)

    def body(x_ref, w_ref, z_ref):
        z_ref[...] = _dot(x_ref[...], w_ref[0]).astype(BF16)

    return pl.pallas_call(
        body, name=name, grid=(nk, t // tm),
        in_specs=[pl.BlockSpec((tm, d), lambda k, i: (i, 0)), pl.BlockSpec((1, d, nb), lambda k, i: (k, 0, 0))],
        out_specs=pl.BlockSpec((tm, nb), lambda k, i: (i, k)), out_shape=jax.ShapeDtypeStruct((t, nk * nb), BF16),
        compiler_params=_params("parallel", "parallel"))(xn, win4)


def _causal_mask():
    row = lax.broadcasted_iota(jnp.int32, (CHUNK, CHUNK), 0)
    col = lax.broadcasted_iota(jnp.int32, (CHUNK, CHUNK), 1)
    return row >= col


def _layernorm_parts(v):
    mu = jnp.mean(v, axis=-1, keepdims=True)
    vc = v - mu
    r = lax.rsqrt(jnp.mean(vc * vc, axis=-1, keepdims=True) + EPS)
    return vc * r, r


def sgu_fwd(z, norm_g, sgu_w, sgu_b3, name):
    t = z.shape[0]
    d = norm_g.shape[1]
    ng = sgu_w.shape[0]
    dg = d // ng
    tm = _tile(t, 256)

    def body(zu_ref, zv_ref, ng_ref, w_ref, b_ref, a_ref):
        vhat, _ = _layernorm_parts(_gelu(zv_ref[...].astype(F32)))
        vn = (vhat * ng_ref[...]).astype(BF16)
        u = _gelu(zu_ref[...].astype(F32))
        mask = _causal_mask()
        for g in range(ng):
            wg = jnp.where(mask, w_ref[g], 0.0).astype(BF16)
            for ci in range(tm // CHUNK):
                rs, cs = slice(ci * CHUNK, (ci + 1) * CHUNK), slice(g * dg, (g + 1) * dg)
                sv = _dot(wg, vn[rs, cs]) + b_ref[g]
                a_ref[rs, cs] = (u[rs, cs] * sv).astype(BF16)

    return pl.pallas_call(
        body, name=name, grid=(t // tm,),
        in_specs=[_rows(tm, d, 0), _rows(tm, d, 1), _const((1, d)), _const((ng, CHUNK, CHUNK)), _const((ng, CHUNK, 1))],
        out_specs=_rows(tm, d), out_shape=jax.ShapeDtypeStruct((t, d), BF16),
        compiler_params=_params("parallel"))(z, z, norm_g, sgu_w, sgu_b3)


def pool_fwd(z, pool_w, pool_scale, name):
    t = z.shape[0]
    d = pool_scale.shape[1]
    ng = pool_w.shape[0]
    dg = d // ng
    tm = _tile(t, 256)
    per = tm // HALO

    def body(c_ref, prev_ref, w_ref, s_ref, diff_ref, b_ref):
        i = pl.program_id(0)
        cur = c_ref[...].astype(F32)
        prev = jnp.where(i > 0, prev_ref[...].astype(F32), 0.0)
        ext = jnp.concatenate([prev, cur], axis=0)
        tok = i * tm + lax.broadcasted_iota(jnp.int32, (tm, 1), 0)
        for g, win in enumerate(POOL_WINDOWS):
            cs = slice(g * dg, (g + 1) * dg)
            s = ext[:, cs]
            sh = 1
            while sh < win:
                s = s + pltpu.roll(s, sh, 0)
                sh *= 2
            per_count = 1.0 / jnp.minimum(tok + 1, win).astype(F32)
            diff = (s[HALO:] * per_count - cur[:, cs]).astype(BF16)
            diff_ref[:, cs] = diff
            b_ref[:, cs] = (_dot(diff, w_ref[g]) * s_ref[:, cs]).astype(BF16)

    return pl.pallas_call(
        body, name=name, grid=(t // tm,),
        in_specs=[_rows(tm, d, 2), pl.BlockSpec((HALO, d), lambda i: (jnp.maximum(i * per - 1, 0), 2)),
                  _const((ng, dg, dg)), _const((1, d))],
        out_specs=[_rows(tm, d)] * 2, out_shape=[jax.ShapeDtypeStruct((t, d), BF16)] * 2,
        compiler_params=_params("parallel"))(z, z, pool_w, pool_scale)


def mixer_y(a, b, z, woa, wob, name):
    t, d = a.shape
    tm = _tile(t, 256)

    def body(a_ref, b_ref, ga_ref, gb_ref, wa_ref, wb_ref, ya_ref, yb_ref, y_ref):
        ya = _dot(a_ref[...], wa_ref[...])
        yb = _dot(b_ref[...], wb_ref[...])
        ya_ref[...] = ya.astype(BF16)
        yb_ref[...] = yb.astype(BF16)
        y_ref[...] = (_sigmoid(ga_ref[...].astype(F32)) * ya + _sigmoid(gb_ref[...].astype(F32)) * yb).astype(BF16)

    return pl.pallas_call(
        body, name=name, grid=(t // tm,),
        in_specs=[_rows(tm, d), _rows(tm, d), _rows(tm, d, 3), _rows(tm, d, 4), _const((d, d)), _const((d, d))],
        out_specs=[_rows(tm, d)] * 3, out_shape=[jax.ShapeDtypeStruct((t, d), BF16)] * 3,
        compiler_params=_params("parallel"))(a, b, z, z, woa, wob)


def ple_loss(xn, p, wpg, wpp, h, g_post, g_pre, target, name):
    t, d = xn.shape
    dp = p.shape[1]
    tm = _tile(t, 256)

    def body(x_ref, p_ref, wg_ref, wp_ref, h_ref, gp_ref, gn_ref, tg_ref, ds_ref, de_ref, dhp_ref, dgp_ref, dgn_ref, loss_ref):
        gate = _sigmoid(_dot(x_ref[...], wg_ref[...]))
        e = _dot(p_ref[...], wp_ref[...])
        q = gate * e
        hv = h_ref[...]
        err = hv + q * _rstd(q) * gp_ref[...] - tg_ref[...]
        _accumulate(loss_ref, jnp.full(loss_ref.shape, (0.5 / d) * jnp.sum(err * err), F32))
        dhv = err * (1.0 / d)
        dq, dgp = _rms_bwd(q, gp_ref[...], dhv)
        ds = (dq * e * gate * (1.0 - gate)).astype(BF16)
        ds_ref[...] = ds
        de_ref[...] = (dq * gate).astype(BF16)
        dx, dgn = _rms_bwd(hv, gn_ref[...], _dot_nt(ds, wg_ref[...]))
        dhp_ref[...] = dhv + dx
        _accumulate(dgp_ref, dgp)
        _accumulate(dgn_ref, dgn)

    return pl.pallas_call(
        body, name=name, grid=(t // tm,),
        in_specs=[_rows(tm, d), _rows(tm, dp), _const((d, d)), _const((dp, d)), _rows(tm, d), _const((1, d)), _const((1, d)),
                  _rows(tm, d)],
        out_specs=[_rows(tm, d)] * 3 + [_const((1, d))] * 2 + [_const((8, 128))],
        out_shape=[jax.ShapeDtypeStruct((t, d), BF16), jax.ShapeDtypeStruct((t, d), BF16), jax.ShapeDtypeStruct((t, d), F32),
                   jax.ShapeDtypeStruct((1, d), F32), jax.ShapeDtypeStruct((1, d), F32), jax.ShapeDtypeStruct((8, 128), F32)],
        compiler_params=_params("arbitrary"))(xn, p, wpg, wpp, h, g_post, g_pre, target)


def ffn_bwd_a(dh, f, g_post, wd4, s4, t4, scale, name, deps=()):
    t, d = dh.shape
    nk, fk, _ = wd4.shape
    tm = _tile(t, 256)

    def body(dh_ref, f_ref, gp_ref, w_ref, s_ref, t_ref, df_ref, dg_ref, du_ref, dgp_ref):
        df, dgp = _rms_bwd(f_ref[...].astype(F32), gp_ref[...], dh_ref[...])
        df = (scale * df).astype(BF16)
        df_ref[...] = df
        _accumulate(dgp_ref, scale * dgp)
        for k in range(nk):
            da = _dot_nt(df, w_ref[k])
            du_ref[k] = (da * s_ref[k].astype(F32)).astype(BF16)
            dg_ref[k] = (da * t_ref[k].astype(F32)).astype(BF16)

    return pl.pallas_call(
        _ordered_after(body, 6, deps), name=name, grid=(t // tm,),
        in_specs=[_rows(tm, d), _rows(tm, d), _const((1, d)), _const((nk, fk, d)), _kmajor(nk, tm, fk), _kmajor(nk, tm, fk)]
        + [ANY_SPEC] * len(deps),
        out_specs=[_rows(tm, d), _kmajor(nk, tm, fk), _kmajor(nk, tm, fk), _const((1, d))],
        out_shape=[jax.ShapeDtypeStruct((t, d), BF16), jax.ShapeDtypeStruct((nk, t, fk), BF16),
                   jax.ShapeDtypeStruct((nk, t, fk), BF16), jax.ShapeDtypeStruct((1, d), F32)],
        compiler_params=_params("arbitrary"))(dh, f, g_post, wd4, s4, t4, *deps)


def dx_norm_bwd(pairs, h, g_pre, dh_in, name, deps=()):
    t, d = h.shape
    tm = _tile(t, 256)
    n = len(pairs)

    def body(*refs):
        dys, ws = refs[:n], refs[n:2 * n]
        h_ref, g_ref, dhi_ref, dho_ref, dg_ref = refs[2 * n:]
        acc = None
        for (_, w4, sections), dy_ref, w_ref in zip(pairs, dys, ws):
            if sections:
                wide = w4.shape[2]
                edges = sorted(set(range(0, 5 * d + 1, d)) | set(range(0, 5 * d + 1, wide)))
                parts = [_dot_nt(dy_ref[DZ_SLOT[lo // d], :, lo % d:lo % d + hi - lo], w_ref[lo // wide, :, lo % wide:lo % wide + hi - lo])
                         for lo, hi in zip(edges[:-1], edges[1:])]
            else:
                parts = [_dot(dy_ref[k], w_ref[k]) for k in range(w4.shape[0])]
            for part in parts:
                acc = part if acc is None else acc + part
        dx, dg = _rms_bwd(h_ref[...], g_ref[...], acc)
        dho_ref[...] = dhi_ref[...] + dx
        _accumulate(dg_ref, dg)

    dy_specs = [_kmajor(dy.shape[0], tm, dy.shape[2]) for dy, _, _ in pairs]
    return pl.pallas_call(
        _ordered_after(body, 2 * n + 3, deps), name=name, grid=(t // tm,),
        in_specs=dy_specs + [_const(w4.shape) for _, w4, _ in pairs] + [_rows(tm, d), _const((1, d)), _rows(tm, d)]
        + [ANY_SPEC] * len(deps),
        out_specs=[_rows(tm, d), _const((1, d))],
        out_shape=[jax.ShapeDtypeStruct((t, d), F32), jax.ShapeDtypeStruct((1, d), F32)],
        compiler_params=_params("arbitrary"))(*[dy for dy, _, _ in pairs], *[w4 for _, w4, _ in pairs], h, g_pre, dh_in, *deps)


def mixer_bwd_y(dh, m, g_post, w_o, ya, yb, z, woa, wob, name, deps=()):
    t, d = dh.shape
    tm = _tile(t, 256)

    def body(dh_ref, m_ref, gp_ref, wo_ref, ya_ref, yb_ref, ga_ref, gb_ref, wa_ref, wb_ref,
             dm_ref, dya_ref, dyb_ref, dz_ref, da_ref, db_ref, dgp_ref):
        dm, dgp = _rms_bwd(m_ref[...].astype(F32), gp_ref[...], dh_ref[...])
        dm = dm.astype(BF16)
        dm_ref[...] = dm
        _accumulate(dgp_ref, dgp)
        dy = _dot_nt(dm, wo_ref[...])
        sa = _sigmoid(ga_ref[...].astype(F32))
        sb = _sigmoid(gb_ref[...].astype(F32))
        dya = (dy * sa).astype(BF16)
        dyb = (dy * sb).astype(BF16)
        dya_ref[...] = dya
        dyb_ref[...] = dyb
        dz_ref[0] = (dy * ya_ref[...].astype(F32) * sa * (1.0 - sa)).astype(BF16)
        dz_ref[1] = (dy * yb_ref[...].astype(F32) * sb * (1.0 - sb)).astype(BF16)
        da_ref[...] = _dot_nt(dya, wa_ref[...]).astype(BF16)
        db_ref[...] = _dot_nt(dyb, wb_ref[...]).astype(BF16)

    return pl.pallas_call(
        _ordered_after(body, 10, deps), name=name, grid=(t // tm,),
        in_specs=[_rows(tm, d), _rows(tm, d), _const((1, d)), _const((d, d)), _rows(tm, d), _rows(tm, d),
                  _rows(tm, d, 3), _rows(tm, d, 4), _const((d, d)), _const((d, d))] + [ANY_SPEC] * len(deps),
        out_specs=[_rows(tm, d)] * 3 + [pl.BlockSpec((2, tm, d), lambda i: (0, i, 0))] + [_rows(tm, d)] * 2 + [_const((1, d))],
        out_shape=[jax.ShapeDtypeStruct((t, d), BF16)] * 3 + [jax.ShapeDtypeStruct((5, t, d), BF16)]
        + [jax.ShapeDtypeStruct((t, d), BF16)] * 2 + [jax.ShapeDtypeStruct((1, d), F32)],
        compiler_params=_params("arbitrary"))(dh, m, g_post, w_o, ya, yb, z, z, woa, wob, *deps)


def sgu_bwd(z, da, dz, norm_g, sgu_w, sgu_b3, name):
    t, d = da.shape
    ng = sgu_w.shape[0]
    dg = d // ng
    tm = _tile(t, 256)
    steps = t // tm

    def body(zu_ref, zv_ref, da_ref, ng_ref, w_ref, b_ref, _, dz_ref, dw_ref, db_ref, dng_ref, dvn_ref, dsv_ref):
        i = pl.program_id(0)
        zv = zv_ref[...].astype(F32)
        zu = zu_ref[...].astype(F32)
        v, gv = _gelu_and_grad(zv)
        vhat, r = _layernorm_parts(v)
        gain = ng_ref[...]
        vn = (vhat * gain).astype(BF16)
        u, gu = _gelu_and_grad(zu)
        dav = da_ref[...].astype(F32)
        mask = _causal_mask()

        @pl.when(i == 0)
        def _():
            dw_ref[...] = jnp.zeros_like(dw_ref)
            dsv_ref[...] = jnp.zeros_like(dsv_ref)

        for g in range(ng):
            wg = jnp.where(mask, w_ref[g], 0.0).astype(BF16)
            dw = jnp.zeros((CHUNK, CHUNK), F32)
            dsv_sum = jnp.zeros((CHUNK, dg), F32)
            for ci in range(tm // CHUNK):
                rs, cs = slice(ci * CHUNK, (ci + 1) * CHUNK), slice(g * dg, (g + 1) * dg)
                vn_blk = vn[rs, cs]
                sv = _dot(wg, vn_blk) + b_ref[g]
                dz_ref[0, rs, cs] = (dav[rs, cs] * sv * gu[rs, cs]).astype(BF16)
                dsv = dav[rs, cs] * u[rs, cs]
                dsv_sum += dsv
                dsv = dsv.astype(BF16)
                dw += _dot_nt(dsv, vn_blk)
                dvn_ref[rs, cs] = _dot_tn(wg, dsv)
            dw_ref[g] += dw
            dsv_ref[:, cs] += dsv_sum

        dvn = dvn_ref[...]
        _accumulate(dng_ref, jnp.sum(dvn * vhat, axis=0, keepdims=True))
        dvh = dvn * gain
        dv = r * (dvh - jnp.mean(dvh, axis=-1, keepdims=True) - vhat * jnp.mean(dvh * vhat, axis=-1, keepdims=True))
        dz_ref[1] = (dv * gv).astype(BF16)

        @pl.when(i == steps - 1)
        def _():
            for g in range(ng):
                dw_ref[g] = jnp.where(mask, dw_ref[g], 0.0)
                row_sum = jnp.sum(dsv_ref[:, g * dg:(g + 1) * dg], axis=1, keepdims=True)
                db_ref[g] = jnp.broadcast_to(row_sum, (CHUNK, CHUNK))

    return pl.pallas_call(
        body, name=name, grid=(steps,),
        in_specs=[_rows(tm, d, 0), _rows(tm, d, 1), _rows(tm, d), _const((1, d)), _const((ng, CHUNK, CHUNK)), _const((ng, CHUNK, 1)),
                  ANY_SPEC],
        out_specs=[pl.BlockSpec((2, tm, d), lambda i: (DZ_SLOT[0] // 2, i, 0)), _const((ng, CHUNK, CHUNK)),
                   _const((ng, CHUNK, CHUNK)), _const((1, d))],
        out_shape=[jax.ShapeDtypeStruct(dz.shape, BF16), jax.ShapeDtypeStruct((ng, CHUNK, CHUNK), F32),
                   jax.ShapeDtypeStruct((ng, CHUNK, CHUNK), F32), jax.ShapeDtypeStruct((1, d), F32)],
        scratch_shapes=[pltpu.VMEM((tm, d), F32), pltpu.VMEM((CHUNK, d), F32)], input_output_aliases={6: 0},
        compiler_params=_params("arbitrary"))(z, z, da, norm_g, sgu_w, sgu_b3, dz)


def pool_bwd(db, diff, dz, pool_w, pool_scale, name):
    t, d = db.shape
    ng = pool_w.shape[0]
    dg = d // ng
    tm = _tile(t, 256)
    per = tm // HALO
    steps = t // tm

    def body(db_ref, next_ref, diff_ref, w_ref, s_ref, _, dc_ref, dw_ref, ds_ref):
        i = pl.program_id(0)
        dbc = db_ref[...].astype(F32)
        nxt = jnp.where(i < steps - 1, next_ref[...].astype(F32), 0.0)
        ext = jnp.concatenate([dbc, nxt], axis=0)
        rows = tm + HALO
        tok = i * tm + lax.broadcasted_iota(jnp.int32, (rows, 1), 0)

        @pl.when(i == 0)
        def _():
            dw_ref[...] = jnp.zeros_like(dw_ref)
            ds_ref[...] = jnp.zeros_like(ds_ref)

        for g, win in enumerate(POOL_WINDOWS):
            cs = slice(g * dg, (g + 1) * dg)
            dp = (ext[:, cs] * s_ref[:, cs]).astype(BF16)
            dd = _dot_nt(dp, w_ref[g])
            s = dd * (1.0 / jnp.minimum(tok + 1, win).astype(F32))
            sh = 1
            while sh < win:
                s = s + pltpu.roll(s, rows - sh, 0)
                sh *= 2
            dc_ref[0, :, cs] = (s[:tm] - dd[:tm]).astype(BF16)
            dfg = diff_ref[:, cs]
            ds_ref[:, cs] += jnp.sum(dbc[:, cs] * _dot(dfg, w_ref[g]), axis=0, keepdims=True)
            dw_ref[g] += _dot_tn(dfg, dp[:tm])

    return pl.pallas_call(
        body, name=name, grid=(steps,),
        in_specs=[_rows(tm, d), pl.BlockSpec((HALO, d), lambda i: (jnp.minimum((i + 1) * per, t // HALO - 1), 0)),
                  _rows(tm, d), _const((ng, dg, dg)), _const((1, d)), ANY_SPEC],
        out_specs=[pl.BlockSpec((1, tm, d), lambda i: (DZ_SLOT[2], i, 0)), _const((ng, dg, dg)), _const((1, d))],
        out_shape=[jax.ShapeDtypeStruct(dz.shape, BF16), jax.ShapeDtypeStruct((ng, dg, dg), F32), jax.ShapeDtypeStruct((1, d), F32)],
        input_output_aliases={5: 0},
        compiler_params=_params("arbitrary"))(db, db, diff, pool_w, pool_scale, dz)


def dw_tn(x, dy, nk, name, x_kmajor=False, dy_mode="same"):
    t = x.shape[-2]
    kx = x.shape[-1]
    n = dy.shape[-1] // nk if dy_mode == "cols" else dy.shape[-1]
    tt = _tile(t, DW_TOKENS)
    steps = t // tt

    def body(x_ref, dy_ref, o_ref, acc_ref):
        s = pl.program_id(1)
        part = _dot_tn(x_ref[0] if x_kmajor else x_ref[...], dy_ref[0] if dy_mode == "kmajor" else dy_ref[...])
        if steps == 1:
            o_ref[0] = part.astype(BF16)
            return

        @pl.when(s == 0)
        def _():
            acc_ref[...] = jnp.zeros_like(acc_ref)

        acc_ref[...] += part

        @pl.when(s == steps - 1)
        def _():
            o_ref[0] = acc_ref[...].astype(BF16)

    x_spec = pl.BlockSpec((1, tt, kx), lambda k, s: (k, s, 0)) if x_kmajor else pl.BlockSpec((tt, kx), lambda k, s: (s, 0))
    dy_spec = {"kmajor": pl.BlockSpec((1, tt, n), lambda k, s: (k, s, 0)), "cols": pl.BlockSpec((tt, n), lambda k, s: (s, k)),
               "same": pl.BlockSpec((tt, n), lambda k, s: (s, 0))}[dy_mode]
    return pl.pallas_call(
        body, name=name, grid=(nk, steps), in_specs=[x_spec, dy_spec],
        out_specs=pl.BlockSpec((1, kx, n), lambda k, s: (k, 0, 0)), out_shape=jax.ShapeDtypeStruct((nk, kx, n), BF16),
        scratch_shapes=[pltpu.VMEM((kx, n) if steps > 1 else (8, 128), F32)],
        compiler_params=_params("parallel", "arbitrary"))(x, dy)


def dw_in_tiles(xn, dz, nk, name):
    t, d = xn.shape
    sections = len(DZ_SLOT)
    per_section = d // DW_IN_TILE
    per_shard = sections * per_section // nk

    def body(x_ref, dy_ref, o_ref):
        o_ref[0] = _dot_tn(x_ref[...], dy_ref[0]).astype(BF16)

    def slot(j):
        return (j // per_section + DZ_SLOT[0]) % sections

    return pl.pallas_call(
        body, name=name, grid=(sections * per_section,),
        in_specs=[pl.BlockSpec((t, d), lambda j: (0, 0)), pl.BlockSpec((1, t, DW_IN_TILE), lambda j: (slot(j), 0, j % per_section))],
        out_specs=pl.BlockSpec((1, d, DW_IN_TILE), lambda j: (j // per_shard, 0, j % per_shard)),
        out_shape=jax.ShapeDtypeStruct((nk, d, sections * d // nk), BF16), compiler_params=_params("parallel"))(xn, dz)


def _place():
    x, y, c = lax.axis_index("x"), lax.axis_index("y"), lax.axis_index("c")
    chips = [((1 - x) if fx else x, (1 - y) if fy else y) for fx, fy in OTHER_CHIPS]
    return x, y, c, chips


def _sibling_handshake(x, y, c):
    barrier = pltpu.get_barrier_semaphore()
    pl.semaphore_signal(barrier, inc=1, device_id=(x, y, 1 - c), device_id_type=MESH_IDS)
    pl.semaphore_wait(barrier, 1)


def _half(rows, which):
    return pl.ds(pl.multiple_of(which * (rows // 2), 16), rows // 2)


def _hbm(a):
    return pltpu.with_memory_space_constraint(a, pltpu.HBM)


def _by_shape(arrays):
    buckets = {}
    for i, a in enumerate(arrays):
        buckets.setdefault(a.shape, []).append(i)
    return list(buckets.values())


def cast_place(shards, chip, name, deps=(), plain=False):
    n = len(shards)

    def body(chip_ref, *refs):
        outs = refs[n + len(deps):]
        for a, w_ref in enumerate(refs[:n]):
            cast = w_ref[...].astype(BF16)
            outs[a][0] = cast
            if plain:
                outs[n + a][...] = cast

    def rows(s):
        return (s.shape[0] // ROW_STEPS, s.shape[1])

    out = pl.pallas_call(
        body, name=name,
        grid_spec=pltpu.PrefetchScalarGridSpec(
            num_scalar_prefetch=1, grid=(ROW_STEPS,),
            in_specs=[pl.BlockSpec(rows(s), lambda i, chip_ref: (i, 0)) for s in shards] + [ANY_SPEC] * len(deps),
            out_specs=[pl.BlockSpec((1,) + rows(s), lambda i, chip_ref: (chip_ref[0], i, 0)) for s in shards]
            + [pl.BlockSpec(rows(s), lambda i, chip_ref: (i, 0)) for s in shards] * plain),
        out_shape=[jax.ShapeDtypeStruct((N_CHIPS,) + s.shape, BF16) for s in shards]
        + [jax.ShapeDtypeStruct(s.shape, BF16) for s in shards] * plain,
        compiler_params=_params("parallel"))(chip, *shards, *deps)
    return (out[:n], out[n:]) if plain else out


def _gather_copy(buf, sends, recvs, i, j, me, chip_xy, c):
    cx, cy = chip_xy
    mine = _half(buf.shape[1], c)
    return pltpu.make_async_remote_copy(
        src_ref=buf.at[me, mine], dst_ref=buf.at[me, mine], send_sem=sends.at[3 * i + j], recv_sem=recvs.at[3 * i + j],
        device_id=(cx, cy, c), device_id_type=MESH_IDS)


def allgather_start(bufs, name, deps=()):
    n = len(bufs)
    extra = len(deps)

    def body(*refs):
        ins = refs[:n]
        sends, recvs = refs[n + extra], refs[n + extra + 1]
        token = refs[2 * n + extra + 2]
        x, y, c, chips = _place()
        for i in range(n):
            for j, chip_xy in enumerate(chips):
                _gather_copy(ins[i], sends, recvs, i, j, 2 * x + y, chip_xy, c).start()
        token[...] = jnp.zeros_like(token)

    out = pl.pallas_call(
        body, name=name, in_specs=[HBM_SPEC] * n + [ANY_SPEC] * extra,
        out_specs=[SEM_SPEC, SEM_SPEC] + [HBM_SPEC] * n + [VMEM_SPEC],
        out_shape=[pltpu.SemaphoreType.DMA((3 * n,)), pltpu.SemaphoreType.DMA((3 * n,))]
        + [pltpu.HBM(b.shape, b.dtype) for b in bufs] + [jax.ShapeDtypeStruct((8, 128), F32)],
        input_output_aliases={i: i + 2 for i in range(n)},
        compiler_params=pltpu.CompilerParams(has_side_effects=DATAFLOW))(*[_hbm(b) for b in bufs], *deps)
    return out[0], out[1], list(out[2:2 + n]), out[2 + n]


def allgather_wait(sends, recvs, bufs, after, name):
    n = len(bufs)
    after = tuple(after) if isinstance(after, (tuple, list)) else (after,)

    def body(*refs):
        ins = refs[:n]
        send_sems, recv_sems = refs[n], refs[n + 1]
        x, y, c, chips = _place()
        for i in range(n):
            for j, (cx, cy) in enumerate(chips):
                mine = _half(ins[i].shape[1], c)
                cp = pltpu.make_async_remote_copy(
                    src_ref=ins[i].at[2 * x + y, mine], dst_ref=ins[i].at[2 * cx + cy, mine], send_sem=send_sems.at[3 * i + j],
                    recv_sem=recv_sems.at[3 * i + j], device_id=(cx, cy, c), device_id_type=MESH_IDS)
                cp.wait_send()
                cp.wait_recv()

    return pl.pallas_call(
        body, name=name, in_specs=[HBM_SPEC] * n + [SEM_SPEC, SEM_SPEC] + [ANY_SPEC] * len(after), out_specs=[HBM_SPEC] * n,
        out_shape=[pltpu.HBM(b.shape, b.dtype) for b in bufs], input_output_aliases={i: i for i in range(n)},
        compiler_params=pltpu.CompilerParams(has_side_effects=DATAFLOW))(*bufs, sends, recvs, *after)


def d2d_forward(bufs, name):
    n = len(bufs)

    def body(*refs):
        ins = refs[:n]
        send_sems, recv_sems = refs[2 * n:]
        x, y, c, chips = _place()
        _sibling_handshake(x, y, c)
        copies = []
        for i in range(n):
            mine = _half(ins[i].shape[1], c)
            for j, (cx, cy) in enumerate(chips):
                landed = ins[i].at[2 * cx + cy, mine]
                cp = pltpu.make_async_remote_copy(
                    src_ref=landed, dst_ref=landed, send_sem=send_sems.at[i, j], recv_sem=recv_sems.at[i, j],
                    device_id=(x, y, 1 - c), device_id_type=MESH_IDS)
                cp.start()
                copies.append(cp)
        for i in range(n):
            theirs = _half(ins[i].shape[1], 1 - c)
            for j, (cx, cy) in enumerate(chips):
                passed = ins[i].at[2 * cx + cy, theirs]
                pltpu.make_async_remote_copy(
                    src_ref=passed, dst_ref=passed, send_sem=send_sems.at[i, j], recv_sem=recv_sems.at[i, j],
                    device_id=(x, y, 1 - c), device_id_type=MESH_IDS).wait_recv()
        for cp in copies:
            cp.wait_send()

    return pl.pallas_call(
        body, name=name, in_specs=[HBM_SPEC] * n, out_specs=[HBM_SPEC] * n,
        out_shape=[jax.ShapeDtypeStruct(b.shape, b.dtype) for b in bufs], input_output_aliases={i: i for i in range(n)},
        scratch_shapes=[pltpu.SemaphoreType.DMA((n, 3))] * 2,
        compiler_params=pltpu.CompilerParams(has_side_effects=True, collective_id=SIBLING_BARRIER))(*bufs)


def _forward_copy(buf, sends, recvs, i, j, chip_xy, x, y, c):
    cx, cy = chip_xy
    rows = buf.shape[1]
    return pltpu.make_async_remote_copy(
        src_ref=buf.at[2 * cx + cy, _half(rows, c)], dst_ref=buf.at[2 * cx + cy, _half(rows, 1 - c)],
        send_sem=sends.at[3 * i + j], recv_sem=recvs.at[3 * i + j], device_id=(x, y, 1 - c), device_id_type=MESH_IDS)


def d2d_forward_start(bufs, name):
    n = len(bufs)

    def body(*refs):
        ins = refs[:n]
        sends, recvs = refs[n], refs[n + 1]
        token = refs[2 * n + 2]
        x, y, c, chips = _place()
        _sibling_handshake(x, y, c)
        for i in range(n):
            mine = _half(ins[i].shape[1], c)
            for j, (cx, cy) in enumerate(chips):
                landed = ins[i].at[2 * cx + cy, mine]
                pltpu.make_async_remote_copy(
                    src_ref=landed, dst_ref=landed, send_sem=sends.at[3 * i + j], recv_sem=recvs.at[3 * i + j],
                    device_id=(x, y, 1 - c), device_id_type=MESH_IDS).start()
        token[...] = jnp.zeros_like(token)

    out = pl.pallas_call(
        body, name=name, in_specs=[HBM_SPEC] * n,
        out_specs=[SEM_SPEC, SEM_SPEC] + [HBM_SPEC] * n + [VMEM_SPEC],
        out_shape=[pltpu.SemaphoreType.DMA((3 * n,)), pltpu.SemaphoreType.DMA((3 * n,))]
        + [pltpu.HBM(b.shape, b.dtype) for b in bufs] + [jax.ShapeDtypeStruct((8, 128), F32)],
        input_output_aliases={i: i + 2 for i in range(n)},
        compiler_params=pltpu.CompilerParams(has_side_effects=DATAFLOW, collective_id=SIBLING_BARRIER))(*[_hbm(b) for b in bufs])
    return (out[0], out[1], list(out[2:2 + n])), out[2 + n]


def d2d_forward_wait(started, after, name):
    sends, recvs, bufs = started
    n = len(bufs)

    def body(*refs):
        ins = refs[:n]
        send_sems, recv_sems = refs[n], refs[n + 1]
        x, y, c, chips = _place()
        for i in range(n):
            for j, chip_xy in enumerate(chips):
                cp = _forward_copy(ins[i], send_sems, recv_sems, i, j, chip_xy, x, y, c)
                cp.wait_send()
                cp.wait_recv()

    return pl.pallas_call(
        body, name=name, in_specs=[HBM_SPEC] * n + [SEM_SPEC, SEM_SPEC, ANY_SPEC], out_specs=[HBM_SPEC] * n,
        out_shape=[pltpu.HBM(b.shape, b.dtype) for b in bufs], input_output_aliases={i: i for i in range(n)},
        compiler_params=pltpu.CompilerParams(has_side_effects=DATAFLOW))(*bufs, sends, recvs, after)


def _sibling_copy(src, land, sends, recvs, i, x, y, c, halves):
    part = src.at[:, _half(src.shape[1], 1 - c)] if halves else src
    return pltpu.make_async_remote_copy(
        src_ref=part, dst_ref=land, send_sem=sends.at[i], recv_sem=recvs.at[i], device_id=(x, y, 1 - c),
        device_id_type=MESH_IDS)


def sibling_start(arrays, halves, name):
    n = len(arrays)
    lands = [lax.empty((a.shape[0], a.shape[1] // 2, a.shape[2]) if halves else a.shape, a.dtype) for a in arrays]

    def body(*refs):
        srcs, zones = refs[:n], refs[n:2 * n]
        sends, recvs = refs[2 * n], refs[2 * n + 1]
        token = refs[4 * n + 2]
        x, y, c, _ = _place()
        _sibling_handshake(x, y, c)
        for i in range(n):
            _sibling_copy(srcs[i], zones[i], sends, recvs, i, x, y, c, halves).start()
        token[...] = jnp.zeros_like(token)

    out = pl.pallas_call(
        body, name=name, in_specs=[HBM_SPEC] * (2 * n),
        out_specs=[SEM_SPEC, SEM_SPEC] + [HBM_SPEC] * (2 * n) + [VMEM_SPEC],
        out_shape=[pltpu.SemaphoreType.DMA((n,)), pltpu.SemaphoreType.DMA((n,))]
        + [pltpu.HBM(a.shape, a.dtype) for a in arrays + lands] + [jax.ShapeDtypeStruct((8, 128), F32)],
        input_output_aliases={i: i + 2 for i in range(2 * n)},
        compiler_params=pltpu.CompilerParams(has_side_effects=DATAFLOW, collective_id=SIBLING_BARRIER))(
            *[_hbm(a) for a in arrays + lands])
    return (out[0], out[1], list(out[2:2 + n]), list(out[2 + n:2 + 2 * n])), out[2 + 2 * n]


def sibling_wait(started, halves, after, name):
    sends, recvs, arrays, lands = started
    n = len(arrays)
    after = tuple(after) if isinstance(after, (tuple, list)) else (after,)

    def body(*refs):
        srcs, zones = refs[:n], refs[n:2 * n]
        send_sems, recv_sems = refs[2 * n], refs[2 * n + 1]
        x, y, c, _ = _place()
        for i in range(n):
            cp = _sibling_copy(srcs[i], zones[i], send_sems, recv_sems, i, x, y, c, halves)
            cp.wait_send()
            cp.wait_recv()

    out = pl.pallas_call(
        body, name=name, in_specs=[HBM_SPEC] * (2 * n) + [SEM_SPEC, SEM_SPEC] + [ANY_SPEC] * len(after),
        out_specs=[HBM_SPEC] * (2 * n),
        out_shape=[pltpu.HBM(a.shape, a.dtype) for a in arrays + lands], input_output_aliases={i: i for i in range(2 * n)},
        compiler_params=pltpu.CompilerParams(has_side_effects=DATAFLOW))(*arrays, *lands, sends, recvs, *after)
    return list(out[:n]), list(out[n:])


def _scatter_copy(src, land, sends, recvs, i, j, chip_xy, c):
    cx, cy = chip_xy
    return pltpu.make_async_remote_copy(
        src_ref=src.at[2 * cx + cy], dst_ref=land.at[j], send_sem=sends.at[3 * i + j], recv_sem=recvs.at[3 * i + j],
        device_id=(cx, cy, c), device_id_type=MESH_IDS)


def scatter_start(sums, name):
    n = len(sums)
    lands = [lax.empty((3,) + s.shape[1:], s.dtype) for s in sums]

    def body(*refs):
        srcs, zones = refs[:n], refs[n:2 * n]
        sends, recvs = refs[2 * n], refs[2 * n + 1]
        token = refs[4 * n + 2]
        _, _, c, chips = _place()
        for i in range(n):
            for j, chip_xy in enumerate(chips):
                _scatter_copy(srcs[i], zones[i], sends, recvs, i, j, chip_xy, c).start()
        token[...] = jnp.zeros_like(token)

    out = pl.pallas_call(
        body, name=name, in_specs=[HBM_SPEC] * (2 * n),
        out_specs=[SEM_SPEC, SEM_SPEC] + [HBM_SPEC] * (2 * n) + [VMEM_SPEC],
        out_shape=[pltpu.SemaphoreType.DMA((3 * n,)), pltpu.SemaphoreType.DMA((3 * n,))]
        + [pltpu.HBM(a.shape, a.dtype) for a in sums + lands] + [jax.ShapeDtypeStruct((8, 128), F32)],
        input_output_aliases={i: i + 2 for i in range(2 * n)},
        compiler_params=pltpu.CompilerParams(has_side_effects=DATAFLOW))(*[_hbm(a) for a in sums + lands])
    return out[0], out[1], list(out[2:2 + n]), list(out[2 + n:2 + 2 * n]), out[2 + 2 * n]


def scatter_wait(sends, recvs, sums, lands, after, name):
    n = len(sums)

    def body(*refs):
        srcs, zones = refs[:n], refs[n:2 * n]
        send_sems, recv_sems = refs[2 * n], refs[2 * n + 1]
        _, _, c, chips = _place()
        for i in range(n):
            for j, chip_xy in enumerate(chips):
                cp = _scatter_copy(srcs[i], zones[i], send_sems, recv_sems, i, j, chip_xy, c)
                cp.wait_send()
                cp.wait_recv()

    out = pl.pallas_call(
        body, name=name, in_specs=[HBM_SPEC] * (2 * n) + [SEM_SPEC, SEM_SPEC, ANY_SPEC], out_specs=[HBM_SPEC] * (2 * n),
        out_shape=[pltpu.HBM(a.shape, a.dtype) for a in sums + lands], input_output_aliases={i: i for i in range(2 * n)},
        compiler_params=pltpu.CompilerParams(has_side_effects=DATAFLOW))(*sums, *lands, sends, recvs, after)
    return list(out[:n]), list(out[n:])


def add_halves(grads, recvs, core, name):
    n = len(grads)
    nk = grads[0].shape[0]
    views = [g.reshape(nk, 2, g.shape[1] // 2, g.shape[2]) for g in grads]

    def body(core_ref, *refs):
        for g_ref, r_ref, o_ref in zip(refs[:n], refs[n:2 * n], refs[2 * n:]):
            o_ref[0] = (g_ref[0, 0].astype(F32) + r_ref[0].astype(F32)).astype(BF16)

    return pl.pallas_call(
        body, name=name,
        grid_spec=pltpu.PrefetchScalarGridSpec(
            num_scalar_prefetch=1, grid=(nk,),
            in_specs=[pl.BlockSpec((1, 1) + v.shape[2:], lambda k, core_ref: (k, core_ref[0], 0, 0)) for v in views]
            + [pl.BlockSpec((1,) + r.shape[1:], lambda k, core_ref: (k, 0, 0)) for r in recvs],
            out_specs=[pl.BlockSpec((1,) + r.shape[1:], lambda k, core_ref: (k, 0, 0)) for r in recvs]),
        out_shape=[jax.ShapeDtypeStruct(r.shape, BF16) for r in recvs], compiler_params=_params("parallel"))(core, *views, *recvs)


def add_chips(sums, lands, chip, name):
    n = len(sums)

    def body(chip_ref, *refs):
        for s_ref, r_ref, o_ref in zip(refs[:n], refs[n:2 * n], refs[2 * n:]):
            o_ref[...] = ((s_ref[0].astype(F32) + r_ref[0].astype(F32)) + r_ref[1].astype(F32)) + r_ref[2].astype(F32)

    def rows(s):
        return (s.shape[1] // ROW_STEPS, s.shape[2])

    return pl.pallas_call(
        body, name=name,
        grid_spec=pltpu.PrefetchScalarGridSpec(
            num_scalar_prefetch=1, grid=(ROW_STEPS,),
            in_specs=[pl.BlockSpec((1,) + rows(s), lambda i, chip_ref: (chip_ref[0], i, 0)) for s in sums]
            + [pl.BlockSpec((3,) + rows(s), lambda i, chip_ref: (0, i, 0)) for s in sums],
            out_specs=[pl.BlockSpec(rows(s), lambda i, chip_ref: (i, 0)) for s in sums]),
        out_shape=[jax.ShapeDtypeStruct(s.shape[1:], F32) for s in sums], compiler_params=_params("parallel"))(chip, *sums, *lands)


def _adamw_math(w, g, m, v):
    m = ADAM_B1 * m + (1.0 - ADAM_B1) * g
    v = ADAM_B2 * v + (1.0 - ADAM_B2) * (g * g)
    m_hat = m / (1.0 - ADAM_B1 ** ADAM_STEP)
    v_hat = v / (1.0 - ADAM_B2 ** ADAM_STEP)
    return -ADAM_LR * (m_hat / (jnp.sqrt(v_hat) + ADAM_EPS) + ADAM_WD * w), m, v


def adamw_one_half(ws, gs, ms, vs, core, name, own, into=None):
    n = len(ws)
    r, cdim = ws[0].shape
    half = r // 2
    tr = _tile(half, max(8, 256 // n))
    steps = half // tr
    prior = () if into is None else tuple(a for group in into for a in group)

    def body(core_ref, *refs):
        ins, outs = refs[:4 * n], refs[4 * n + len(prior):]
        for a in range(n):
            w_ref, g_ref, m_ref, v_ref = ins[a::n]
            g = g_ref[...]
            outs[a][...] = g
            outs[n + a][...], outs[2 * n + a][...], outs[3 * n + a][...] = _adamw_math(w_ref[...], g, m_ref[...], v_ref[...])

    def rows(i, core_ref):
        return ((core_ref[0] if own else 1 - core_ref[0]) * steps + i, 0)

    whole = pl.BlockSpec((tr, cdim), rows)
    part = pl.BlockSpec((tr, cdim), lambda i, core_ref: (i, 0))
    out = pl.pallas_call(
        body, name=name,
        grid_spec=pltpu.PrefetchScalarGridSpec(
            num_scalar_prefetch=1, grid=(steps,),
            in_specs=[whole] * n + [part] * n + [whole] * (2 * n) + [ANY_SPEC] * len(prior), out_specs=[whole] * (4 * n)),
        out_shape=[jax.ShapeDtypeStruct((r, cdim), F32)] * (4 * n),
        input_output_aliases={1 + 4 * n + j: j for j in range(len(prior))},
        compiler_params=_params("parallel"))(core, *ws, *gs, *ms, *vs, *prior)
    return [out[k * n:(k + 1) * n] for k in range(4)]


def adamw_halves_sparsecore(ws, owns, others, ms, vs, name):
    n = len(ws)
    r, cdim = ws[0].shape
    half_groups = r // 2 // SC_ROWS
    per_tile = -(-2 * half_groups // SC_TILES)

    def body(*refs):
        ins, outs, (wb, gb, mb, vb) = refs[:5 * n], refs[5 * n:9 * n], refs[9 * n:]
        tile = lax.axis_index("sc_tile") * 2 + lax.axis_index("sc_core")
        core = lax.axis_index("c")

        @pl.loop(0, per_tile)
        def _(it):
            group = tile + SC_TILES * it

            @pl.when(group < 2 * half_groups)
            def _():
                rows = pl.ds(group * SC_ROWS, SC_ROWS)
                in_own = (group < half_groups) == (core == 0)
                half_rows = pl.ds((group % half_groups) * SC_ROWS, SC_ROWS)
                for a in range(n):
                    w_hbm, own_hbm, other_hbm, m_hbm, v_hbm = ins[a::n]
                    g_out, d_out, m_out, v_out = outs[a::n]
                    pltpu.sync_copy(w_hbm.at[rows], wb)
                    pltpu.sync_copy(m_hbm.at[rows], mb)
                    pltpu.sync_copy(v_hbm.at[rows], vb)

                    @pl.when(in_own)
                    def _():
                        pltpu.sync_copy(own_hbm.at[half_rows], gb)

                    @pl.when(jnp.logical_not(in_own))
                    def _():
                        pltpu.sync_copy(other_hbm.at[half_rows], gb)

                    pltpu.sync_copy(gb, g_out.at[rows])

                    @pl.loop(0, SC_ROWS)
                    def _(row):
                        @pl.loop(0, cdim, step=SC_LANES)
                        def _(col):
                            at = (row, pl.ds(col, SC_LANES))
                            wb[at], mb[at], vb[at] = _adamw_math(wb[at], gb[at], mb[at], vb[at])

                    pltpu.sync_copy(wb, d_out.at[rows])
                    pltpu.sync_copy(mb, m_out.at[rows])
                    pltpu.sync_copy(vb, v_out.at[rows])

    out = pl.kernel(
        body, name=name, out_type=[jax.ShapeDtypeStruct((r, cdim), F32)] * (4 * n),
        mesh=plsc.VectorSubcoreMesh(core_axis_name="sc_core", subcore_axis_name="sc_tile"),
        scratch_types=[pltpu.VMEM((SC_ROWS, cdim), F32)] * 4)(*ws, *owns, *others, *ms, *vs)
    return [out[a::n] for a in range(n)]


def small_allreduce_adamw(g, w, m, v, deps=()):
    r, cdim = g.shape

    def body(g_ref, w_ref, m_ref, v_ref, go_ref, d_ref, mo_ref, vo_ref, pair, quad, d2d_send, d2d_recv, ici_send, ici_recv):
        x, y, c, chips = _place()
        me = 2 * x + y
        pair[c] = g_ref[...]
        swap = pltpu.make_async_remote_copy(
            src_ref=g_ref, dst_ref=pair.at[c], send_sem=d2d_send, recv_sem=d2d_recv, device_id=(x, y, 1 - c),
            device_id_type=MESH_IDS)
        swap.start()
        swap.wait()
        quad[me] = pair[0] + pair[1]
        copies = []
        for j, (cx, cy) in enumerate(chips):
            cp = pltpu.make_async_remote_copy(
                src_ref=quad.at[me], dst_ref=quad.at[me], send_sem=ici_send.at[j], recv_sem=ici_recv.at[j],
                device_id=(cx, cy, c), device_id_type=MESH_IDS)
            cp.start()
            copies.append(cp)
        for j, (cx, cy) in enumerate(chips):
            slot = quad.at[2 * cx + cy]
            pltpu.make_async_remote_copy(
                src_ref=slot, dst_ref=slot, send_sem=ici_send.at[j], recv_sem=ici_recv.at[j], device_id=(cx, cy, c),
                device_id_type=MESH_IDS).wait_recv()
        for cp in copies:
            cp.wait_send()
        total = (quad[0] + quad[1]) + (quad[2] + quad[3])
        go_ref[...] = total
        d_ref[...], mo_ref[...], vo_ref[...] = _adamw_math(w_ref[...], total, m_ref[...], v_ref[...])

    return pl.pallas_call(
        _ordered_after(body, 4, deps), name="small_allreduce_adamw", in_specs=[VMEM_SPEC] * 4 + [ANY_SPEC] * len(deps),
        out_specs=[VMEM_SPEC] * 4,
        out_shape=[jax.ShapeDtypeStruct((r, cdim), F32)] * 4,
        scratch_shapes=[pltpu.VMEM((2, r, cdim), F32), pltpu.VMEM((N_CHIPS, r, cdim), F32), pltpu.SemaphoreType.DMA,
                        pltpu.SemaphoreType.DMA, pltpu.SemaphoreType.DMA((3,)), pltpu.SemaphoreType.DMA((3,))],
        compiler_params=pltpu.CompilerParams(has_side_effects=True, vmem_limit_bytes=VMEM_LIMIT_V7X))(g, w, m, v, *deps)


GROUPS = (
    ("ffn1", ("ffn1_w_gate", "ffn1_w_up", "ffn1_w_down")),
    ("mixer", ("w_in", "pool_w", "w_out_a", "w_out_b", "w_o")),
    ("ffn2", ("ffn2_w_gate", "ffn2_w_up", "ffn2_w_down")),
    ("ple", ("ple_w_gate", "ple_w_proj")),
)
GATHERS = (
    ("ffn1_in", ("ffn1_w_gate", "ffn1_w_up")),
    ("ffn1_out", ("ffn1_w_down",)),
) + GROUPS[1:]
GAINS = ("ffn1_pre_g", "ffn1_post_g", "mix_pre_g", "sgu_norm_g", "pool_scale", "mix_post_g",
         "ffn2_pre_g", "ffn2_post_g", "ple_pre_g", "ple_post_g")
SMALL = GAINS + ("sgu_b", "sgu_w")
PACKED = SMALL + ("loss",)
WEIGHTS = ("ffn1_pre_g", "ffn1_w_gate", "ffn1_w_up", "ffn1_w_down", "ffn1_post_g", "mix_pre_g", "w_in", "sgu_norm_g",
           "sgu_w", "sgu_b", "pool_w", "pool_scale", "w_out_a", "w_out_b", "w_o", "mix_post_g", "ffn2_pre_g",
           "ffn2_w_gate", "ffn2_w_up", "ffn2_w_down", "ffn2_post_g", "ple_pre_g", "ple_w_gate", "ple_w_proj", "ple_post_g")
PACK_ROWS = 16


TRANSPOSED = ("ffn1_w_gate", "ffn1_w_up", "ffn2_w_gate", "ffn2_w_up")


def _shard2d(name, a):
    a = a[0]
    return a.T if name in TRANSPOSED else a.reshape(-1, a.shape[-1])


def _unshard2d(name, a2d, shape):
    return (a2d.T if name in TRANSPOSED else a2d).reshape(shape)


def pack_rows(gains, sgu_b, loss_tile, name):
    n = len(gains)
    d = gains[0].shape[1]
    g = sgu_b.shape[0]

    def body(*refs):
        o_ref = refs[-1]
        o_ref[...] = jnp.zeros_like(o_ref)
        for i in range(n):
            o_ref[i:i + 1, :] = refs[i][...]
        o_ref[n:n + g, 0:CHUNK] = refs[n][...]
        o_ref[n + g:n + g + 1, 0:CHUNK] = refs[n + 1][0:1, :]

    return pl.pallas_call(
        body, name=name, in_specs=[VMEM_SPEC] * (n + 2), out_specs=VMEM_SPEC,
        out_shape=jax.ShapeDtypeStruct((PACK_ROWS, d), F32))(*gains, sgu_b, loss_tile)


def _pack_small(parts, tag):
    d = parts[GAINS[0]].shape[-1]
    rows = pack_rows([parts[n] for n in GAINS], parts["sgu_b"].reshape(-1, CHUNK), parts["loss"], "pack_" + tag)
    return jnp.concatenate([rows, parts["sgu_w"].reshape(-1, d)], axis=0)


def _unpack_small(packed, like):
    n, g = len(GAINS), like["sgu_b"].size // CHUNK
    out = {name: packed[i:i + 1] for i, name in enumerate(GAINS)}
    out["sgu_b"] = packed[n:n + g, :CHUNK].reshape(like["sgu_b"].shape)
    out["loss"] = packed[n + g, 0]
    out["sgu_w"] = packed[PACK_ROWS:].reshape(like["sgu_w"].shape)
    return out


def _ffn_fwd(xn, h, w, pre, g_post, g_next, tag, between=None, first=None):
    a4, s4, t4 = ffn_gu(xn, w[pre + "w_gate"], w[pre + "w_up"], tag + "_gu") if first is None else first(xn)
    deps = ()
    if between is not None:
        more, deps = between(a4)
        w.update(more)
    f, h_new, xn_next = mm_norm_res(a4, w[pre + "w_down"], h, g_post, g_next, 0.5, tag + "_down", deps=deps)
    return dict(xn=xn, h=h, a4=a4, s4=s4, t4=t4, f=f), h_new, xn_next


def _ffn_bwd_w(dh, saved, w, pre, g_post, tag, deps):
    df, dg4, du4, d_post = ffn_bwd_a(dh, saved["f"], g_post, w[pre + "w_down"], saved["s4"], saved["t4"], 0.5, tag + "_bwd_a",
                                     deps=deps)
    return dg4, du4, _ffn_dw(saved, df, dg4, du4, pre, tag), d_post


def _ffn_dw(saved, df, dg4, du4, pre, tag):
    nk = N_CHIPS
    return {
        pre + "w_down": dw_tn(saved["a4"], df, nk, tag + "_dw_down", x_kmajor=True, dy_mode="same"),
        pre + "w_gate": dw_tn(dg4, saved["xn"], nk, tag + "_dw_gate", x_kmajor=True, dy_mode="same"),
        pre + "w_up": dw_tn(du4, saved["xn"], nk, tag + "_dw_up", x_kmajor=True, dy_mode="same"),
    }


def _ffn_bwd_x(dh, dg4, du4, saved, w, pre, g_pre, tag, deps):
    return dx_norm_bwd([(dg4, w[pre + "w_gate"], False), (du4, w[pre + "w_up"], False)], saved["h"], g_pre, dh,
                       tag + "_bwd_x", deps=deps)


def _start_gathers(weights, chip):
    first_tag, first_names = GATHERS[0]
    made, own = cast_place([_shard2d(n, weights[n]) for n in first_names], chip, "cast_" + first_tag, plain=True)
    sends, recvs, bufs, token = allgather_start(list(made), "allgather_start_" + first_tag)
    gathering = {first_tag: (sends, recvs, bufs)}
    later = [n for _, names in GATHERS[1:] for n in names]
    cast = dict(zip(later, cast_place([_shard2d(n, weights[n]) for n in later], chip, "cast_later", deps=(token,))))
    for tag, names in GATHERS[1:]:
        sends, recvs, bufs, token = allgather_start([cast[n] for n in names], "allgather_start_" + tag, deps=(token,))
        gathering[tag] = (sends, recvs, bufs)
    return gathering, token, dict(zip(first_names, own))


def _gathered(started, names, after, tag):
    sends, recvs, bufs = started
    landed = allgather_wait(sends, recvs, bufs, after, "allgather_wait_" + tag)
    return dict(zip(names, d2d_forward(landed, "d2d_forward_" + tag)))


def _forward_early(started, after, tag):
    sends, recvs, bufs = started
    landed = allgather_wait(sends, recvs, bufs, after, "allgather_wait_" + tag)
    return d2d_forward_start(landed, "d2d_forward_start_" + tag)


def _forwarded(forwarding, names, after, tag):
    return dict(zip(names, d2d_forward_wait(forwarding, after, "d2d_forward_wait_" + tag)))


def _exchange_start(names, big_g, tag):
    return sibling_start([big_g[n] for n in names], True, "exchange_start_" + tag)


def _scatter_begin(names, exchanging, after, core, tag):
    partial, from_sibling = sibling_wait(exchanging, True, after, "exchange_wait_" + tag)
    chip_sums = list(add_halves(partial, from_sibling, core, "add_halves_" + tag))
    sends, recvs, sums, lands, token = scatter_start(chip_sums, "scatter_start_" + tag)
    return (names, sends, recvs, sums, lands), token


def _share_begin(scattering, after, chip, tag):
    names, sends, recvs, sums, lands = scattering
    sums, lands = scatter_wait(sends, recvs, sums, lands, after, "scatter_wait_" + tag)
    reduced = list(add_chips(sums, lands, chip, "add_chips_" + tag))
    sharing, token = sibling_start(reduced, False, "share_start_" + tag)
    return (names, sharing), token


def _update(shared, after, weights, moments_m, moments_v, tag, results):
    names, sharing = shared
    reduced, others = sibling_wait(sharing, False, after, "share_wait_" + tag)
    for b, idx in enumerate(_by_shape(reduced)):
        in_bucket = [names[i] for i in idx]
        outs = adamw_halves_sparsecore(
            [_shard2d(n, weights[n]) for n in in_bucket], [reduced[i] for i in idx], [others[i] for i in idx],
            [_shard2d(n, moments_m[n]) for n in in_bucket], [_shard2d(n, moments_v[n]) for n in in_bucket], f"adamw_{tag}_{b}")
        for n, per_weight in zip(in_bucket, outs):
            for store, value in zip(results, per_weight):
                store[n] = _unshard2d(n, value, weights[n].shape)


def _update_last(scattering, after, weights, moments_m, moments_v, core, chip, tag, results):
    names, sends, recvs, sums, lands = scattering
    sums, lands = scatter_wait(sends, recvs, sums, lands, after, "scatter_wait_" + tag)
    reduced = list(add_chips(sums, lands, chip, "add_chips_" + tag))
    (sends, recvs, reduced, zones), _ = sibling_start(reduced, False, "share_start_" + tag)
    state = [[_shard2d(n, held[n]) for n in names] for held in (weights, moments_m, moments_v)]
    first = adamw_one_half(state[0], reduced, state[1], state[2], core, f"adamw_{tag}_own", own=True)
    _, others = sibling_wait((sends, recvs, reduced, zones), False, first[1][-1], "share_wait_" + tag)
    outs = adamw_one_half(state[0], others, state[1], state[2], core, f"adamw_{tag}_other", own=False, into=first)
    for store, values in zip(results, outs):
        for n, value in zip(names, values):
            store[n] = _unshard2d(n, value, weights[n].shape)


def kernel(x, p, ffn1_pre_g, ffn1_w_gate, ffn1_w_up, ffn1_w_down, ffn1_post_g, mix_pre_g, w_in, sgu_norm_g, sgu_w, sgu_b, pool_w, pool_scale, w_out_a, w_out_b, w_o, mix_post_g, ffn2_pre_g, ffn2_w_gate, ffn2_w_up, ffn2_w_down, ffn2_post_g, ple_pre_g, ple_w_gate, ple_w_proj, ple_post_g, loss_target, m_ffn1_pre_g, m_ffn1_w_gate, m_ffn1_w_up, m_ffn1_w_down, m_ffn1_post_g, m_mix_pre_g, m_w_in, m_sgu_norm_g, m_sgu_w, m_sgu_b, m_pool_w, m_pool_scale, m_w_out_a, m_w_out_b, m_w_o, m_mix_post_g, m_ffn2_pre_g, m_ffn2_w_gate, m_ffn2_w_up, m_ffn2_w_down, m_ffn2_post_g, m_ple_pre_g, m_ple_w_gate, m_ple_w_proj, m_ple_post_g, v_ffn1_pre_g, v_ffn1_w_gate, v_ffn1_w_up, v_ffn1_w_down, v_ffn1_post_g, v_mix_pre_g, v_w_in, v_sgu_norm_g, v_sgu_w, v_sgu_b, v_pool_w, v_pool_scale, v_w_out_a, v_w_out_b, v_w_o, v_mix_post_g, v_ffn2_pre_g, v_ffn2_w_gate, v_ffn2_w_up, v_ffn2_w_down, v_ffn2_post_g, v_ple_pre_g, v_ple_w_gate, v_ple_w_proj, v_ple_post_g):
    given = dict(locals())
    weights = {n: given[n] for n in WEIGHTS}
    moments_m = {n: given["m_" + n] for n in WEIGHTS}
    moments_v = {n: given["v_" + n] for n in WEIGHTS}
    core = lax.axis_index("c").astype(jnp.int32).reshape(1)
    chip = (2 * lax.axis_index("x") + lax.axis_index("y")).astype(jnp.int32).reshape(1)

    d = x.shape[-1]
    gathering, last_start, own_first = _start_gathers(weights, chip)
    token = (last_start,)
    gain = {n: weights[n] for n in GAINS}
    sgu_w3 = sgu_w[0]
    sgu_b3 = sgu_b[0][:, :, None]
    groups = dict(GROUPS + GATHERS)

    h0 = x[0]
    tgt = loss_target[0]
    p_bf = p[0, 0].astype(BF16)
    xn1 = rms_cast(h0, gain["ffn1_pre_g"], "ffn1_pre_norm")
    w = {}

    def ffn1_first(xn):
        own = ffn_gu(xn, own_first["ffn1_w_gate"][None], own_first["ffn1_w_up"][None], "ffn1_gu_own", chip=chip)
        w.update(_gathered(gathering["ffn1_in"], groups["ffn1_in"], (own[0], last_start), "ffn1_in"))
        return ffn_gu(xn, w["ffn1_w_gate"], w["ffn1_w_up"], "ffn1_gu", chip=chip, into=own)

    s1, h1, xn2 = _ffn_fwd(xn1, h0, w, "ffn1_", gain["ffn1_post_g"], gain["mix_pre_g"], "ffn1",
                           lambda a4: (_gathered(gathering["ffn1_out"], groups["ffn1_out"], a4, "ffn1_out"), ()), ffn1_first)
    w.update(_gathered(gathering["mixer"], groups["mixer"], h1, "mixer"))
    full = {n: w[n].reshape(-1, d) for n in ("w_out_a", "w_out_b", "w_o")}
    n_groups = pool_w.shape[1]
    rows_per = pool_w.shape[2]
    dgp = pool_w.shape[3]
    pool_full = w["pool_w"].reshape(N_CHIPS, n_groups, rows_per, dgp).transpose(1, 0, 2, 3).reshape(n_groups, N_CHIPS * rows_per, dgp)
    z = mixer_in(xn2, w["w_in"], "mixer_in")
    a = sgu_fwd(z, gain["sgu_norm_g"], sgu_w3, sgu_b3, "sgu_fwd")
    diff, b = pool_fwd(z, pool_full, gain["pool_scale"], "pool_fwd")
    ya, yb, y = mixer_y(a, b, z, full["w_out_a"], full["w_out_b"], "mixer_y")
    forwarding, tok = _forward_early(gathering["ffn2"], y, "ffn2")
    m, h2, xn3 = mm_norm_res(y[None], full["w_o"][None], h1, gain["mix_post_g"], gain["ffn2_pre_g"], 1.0, "mixer_out",
                             deps=(tok,))
    w.update(_forwarded(forwarding, groups["ffn2"], h2, "ffn2"))
    early = {}

    def forward_ple(a4):
        early["ple"], tok_ple = _forward_early(gathering["ple"], a4, "ple")
        return {}, (tok_ple,)

    s2, h3, xn4 = _ffn_fwd(xn3, h2, w, "ffn2_", gain["ffn2_post_g"], gain["ple_pre_g"], "ffn2", forward_ple)
    w.update(_forwarded(early["ple"], groups["ple"], h3, "ple"))
    full["ple_w_gate"] = w["ple_w_gate"].reshape(-1, d)
    proj_full = w["ple_w_proj"].transpose(1, 0, 2).reshape(ple_w_proj.shape[1], -1)

    small_g, big_g = {}, {}
    ds, de, dh3, small_g["ple_post_g"], small_g["ple_pre_g"], loss_part = ple_loss(
        xn4, p_bf, full["ple_w_gate"], proj_full, h3, gain["ple_post_g"], gain["ple_pre_g"], tgt, "ple_loss")
    small_g["loss"] = loss_part
    grad, delta, new_m, new_v = {}, {}, {}, {}
    results = (grad, delta, new_m, new_v)
    update = functools.partial(_update, weights=weights, moments_m=moments_m, moments_v=moments_v, results=results)

    def reduce_behind(tag, previous):
        exchanging, tok = _exchange_start(groups[tag], big_g, tag)
        if previous is not None:
            shared, tok = _share_begin(previous[1], tok, chip, previous[0])
        scattering, tok_scatter = _scatter_begin(groups[tag], exchanging, tok, core, tag)
        if previous is not None:
            update(shared, tok_scatter, tag=previous[0])
        return (tag, scattering), (tok_scatter,)

    big_g["ple_w_gate"] = dw_tn(xn4, ds, 1, "dw_ple_gate").reshape(N_CHIPS, -1, d)
    big_g["ple_w_proj"] = dw_tn(p_bf, de, N_CHIPS, "dw_ple_proj", dy_mode="cols")
    reducing, deps = reduce_behind("ple", None)

    dg4, du4, g2, small_g["ffn2_post_g"] = _ffn_bwd_w(dh3, s2, w, "ffn2_", gain["ffn2_post_g"], "ffn2", deps)
    big_g.update(g2)
    dh2, small_g["ffn2_pre_g"] = _ffn_bwd_x(dh3, dg4, du4, s2, w, "ffn2_", gain["ffn2_pre_g"], "ffn2", ())
    reducing, deps = reduce_behind("ffn2", reducing)

    dm, dya, dyb, dz, da, db, small_g["mix_post_g"] = mixer_bwd_y(
        dh2, m, gain["mix_post_g"], full["w_o"], ya, yb, z, full["w_out_a"], full["w_out_b"], "mixer_bwd_y", deps=deps)
    big_g["w_o"] = dw_tn(y, dm, 1, "dw_o").reshape(N_CHIPS, -1, d)
    big_g["w_out_a"] = dw_tn(a, dya, 1, "dw_out_a").reshape(N_CHIPS, -1, d)
    big_g["w_out_b"] = dw_tn(b, dyb, 1, "dw_out_b").reshape(N_CHIPS, -1, d)
    dz, d_sgu_w, d_sgu_b, small_g["sgu_norm_g"] = sgu_bwd(z, da, dz, gain["sgu_norm_g"], sgu_w3, sgu_b3, "sgu_bwd")
    dz, d_pool_w, small_g["pool_scale"] = pool_bwd(db, diff, dz, pool_full, gain["pool_scale"], "pool_bwd")
    big_g["pool_w"] = d_pool_w.astype(BF16).reshape(n_groups, N_CHIPS, rows_per, dgp).transpose(1, 0, 2, 3).reshape(
        N_CHIPS, n_groups * rows_per, dgp)
    big_g["w_in"] = dw_in_tiles(xn2, dz, N_CHIPS, "dw_in")
    dh1, small_g["mix_pre_g"] = dx_norm_bwd([(dz, w["w_in"], True)], h1, gain["mix_pre_g"], dh2, "mixer_bwd_x")
    reducing, deps = reduce_behind("mixer", reducing)

    dg4, du4, g1, small_g["ffn1_post_g"] = _ffn_bwd_w(dh1, s1, w, "ffn1_", gain["ffn1_post_g"], "ffn1", deps)
    big_g.update(g1)
    reducing, deps = reduce_behind("ffn1", reducing)
    dh0, small_g["ffn1_pre_g"] = _ffn_bwd_x(dh1, dg4, du4, s1, w, "ffn1_", gain["ffn1_pre_g"], "ffn1", deps)
    small_g["sgu_w"] = d_sgu_w
    small_g["sgu_b"] = d_sgu_b[:, :, 0]

    no_state = {"loss": jnp.zeros((8, 128), F32)}
    packed = small_allreduce_adamw(
        _pack_small(small_g, "grads"), _pack_small({n: weights[n] for n in SMALL} | no_state, "weights"),
        _pack_small({n: moments_m[n] for n in SMALL} | no_state, "m"), _pack_small({n: moments_v[n] for n in SMALL} | no_state, "v"))
    like = {n: weights[n] for n in SMALL}
    for store, block in zip((grad, delta, new_m, new_v), packed):
        store.update(_unpack_small(block, like))
    _update_last(reducing[1], packed[0], weights, moments_m, moments_v, core, chip, "ffn1", results)

    return (grad["loss"], dh0[None], *[grad[n] for n in WEIGHTS], *[delta[n] for n in WEIGHTS],
            *[new_m[n] for n in WEIGHTS], *[new_v[n] for n in WEIGHTS])
```

```python
import functools

import jax
import jax.numpy as jnp
from jax import lax
from jax.experimental import pallas as pl
from jax.experimental.pallas import tpu as pltpu
from jax.experimental.pallas import tpu_sc as plsc

F32 = jnp.float32
BF16 = jnp.bfloat16
EPS = 1e-6
CHUNK = 128
POOL_WINDOWS = (2, 4, 8, 16)
HALO = 16
N_CHIPS = 4
ADAM_LR, ADAM_B1, ADAM_B2, ADAM_EPS, ADAM_WD, ADAM_STEP = 0.001, 0.9, 0.999, 1e-08, 0.01, 10
VMEM_LIMIT_V7X = 58 * 1024 * 1024
MESH_IDS = pl.DeviceIdType.MESH
HBM_SPEC = pl.BlockSpec(memory_space=pltpu.HBM)
VMEM_SPEC = pl.BlockSpec(memory_space=pltpu.VMEM)
SEM_SPEC = pl.BlockSpec(memory_space=pltpu.SEMAPHORE)
ANY_SPEC = pl.BlockSpec(memory_space=pl.ANY)
DATAFLOW = pltpu.SideEffectType.DATAFLOW_SIDE_EFFECTING
OTHER_CHIPS = ((1, 0), (0, 1), (1, 1))
DZ_SLOT = (2, 3, 4, 0, 1)
SC_TILES, SC_LANES, SC_ROWS = 32, 16, 8
ROW_STEPS = 4
SIBLING_BARRIER = 1
DW_TOKENS = 4096
DW_IN_TILE = 256

NT = (((1,), (1,)), ((), ()))
TN = (((0,), (0,)), ((), ()))


def _params(*sem, **more):
    return pltpu.CompilerParams(dimension_semantics=sem or None, vmem_limit_bytes=VMEM_LIMIT_V7X, **more)


def _tile(t, want):
    return max(c for c in range(8, min(t, want) + 1, 8) if t % c == 0)


def _const(shape):
    return pl.BlockSpec(shape, lambda *_: (0,) * len(shape))


def _rows(tm, d, col=0):
    return pl.BlockSpec((tm, d), lambda i: (i, col))


def _kmajor(nk, tm, kb):
    return pl.BlockSpec((nk, tm, kb), lambda i: (0, i, 0))


def _dot(a, b):
    return jnp.dot(a, b, preferred_element_type=F32)


def _dot_nt(a, b):
    return lax.dot_general(a, b, NT, preferred_element_type=F32)


def _dot_tn(a, b):
    return lax.dot_general(a, b, TN, preferred_element_type=F32)


def _gelu(x):
    return 0.5 * x * (1.0 + jnp.tanh(0.7978845608028654 * (x + 0.044715 * x * x * x)))


def _gelu_and_grad(x):
    k, kc = 0.7978845608028654, 0.7978845608028654 * 0.044715
    x2 = x * x
    th = jnp.tanh(x * (k + kc * x2))
    cdf = 0.5 + 0.5 * th
    return x * cdf, cdf + x * (0.5 - 0.5 * th * th) * (k + 3.0 * kc * x2)


def _sigmoid(x):
    return 1.0 / (1.0 + jnp.exp(-x))


def _rstd(h):
    return lax.rsqrt(jnp.mean(h * h, axis=-1, keepdims=True) + EPS)


def _rms_bwd(h, g, dy):
    r = _rstd(h)
    t = dy * g
    dh = r * t - h * (r * r * r) * jnp.mean(h * t, axis=-1, keepdims=True)
    return dh, jnp.sum(dy * h * r, axis=0, keepdims=True)


def _ordered_after(body, n_in, deps):
    if not deps:
        return body
    return lambda *refs: body(*refs[:n_in], *refs[n_in + len(deps):])


def _accumulate(ref, value):
    @pl.when(pl.program_id(0) == 0)
    def _():
        ref[...] = jnp.zeros_like(ref)

    ref[...] += value


def rms_cast(h, g, name):
    t, d = h.shape
    tm = _tile(t, 512)

    def body(h_ref, g_ref, o_ref):
        hv = h_ref[...]
        o_ref[...] = (hv * _rstd(hv) * g_ref[...]).astype(BF16)

    return pl.pallas_call(
        body, name=name, grid=(t // tm,), in_specs=[_rows(tm, d), _const((1, d))], out_specs=_rows(tm, d),
        out_shape=jax.ShapeDtypeStruct((t, d), BF16), compiler_params=_params("parallel"))(h, g)


def ffn_gu(xn, wgt, wut, name, chip=None, into=None):
    t, d = xn.shape
    nkw, fk, _ = wgt.shape
    tm = _tile(t, 512)
    count = N_CHIPS if chip is None else (1 if into is None else N_CHIPS - 1)
    first = 0 if into is None else 1

    def body(*refs):
        x_ref, wg_ref, wu_ref = refs[-6 - (0 if into is None else 3):][:3]
        a_ref, s_ref, t_ref = refs[-3:]
        xv = x_ref[...]
        g = _dot_nt(xv, wg_ref[0])
        sg = _sigmoid(g)
        s = g * sg
        s_ref[0] = s.astype(BF16)
        ds = sg * (1.0 + g * (1.0 - sg))
        u = _dot_nt(xv, wu_ref[0])
        a_ref[0] = (s * u).astype(BF16)
        t_ref[0] = (u * ds).astype(BF16)

    def slot(k, *chip_ref):
        return (chip_ref[0][0] + first + k) % N_CHIPS if chip_ref else k

    w_spec = pl.BlockSpec((1, fk, d), lambda k, i, *c: (slot(k, *c) if nkw > 1 else 0, 0, 0))
    o_spec = pl.BlockSpec((1, tm, fk), lambda k, i, *c: (slot(k, *c), i, 0))
    shape = jax.ShapeDtypeStruct((N_CHIPS, t, fk), BF16)
    prior = () if into is None else tuple(into)
    call = pl.pallas_call(
        body, name=name,
        grid_spec=pltpu.PrefetchScalarGridSpec(
            num_scalar_prefetch=0 if chip is None else 1, grid=(count, t // tm),
            in_specs=[pl.BlockSpec((tm, d), lambda k, i, *c: (i, 0)), w_spec, w_spec] + [ANY_SPEC] * len(prior),
            out_specs=[o_spec] * 3),
        out_shape=[shape] * 3, input_output_aliases={4 + j: j for j in range(len(prior))},
        compiler_params=_params("parallel", "parallel"))
    return call(*(() if chip is None else (chip,)), xn, wgt, wut, *prior)


def mm_norm_res(a3, w3, h_old, g_post, g_next, scale, name, deps=()):
    nk, t, kb = a3.shape
    d = w3.shape[2]
    tm = _tile(t, 256)

    def body(a_ref, w_ref, h_ref, gp_ref, gn_ref, f_ref, hn_ref, xn_ref):
        f = _dot(a_ref[0], w_ref[0])
        for k in range(1, nk):
            f += _dot(a_ref[k], w_ref[k])
        f_ref[...] = f.astype(BF16)
        hn = h_ref[...] + scale * (f * _rstd(f) * gp_ref[...])
        hn_ref[...] = hn
        xn_ref[...] = (hn * _rstd(hn) * gn_ref[...]).astype(BF16)

    return pl.pallas_call(
        _ordered_after(body, 5, deps), name=name, grid=(t // tm,),
        in_specs=[_kmajor(nk, tm, kb), _const((nk, kb, d)), _rows(tm, d), _const((1, d)), _const((1, d))] + [ANY_SPEC] * len(deps),
        out_specs=[_rows(tm, d)] * 3,
        out_shape=[jax.ShapeDtypeStruct((t, d), BF16), jax.ShapeDtypeStruct((t, d), F32), jax.ShapeDtypeStruct((t, d), BF16)],
        compiler_params=_params("parallel"))(a3, w3, h_old, g_post, g_next, *deps)


def mixer_in(xn, win4, name):
    t, d = xn.shape
    nk, _, nb = win4.shape
    tm = _tile(t, 512)

    def body(x_ref, w_ref, z_ref):
        z_ref[...] = _dot(x_ref[...], w_ref[0]).astype(BF16)

    return pl.pallas_call(
        body, name=name, grid=(nk, t // tm),
        in_specs=[pl.BlockSpec((tm, d), lambda k, i: (i, 0)), pl.BlockSpec((1, d, nb), lambda k, i: (k, 0, 0))],
        out_specs=pl.BlockSpec((tm, nb), lambda k, i: (i, k)), out_shape=jax.ShapeDtypeStruct((t, nk * nb), BF16),
        compiler_params=_params("parallel", "parallel"))(xn, win4)


def _causal_mask():
    row = lax.broadcasted_iota(jnp.int32, (CHUNK, CHUNK), 0)
    col = lax.broadcasted_iota(jnp.int32, (CHUNK, CHUNK), 1)
    return row >= col


def _layernorm_parts(v):
    mu = jnp.mean(v, axis=-1, keepdims=True)
    vc = v - mu
    r = lax.rsqrt(jnp.mean(vc * vc, axis=-1, keepdims=True) + EPS)
    return vc * r, r


def sgu_fwd(z, norm_g, sgu_w, sgu_b3, name):
    t = z.shape[0]
    d = norm_g.shape[1]
    ng = sgu_w.shape[0]
    dg = d // ng
    tm = _tile(t, 256)

    def body(zu_ref, zv_ref, ng_ref, w_ref, b_ref, a_ref):
        vhat, _ = _layernorm_parts(_gelu(zv_ref[...].astype(F32)))
        vn = (vhat * ng_ref[...]).astype(BF16)
        u = _gelu(zu_ref[...].astype(F32))
        mask = _causal_mask()
        for g in range(ng):
            wg = jnp.where(mask, w_ref[g], 0.0).astype(BF16)
            for ci in range(tm // CHUNK):
                rs, cs = slice(ci * CHUNK, (ci + 1) * CHUNK), slice(g * dg, (g + 1) * dg)
                sv = _dot(wg, vn[rs, cs]) + b_ref[g]
                a_ref[rs, cs] = (u[rs, cs] * sv).astype(BF16)

    return pl.pallas_call(
        body, name=name, grid=(t // tm,),
        in_specs=[_rows(tm, d, 0), _rows(tm, d, 1), _const((1, d)), _const((ng, CHUNK, CHUNK)), _const((ng, CHUNK, 1))],
        out_specs=_rows(tm, d), out_shape=jax.ShapeDtypeStruct((t, d), BF16),
        compiler_params=_params("parallel"))(z, z, norm_g, sgu_w, sgu_b3)


def pool_fwd(z, pool_w, pool_scale, name):
    t = z.shape[0]
    d = pool_scale.shape[1]
    ng = pool_w.shape[0]
    dg = d // ng
    tm = _tile(t, 256)
    per = tm // HALO

    def body(c_ref, prev_ref, w_ref, s_ref, diff_ref, b_ref):
        i = pl.program_id(0)
        cur = c_ref[...].astype(F32)
        prev = jnp.where(i > 0, prev_ref[...].astype(F32), 0.0)
        ext = jnp.concatenate([prev, cur], axis=0)
        tok = i * tm + lax.broadcasted_iota(jnp.int32, (tm, 1), 0)
        for g, win in enumerate(POOL_WINDOWS):
            cs = slice(g * dg, (g + 1) * dg)
            s = ext[:, cs]
            sh = 1
            while sh < win:
                s = s + pltpu.roll(s, sh, 0)
                sh *= 2
            per_count = 1.0 / jnp.minimum(tok + 1, win).astype(F32)
            diff = (s[HALO:] * per_count - cur[:, cs]).astype(BF16)
            diff_ref[:, cs] = diff
            b_ref[:, cs] = (_dot(diff, w_ref[g]) * s_ref[:, cs]).astype(BF16)

    return pl.pallas_call(
        body, name=name, grid=(t // tm,),
        in_specs=[_rows(tm, d, 2), pl.BlockSpec((HALO, d), lambda i: (jnp.maximum(i * per - 1, 0), 2)),
                  _const((ng, dg, dg)), _const((1, d))],
        out_specs=[_rows(tm, d)] * 2, out_shape=[jax.ShapeDtypeStruct((t, d), BF16)] * 2,
        compiler_params=_params("parallel"))(z, z, pool_w, pool_scale)


def mixer_y(a, b, z, woa, wob, name):
    t, d = a.shape
    tm = _tile(t, 256)

    def body(a_ref, b_ref, ga_ref, gb_ref, wa_ref, wb_ref, ya_ref, yb_ref, y_ref):
        ya = _dot(a_ref[...], wa_ref[...])
        yb = _dot(b_ref[...], wb_ref[...])
        ya_ref[...] = ya.astype(BF16)
        yb_ref[...] = yb.astype(BF16)
        y_ref[...] = (_sigmoid(ga_ref[...].astype(F32)) * ya + _sigmoid(gb_ref[...].astype(F32)) * yb).astype(BF16)

    return pl.pallas_call(
        body, name=name, grid=(t // tm,),
        in_specs=[_rows(tm, d), _rows(tm, d), _rows(tm, d, 3), _rows(tm, d, 4), _const((d, d)), _const((d, d))],
        out_specs=[_rows(tm, d)] * 3, out_shape=[jax.ShapeDtypeStruct((t, d), BF16)] * 3,
        compiler_params=_params("parallel"))(a, b, z, z, woa, wob)


def ple_loss(xn, p, wpg, wpp, h, g_post, g_pre, target, name):
    t, d = xn.shape
    dp = p.shape[1]
    tm = _tile(t, 256)

    def body(x_ref, p_ref, wg_ref, wp_ref, h_ref, gp_ref, gn_ref, tg_ref, ds_ref, de_ref, dhp_ref, dgp_ref, dgn_ref, loss_ref):
        gate = _sigmoid(_dot(x_ref[...], wg_ref[...]))
        e = _dot(p_ref[...], wp_ref[...])
        q = gate * e
        hv = h_ref[...]
        err = hv + q * _rstd(q) * gp_ref[...] - tg_ref[...]
        _accumulate(loss_ref, jnp.full(loss_ref.shape, (0.5 / d) * jnp.sum(err * err), F32))
        dhv = err * (1.0 / d)
        dq, dgp = _rms_bwd(q, gp_ref[...], dhv)
        ds = (dq * e * gate * (1.0 - gate)).astype(BF16)
        ds_ref[...] = ds
        de_ref[...] = (dq * gate).astype(BF16)
        dx, dgn = _rms_bwd(hv, gn_ref[...], _dot_nt(ds, wg_ref[...]))
        dhp_ref[...] = dhv + dx
        _accumulate(dgp_ref, dgp)
        _accumulate(dgn_ref, dgn)

    return pl.pallas_call(
        body, name=name, grid=(t // tm,),
        in_specs=[_rows(tm, d), _rows(tm, dp), _const((d, d)), _const((dp, d)), _rows(tm, d), _const((1, d)), _const((1, d)),
                  _rows(tm, d)],
        out_specs=[_rows(tm, d)] * 3 + [_const((1, d))] * 2 + [_const((8, 128))],
        out_shape=[jax.ShapeDtypeStruct((t, d), BF16), jax.ShapeDtypeStruct((t, d), BF16), jax.ShapeDtypeStruct((t, d), F32),
                   jax.ShapeDtypeStruct((1, d), F32), jax.ShapeDtypeStruct((1, d), F32), jax.ShapeDtypeStruct((8, 128), F32)],
        compiler_params=_params("arbitrary"))(xn, p, wpg, wpp, h, g_post, g_pre, target)


def ffn_bwd_a(dh, f, g_post, wd4, s4, t4, scale, name, deps=()):
    t, d = dh.shape
    nk, fk, _ = wd4.shape
    tm = _tile(t, 256)

    def body(dh_ref, f_ref, gp_ref, w_ref, s_ref, t_ref, df_ref, dg_ref, du_ref, dgp_ref):
        df, dgp = _rms_bwd(f_ref[...].astype(F32), gp_ref[...], dh_ref[...])
        df = (scale * df).astype(BF16)
        df_ref[...] = df
        _accumulate(dgp_ref, scale * dgp)
        for k in range(nk):
            da = _dot_nt(df, w_ref[k])
            du_ref[k] = (da * s_ref[k].astype(F32)).astype(BF16)
            dg_ref[k] = (da * t_ref[k].astype(F32)).astype(BF16)

    return pl.pallas_call(
        _ordered_after(body, 6, deps), name=name, grid=(t // tm,),
        in_specs=[_rows(tm, d), _rows(tm, d), _const((1, d)), _const((nk, fk, d)), _kmajor(nk, tm, fk), _kmajor(nk, tm, fk)]
        + [ANY_SPEC] * len(deps),
        out_specs=[_rows(tm, d), _kmajor(nk, tm, fk), _kmajor(nk, tm, fk), _const((1, d))],
        out_shape=[jax.ShapeDtypeStruct((t, d), BF16), jax.ShapeDtypeStruct((nk, t, fk), BF16),
                   jax.ShapeDtypeStruct((nk, t, fk), BF16), jax.ShapeDtypeStruct((1, d), F32)],
        compiler_params=_params("arbitrary"))(dh, f, g_post, wd4, s4, t4, *deps)


def dx_norm_bwd(pairs, h, g_pre, dh_in, name, deps=()):
    t, d = h.shape
    tm = _tile(t, 256)
    n = len(pairs)

    def body(*refs):
        dys, ws = refs[:n], refs[n:2 * n]
        h_ref, g_ref, dhi_ref, dho_ref, dg_ref = refs[2 * n:]
        acc = None
        for (_, w4, sections), dy_ref, w_ref in zip(pairs, dys, ws):
            if sections:
                wide = w4.shape[2]
                edges = sorted(set(range(0, 5 * d + 1, d)) | set(range(0, 5 * d + 1, wide)))
                parts = [_dot_nt(dy_ref[DZ_SLOT[lo // d], :, lo % d:lo % d + hi - lo], w_ref[lo // wide, :, lo % wide:lo % wide + hi - lo])
                         for lo, hi in zip(edges[:-1], edges[1:])]
            else:
                parts = [_dot(dy_ref[k], w_ref[k]) for k in range(w4.shape[0])]
            for part in parts:
                acc = part if acc is None else acc + part
        dx, dg = _rms_bwd(h_ref[...], g_ref[...], acc)
        dho_ref[...] = dhi_ref[...] + dx
        _accumulate(dg_ref, dg)

    dy_specs = [_kmajor(dy.shape[0], tm, dy.shape[2]) for dy, _, _ in pairs]
    return pl.pallas_call(
        _ordered_after(body, 2 * n + 3, deps), name=name, grid=(t // tm,),
        in_specs=dy_specs + [_const(w4.shape) for _, w4, _ in pairs] + [_rows(tm, d), _const((1, d)), _rows(tm, d)]
        + [ANY_SPEC] * len(deps),
        out_specs=[_rows(tm, d), _const((1, d))],
        out_shape=[jax.ShapeDtypeStruct((t, d), F32), jax.ShapeDtypeStruct((1, d), F32)],
        compiler_params=_params("arbitrary"))(*[dy for dy, _, _ in pairs], *[w4 for _, w4, _ in pairs], h, g_pre, dh_in, *deps)


def mixer_bwd_y(dh, m, g_post, w_o, ya, yb, z, woa, wob, name, deps=()):
    t, d = dh.shape
    tm = _tile(t, 256)

    def body(dh_ref, m_ref, gp_ref, wo_ref, ya_ref, yb_ref, ga_ref, gb_ref, wa_ref, wb_ref,
             dm_ref, dya_ref, dyb_ref, dz_ref, da_ref, db_ref, dgp_ref):
        dm, dgp = _rms_bwd(m_ref[...].astype(F32), gp_ref[...], dh_ref[...])
        dm = dm.astype(BF16)
        dm_ref[...] = dm
        _accumulate(dgp_ref, dgp)
        dy = _dot_nt(dm, wo_ref[...])
        sa = _sigmoid(ga_ref[...].astype(F32))
        sb = _sigmoid(gb_ref[...].astype(F32))
        dya = (dy * sa).astype(BF16)
        dyb = (dy * sb).astype(BF16)
        dya_ref[...] = dya
        dyb_ref[...] = dyb
        dz_ref[0] = (dy * ya_ref[...].astype(F32) * sa * (1.0 - sa)).astype(BF16)
        dz_ref[1] = (dy * yb_ref[...].astype(F32) * sb * (1.0 - sb)).astype(BF16)
        da_ref[...] = _dot_nt(dya, wa_ref[...]).astype(BF16)
        db_ref[...] = _dot_nt(dyb, wb_ref[...]).astype(BF16)

    return pl.pallas_call(
        _ordered_after(body, 10, deps), name=name, grid=(t // tm,),
        in_specs=[_rows(tm, d), _rows(tm, d), _const((1, d)), _const((d, d)), _rows(tm, d), _rows(tm, d),
                  _rows(tm, d, 3), _rows(tm, d, 4), _const((d, d)), _const((d, d))] + [ANY_SPEC] * len(deps),
        out_specs=[_rows(tm, d)] * 3 + [pl.BlockSpec((2, tm, d), lambda i: (0, i, 0))] + [_rows(tm, d)] * 2 + [_const((1, d))],
        out_shape=[jax.ShapeDtypeStruct((t, d), BF16)] * 3 + [jax.ShapeDtypeStruct((5, t, d), BF16)]
        + [jax.ShapeDtypeStruct((t, d), BF16)] * 2 + [jax.ShapeDtypeStruct((1, d), F32)],
        compiler_params=_params("arbitrary"))(dh, m, g_post, w_o, ya, yb, z, z, woa, wob, *deps)


def sgu_bwd(z, da, dz, norm_g, sgu_w, sgu_b3, name):
    t, d = da.shape
    ng = sgu_w.shape[0]
    dg = d // ng
    tm = _tile(t, 256)
    steps = t // tm

    def body(zu_ref, zv_ref, da_ref, ng_ref, w_ref, b_ref, _, dz_ref, dw_ref, db_ref, dng_ref, dvn_ref, dsv_ref):
        i = pl.program_id(0)
        zv = zv_ref[...].astype(F32)
        zu = zu_ref[...].astype(F32)
        v, gv = _gelu_and_grad(zv)
        vhat, r = _layernorm_parts(v)
        gain = ng_ref[...]
        vn = (vhat * gain).astype(BF16)
        u, gu = _gelu_and_grad(zu)
        dav = da_ref[...].astype(F32)
        mask = _causal_mask()

        @pl.when(i == 0)
        def _():
            dw_ref[...] = jnp.zeros_like(dw_ref)
            dsv_ref[...] = jnp.zeros_like(dsv_ref)

        for g in range(ng):
            wg = jnp.where(mask, w_ref[g], 0.0).astype(BF16)
            dw = jnp.zeros((CHUNK, CHUNK), F32)
            dsv_sum = jnp.zeros((CHUNK, dg), F32)
            for ci in range(tm // CHUNK):
                rs, cs = slice(ci * CHUNK, (ci + 1) * CHUNK), slice(g * dg, (g + 1) * dg)
                vn_blk = vn[rs, cs]
                sv = _dot(wg, vn_blk) + b_ref[g]
                dz_ref[0, rs, cs] = (dav[rs, cs] * sv * gu[rs, cs]).astype(BF16)
                dsv = dav[rs, cs] * u[rs, cs]
                dsv_sum += dsv
                dsv = dsv.astype(BF16)
                dw += _dot_nt(dsv, vn_blk)
                dvn_ref[rs, cs] = _dot_tn(wg, dsv)
            dw_ref[g] += dw
            dsv_ref[:, cs] += dsv_sum

        dvn = dvn_ref[...]
        _accumulate(dng_ref, jnp.sum(dvn * vhat, axis=0, keepdims=True))
        dvh = dvn * gain
        dv = r * (dvh - jnp.mean(dvh, axis=-1, keepdims=True) - vhat * jnp.mean(dvh * vhat, axis=-1, keepdims=True))
        dz_ref[1] = (dv * gv).astype(BF16)

        @pl.when(i == steps - 1)
        def _():
            for g in range(ng):
                dw_ref[g] = jnp.where(mask, dw_ref[g], 0.0)
                row_sum = jnp.sum(dsv_ref[:, g * dg:(g + 1) * dg], axis=1, keepdims=True)
                db_ref[g] = jnp.broadcast_to(row_sum, (CHUNK, CHUNK))

    return pl.pallas_call(
        body, name=name, grid=(steps,),
        in_specs=[_rows(tm, d, 0), _rows(tm, d, 1), _rows(tm, d), _const((1, d)), _const((ng, CHUNK, CHUNK)), _const((ng, CHUNK, 1)),
                  ANY_SPEC],
        out_specs=[pl.BlockSpec((2, tm, d), lambda i: (DZ_SLOT[0] // 2, i, 0)), _const((ng, CHUNK, CHUNK)),
                   _const((ng, CHUNK, CHUNK)), _const((1, d))],
        out_shape=[jax.ShapeDtypeStruct(dz.shape, BF16), jax.ShapeDtypeStruct((ng, CHUNK, CHUNK), F32),
                   jax.ShapeDtypeStruct((ng, CHUNK, CHUNK), F32), jax.ShapeDtypeStruct((1, d), F32)],
        scratch_shapes=[pltpu.VMEM((tm, d), F32), pltpu.VMEM((CHUNK, d), F32)], input_output_aliases={6: 0},
        compiler_params=_params("arbitrary"))(z, z, da, norm_g, sgu_w, sgu_b3, dz)


def pool_bwd(db, diff, dz, pool_w, pool_scale, name):
    t, d = db.shape
    ng = pool_w.shape[0]
    dg = d // ng
    tm = _tile(t, 256)
    per = tm // HALO
    steps = t // tm

    def body(db_ref, next_ref, diff_ref, w_ref, s_ref, _, dc_ref, dw_ref, ds_ref):
        i = pl.program_id(0)
        dbc = db_ref[...].astype(F32)
        nxt = jnp.where(i < steps - 1, next_ref[...].astype(F32), 0.0)
        ext = jnp.concatenate([dbc, nxt], axis=0)
        rows = tm + HALO
        tok = i * tm + lax.broadcasted_iota(jnp.int32, (rows, 1), 0)

        @pl.when(i == 0)
        def _():
            dw_ref[...] = jnp.zeros_like(dw_ref)
            ds_ref[...] = jnp.zeros_like(ds_ref)

        for g, win in enumerate(POOL_WINDOWS):
            cs = slice(g * dg, (g + 1) * dg)
            dp = (ext[:, cs] * s_ref[:, cs]).astype(BF16)
            dd = _dot_nt(dp, w_ref[g])
            s = dd * (1.0 / jnp.minimum(tok + 1, win).astype(F32))
            sh = 1
            while sh < win:
                s = s + pltpu.roll(s, rows - sh, 0)
                sh *= 2
            dc_ref[0, :, cs] = (s[:tm] - dd[:tm]).astype(BF16)
            dfg = diff_ref[:, cs]
            ds_ref[:, cs] += jnp.sum(dbc[:, cs] * _dot(dfg, w_ref[g]), axis=0, keepdims=True)
            dw_ref[g] += _dot_tn(dfg, dp[:tm])

    return pl.pallas_call(
        body, name=name, grid=(steps,),
        in_specs=[_rows(tm, d), pl.BlockSpec((HALO, d), lambda i: (jnp.minimum((i + 1) * per, t // HALO - 1), 0)),
                  _rows(tm, d), _const((ng, dg, dg)), _const((1, d)), ANY_SPEC],
        out_specs=[pl.BlockSpec((1, tm, d), lambda i: (DZ_SLOT[2], i, 0)), _const((ng, dg, dg)), _const((1, d))],
        out_shape=[jax.ShapeDtypeStruct(dz.shape, BF16), jax.ShapeDtypeStruct((ng, dg, dg), F32), jax.ShapeDtypeStruct((1, d), F32)],
        input_output_aliases={5: 0},
        compiler_params=_params("arbitrary"))(db, db, diff, pool_w, pool_scale, dz)


def dw_tn(x, dy, nk, name, x_kmajor=False, dy_mode="same", deps=()):
    t = x.shape[-2]
    kx = x.shape[-1]
    n = dy.shape[-1] // nk if dy_mode == "cols" else dy.shape[-1]
    tt = _tile(t, DW_TOKENS)
    steps = t // tt

    def body(x_ref, dy_ref, o_ref, acc_ref):
        s = pl.program_id(1)
        part = _dot_tn(x_ref[0] if x_kmajor else x_ref[...], dy_ref[0] if dy_mode == "kmajor" else dy_ref[...])
        if steps == 1:
            o_ref[0] = part.astype(BF16)
            return

        @pl.when(s == 0)
        def _():
            acc_ref[...] = jnp.zeros_like(acc_ref)

        acc_ref[...] += part

        @pl.when(s == steps - 1)
        def _():
            o_ref[0] = acc_ref[...].astype(BF16)

    x_spec = pl.BlockSpec((1, tt, kx), lambda k, s: (k, s, 0)) if x_kmajor else pl.BlockSpec((tt, kx), lambda k, s: (s, 0))
    dy_spec = {"kmajor": pl.BlockSpec((1, tt, n), lambda k, s: (k, s, 0)), "cols": pl.BlockSpec((tt, n), lambda k, s: (s, k)),
               "same": pl.BlockSpec((tt, n), lambda k, s: (s, 0))}[dy_mode]
    return pl.pallas_call(
        _ordered_after(body, 2, deps), name=name, grid=(nk, steps), in_specs=[x_spec, dy_spec] + [ANY_SPEC] * len(deps),
        out_specs=pl.BlockSpec((1, kx, n), lambda k, s: (k, 0, 0)), out_shape=jax.ShapeDtypeStruct((nk, kx, n), BF16),
        scratch_shapes=[pltpu.VMEM((kx, n) if steps > 1 else (8, 128), F32)],
        compiler_params=_params("parallel", "arbitrary"))(x, dy, *deps)


def dw_in_tiles(xn, dz, nk, name):
    t, d = xn.shape
    sections = len(DZ_SLOT)
    per_section = d // DW_IN_TILE
    per_shard = sections * per_section // nk

    def body(x_ref, dy_ref, o_ref):
        o_ref[0] = _dot_tn(x_ref[...], dy_ref[0]).astype(BF16)

    def slot(j):
        return (j // per_section + DZ_SLOT[0]) % sections

    return pl.pallas_call(
        body, name=name, grid=(sections * per_section,),
        in_specs=[pl.BlockSpec((t, d), lambda j: (0, 0)), pl.BlockSpec((1, t, DW_IN_TILE), lambda j: (slot(j), 0, j % per_section))],
        out_specs=pl.BlockSpec((1, d, DW_IN_TILE), lambda j: (j // per_shard, 0, j % per_shard)),
        out_shape=jax.ShapeDtypeStruct((nk, d, sections * d // nk), BF16), compiler_params=_params("parallel"))(xn, dz)


def _place():
    x, y, c = lax.axis_index("x"), lax.axis_index("y"), lax.axis_index("c")
    chips = [((1 - x) if fx else x, (1 - y) if fy else y) for fx, fy in OTHER_CHIPS]
    return x, y, c, chips


def _sibling_handshake(x, y, c):
    barrier = pltpu.get_barrier_semaphore()
    pl.semaphore_signal(barrier, inc=1, device_id=(x, y, 1 - c), device_id_type=MESH_IDS)
    pl.semaphore_wait(barrier, 1)


def _half(rows, which):
    return pl.ds(pl.multiple_of(which * (rows // 2), 16), rows // 2)


def _hbm(a):
    return pltpu.with_memory_space_constraint(a, pltpu.HBM)


def _by_shape(arrays):
    buckets = {}
    for i, a in enumerate(arrays):
        buckets.setdefault(a.shape, []).append(i)
    return list(buckets.values())


def cast_place(shards, chip, name, deps=(), plain=False):
    n = len(shards)

    def body(chip_ref, *refs):
        outs = refs[n + len(deps):]
        for a, w_ref in enumerate(refs[:n]):
            cast = w_ref[...].astype(BF16)
            outs[a][0] = cast
            if plain:
                outs[n + a][...] = cast

    def rows(s):
        return (s.shape[0] // ROW_STEPS, s.shape[1])

    out = pl.pallas_call(
        body, name=name,
        grid_spec=pltpu.PrefetchScalarGridSpec(
            num_scalar_prefetch=1, grid=(ROW_STEPS,),
            in_specs=[pl.BlockSpec(rows(s), lambda i, chip_ref: (i, 0)) for s in shards] + [ANY_SPEC] * len(deps),
            out_specs=[pl.BlockSpec((1,) + rows(s), lambda i, chip_ref: (chip_ref[0], i, 0)) for s in shards]
            + [pl.BlockSpec(rows(s), lambda i, chip_ref: (i, 0)) for s in shards] * plain),
        out_shape=[jax.ShapeDtypeStruct((N_CHIPS,) + s.shape, BF16) for s in shards]
        + [jax.ShapeDtypeStruct(s.shape, BF16) for s in shards] * plain,
        compiler_params=_params("parallel"))(chip, *shards, *deps)
    return (out[:n], out[n:]) if plain else out


def _gather_copy(buf, sends, recvs, i, j, me, chip_xy, c):
    cx, cy = chip_xy
    mine = _half(buf.shape[1], c)
    return pltpu.make_async_remote_copy(
        src_ref=buf.at[me, mine], dst_ref=buf.at[me, mine], send_sem=sends.at[3 * i + j], recv_sem=recvs.at[3 * i + j],
        device_id=(cx, cy, c), device_id_type=MESH_IDS)


def allgather_start(bufs, name, deps=()):
    n = len(bufs)
    extra = len(deps)

    def body(*refs):
        ins = refs[:n]
        sends, recvs = refs[n + extra], refs[n + extra + 1]
        token = refs[2 * n + extra + 2]
        x, y, c, chips = _place()
        for i in range(n):
            for j, chip_xy in enumerate(chips):
                _gather_copy(ins[i], sends, recvs, i, j, 2 * x + y, chip_xy, c).start()
        token[...] = jnp.zeros_like(token)

    out = pl.pallas_call(
        body, name=name, in_specs=[HBM_SPEC] * n + [ANY_SPEC] * extra,
        out_specs=[SEM_SPEC, SEM_SPEC] + [HBM_SPEC] * n + [VMEM_SPEC],
        out_shape=[pltpu.SemaphoreType.DMA((3 * n,)), pltpu.SemaphoreType.DMA((3 * n,))]
        + [pltpu.HBM(b.shape, b.dtype) for b in bufs] + [jax.ShapeDtypeStruct((8, 128), F32)],
        input_output_aliases={i: i + 2 for i in range(n)},
        compiler_params=pltpu.CompilerParams(has_side_effects=DATAFLOW))(*[_hbm(b) for b in bufs], *deps)
    return out[0], out[1], list(out[2:2 + n]), out[2 + n]


def allgather_wait(sends, recvs, bufs, after, name, first=0):
    n = len(bufs)
    after = tuple(after) if isinstance(after, (tuple, list)) else (after,)

    def body(*refs):
        ins = refs[:n]
        send_sems, recv_sems = refs[n], refs[n + 1]
        x, y, c, chips = _place()
        for i in range(n):
            for j, (cx, cy) in enumerate(chips):
                mine = _half(ins[i].shape[1], c)
                cp = pltpu.make_async_remote_copy(
                    src_ref=ins[i].at[2 * x + y, mine], dst_ref=ins[i].at[2 * cx + cy, mine],
                    send_sem=send_sems.at[3 * (first + i) + j], recv_sem=recv_sems.at[3 * (first + i) + j],
                    device_id=(cx, cy, c), device_id_type=MESH_IDS)
                cp.wait_send()
                cp.wait_recv()

    return pl.pallas_call(
        body, name=name, in_specs=[HBM_SPEC] * n + [SEM_SPEC, SEM_SPEC] + [ANY_SPEC] * len(after), out_specs=[HBM_SPEC] * n,
        out_shape=[pltpu.HBM(b.shape, b.dtype) for b in bufs], input_output_aliases={i: i for i in range(n)},
        compiler_params=pltpu.CompilerParams(has_side_effects=DATAFLOW))(*bufs, sends, recvs, *after)


def d2d_forward(bufs, name):
    n = len(bufs)

    def body(*refs):
        ins = refs[:n]
        send_sems, recv_sems = refs[2 * n:]
        x, y, c, chips = _place()
        _sibling_handshake(x, y, c)
        copies = []
        for i in range(n):
            mine = _half(ins[i].shape[1], c)
            for j, (cx, cy) in enumerate(chips):
                landed = ins[i].at[2 * cx + cy, mine]
                cp = pltpu.make_async_remote_copy(
                    src_ref=landed, dst_ref=landed, send_sem=send_sems.at[i, j], recv_sem=recv_sems.at[i, j],
                    device_id=(x, y, 1 - c), device_id_type=MESH_IDS)
                cp.start()
                copies.append(cp)
        for i in range(n):
            theirs = _half(ins[i].shape[1], 1 - c)
            for j, (cx, cy) in enumerate(chips):
                passed = ins[i].at[2 * cx + cy, theirs]
                pltpu.make_async_remote_copy(
                    src_ref=passed, dst_ref=passed, send_sem=send_sems.at[i, j], recv_sem=recv_sems.at[i, j],
                    device_id=(x, y, 1 - c), device_id_type=MESH_IDS).wait_recv()
        for cp in copies:
            cp.wait_send()

    return pl.pallas_call(
        body, name=name, in_specs=[HBM_SPEC] * n, out_specs=[HBM_SPEC] * n,
        out_shape=[jax.ShapeDtypeStruct(b.shape, b.dtype) for b in bufs], input_output_aliases={i: i for i in range(n)},
        scratch_shapes=[pltpu.SemaphoreType.DMA((n, 3))] * 2,
        compiler_params=pltpu.CompilerParams(has_side_effects=True, collective_id=SIBLING_BARRIER))(*bufs)


def _forward_copy(buf, sends, recvs, i, j, chip_xy, x, y, c):
    cx, cy = chip_xy
    rows = buf.shape[1]
    return pltpu.make_async_remote_copy(
        src_ref=buf.at[2 * cx + cy, _half(rows, c)], dst_ref=buf.at[2 * cx + cy, _half(rows, 1 - c)],
        send_sem=sends.at[3 * i + j], recv_sem=recvs.at[3 * i + j], device_id=(x, y, 1 - c), device_id_type=MESH_IDS)


def d2d_forward_start(bufs, name):
    n = len(bufs)

    def body(*refs):
        ins = refs[:n]
        sends, recvs = refs[n], refs[n + 1]
        token = refs[2 * n + 2]
        x, y, c, chips = _place()
        _sibling_handshake(x, y, c)
        for i in range(n):
            mine = _half(ins[i].shape[1], c)
            for j, (cx, cy) in enumerate(chips):
                landed = ins[i].at[2 * cx + cy, mine]
                pltpu.make_async_remote_copy(
                    src_ref=landed, dst_ref=landed, send_sem=sends.at[3 * i + j], recv_sem=recvs.at[3 * i + j],
                    device_id=(x, y, 1 - c), device_id_type=MESH_IDS).start()
        token[...] = jnp.zeros_like(token)

    out = pl.pallas_call(
        body, name=name, in_specs=[HBM_SPEC] * n,
        out_specs=[SEM_SPEC, SEM_SPEC] + [HBM_SPEC] * n + [VMEM_SPEC],
        out_shape=[pltpu.SemaphoreType.DMA((3 * n,)), pltpu.SemaphoreType.DMA((3 * n,))]
        + [pltpu.HBM(b.shape, b.dtype) for b in bufs] + [jax.ShapeDtypeStruct((8, 128), F32)],
        input_output_aliases={i: i + 2 for i in range(n)},
        compiler_params=pltpu.CompilerParams(has_side_effects=DATAFLOW, collective_id=SIBLING_BARRIER))(*[_hbm(b) for b in bufs])
    return (out[0], out[1], list(out[2:2 + n])), out[2 + n]


def d2d_forward_wait(started, after, name):
    sends, recvs, bufs = started
    n = len(bufs)

    def body(*refs):
        ins = refs[:n]
        send_sems, recv_sems = refs[n], refs[n + 1]
        x, y, c, chips = _place()
        for i in range(n):
            for j, chip_xy in enumerate(chips):
                cp = _forward_copy(ins[i], send_sems, recv_sems, i, j, chip_xy, x, y, c)
                cp.wait_send()
                cp.wait_recv()

    return pl.pallas_call(
        body, name=name, in_specs=[HBM_SPEC] * n + [SEM_SPEC, SEM_SPEC, ANY_SPEC], out_specs=[HBM_SPEC] * n,
        out_shape=[pltpu.HBM(b.shape, b.dtype) for b in bufs], input_output_aliases={i: i for i in range(n)},
        compiler_params=pltpu.CompilerParams(has_side_effects=DATAFLOW))(*bufs, sends, recvs, after)


def _sibling_copy(src, land, sends, recvs, i, x, y, c, halves):
    part = src.at[:, _half(src.shape[1], 1 - c)] if halves else src
    return pltpu.make_async_remote_copy(
        src_ref=part, dst_ref=land, send_sem=sends.at[i], recv_sem=recvs.at[i], device_id=(x, y, 1 - c),
        device_id_type=MESH_IDS)


def sibling_start(arrays, halves, name):
    n = len(arrays)
    lands = [lax.empty((a.shape[0], a.shape[1] // 2, a.shape[2]) if halves else a.shape, a.dtype) for a in arrays]

    def body(*refs):
        srcs, zones = refs[:n], refs[n:2 * n]
        sends, recvs = refs[2 * n], refs[2 * n + 1]
        token = refs[4 * n + 2]
        x, y, c, _ = _place()
        _sibling_handshake(x, y, c)
        for i in range(n):
            _sibling_copy(srcs[i], zones[i], sends, recvs, i, x, y, c, halves).start()
        token[...] = jnp.zeros_like(token)

    out = pl.pallas_call(
        body, name=name, in_specs=[HBM_SPEC] * (2 * n),
        out_specs=[SEM_SPEC, SEM_SPEC] + [HBM_SPEC] * (2 * n) + [VMEM_SPEC],
        out_shape=[pltpu.SemaphoreType.DMA((n,)), pltpu.SemaphoreType.DMA((n,))]
        + [pltpu.HBM(a.shape, a.dtype) for a in arrays + lands] + [jax.ShapeDtypeStruct((8, 128), F32)],
        input_output_aliases={i: i + 2 for i in range(2 * n)},
        compiler_params=pltpu.CompilerParams(has_side_effects=DATAFLOW, collective_id=SIBLING_BARRIER))(
            *[_hbm(a) for a in arrays + lands])
    return (out[0], out[1], list(out[2:2 + n]), list(out[2 + n:2 + 2 * n])), out[2 + 2 * n]


def sibling_wait(started, halves, after, name):
    sends, recvs, arrays, lands = started
    n = len(arrays)
    after = tuple(after) if isinstance(after, (tuple, list)) else (after,)

    def body(*refs):
        srcs, zones = refs[:n], refs[n:2 * n]
        send_sems, recv_sems = refs[2 * n], refs[2 * n + 1]
        x, y, c, _ = _place()
        for i in range(n):
            cp = _sibling_copy(srcs[i], zones[i], send_sems, recv_sems, i, x, y, c, halves)
            cp.wait_send()
            cp.wait_recv()

    out = pl.pallas_call(
        body, name=name, in_specs=[HBM_SPEC] * (2 * n) + [SEM_SPEC, SEM_SPEC] + [ANY_SPEC] * len(after),
        out_specs=[HBM_SPEC] * (2 * n),
        out_shape=[pltpu.HBM(a.shape, a.dtype) for a in arrays + lands], input_output_aliases={i: i for i in range(2 * n)},
        compiler_params=pltpu.CompilerParams(has_side_effects=DATAFLOW))(*arrays, *lands, sends, recvs, *after)
    return list(out[:n]), list(out[n:])


def _scatter_copy(src, land, sends, recvs, i, j, chip_xy, c):
    cx, cy = chip_xy
    return pltpu.make_async_remote_copy(
        src_ref=src.at[2 * cx + cy], dst_ref=land.at[j], send_sem=sends.at[3 * i + j], recv_sem=recvs.at[3 * i + j],
        device_id=(cx, cy, c), device_id_type=MESH_IDS)


def scatter_start(sums, name):
    n = len(sums)
    lands = [lax.empty((3,) + s.shape[1:], s.dtype) for s in sums]

    def body(*refs):
        srcs, zones = refs[:n], refs[n:2 * n]
        sends, recvs = refs[2 * n], refs[2 * n + 1]
        token = refs[4 * n + 2]
        _, _, c, chips = _place()
        for i in range(n):
            for j, chip_xy in enumerate(chips):
                _scatter_copy(srcs[i], zones[i], sends, recvs, i, j, chip_xy, c).start()
        token[...] = jnp.zeros_like(token)

    out = pl.pallas_call(
        body, name=name, in_specs=[HBM_SPEC] * (2 * n),
        out_specs=[SEM_SPEC, SEM_SPEC] + [HBM_SPEC] * (2 * n) + [VMEM_SPEC],
        out_shape=[pltpu.SemaphoreType.DMA((3 * n,)), pltpu.SemaphoreType.DMA((3 * n,))]
        + [pltpu.HBM(a.shape, a.dtype) for a in sums + lands] + [jax.ShapeDtypeStruct((8, 128), F32)],
        input_output_aliases={i: i + 2 for i in range(2 * n)},
        compiler_params=pltpu.CompilerParams(has_side_effects=DATAFLOW))(*[_hbm(a) for a in sums + lands])
    return out[0], out[1], list(out[2:2 + n]), list(out[2 + n:2 + 2 * n]), out[2 + 2 * n]


def scatter_wait(sends, recvs, sums, lands, after, name):
    n = len(sums)

    def body(*refs):
        srcs, zones = refs[:n], refs[n:2 * n]
        send_sems, recv_sems = refs[2 * n], refs[2 * n + 1]
        _, _, c, chips = _place()
        for i in range(n):
            for j, chip_xy in enumerate(chips):
                cp = _scatter_copy(srcs[i], zones[i], send_sems, recv_sems, i, j, chip_xy, c)
                cp.wait_send()
                cp.wait_recv()

    out = pl.pallas_call(
        body, name=name, in_specs=[HBM_SPEC] * (2 * n) + [SEM_SPEC, SEM_SPEC, ANY_SPEC], out_specs=[HBM_SPEC] * (2 * n),
        out_shape=[pltpu.HBM(a.shape, a.dtype) for a in sums + lands], input_output_aliases={i: i for i in range(2 * n)},
        compiler_params=pltpu.CompilerParams(has_side_effects=DATAFLOW))(*sums, *lands, sends, recvs, after)
    return list(out[:n]), list(out[n:])


def add_halves(grads, recvs, core, name):
    n = len(grads)
    nk = grads[0].shape[0]
    views = [g.reshape(nk, 2, g.shape[1] // 2, g.shape[2]) for g in grads]

    def body(core_ref, *refs):
        for g_ref, r_ref, o_ref in zip(refs[:n], refs[n:2 * n], refs[2 * n:]):
            o_ref[0] = (g_ref[0, 0].astype(F32) + r_ref[0].astype(F32)).astype(BF16)

    return pl.pallas_call(
        body, name=name,
        grid_spec=pltpu.PrefetchScalarGridSpec(
            num_scalar_prefetch=1, grid=(nk,),
            in_specs=[pl.BlockSpec((1, 1) + v.shape[2:], lambda k, core_ref: (k, core_ref[0], 0, 0)) for v in views]
            + [pl.BlockSpec((1,) + r.shape[1:], lambda k, core_ref: (k, 0, 0)) for r in recvs],
            out_specs=[pl.BlockSpec((1,) + r.shape[1:], lambda k, core_ref: (k, 0, 0)) for r in recvs]),
        out_shape=[jax.ShapeDtypeStruct(r.shape, BF16) for r in recvs], compiler_params=_params("parallel"))(core, *views, *recvs)


def add_chips(sums, lands, chip, name):
    n = len(sums)

    def body(chip_ref, *refs):
        for s_ref, r_ref, o_ref in zip(refs[:n], refs[n:2 * n], refs[2 * n:]):
            o_ref[...] = ((s_ref[0].astype(F32) + r_ref[0].astype(F32)) + r_ref[1].astype(F32)) + r_ref[2].astype(F32)

    def rows(s):
        return (s.shape[1] // ROW_STEPS, s.shape[2])

    return pl.pallas_call(
        body, name=name,
        grid_spec=pltpu.PrefetchScalarGridSpec(
            num_scalar_prefetch=1, grid=(ROW_STEPS,),
            in_specs=[pl.BlockSpec((1,) + rows(s), lambda i, chip_ref: (chip_ref[0], i, 0)) for s in sums]
            + [pl.BlockSpec((3,) + rows(s), lambda i, chip_ref: (0, i, 0)) for s in sums],
            out_specs=[pl.BlockSpec(rows(s), lambda i, chip_ref: (i, 0)) for s in sums]),
        out_shape=[jax.ShapeDtypeStruct(s.shape[1:], F32) for s in sums], compiler_params=_params("parallel"))(chip, *sums, *lands)


def _adamw_math(w, g, m, v):
    m = ADAM_B1 * m + (1.0 - ADAM_B1) * g
    v = ADAM_B2 * v + (1.0 - ADAM_B2) * (g * g)
    m_hat = m / (1.0 - ADAM_B1 ** ADAM_STEP)
    v_hat = v / (1.0 - ADAM_B2 ** ADAM_STEP)
    return -ADAM_LR * (m_hat / (jnp.sqrt(v_hat) + ADAM_EPS) + ADAM_WD * w), m, v


def adamw_one_half(ws, gs, ms, vs, core, name, own, into=None):
    n = len(ws)
    r, cdim = ws[0].shape
    half = r // 2
    tr = _tile(half, max(8, 256 // n))
    steps = half // tr
    prior = () if into is None else tuple(a for group in into for a in group)

    def body(core_ref, *refs):
        ins, outs = refs[:4 * n], refs[4 * n + len(prior):]
        for a in range(n):
            w_ref, g_ref, m_ref, v_ref = ins[a::n]
            g = g_ref[...]
            outs[a][...] = g
            outs[n + a][...], outs[2 * n + a][...], outs[3 * n + a][...] = _adamw_math(w_ref[...], g, m_ref[...], v_ref[...])

    def rows(i, core_ref):
        return ((core_ref[0] if own else 1 - core_ref[0]) * steps + i, 0)

    whole = pl.BlockSpec((tr, cdim), rows)
    part = pl.BlockSpec((tr, cdim), lambda i, core_ref: (i, 0))
    out = pl.pallas_call(
        body, name=name,
        grid_spec=pltpu.PrefetchScalarGridSpec(
            num_scalar_prefetch=1, grid=(steps,),
            in_specs=[whole] * n + [part] * n + [whole] * (2 * n) + [ANY_SPEC] * len(prior), out_specs=[whole] * (4 * n)),
        out_shape=[jax.ShapeDtypeStruct((r, cdim), F32)] * (4 * n),
        input_output_aliases={1 + 4 * n + j: j for j in range(len(prior))},
        compiler_params=_params("parallel"))(core, *ws, *gs, *ms, *vs, *prior)
    return [out[k * n:(k + 1) * n] for k in range(4)]


def adamw_halves_sparsecore(ws, owns, others, ms, vs, name):
    n = len(ws)
    r, cdim = ws[0].shape
    half_groups = r // 2 // SC_ROWS
    per_tile = -(-2 * half_groups // SC_TILES)

    def body(*refs):
        ins, outs, (wb, gb, mb, vb) = refs[:5 * n], refs[5 * n:9 * n], refs[9 * n:]
        tile = lax.axis_index("sc_tile") * 2 + lax.axis_index("sc_core")
        core = lax.axis_index("c")

        @pl.loop(0, per_tile)
        def _(it):
            group = tile + SC_TILES * it

            @pl.when(group < 2 * half_groups)
            def _():
                rows = pl.ds(group * SC_ROWS, SC_ROWS)
                in_own = (group < half_groups) == (core == 0)
                half_rows = pl.ds((group % half_groups) * SC_ROWS, SC_ROWS)
                for a in range(n):
                    w_hbm, own_hbm, other_hbm, m_hbm, v_hbm = ins[a::n]
                    g_out, d_out, m_out, v_out = outs[a::n]
                    pltpu.sync_copy(w_hbm.at[rows], wb)
                    pltpu.sync_copy(m_hbm.at[rows], mb)
                    pltpu.sync_copy(v_hbm.at[rows], vb)

                    @pl.when(in_own)
                    def _():
                        pltpu.sync_copy(own_hbm.at[half_rows], gb)

                    @pl.when(jnp.logical_not(in_own))
                    def _():
                        pltpu.sync_copy(other_hbm.at[half_rows], gb)

                    pltpu.sync_copy(gb, g_out.at[rows])

                    @pl.loop(0, SC_ROWS)
                    def _(row):
                        @pl.loop(0, cdim, step=SC_LANES)
                        def _(col):
                            at = (row, pl.ds(col, SC_LANES))
                            wb[at], mb[at], vb[at] = _adamw_math(wb[at], gb[at], mb[at], vb[at])

                    pltpu.sync_copy(wb, d_out.at[rows])
                    pltpu.sync_copy(mb, m_out.at[rows])
                    pltpu.sync_copy(vb, v_out.at[rows])

    out = pl.kernel(
        body, name=name, out_type=[jax.ShapeDtypeStruct((r, cdim), F32)] * (4 * n),
        mesh=plsc.VectorSubcoreMesh(core_axis_name="sc_core", subcore_axis_name="sc_tile"),
        scratch_types=[pltpu.VMEM((SC_ROWS, cdim), F32)] * 4)(*ws, *owns, *others, *ms, *vs)
    return [out[a::n] for a in range(n)]


def small_allreduce_adamw(g, w, m, v, deps=()):
    r, cdim = g.shape

    def body(g_ref, w_ref, m_ref, v_ref, go_ref, d_ref, mo_ref, vo_ref, pair, quad, d2d_send, d2d_recv, ici_send, ici_recv):
        x, y, c, chips = _place()
        me = 2 * x + y
        pair[c] = g_ref[...]
        swap = pltpu.make_async_remote_copy(
            src_ref=g_ref, dst_ref=pair.at[c], send_sem=d2d_send, recv_sem=d2d_recv, device_id=(x, y, 1 - c),
            device_id_type=MESH_IDS)
        swap.start()
        swap.wait()
        quad[me] = pair[0] + pair[1]
        copies = []
        for j, (cx, cy) in enumerate(chips):
            cp = pltpu.make_async_remote_copy(
                src_ref=quad.at[me], dst_ref=quad.at[me], send_sem=ici_send.at[j], recv_sem=ici_recv.at[j],
                device_id=(cx, cy, c), device_id_type=MESH_IDS)
            cp.start()
            copies.append(cp)
        for j, (cx, cy) in enumerate(chips):
            slot = quad.at[2 * cx + cy]
            pltpu.make_async_remote_copy(
                src_ref=slot, dst_ref=slot, send_sem=ici_send.at[j], recv_sem=ici_recv.at[j], device_id=(cx, cy, c),
                device_id_type=MESH_IDS).wait_recv()
        for cp in copies:
            cp.wait_send()
        total = (quad[0] + quad[1]) + (quad[2] + quad[3])
        go_ref[...] = total
        d_ref[...], mo_ref[...], vo_ref[...] = _adamw_math(w_ref[...], total, m_ref[...], v_ref[...])

    return pl.pallas_call(
        _ordered_after(body, 4, deps), name="small_allreduce_adamw", in_specs=[VMEM_SPEC] * 4 + [ANY_SPEC] * len(deps),
        out_specs=[VMEM_SPEC] * 4,
        out_shape=[jax.ShapeDtypeStruct((r, cdim), F32)] * 4,
        scratch_shapes=[pltpu.VMEM((2, r, cdim), F32), pltpu.VMEM((N_CHIPS, r, cdim), F32), pltpu.SemaphoreType.DMA,
                        pltpu.SemaphoreType.DMA, pltpu.SemaphoreType.DMA((3,)), pltpu.SemaphoreType.DMA((3,))],
        compiler_params=pltpu.CompilerParams(has_side_effects=True, vmem_limit_bytes=VMEM_LIMIT_V7X))(g, w, m, v, *deps)


GROUPS = (
    ("ffn1", ("ffn1_w_gate", "ffn1_w_up", "ffn1_w_down")),
    ("mixer", ("w_in", "pool_w", "w_out_a", "w_out_b", "w_o")),
    ("ffn2", ("ffn2_w_gate", "ffn2_w_up", "ffn2_w_down")),
    ("ple", ("ple_w_gate", "ple_w_proj")),
)
GATHERS = (
    ("ffn1_in", ("ffn1_w_gate", "ffn1_w_up")),
    ("ffn1_out", ("ffn1_w_down",)),
) + GROUPS[1:]
GAINS = ("ffn1_pre_g", "ffn1_post_g", "mix_pre_g", "sgu_norm_g", "pool_scale", "mix_post_g",
         "ffn2_pre_g", "ffn2_post_g", "ple_pre_g", "ple_post_g")
SMALL = GAINS + ("sgu_b", "sgu_w")
PACKED = SMALL + ("loss",)
WEIGHTS = ("ffn1_pre_g", "ffn1_w_gate", "ffn1_w_up", "ffn1_w_down", "ffn1_post_g", "mix_pre_g", "w_in", "sgu_norm_g",
           "sgu_w", "sgu_b", "pool_w", "pool_scale", "w_out_a", "w_out_b", "w_o", "mix_post_g", "ffn2_pre_g",
           "ffn2_w_gate", "ffn2_w_up", "ffn2_w_down", "ffn2_post_g", "ple_pre_g", "ple_w_gate", "ple_w_proj", "ple_post_g")
PACK_ROWS = 16


TRANSPOSED = ("ffn1_w_gate", "ffn1_w_up", "ffn2_w_gate", "ffn2_w_up")


def _shard2d(name, a):
    a = a[0]
    return a.T if name in TRANSPOSED else a.reshape(-1, a.shape[-1])


def _unshard2d(name, a2d, shape):
    return (a2d.T if name in TRANSPOSED else a2d).reshape(shape)


def pack_rows(gains, sgu_b, loss_tile, name):
    n = len(gains)
    d = gains[0].shape[1]
    g = sgu_b.shape[0]

    def body(*refs):
        o_ref = refs[-1]
        o_ref[...] = jnp.zeros_like(o_ref)
        for i in range(n):
            o_ref[i:i + 1, :] = refs[i][...]
        o_ref[n:n + g, 0:CHUNK] = refs[n][...]
        o_ref[n + g:n + g + 1, 0:CHUNK] = refs[n + 1][0:1, :]

    return pl.pallas_call(
        body, name=name, in_specs=[VMEM_SPEC] * (n + 2), out_specs=VMEM_SPEC,
        out_shape=jax.ShapeDtypeStruct((PACK_ROWS, d), F32))(*gains, sgu_b, loss_tile)


def _pack_small(parts, tag):
    d = parts[GAINS[0]].shape[-1]
    rows = pack_rows([parts[n] for n in GAINS], parts["sgu_b"].reshape(-1, CHUNK), parts["loss"], "pack_" + tag)
    return jnp.concatenate([rows, parts["sgu_w"].reshape(-1, d)], axis=0)


def _unpack_small(packed, like):
    n, g = len(GAINS), like["sgu_b"].size // CHUNK
    out = {name: packed[i:i + 1] for i, name in enumerate(GAINS)}
    out["sgu_b"] = packed[n:n + g, :CHUNK].reshape(like["sgu_b"].shape)
    out["loss"] = packed[n + g, 0]
    out["sgu_w"] = packed[PACK_ROWS:].reshape(like["sgu_w"].shape)
    return out


def _ffn_fwd(xn, h, w, pre, g_post, g_next, tag, between=None, first=None):
    a4, s4, t4 = ffn_gu(xn, w[pre + "w_gate"], w[pre + "w_up"], tag + "_gu") if first is None else first(xn)
    deps = ()
    if between is not None:
        more, deps = between(a4)
        w.update(more)
    f, h_new, xn_next = mm_norm_res(a4, w[pre + "w_down"], h, g_post, g_next, 0.5, tag + "_down", deps=deps)
    return dict(xn=xn, h=h, a4=a4, s4=s4, t4=t4, f=f), h_new, xn_next


def _ffn_bwd_w(dh, saved, w, pre, g_post, tag, deps):
    df, dg4, du4, d_post = ffn_bwd_a(dh, saved["f"], g_post, w[pre + "w_down"], saved["s4"], saved["t4"], 0.5, tag + "_bwd_a",
                                     deps=deps)
    grads, exchanging, token = _ffn_dw(saved, df, dg4, du4, pre, tag)
    return dg4, du4, grads, d_post, (exchanging, token)


def _ffn_dw(saved, df, dg4, du4, pre, tag):
    jobs = {pre + "w_down": (saved["a4"], df), pre + "w_gate": (dg4, saved["xn"]), pre + "w_up": (du4, saved["xn"])}
    grads, started, deps = {}, {}, ()
    for name, (x, dy) in jobs.items():
        grads[name] = dw_tn(x, dy, N_CHIPS, f"dw_{name}", x_kmajor=True, dy_mode="same", deps=deps)
        started[name], token = sibling_start([grads[name]], True, "exchange_start_" + name)
        deps = (token,)
    return grads, [started[pre + kind] for kind in ("w_gate", "w_up", "w_down")], token


def _ffn_bwd_x(dh, dg4, du4, saved, w, pre, g_pre, tag, deps):
    return dx_norm_bwd([(dg4, w[pre + "w_gate"], False), (du4, w[pre + "w_up"], False)], saved["h"], g_pre, dh,
                       tag + "_bwd_x", deps=deps)


def _start_gathers(weights, chip):
    first_tag, first_names = GATHERS[0]
    made, own = cast_place([_shard2d(n, weights[n]) for n in first_names], chip, "cast_" + first_tag, plain=True)
    sends, recvs, bufs, token = allgather_start(list(made), "allgather_start_" + first_tag)
    gathering = {first_tag: (sends, recvs, bufs, 0)}
    later = [n for _, names in GATHERS[1:] for n in names]
    cast = cast_place([_shard2d(n, weights[n]) for n in later], chip, "cast_later", deps=(token,))
    sends, recvs, bufs, token = allgather_start(list(cast), "allgather_start_later")
    first = 0
    for tag, names in GATHERS[1:]:
        gathering[tag] = (sends, recvs, bufs[first:first + len(names)], first)
        first += len(names)
    return gathering, token, dict(zip(first_names, own))


def _gathered(started, names, after, tag):
    sends, recvs, bufs, first = started
    landed = allgather_wait(sends, recvs, bufs, after, "allgather_wait_" + tag, first)
    return dict(zip(names, d2d_forward(landed, "d2d_forward_" + tag)))


def _forward_early(started, after, tag):
    sends, recvs, bufs, first = started
    landed = allgather_wait(sends, recvs, bufs, after, "allgather_wait_" + tag, first)
    return d2d_forward_start(landed, "d2d_forward_start_" + tag)


def _forwarded(forwarding, names, after, tag):
    return dict(zip(names, d2d_forward_wait(forwarding, after, "d2d_forward_wait_" + tag)))


def _exchange_start(names, big_g, tag):
    started, token = sibling_start([big_g[n] for n in names], True, "exchange_start_" + tag)
    return [started], token


def _scatter_begin(names, exchanging, after, core, tag):
    partial, from_sibling = [], []
    for e, started in enumerate(exchanging):
        mine, theirs = sibling_wait(started, True, after, f"exchange_wait_{tag}_{e}")
        partial += mine
        from_sibling += theirs
    chip_sums = list(add_halves(partial, from_sibling, core, "add_halves_" + tag))
    sends, recvs, sums, lands, token = scatter_start(chip_sums, "scatter_start_" + tag)
    return (names, sends, recvs, sums, lands), token


def _share_begin(scattering, after, chip, tag):
    names, sends, recvs, sums, lands = scattering
    sums, lands = scatter_wait(sends, recvs, sums, lands, after, "scatter_wait_" + tag)
    reduced = list(add_chips(sums, lands, chip, "add_chips_" + tag))
    sharing, token = sibling_start(reduced, False, "share_start_" + tag)
    return (names, sharing), token


def _update(shared, after, weights, moments_m, moments_v, tag, results):
    names, sharing = shared
    reduced, others = sibling_wait(sharing, False, after, "share_wait_" + tag)
    for b, idx in enumerate(_by_shape(reduced)):
        in_bucket = [names[i] for i in idx]
        outs = adamw_halves_sparsecore(
            [_shard2d(n, weights[n]) for n in in_bucket], [reduced[i] for i in idx], [others[i] for i in idx],
            [_shard2d(n, moments_m[n]) for n in in_bucket], [_shard2d(n, moments_v[n]) for n in in_bucket], f"adamw_{tag}_{b}")
        for n, per_weight in zip(in_bucket, outs):
            for store, value in zip(results, per_weight):
                store[n] = _unshard2d(n, value, weights[n].shape)


def _update_last(scattering, after, weights, moments_m, moments_v, core, chip, tag, results):
    names, sends, recvs, sums, lands = scattering
    sums, lands = scatter_wait(sends, recvs, sums, lands, after, "scatter_wait_" + tag)
    reduced = list(add_chips(sums, lands, chip, "add_chips_" + tag))
    (sends, recvs, reduced, zones), _ = sibling_start(reduced, False, "share_start_" + tag)
    state = [[_shard2d(n, held[n]) for n in names] for held in (weights, moments_m, moments_v)]
    first = adamw_one_half(state[0], reduced, state[1], state[2], core, f"adamw_{tag}_own", own=True)
    _, others = sibling_wait((sends, recvs, reduced, zones), False, first[1][-1], "share_wait_" + tag)
    outs = adamw_one_half(state[0], others, state[1], state[2], core, f"adamw_{tag}_other", own=False, into=first)
    for store, values in zip(results, outs):
        for n, value in zip(names, values):
            store[n] = _unshard2d(n, value, weights[n].shape)


def kernel(x, p, ffn1_pre_g, ffn1_w_gate, ffn1_w_up, ffn1_w_down, ffn1_post_g, mix_pre_g, w_in, sgu_norm_g, sgu_w, sgu_b, pool_w, pool_scale, w_out_a, w_out_b, w_o, mix_post_g, ffn2_pre_g, ffn2_w_gate, ffn2_w_up, ffn2_w_down, ffn2_post_g, ple_pre_g, ple_w_gate, ple_w_proj, ple_post_g, loss_target, m_ffn1_pre_g, m_ffn1_w_gate, m_ffn1_w_up, m_ffn1_w_down, m_ffn1_post_g, m_mix_pre_g, m_w_in, m_sgu_norm_g, m_sgu_w, m_sgu_b, m_pool_w, m_pool_scale, m_w_out_a, m_w_out_b, m_w_o, m_mix_post_g, m_ffn2_pre_g, m_ffn2_w_gate, m_ffn2_w_up, m_ffn2_w_down, m_ffn2_post_g, m_ple_pre_g, m_ple_w_gate, m_ple_w_proj, m_ple_post_g, v_ffn1_pre_g, v_ffn1_w_gate, v_ffn1_w_up, v_ffn1_w_down, v_ffn1_post_g, v_mix_pre_g, v_w_in, v_sgu_norm_g, v_sgu_w, v_sgu_b, v_pool_w, v_pool_scale, v_w_out_a, v_w_out_b, v_w_o, v_mix_post_g, v_ffn2_pre_g, v_ffn2_w_gate, v_ffn2_w_up, v_ffn2_w_down, v_ffn2_post_g, v_ple_pre_g, v_ple_w_gate, v_ple_w_proj, v_ple_post_g):
    given = dict(locals())
    weights = {n: given[n] for n in WEIGHTS}
    moments_m = {n: given["m_" + n] for n in WEIGHTS}
    moments_v = {n: given["v_" + n] for n in WEIGHTS}
    core = lax.axis_index("c").astype(jnp.int32).reshape(1)
    chip = (2 * lax.axis_index("x") + lax.axis_index("y")).astype(jnp.int32).reshape(1)

    d = x.shape[-1]
    gathering, last_start, own_first = _start_gathers(weights, chip)
    token = (last_start,)
    gain = {n: weights[n] for n in GAINS}
    sgu_w3 = sgu_w[0]
    sgu_b3 = sgu_b[0][:, :, None]
    groups = dict(GROUPS + GATHERS)

    h0 = x[0]
    tgt = loss_target[0]
    p_bf = p[0, 0].astype(BF16)
    xn1 = rms_cast(h0, gain["ffn1_pre_g"], "ffn1_pre_norm")
    w = {}

    def ffn1_first(xn):
        own = ffn_gu(xn, own_first["ffn1_w_gate"][None], own_first["ffn1_w_up"][None], "ffn1_gu_own", chip=chip)
        w.update(_gathered(gathering["ffn1_in"], groups["ffn1_in"], (own[0], last_start), "ffn1_in"))
        return ffn_gu(xn, w["ffn1_w_gate"], w["ffn1_w_up"], "ffn1_gu", chip=chip, into=own)

    s1, h1, xn2 = _ffn_fwd(xn1, h0, w, "ffn1_", gain["ffn1_post_g"], gain["mix_pre_g"], "ffn1",
                           lambda a4: (_gathered(gathering["ffn1_out"], groups["ffn1_out"], a4, "ffn1_out"), ()), ffn1_first)
    w.update(_gathered(gathering["mixer"], groups["mixer"], h1, "mixer"))
    full = {n: w[n].reshape(-1, d) for n in ("w_out_a", "w_out_b", "w_o")}
    n_groups = pool_w.shape[1]
    rows_per = pool_w.shape[2]
    dgp = pool_w.shape[3]
    pool_full = w["pool_w"].reshape(N_CHIPS, n_groups, rows_per, dgp).transpose(1, 0, 2, 3).reshape(n_groups, N_CHIPS * rows_per, dgp)
    z = mixer_in(xn2, w["w_in"], "mixer_in")
    a = sgu_fwd(z, gain["sgu_norm_g"], sgu_w3, sgu_b3, "sgu_fwd")
    diff, b = pool_fwd(z, pool_full, gain["pool_scale"], "pool_fwd")
    ya, yb, y = mixer_y(a, b, z, full["w_out_a"], full["w_out_b"], "mixer_y")
    forwarding, tok = _forward_early(gathering["ffn2"], y, "ffn2")
    m, h2, xn3 = mm_norm_res(y[None], full["w_o"][None], h1, gain["mix_post_g"], gain["ffn2_pre_g"], 1.0, "mixer_out",
                             deps=(tok,))
    w.update(_forwarded(forwarding, groups["ffn2"], h2, "ffn2"))
    early = {}

    def forward_ple(a4):
        early["ple"], tok_ple = _forward_early(gathering["ple"], a4, "ple")
        return {}, (tok_ple,)

    s2, h3, xn4 = _ffn_fwd(xn3, h2, w, "ffn2_", gain["ffn2_post_g"], gain["ple_pre_g"], "ffn2", forward_ple)
    w.update(_forwarded(early["ple"], groups["ple"], h3, "ple"))
    full["ple_w_gate"] = w["ple_w_gate"].reshape(-1, d)
    proj_full = w["ple_w_proj"].transpose(1, 0, 2).reshape(ple_w_proj.shape[1], -1)

    small_g, big_g = {}, {}
    ds, de, dh3, small_g["ple_post_g"], small_g["ple_pre_g"], loss_part = ple_loss(
        xn4, p_bf, full["ple_w_gate"], proj_full, h3, gain["ple_post_g"], gain["ple_pre_g"], tgt, "ple_loss")
    small_g["loss"] = loss_part
    grad, delta, new_m, new_v = {}, {}, {}, {}
    results = (grad, delta, new_m, new_v)
    update = functools.partial(_update, weights=weights, moments_m=moments_m, moments_v=moments_v, results=results)

    def reduce_behind(tag, previous, started=None):
        exchanging, tok = _exchange_start(groups[tag], big_g, tag) if started is None else started
        if previous is not None:
            shared, tok = _share_begin(previous[1], tok, chip, previous[0])
        scattering, tok_scatter = _scatter_begin(groups[tag], exchanging, tok, core, tag)
        if previous is not None:
            update(shared, tok_scatter, tag=previous[0])
        return (tag, scattering), (tok_scatter,)

    big_g["ple_w_gate"] = dw_tn(xn4, ds, 1, "dw_ple_gate").reshape(N_CHIPS, -1, d)
    big_g["ple_w_proj"] = dw_tn(p_bf, de, N_CHIPS, "dw_ple_proj", dy_mode="cols")
    reducing, deps = reduce_behind("ple", None)

    dg4, du4, g2, small_g["ffn2_post_g"], exchanging = _ffn_bwd_w(dh3, s2, w, "ffn2_", gain["ffn2_post_g"], "ffn2", deps)
    big_g.update(g2)
    dh2, small_g["ffn2_pre_g"] = _ffn_bwd_x(dh3, dg4, du4, s2, w, "ffn2_", gain["ffn2_pre_g"], "ffn2", ())
    reducing, deps = reduce_behind("ffn2", reducing, exchanging)

    dm, dya, dyb, dz, da, db, small_g["mix_post_g"] = mixer_bwd_y(
        dh2, m, gain["mix_post_g"], full["w_o"], ya, yb, z, full["w_out_a"], full["w_out_b"], "mixer_bwd_y", deps=deps)
    big_g["w_o"] = dw_tn(y, dm, 1, "dw_o").reshape(N_CHIPS, -1, d)
    big_g["w_out_a"] = dw_tn(a, dya, 1, "dw_out_a").reshape(N_CHIPS, -1, d)
    big_g["w_out_b"] = dw_tn(b, dyb, 1, "dw_out_b").reshape(N_CHIPS, -1, d)
    dz, d_sgu_w, d_sgu_b, small_g["sgu_norm_g"] = sgu_bwd(z, da, dz, gain["sgu_norm_g"], sgu_w3, sgu_b3, "sgu_bwd")
    dz, d_pool_w, small_g["pool_scale"] = pool_bwd(db, diff, dz, pool_full, gain["pool_scale"], "pool_bwd")
    big_g["pool_w"] = d_pool_w.astype(BF16).reshape(n_groups, N_CHIPS, rows_per, dgp).transpose(1, 0, 2, 3).reshape(
        N_CHIPS, n_groups * rows_per, dgp)
    big_g["w_in"] = dw_in_tiles(xn2, dz, N_CHIPS, "dw_in")
    dh1, small_g["mix_pre_g"] = dx_norm_bwd([(dz, w["w_in"], True)], h1, gain["mix_pre_g"], dh2, "mixer_bwd_x")
    reducing, deps = reduce_behind("mixer", reducing)

    dg4, du4, g1, small_g["ffn1_post_g"], exchanging = _ffn_bwd_w(dh1, s1, w, "ffn1_", gain["ffn1_post_g"], "ffn1", deps)
    big_g.update(g1)
    reducing, deps = reduce_behind("ffn1", reducing, exchanging)
    dh0, small_g["ffn1_pre_g"] = _ffn_bwd_x(dh1, dg4, du4, s1, w, "ffn1_", gain["ffn1_pre_g"], "ffn1", deps)
    small_g["sgu_w"] = d_sgu_w
    small_g["sgu_b"] = d_sgu_b[:, :, 0]

    no_state = {"loss": jnp.zeros((8, 128), F32)}
    packed = small_allreduce_adamw(
        _pack_small(small_g, "grads"), _pack_small({n: weights[n] for n in SMALL} | no_state, "weights"),
        _pack_small({n: moments_m[n] for n in SMALL} | no_state, "m"), _pack_small({n: moments_v[n] for n in SMALL} | no_state, "v"))
    like = {n: weights[n] for n in SMALL}
    for store, block in zip((grad, delta, new_m, new_v), packed):
        store.update(_unpack_small(block, like))
    _update_last(reducing[1], packed[0], weights, moments_m, moments_v, core, chip, "ffn1", results)

    return (grad["loss"], dh0[None], *[grad[n] for n in WEIGHTS], *[delta[n] for n in WEIGHTS],
            *[new_m[n] for n in WEIGHTS], *[new_v[n] for n in WEIGHTS])
```

```python
import functools

import jax
import jax.numpy as jnp
from jax import lax
from jax.experimental import pallas as pl
from jax.experimental.pallas import tpu as pltpu
from jax.experimental.pallas import tpu_sc as plsc

F32 = jnp.float32
BF16 = jnp.bfloat16
EPS = 1e-6
CHUNK = 128
POOL_WINDOWS = (2, 4, 8, 16)
HALO = 16
N_CHIPS = 4
ADAM_LR, ADAM_B1, ADAM_B2, ADAM_EPS, ADAM_WD, ADAM_STEP = 0.001, 0.9, 0.999, 1e-08, 0.01, 10
VMEM_LIMIT_V7X = 58 * 1024 * 1024
MESH_IDS = pl.DeviceIdType.MESH
HBM_SPEC = pl.BlockSpec(memory_space=pltpu.HBM)
VMEM_SPEC = pl.BlockSpec(memory_space=pltpu.VMEM)
SEM_SPEC = pl.BlockSpec(memory_space=pltpu.SEMAPHORE)
ANY_SPEC = pl.BlockSpec(memory_space=pl.ANY)
DATAFLOW = pltpu.SideEffectType.DATAFLOW_SIDE_EFFECTING
OTHER_CHIPS = ((1, 0), (0, 1), (1, 1))
DZ_SLOT = (2, 3, 4, 0, 1)
SC_TILES, SC_LANES, SC_ROWS = 32, 16, 8
ROW_STEPS = 4
SIBLING_BARRIER = 1
DW_TOKENS = 4096
DW_IN_TILE = 256

NT = (((1,), (1,)), ((), ()))
TN = (((0,), (0,)), ((), ()))


def _params(*sem, **more):
    return pltpu.CompilerParams(dimension_semantics=sem or None, vmem_limit_bytes=VMEM_LIMIT_V7X, **more)


def _tile(t, want):
    return max(c for c in range(8, min(t, want) + 1, 8) if t % c == 0)


def _const(shape):
    return pl.BlockSpec(shape, lambda *_: (0,) * len(shape))


def _rows(tm, d, col=0):
    return pl.BlockSpec((tm, d), lambda i: (i, col))


def _kmajor(nk, tm, kb):
    return pl.BlockSpec((nk, tm, kb), lambda i: (0, i, 0))


def _dot(a, b):
    return jnp.dot(a, b, preferred_element_type=F32)


def _dot_nt(a, b):
    return lax.dot_general(a, b, NT, preferred_element_type=F32)


def _dot_tn(a, b):
    return lax.dot_general(a, b, TN, preferred_element_type=F32)


def _gelu(x):
    return 0.5 * x * (1.0 + jnp.tanh(0.7978845608028654 * (x + 0.044715 * x * x * x)))


def _gelu_and_grad(x):
    k, kc = 0.7978845608028654, 0.7978845608028654 * 0.044715
    x2 = x * x
    th = jnp.tanh(x * (k + kc * x2))
    cdf = 0.5 + 0.5 * th
    return x * cdf, cdf + x * (0.5 - 0.5 * th * th) * (k + 3.0 * kc * x2)


def _sigmoid(x):
    return 1.0 / (1.0 + jnp.exp(-x))


def _rstd(h):
    return lax.rsqrt(jnp.mean(h * h, axis=-1, keepdims=True) + EPS)


def _rms_bwd(h, g, dy):
    r = _rstd(h)
    t = dy * g
    dh = r * t - h * (r * r * r) * jnp.mean(h * t, axis=-1, keepdims=True)
    return dh, jnp.sum(dy * h * r, axis=0, keepdims=True)


def _ordered_after(body, n_in, deps):
    if not deps:
        return body
    return lambda *refs: body(*refs[:n_in], *refs[n_in + len(deps):])


def _accumulate(ref, value):
    @pl.when(pl.program_id(0) == 0)
    def _():
        ref[...] = jnp.zeros_like(ref)

    ref[...] += value


def rms_cast(h, g, name):
    t, d = h.shape
    tm = _tile(t, 512)

    def body(h_ref, g_ref, o_ref):
        hv = h_ref[...]
        o_ref[...] = (hv * _rstd(hv) * g_ref[...]).astype(BF16)

    return pl.pallas_call(
        body, name=name, grid=(t // tm,), in_specs=[_rows(tm, d), _const((1, d))], out_specs=_rows(tm, d),
        out_shape=jax.ShapeDtypeStruct((t, d), BF16), compiler_params=_params("parallel"))(h, g)


def ffn_gu(xn, wgt, wut, name, chip=None, into=None):
    t, d = xn.shape
    nkw, fk, _ = wgt.shape
    tm = _tile(t, 512)
    count = N_CHIPS if chip is None else (1 if into is None else N_CHIPS - 1)
    first = 0 if into is None else 1

    def body(*refs):
        x_ref, wg_ref, wu_ref = refs[-6 - (0 if into is None else 3):][:3]
        a_ref, s_ref, t_ref = refs[-3:]
        xv = x_ref[...]
        g = _dot_nt(xv, wg_ref[0])
        sg = _sigmoid(g)
        s = g * sg
        s_ref[0] = s.astype(BF16)
        ds = sg * (1.0 + g * (1.0 - sg))
        u = _dot_nt(xv, wu_ref[0])
        a_ref[0] = (s * u).astype(BF16)
        t_ref[0] = (u * ds).astype(BF16)

    def slot(k, *chip_ref):
        return (chip_ref[0][0] + first + k) % N_CHIPS if chip_ref else k

    w_spec = pl.BlockSpec((1, fk, d), lambda k, i, *c: (slot(k, *c) if nkw > 1 else 0, 0, 0))
    o_spec = pl.BlockSpec((1, tm, fk), lambda k, i, *c: (slot(k, *c), i, 0))
    shape = jax.ShapeDtypeStruct((N_CHIPS, t, fk), BF16)
    prior = () if into is None else tuple(into)
    call = pl.pallas_call(
        body, name=name,
        grid_spec=pltpu.PrefetchScalarGridSpec(
            num_scalar_prefetch=0 if chip is None else 1, grid=(count, t // tm),
            in_specs=[pl.BlockSpec((tm, d), lambda k, i, *c: (i, 0)), w_spec, w_spec] + [ANY_SPEC] * len(prior),
            out_specs=[o_spec] * 3),
        out_shape=[shape] * 3, input_output_aliases={4 + j: j for j in range(len(prior))},
        compiler_params=_params("parallel", "parallel"))
    return call(*(() if chip is None else (chip,)), xn, wgt, wut, *prior)


def mm_norm_res(a3, w3, h_old, g_post, g_next, scale, name, deps=()):
    nk, t, kb = a3.shape
    d = w3.shape[2]
    tm = _tile(t, 256)

    def body(a_ref, w_ref, h_ref, gp_ref, gn_ref, f_ref, hn_ref, xn_ref):
        f = _dot(a_ref[0], w_ref[0])
        for k in range(1, nk):
            f += _dot(a_ref[k], w_ref[k])
        f_ref[...] = f.astype(BF16)
        hn = h_ref[...] + scale * (f * _rstd(f) * gp_ref[...])
        hn_ref[...] = hn
        xn_ref[...] = (hn * _rstd(hn) * gn_ref[...]).astype(BF16)

    return pl.pallas_call(
        _ordered_after(body, 5, deps), name=name, grid=(t // tm,),
        in_specs=[_kmajor(nk, tm, kb), _const((nk, kb, d)), _rows(tm, d), _const((1, d)), _const((1, d))] + [ANY_SPEC] * len(deps),
        out_specs=[_rows(tm, d)] * 3,
        out_shape=[jax.ShapeDtypeStruct((t, d), BF16), jax.ShapeDtypeStruct((t, d), F32), jax.ShapeDtypeStruct((t, d), BF16)],
        compiler_params=_params("parallel"))(a3, w3, h_old, g_post, g_next, *deps)


def mixer_in(xn, win4, name):
    t, d = xn.shape
    nk, _, nb = win4.shape
    tm = _tile(t, 512)

    def body(x_ref, w_ref, z_ref):
        z_ref[...] = _dot(x_ref[...], w_ref[0]).astype(BF16)

    return pl.pallas_call(
        body, name=name, grid=(nk, t // tm),
        in_specs=[pl.BlockSpec((tm, d), lambda k, i: (i, 0)), pl.BlockSpec((1, d, nb), lambda k, i: (k, 0, 0))],
        out_specs=pl.BlockSpec((tm, nb), lambda k, i: (i, k)), out_shape=jax.ShapeDtypeStruct((t, nk * nb), BF16),
        compiler_params=_params("parallel", "parallel"))(xn, win4)


def _causal_mask():
    row = lax.broadcasted_iota(jnp.int32, (CHUNK, CHUNK), 0)
    col = lax.broadcasted_iota(jnp.int32, (CHUNK, CHUNK), 1)
    return row >= col


def _layernorm_parts(v):
    mu = jnp.mean(v, axis=-1, keepdims=True)
    vc = v - mu
    r = lax.rsqrt(jnp.mean(vc * vc, axis=-1, keepdims=True) + EPS)
    return vc * r, r


def sgu_fwd(z, norm_g, sgu_w, sgu_b3, name):
    t = z.shape[0]
    d = norm_g.shape[1]
    ng = sgu_w.shape[0]
    dg = d // ng
    tm = _tile(t, 256)

    def body(zu_ref, zv_ref, ng_ref, w_ref, b_ref, a_ref):
        vhat, _ = _layernorm_parts(_gelu(zv_ref[...].astype(F32)))
        vn = (vhat * ng_ref[...]).astype(BF16)
        u = _gelu(zu_ref[...].astype(F32))
        mask = _causal_mask()
        for g in range(ng):
            wg = jnp.where(mask, w_ref[g], 0.0).astype(BF16)
            for ci in range(tm // CHUNK):
                rs, cs = slice(ci * CHUNK, (ci + 1) * CHUNK), slice(g * dg, (g + 1) * dg)
                sv = _dot(wg, vn[rs, cs]) + b_ref[g]
                a_ref[rs, cs] = (u[rs, cs] * sv).astype(BF16)

    return pl.pallas_call(
        body, name=name, grid=(t // tm,),
        in_specs=[_rows(tm, d, 0), _rows(tm, d, 1), _const((1, d)), _const((ng, CHUNK, CHUNK)), _const((ng, CHUNK, 1))],
        out_specs=_rows(tm, d), out_shape=jax.ShapeDtypeStruct((t, d), BF16),
        compiler_params=_params("parallel"))(z, z, norm_g, sgu_w, sgu_b3)


def pool_fwd(z, pool_w, pool_scale, name):
    t = z.shape[0]
    d = pool_scale.shape[1]
    ng = pool_w.shape[0]
    dg = d // ng
    tm = _tile(t, 256)
    per = tm // HALO

    def body(c_ref, prev_ref, w_ref, s_ref, diff_ref, b_ref):
        i = pl.program_id(0)
        cur = c_ref[...].astype(F32)
        prev = jnp.where(i > 0, prev_ref[...].astype(F32), 0.0)
        ext = jnp.concatenate([prev, cur], axis=0)
        tok = i * tm + lax.broadcasted_iota(jnp.int32, (tm, 1), 0)
        for g, win in enumerate(POOL_WINDOWS):
            cs = slice(g * dg, (g + 1) * dg)
            s = ext[:, cs]
            sh = 1
            while sh < win:
                s = s + pltpu.roll(s, sh, 0)
                sh *= 2
            per_count = 1.0 / jnp.minimum(tok + 1, win).astype(F32)
            diff = (s[HALO:] * per_count - cur[:, cs]).astype(BF16)
            diff_ref[:, cs] = diff
            b_ref[:, cs] = (_dot(diff, w_ref[g]) * s_ref[:, cs]).astype(BF16)

    return pl.pallas_call(
        body, name=name, grid=(t // tm,),
        in_specs=[_rows(tm, d, 2), pl.BlockSpec((HALO, d), lambda i: (jnp.maximum(i * per - 1, 0), 2)),
                  _const((ng, dg, dg)), _const((1, d))],
        out_specs=[_rows(tm, d)] * 2, out_shape=[jax.ShapeDtypeStruct((t, d), BF16)] * 2,
        compiler_params=_params("parallel"))(z, z, pool_w, pool_scale)


def mixer_y(a, b, z, woa, wob, name):
    t, d = a.shape
    tm = _tile(t, 256)

    def body(a_ref, b_ref, ga_ref, gb_ref, wa_ref, wb_ref, ya_ref, yb_ref, y_ref):
        ya = _dot(a_ref[...], wa_ref[...])
        yb = _dot(b_ref[...], wb_ref[...])
        ya_ref[...] = ya.astype(BF16)
        yb_ref[...] = yb.astype(BF16)
        y_ref[...] = (_sigmoid(ga_ref[...].astype(F32)) * ya + _sigmoid(gb_ref[...].astype(F32)) * yb).astype(BF16)

    return pl.pallas_call(
        body, name=name, grid=(t // tm,),
        in_specs=[_rows(tm, d), _rows(tm, d), _rows(tm, d, 3), _rows(tm, d, 4), _const((d, d)), _const((d, d))],
        out_specs=[_rows(tm, d)] * 3, out_shape=[jax.ShapeDtypeStruct((t, d), BF16)] * 3,
        compiler_params=_params("parallel"))(a, b, z, z, woa, wob)


def ple_loss(xn, p, wpg, wpp, h, g_post, g_pre, target, name):
    t, d = xn.shape
    dp = p.shape[1]
    tm = _tile(t, 256)

    def body(x_ref, p_ref, wg_ref, wp_ref, h_ref, gp_ref, gn_ref, tg_ref, ds_ref, de_ref, dhp_ref, dgp_ref, dgn_ref, loss_ref):
        gate = _sigmoid(_dot(x_ref[...], wg_ref[...]))
        e = _dot(p_ref[...], wp_ref[...])
        q = gate * e
        hv = h_ref[...]
        err = hv + q * _rstd(q) * gp_ref[...] - tg_ref[...]
        _accumulate(loss_ref, jnp.full(loss_ref.shape, (0.5 / d) * jnp.sum(err * err), F32))
        dhv = err * (1.0 / d)
        dq, dgp = _rms_bwd(q, gp_ref[...], dhv)
        ds = (dq * e * gate * (1.0 - gate)).astype(BF16)
        ds_ref[...] = ds
        de_ref[...] = (dq * gate).astype(BF16)
        dx, dgn = _rms_bwd(hv, gn_ref[...], _dot_nt(ds, wg_ref[...]))
        dhp_ref[...] = dhv + dx
        _accumulate(dgp_ref, dgp)
        _accumulate(dgn_ref, dgn)

    return pl.pallas_call(
        body, name=name, grid=(t // tm,),
        in_specs=[_rows(tm, d), _rows(tm, dp), _const((d, d)), _const((dp, d)), _rows(tm, d), _const((1, d)), _const((1, d)),
                  _rows(tm, d)],
        out_specs=[_rows(tm, d)] * 3 + [_const((1, d))] * 2 + [_const((8, 128))],
        out_shape=[jax.ShapeDtypeStruct((t, d), BF16), jax.ShapeDtypeStruct((t, d), BF16), jax.ShapeDtypeStruct((t, d), F32),
                   jax.ShapeDtypeStruct((1, d), F32), jax.ShapeDtypeStruct((1, d), F32), jax.ShapeDtypeStruct((8, 128), F32)],
        compiler_params=_params("arbitrary"))(xn, p, wpg, wpp, h, g_post, g_pre, target)


def ffn_bwd_a(dh, f, g_post, wd4, s4, t4, scale, name, deps=()):
    t, d = dh.shape
    nk, fk, _ = wd4.shape
    tm = _tile(t, 256)

    def body(dh_ref, f_ref, gp_ref, w_ref, s_ref, t_ref, df_ref, dg_ref, du_ref, dgp_ref):
        df, dgp = _rms_bwd(f_ref[...].astype(F32), gp_ref[...], dh_ref[...])
        df = (scale * df).astype(BF16)
        df_ref[...] = df
        _accumulate(dgp_ref, scale * dgp)
        for k in range(nk):
            da = _dot_nt(df, w_ref[k])
            du_ref[k] = (da * s_ref[k].astype(F32)).astype(BF16)
            dg_ref[k] = (da * t_ref[k].astype(F32)).astype(BF16)

    return pl.pallas_call(
        _ordered_after(body, 6, deps), name=name, grid=(t // tm,),
        in_specs=[_rows(tm, d), _rows(tm, d), _const((1, d)), _const((nk, fk, d)), _kmajor(nk, tm, fk), _kmajor(nk, tm, fk)]
        + [ANY_SPEC] * len(deps),
        out_specs=[_rows(tm, d), _kmajor(nk, tm, fk), _kmajor(nk, tm, fk), _const((1, d))],
        out_shape=[jax.ShapeDtypeStruct((t, d), BF16), jax.ShapeDtypeStruct((nk, t, fk), BF16),
                   jax.ShapeDtypeStruct((nk, t, fk), BF16), jax.ShapeDtypeStruct((1, d), F32)],
        compiler_params=_params("arbitrary"))(dh, f, g_post, wd4, s4, t4, *deps)


def dx_norm_bwd(pairs, h, g_pre, dh_in, name, deps=()):
    t, d = h.shape
    tm = _tile(t, 256)
    n = len(pairs)

    def body(*refs):
        dys, ws = refs[:n], refs[n:2 * n]
        h_ref, g_ref, dhi_ref, dho_ref, dg_ref = refs[2 * n:]
        acc = None
        for (_, w4, sections), dy_ref, w_ref in zip(pairs, dys, ws):
            if sections:
                wide = w4.shape[2]
                edges = sorted(set(range(0, 5 * d + 1, d)) | set(range(0, 5 * d + 1, wide)))
                parts = [_dot_nt(dy_ref[DZ_SLOT[lo // d], :, lo % d:lo % d + hi - lo], w_ref[lo // wide, :, lo % wide:lo % wide + hi - lo])
                         for lo, hi in zip(edges[:-1], edges[1:])]
            else:
                parts = [_dot(dy_ref[k], w_ref[k]) for k in range(w4.shape[0])]
            for part in parts:
                acc = part if acc is None else acc + part
        dx, dg = _rms_bwd(h_ref[...], g_ref[...], acc)
        dho_ref[...] = dhi_ref[...] + dx
        _accumulate(dg_ref, dg)

    dy_specs = [_kmajor(dy.shape[0], tm, dy.shape[2]) for dy, _, _ in pairs]
    return pl.pallas_call(
        _ordered_after(body, 2 * n + 3, deps), name=name, grid=(t // tm,),
        in_specs=dy_specs + [_const(w4.shape) for _, w4, _ in pairs] + [_rows(tm, d), _const((1, d)), _rows(tm, d)]
        + [ANY_SPEC] * len(deps),
        out_specs=[_rows(tm, d), _const((1, d))],
        out_shape=[jax.ShapeDtypeStruct((t, d), F32), jax.ShapeDtypeStruct((1, d), F32)],
        compiler_params=_params("arbitrary"))(*[dy for dy, _, _ in pairs], *[w4 for _, w4, _ in pairs], h, g_pre, dh_in, *deps)


def mixer_bwd_y(dh, m, g_post, w_o, ya, yb, z, woa, wob, name, deps=()):
    t, d = dh.shape
    tm = _tile(t, 256)

    def body(dh_ref, m_ref, gp_ref, wo_ref, ya_ref, yb_ref, ga_ref, gb_ref, wa_ref, wb_ref,
             dm_ref, dya_ref, dyb_ref, dz_ref, da_ref, db_ref, dgp_ref):
        dm, dgp = _rms_bwd(m_ref[...].astype(F32), gp_ref[...], dh_ref[...])
        dm = dm.astype(BF16)
        dm_ref[...] = dm
        _accumulate(dgp_ref, dgp)
        dy = _dot_nt(dm, wo_ref[...])
        sa = _sigmoid(ga_ref[...].astype(F32))
        sb = _sigmoid(gb_ref[...].astype(F32))
        dya = (dy * sa).astype(BF16)
        dyb = (dy * sb).astype(BF16)
        dya_ref[...] = dya
        dyb_ref[...] = dyb
        dz_ref[0] = (dy * ya_ref[...].astype(F32) * sa * (1.0 - sa)).astype(BF16)
        dz_ref[1] = (dy * yb_ref[...].astype(F32) * sb * (1.0 - sb)).astype(BF16)
        da_ref[...] = _dot_nt(dya, wa_ref[...]).astype(BF16)
        db_ref[...] = _dot_nt(dyb, wb_ref[...]).astype(BF16)

    return pl.pallas_call(
        _ordered_after(body, 10, deps), name=name, grid=(t // tm,),
        in_specs=[_rows(tm, d), _rows(tm, d), _const((1, d)), _const((d, d)), _rows(tm, d), _rows(tm, d),
                  _rows(tm, d, 3), _rows(tm, d, 4), _const((d, d)), _const((d, d))] + [ANY_SPEC] * len(deps),
        out_specs=[_rows(tm, d)] * 3 + [pl.BlockSpec((2, tm, d), lambda i: (0, i, 0))] + [_rows(tm, d)] * 2 + [_const((1, d))],
        out_shape=[jax.ShapeDtypeStruct((t, d), BF16)] * 3 + [jax.ShapeDtypeStruct((5, t, d), BF16)]
        + [jax.ShapeDtypeStruct((t, d), BF16)] * 2 + [jax.ShapeDtypeStruct((1, d), F32)],
        compiler_params=_params("arbitrary"))(dh, m, g_post, w_o, ya, yb, z, z, woa, wob, *deps)


def sgu_bwd(z, da, dz, norm_g, sgu_w, sgu_b3, name):
    t, d = da.shape
    ng = sgu_w.shape[0]
    dg = d // ng
    tm = _tile(t, 256)
    steps = t // tm

    def body(zu_ref, zv_ref, da_ref, ng_ref, w_ref, b_ref, _, dz_ref, dw_ref, db_ref, dng_ref, dvn_ref, dsv_ref):
        i = pl.program_id(0)
        zv = zv_ref[...].astype(F32)
        zu = zu_ref[...].astype(F32)
        v, gv = _gelu_and_grad(zv)
        vhat, r = _layernorm_parts(v)
        gain = ng_ref[...]
        vn = (vhat * gain).astype(BF16)
        u, gu = _gelu_and_grad(zu)
        dav = da_ref[...].astype(F32)
        mask = _causal_mask()

        @pl.when(i == 0)
        def _():
            dw_ref[...] = jnp.zeros_like(dw_ref)
            dsv_ref[...] = jnp.zeros_like(dsv_ref)

        for g in range(ng):
            wg = jnp.where(mask, w_ref[g], 0.0).astype(BF16)
            dw = jnp.zeros((CHUNK, CHUNK), F32)
            dsv_sum = jnp.zeros((CHUNK, dg), F32)
            for ci in range(tm // CHUNK):
                rs, cs = slice(ci * CHUNK, (ci + 1) * CHUNK), slice(g * dg, (g + 1) * dg)
                vn_blk = vn[rs, cs]
                sv = _dot(wg, vn_blk) + b_ref[g]
                dz_ref[0, rs, cs] = (dav[rs, cs] * sv * gu[rs, cs]).astype(BF16)
                dsv = dav[rs, cs] * u[rs, cs]
                dsv_sum += dsv
                dsv = dsv.astype(BF16)
                dw += _dot_nt(dsv, vn_blk)
                dvn_ref[rs, cs] = _dot_tn(wg, dsv)
            dw_ref[g] += dw
            dsv_ref[:, cs] += dsv_sum

        dvn = dvn_ref[...]
        _accumulate(dng_ref, jnp.sum(dvn * vhat, axis=0, keepdims=True))
        dvh = dvn * gain
        dv = r * (dvh - jnp.mean(dvh, axis=-1, keepdims=True) - vhat * jnp.mean(dvh * vhat, axis=-1, keepdims=True))
        dz_ref[1] = (dv * gv).astype(BF16)

        @pl.when(i == steps - 1)
        def _():
            for g in range(ng):
                dw_ref[g] = jnp.where(mask, dw_ref[g], 0.0)
                row_sum = jnp.sum(dsv_ref[:, g * dg:(g + 1) * dg], axis=1, keepdims=True)
                db_ref[g] = jnp.broadcast_to(row_sum, (CHUNK, CHUNK))

    return pl.pallas_call(
        body, name=name, grid=(steps,),
        in_specs=[_rows(tm, d, 0), _rows(tm, d, 1), _rows(tm, d), _const((1, d)), _const((ng, CHUNK, CHUNK)), _const((ng, CHUNK, 1)),
                  ANY_SPEC],
        out_specs=[pl.BlockSpec((2, tm, d), lambda i: (DZ_SLOT[0] // 2, i, 0)), _const((ng, CHUNK, CHUNK)),
                   _const((ng, CHUNK, CHUNK)), _const((1, d))],
        out_shape=[jax.ShapeDtypeStruct(dz.shape, BF16), jax.ShapeDtypeStruct((ng, CHUNK, CHUNK), F32),
                   jax.ShapeDtypeStruct((ng, CHUNK, CHUNK), F32), jax.ShapeDtypeStruct((1, d), F32)],
        scratch_shapes=[pltpu.VMEM((tm, d), F32), pltpu.VMEM((CHUNK, d), F32)], input_output_aliases={6: 0},
        compiler_params=_params("arbitrary"))(z, z, da, norm_g, sgu_w, sgu_b3, dz)


def pool_bwd(db, diff, dz, pool_w, pool_scale, name):
    t, d = db.shape
    ng = pool_w.shape[0]
    dg = d // ng
    tm = _tile(t, 256)
    per = tm // HALO
    steps = t // tm

    def body(db_ref, next_ref, diff_ref, w_ref, s_ref, _, dc_ref, dw_ref, ds_ref):
        i = pl.program_id(0)
        dbc = db_ref[...].astype(F32)
        nxt = jnp.where(i < steps - 1, next_ref[...].astype(F32), 0.0)
        ext = jnp.concatenate([dbc, nxt], axis=0)
        rows = tm + HALO
        tok = i * tm + lax.broadcasted_iota(jnp.int32, (rows, 1), 0)

        @pl.when(i == 0)
        def _():
            dw_ref[...] = jnp.zeros_like(dw_ref)
            ds_ref[...] = jnp.zeros_like(ds_ref)

        for g, win in enumerate(POOL_WINDOWS):
            cs = slice(g * dg, (g + 1) * dg)
            dp = (ext[:, cs] * s_ref[:, cs]).astype(BF16)
            dd = _dot_nt(dp, w_ref[g])
            s = dd * (1.0 / jnp.minimum(tok + 1, win).astype(F32))
            sh = 1
            while sh < win:
                s = s + pltpu.roll(s, rows - sh, 0)
                sh *= 2
            dc_ref[0, :, cs] = (s[:tm] - dd[:tm]).astype(BF16)
            dfg = diff_ref[:, cs]
            ds_ref[:, cs] += jnp.sum(dbc[:, cs] * _dot(dfg, w_ref[g]), axis=0, keepdims=True)
            dw_ref[g] += _dot_tn(dfg, dp[:tm])

    return pl.pallas_call(
        body, name=name, grid=(steps,),
        in_specs=[_rows(tm, d), pl.BlockSpec((HALO, d), lambda i: (jnp.minimum((i + 1) * per, t // HALO - 1), 0)),
                  _rows(tm, d), _const((ng, dg, dg)), _const((1, d)), ANY_SPEC],
        out_specs=[pl.BlockSpec((1, tm, d), lambda i: (DZ_SLOT[2], i, 0)), _const((ng, dg, dg)), _const((1, d))],
        out_shape=[jax.ShapeDtypeStruct(dz.shape, BF16), jax.ShapeDtypeStruct((ng, dg, dg), F32), jax.ShapeDtypeStruct((1, d), F32)],
        input_output_aliases={5: 0},
        compiler_params=_params("arbitrary"))(db, db, diff, pool_w, pool_scale, dz)


def dw_tn(x, dy, nk, name, x_kmajor=False, dy_mode="same", deps=()):
    t = x.shape[-2]
    kx = x.shape[-1]
    n = dy.shape[-1] // nk if dy_mode == "cols" else dy.shape[-1]
    tt = _tile(t, DW_TOKENS)
    steps = t // tt

    def body(x_ref, dy_ref, o_ref, acc_ref):
        s = pl.program_id(1)
        part = _dot_tn(x_ref[0] if x_kmajor else x_ref[...], dy_ref[0] if dy_mode == "kmajor" else dy_ref[...])
        if steps == 1:
            o_ref[0] = part.astype(BF16)
            return

        @pl.when(s == 0)
        def _():
            acc_ref[...] = jnp.zeros_like(acc_ref)

        acc_ref[...] += part

        @pl.when(s == steps - 1)
        def _():
            o_ref[0] = acc_ref[...].astype(BF16)

    x_spec = pl.BlockSpec((1, tt, kx), lambda k, s: (k, s, 0)) if x_kmajor else pl.BlockSpec((tt, kx), lambda k, s: (s, 0))
    dy_spec = {"kmajor": pl.BlockSpec((1, tt, n), lambda k, s: (k, s, 0)), "cols": pl.BlockSpec((tt, n), lambda k, s: (s, k)),
               "same": pl.BlockSpec((tt, n), lambda k, s: (s, 0))}[dy_mode]
    return pl.pallas_call(
        _ordered_after(body, 2, deps), name=name, grid=(nk, steps), in_specs=[x_spec, dy_spec] + [ANY_SPEC] * len(deps),
        out_specs=pl.BlockSpec((1, kx, n), lambda k, s: (k, 0, 0)), out_shape=jax.ShapeDtypeStruct((nk, kx, n), BF16),
        scratch_shapes=[pltpu.VMEM((kx, n) if steps > 1 else (8, 128), F32)],
        compiler_params=_params("parallel", "arbitrary"))(x, dy, *deps)


def dw_in_tiles(xn, dz, nk, name):
    t, d = xn.shape
    sections = len(DZ_SLOT)
    per_section = d // DW_IN_TILE
    per_shard = sections * per_section // nk

    def body(x_ref, dy_ref, o_ref):
        o_ref[0] = _dot_tn(x_ref[...], dy_ref[0]).astype(BF16)

    def slot(j):
        return (j // per_section + DZ_SLOT[0]) % sections

    return pl.pallas_call(
        body, name=name, grid=(sections * per_section,),
        in_specs=[pl.BlockSpec((t, d), lambda j: (0, 0)), pl.BlockSpec((1, t, DW_IN_TILE), lambda j: (slot(j), 0, j % per_section))],
        out_specs=pl.BlockSpec((1, d, DW_IN_TILE), lambda j: (j // per_shard, 0, j % per_shard)),
        out_shape=jax.ShapeDtypeStruct((nk, d, sections * d // nk), BF16), compiler_params=_params("parallel"))(xn, dz)


def _place():
    x, y, c = lax.axis_index("x"), lax.axis_index("y"), lax.axis_index("c")
    chips = [((1 - x) if fx else x, (1 - y) if fy else y) for fx, fy in OTHER_CHIPS]
    return x, y, c, chips


def _sibling_handshake(x, y, c):
    barrier = pltpu.get_barrier_semaphore()
    pl.semaphore_signal(barrier, inc=1, device_id=(x, y, 1 - c), device_id_type=MESH_IDS)
    pl.semaphore_wait(barrier, 1)


def _half(rows, which):
    return pl.ds(pl.multiple_of(which * (rows // 2), 16), rows // 2)


def _hbm(a):
    return pltpu.with_memory_space_constraint(a, pltpu.HBM)


def _by_shape(arrays):
    buckets = {}
    for i, a in enumerate(arrays):
        buckets.setdefault(a.shape, []).append(i)
    return list(buckets.values())


def cast_place(shards, chip, name, deps=(), plain=False):
    n = len(shards)

    def body(chip_ref, *refs):
        outs = refs[n + len(deps):]
        for a, w_ref in enumerate(refs[:n]):
            cast = w_ref[...].astype(BF16)
            outs[a][0] = cast
            if plain:
                outs[n + a][...] = cast

    def rows(s):
        return (s.shape[0] // ROW_STEPS, s.shape[1])

    out = pl.pallas_call(
        body, name=name,
        grid_spec=pltpu.PrefetchScalarGridSpec(
            num_scalar_prefetch=1, grid=(ROW_STEPS,),
            in_specs=[pl.BlockSpec(rows(s), lambda i, chip_ref: (i, 0)) for s in shards] + [ANY_SPEC] * len(deps),
            out_specs=[pl.BlockSpec((1,) + rows(s), lambda i, chip_ref: (chip_ref[0], i, 0)) for s in shards]
            + [pl.BlockSpec(rows(s), lambda i, chip_ref: (i, 0)) for s in shards] * plain),
        out_shape=[jax.ShapeDtypeStruct((N_CHIPS,) + s.shape, BF16) for s in shards]
        + [jax.ShapeDtypeStruct(s.shape, BF16) for s in shards] * plain,
        compiler_params=_params("parallel"))(chip, *shards, *deps)
    return (out[:n], out[n:]) if plain else out


def _gather_copy(buf, sends, recvs, i, j, me, chip_xy, c):
    cx, cy = chip_xy
    mine = _half(buf.shape[1], c)
    return pltpu.make_async_remote_copy(
        src_ref=buf.at[me, mine], dst_ref=buf.at[me, mine], send_sem=sends.at[3 * i + j], recv_sem=recvs.at[3 * i + j],
        device_id=(cx, cy, c), device_id_type=MESH_IDS)


def allgather_start(bufs, name, deps=()):
    n = len(bufs)
    extra = len(deps)

    def body(*refs):
        ins = refs[:n]
        sends, recvs = refs[n + extra], refs[n + extra + 1]
        token = refs[2 * n + extra + 2]
        x, y, c, chips = _place()
        for i in range(n):
            for j, chip_xy in enumerate(chips):
                _gather_copy(ins[i], sends, recvs, i, j, 2 * x + y, chip_xy, c).start()
        token[...] = jnp.zeros_like(token)

    out = pl.pallas_call(
        body, name=name, in_specs=[HBM_SPEC] * n + [ANY_SPEC] * extra,
        out_specs=[SEM_SPEC, SEM_SPEC] + [HBM_SPEC] * n + [VMEM_SPEC],
        out_shape=[pltpu.SemaphoreType.DMA((3 * n,)), pltpu.SemaphoreType.DMA((3 * n,))]
        + [pltpu.HBM(b.shape, b.dtype) for b in bufs] + [jax.ShapeDtypeStruct((8, 128), F32)],
        input_output_aliases={i: i + 2 for i in range(n)},
        compiler_params=pltpu.CompilerParams(has_side_effects=DATAFLOW))(*[_hbm(b) for b in bufs], *deps)
    return out[0], out[1], list(out[2:2 + n]), out[2 + n]


def allgather_wait(sends, recvs, bufs, after, name, first=0):
    n = len(bufs)
    after = tuple(after) if isinstance(after, (tuple, list)) else (after,)

    def body(*refs):
        ins = refs[:n]
        send_sems, recv_sems = refs[n], refs[n + 1]
        x, y, c, chips = _place()
        for i in range(n):
            for j, (cx, cy) in enumerate(chips):
                mine = _half(ins[i].shape[1], c)
                cp = pltpu.make_async_remote_copy(
                    src_ref=ins[i].at[2 * x + y, mine], dst_ref=ins[i].at[2 * cx + cy, mine],
                    send_sem=send_sems.at[3 * (first + i) + j], recv_sem=recv_sems.at[3 * (first + i) + j],
                    device_id=(cx, cy, c), device_id_type=MESH_IDS)
                cp.wait_send()
                cp.wait_recv()

    return pl.pallas_call(
        body, name=name, in_specs=[HBM_SPEC] * n + [SEM_SPEC, SEM_SPEC] + [ANY_SPEC] * len(after), out_specs=[HBM_SPEC] * n,
        out_shape=[pltpu.HBM(b.shape, b.dtype) for b in bufs], input_output_aliases={i: i for i in range(n)},
        compiler_params=pltpu.CompilerParams(has_side_effects=DATAFLOW))(*bufs, sends, recvs, *after)


def d2d_forward(bufs, name):
    n = len(bufs)

    def body(*refs):
        ins = refs[:n]
        send_sems, recv_sems = refs[2 * n:]
        x, y, c, chips = _place()
        _sibling_handshake(x, y, c)
        copies = []
        for i in range(n):
            mine = _half(ins[i].shape[1], c)
            for j, (cx, cy) in enumerate(chips):
                landed = ins[i].at[2 * cx + cy, mine]
                cp = pltpu.make_async_remote_copy(
                    src_ref=landed, dst_ref=landed, send_sem=send_sems.at[i, j], recv_sem=recv_sems.at[i, j],
                    device_id=(x, y, 1 - c), device_id_type=MESH_IDS)
                cp.start()
                copies.append(cp)
        for i in range(n):
            theirs = _half(ins[i].shape[1], 1 - c)
            for j, (cx, cy) in enumerate(chips):
                passed = ins[i].at[2 * cx + cy, theirs]
                pltpu.make_async_remote_copy(
                    src_ref=passed, dst_ref=passed, send_sem=send_sems.at[i, j], recv_sem=recv_sems.at[i, j],
                    device_id=(x, y, 1 - c), device_id_type=MESH_IDS).wait_recv()
        for cp in copies:
            cp.wait_send()

    return pl.pallas_call(
        body, name=name, in_specs=[HBM_SPEC] * n, out_specs=[HBM_SPEC] * n,
        out_shape=[jax.ShapeDtypeStruct(b.shape, b.dtype) for b in bufs], input_output_aliases={i: i for i in range(n)},
        scratch_shapes=[pltpu.SemaphoreType.DMA((n, 3))] * 2,
        compiler_params=pltpu.CompilerParams(has_side_effects=True, collective_id=SIBLING_BARRIER))(*bufs)


def _forward_copy(buf, sends, recvs, i, j, chip_xy, x, y, c):
    cx, cy = chip_xy
    rows = buf.shape[1]
    return pltpu.make_async_remote_copy(
        src_ref=buf.at[2 * cx + cy, _half(rows, c)], dst_ref=buf.at[2 * cx + cy, _half(rows, 1 - c)],
        send_sem=sends.at[3 * i + j], recv_sem=recvs.at[3 * i + j], device_id=(x, y, 1 - c), device_id_type=MESH_IDS)


def d2d_forward_start(bufs, name):
    n = len(bufs)

    def body(*refs):
        ins = refs[:n]
        sends, recvs = refs[n], refs[n + 1]
        token = refs[2 * n + 2]
        x, y, c, chips = _place()
        _sibling_handshake(x, y, c)
        for i in range(n):
            mine = _half(ins[i].shape[1], c)
            for j, (cx, cy) in enumerate(chips):
                landed = ins[i].at[2 * cx + cy, mine]
                pltpu.make_async_remote_copy(
                    src_ref=landed, dst_ref=landed, send_sem=sends.at[3 * i + j], recv_sem=recvs.at[3 * i + j],
                    device_id=(x, y, 1 - c), device_id_type=MESH_IDS).start()
        token[...] = jnp.zeros_like(token)

    out = pl.pallas_call(
        body, name=name, in_specs=[HBM_SPEC] * n,
        out_specs=[SEM_SPEC, SEM_SPEC] + [HBM_SPEC] * n + [VMEM_SPEC],
        out_shape=[pltpu.SemaphoreType.DMA((3 * n,)), pltpu.SemaphoreType.DMA((3 * n,))]
        + [pltpu.HBM(b.shape, b.dtype) for b in bufs] + [jax.ShapeDtypeStruct((8, 128), F32)],
        input_output_aliases={i: i + 2 for i in range(n)},
        compiler_params=pltpu.CompilerParams(has_side_effects=DATAFLOW, collective_id=SIBLING_BARRIER))(*[_hbm(b) for b in bufs])
    return (out[0], out[1], list(out[2:2 + n])), out[2 + n]


def d2d_forward_wait(started, after, name):
    sends, recvs, bufs = started
    n = len(bufs)

    def body(*refs):
        ins = refs[:n]
        send_sems, recv_sems = refs[n], refs[n + 1]
        x, y, c, chips = _place()
        for i in range(n):
            for j, chip_xy in enumerate(chips):
                cp = _forward_copy(ins[i], send_sems, recv_sems, i, j, chip_xy, x, y, c)
                cp.wait_send()
                cp.wait_recv()

    return pl.pallas_call(
        body, name=name, in_specs=[HBM_SPEC] * n + [SEM_SPEC, SEM_SPEC, ANY_SPEC], out_specs=[HBM_SPEC] * n,
        out_shape=[pltpu.HBM(b.shape, b.dtype) for b in bufs], input_output_aliases={i: i for i in range(n)},
        compiler_params=pltpu.CompilerParams(has_side_effects=DATAFLOW))(*bufs, sends, recvs, after)


def _sibling_copy(src, land, sends, recvs, i, x, y, c, halves):
    part = src.at[:, _half(src.shape[1], 1 - c)] if halves else src
    return pltpu.make_async_remote_copy(
        src_ref=part, dst_ref=land, send_sem=sends.at[i], recv_sem=recvs.at[i], device_id=(x, y, 1 - c),
        device_id_type=MESH_IDS)


def sibling_start(arrays, halves, name):
    n = len(arrays)
    lands = [lax.empty((a.shape[0], a.shape[1] // 2, a.shape[2]) if halves else a.shape, a.dtype) for a in arrays]

    def body(*refs):
        srcs, zones = refs[:n], refs[n:2 * n]
        sends, recvs = refs[2 * n], refs[2 * n + 1]
        token = refs[4 * n + 2]
        x, y, c, _ = _place()
        _sibling_handshake(x, y, c)
        for i in range(n):
            _sibling_copy(srcs[i], zones[i], sends, recvs, i, x, y, c, halves).start()
        token[...] = jnp.zeros_like(token)

    out = pl.pallas_call(
        body, name=name, in_specs=[HBM_SPEC] * (2 * n),
        out_specs=[SEM_SPEC, SEM_SPEC] + [HBM_SPEC] * (2 * n) + [VMEM_SPEC],
        out_shape=[pltpu.SemaphoreType.DMA((n,)), pltpu.SemaphoreType.DMA((n,))]
        + [pltpu.HBM(a.shape, a.dtype) for a in arrays + lands] + [jax.ShapeDtypeStruct((8, 128), F32)],
        input_output_aliases={i: i + 2 for i in range(2 * n)},
        compiler_params=pltpu.CompilerParams(has_side_effects=DATAFLOW, collective_id=SIBLING_BARRIER))(
            *[_hbm(a) for a in arrays + lands])
    return (out[0], out[1], list(out[2:2 + n]), list(out[2 + n:2 + 2 * n])), out[2 + 2 * n]


def sibling_wait(started, halves, after, name):
    sends, recvs, arrays, lands = started
    n = len(arrays)
    after = tuple(after) if isinstance(after, (tuple, list)) else (after,)

    def body(*refs):
        srcs, zones = refs[:n], refs[n:2 * n]
        send_sems, recv_sems = refs[2 * n], refs[2 * n + 1]
        x, y, c, _ = _place()
        for i in range(n):
            cp = _sibling_copy(srcs[i], zones[i], send_sems, recv_sems, i, x, y, c, halves)
            cp.wait_send()
            cp.wait_recv()

    out = pl.pallas_call(
        body, name=name, in_specs=[HBM_SPEC] * (2 * n) + [SEM_SPEC, SEM_SPEC] + [ANY_SPEC] * len(after),
        out_specs=[HBM_SPEC] * (2 * n),
        out_shape=[pltpu.HBM(a.shape, a.dtype) for a in arrays + lands], input_output_aliases={i: i for i in range(2 * n)},
        compiler_params=pltpu.CompilerParams(has_side_effects=DATAFLOW))(*arrays, *lands, sends, recvs, *after)
    return list(out[:n]), list(out[n:])


def _scatter_copy(src, land, sends, recvs, i, j, chip_xy, c):
    cx, cy = chip_xy
    return pltpu.make_async_remote_copy(
        src_ref=src.at[2 * cx + cy], dst_ref=land.at[j], send_sem=sends.at[3 * i + j], recv_sem=recvs.at[3 * i + j],
        device_id=(cx, cy, c), device_id_type=MESH_IDS)


def scatter_start(sums, name):
    n = len(sums)
    lands = [lax.empty((3,) + s.shape[1:], s.dtype) for s in sums]

    def body(*refs):
        srcs, zones = refs[:n], refs[n:2 * n]
        sends, recvs = refs[2 * n], refs[2 * n + 1]
        token = refs[4 * n + 2]
        _, _, c, chips = _place()
        for i in range(n):
            for j, chip_xy in enumerate(chips):
                _scatter_copy(srcs[i], zones[i], sends, recvs, i, j, chip_xy, c).start()
        token[...] = jnp.zeros_like(token)

    out = pl.pallas_call(
        body, name=name, in_specs=[HBM_SPEC] * (2 * n),
        out_specs=[SEM_SPEC, SEM_SPEC] + [HBM_SPEC] * (2 * n) + [VMEM_SPEC],
        out_shape=[pltpu.SemaphoreType.DMA((3 * n,)), pltpu.SemaphoreType.DMA((3 * n,))]
        + [pltpu.HBM(a.shape, a.dtype) for a in sums + lands] + [jax.ShapeDtypeStruct((8, 128), F32)],
        input_output_aliases={i: i + 2 for i in range(2 * n)},
        compiler_params=pltpu.CompilerParams(has_side_effects=DATAFLOW))(*[_hbm(a) for a in sums + lands])
    return out[0], out[1], list(out[2:2 + n]), list(out[2 + n:2 + 2 * n]), out[2 + 2 * n]


def scatter_wait(sends, recvs, sums, lands, after, name):
    n = len(sums)

    def body(*refs):
        srcs, zones = refs[:n], refs[n:2 * n]
        send_sems, recv_sems = refs[2 * n], refs[2 * n + 1]
        _, _, c, chips = _place()
        for i in range(n):
            for j, chip_xy in enumerate(chips):
                cp = _scatter_copy(srcs[i], zones[i], send_sems, recv_sems, i, j, chip_xy, c)
                cp.wait_send()
                cp.wait_recv()

    out = pl.pallas_call(
        body, name=name, in_specs=[HBM_SPEC] * (2 * n) + [SEM_SPEC, SEM_SPEC, ANY_SPEC], out_specs=[HBM_SPEC] * (2 * n),
        out_shape=[pltpu.HBM(a.shape, a.dtype) for a in sums + lands], input_output_aliases={i: i for i in range(2 * n)},
        compiler_params=pltpu.CompilerParams(has_side_effects=DATAFLOW))(*sums, *lands, sends, recvs, after)
    return list(out[:n]), list(out[n:])


def add_halves(grads, recvs, core, name):
    n = len(grads)
    nk = grads[0].shape[0]
    views = [g.reshape(nk, 2, g.shape[1] // 2, g.shape[2]) for g in grads]

    def body(core_ref, *refs):
        for g_ref, r_ref, o_ref in zip(refs[:n], refs[n:2 * n], refs[2 * n:]):
            o_ref[0] = (g_ref[0, 0].astype(F32) + r_ref[0].astype(F32)).astype(BF16)

    return pl.pallas_call(
        body, name=name,
        grid_spec=pltpu.PrefetchScalarGridSpec(
            num_scalar_prefetch=1, grid=(nk,),
            in_specs=[pl.BlockSpec((1, 1) + v.shape[2:], lambda k, core_ref: (k, core_ref[0], 0, 0)) for v in views]
            + [pl.BlockSpec((1,) + r.shape[1:], lambda k, core_ref: (k, 0, 0)) for r in recvs],
            out_specs=[pl.BlockSpec((1,) + r.shape[1:], lambda k, core_ref: (k, 0, 0)) for r in recvs]),
        out_shape=[jax.ShapeDtypeStruct(r.shape, BF16) for r in recvs], compiler_params=_params("parallel"))(core, *views, *recvs)


def add_chips(sums, lands, chip, name):
    n = len(sums)

    def body(chip_ref, *refs):
        for s_ref, r_ref, o_ref in zip(refs[:n], refs[n:2 * n], refs[2 * n:]):
            o_ref[...] = ((s_ref[0].astype(F32) + r_ref[0].astype(F32)) + r_ref[1].astype(F32)) + r_ref[2].astype(F32)

    def rows(s):
        return (s.shape[1] // ROW_STEPS, s.shape[2])

    return pl.pallas_call(
        body, name=name,
        grid_spec=pltpu.PrefetchScalarGridSpec(
            num_scalar_prefetch=1, grid=(ROW_STEPS,),
            in_specs=[pl.BlockSpec((1,) + rows(s), lambda i, chip_ref: (chip_ref[0], i, 0)) for s in sums]
            + [pl.BlockSpec((3,) + rows(s), lambda i, chip_ref: (0, i, 0)) for s in sums],
            out_specs=[pl.BlockSpec(rows(s), lambda i, chip_ref: (i, 0)) for s in sums]),
        out_shape=[jax.ShapeDtypeStruct(s.shape[1:], F32) for s in sums], compiler_params=_params("parallel"))(chip, *sums, *lands)


def _adamw_math(w, g, m, v):
    m = ADAM_B1 * m + (1.0 - ADAM_B1) * g
    v = ADAM_B2 * v + (1.0 - ADAM_B2) * (g * g)
    m_hat = m / (1.0 - ADAM_B1 ** ADAM_STEP)
    v_hat = v / (1.0 - ADAM_B2 ** ADAM_STEP)
    return -ADAM_LR * (m_hat / (jnp.sqrt(v_hat) + ADAM_EPS) + ADAM_WD * w), m, v


def adamw_one_half(ws, gs, ms, vs, core, name, own, into=None):
    n = len(ws)
    r, cdim = ws[0].shape
    half = r // 2
    tr = _tile(half, max(8, 256 // n))
    steps = half // tr
    prior = () if into is None else tuple(a for group in into for a in group)

    def body(core_ref, *refs):
        ins, outs = refs[:4 * n], refs[4 * n + len(prior):]
        for a in range(n):
            w_ref, g_ref, m_ref, v_ref = ins[a::n]
            g = g_ref[...]
            outs[a][...] = g
            outs[n + a][...], outs[2 * n + a][...], outs[3 * n + a][...] = _adamw_math(w_ref[...], g, m_ref[...], v_ref[...])

    def rows(i, core_ref):
        return ((core_ref[0] if own else 1 - core_ref[0]) * steps + i, 0)

    whole = pl.BlockSpec((tr, cdim), rows)
    part = pl.BlockSpec((tr, cdim), lambda i, core_ref: (i, 0))
    out = pl.pallas_call(
        body, name=name,
        grid_spec=pltpu.PrefetchScalarGridSpec(
            num_scalar_prefetch=1, grid=(steps,),
            in_specs=[whole] * n + [part] * n + [whole] * (2 * n) + [ANY_SPEC] * len(prior), out_specs=[whole] * (4 * n)),
        out_shape=[jax.ShapeDtypeStruct((r, cdim), F32)] * (4 * n),
        input_output_aliases={1 + 4 * n + j: j for j in range(len(prior))},
        compiler_params=_params("parallel"))(core, *ws, *gs, *ms, *vs, *prior)
    return [out[k * n:(k + 1) * n] for k in range(4)]


def adamw_halves_sparsecore(ws, owns, others, ms, vs, name):
    n = len(ws)
    r, cdim = ws[0].shape
    half_groups = r // 2 // SC_ROWS
    per_tile = -(-2 * half_groups // SC_TILES)

    def body(*refs):
        ins, outs, (wb, gb, mb, vb) = refs[:5 * n], refs[5 * n:9 * n], refs[9 * n:]
        tile = lax.axis_index("sc_tile") * 2 + lax.axis_index("sc_core")
        core = lax.axis_index("c")

        @pl.loop(0, per_tile)
        def _(it):
            group = tile + SC_TILES * it

            @pl.when(group < 2 * half_groups)
            def _():
                rows = pl.ds(group * SC_ROWS, SC_ROWS)
                in_own = (group < half_groups) == (core == 0)
                half_rows = pl.ds((group % half_groups) * SC_ROWS, SC_ROWS)
                for a in range(n):
                    w_hbm, own_hbm, other_hbm, m_hbm, v_hbm = ins[a::n]
                    g_out, d_out, m_out, v_out = outs[a::n]
                    pltpu.sync_copy(w_hbm.at[rows], wb)
                    pltpu.sync_copy(m_hbm.at[rows], mb)
                    pltpu.sync_copy(v_hbm.at[rows], vb)

                    @pl.when(in_own)
                    def _():
                        pltpu.sync_copy(own_hbm.at[half_rows], gb)

                    @pl.when(jnp.logical_not(in_own))
                    def _():
                        pltpu.sync_copy(other_hbm.at[half_rows], gb)

                    pltpu.sync_copy(gb, g_out.at[rows])

                    @pl.loop(0, SC_ROWS)
                    def _(row):
                        @pl.loop(0, cdim, step=SC_LANES)
                        def _(col):
                            at = (row, pl.ds(col, SC_LANES))
                            wb[at], mb[at], vb[at] = _adamw_math(wb[at], gb[at], mb[at], vb[at])

                    pltpu.sync_copy(wb, d_out.at[rows])
                    pltpu.sync_copy(mb, m_out.at[rows])
                    pltpu.sync_copy(vb, v_out.at[rows])

    out = pl.kernel(
        body, name=name, out_type=[jax.ShapeDtypeStruct((r, cdim), F32)] * (4 * n),
        mesh=plsc.VectorSubcoreMesh(core_axis_name="sc_core", subcore_axis_name="sc_tile"),
        scratch_types=[pltpu.VMEM((SC_ROWS, cdim), F32)] * 4)(*ws, *owns, *others, *ms, *vs)
    return [out[a::n] for a in range(n)]


def small_allreduce_adamw(g, w, m, v, name):
    r, cdim = g.shape

    def body(g_ref, w_ref, m_ref, v_ref, go_ref, d_ref, mo_ref, vo_ref, pair, quad, d2d_send, d2d_recv, ici_send, ici_recv):
        x, y, c, chips = _place()
        me = 2 * x + y
        pair[c] = g_ref[...]
        swap = pltpu.make_async_remote_copy(
            src_ref=g_ref, dst_ref=pair.at[c], send_sem=d2d_send, recv_sem=d2d_recv, device_id=(x, y, 1 - c),
            device_id_type=MESH_IDS)
        swap.start()
        swap.wait()
        quad[me] = pair[0] + pair[1]
        copies = []
        for j, (cx, cy) in enumerate(chips):
            cp = pltpu.make_async_remote_copy(
                src_ref=quad.at[me], dst_ref=quad.at[me], send_sem=ici_send.at[j], recv_sem=ici_recv.at[j],
                device_id=(cx, cy, c), device_id_type=MESH_IDS)
            cp.start()
            copies.append(cp)
        for j, (cx, cy) in enumerate(chips):
            slot = quad.at[2 * cx + cy]
            pltpu.make_async_remote_copy(
                src_ref=slot, dst_ref=slot, send_sem=ici_send.at[j], recv_sem=ici_recv.at[j], device_id=(cx, cy, c),
                device_id_type=MESH_IDS).wait_recv()
        for cp in copies:
            cp.wait_send()
        total = (quad[0] + quad[1]) + (quad[2] + quad[3])
        go_ref[...] = total
        d_ref[...], mo_ref[...], vo_ref[...] = _adamw_math(w_ref[...], total, m_ref[...], v_ref[...])

    return pl.pallas_call(
        body, name=name, in_specs=[VMEM_SPEC] * 4, out_specs=[VMEM_SPEC] * 4,
        out_shape=[jax.ShapeDtypeStruct((r, cdim), F32)] * 4,
        scratch_shapes=[pltpu.VMEM((2, r, cdim), F32), pltpu.VMEM((N_CHIPS, r, cdim), F32), pltpu.SemaphoreType.DMA,
                        pltpu.SemaphoreType.DMA, pltpu.SemaphoreType.DMA((3,)), pltpu.SemaphoreType.DMA((3,))],
        compiler_params=pltpu.CompilerParams(has_side_effects=True, vmem_limit_bytes=VMEM_LIMIT_V7X))(g, w, m, v)


GROUPS = (
    ("ffn1", ("ffn1_w_gate", "ffn1_w_up", "ffn1_w_down")),
    ("mixer", ("w_in", "pool_w", "w_out_a", "w_out_b", "w_o")),
    ("ffn2", ("ffn2_w_gate", "ffn2_w_up", "ffn2_w_down")),
    ("ple", ("ple_w_gate", "ple_w_proj")),
)
GATHERS = (
    ("ffn1_in", ("ffn1_w_gate", "ffn1_w_up")),
    ("ffn1_out", ("ffn1_w_down",)),
) + GROUPS[1:]
GAINS = ("ffn1_pre_g", "ffn1_post_g", "mix_pre_g", "sgu_norm_g", "pool_scale", "mix_post_g",
         "ffn2_pre_g", "ffn2_post_g", "ple_pre_g", "ple_post_g")
SMALL = GAINS + ("sgu_b", "sgu_w")
PACKED = SMALL + ("loss",)
WEIGHTS = ("ffn1_pre_g", "ffn1_w_gate", "ffn1_w_up", "ffn1_w_down", "ffn1_post_g", "mix_pre_g", "w_in", "sgu_norm_g",
           "sgu_w", "sgu_b", "pool_w", "pool_scale", "w_out_a", "w_out_b", "w_o", "mix_post_g", "ffn2_pre_g",
           "ffn2_w_gate", "ffn2_w_up", "ffn2_w_down", "ffn2_post_g", "ple_pre_g", "ple_w_gate", "ple_w_proj", "ple_post_g")
PACK_ROWS = 16


TRANSPOSED = ("ffn1_w_gate", "ffn1_w_up", "ffn2_w_gate", "ffn2_w_up")


def _shard2d(name, a):
    a = a[0]
    return a.T if name in TRANSPOSED else a.reshape(-1, a.shape[-1])


def _unshard2d(name, a2d, shape):
    return (a2d.T if name in TRANSPOSED else a2d).reshape(shape)


def pack_rows(gains, sgu_b, loss_tile, name):
    n = len(gains)
    d = gains[0].shape[1]
    g = sgu_b.shape[0]

    def body(*refs):
        o_ref = refs[-1]
        o_ref[...] = jnp.zeros_like(o_ref)
        for i in range(n):
            o_ref[i:i + 1, :] = refs[i][...]
        o_ref[n:n + g, 0:CHUNK] = refs[n][...]
        o_ref[n + g:n + g + 1, 0:CHUNK] = refs[n + 1][0:1, :]

    return pl.pallas_call(
        body, name=name, in_specs=[VMEM_SPEC] * (n + 2), out_specs=VMEM_SPEC,
        out_shape=jax.ShapeDtypeStruct((PACK_ROWS, d), F32))(*gains, sgu_b, loss_tile)


def _pack_small(parts, tag):
    d = parts[GAINS[0]].shape[-1]
    rows = pack_rows([parts[n] for n in GAINS], parts["sgu_b"].reshape(-1, CHUNK), parts["loss"], "pack_" + tag)
    return jnp.concatenate([rows, parts["sgu_w"].reshape(-1, d)], axis=0)


def _unpack_small(packed, like):
    n, g = len(GAINS), like["sgu_b"].size // CHUNK
    out = {name: packed[i:i + 1] for i, name in enumerate(GAINS)}
    out["sgu_b"] = packed[n:n + g, :CHUNK].reshape(like["sgu_b"].shape)
    out["loss"] = packed[n + g, 0]
    out["sgu_w"] = packed[PACK_ROWS:].reshape(like["sgu_w"].shape)
    return out


def _ffn_fwd(xn, h, w, pre, g_post, g_next, tag, between=None, first=None):
    a4, s4, t4 = ffn_gu(xn, w[pre + "w_gate"], w[pre + "w_up"], tag + "_gu") if first is None else first(xn)
    deps = ()
    if between is not None:
        more, deps = between(a4)
        w.update(more)
    f, h_new, xn_next = mm_norm_res(a4, w[pre + "w_down"], h, g_post, g_next, 0.5, tag + "_down", deps=deps)
    return dict(xn=xn, h=h, a4=a4, s4=s4, t4=t4, f=f), h_new, xn_next


def _ffn_bwd_w(dh, saved, w, pre, g_post, tag, deps):
    df, dg4, du4, d_post = ffn_bwd_a(dh, saved["f"], g_post, w[pre + "w_down"], saved["s4"], saved["t4"], 0.5, tag + "_bwd_a",
                                     deps=deps)
    grads, exchanging, token = _ffn_dw(saved, df, dg4, du4, pre, tag)
    return dg4, du4, grads, d_post, (exchanging, token)


def _ffn_dw(saved, df, dg4, du4, pre, tag):
    jobs = {pre + "w_down": (saved["a4"], df), pre + "w_gate": (dg4, saved["xn"]), pre + "w_up": (du4, saved["xn"])}
    grads, started, deps = {}, {}, ()
    for name, (x, dy) in jobs.items():
        grads[name] = dw_tn(x, dy, N_CHIPS, f"dw_{name}", x_kmajor=True, dy_mode="same", deps=deps)
        started[name], token = sibling_start([grads[name]], True, "exchange_start_" + name)
        deps = (token,)
    return grads, [started[pre + kind] for kind in ("w_gate", "w_up", "w_down")], token


def _ffn_bwd_x(dh, dg4, du4, saved, w, pre, g_pre, tag, deps):
    return dx_norm_bwd([(dg4, w[pre + "w_gate"], False), (du4, w[pre + "w_up"], False)], saved["h"], g_pre, dh,
                       tag + "_bwd_x", deps=deps)


def _start_gathers(weights, chip):
    first_tag, first_names = GATHERS[0]
    made, own = cast_place([_shard2d(n, weights[n]) for n in first_names], chip, "cast_" + first_tag, plain=True)
    sends, recvs, bufs, token = allgather_start(list(made), "allgather_start_" + first_tag)
    gathering = {first_tag: (sends, recvs, bufs, 0)}
    later = [n for _, names in GATHERS[1:] for n in names]
    cast = cast_place([_shard2d(n, weights[n]) for n in later], chip, "cast_later", deps=(token,))
    sends, recvs, bufs, token = allgather_start(list(cast), "allgather_start_later")
    first = 0
    for tag, names in GATHERS[1:]:
        gathering[tag] = (sends, recvs, bufs[first:first + len(names)], first)
        first += len(names)
    return gathering, token, dict(zip(first_names, own))


def _gathered(started, names, after, tag):
    sends, recvs, bufs, first = started
    landed = allgather_wait(sends, recvs, bufs, after, "allgather_wait_" + tag, first)
    return dict(zip(names, d2d_forward(landed, "d2d_forward_" + tag)))


def _forward_early(started, after, tag):
    sends, recvs, bufs, first = started
    landed = allgather_wait(sends, recvs, bufs, after, "allgather_wait_" + tag, first)
    return d2d_forward_start(landed, "d2d_forward_start_" + tag)


def _forwarded(forwarding, names, after, tag):
    return dict(zip(names, d2d_forward_wait(forwarding, after, "d2d_forward_wait_" + tag)))


def _exchange_start(names, big_g, tag):
    started, token = sibling_start([big_g[n] for n in names], True, "exchange_start_" + tag)
    return [started], token


def _scatter_begin(names, exchanging, after, core, tag):
    partial, from_sibling = [], []
    for e, started in enumerate(exchanging):
        mine, theirs = sibling_wait(started, True, after, f"exchange_wait_{tag}_{e}")
        partial += mine
        from_sibling += theirs
    chip_sums = list(add_halves(partial, from_sibling, core, "add_halves_" + tag))
    sends, recvs, sums, lands, token = scatter_start(chip_sums, "scatter_start_" + tag)
    return (names, sends, recvs, sums, lands), token


def _share_begin(scattering, after, chip, tag):
    names, sends, recvs, sums, lands = scattering
    sums, lands = scatter_wait(sends, recvs, sums, lands, after, "scatter_wait_" + tag)
    reduced = list(add_chips(sums, lands, chip, "add_chips_" + tag))
    sharing, token = sibling_start(reduced, False, "share_start_" + tag)
    return (names, sharing), token


def _update(shared, after, weights, moments_m, moments_v, tag, results):
    names, sharing = shared
    reduced, others = sibling_wait(sharing, False, after, "share_wait_" + tag)
    for b, idx in enumerate(_by_shape(reduced)):
        in_bucket = [names[i] for i in idx]
        outs = adamw_halves_sparsecore(
            [_shard2d(n, weights[n]) for n in in_bucket], [reduced[i] for i in idx], [others[i] for i in idx],
            [_shard2d(n, moments_m[n]) for n in in_bucket], [_shard2d(n, moments_v[n]) for n in in_bucket], f"adamw_{tag}_{b}")
        for n, per_weight in zip(in_bucket, outs):
            for store, value in zip(results, per_weight):
                store[n] = _unshard2d(n, value, weights[n].shape)


def _update_last(scattering, after, weights, moments_m, moments_v, core, chip, tag, results):
    names, sends, recvs, sums, lands = scattering
    sums, lands = scatter_wait(sends, recvs, sums, lands, after, "scatter_wait_" + tag)
    reduced = list(add_chips(sums, lands, chip, "add_chips_" + tag))
    (sends, recvs, reduced, zones), _ = sibling_start(reduced, False, "share_start_" + tag)
    state = [[_shard2d(n, held[n]) for n in names] for held in (weights, moments_m, moments_v)]
    first = adamw_one_half(state[0], reduced, state[1], state[2], core, f"adamw_{tag}_own", own=True)
    _, others = sibling_wait((sends, recvs, reduced, zones), False, first[1][-1], "share_wait_" + tag)
    outs = adamw_one_half(state[0], others, state[1], state[2], core, f"adamw_{tag}_other", own=False, into=first)
    for store, values in zip(results, outs):
        for n, value in zip(names, values):
            store[n] = _unshard2d(n, value, weights[n].shape)


def kernel(x, p, ffn1_pre_g, ffn1_w_gate, ffn1_w_up, ffn1_w_down, ffn1_post_g, mix_pre_g, w_in, sgu_norm_g, sgu_w, sgu_b, pool_w, pool_scale, w_out_a, w_out_b, w_o, mix_post_g, ffn2_pre_g, ffn2_w_gate, ffn2_w_up, ffn2_w_down, ffn2_post_g, ple_pre_g, ple_w_gate, ple_w_proj, ple_post_g, loss_target, m_ffn1_pre_g, m_ffn1_w_gate, m_ffn1_w_up, m_ffn1_w_down, m_ffn1_post_g, m_mix_pre_g, m_w_in, m_sgu_norm_g, m_sgu_w, m_sgu_b, m_pool_w, m_pool_scale, m_w_out_a, m_w_out_b, m_w_o, m_mix_post_g, m_ffn2_pre_g, m_ffn2_w_gate, m_ffn2_w_up, m_ffn2_w_down, m_ffn2_post_g, m_ple_pre_g, m_ple_w_gate, m_ple_w_proj, m_ple_post_g, v_ffn1_pre_g, v_ffn1_w_gate, v_ffn1_w_up, v_ffn1_w_down, v_ffn1_post_g, v_mix_pre_g, v_w_in, v_sgu_norm_g, v_sgu_w, v_sgu_b, v_pool_w, v_pool_scale, v_w_out_a, v_w_out_b, v_w_o, v_mix_post_g, v_ffn2_pre_g, v_ffn2_w_gate, v_ffn2_w_up, v_ffn2_w_down, v_ffn2_post_g, v_ple_pre_g, v_ple_w_gate, v_ple_w_proj, v_ple_post_g):
    given = dict(locals())
    weights = {n: given[n] for n in WEIGHTS}
    moments_m = {n: given["m_" + n] for n in WEIGHTS}
    moments_v = {n: given["v_" + n] for n in WEIGHTS}
    core = lax.axis_index("c").astype(jnp.int32).reshape(1)
    chip = (2 * lax.axis_index("x") + lax.axis_index("y")).astype(jnp.int32).reshape(1)

    d = x.shape[-1]
    gathering, last_start, own_first = _start_gathers(weights, chip)
    token = (last_start,)
    gain = {n: weights[n] for n in GAINS}
    sgu_w3 = sgu_w[0]
    sgu_b3 = sgu_b[0][:, :, None]
    groups = dict(GROUPS + GATHERS)

    h0 = x[0]
    tgt = loss_target[0]
    p_bf = p[0, 0].astype(BF16)
    xn1 = rms_cast(h0, gain["ffn1_pre_g"], "ffn1_pre_norm")
    w = {}

    def ffn1_first(xn):
        own = ffn_gu(xn, own_first["ffn1_w_gate"][None], own_first["ffn1_w_up"][None], "ffn1_gu_own", chip=chip)
        w.update(_gathered(gathering["ffn1_in"], groups["ffn1_in"], (own[0], last_start), "ffn1_in"))
        return ffn_gu(xn, w["ffn1_w_gate"], w["ffn1_w_up"], "ffn1_gu", chip=chip, into=own)

    s1, h1, xn2 = _ffn_fwd(xn1, h0, w, "ffn1_", gain["ffn1_post_g"], gain["mix_pre_g"], "ffn1",
                           lambda a4: (_gathered(gathering["ffn1_out"], groups["ffn1_out"], a4, "ffn1_out"), ()), ffn1_first)
    w.update(_gathered(gathering["mixer"], groups["mixer"], h1, "mixer"))
    full = {n: w[n].reshape(-1, d) for n in ("w_out_a", "w_out_b", "w_o")}
    n_groups = pool_w.shape[1]
    rows_per = pool_w.shape[2]
    dgp = pool_w.shape[3]
    pool_full = w["pool_w"].reshape(N_CHIPS, n_groups, rows_per, dgp).transpose(1, 0, 2, 3).reshape(n_groups, N_CHIPS * rows_per, dgp)
    z = mixer_in(xn2, w["w_in"], "mixer_in")
    a = sgu_fwd(z, gain["sgu_norm_g"], sgu_w3, sgu_b3, "sgu_fwd")
    diff, b = pool_fwd(z, pool_full, gain["pool_scale"], "pool_fwd")
    ya, yb, y = mixer_y(a, b, z, full["w_out_a"], full["w_out_b"], "mixer_y")
    forwarding, tok = _forward_early(gathering["ffn2"], y, "ffn2")
    m, h2, xn3 = mm_norm_res(y[None], full["w_o"][None], h1, gain["mix_post_g"], gain["ffn2_pre_g"], 1.0, "mixer_out",
                             deps=(tok,))
    w.update(_forwarded(forwarding, groups["ffn2"], h2, "ffn2"))
    early = {}

    def forward_ple(a4):
        early["ple"], tok_ple = _forward_early(gathering["ple"], a4, "ple")
        return {}, (tok_ple,)

    s2, h3, xn4 = _ffn_fwd(xn3, h2, w, "ffn2_", gain["ffn2_post_g"], gain["ple_pre_g"], "ffn2", forward_ple)
    w.update(_forwarded(early["ple"], groups["ple"], h3, "ple"))
    full["ple_w_gate"] = w["ple_w_gate"].reshape(-1, d)
    proj_full = w["ple_w_proj"].transpose(1, 0, 2).reshape(ple_w_proj.shape[1], -1)

    small_g, big_g = {}, {}
    ds, de, dh3, small_g["ple_post_g"], small_g["ple_pre_g"], loss_part = ple_loss(
        xn4, p_bf, full["ple_w_gate"], proj_full, h3, gain["ple_post_g"], gain["ple_pre_g"], tgt, "ple_loss")
    small_g["loss"] = loss_part
    grad, delta, new_m, new_v = {}, {}, {}, {}
    results = (grad, delta, new_m, new_v)
    update = functools.partial(_update, weights=weights, moments_m=moments_m, moments_v=moments_v, results=results)

    def reduce_behind(tag, previous, started=None):
        exchanging, tok = _exchange_start(groups[tag], big_g, tag) if started is None else started
        if previous is not None:
            shared, tok = _share_begin(previous[1], tok, chip, previous[0])
        scattering, tok_scatter = _scatter_begin(groups[tag], exchanging, tok, core, tag)
        if previous is not None:
            update(shared, tok_scatter, tag=previous[0])
        return (tag, scattering), (tok_scatter,)

    big_g["ple_w_gate"] = dw_tn(xn4, ds, 1, "dw_ple_gate").reshape(N_CHIPS, -1, d)
    big_g["ple_w_proj"] = dw_tn(p_bf, de, N_CHIPS, "dw_ple_proj", dy_mode="cols")
    reducing, deps = reduce_behind("ple", None)

    dg4, du4, g2, small_g["ffn2_post_g"], exchanging = _ffn_bwd_w(dh3, s2, w, "ffn2_", gain["ffn2_post_g"], "ffn2", deps)
    big_g.update(g2)
    dh2, small_g["ffn2_pre_g"] = _ffn_bwd_x(dh3, dg4, du4, s2, w, "ffn2_", gain["ffn2_pre_g"], "ffn2", ())
    reducing, deps = reduce_behind("ffn2", reducing, exchanging)

    dm, dya, dyb, dz, da, db, small_g["mix_post_g"] = mixer_bwd_y(
        dh2, m, gain["mix_post_g"], full["w_o"], ya, yb, z, full["w_out_a"], full["w_out_b"], "mixer_bwd_y", deps=deps)
    big_g["w_o"] = dw_tn(y, dm, 1, "dw_o").reshape(N_CHIPS, -1, d)
    big_g["w_out_a"] = dw_tn(a, dya, 1, "dw_out_a").reshape(N_CHIPS, -1, d)
    big_g["w_out_b"] = dw_tn(b, dyb, 1, "dw_out_b").reshape(N_CHIPS, -1, d)
    dz, d_sgu_w, d_sgu_b, small_g["sgu_norm_g"] = sgu_bwd(z, da, dz, gain["sgu_norm_g"], sgu_w3, sgu_b3, "sgu_bwd")
    dz, d_pool_w, small_g["pool_scale"] = pool_bwd(db, diff, dz, pool_full, gain["pool_scale"], "pool_bwd")
    big_g["pool_w"] = d_pool_w.astype(BF16).reshape(n_groups, N_CHIPS, rows_per, dgp).transpose(1, 0, 2, 3).reshape(
        N_CHIPS, n_groups * rows_per, dgp)
    big_g["w_in"] = dw_in_tiles(xn2, dz, N_CHIPS, "dw_in")
    dh1, small_g["mix_pre_g"] = dx_norm_bwd([(dz, w["w_in"], True)], h1, gain["mix_pre_g"], dh2, "mixer_bwd_x")
    reducing, deps = reduce_behind("mixer", reducing)

    dg4, du4, g1, small_g["ffn1_post_g"], exchanging = _ffn_bwd_w(dh1, s1, w, "ffn1_", gain["ffn1_post_g"], "ffn1", deps)
    big_g.update(g1)
    reducing, deps = reduce_behind("ffn1", reducing, exchanging)
    small_g["sgu_w"] = d_sgu_w
    small_g["sgu_b"] = d_sgu_b[:, :, 0]

    last_gain = "ffn1_pre_g"
    small_g[last_gain] = jnp.zeros_like(gain[last_gain])
    no_state = {"loss": jnp.zeros((8, 128), F32)}
    packed = small_allreduce_adamw(
        _pack_small(small_g, "grads"), _pack_small({n: weights[n] for n in SMALL} | no_state, "weights"),
        _pack_small({n: moments_m[n] for n in SMALL} | no_state, "m"), _pack_small({n: moments_v[n] for n in SMALL} | no_state, "v"),
        "small_allreduce_adamw")
    like = {n: weights[n] for n in SMALL}
    for store, block in zip((grad, delta, new_m, new_v), packed):
        store.update(_unpack_small(block, like))
    dh0, d_last_gain = _ffn_bwd_x(dh1, dg4, du4, s1, w, "ffn1_", gain["ffn1_pre_g"], "ffn1", deps + (packed[0],))
    rows = [jnp.pad(a, ((0, 7), (0, 0))) for a in (d_last_gain, weights[last_gain], moments_m[last_gain], moments_v[last_gain])]
    last = small_allreduce_adamw(*rows, "last_gain_allreduce_adamw")
    for store, block in zip((grad, delta, new_m, new_v), last):
        store[last_gain] = block[:1]
    _update_last(reducing[1], last[0], weights, moments_m, moments_v, core, chip, "ffn1", results)

    return (grad["loss"], dh0[None], *[grad[n] for n in WEIGHTS], *[delta[n] for n in WEIGHTS],
            *[new_m[n] for n in WEIGHTS], *[new_v[n] for n in WEIGHTS])
```

```python
import functools

import jax
import jax.numpy as jnp
from jax import lax
from jax.experimental import pallas as pl
from jax.experimental.pallas import tpu as pltpu
from jax.experimental.pallas import tpu_sc as plsc

F32 = jnp.float32
BF16 = jnp.bfloat16
EPS = 1e-6
CHUNK = 128
POOL_WINDOWS = (2, 4, 8, 16)
HALO = 16
N_CHIPS = 4
ADAM_LR, ADAM_B1, ADAM_B2, ADAM_EPS, ADAM_WD, ADAM_STEP = 0.001, 0.9, 0.999, 1e-08, 0.01, 10
VMEM_LIMIT_V7X = 58 * 1024 * 1024
MESH_IDS = pl.DeviceIdType.MESH
HBM_SPEC = pl.BlockSpec(memory_space=pltpu.HBM)
VMEM_SPEC = pl.BlockSpec(memory_space=pltpu.VMEM)
SEM_SPEC = pl.BlockSpec(memory_space=pltpu.SEMAPHORE)
ANY_SPEC = pl.BlockSpec(memory_space=pl.ANY)
DATAFLOW = pltpu.SideEffectType.DATAFLOW_SIDE_EFFECTING
OTHER_CHIPS = ((1, 0), (0, 1), (1, 1))
DZ_SLOT = (2, 3, 4, 0, 1)
SC_TILES, SC_LANES, SC_ROWS = 32, 16, 8
ROW_STEPS = 4
SIBLING_BARRIER = 1
DW_TOKENS = 4096
DW_IN_TILE = 256

NT = (((1,), (1,)), ((), ()))
TN = (((0,), (0,)), ((), ()))


def _params(*sem, **more):
    return pltpu.CompilerParams(dimension_semantics=sem or None, vmem_limit_bytes=VMEM_LIMIT_V7X, **more)


def _tile(t, want):
    return max(c for c in range(8, min(t, want) + 1, 8) if t % c == 0)


def _const(shape):
    return pl.BlockSpec(shape, lambda *_: (0,) * len(shape))


def _rows(tm, d, col=0):
    return pl.BlockSpec((tm, d), lambda i: (i, col))


def _kmajor(nk, tm, kb):
    return pl.BlockSpec((nk, tm, kb), lambda i: (0, i, 0))


def _dot(a, b):
    return jnp.dot(a, b, preferred_element_type=F32)


def _dot_nt(a, b):
    return lax.dot_general(a, b, NT, preferred_element_type=F32)


def _dot_tn(a, b):
    return lax.dot_general(a, b, TN, preferred_element_type=F32)


def _gelu(x):
    return 0.5 * x * (1.0 + jnp.tanh(0.7978845608028654 * (x + 0.044715 * x * x * x)))


def _gelu_and_grad(x):
    k, kc = 0.7978845608028654, 0.7978845608028654 * 0.044715
    x2 = x * x
    th = jnp.tanh(x * (k + kc * x2))
    cdf = 0.5 + 0.5 * th
    return x * cdf, cdf + x * (0.5 - 0.5 * th * th) * (k + 3.0 * kc * x2)


def _sigmoid(x):
    return 1.0 / (1.0 + jnp.exp(-x))


def _rstd(h):
    return lax.rsqrt(jnp.mean(h * h, axis=-1, keepdims=True) + EPS)


def _rms_bwd(h, g, dy):
    r = _rstd(h)
    t = dy * g
    dh = r * t - h * (r * r * r) * jnp.mean(h * t, axis=-1, keepdims=True)
    return dh, jnp.sum(dy * h * r, axis=0, keepdims=True)


def _ordered_after(body, n_in, deps):
    if not deps:
        return body
    return lambda *refs: body(*refs[:n_in], *refs[n_in + len(deps):])


def _accumulate(ref, value):
    @pl.when(pl.program_id(0) == 0)
    def _():
        ref[...] = jnp.zeros_like(ref)

    ref[...] += value


def rms_cast(h, g, name):
    t, d = h.shape
    tm = _tile(t, 512)

    def body(h_ref, g_ref, o_ref):
        hv = h_ref[...]
        o_ref[...] = (hv * _rstd(hv) * g_ref[...]).astype(BF16)

    return pl.pallas_call(
        body, name=name, grid=(t // tm,), in_specs=[_rows(tm, d), _const((1, d))], out_specs=_rows(tm, d),
        out_shape=jax.ShapeDtypeStruct((t, d), BF16), compiler_params=_params("parallel"))(h, g)


def ffn_gu(xn, wgt, wut, name, chip=None, into=None):
    t, d = xn.shape
    nkw, fk, _ = wgt.shape
    tm = _tile(t, 512)
    count = N_CHIPS if chip is None else (1 if into is None else N_CHIPS - 1)
    first = 0 if into is None else 1

    def body(*refs):
        x_ref, wg_ref, wu_ref = refs[-6 - (0 if into is None else 3):][:3]
        a_ref, s_ref, t_ref = refs[-3:]
        xv = x_ref[...]
        g = _dot_nt(xv, wg_ref[0])
        sg = _sigmoid(g)
        s = g * sg
        s_ref[0] = s.astype(BF16)
        ds = sg * (1.0 + g * (1.0 - sg))
        u = _dot_nt(xv, wu_ref[0])
        a_ref[0] = (s * u).astype(BF16)
        t_ref[0] = (u * ds).astype(BF16)

    def slot(k, *chip_ref):
        return (chip_ref[0][0] + first + k) % N_CHIPS if chip_ref else k

    w_spec = pl.BlockSpec((1, fk, d), lambda k, i, *c: (slot(k, *c) if nkw > 1 else 0, 0, 0))
    o_spec = pl.BlockSpec((1, tm, fk), lambda k, i, *c: (slot(k, *c), i, 0))
    shape = jax.ShapeDtypeStruct((N_CHIPS, t, fk), BF16)
    prior = () if into is None else tuple(into)
    call = pl.pallas_call(
        body, name=name,
        grid_spec=pltpu.PrefetchScalarGridSpec(
            num_scalar_prefetch=0 if chip is None else 1, grid=(count, t // tm),
            in_specs=[pl.BlockSpec((tm, d), lambda k, i, *c: (i, 0)), w_spec, w_spec] + [ANY_SPEC] * len(prior),
            out_specs=[o_spec] * 3),
        out_shape=[shape] * 3, input_output_aliases={4 + j: j for j in range(len(prior))},
        compiler_params=_params("parallel", "parallel"))
    return call(*(() if chip is None else (chip,)), xn, wgt, wut, *prior)


def mm_norm_res(a3, w3, h_old, g_post, g_next, scale, name, deps=()):
    nk, t, kb = a3.shape
    d = w3.shape[2]
    tm = _tile(t, 256)

    def body(a_ref, w_ref, h_ref, gp_ref, gn_ref, f_ref, hn_ref, xn_ref):
        f = _dot(a_ref[0], w_ref[0])
        for k in range(1, nk):
            f += _dot(a_ref[k], w_ref[k])
        f_ref[...] = f.astype(BF16)
        hn = h_ref[...] + scale * (f * _rstd(f) * gp_ref[...])
        hn_ref[...] = hn
        xn_ref[...] = (hn * _rstd(hn) * gn_ref[...]).astype(BF16)

    return pl.pallas_call(
        _ordered_after(body, 5, deps), name=name, grid=(t // tm,),
        in_specs=[_kmajor(nk, tm, kb), _const((nk, kb, d)), _rows(tm, d), _const((1, d)), _const((1, d))] + [ANY_SPEC] * len(deps),
        out_specs=[_rows(tm, d)] * 3,
        out_shape=[jax.ShapeDtypeStruct((t, d), BF16), jax.ShapeDtypeStruct((t, d), F32), jax.ShapeDtypeStruct((t, d), BF16)],
        compiler_params=_params("parallel"))(a3, w3, h_old, g_post, g_next, *deps)


def mixer_in(xn, win4, name):
    t, d = xn.shape
    nk, _, nb = win4.shape
    tm = _tile(t, 512)

    def body(x_ref, w_ref, z_ref):
        z_ref[...] = _dot(x_ref[...], w_ref[0]).astype(BF16)

    return pl.pallas_call(
        body, name=name, grid=(nk, t // tm),
        in_specs=[pl.BlockSpec((tm, d), lambda k, i: (i, 0)), pl.BlockSpec((1, d, nb), lambda k, i: (k, 0, 0))],
        out_specs=pl.BlockSpec((tm, nb), lambda k, i: (i, k)), out_shape=jax.ShapeDtypeStruct((t, nk * nb), BF16),
        compiler_params=_params("parallel", "parallel"))(xn, win4)


def _causal_mask():
    row = lax.broadcasted_iota(jnp.int32, (CHUNK, CHUNK), 0)
    col = lax.broadcasted_iota(jnp.int32, (CHUNK, CHUNK), 1)
    return row >= col


def _layernorm_parts(v):
    mu = jnp.mean(v, axis=-1, keepdims=True)
    vc = v - mu
    r = lax.rsqrt(jnp.mean(vc * vc, axis=-1, keepdims=True) + EPS)
    return vc * r, r


def sgu_fwd(z, norm_g, sgu_w, sgu_b3, name):
    t = z.shape[0]
    d = norm_g.shape[1]
    ng = sgu_w.shape[0]
    dg = d // ng
    tm = _tile(t, 256)

    def body(zu_ref, zv_ref, ng_ref, w_ref, b_ref, a_ref):
        vhat, _ = _layernorm_parts(_gelu(zv_ref[...].astype(F32)))
        vn = (vhat * ng_ref[...]).astype(BF16)
        u = _gelu(zu_ref[...].astype(F32))
        mask = _causal_mask()
        for g in range(ng):
            wg = jnp.where(mask, w_ref[g], 0.0).astype(BF16)
            for ci in range(tm // CHUNK):
                rs, cs = slice(ci * CHUNK, (ci + 1) * CHUNK), slice(g * dg, (g + 1) * dg)
                sv = _dot(wg, vn[rs, cs]) + b_ref[g]
                a_ref[rs, cs] = (u[rs, cs] * sv).astype(BF16)

    return pl.pallas_call(
        body, name=name, grid=(t // tm,),
        in_specs=[_rows(tm, d, 0), _rows(tm, d, 1), _const((1, d)), _const((ng, CHUNK, CHUNK)), _const((ng, CHUNK, 1))],
        out_specs=_rows(tm, d), out_shape=jax.ShapeDtypeStruct((t, d), BF16),
        compiler_params=_params("parallel"))(z, z, norm_g, sgu_w, sgu_b3)


def pool_fwd(z, pool_w, pool_scale, name):
    t = z.shape[0]
    d = pool_scale.shape[1]
    ng = pool_w.shape[0]
    dg = d // ng
    tm = _tile(t, 256)
    per = tm // HALO

    def body(c_ref, prev_ref, w_ref, s_ref, diff_ref, b_ref):
        i = pl.program_id(0)
        cur = c_ref[...].astype(F32)
        prev = jnp.where(i > 0, prev_ref[...].astype(F32), 0.0)
        ext = jnp.concatenate([prev, cur], axis=0)
        tok = i * tm + lax.broadcasted_iota(jnp.int32, (tm, 1), 0)
        for g, win in enumerate(POOL_WINDOWS):
            cs = slice(g * dg, (g + 1) * dg)
            s = ext[:, cs]
            sh = 1
            while sh < win:
                s = s + pltpu.roll(s, sh, 0)
                sh *= 2
            per_count = 1.0 / jnp.minimum(tok + 1, win).astype(F32)
            diff = (s[HALO:] * per_count - cur[:, cs]).astype(BF16)
            diff_ref[:, cs] = diff
            b_ref[:, cs] = (_dot(diff, w_ref[g]) * s_ref[:, cs]).astype(BF16)

    return pl.pallas_call(
        body, name=name, grid=(t // tm,),
        in_specs=[_rows(tm, d, 2), pl.BlockSpec((HALO, d), lambda i: (jnp.maximum(i * per - 1, 0), 2)),
                  _const((ng, dg, dg)), _const((1, d))],
        out_specs=[_rows(tm, d)] * 2, out_shape=[jax.ShapeDtypeStruct((t, d), BF16)] * 2,
        compiler_params=_params("parallel"))(z, z, pool_w, pool_scale)


def mixer_y(a, b, z, woa, wob, name):
    t, d = a.shape
    tm = _tile(t, 256)

    def body(a_ref, b_ref, ga_ref, gb_ref, wa_ref, wb_ref, ya_ref, yb_ref, y_ref):
        ya = _dot(a_ref[...], wa_ref[...])
        yb = _dot(b_ref[...], wb_ref[...])
        ya_ref[...] = ya.astype(BF16)
        yb_ref[...] = yb.astype(BF16)
        y_ref[...] = (_sigmoid(ga_ref[...].astype(F32)) * ya + _sigmoid(gb_ref[...].astype(F32)) * yb).astype(BF16)

    return pl.pallas_call(
        body, name=name, grid=(t // tm,),
        in_specs=[_rows(tm, d), _rows(tm, d), _rows(tm, d, 3), _rows(tm, d, 4), _const((d, d)), _const((d, d))],
        out_specs=[_rows(tm, d)] * 3, out_shape=[jax.ShapeDtypeStruct((t, d), BF16)] * 3,
        compiler_params=_params("parallel"))(a, b, z, z, woa, wob)


def ple_loss(xn, p, wpg, wpp, h, g_post, g_pre, target, name):
    t, d = xn.shape
    dp = p.shape[1]
    tm = _tile(t, 256)

    def body(x_ref, p_ref, wg_ref, wp_ref, h_ref, gp_ref, gn_ref, tg_ref, ds_ref, de_ref, dhp_ref, dgp_ref, dgn_ref, loss_ref):
        gate = _sigmoid(_dot(x_ref[...], wg_ref[...]))
        e = _dot(p_ref[...], wp_ref[...])
        q = gate * e
        hv = h_ref[...]
        err = hv + q * _rstd(q) * gp_ref[...] - tg_ref[...]
        _accumulate(loss_ref, jnp.full(loss_ref.shape, (0.5 / d) * jnp.sum(err * err), F32))
        dhv = err * (1.0 / d)
        dq, dgp = _rms_bwd(q, gp_ref[...], dhv)
        ds = (dq * e * gate * (1.0 - gate)).astype(BF16)
        ds_ref[...] = ds
        de_ref[...] = (dq * gate).astype(BF16)
        dx, dgn = _rms_bwd(hv, gn_ref[...], _dot_nt(ds, wg_ref[...]))
        dhp_ref[...] = dhv + dx
        _accumulate(dgp_ref, dgp)
        _accumulate(dgn_ref, dgn)

    return pl.pallas_call(
        body, name=name, grid=(t // tm,),
        in_specs=[_rows(tm, d), _rows(tm, dp), _const((d, d)), _const((dp, d)), _rows(tm, d), _const((1, d)), _const((1, d)),
                  _rows(tm, d)],
        out_specs=[_rows(tm, d)] * 3 + [_const((1, d))] * 2 + [_const((8, 128))],
        out_shape=[jax.ShapeDtypeStruct((t, d), BF16), jax.ShapeDtypeStruct((t, d), BF16), jax.ShapeDtypeStruct((t, d), F32),
                   jax.ShapeDtypeStruct((1, d), F32), jax.ShapeDtypeStruct((1, d), F32), jax.ShapeDtypeStruct((8, 128), F32)],
        compiler_params=_params("arbitrary"))(xn, p, wpg, wpp, h, g_post, g_pre, target)


def ffn_bwd_a(dh, f, g_post, wd4, s4, t4, scale, name, deps=()):
    t, d = dh.shape
    nk, fk, _ = wd4.shape
    tm = _tile(t, 256)

    def body(dh_ref, f_ref, gp_ref, w_ref, s_ref, t_ref, df_ref, dg_ref, du_ref, dgp_ref):
        df, dgp = _rms_bwd(f_ref[...].astype(F32), gp_ref[...], dh_ref[...])
        df = (scale * df).astype(BF16)
        df_ref[...] = df
        _accumulate(dgp_ref, scale * dgp)
        for k in range(nk):
            da = _dot_nt(df, w_ref[k])
            du_ref[k] = (da * s_ref[k].astype(F32)).astype(BF16)
            dg_ref[k] = (da * t_ref[k].astype(F32)).astype(BF16)

    return pl.pallas_call(
        _ordered_after(body, 6, deps), name=name, grid=(t // tm,),
        in_specs=[_rows(tm, d), _rows(tm, d), _const((1, d)), _const((nk, fk, d)), _kmajor(nk, tm, fk), _kmajor(nk, tm, fk)]
        + [ANY_SPEC] * len(deps),
        out_specs=[_rows(tm, d), _kmajor(nk, tm, fk), _kmajor(nk, tm, fk), _const((1, d))],
        out_shape=[jax.ShapeDtypeStruct((t, d), BF16), jax.ShapeDtypeStruct((nk, t, fk), BF16),
                   jax.ShapeDtypeStruct((nk, t, fk), BF16), jax.ShapeDtypeStruct((1, d), F32)],
        compiler_params=_params("arbitrary"))(dh, f, g_post, wd4, s4, t4, *deps)


def dx_norm_bwd(pairs, h, g_pre, dh_in, name, deps=()):
    t, d = h.shape
    tm = _tile(t, 256)
    n = len(pairs)

    def body(*refs):
        dys, ws = refs[:n], refs[n:2 * n]
        h_ref, g_ref, dhi_ref, dho_ref, dg_ref = refs[2 * n:]
        acc = None
        for (_, w4, sections), dy_ref, w_ref in zip(pairs, dys, ws):
            if sections:
                wide = w4.shape[2]
                edges = sorted(set(range(0, 5 * d + 1, d)) | set(range(0, 5 * d + 1, wide)))
                parts = [_dot_nt(dy_ref[DZ_SLOT[lo // d], :, lo % d:lo % d + hi - lo], w_ref[lo // wide, :, lo % wide:lo % wide + hi - lo])
                         for lo, hi in zip(edges[:-1], edges[1:])]
            else:
                parts = [_dot(dy_ref[k], w_ref[k]) for k in range(w4.shape[0])]
            for part in parts:
                acc = part if acc is None else acc + part
        dx, dg = _rms_bwd(h_ref[...], g_ref[...], acc)
        dho_ref[...] = dhi_ref[...] + dx
        _accumulate(dg_ref, dg)

    dy_specs = [_kmajor(dy.shape[0], tm, dy.shape[2]) for dy, _, _ in pairs]
    return pl.pallas_call(
        _ordered_after(body, 2 * n + 3, deps), name=name, grid=(t // tm,),
        in_specs=dy_specs + [_const(w4.shape) for _, w4, _ in pairs] + [_rows(tm, d), _const((1, d)), _rows(tm, d)]
        + [ANY_SPEC] * len(deps),
        out_specs=[_rows(tm, d), _const((1, d))],
        out_shape=[jax.ShapeDtypeStruct((t, d), F32), jax.ShapeDtypeStruct((1, d), F32)],
        compiler_params=_params("arbitrary"))(*[dy for dy, _, _ in pairs], *[w4 for _, w4, _ in pairs], h, g_pre, dh_in, *deps)


def mixer_bwd_y(dh, m, g_post, w_o, ya, yb, z, woa, wob, name, deps=()):
    t, d = dh.shape
    tm = _tile(t, 256)

    def body(dh_ref, m_ref, gp_ref, wo_ref, ya_ref, yb_ref, ga_ref, gb_ref, wa_ref, wb_ref,
             dm_ref, dya_ref, dyb_ref, dz_ref, da_ref, db_ref, dgp_ref):
        dm, dgp = _rms_bwd(m_ref[...].astype(F32), gp_ref[...], dh_ref[...])
        dm = dm.astype(BF16)
        dm_ref[...] = dm
        _accumulate(dgp_ref, dgp)
        dy = _dot_nt(dm, wo_ref[...])
        sa = _sigmoid(ga_ref[...].astype(F32))
        sb = _sigmoid(gb_ref[...].astype(F32))
        dya = (dy * sa).astype(BF16)
        dyb = (dy * sb).astype(BF16)
        dya_ref[...] = dya
        dyb_ref[...] = dyb
        dz_ref[0] = (dy * ya_ref[...].astype(F32) * sa * (1.0 - sa)).astype(BF16)
        dz_ref[1] = (dy * yb_ref[...].astype(F32) * sb * (1.0 - sb)).astype(BF16)
        da_ref[...] = _dot_nt(dya, wa_ref[...]).astype(BF16)
        db_ref[...] = _dot_nt(dyb, wb_ref[...]).astype(BF16)

    return pl.pallas_call(
        _ordered_after(body, 10, deps), name=name, grid=(t // tm,),
        in_specs=[_rows(tm, d), _rows(tm, d), _const((1, d)), _const((d, d)), _rows(tm, d), _rows(tm, d),
                  _rows(tm, d, 3), _rows(tm, d, 4), _const((d, d)), _const((d, d))] + [ANY_SPEC] * len(deps),
        out_specs=[_rows(tm, d)] * 3 + [pl.BlockSpec((2, tm, d), lambda i: (0, i, 0))] + [_rows(tm, d)] * 2 + [_const((1, d))],
        out_shape=[jax.ShapeDtypeStruct((t, d), BF16)] * 3 + [jax.ShapeDtypeStruct((5, t, d), BF16)]
        + [jax.ShapeDtypeStruct((t, d), BF16)] * 2 + [jax.ShapeDtypeStruct((1, d), F32)],
        compiler_params=_params("arbitrary"))(dh, m, g_post, w_o, ya, yb, z, z, woa, wob, *deps)


def sgu_bwd(z, da, dz, norm_g, sgu_w, sgu_b3, name):
    t, d = da.shape
    ng = sgu_w.shape[0]
    dg = d // ng
    tm = _tile(t, 256)
    steps = t // tm

    def body(zu_ref, zv_ref, da_ref, ng_ref, w_ref, b_ref, _, dz_ref, dw_ref, db_ref, dng_ref, dvn_ref, dsv_ref):
        i = pl.program_id(0)
        zv = zv_ref[...].astype(F32)
        zu = zu_ref[...].astype(F32)
        v, gv = _gelu_and_grad(zv)
        vhat, r = _layernorm_parts(v)
        gain = ng_ref[...]
        vn = (vhat * gain).astype(BF16)
        u, gu = _gelu_and_grad(zu)
        dav = da_ref[...].astype(F32)
        mask = _causal_mask()

        @pl.when(i == 0)
        def _():
            dw_ref[...] = jnp.zeros_like(dw_ref)
            dsv_ref[...] = jnp.zeros_like(dsv_ref)

        for g in range(ng):
            wg = jnp.where(mask, w_ref[g], 0.0).astype(BF16)
            dw = jnp.zeros((CHUNK, CHUNK), F32)
            dsv_sum = jnp.zeros((CHUNK, dg), F32)
            for ci in range(tm // CHUNK):
                rs, cs = slice(ci * CHUNK, (ci + 1) * CHUNK), slice(g * dg, (g + 1) * dg)
                vn_blk = vn[rs, cs]
                sv = _dot(wg, vn_blk) + b_ref[g]
                dz_ref[0, rs, cs] = (dav[rs, cs] * sv * gu[rs, cs]).astype(BF16)
                dsv = dav[rs, cs] * u[rs, cs]
                dsv_sum += dsv
                dsv = dsv.astype(BF16)
                dw += _dot_nt(dsv, vn_blk)
                dvn_ref[rs, cs] = _dot_tn(wg, dsv)
            dw_ref[g] += dw
            dsv_ref[:, cs] += dsv_sum

        dvn = dvn_ref[...]
        _accumulate(dng_ref, jnp.sum(dvn * vhat, axis=0, keepdims=True))
        dvh = dvn * gain
        dv = r * (dvh - jnp.mean(dvh, axis=-1, keepdims=True) - vhat * jnp.mean(dvh * vhat, axis=-1, keepdims=True))
        dz_ref[1] = (dv * gv).astype(BF16)

        @pl.when(i == steps - 1)
        def _():
            for g in range(ng):
                dw_ref[g] = jnp.where(mask, dw_ref[g], 0.0)
                row_sum = jnp.sum(dsv_ref[:, g * dg:(g + 1) * dg], axis=1, keepdims=True)
                db_ref[g] = jnp.broadcast_to(row_sum, (CHUNK, CHUNK))

    return pl.pallas_call(
        body, name=name, grid=(steps,),
        in_specs=[_rows(tm, d, 0), _rows(tm, d, 1), _rows(tm, d), _const((1, d)), _const((ng, CHUNK, CHUNK)), _const((ng, CHUNK, 1)),
                  ANY_SPEC],
        out_specs=[pl.BlockSpec((2, tm, d), lambda i: (DZ_SLOT[0] // 2, i, 0)), _const((ng, CHUNK, CHUNK)),
                   _const((ng, CHUNK, CHUNK)), _const((1, d))],
        out_shape=[jax.ShapeDtypeStruct(dz.shape, BF16), jax.ShapeDtypeStruct((ng, CHUNK, CHUNK), F32),
                   jax.ShapeDtypeStruct((ng, CHUNK, CHUNK), F32), jax.ShapeDtypeStruct((1, d), F32)],
        scratch_shapes=[pltpu.VMEM((tm, d), F32), pltpu.VMEM((CHUNK, d), F32)], input_output_aliases={6: 0},
        compiler_params=_params("arbitrary"))(z, z, da, norm_g, sgu_w, sgu_b3, dz)


def pool_bwd(db, diff, dz, pool_w, pool_scale, name):
    t, d = db.shape
    ng = pool_w.shape[0]
    dg = d // ng
    tm = _tile(t, 256)
    per = tm // HALO
    steps = t // tm

    def body(db_ref, next_ref, diff_ref, w_ref, s_ref, _, dc_ref, dw_ref, ds_ref):
        i = pl.program_id(0)
        dbc = db_ref[...].astype(F32)
        nxt = jnp.where(i < steps - 1, next_ref[...].astype(F32), 0.0)
        ext = jnp.concatenate([dbc, nxt], axis=0)
        rows = tm + HALO
        tok = i * tm + lax.broadcasted_iota(jnp.int32, (rows, 1), 0)

        @pl.when(i == 0)
        def _():
            dw_ref[...] = jnp.zeros_like(dw_ref)
            ds_ref[...] = jnp.zeros_like(ds_ref)

        for g, win in enumerate(POOL_WINDOWS):
            cs = slice(g * dg, (g + 1) * dg)
            dp = (ext[:, cs] * s_ref[:, cs]).astype(BF16)
            dd = _dot_nt(dp, w_ref[g])
            s = dd * (1.0 / jnp.minimum(tok + 1, win).astype(F32))
            sh = 1
            while sh < win:
                s = s + pltpu.roll(s, rows - sh, 0)
                sh *= 2
            dc_ref[0, :, cs] = (s[:tm] - dd[:tm]).astype(BF16)
            dfg = diff_ref[:, cs]
            ds_ref[:, cs] += jnp.sum(dbc[:, cs] * _dot(dfg, w_ref[g]), axis=0, keepdims=True)
            dw_ref[g] += _dot_tn(dfg, dp[:tm])

    return pl.pallas_call(
        body, name=name, grid=(steps,),
        in_specs=[_rows(tm, d), pl.BlockSpec((HALO, d), lambda i: (jnp.minimum((i + 1) * per, t // HALO - 1), 0)),
                  _rows(tm, d), _const((ng, dg, dg)), _const((1, d)), ANY_SPEC],
        out_specs=[pl.BlockSpec((1, tm, d), lambda i: (DZ_SLOT[2], i, 0)), _const((ng, dg, dg)), _const((1, d))],
        out_shape=[jax.ShapeDtypeStruct(dz.shape, BF16), jax.ShapeDtypeStruct((ng, dg, dg), F32), jax.ShapeDtypeStruct((1, d), F32)],
        input_output_aliases={5: 0},
        compiler_params=_params("arbitrary"))(db, db, diff, pool_w, pool_scale, dz)


def dw_tn(x, dy, nk, name, x_kmajor=False, dy_mode="same", deps=()):
    t = x.shape[-2]
    kx = x.shape[-1]
    n = dy.shape[-1] // nk if dy_mode == "cols" else dy.shape[-1]
    tt = _tile(t, DW_TOKENS)
    steps = t // tt

    def body(x_ref, dy_ref, o_ref, acc_ref):
        s = pl.program_id(1)
        part = _dot_tn(x_ref[0] if x_kmajor else x_ref[...], dy_ref[0] if dy_mode == "kmajor" else dy_ref[...])
        if steps == 1:
            o_ref[0] = part.astype(BF16)
            return

        @pl.when(s == 0)
        def _():
            acc_ref[...] = jnp.zeros_like(acc_ref)

        acc_ref[...] += part

        @pl.when(s == steps - 1)
        def _():
            o_ref[0] = acc_ref[...].astype(BF16)

    x_spec = pl.BlockSpec((1, tt, kx), lambda k, s: (k, s, 0)) if x_kmajor else pl.BlockSpec((tt, kx), lambda k, s: (s, 0))
    dy_spec = {"kmajor": pl.BlockSpec((1, tt, n), lambda k, s: (k, s, 0)), "cols": pl.BlockSpec((tt, n), lambda k, s: (s, k)),
               "same": pl.BlockSpec((tt, n), lambda k, s: (s, 0))}[dy_mode]
    return pl.pallas_call(
        _ordered_after(body, 2, deps), name=name, grid=(nk, steps), in_specs=[x_spec, dy_spec] + [ANY_SPEC] * len(deps),
        out_specs=pl.BlockSpec((1, kx, n), lambda k, s: (k, 0, 0)), out_shape=jax.ShapeDtypeStruct((nk, kx, n), BF16),
        scratch_shapes=[pltpu.VMEM((kx, n) if steps > 1 else (8, 128), F32)],
        compiler_params=_params("parallel", "arbitrary"))(x, dy, *deps)


def dw_in_tiles(xn, dz, nk, name):
    t, d = xn.shape
    sections = len(DZ_SLOT)
    per_section = d // DW_IN_TILE
    per_shard = sections * per_section // nk

    def body(x_ref, dy_ref, o_ref):
        o_ref[0] = _dot_tn(x_ref[...], dy_ref[0]).astype(BF16)

    def slot(j):
        return (j // per_section + DZ_SLOT[0]) % sections

    return pl.pallas_call(
        body, name=name, grid=(sections * per_section,),
        in_specs=[pl.BlockSpec((t, d), lambda j: (0, 0)), pl.BlockSpec((1, t, DW_IN_TILE), lambda j: (slot(j), 0, j % per_section))],
        out_specs=pl.BlockSpec((1, d, DW_IN_TILE), lambda j: (j // per_shard, 0, j % per_shard)),
        out_shape=jax.ShapeDtypeStruct((nk, d, sections * d // nk), BF16), compiler_params=_params("parallel"))(xn, dz)


def _place():
    x, y, c = lax.axis_index("x"), lax.axis_index("y"), lax.axis_index("c")
    chips = [((1 - x) if fx else x, (1 - y) if fy else y) for fx, fy in OTHER_CHIPS]
    return x, y, c, chips


def _sibling_handshake(x, y, c):
    barrier = pltpu.get_barrier_semaphore()
    pl.semaphore_signal(barrier, inc=1, device_id=(x, y, 1 - c), device_id_type=MESH_IDS)
    pl.semaphore_wait(barrier, 1)


def _half(rows, which):
    return pl.ds(pl.multiple_of(which * (rows // 2), 16), rows // 2)


def _hbm(a):
    return pltpu.with_memory_space_constraint(a, pltpu.HBM)


def _by_shape(arrays):
    buckets = {}
    for i, a in enumerate(arrays):
        buckets.setdefault(a.shape, []).append(i)
    return list(buckets.values())


def cast_place(shards, chip, name, deps=(), plain=False):
    n = len(shards)

    def body(chip_ref, *refs):
        outs = refs[n + len(deps):]
        for a, w_ref in enumerate(refs[:n]):
            cast = w_ref[...].astype(BF16)
            outs[a][0] = cast
            if plain:
                outs[n + a][...] = cast

    def rows(s):
        return (s.shape[0] // ROW_STEPS, s.shape[1])

    out = pl.pallas_call(
        body, name=name,
        grid_spec=pltpu.PrefetchScalarGridSpec(
            num_scalar_prefetch=1, grid=(ROW_STEPS,),
            in_specs=[pl.BlockSpec(rows(s), lambda i, chip_ref: (i, 0)) for s in shards] + [ANY_SPEC] * len(deps),
            out_specs=[pl.BlockSpec((1,) + rows(s), lambda i, chip_ref: (chip_ref[0], i, 0)) for s in shards]
            + [pl.BlockSpec(rows(s), lambda i, chip_ref: (i, 0)) for s in shards] * plain),
        out_shape=[jax.ShapeDtypeStruct((N_CHIPS,) + s.shape, BF16) for s in shards]
        + [jax.ShapeDtypeStruct(s.shape, BF16) for s in shards] * plain,
        compiler_params=_params("parallel"))(chip, *shards, *deps)
    return (out[:n], out[n:]) if plain else out


def _gather_copy(buf, sends, recvs, i, j, me, chip_xy, c):
    cx, cy = chip_xy
    mine = _half(buf.shape[1], c)
    return pltpu.make_async_remote_copy(
        src_ref=buf.at[me, mine], dst_ref=buf.at[me, mine], send_sem=sends.at[3 * i + j], recv_sem=recvs.at[3 * i + j],
        device_id=(cx, cy, c), device_id_type=MESH_IDS)


def allgather_start(bufs, name):
    n = len(bufs)

    def body(*refs):
        ins = refs[:n]
        sends, recvs = refs[n], refs[n + 1]
        token = refs[2 * n + 2]
        x, y, c, chips = _place()
        for i in range(n):
            for j, chip_xy in enumerate(chips):
                _gather_copy(ins[i], sends, recvs, i, j, 2 * x + y, chip_xy, c).start()
        token[...] = jnp.zeros_like(token)

    out = pl.pallas_call(
        body, name=name, in_specs=[HBM_SPEC] * n,
        out_specs=[SEM_SPEC, SEM_SPEC] + [HBM_SPEC] * n + [VMEM_SPEC],
        out_shape=[pltpu.SemaphoreType.DMA((3 * n,)), pltpu.SemaphoreType.DMA((3 * n,))]
        + [pltpu.HBM(b.shape, b.dtype) for b in bufs] + [jax.ShapeDtypeStruct((8, 128), F32)],
        input_output_aliases={i: i + 2 for i in range(n)},
        compiler_params=pltpu.CompilerParams(has_side_effects=DATAFLOW))(*[_hbm(b) for b in bufs])
    return out[0], out[1], list(out[2:2 + n]), out[2 + n]


def allgather_wait(sends, recvs, bufs, after, name, first=0):
    n = len(bufs)
    after = tuple(after) if isinstance(after, (tuple, list)) else (after,)

    def body(*refs):
        ins = refs[:n]
        send_sems, recv_sems = refs[n], refs[n + 1]
        x, y, c, chips = _place()
        for i in range(n):
            for j, (cx, cy) in enumerate(chips):
                mine = _half(ins[i].shape[1], c)
                cp = pltpu.make_async_remote_copy(
                    src_ref=ins[i].at[2 * x + y, mine], dst_ref=ins[i].at[2 * cx + cy, mine],
                    send_sem=send_sems.at[3 * (first + i) + j], recv_sem=recv_sems.at[3 * (first + i) + j],
                    device_id=(cx, cy, c), device_id_type=MESH_IDS)
                cp.wait_send()
                cp.wait_recv()

    return pl.pallas_call(
        body, name=name, in_specs=[HBM_SPEC] * n + [SEM_SPEC, SEM_SPEC] + [ANY_SPEC] * len(after), out_specs=[HBM_SPEC] * n,
        out_shape=[pltpu.HBM(b.shape, b.dtype) for b in bufs], input_output_aliases={i: i for i in range(n)},
        compiler_params=pltpu.CompilerParams(has_side_effects=DATAFLOW))(*bufs, sends, recvs, *after)


def d2d_forward(bufs, name):
    n = len(bufs)

    def body(*refs):
        ins = refs[:n]
        send_sems, recv_sems = refs[2 * n:]
        x, y, c, chips = _place()
        _sibling_handshake(x, y, c)
        copies = []
        for i in range(n):
            mine = _half(ins[i].shape[1], c)
            for j, (cx, cy) in enumerate(chips):
                landed = ins[i].at[2 * cx + cy, mine]
                cp = pltpu.make_async_remote_copy(
                    src_ref=landed, dst_ref=landed, send_sem=send_sems.at[i, j], recv_sem=recv_sems.at[i, j],
                    device_id=(x, y, 1 - c), device_id_type=MESH_IDS)
                cp.start()
                copies.append(cp)
        for i in range(n):
            theirs = _half(ins[i].shape[1], 1 - c)
            for j, (cx, cy) in enumerate(chips):
                passed = ins[i].at[2 * cx + cy, theirs]
                pltpu.make_async_remote_copy(
                    src_ref=passed, dst_ref=passed, send_sem=send_sems.at[i, j], recv_sem=recv_sems.at[i, j],
                    device_id=(x, y, 1 - c), device_id_type=MESH_IDS).wait_recv()
        for cp in copies:
            cp.wait_send()

    return pl.pallas_call(
        body, name=name, in_specs=[HBM_SPEC] * n, out_specs=[HBM_SPEC] * n,
        out_shape=[jax.ShapeDtypeStruct(b.shape, b.dtype) for b in bufs], input_output_aliases={i: i for i in range(n)},
        scratch_shapes=[pltpu.SemaphoreType.DMA((n, 3))] * 2,
        compiler_params=pltpu.CompilerParams(has_side_effects=True, collective_id=SIBLING_BARRIER))(*bufs)


def _forward_copy(buf, sends, recvs, i, j, chip_xy, x, y, c):
    cx, cy = chip_xy
    rows = buf.shape[1]
    return pltpu.make_async_remote_copy(
        src_ref=buf.at[2 * cx + cy, _half(rows, c)], dst_ref=buf.at[2 * cx + cy, _half(rows, 1 - c)],
        send_sem=sends.at[3 * i + j], recv_sem=recvs.at[3 * i + j], device_id=(x, y, 1 - c), device_id_type=MESH_IDS)


def d2d_forward_start(bufs, name):
    n = len(bufs)

    def body(*refs):
        ins = refs[:n]
        sends, recvs = refs[n], refs[n + 1]
        token = refs[2 * n + 2]
        x, y, c, chips = _place()
        _sibling_handshake(x, y, c)
        for i in range(n):
            mine = _half(ins[i].shape[1], c)
            for j, (cx, cy) in enumerate(chips):
                landed = ins[i].at[2 * cx + cy, mine]
                pltpu.make_async_remote_copy(
                    src_ref=landed, dst_ref=landed, send_sem=sends.at[3 * i + j], recv_sem=recvs.at[3 * i + j],
                    device_id=(x, y, 1 - c), device_id_type=MESH_IDS).start()
        token[...] = jnp.zeros_like(token)

    out = pl.pallas_call(
        body, name=name, in_specs=[HBM_SPEC] * n,
        out_specs=[SEM_SPEC, SEM_SPEC] + [HBM_SPEC] * n + [VMEM_SPEC],
        out_shape=[pltpu.SemaphoreType.DMA((3 * n,)), pltpu.SemaphoreType.DMA((3 * n,))]
        + [pltpu.HBM(b.shape, b.dtype) for b in bufs] + [jax.ShapeDtypeStruct((8, 128), F32)],
        input_output_aliases={i: i + 2 for i in range(n)},
        compiler_params=pltpu.CompilerParams(has_side_effects=DATAFLOW, collective_id=SIBLING_BARRIER))(*[_hbm(b) for b in bufs])
    return (out[0], out[1], list(out[2:2 + n])), out[2 + n]


def d2d_forward_wait(started, after, name):
    sends, recvs, bufs = started
    n = len(bufs)

    def body(*refs):
        ins = refs[:n]
        send_sems, recv_sems = refs[n], refs[n + 1]
        x, y, c, chips = _place()
        for i in range(n):
            for j, chip_xy in enumerate(chips):
                cp = _forward_copy(ins[i], send_sems, recv_sems, i, j, chip_xy, x, y, c)
                cp.wait_send()
                cp.wait_recv()

    return pl.pallas_call(
        body, name=name, in_specs=[HBM_SPEC] * n + [SEM_SPEC, SEM_SPEC, ANY_SPEC], out_specs=[HBM_SPEC] * n,
        out_shape=[pltpu.HBM(b.shape, b.dtype) for b in bufs], input_output_aliases={i: i for i in range(n)},
        compiler_params=pltpu.CompilerParams(has_side_effects=DATAFLOW))(*bufs, sends, recvs, after)


def _sibling_copy(src, land, sends, recvs, i, x, y, c, halves):
    part = src.at[:, _half(src.shape[1], 1 - c)] if halves else src
    return pltpu.make_async_remote_copy(
        src_ref=part, dst_ref=land, send_sem=sends.at[i], recv_sem=recvs.at[i], device_id=(x, y, 1 - c),
        device_id_type=MESH_IDS)


def sibling_start(arrays, halves, name):
    n = len(arrays)
    lands = [lax.empty((a.shape[0], a.shape[1] // 2, a.shape[2]) if halves else a.shape, a.dtype) for a in arrays]

    def body(*refs):
        srcs, zones = refs[:n], refs[n:2 * n]
        sends, recvs = refs[2 * n], refs[2 * n + 1]
        token = refs[4 * n + 2]
        x, y, c, _ = _place()
        _sibling_handshake(x, y, c)
        for i in range(n):
            _sibling_copy(srcs[i], zones[i], sends, recvs, i, x, y, c, halves).start()
        token[...] = jnp.zeros_like(token)

    out = pl.pallas_call(
        body, name=name, in_specs=[HBM_SPEC] * (2 * n),
        out_specs=[SEM_SPEC, SEM_SPEC] + [HBM_SPEC] * (2 * n) + [VMEM_SPEC],
        out_shape=[pltpu.SemaphoreType.DMA((n,)), pltpu.SemaphoreType.DMA((n,))]
        + [pltpu.HBM(a.shape, a.dtype) for a in arrays + lands] + [jax.ShapeDtypeStruct((8, 128), F32)],
        input_output_aliases={i: i + 2 for i in range(2 * n)},
        compiler_params=pltpu.CompilerParams(has_side_effects=DATAFLOW, collective_id=SIBLING_BARRIER))(
            *[_hbm(a) for a in arrays + lands])
    return (out[0], out[1], list(out[2:2 + n]), list(out[2 + n:2 + 2 * n])), out[2 + 2 * n]


def sibling_wait(started, halves, after, name):
    sends, recvs, arrays, lands = started
    n = len(arrays)
    after = tuple(after) if isinstance(after, (tuple, list)) else (after,)

    def body(*refs):
        srcs, zones = refs[:n], refs[n:2 * n]
        send_sems, recv_sems = refs[2 * n], refs[2 * n + 1]
        x, y, c, _ = _place()
        for i in range(n):
            cp = _sibling_copy(srcs[i], zones[i], send_sems, recv_sems, i, x, y, c, halves)
            cp.wait_send()
            cp.wait_recv()

    out = pl.pallas_call(
        body, name=name, in_specs=[HBM_SPEC] * (2 * n) + [SEM_SPEC, SEM_SPEC] + [ANY_SPEC] * len(after),
        out_specs=[HBM_SPEC] * (2 * n),
        out_shape=[pltpu.HBM(a.shape, a.dtype) for a in arrays + lands], input_output_aliases={i: i for i in range(2 * n)},
        compiler_params=pltpu.CompilerParams(has_side_effects=DATAFLOW))(*arrays, *lands, sends, recvs, *after)
    return list(out[:n]), list(out[n:])


def _scatter_copy(src, land, sends, recvs, i, j, chip_xy, c):
    cx, cy = chip_xy
    return pltpu.make_async_remote_copy(
        src_ref=src.at[2 * cx + cy], dst_ref=land.at[j], send_sem=sends.at[3 * i + j], recv_sem=recvs.at[3 * i + j],
        device_id=(cx, cy, c), device_id_type=MESH_IDS)


def scatter_start(sums, name):
    n = len(sums)
    lands = [lax.empty((3,) + s.shape[1:], s.dtype) for s in sums]

    def body(*refs):
        srcs, zones = refs[:n], refs[n:2 * n]
        sends, recvs = refs[2 * n], refs[2 * n + 1]
        token = refs[4 * n + 2]
        _, _, c, chips = _place()
        for i in range(n):
            for j, chip_xy in enumerate(chips):
                _scatter_copy(srcs[i], zones[i], sends, recvs, i, j, chip_xy, c).start()
        token[...] = jnp.zeros_like(token)

    out = pl.pallas_call(
        body, name=name, in_specs=[HBM_SPEC] * (2 * n),
        out_specs=[SEM_SPEC, SEM_SPEC] + [HBM_SPEC] * (2 * n) + [VMEM_SPEC],
        out_shape=[pltpu.SemaphoreType.DMA((3 * n,)), pltpu.SemaphoreType.DMA((3 * n,))]
        + [pltpu.HBM(a.shape, a.dtype) for a in sums + lands] + [jax.ShapeDtypeStruct((8, 128), F32)],
        input_output_aliases={i: i + 2 for i in range(2 * n)},
        compiler_params=pltpu.CompilerParams(has_side_effects=DATAFLOW))(*[_hbm(a) for a in sums + lands])
    return out[0], out[1], list(out[2:2 + n]), list(out[2 + n:2 + 2 * n]), out[2 + 2 * n]


def scatter_wait(sends, recvs, sums, lands, after, name):
    n = len(sums)

    def body(*refs):
        srcs, zones = refs[:n], refs[n:2 * n]
        send_sems, recv_sems = refs[2 * n], refs[2 * n + 1]
        _, _, c, chips = _place()
        for i in range(n):
            for j, chip_xy in enumerate(chips):
                cp = _scatter_copy(srcs[i], zones[i], send_sems, recv_sems, i, j, chip_xy, c)
                cp.wait_send()
                cp.wait_recv()

    out = pl.pallas_call(
        body, name=name, in_specs=[HBM_SPEC] * (2 * n) + [SEM_SPEC, SEM_SPEC, ANY_SPEC], out_specs=[HBM_SPEC] * (2 * n),
        out_shape=[pltpu.HBM(a.shape, a.dtype) for a in sums + lands], input_output_aliases={i: i for i in range(2 * n)},
        compiler_params=pltpu.CompilerParams(has_side_effects=DATAFLOW))(*sums, *lands, sends, recvs, after)
    return list(out[:n]), list(out[n:])


def add_halves(grads, recvs, core, name):
    n = len(grads)
    nk = grads[0].shape[0]
    views = [g.reshape(nk, 2, g.shape[1] // 2, g.shape[2]) for g in grads]

    def body(core_ref, *refs):
        for g_ref, r_ref, o_ref in zip(refs[:n], refs[n:2 * n], refs[2 * n:]):
            o_ref[0] = (g_ref[0, 0].astype(F32) + r_ref[0].astype(F32)).astype(BF16)

    def rows(r):
        return (r.shape[1] // 2, r.shape[2])

    return pl.pallas_call(
        body, name=name,
        grid_spec=pltpu.PrefetchScalarGridSpec(
            num_scalar_prefetch=1, grid=(nk, 2),
            in_specs=[pl.BlockSpec((1, 1) + rows(r), lambda k, j, core_ref: (k, core_ref[0], j, 0)) for r in recvs]
            + [pl.BlockSpec((1,) + rows(r), lambda k, j, core_ref: (k, j, 0)) for r in recvs],
            out_specs=[pl.BlockSpec((1,) + rows(r), lambda k, j, core_ref: (k, j, 0)) for r in recvs]),
        out_shape=[jax.ShapeDtypeStruct(r.shape, BF16) for r in recvs],
        compiler_params=_params("parallel", "parallel"))(core, *views, *recvs)


def add_chips(sums, lands, chip, name):
    n = len(sums)

    def body(chip_ref, *refs):
        for s_ref, r_ref, o_ref in zip(refs[:n], refs[n:2 * n], refs[2 * n:]):
            o_ref[...] = ((s_ref[0].astype(F32) + r_ref[0].astype(F32)) + r_ref[1].astype(F32)) + r_ref[2].astype(F32)

    def rows(s):
        return (s.shape[1] // ROW_STEPS, s.shape[2])

    return pl.pallas_call(
        body, name=name,
        grid_spec=pltpu.PrefetchScalarGridSpec(
            num_scalar_prefetch=1, grid=(ROW_STEPS,),
            in_specs=[pl.BlockSpec((1,) + rows(s), lambda i, chip_ref: (chip_ref[0], i, 0)) for s in sums]
            + [pl.BlockSpec((3,) + rows(s), lambda i, chip_ref: (0, i, 0)) for s in sums],
            out_specs=[pl.BlockSpec(rows(s), lambda i, chip_ref: (i, 0)) for s in sums]),
        out_shape=[jax.ShapeDtypeStruct(s.shape[1:], F32) for s in sums], compiler_params=_params("parallel"))(chip, *sums, *lands)


def _adamw_math(w, g, m, v):
    m = ADAM_B1 * m + (1.0 - ADAM_B1) * g
    v = ADAM_B2 * v + (1.0 - ADAM_B2) * (g * g)
    m_hat = m / (1.0 - ADAM_B1 ** ADAM_STEP)
    v_hat = v / (1.0 - ADAM_B2 ** ADAM_STEP)
    return -ADAM_LR * (m_hat / (jnp.sqrt(v_hat) + ADAM_EPS) + ADAM_WD * w), m, v


def adamw_one_half(ws, gs, ms, vs, core, name, own, into=None):
    n = len(ws)
    r, cdim = ws[0].shape
    half = r // 2
    tr = _tile(half, max(8, 256 // n))
    steps = half // tr
    prior = () if into is None else tuple(a for group in into for a in group)

    def body(core_ref, *refs):
        ins, outs = refs[:4 * n], refs[4 * n + len(prior):]
        for a in range(n):
            w_ref, g_ref, m_ref, v_ref = ins[a::n]
            g = g_ref[...]
            outs[a][...] = g
            outs[n + a][...], outs[2 * n + a][...], outs[3 * n + a][...] = _adamw_math(w_ref[...], g, m_ref[...], v_ref[...])

    def rows(i, core_ref):
        return ((core_ref[0] if own else 1 - core_ref[0]) * steps + i, 0)

    whole = pl.BlockSpec((tr, cdim), rows)
    part = pl.BlockSpec((tr, cdim), lambda i, core_ref: (i, 0))
    out = pl.pallas_call(
        body, name=name,
        grid_spec=pltpu.PrefetchScalarGridSpec(
            num_scalar_prefetch=1, grid=(steps,),
            in_specs=[whole] * n + [part] * n + [whole] * (2 * n) + [ANY_SPEC] * len(prior), out_specs=[whole] * (4 * n)),
        out_shape=[jax.ShapeDtypeStruct((r, cdim), F32)] * (4 * n),
        input_output_aliases={1 + 4 * n + j: j for j in range(len(prior))},
        compiler_params=_params("parallel"))(core, *ws, *gs, *ms, *vs, *prior)
    return [out[k * n:(k + 1) * n] for k in range(4)]


def adamw_halves_sparsecore(ws, owns, others, ms, vs, name):
    n = len(ws)
    r, cdim = ws[0].shape
    half_groups = r // 2 // SC_ROWS
    per_tile = -(-2 * half_groups // SC_TILES)

    def body(*refs):
        ins, outs, (wb, gb, mb, vb) = refs[:5 * n], refs[5 * n:9 * n], refs[9 * n:]
        tile = lax.axis_index("sc_tile") * 2 + lax.axis_index("sc_core")
        core = lax.axis_index("c")

        @pl.loop(0, per_tile)
        def _(it):
            group = tile + SC_TILES * it

            @pl.when(group < 2 * half_groups)
            def _():
                rows = pl.ds(group * SC_ROWS, SC_ROWS)
                in_own = (group < half_groups) == (core == 0)
                half_rows = pl.ds((group % half_groups) * SC_ROWS, SC_ROWS)
                for a in range(n):
                    w_hbm, own_hbm, other_hbm, m_hbm, v_hbm = ins[a::n]
                    g_out, d_out, m_out, v_out = outs[a::n]
                    pltpu.sync_copy(w_hbm.at[rows], wb)
                    pltpu.sync_copy(m_hbm.at[rows], mb)
                    pltpu.sync_copy(v_hbm.at[rows], vb)

                    @pl.when(in_own)
                    def _():
                        pltpu.sync_copy(own_hbm.at[half_rows], gb)

                    @pl.when(jnp.logical_not(in_own))
                    def _():
                        pltpu.sync_copy(other_hbm.at[half_rows], gb)

                    pltpu.sync_copy(gb, g_out.at[rows])

                    @pl.loop(0, SC_ROWS)
                    def _(row):
                        @pl.loop(0, cdim, step=SC_LANES)
                        def _(col):
                            at = (row, pl.ds(col, SC_LANES))
                            wb[at], mb[at], vb[at] = _adamw_math(wb[at], gb[at], mb[at], vb[at])

                    pltpu.sync_copy(wb, d_out.at[rows])
                    pltpu.sync_copy(mb, m_out.at[rows])
                    pltpu.sync_copy(vb, v_out.at[rows])

    out = pl.kernel(
        body, name=name, out_type=[jax.ShapeDtypeStruct((r, cdim), F32)] * (4 * n),
        mesh=plsc.VectorSubcoreMesh(core_axis_name="sc_core", subcore_axis_name="sc_tile"),
        scratch_types=[pltpu.VMEM((SC_ROWS, cdim), F32)] * 4)(*ws, *owns, *others, *ms, *vs)
    return [out[a::n] for a in range(n)]


def small_allreduce_adamw(g, w, m, v):
    r, cdim = g.shape

    def body(g_ref, w_ref, m_ref, v_ref, go_ref, d_ref, mo_ref, vo_ref, pair, quad, d2d_send, d2d_recv, ici_send, ici_recv):
        x, y, c, chips = _place()
        me = 2 * x + y
        pair[c] = g_ref[...]
        swap = pltpu.make_async_remote_copy(
            src_ref=g_ref, dst_ref=pair.at[c], send_sem=d2d_send, recv_sem=d2d_recv, device_id=(x, y, 1 - c),
            device_id_type=MESH_IDS)
        swap.start()
        swap.wait()
        quad[me] = pair[0] + pair[1]
        copies = []
        for j, (cx, cy) in enumerate(chips):
            cp = pltpu.make_async_remote_copy(
                src_ref=quad.at[me], dst_ref=quad.at[me], send_sem=ici_send.at[j], recv_sem=ici_recv.at[j],
                device_id=(cx, cy, c), device_id_type=MESH_IDS)
            cp.start()
            copies.append(cp)
        for j, (cx, cy) in enumerate(chips):
            slot = quad.at[2 * cx + cy]
            pltpu.make_async_remote_copy(
                src_ref=slot, dst_ref=slot, send_sem=ici_send.at[j], recv_sem=ici_recv.at[j], device_id=(cx, cy, c),
                device_id_type=MESH_IDS).wait_recv()
        for cp in copies:
            cp.wait_send()
        total = (quad[0] + quad[1]) + (quad[2] + quad[3])
        go_ref[...] = total
        d_ref[...], mo_ref[...], vo_ref[...] = _adamw_math(w_ref[...], total, m_ref[...], v_ref[...])

    return pl.pallas_call(
        body, name="small_allreduce_adamw", in_specs=[VMEM_SPEC] * 4, out_specs=[VMEM_SPEC] * 4,
        out_shape=[jax.ShapeDtypeStruct((r, cdim), F32)] * 4,
        scratch_shapes=[pltpu.VMEM((2, r, cdim), F32), pltpu.VMEM((N_CHIPS, r, cdim), F32), pltpu.SemaphoreType.DMA,
                        pltpu.SemaphoreType.DMA, pltpu.SemaphoreType.DMA((3,)), pltpu.SemaphoreType.DMA((3,))],
        compiler_params=pltpu.CompilerParams(has_side_effects=True, vmem_limit_bytes=VMEM_LIMIT_V7X))(g, w, m, v)


GROUPS = (
    ("ffn1", ("ffn1_w_gate", "ffn1_w_up", "ffn1_w_down")),
    ("mixer", ("w_in", "pool_w", "w_out_a", "w_out_b", "w_o")),
    ("ffn2", ("ffn2_w_gate", "ffn2_w_up", "ffn2_w_down")),
    ("ple", ("ple_w_gate", "ple_w_proj")),
)
GATHERS = (
    ("ffn1_in", ("ffn1_w_gate", "ffn1_w_up")),
    ("ffn1_out", ("ffn1_w_down",)),
) + GROUPS[1:]
GAINS = ("ffn1_pre_g", "ffn1_post_g", "mix_pre_g", "sgu_norm_g", "pool_scale", "mix_post_g",
         "ffn2_pre_g", "ffn2_post_g", "ple_pre_g", "ple_post_g")
SMALL = GAINS + ("sgu_b", "sgu_w")
PACKED = SMALL + ("loss",)
WEIGHTS = ("ffn1_pre_g", "ffn1_w_gate", "ffn1_w_up", "ffn1_w_down", "ffn1_post_g", "mix_pre_g", "w_in", "sgu_norm_g",
           "sgu_w", "sgu_b", "pool_w", "pool_scale", "w_out_a", "w_out_b", "w_o", "mix_post_g", "ffn2_pre_g",
           "ffn2_w_gate", "ffn2_w_up", "ffn2_w_down", "ffn2_post_g", "ple_pre_g", "ple_w_gate", "ple_w_proj", "ple_post_g")
PACK_ROWS = 16


TRANSPOSED = ("ffn1_w_gate", "ffn1_w_up", "ffn2_w_gate", "ffn2_w_up")


def _shard2d(name, a):
    a = a[0]
    return a.T if name in TRANSPOSED else a.reshape(-1, a.shape[-1])


def _unshard2d(name, a2d, shape):
    return (a2d.T if name in TRANSPOSED else a2d).reshape(shape)


def pack_rows(gains, sgu_b, loss_tile, name):
    n = len(gains)
    d = gains[0].shape[1]
    g = sgu_b.shape[0]

    def body(*refs):
        o_ref = refs[-1]
        o_ref[...] = jnp.zeros_like(o_ref)
        for i in range(n):
            o_ref[i:i + 1, :] = refs[i][...]
        o_ref[n:n + g, 0:CHUNK] = refs[n][...]
        o_ref[n + g:n + g + 1, 0:CHUNK] = refs[n + 1][0:1, :]

    return pl.pallas_call(
        body, name=name, in_specs=[VMEM_SPEC] * (n + 2), out_specs=VMEM_SPEC,
        out_shape=jax.ShapeDtypeStruct((PACK_ROWS, d), F32))(*gains, sgu_b, loss_tile)


def _pack_small(parts, tag):
    d = parts[GAINS[0]].shape[-1]
    rows = pack_rows([parts[n] for n in GAINS], parts["sgu_b"].reshape(-1, CHUNK), parts["loss"], "pack_" + tag)
    return jnp.concatenate([rows, parts["sgu_w"].reshape(-1, d)], axis=0)


def _unpack_small(packed, like):
    n, g = len(GAINS), like["sgu_b"].size // CHUNK
    out = {name: packed[i:i + 1] for i, name in enumerate(GAINS)}
    out["sgu_b"] = packed[n:n + g, :CHUNK].reshape(like["sgu_b"].shape)
    out["loss"] = packed[n + g, 0]
    out["sgu_w"] = packed[PACK_ROWS:].reshape(like["sgu_w"].shape)
    return out


def _ffn_fwd(xn, h, w, pre, g_post, g_next, tag, between=None, first=None):
    a4, s4, t4 = ffn_gu(xn, w[pre + "w_gate"], w[pre + "w_up"], tag + "_gu") if first is None else first(xn)
    deps = ()
    if between is not None:
        more, deps = between(a4)
        w.update(more)
    f, h_new, xn_next = mm_norm_res(a4, w[pre + "w_down"], h, g_post, g_next, 0.5, tag + "_down", deps=deps)
    return dict(xn=xn, h=h, a4=a4, s4=s4, t4=t4, f=f), h_new, xn_next


def _ffn_bwd_w(dh, saved, w, pre, g_post, tag, deps):
    df, dg4, du4, d_post = ffn_bwd_a(dh, saved["f"], g_post, w[pre + "w_down"], saved["s4"], saved["t4"], 0.5, tag + "_bwd_a",
                                     deps=deps)
    grads, exchanging, token = _ffn_dw(saved, df, dg4, du4, pre, tag)
    return dg4, du4, grads, d_post, (exchanging, token)


def _ffn_dw(saved, df, dg4, du4, pre, tag):
    jobs = {pre + "w_down": (saved["a4"], df), pre + "w_gate": (dg4, saved["xn"]), pre + "w_up": (du4, saved["xn"])}
    grads, started, deps = {}, {}, ()
    for name, (x, dy) in jobs.items():
        grads[name] = dw_tn(x, dy, N_CHIPS, f"dw_{name}", x_kmajor=True, dy_mode="same", deps=deps)
        started[name], token = sibling_start([grads[name]], True, "exchange_start_" + name)
        deps = (token,)
    return grads, [started[pre + kind] for kind in ("w_gate", "w_up", "w_down")], token


def _ffn_bwd_x(dh, dg4, du4, saved, w, pre, g_pre, tag, deps):
    return dx_norm_bwd([(dg4, w[pre + "w_gate"], False), (du4, w[pre + "w_up"], False)], saved["h"], g_pre, dh,
                       tag + "_bwd_x", deps=deps)


def _start_gathers(weights, chip):
    first_tag, first_names = GATHERS[0]
    made, own = cast_place([_shard2d(n, weights[n]) for n in first_names], chip, "cast_" + first_tag, plain=True)
    sends, recvs, bufs, token = allgather_start(list(made), "allgather_start_" + first_tag)
    gathering = {first_tag: (sends, recvs, bufs, 0)}
    later = [n for _, names in GATHERS[1:] for n in names]
    cast = cast_place([_shard2d(n, weights[n]) for n in later], chip, "cast_later", deps=(token,))
    sends, recvs, bufs, token = allgather_start(list(cast), "allgather_start_later")
    first = 0
    for tag, names in GATHERS[1:]:
        gathering[tag] = (sends, recvs, bufs[first:first + len(names)], first)
        first += len(names)
    return gathering, token, dict(zip(first_names, own))


def _gathered(started, names, after, tag):
    sends, recvs, bufs, first = started
    landed = allgather_wait(sends, recvs, bufs, after, "allgather_wait_" + tag, first)
    return dict(zip(names, d2d_forward(landed, "d2d_forward_" + tag)))


def _forward_early(started, after, tag):
    sends, recvs, bufs, first = started
    landed = allgather_wait(sends, recvs, bufs, after, "allgather_wait_" + tag, first)
    return d2d_forward_start(landed, "d2d_forward_start_" + tag)


def _forwarded(forwarding, names, after, tag):
    return dict(zip(names, d2d_forward_wait(forwarding, after, "d2d_forward_wait_" + tag)))


def _exchange_start(names, big_g, tag):
    started, token = sibling_start([big_g[n] for n in names], True, "exchange_start_" + tag)
    return [started], token


def _scatter_begin(names, exchanging, after, core, tag):
    partial, from_sibling = [], []
    for e, started in enumerate(exchanging):
        mine, theirs = sibling_wait(started, True, after, f"exchange_wait_{tag}_{e}")
        partial += mine
        from_sibling += theirs
    chip_sums = list(add_halves(partial, from_sibling, core, "add_halves_" + tag))
    sends, recvs, sums, lands, token = scatter_start(chip_sums, "scatter_start_" + tag)
    return (names, sends, recvs, sums, lands), token


def _share_begin(scattering, after, chip, tag):
    names, sends, recvs, sums, lands = scattering
    sums, lands = scatter_wait(sends, recvs, sums, lands, after, "scatter_wait_" + tag)
    reduced = list(add_chips(sums, lands, chip, "add_chips_" + tag))
    sharing, token = sibling_start(reduced, False, "share_start_" + tag)
    return (names, sharing), token


def _update(shared, after, weights, moments_m, moments_v, tag, results):
    names, sharing = shared
    reduced, others = sibling_wait(sharing, False, after, "share_wait_" + tag)
    for b, idx in enumerate(_by_shape(reduced)):
        in_bucket = [names[i] for i in idx]
        outs = adamw_halves_sparsecore(
            [_shard2d(n, weights[n]) for n in in_bucket], [reduced[i] for i in idx], [others[i] for i in idx],
            [_shard2d(n, moments_m[n]) for n in in_bucket], [_shard2d(n, moments_v[n]) for n in in_bucket], f"adamw_{tag}_{b}")
        for n, per_weight in zip(in_bucket, outs):
            for store, value in zip(results, per_weight):
                store[n] = _unshard2d(n, value, weights[n].shape)


def _update_last(scattering, after, weights, moments_m, moments_v, core, chip, tag, results):
    names, sends, recvs, sums, lands = scattering
    sums, lands = scatter_wait(sends, recvs, sums, lands, after, "scatter_wait_" + tag)
    reduced = list(add_chips(sums, lands, chip, "add_chips_" + tag))
    (sends, recvs, reduced, zones), _ = sibling_start(reduced, False, "share_start_" + tag)
    state = [[_shard2d(n, held[n]) for n in names] for held in (weights, moments_m, moments_v)]
    first = adamw_one_half(state[0], reduced, state[1], state[2], core, f"adamw_{tag}_own", own=True)
    _, others = sibling_wait((sends, recvs, reduced, zones), False, first[1][-1], "share_wait_" + tag)
    outs = adamw_one_half(state[0], others, state[1], state[2], core, f"adamw_{tag}_other", own=False, into=first)
    for store, values in zip(results, outs):
        for n, value in zip(names, values):
            store[n] = _unshard2d(n, value, weights[n].shape)


def kernel(x, p, ffn1_pre_g, ffn1_w_gate, ffn1_w_up, ffn1_w_down, ffn1_post_g, mix_pre_g, w_in, sgu_norm_g, sgu_w, sgu_b, pool_w, pool_scale, w_out_a, w_out_b, w_o, mix_post_g, ffn2_pre_g, ffn2_w_gate, ffn2_w_up, ffn2_w_down, ffn2_post_g, ple_pre_g, ple_w_gate, ple_w_proj, ple_post_g, loss_target, m_ffn1_pre_g, m_ffn1_w_gate, m_ffn1_w_up, m_ffn1_w_down, m_ffn1_post_g, m_mix_pre_g, m_w_in, m_sgu_norm_g, m_sgu_w, m_sgu_b, m_pool_w, m_pool_scale, m_w_out_a, m_w_out_b, m_w_o, m_mix_post_g, m_ffn2_pre_g, m_ffn2_w_gate, m_ffn2_w_up, m_ffn2_w_down, m_ffn2_post_g, m_ple_pre_g, m_ple_w_gate, m_ple_w_proj, m_ple_post_g, v_ffn1_pre_g, v_ffn1_w_gate, v_ffn1_w_up, v_ffn1_w_down, v_ffn1_post_g, v_mix_pre_g, v_w_in, v_sgu_norm_g, v_sgu_w, v_sgu_b, v_pool_w, v_pool_scale, v_w_out_a, v_w_out_b, v_w_o, v_mix_post_g, v_ffn2_pre_g, v_ffn2_w_gate, v_ffn2_w_up, v_ffn2_w_down, v_ffn2_post_g, v_ple_pre_g, v_ple_w_gate, v_ple_w_proj, v_ple_post_g):
    given = dict(locals())
    weights = {n: given[n] for n in WEIGHTS}
    moments_m = {n: given["m_" + n] for n in WEIGHTS}
    moments_v = {n: given["v_" + n] for n in WEIGHTS}
    core = lax.axis_index("c").astype(jnp.int32).reshape(1)
    chip = (2 * lax.axis_index("x") + lax.axis_index("y")).astype(jnp.int32).reshape(1)

    d = x.shape[-1]
    gathering, last_start, own_first = _start_gathers(weights, chip)
    token = (last_start,)
    gain = {n: weights[n] for n in GAINS}
    sgu_w3 = sgu_w[0]
    sgu_b3 = sgu_b[0][:, :, None]
    groups = dict(GROUPS + GATHERS)

    h0 = x[0]
    tgt = loss_target[0]
    p_bf = p[0, 0].astype(BF16)
    xn1 = rms_cast(h0, gain["ffn1_pre_g"], "ffn1_pre_norm")
    w = {}

    def ffn1_first(xn):
        own = ffn_gu(xn, own_first["ffn1_w_gate"][None], own_first["ffn1_w_up"][None], "ffn1_gu_own", chip=chip)
        w.update(_gathered(gathering["ffn1_in"], groups["ffn1_in"], (own[0], last_start), "ffn1_in"))
        return ffn_gu(xn, w["ffn1_w_gate"], w["ffn1_w_up"], "ffn1_gu", chip=chip, into=own)

    s1, h1, xn2 = _ffn_fwd(xn1, h0, w, "ffn1_", gain["ffn1_post_g"], gain["mix_pre_g"], "ffn1",
                           lambda a4: (_gathered(gathering["ffn1_out"], groups["ffn1_out"], a4, "ffn1_out"), ()), ffn1_first)
    w.update(_gathered(gathering["mixer"], groups["mixer"], h1, "mixer"))
    full = {n: w[n].reshape(-1, d) for n in ("w_out_a", "w_out_b", "w_o")}
    n_groups = pool_w.shape[1]
    rows_per = pool_w.shape[2]
    dgp = pool_w.shape[3]
    pool_full = w["pool_w"].reshape(N_CHIPS, n_groups, rows_per, dgp).transpose(1, 0, 2, 3).reshape(n_groups, N_CHIPS * rows_per, dgp)
    z = mixer_in(xn2, w["w_in"], "mixer_in")
    a = sgu_fwd(z, gain["sgu_norm_g"], sgu_w3, sgu_b3, "sgu_fwd")
    diff, b = pool_fwd(z, pool_full, gain["pool_scale"], "pool_fwd")
    ya, yb, y = mixer_y(a, b, z, full["w_out_a"], full["w_out_b"], "mixer_y")
    forwarding, tok = _forward_early(gathering["ffn2"], y, "ffn2")
    m, h2, xn3 = mm_norm_res(y[None], full["w_o"][None], h1, gain["mix_post_g"], gain["ffn2_pre_g"], 1.0, "mixer_out",
                             deps=(tok,))
    w.update(_forwarded(forwarding, groups["ffn2"], h2, "ffn2"))
    early = {}

    def forward_ple(a4):
        early["ple"], tok_ple = _forward_early(gathering["ple"], a4, "ple")
        return {}, (tok_ple,)

    s2, h3, xn4 = _ffn_fwd(xn3, h2, w, "ffn2_", gain["ffn2_post_g"], gain["ple_pre_g"], "ffn2", forward_ple)
    w.update(_forwarded(early["ple"], groups["ple"], h3, "ple"))
    full["ple_w_gate"] = w["ple_w_gate"].reshape(-1, d)
    proj_full = w["ple_w_proj"].transpose(1, 0, 2).reshape(ple_w_proj.shape[1], -1)

    small_g, big_g = {}, {}
    ds, de, dh3, small_g["ple_post_g"], small_g["ple_pre_g"], loss_part = ple_loss(
        xn4, p_bf, full["ple_w_gate"], proj_full, h3, gain["ple_post_g"], gain["ple_pre_g"], tgt, "ple_loss")
    small_g["loss"] = loss_part
    grad, delta, new_m, new_v = {}, {}, {}, {}
    results = (grad, delta, new_m, new_v)
    update = functools.partial(_update, weights=weights, moments_m=moments_m, moments_v=moments_v, results=results)

    def reduce_behind(tag, previous, started=None):
        exchanging, tok = _exchange_start(groups[tag], big_g, tag) if started is None else started
        if previous is not None:
            shared, tok = _share_begin(previous[1], tok, chip, previous[0])
        scattering, tok_scatter = _scatter_begin(groups[tag], exchanging, tok, core, tag)
        if previous is not None:
            update(shared, tok_scatter, tag=previous[0])
        return (tag, scattering), (tok_scatter,)

    big_g["ple_w_gate"] = dw_tn(xn4, ds, 1, "dw_ple_gate").reshape(N_CHIPS, -1, d)
    big_g["ple_w_proj"] = dw_tn(p_bf, de, N_CHIPS, "dw_ple_proj", dy_mode="cols")
    reducing, deps = reduce_behind("ple", None)

    dg4, du4, g2, small_g["ffn2_post_g"], exchanging = _ffn_bwd_w(dh3, s2, w, "ffn2_", gain["ffn2_post_g"], "ffn2", deps)
    big_g.update(g2)
    dh2, small_g["ffn2_pre_g"] = _ffn_bwd_x(dh3, dg4, du4, s2, w, "ffn2_", gain["ffn2_pre_g"], "ffn2", ())
    reducing, deps = reduce_behind("ffn2", reducing, exchanging)

    dm, dya, dyb, dz, da, db, small_g["mix_post_g"] = mixer_bwd_y(
        dh2, m, gain["mix_post_g"], full["w_o"], ya, yb, z, full["w_out_a"], full["w_out_b"], "mixer_bwd_y", deps=deps)
    big_g["w_o"] = dw_tn(y, dm, 1, "dw_o").reshape(N_CHIPS, -1, d)
    big_g["w_out_a"] = dw_tn(a, dya, 1, "dw_out_a").reshape(N_CHIPS, -1, d)
    big_g["w_out_b"] = dw_tn(b, dyb, 1, "dw_out_b").reshape(N_CHIPS, -1, d)
    dz, d_sgu_w, d_sgu_b, small_g["sgu_norm_g"] = sgu_bwd(z, da, dz, gain["sgu_norm_g"], sgu_w3, sgu_b3, "sgu_bwd")
    dz, d_pool_w, small_g["pool_scale"] = pool_bwd(db, diff, dz, pool_full, gain["pool_scale"], "pool_bwd")
    big_g["pool_w"] = d_pool_w.astype(BF16).reshape(n_groups, N_CHIPS, rows_per, dgp).transpose(1, 0, 2, 3).reshape(
        N_CHIPS, n_groups * rows_per, dgp)
    big_g["w_in"] = dw_in_tiles(xn2, dz, N_CHIPS, "dw_in")
    dh1, small_g["mix_pre_g"] = dx_norm_bwd([(dz, w["w_in"], True)], h1, gain["mix_pre_g"], dh2, "mixer_bwd_x")
    reducing, deps = reduce_behind("mixer", reducing)

    dg4, du4, g1, small_g["ffn1_post_g"], exchanging = _ffn_bwd_w(dh1, s1, w, "ffn1_", gain["ffn1_post_g"], "ffn1", deps)
    big_g.update(g1)
    reducing, deps = reduce_behind("ffn1", reducing, exchanging)
    dh0, small_g["ffn1_pre_g"] = _ffn_bwd_x(dh1, dg4, du4, s1, w, "ffn1_", gain["ffn1_pre_g"], "ffn1", deps)
    small_g["sgu_w"] = d_sgu_w
    small_g["sgu_b"] = d_sgu_b[:, :, 0]

    no_state = {"loss": jnp.zeros((8, 128), F32)}
    packed = small_allreduce_adamw(
        _pack_small(small_g, "grads"), _pack_small({n: weights[n] for n in SMALL} | no_state, "weights"),
        _pack_small({n: moments_m[n] for n in SMALL} | no_state, "m"), _pack_small({n: moments_v[n] for n in SMALL} | no_state, "v"))
    like = {n: weights[n] for n in SMALL}
    for store, block in zip((grad, delta, new_m, new_v), packed):
        store.update(_unpack_small(block, like))
    _update_last(reducing[1], packed[0], weights, moments_m, moments_v, core, chip, "ffn1", results)

    return (grad["loss"], dh0[None], *[grad[n] for n in WEIGHTS], *[delta[n] for n in WEIGHTS],
            *[new_m[n] for n in WEIGHTS], *[new_v[n] for n in WEIGHTS])
```

```python
import functools

import jax
import jax.numpy as jnp
from jax import lax
from jax.experimental import pallas as pl
from jax.experimental.pallas import tpu as pltpu
from jax.experimental.pallas import tpu_sc as plsc

F32 = jnp.float32
BF16 = jnp.bfloat16
EPS = 1e-6
CHUNK = 128
POOL_WINDOWS = (2, 4, 8, 16)
HALO = 16
N_CHIPS = 4
ADAM_LR, ADAM_B1, ADAM_B2, ADAM_EPS, ADAM_WD, ADAM_STEP = 0.001, 0.9, 0.999, 1e-08, 0.01, 10
VMEM_LIMIT_V7X = 58 * 1024 * 1024
MESH_IDS = pl.DeviceIdType.MESH
HBM_SPEC = pl.BlockSpec(memory_space=pltpu.HBM)
VMEM_SPEC = pl.BlockSpec(memory_space=pltpu.VMEM)
SEM_SPEC = pl.BlockSpec(memory_space=pltpu.SEMAPHORE)
ANY_SPEC = pl.BlockSpec(memory_space=pl.ANY)
DATAFLOW = pltpu.SideEffectType.DATAFLOW_SIDE_EFFECTING
OTHER_CHIPS = ((1, 0), (0, 1), (1, 1))
DZ_SLOT = (2, 3, 4, 0, 1)
SC_TILES, SC_LANES, SC_ROWS = 32, 16, 8
ROW_STEPS = 4
SIBLING_BARRIER = 1
CHIPS_BARRIERS = (2, 3)
DW_TOKENS = 4096
DW_IN_TILE = 256

NT = (((1,), (1,)), ((), ()))
TN = (((0,), (0,)), ((), ()))


def _params(*sem, **more):
    return pltpu.CompilerParams(dimension_semantics=sem or None, vmem_limit_bytes=VMEM_LIMIT_V7X, **more)


def _tile(t, want):
    return max(c for c in range(8, min(t, want) + 1, 8) if t % c == 0)


def _const(shape):
    return pl.BlockSpec(shape, lambda *_: (0,) * len(shape))


def _rows(tm, d, col=0):
    return pl.BlockSpec((tm, d), lambda i: (i, col))


def _kmajor(nk, tm, kb):
    return pl.BlockSpec((nk, tm, kb), lambda i: (0, i, 0))


def _dot(a, b):
    return jnp.dot(a, b, preferred_element_type=F32)


def _dot_nt(a, b):
    return lax.dot_general(a, b, NT, preferred_element_type=F32)


def _dot_tn(a, b):
    return lax.dot_general(a, b, TN, preferred_element_type=F32)


def _gelu(x):
    return 0.5 * x * (1.0 + jnp.tanh(0.7978845608028654 * (x + 0.044715 * x * x * x)))


def _gelu_and_grad(x):
    k, kc = 0.7978845608028654, 0.7978845608028654 * 0.044715
    x2 = x * x
    th = jnp.tanh(x * (k + kc * x2))
    cdf = 0.5 + 0.5 * th
    return x * cdf, cdf + x * (0.5 - 0.5 * th * th) * (k + 3.0 * kc * x2)


def _sigmoid(x):
    return 1.0 / (1.0 + jnp.exp(-x))


def _rstd(h):
    return lax.rsqrt(jnp.mean(h * h, axis=-1, keepdims=True) + EPS)


def _rms_bwd(h, g, dy):
    r = _rstd(h)
    t = dy * g
    dh = r * t - h * (r * r * r) * jnp.mean(h * t, axis=-1, keepdims=True)
    return dh, jnp.sum(dy * h * r, axis=0, keepdims=True)


def _ordered_after(body, n_in, deps):
    if not deps:
        return body
    return lambda *refs: body(*refs[:n_in], *refs[n_in + len(deps):])


def _accumulate(ref, value):
    @pl.when(pl.program_id(0) == 0)
    def _():
        ref[...] = jnp.zeros_like(ref)

    ref[...] += value


def rms_cast(h, g, name):
    t, d = h.shape
    tm = _tile(t, 512)

    def body(h_ref, g_ref, o_ref):
        hv = h_ref[...]
        o_ref[...] = (hv * _rstd(hv) * g_ref[...]).astype(BF16)

    return pl.pallas_call(
        body, name=name, grid=(t // tm,), in_specs=[_rows(tm, d), _const((1, d))], out_specs=_rows(tm, d),
        out_shape=jax.ShapeDtypeStruct((t, d), BF16), compiler_params=_params("parallel"))(h, g)


def ffn_gu(xn, wgt, wut, name, chip=None, into=None):
    t, d = xn.shape
    nkw, fk, _ = wgt.shape
    tm = _tile(t, 512)
    count = N_CHIPS if chip is None else (1 if into is None else N_CHIPS - 1)
    first = 0 if into is None else 1

    def body(*refs):
        x_ref, wg_ref, wu_ref = refs[-6 - (0 if into is None else 3):][:3]
        a_ref, s_ref, t_ref = refs[-3:]
        xv = x_ref[...]
        g = _dot_nt(xv, wg_ref[0])
        sg = _sigmoid(g)
        s = g * sg
        s_ref[0] = s.astype(BF16)
        ds = sg * (1.0 + g * (1.0 - sg))
        u = _dot_nt(xv, wu_ref[0])
        a_ref[0] = (s * u).astype(BF16)
        t_ref[0] = (u * ds).astype(BF16)

    def slot(k, *chip_ref):
        return (chip_ref[0][0] + first + k) % N_CHIPS if chip_ref else k

    w_spec = pl.BlockSpec((1, fk, d), lambda k, i, *c: (slot(k, *c) if nkw > 1 else 0, 0, 0))
    o_spec = pl.BlockSpec((1, tm, fk), lambda k, i, *c: (slot(k, *c), i, 0))
    shape = jax.ShapeDtypeStruct((N_CHIPS, t, fk), BF16)
    prior = () if into is None else tuple(into)
    call = pl.pallas_call(
        body, name=name,
        grid_spec=pltpu.PrefetchScalarGridSpec(
            num_scalar_prefetch=0 if chip is None else 1, grid=(count, t // tm),
            in_specs=[pl.BlockSpec((tm, d), lambda k, i, *c: (i, 0)), w_spec, w_spec] + [ANY_SPEC] * len(prior),
            out_specs=[o_spec] * 3),
        out_shape=[shape] * 3, input_output_aliases={4 + j: j for j in range(len(prior))},
        compiler_params=_params("parallel", "parallel"))
    return call(*(() if chip is None else (chip,)), xn, wgt, wut, *prior)


def mm_norm_res(a3, w3, h_old, g_post, g_next, scale, name, deps=()):
    nk, t, kb = a3.shape
    d = w3.shape[2]
    tm = _tile(t, 256)

    def body(a_ref, w_ref, h_ref, gp_ref, gn_ref, f_ref, hn_ref, xn_ref):
        f = _dot(a_ref[0], w_ref[0])
        for k in range(1, nk):
            f += _dot(a_ref[k], w_ref[k])
        f_ref[...] = f.astype(BF16)
        hn = h_ref[...] + scale * (f * _rstd(f) * gp_ref[...])
        hn_ref[...] = hn
        xn_ref[...] = (hn * _rstd(hn) * gn_ref[...]).astype(BF16)

    return pl.pallas_call(
        _ordered_after(body, 5, deps), name=name, grid=(t // tm,),
        in_specs=[_kmajor(nk, tm, kb), _const((nk, kb, d)), _rows(tm, d), _const((1, d)), _const((1, d))] + [ANY_SPEC] * len(deps),
        out_specs=[_rows(tm, d)] * 3,
        out_shape=[jax.ShapeDtypeStruct((t, d), BF16), jax.ShapeDtypeStruct((t, d), F32), jax.ShapeDtypeStruct((t, d), BF16)],
        compiler_params=_params("parallel"))(a3, w3, h_old, g_post, g_next, *deps)


def mixer_in(xn, win4, name):
    t, d = xn.shape
    nk, _, nb = win4.shape
    tm = _tile(t, 512)

    def body(x_ref, w_ref, z_ref):
        z_ref[...] = _dot(x_ref[...], w_ref[0]).astype(BF16)

    return pl.pallas_call(
        body, name=name, grid=(nk, t // tm),
        in_specs=[pl.BlockSpec((tm, d), lambda k, i: (i, 0)), pl.BlockSpec((1, d, nb), lambda k, i: (k, 0, 0))],
        out_specs=pl.BlockSpec((tm, nb), lambda k, i: (i, k)), out_shape=jax.ShapeDtypeStruct((t, nk * nb), BF16),
        compiler_params=_params("parallel", "parallel"))(xn, win4)


def _causal_mask():
    row = lax.broadcasted_iota(jnp.int32, (CHUNK, CHUNK), 0)
    col = lax.broadcasted_iota(jnp.int32, (CHUNK, CHUNK), 1)
    return row >= col


def _layernorm_parts(v):
    mu = jnp.mean(v, axis=-1, keepdims=True)
    vc = v - mu
    r = lax.rsqrt(jnp.mean(vc * vc, axis=-1, keepdims=True) + EPS)
    return vc * r, r


def sgu_fwd(z, norm_g, sgu_w, sgu_b3, name):
    t = z.shape[0]
    d = norm_g.shape[1]
    ng = sgu_w.shape[0]
    dg = d // ng
    tm = _tile(t, 256)

    def body(zu_ref, zv_ref, ng_ref, w_ref, b_ref, a_ref):
        vhat, _ = _layernorm_parts(_gelu(zv_ref[...].astype(F32)))
        vn = (vhat * ng_ref[...]).astype(BF16)
        u = _gelu(zu_ref[...].astype(F32))
        mask = _causal_mask()
        for g in range(ng):
            wg = jnp.where(mask, w_ref[g], 0.0).astype(BF16)
            for ci in range(tm // CHUNK):
                rs, cs = slice(ci * CHUNK, (ci + 1) * CHUNK), slice(g * dg, (g + 1) * dg)
                sv = _dot(wg, vn[rs, cs]) + b_ref[g]
                a_ref[rs, cs] = (u[rs, cs] * sv).astype(BF16)

    return pl.pallas_call(
        body, name=name, grid=(t // tm,),
        in_specs=[_rows(tm, d, 0), _rows(tm, d, 1), _const((1, d)), _const((ng, CHUNK, CHUNK)), _const((ng, CHUNK, 1))],
        out_specs=_rows(tm, d), out_shape=jax.ShapeDtypeStruct((t, d), BF16),
        compiler_params=_params("parallel"))(z, z, norm_g, sgu_w, sgu_b3)


def pool_fwd(z, pool_w, pool_scale, name):
    t = z.shape[0]
    d = pool_scale.shape[1]
    ng = pool_w.shape[0]
    dg = d // ng
    tm = _tile(t, 256)
    per = tm // HALO

    def body(c_ref, prev_ref, w_ref, s_ref, diff_ref, b_ref):
        i = pl.program_id(0)
        cur = c_ref[...].astype(F32)
        prev = jnp.where(i > 0, prev_ref[...].astype(F32), 0.0)
        ext = jnp.concatenate([prev, cur], axis=0)
        tok = i * tm + lax.broadcasted_iota(jnp.int32, (tm, 1), 0)
        for g, win in enumerate(POOL_WINDOWS):
            cs = slice(g * dg, (g + 1) * dg)
            s = ext[:, cs]
            sh = 1
            while sh < win:
                s = s + pltpu.roll(s, sh, 0)
                sh *= 2
            per_count = 1.0 / jnp.minimum(tok + 1, win).astype(F32)
            diff = (s[HALO:] * per_count - cur[:, cs]).astype(BF16)
            diff_ref[:, cs] = diff
            b_ref[:, cs] = (_dot(diff, w_ref[g]) * s_ref[:, cs]).astype(BF16)

    return pl.pallas_call(
        body, name=name, grid=(t // tm,),
        in_specs=[_rows(tm, d, 2), pl.BlockSpec((HALO, d), lambda i: (jnp.maximum(i * per - 1, 0), 2)),
                  _const((ng, dg, dg)), _const((1, d))],
        out_specs=[_rows(tm, d)] * 2, out_shape=[jax.ShapeDtypeStruct((t, d), BF16)] * 2,
        compiler_params=_params("parallel"))(z, z, pool_w, pool_scale)


def mixer_y(a, b, z, woa, wob, name):
    t, d = a.shape
    tm = _tile(t, 256)

    def body(a_ref, b_ref, ga_ref, gb_ref, wa_ref, wb_ref, ya_ref, yb_ref, y_ref):
        ya = _dot(a_ref[...], wa_ref[...])
        yb = _dot(b_ref[...], wb_ref[...])
        ya_ref[...] = ya.astype(BF16)
        yb_ref[...] = yb.astype(BF16)
        y_ref[...] = (_sigmoid(ga_ref[...].astype(F32)) * ya + _sigmoid(gb_ref[...].astype(F32)) * yb).astype(BF16)

    return pl.pallas_call(
        body, name=name, grid=(t // tm,),
        in_specs=[_rows(tm, d), _rows(tm, d), _rows(tm, d, 3), _rows(tm, d, 4), _const((d, d)), _const((d, d))],
        out_specs=[_rows(tm, d)] * 3, out_shape=[jax.ShapeDtypeStruct((t, d), BF16)] * 3,
        compiler_params=_params("parallel"))(a, b, z, z, woa, wob)


def ple_loss(xn, p, wpg, wpp, h, g_post, g_pre, target, name):
    t, d = xn.shape
    dp = p.shape[1]
    tm = _tile(t, 256)

    def body(x_ref, p_ref, wg_ref, wp_ref, h_ref, gp_ref, gn_ref, tg_ref, ds_ref, de_ref, dhp_ref, dgp_ref, dgn_ref, loss_ref):
        gate = _sigmoid(_dot(x_ref[...], wg_ref[...]))
        e = _dot(p_ref[...], wp_ref[...])
        q = gate * e
        hv = h_ref[...]
        err = hv + q * _rstd(q) * gp_ref[...] - tg_ref[...]
        _accumulate(loss_ref, jnp.full(loss_ref.shape, (0.5 / d) * jnp.sum(err * err), F32))
        dhv = err * (1.0 / d)
        dq, dgp = _rms_bwd(q, gp_ref[...], dhv)
        ds = (dq * e * gate * (1.0 - gate)).astype(BF16)
        ds_ref[...] = ds
        de_ref[...] = (dq * gate).astype(BF16)
        dx, dgn = _rms_bwd(hv, gn_ref[...], _dot_nt(ds, wg_ref[...]))
        dhp_ref[...] = dhv + dx
        _accumulate(dgp_ref, dgp)
        _accumulate(dgn_ref, dgn)

    return pl.pallas_call(
        body, name=name, grid=(t // tm,),
        in_specs=[_rows(tm, d), _rows(tm, dp), _const((d, d)), _const((dp, d)), _rows(tm, d), _const((1, d)), _const((1, d)),
                  _rows(tm, d)],
        out_specs=[_rows(tm, d)] * 3 + [_const((1, d))] * 2 + [_const((8, 128))],
        out_shape=[jax.ShapeDtypeStruct((t, d), BF16), jax.ShapeDtypeStruct((t, d), BF16), jax.ShapeDtypeStruct((t, d), F32),
                   jax.ShapeDtypeStruct((1, d), F32), jax.ShapeDtypeStruct((1, d), F32), jax.ShapeDtypeStruct((8, 128), F32)],
        compiler_params=_params("arbitrary"))(xn, p, wpg, wpp, h, g_post, g_pre, target)


def ffn_bwd_a(dh, f, g_post, wd4, s4, t4, scale, name, deps=()):
    t, d = dh.shape
    nk, fk, _ = wd4.shape
    tm = _tile(t, 256)

    def body(dh_ref, f_ref, gp_ref, w_ref, s_ref, t_ref, df_ref, dg_ref, du_ref, dgp_ref):
        df, dgp = _rms_bwd(f_ref[...].astype(F32), gp_ref[...], dh_ref[...])
        df = (scale * df).astype(BF16)
        df_ref[...] = df
        _accumulate(dgp_ref, scale * dgp)
        for k in range(nk):
            da = _dot_nt(df, w_ref[k])
            du_ref[k] = (da * s_ref[k].astype(F32)).astype(BF16)
            dg_ref[k] = (da * t_ref[k].astype(F32)).astype(BF16)

    return pl.pallas_call(
        _ordered_after(body, 6, deps), name=name, grid=(t // tm,),
        in_specs=[_rows(tm, d), _rows(tm, d), _const((1, d)), _const((nk, fk, d)), _kmajor(nk, tm, fk), _kmajor(nk, tm, fk)]
        + [ANY_SPEC] * len(deps),
        out_specs=[_rows(tm, d), _kmajor(nk, tm, fk), _kmajor(nk, tm, fk), _const((1, d))],
        out_shape=[jax.ShapeDtypeStruct((t, d), BF16), jax.ShapeDtypeStruct((nk, t, fk), BF16),
                   jax.ShapeDtypeStruct((nk, t, fk), BF16), jax.ShapeDtypeStruct((1, d), F32)],
        compiler_params=_params("arbitrary"))(dh, f, g_post, wd4, s4, t4, *deps)


def dx_norm_bwd(pairs, h, g_pre, dh_in, name, deps=()):
    t, d = h.shape
    tm = _tile(t, 256)
    n = len(pairs)

    def body(*refs):
        dys, ws = refs[:n], refs[n:2 * n]
        h_ref, g_ref, dhi_ref, dho_ref, dg_ref = refs[2 * n:]
        acc = None
        for (_, w4, sections), dy_ref, w_ref in zip(pairs, dys, ws):
            if sections:
                wide = w4.shape[2]
                edges = sorted(set(range(0, 5 * d + 1, d)) | set(range(0, 5 * d + 1, wide)))
                parts = [_dot_nt(dy_ref[DZ_SLOT[lo // d], :, lo % d:lo % d + hi - lo], w_ref[lo // wide, :, lo % wide:lo % wide + hi - lo])
                         for lo, hi in zip(edges[:-1], edges[1:])]
            else:
                parts = [_dot(dy_ref[k], w_ref[k]) for k in range(w4.shape[0])]
            for part in parts:
                acc = part if acc is None else acc + part
        dx, dg = _rms_bwd(h_ref[...], g_ref[...], acc)
        dho_ref[...] = dhi_ref[...] + dx
        _accumulate(dg_ref, dg)

    dy_specs = [_kmajor(dy.shape[0], tm, dy.shape[2]) for dy, _, _ in pairs]
    return pl.pallas_call(
        _ordered_after(body, 2 * n + 3, deps), name=name, grid=(t // tm,),
        in_specs=dy_specs + [_const(w4.shape) for _, w4, _ in pairs] + [_rows(tm, d), _const((1, d)), _rows(tm, d)]
        + [ANY_SPEC] * len(deps),
        out_specs=[_rows(tm, d), _const((1, d))],
        out_shape=[jax.ShapeDtypeStruct((t, d), F32), jax.ShapeDtypeStruct((1, d), F32)],
        compiler_params=_params("arbitrary"))(*[dy for dy, _, _ in pairs], *[w4 for _, w4, _ in pairs], h, g_pre, dh_in, *deps)


def mixer_bwd_y(dh, m, g_post, w_o, ya, yb, z, woa, wob, name, deps=()):
    t, d = dh.shape
    tm = _tile(t, 256)

    def body(dh_ref, m_ref, gp_ref, wo_ref, ya_ref, yb_ref, ga_ref, gb_ref, wa_ref, wb_ref,
             dm_ref, dya_ref, dyb_ref, dz_ref, da_ref, db_ref, dgp_ref):
        dm, dgp = _rms_bwd(m_ref[...].astype(F32), gp_ref[...], dh_ref[...])
        dm = dm.astype(BF16)
        dm_ref[...] = dm
        _accumulate(dgp_ref, dgp)
        dy = _dot_nt(dm, wo_ref[...])
        sa = _sigmoid(ga_ref[...].astype(F32))
        sb = _sigmoid(gb_ref[...].astype(F32))
        dya = (dy * sa).astype(BF16)
        dyb = (dy * sb).astype(BF16)
        dya_ref[...] = dya
        dyb_ref[...] = dyb
        dz_ref[0] = (dy * ya_ref[...].astype(F32) * sa * (1.0 - sa)).astype(BF16)
        dz_ref[1] = (dy * yb_ref[...].astype(F32) * sb * (1.0 - sb)).astype(BF16)
        da_ref[...] = _dot_nt(dya, wa_ref[...]).astype(BF16)
        db_ref[...] = _dot_nt(dyb, wb_ref[...]).astype(BF16)

    return pl.pallas_call(
        _ordered_after(body, 10, deps), name=name, grid=(t // tm,),
        in_specs=[_rows(tm, d), _rows(tm, d), _const((1, d)), _const((d, d)), _rows(tm, d), _rows(tm, d),
                  _rows(tm, d, 3), _rows(tm, d, 4), _const((d, d)), _const((d, d))] + [ANY_SPEC] * len(deps),
        out_specs=[_rows(tm, d)] * 3 + [pl.BlockSpec((2, tm, d), lambda i: (0, i, 0))] + [_rows(tm, d)] * 2 + [_const((1, d))],
        out_shape=[jax.ShapeDtypeStruct((t, d), BF16)] * 3 + [jax.ShapeDtypeStruct((5, t, d), BF16)]
        + [jax.ShapeDtypeStruct((t, d), BF16)] * 2 + [jax.ShapeDtypeStruct((1, d), F32)],
        compiler_params=_params("arbitrary"))(dh, m, g_post, w_o, ya, yb, z, z, woa, wob, *deps)


def sgu_bwd(z, da, dz, norm_g, sgu_w, sgu_b3, name):
    t, d = da.shape
    ng = sgu_w.shape[0]
    dg = d // ng
    tm = _tile(t, 256)
    steps = t // tm

    def body(zu_ref, zv_ref, da_ref, ng_ref, w_ref, b_ref, _, dz_ref, dw_ref, db_ref, dng_ref, dvn_ref, dsv_ref):
        i = pl.program_id(0)
        zv = zv_ref[...].astype(F32)
        zu = zu_ref[...].astype(F32)
        v, gv = _gelu_and_grad(zv)
        vhat, r = _layernorm_parts(v)
        gain = ng_ref[...]
        vn = (vhat * gain).astype(BF16)
        u, gu = _gelu_and_grad(zu)
        dav = da_ref[...].astype(F32)
        mask = _causal_mask()

        @pl.when(i == 0)
        def _():
            dw_ref[...] = jnp.zeros_like(dw_ref)
            dsv_ref[...] = jnp.zeros_like(dsv_ref)

        for g in range(ng):
            wg = jnp.where(mask, w_ref[g], 0.0).astype(BF16)
            dw = jnp.zeros((CHUNK, CHUNK), F32)
            dsv_sum = jnp.zeros((CHUNK, dg), F32)
            for ci in range(tm // CHUNK):
                rs, cs = slice(ci * CHUNK, (ci + 1) * CHUNK), slice(g * dg, (g + 1) * dg)
                vn_blk = vn[rs, cs]
                sv = _dot(wg, vn_blk) + b_ref[g]
                dz_ref[0, rs, cs] = (dav[rs, cs] * sv * gu[rs, cs]).astype(BF16)
                dsv = dav[rs, cs] * u[rs, cs]
                dsv_sum += dsv
                dsv = dsv.astype(BF16)
                dw += _dot_nt(dsv, vn_blk)
                dvn_ref[rs, cs] = _dot_tn(wg, dsv)
            dw_ref[g] += dw
            dsv_ref[:, cs] += dsv_sum

        dvn = dvn_ref[...]
        _accumulate(dng_ref, jnp.sum(dvn * vhat, axis=0, keepdims=True))
        dvh = dvn * gain
        dv = r * (dvh - jnp.mean(dvh, axis=-1, keepdims=True) - vhat * jnp.mean(dvh * vhat, axis=-1, keepdims=True))
        dz_ref[1] = (dv * gv).astype(BF16)

        @pl.when(i == steps - 1)
        def _():
            for g in range(ng):
                dw_ref[g] = jnp.where(mask, dw_ref[g], 0.0)
                row_sum = jnp.sum(dsv_ref[:, g * dg:(g + 1) * dg], axis=1, keepdims=True)
                db_ref[g] = jnp.broadcast_to(row_sum, (CHUNK, CHUNK))

    return pl.pallas_call(
        body, name=name, grid=(steps,),
        in_specs=[_rows(tm, d, 0), _rows(tm, d, 1), _rows(tm, d), _const((1, d)), _const((ng, CHUNK, CHUNK)), _const((ng, CHUNK, 1)),
                  ANY_SPEC],
        out_specs=[pl.BlockSpec((2, tm, d), lambda i: (DZ_SLOT[0] // 2, i, 0)), _const((ng, CHUNK, CHUNK)),
                   _const((ng, CHUNK, CHUNK)), _const((1, d))],
        out_shape=[jax.ShapeDtypeStruct(dz.shape, BF16), jax.ShapeDtypeStruct((ng, CHUNK, CHUNK), F32),
                   jax.ShapeDtypeStruct((ng, CHUNK, CHUNK), F32), jax.ShapeDtypeStruct((1, d), F32)],
        scratch_shapes=[pltpu.VMEM((tm, d), F32), pltpu.VMEM((CHUNK, d), F32)], input_output_aliases={6: 0},
        compiler_params=_params("arbitrary"))(z, z, da, norm_g, sgu_w, sgu_b3, dz)


def pool_bwd(db, diff, dz, pool_w, pool_scale, name):
    t, d = db.shape
    ng = pool_w.shape[0]
    dg = d // ng
    tm = _tile(t, 256)
    per = tm // HALO
    steps = t // tm

    def body(db_ref, next_ref, diff_ref, w_ref, s_ref, _, dc_ref, dw_ref, ds_ref):
        i = pl.program_id(0)
        dbc = db_ref[...].astype(F32)
        nxt = jnp.where(i < steps - 1, next_ref[...].astype(F32), 0.0)
        ext = jnp.concatenate([dbc, nxt], axis=0)
        rows = tm + HALO
        tok = i * tm + lax.broadcasted_iota(jnp.int32, (rows, 1), 0)

        @pl.when(i == 0)
        def _():
            dw_ref[...] = jnp.zeros_like(dw_ref)
            ds_ref[...] = jnp.zeros_like(ds_ref)

        for g, win in enumerate(POOL_WINDOWS):
            cs = slice(g * dg, (g + 1) * dg)
            dp = (ext[:, cs] * s_ref[:, cs]).astype(BF16)
            dd = _dot_nt(dp, w_ref[g])
            s = dd * (1.0 / jnp.minimum(tok + 1, win).astype(F32))
            sh = 1
            while sh < win:
                s = s + pltpu.roll(s, rows - sh, 0)
                sh *= 2
            dc_ref[0, :, cs] = (s[:tm] - dd[:tm]).astype(BF16)
            dfg = diff_ref[:, cs]
            ds_ref[:, cs] += jnp.sum(dbc[:, cs] * _dot(dfg, w_ref[g]), axis=0, keepdims=True)
            dw_ref[g] += _dot_tn(dfg, dp[:tm])

    return pl.pallas_call(
        body, name=name, grid=(steps,),
        in_specs=[_rows(tm, d), pl.BlockSpec((HALO, d), lambda i: (jnp.minimum((i + 1) * per, t // HALO - 1), 0)),
                  _rows(tm, d), _const((ng, dg, dg)), _const((1, d)), ANY_SPEC],
        out_specs=[pl.BlockSpec((1, tm, d), lambda i: (DZ_SLOT[2], i, 0)), _const((ng, dg, dg)), _const((1, d))],
        out_shape=[jax.ShapeDtypeStruct(dz.shape, BF16), jax.ShapeDtypeStruct((ng, dg, dg), F32), jax.ShapeDtypeStruct((1, d), F32)],
        input_output_aliases={5: 0},
        compiler_params=_params("arbitrary"))(db, db, diff, pool_w, pool_scale, dz)


def dw_tn(x, dy, nk, name, x_kmajor=False, dy_mode="same", deps=()):
    t = x.shape[-2]
    kx = x.shape[-1]
    n = dy.shape[-1] // nk if dy_mode == "cols" else dy.shape[-1]
    tt = _tile(t, DW_TOKENS)
    steps = t // tt

    def body(x_ref, dy_ref, o_ref, acc_ref):
        s = pl.program_id(1)
        part = _dot_tn(x_ref[0] if x_kmajor else x_ref[...], dy_ref[0] if dy_mode == "kmajor" else dy_ref[...])
        if steps == 1:
            o_ref[0] = part.astype(BF16)
            return

        @pl.when(s == 0)
        def _():
            acc_ref[...] = jnp.zeros_like(acc_ref)

        acc_ref[...] += part

        @pl.when(s == steps - 1)
        def _():
            o_ref[0] = acc_ref[...].astype(BF16)

    x_spec = pl.BlockSpec((1, tt, kx), lambda k, s: (k, s, 0)) if x_kmajor else pl.BlockSpec((tt, kx), lambda k, s: (s, 0))
    dy_spec = {"kmajor": pl.BlockSpec((1, tt, n), lambda k, s: (k, s, 0)), "cols": pl.BlockSpec((tt, n), lambda k, s: (s, k)),
               "same": pl.BlockSpec((tt, n), lambda k, s: (s, 0))}[dy_mode]
    return pl.pallas_call(
        _ordered_after(body, 2, deps), name=name, grid=(nk, steps), in_specs=[x_spec, dy_spec] + [ANY_SPEC] * len(deps),
        out_specs=pl.BlockSpec((1, kx, n), lambda k, s: (k, 0, 0)), out_shape=jax.ShapeDtypeStruct((nk, kx, n), BF16),
        scratch_shapes=[pltpu.VMEM((kx, n) if steps > 1 else (8, 128), F32)],
        compiler_params=_params("parallel", "arbitrary"))(x, dy, *deps)


def dw_in_tiles(xn, dz, nk, name):
    t, d = xn.shape
    sections = len(DZ_SLOT)
    per_section = d // DW_IN_TILE
    per_shard = sections * per_section // nk

    def body(x_ref, dy_ref, o_ref):
        o_ref[0] = _dot_tn(x_ref[...], dy_ref[0]).astype(BF16)

    def slot(j):
        return (j // per_section + DZ_SLOT[0]) % sections

    return pl.pallas_call(
        body, name=name, grid=(sections * per_section,),
        in_specs=[pl.BlockSpec((t, d), lambda j: (0, 0)), pl.BlockSpec((1, t, DW_IN_TILE), lambda j: (slot(j), 0, j % per_section))],
        out_specs=pl.BlockSpec((1, d, DW_IN_TILE), lambda j: (j // per_shard, 0, j % per_shard)),
        out_shape=jax.ShapeDtypeStruct((nk, d, sections * d // nk), BF16), compiler_params=_params("parallel"))(xn, dz)


def _place():
    x, y, c = lax.axis_index("x"), lax.axis_index("y"), lax.axis_index("c")
    chips = [((1 - x) if fx else x, (1 - y) if fy else y) for fx, fy in OTHER_CHIPS]
    return x, y, c, chips


def _sibling_handshake(x, y, c):
    barrier = pltpu.get_barrier_semaphore()
    pl.semaphore_signal(barrier, inc=1, device_id=(x, y, 1 - c), device_id_type=MESH_IDS)
    pl.semaphore_wait(barrier, 1)


def _chips_handshake(c, chips):
    barrier = pltpu.get_barrier_semaphore()
    for cx, cy in chips:
        pl.semaphore_signal(barrier, inc=1, device_id=(cx, cy, c), device_id_type=MESH_IDS)
    pl.semaphore_wait(barrier, len(chips))


def _half(rows, which):
    return pl.ds(pl.multiple_of(which * (rows // 2), 16), rows // 2)


def _hbm(a):
    return pltpu.with_memory_space_constraint(a, pltpu.HBM)


def _by_shape(arrays):
    buckets = {}
    for i, a in enumerate(arrays):
        buckets.setdefault(a.shape, []).append(i)
    return list(buckets.values())


def cast_place(shards, chip, name, deps=(), plain=False):
    n = len(shards)

    def body(chip_ref, *refs):
        outs = refs[n + len(deps):]
        for a, w_ref in enumerate(refs[:n]):
            cast = w_ref[...].astype(BF16)
            outs[a][0] = cast
            if plain:
                outs[n + a][...] = cast

    def rows(s):
        return (s.shape[0] // ROW_STEPS, s.shape[1])

    out = pl.pallas_call(
        body, name=name,
        grid_spec=pltpu.PrefetchScalarGridSpec(
            num_scalar_prefetch=1, grid=(ROW_STEPS,),
            in_specs=[pl.BlockSpec(rows(s), lambda i, chip_ref: (i, 0)) for s in shards] + [ANY_SPEC] * len(deps),
            out_specs=[pl.BlockSpec((1,) + rows(s), lambda i, chip_ref: (chip_ref[0], i, 0)) for s in shards]
            + [pl.BlockSpec(rows(s), lambda i, chip_ref: (i, 0)) for s in shards] * plain),
        out_shape=[jax.ShapeDtypeStruct((N_CHIPS,) + s.shape, BF16) for s in shards]
        + [jax.ShapeDtypeStruct(s.shape, BF16) for s in shards] * plain,
        compiler_params=_params("parallel"))(chip, *shards, *deps)
    return (out[:n], out[n:]) if plain else out


def _gather_copy(buf, sends, recvs, i, j, me, chip_xy, c):
    cx, cy = chip_xy
    mine = _half(buf.shape[1], c)
    return pltpu.make_async_remote_copy(
        src_ref=buf.at[me, mine], dst_ref=buf.at[me, mine], send_sem=sends.at[3 * i + j], recv_sem=recvs.at[3 * i + j],
        device_id=(cx, cy, c), device_id_type=MESH_IDS)


def allgather_start(bufs, name, barrier):
    n = len(bufs)

    def body(*refs):
        ins = refs[:n]
        sends, recvs = refs[n], refs[n + 1]
        token = refs[2 * n + 2]
        x, y, c, chips = _place()
        _chips_handshake(c, chips)
        for i in range(n):
            for j, chip_xy in enumerate(chips):
                _gather_copy(ins[i], sends, recvs, i, j, 2 * x + y, chip_xy, c).start()
        token[...] = jnp.zeros_like(token)

    out = pl.pallas_call(
        body, name=name, in_specs=[HBM_SPEC] * n,
        out_specs=[SEM_SPEC, SEM_SPEC] + [HBM_SPEC] * n + [VMEM_SPEC],
        out_shape=[pltpu.SemaphoreType.DMA((3 * n,)), pltpu.SemaphoreType.DMA((3 * n,))]
        + [pltpu.HBM(b.shape, b.dtype) for b in bufs] + [jax.ShapeDtypeStruct((8, 128), F32)],
        input_output_aliases={i: i + 2 for i in range(n)},
        compiler_params=pltpu.CompilerParams(has_side_effects=DATAFLOW, collective_id=barrier))(*[_hbm(b) for b in bufs])
    return out[0], out[1], list(out[2:2 + n]), out[2 + n]


def allgather_wait(sends, recvs, bufs, after, name, first=0):
    n = len(bufs)
    after = tuple(after) if isinstance(after, (tuple, list)) else (after,)

    def body(*refs):
        ins = refs[:n]
        send_sems, recv_sems = refs[n], refs[n + 1]
        x, y, c, chips = _place()
        for i in range(n):
            for j, (cx, cy) in enumerate(chips):
                mine = _half(ins[i].shape[1], c)
                cp = pltpu.make_async_remote_copy(
                    src_ref=ins[i].at[2 * x + y, mine], dst_ref=ins[i].at[2 * cx + cy, mine],
                    send_sem=send_sems.at[3 * (first + i) + j], recv_sem=recv_sems.at[3 * (first + i) + j],
                    device_id=(cx, cy, c), device_id_type=MESH_IDS)
                cp.wait_send()
                cp.wait_recv()

    return pl.pallas_call(
        body, name=name, in_specs=[HBM_SPEC] * n + [SEM_SPEC, SEM_SPEC] + [ANY_SPEC] * len(after), out_specs=[HBM_SPEC] * n,
        out_shape=[pltpu.HBM(b.shape, b.dtype) for b in bufs], input_output_aliases={i: i for i in range(n)},
        compiler_params=pltpu.CompilerParams(has_side_effects=DATAFLOW))(*bufs, sends, recvs, *after)


def d2d_forward(bufs, name):
    n = len(bufs)

    def body(*refs):
        ins = refs[:n]
        send_sems, recv_sems = refs[2 * n:]
        x, y, c, chips = _place()
        _sibling_handshake(x, y, c)
        copies = []
        for i in range(n):
            mine = _half(ins[i].shape[1], c)
            for j, (cx, cy) in enumerate(chips):
                landed = ins[i].at[2 * cx + cy, mine]
                cp = pltpu.make_async_remote_copy(
                    src_ref=landed, dst_ref=landed, send_sem=send_sems.at[i, j], recv_sem=recv_sems.at[i, j],
                    device_id=(x, y, 1 - c), device_id_type=MESH_IDS)
                cp.start()
                copies.append(cp)
        for i in range(n):
            theirs = _half(ins[i].shape[1], 1 - c)
            for j, (cx, cy) in enumerate(chips):
                passed = ins[i].at[2 * cx + cy, theirs]
                pltpu.make_async_remote_copy(
                    src_ref=passed, dst_ref=passed, send_sem=send_sems.at[i, j], recv_sem=recv_sems.at[i, j],
                    device_id=(x, y, 1 - c), device_id_type=MESH_IDS).wait_recv()
        for cp in copies:
            cp.wait_send()

    return pl.pallas_call(
        body, name=name, in_specs=[HBM_SPEC] * n, out_specs=[HBM_SPEC] * n,
        out_shape=[jax.ShapeDtypeStruct(b.shape, b.dtype) for b in bufs], input_output_aliases={i: i for i in range(n)},
        scratch_shapes=[pltpu.SemaphoreType.DMA((n, 3))] * 2,
        compiler_params=pltpu.CompilerParams(has_side_effects=True, collective_id=SIBLING_BARRIER))(*bufs)


def _forward_copy(buf, sends, recvs, i, j, chip_xy, x, y, c):
    cx, cy = chip_xy
    rows = buf.shape[1]
    return pltpu.make_async_remote_copy(
        src_ref=buf.at[2 * cx + cy, _half(rows, c)], dst_ref=buf.at[2 * cx + cy, _half(rows, 1 - c)],
        send_sem=sends.at[3 * i + j], recv_sem=recvs.at[3 * i + j], device_id=(x, y, 1 - c), device_id_type=MESH_IDS)


def d2d_forward_start(bufs, name):
    n = len(bufs)

    def body(*refs):
        ins = refs[:n]
        sends, recvs = refs[n], refs[n + 1]
        token = refs[2 * n + 2]
        x, y, c, chips = _place()
        _sibling_handshake(x, y, c)
        for i in range(n):
            mine = _half(ins[i].shape[1], c)
            for j, (cx, cy) in enumerate(chips):
                landed = ins[i].at[2 * cx + cy, mine]
                pltpu.make_async_remote_copy(
                    src_ref=landed, dst_ref=landed, send_sem=sends.at[3 * i + j], recv_sem=recvs.at[3 * i + j],
                    device_id=(x, y, 1 - c), device_id_type=MESH_IDS).start()
        token[...] = jnp.zeros_like(token)

    out = pl.pallas_call(
        body, name=name, in_specs=[HBM_SPEC] * n,
        out_specs=[SEM_SPEC, SEM_SPEC] + [HBM_SPEC] * n + [VMEM_SPEC],
        out_shape=[pltpu.SemaphoreType.DMA((3 * n,)), pltpu.SemaphoreType.DMA((3 * n,))]
        + [pltpu.HBM(b.shape, b.dtype) for b in bufs] + [jax.ShapeDtypeStruct((8, 128), F32)],
        input_output_aliases={i: i + 2 for i in range(n)},
        compiler_params=pltpu.CompilerParams(has_side_effects=DATAFLOW, collective_id=SIBLING_BARRIER))(*[_hbm(b) for b in bufs])
    return (out[0], out[1], list(out[2:2 + n])), out[2 + n]


def d2d_forward_wait(started, after, name):
    sends, recvs, bufs = started
    n = len(bufs)

    def body(*refs):
        ins = refs[:n]
        send_sems, recv_sems = refs[n], refs[n + 1]
        x, y, c, chips = _place()
        for i in range(n):
            for j, chip_xy in enumerate(chips):
                cp = _forward_copy(ins[i], send_sems, recv_sems, i, j, chip_xy, x, y, c)
                cp.wait_send()
                cp.wait_recv()

    return pl.pallas_call(
        body, name=name, in_specs=[HBM_SPEC] * n + [SEM_SPEC, SEM_SPEC, ANY_SPEC], out_specs=[HBM_SPEC] * n,
        out_shape=[pltpu.HBM(b.shape, b.dtype) for b in bufs], input_output_aliases={i: i for i in range(n)},
        compiler_params=pltpu.CompilerParams(has_side_effects=DATAFLOW))(*bufs, sends, recvs, after)


def _sibling_copy(src, land, sends, recvs, i, x, y, c, halves):
    part = src.at[:, _half(src.shape[1], 1 - c)] if halves else src
    return pltpu.make_async_remote_copy(
        src_ref=part, dst_ref=land, send_sem=sends.at[i], recv_sem=recvs.at[i], device_id=(x, y, 1 - c),
        device_id_type=MESH_IDS)


def sibling_start(arrays, halves, name):
    n = len(arrays)
    lands = [lax.empty((a.shape[0], a.shape[1] // 2, a.shape[2]) if halves else a.shape, a.dtype) for a in arrays]

    def body(*refs):
        srcs, zones = refs[:n], refs[n:2 * n]
        sends, recvs = refs[2 * n], refs[2 * n + 1]
        token = refs[4 * n + 2]
        x, y, c, _ = _place()
        _sibling_handshake(x, y, c)
        for i in range(n):
            _sibling_copy(srcs[i], zones[i], sends, recvs, i, x, y, c, halves).start()
        token[...] = jnp.zeros_like(token)

    out = pl.pallas_call(
        body, name=name, in_specs=[HBM_SPEC] * (2 * n),
        out_specs=[SEM_SPEC, SEM_SPEC] + [HBM_SPEC] * (2 * n) + [VMEM_SPEC],
        out_shape=[pltpu.SemaphoreType.DMA((n,)), pltpu.SemaphoreType.DMA((n,))]
        + [pltpu.HBM(a.shape, a.dtype) for a in arrays + lands] + [jax.ShapeDtypeStruct((8, 128), F32)],
        input_output_aliases={i: i + 2 for i in range(2 * n)},
        compiler_params=pltpu.CompilerParams(has_side_effects=DATAFLOW, collective_id=SIBLING_BARRIER))(
            *[_hbm(a) for a in arrays + lands])
    return (out[0], out[1], list(out[2:2 + n]), list(out[2 + n:2 + 2 * n])), out[2 + 2 * n]


def sibling_wait(started, halves, after, name):
    sends, recvs, arrays, lands = started
    n = len(arrays)
    after = tuple(after) if isinstance(after, (tuple, list)) else (after,)

    def body(*refs):
        srcs, zones = refs[:n], refs[n:2 * n]
        send_sems, recv_sems = refs[2 * n], refs[2 * n + 1]
        x, y, c, _ = _place()
        for i in range(n):
            cp = _sibling_copy(srcs[i], zones[i], send_sems, recv_sems, i, x, y, c, halves)
            cp.wait_send()
            cp.wait_recv()

    out = pl.pallas_call(
        body, name=name, in_specs=[HBM_SPEC] * (2 * n) + [SEM_SPEC, SEM_SPEC] + [ANY_SPEC] * len(after),
        out_specs=[HBM_SPEC] * (2 * n),
        out_shape=[pltpu.HBM(a.shape, a.dtype) for a in arrays + lands], input_output_aliases={i: i for i in range(2 * n)},
        compiler_params=pltpu.CompilerParams(has_side_effects=DATAFLOW))(*arrays, *lands, sends, recvs, *after)
    return list(out[:n]), list(out[n:])


def _scatter_copy(src, land, sends, recvs, i, j, chip_xy, c):
    cx, cy = chip_xy
    return pltpu.make_async_remote_copy(
        src_ref=src.at[2 * cx + cy], dst_ref=land.at[j], send_sem=sends.at[3 * i + j], recv_sem=recvs.at[3 * i + j],
        device_id=(cx, cy, c), device_id_type=MESH_IDS)


def scatter_start(sums, name, barrier):
    n = len(sums)
    lands = [lax.empty((3,) + s.shape[1:], s.dtype) for s in sums]

    def body(*refs):
        srcs, zones = refs[:n], refs[n:2 * n]
        sends, recvs = refs[2 * n], refs[2 * n + 1]
        token = refs[4 * n + 2]
        _, _, c, chips = _place()
        _chips_handshake(c, chips)
        for i in range(n):
            for j, chip_xy in enumerate(chips):
                _scatter_copy(srcs[i], zones[i], sends, recvs, i, j, chip_xy, c).start()
        token[...] = jnp.zeros_like(token)

    out = pl.pallas_call(
        body, name=name, in_specs=[HBM_SPEC] * (2 * n),
        out_specs=[SEM_SPEC, SEM_SPEC] + [HBM_SPEC] * (2 * n) + [VMEM_SPEC],
        out_shape=[pltpu.SemaphoreType.DMA((3 * n,)), pltpu.SemaphoreType.DMA((3 * n,))]
        + [pltpu.HBM(a.shape, a.dtype) for a in sums + lands] + [jax.ShapeDtypeStruct((8, 128), F32)],
        input_output_aliases={i: i + 2 for i in range(2 * n)},
        compiler_params=pltpu.CompilerParams(has_side_effects=DATAFLOW, collective_id=barrier))(*[_hbm(a) for a in sums + lands])
    return out[0], out[1], list(out[2:2 + n]), list(out[2 + n:2 + 2 * n]), out[2 + 2 * n]


def scatter_wait(sends, recvs, sums, lands, after, name):
    n = len(sums)

    def body(*refs):
        srcs, zones = refs[:n], refs[n:2 * n]
        send_sems, recv_sems = refs[2 * n], refs[2 * n + 1]
        _, _, c, chips = _place()
        for i in range(n):
            for j, chip_xy in enumerate(chips):
                cp = _scatter_copy(srcs[i], zones[i], send_sems, recv_sems, i, j, chip_xy, c)
                cp.wait_send()
                cp.wait_recv()

    out = pl.pallas_call(
        body, name=name, in_specs=[HBM_SPEC] * (2 * n) + [SEM_SPEC, SEM_SPEC, ANY_SPEC], out_specs=[HBM_SPEC] * (2 * n),
        out_shape=[pltpu.HBM(a.shape, a.dtype) for a in sums + lands], input_output_aliases={i: i for i in range(2 * n)},
        compiler_params=pltpu.CompilerParams(has_side_effects=DATAFLOW))(*sums, *lands, sends, recvs, after)
    return list(out[:n]), list(out[n:])


def add_halves(grads, recvs, core, name):
    n = len(grads)
    nk = grads[0].shape[0]
    views = [g.reshape(nk, 2, g.shape[1] // 2, g.shape[2]) for g in grads]

    def body(core_ref, *refs):
        for g_ref, r_ref, o_ref in zip(refs[:n], refs[n:2 * n], refs[2 * n:]):
            o_ref[0] = (g_ref[0, 0].astype(F32) + r_ref[0].astype(F32)).astype(BF16)

    return pl.pallas_call(
        body, name=name,
        grid_spec=pltpu.PrefetchScalarGridSpec(
            num_scalar_prefetch=1, grid=(nk,),
            in_specs=[pl.BlockSpec((1, 1) + v.shape[2:], lambda k, core_ref: (k, core_ref[0], 0, 0)) for v in views]
            + [pl.BlockSpec((1,) + r.shape[1:], lambda k, core_ref: (k, 0, 0)) for r in recvs],
            out_specs=[pl.BlockSpec((1,) + r.shape[1:], lambda k, core_ref: (k, 0, 0)) for r in recvs]),
        out_shape=[jax.ShapeDtypeStruct(r.shape, BF16) for r in recvs], compiler_params=_params("parallel"))(core, *views, *recvs)


def add_chips(sums, lands, chip, name):
    n = len(sums)

    def body(chip_ref, *refs):
        for s_ref, r_ref, o_ref in zip(refs[:n], refs[n:2 * n], refs[2 * n:]):
            o_ref[...] = ((s_ref[0].astype(F32) + r_ref[0].astype(F32)) + r_ref[1].astype(F32)) + r_ref[2].astype(F32)

    def rows(s):
        return (s.shape[1] // ROW_STEPS, s.shape[2])

    return pl.pallas_call(
        body, name=name,
        grid_spec=pltpu.PrefetchScalarGridSpec(
            num_scalar_prefetch=1, grid=(ROW_STEPS,),
            in_specs=[pl.BlockSpec((1,) + rows(s), lambda i, chip_ref: (chip_ref[0], i, 0)) for s in sums]
            + [pl.BlockSpec((3,) + rows(s), lambda i, chip_ref: (0, i, 0)) for s in sums],
            out_specs=[pl.BlockSpec(rows(s), lambda i, chip_ref: (i, 0)) for s in sums]),
        out_shape=[jax.ShapeDtypeStruct(s.shape[1:], F32) for s in sums], compiler_params=_params("parallel"))(chip, *sums, *lands)


def _adamw_math(w, g, m, v):
    m = ADAM_B1 * m + (1.0 - ADAM_B1) * g
    v = ADAM_B2 * v + (1.0 - ADAM_B2) * (g * g)
    m_hat = m / (1.0 - ADAM_B1 ** ADAM_STEP)
    v_hat = v / (1.0 - ADAM_B2 ** ADAM_STEP)
    return -ADAM_LR * (m_hat / (jnp.sqrt(v_hat) + ADAM_EPS) + ADAM_WD * w), m, v


def adamw_one_half(ws, gs, ms, vs, core, name, own, into=None):
    n = len(ws)
    r, cdim = ws[0].shape
    half = r // 2
    tr = _tile(half, max(8, 256 // n))
    steps = half // tr
    prior = () if into is None else tuple(a for group in into for a in group)

    def body(core_ref, *refs):
        ins, outs = refs[:4 * n], refs[4 * n + len(prior):]
        for a in range(n):
            w_ref, g_ref, m_ref, v_ref = ins[a::n]
            g = g_ref[...]
            outs[a][...] = g
            outs[n + a][...], outs[2 * n + a][...], outs[3 * n + a][...] = _adamw_math(w_ref[...], g, m_ref[...], v_ref[...])

    def rows(i, core_ref):
        return ((core_ref[0] if own else 1 - core_ref[0]) * steps + i, 0)

    whole = pl.BlockSpec((tr, cdim), rows)
    part = pl.BlockSpec((tr, cdim), lambda i, core_ref: (i, 0))
    out = pl.pallas_call(
        body, name=name,
        grid_spec=pltpu.PrefetchScalarGridSpec(
            num_scalar_prefetch=1, grid=(steps,),
            in_specs=[whole] * n + [part] * n + [whole] * (2 * n) + [ANY_SPEC] * len(prior), out_specs=[whole] * (4 * n)),
        out_shape=[jax.ShapeDtypeStruct((r, cdim), F32)] * (4 * n),
        input_output_aliases={1 + 4 * n + j: j for j in range(len(prior))},
        compiler_params=_params("parallel"))(core, *ws, *gs, *ms, *vs, *prior)
    return [out[k * n:(k + 1) * n] for k in range(4)]


def adamw_halves_sparsecore(ws, owns, others, ms, vs, name):
    n = len(ws)
    r, cdim = ws[0].shape
    half_groups = r // 2 // SC_ROWS
    per_tile = -(-2 * half_groups // SC_TILES)

    def body(*refs):
        ins, outs, (wb, gb, mb, vb) = refs[:5 * n], refs[5 * n:9 * n], refs[9 * n:]
        tile = lax.axis_index("sc_tile") * 2 + lax.axis_index("sc_core")
        core = lax.axis_index("c")

        @pl.loop(0, per_tile)
        def _(it):
            group = tile + SC_TILES * it

            @pl.when(group < 2 * half_groups)
            def _():
                rows = pl.ds(group * SC_ROWS, SC_ROWS)
                in_own = (group < half_groups) == (core == 0)
                half_rows = pl.ds((group % half_groups) * SC_ROWS, SC_ROWS)
                for a in range(n):
                    w_hbm, own_hbm, other_hbm, m_hbm, v_hbm = ins[a::n]
                    g_out, d_out, m_out, v_out = outs[a::n]
                    pltpu.sync_copy(w_hbm.at[rows], wb)
                    pltpu.sync_copy(m_hbm.at[rows], mb)
                    pltpu.sync_copy(v_hbm.at[rows], vb)

                    @pl.when(in_own)
                    def _():
                        pltpu.sync_copy(own_hbm.at[half_rows], gb)

                    @pl.when(jnp.logical_not(in_own))
                    def _():
                        pltpu.sync_copy(other_hbm.at[half_rows], gb)

                    pltpu.sync_copy(gb, g_out.at[rows])

                    @pl.loop(0, SC_ROWS)
                    def _(row):
                        @pl.loop(0, cdim, step=SC_LANES)
                        def _(col):
                            at = (row, pl.ds(col, SC_LANES))
                            wb[at], mb[at], vb[at] = _adamw_math(wb[at], gb[at], mb[at], vb[at])

                    pltpu.sync_copy(wb, d_out.at[rows])
                    pltpu.sync_copy(mb, m_out.at[rows])
                    pltpu.sync_copy(vb, v_out.at[rows])

    out = pl.kernel(
        body, name=name, out_type=[jax.ShapeDtypeStruct((r, cdim), F32)] * (4 * n),
        mesh=plsc.VectorSubcoreMesh(core_axis_name="sc_core", subcore_axis_name="sc_tile"),
        scratch_types=[pltpu.VMEM((SC_ROWS, cdim), F32)] * 4)(*ws, *owns, *others, *ms, *vs)
    return [out[a::n] for a in range(n)]


def small_allreduce_adamw(g, w, m, v):
    r, cdim = g.shape

    def body(g_ref, w_ref, m_ref, v_ref, go_ref, d_ref, mo_ref, vo_ref, pair, quad, d2d_send, d2d_recv, ici_send, ici_recv):
        x, y, c, chips = _place()
        me = 2 * x + y
        pair[c] = g_ref[...]
        swap = pltpu.make_async_remote_copy(
            src_ref=g_ref, dst_ref=pair.at[c], send_sem=d2d_send, recv_sem=d2d_recv, device_id=(x, y, 1 - c),
            device_id_type=MESH_IDS)
        swap.start()
        swap.wait()
        quad[me] = pair[0] + pair[1]
        copies = []
        for j, (cx, cy) in enumerate(chips):
            cp = pltpu.make_async_remote_copy(
                src_ref=quad.at[me], dst_ref=quad.at[me], send_sem=ici_send.at[j], recv_sem=ici_recv.at[j],
                device_id=(cx, cy, c), device_id_type=MESH_IDS)
            cp.start()
            copies.append(cp)
        for j, (cx, cy) in enumerate(chips):
            slot = quad.at[2 * cx + cy]
            pltpu.make_async_remote_copy(
                src_ref=slot, dst_ref=slot, send_sem=ici_send.at[j], recv_sem=ici_recv.at[j], device_id=(cx, cy, c),
                device_id_type=MESH_IDS).wait_recv()
        for cp in copies:
            cp.wait_send()
        total = (quad[0] + quad[1]) + (quad[2] + quad[3])
        go_ref[...] = total
        d_ref[...], mo_ref[...], vo_ref[...] = _adamw_math(w_ref[...], total, m_ref[...], v_ref[...])

    return pl.pallas_call(
        body, name="small_allreduce_adamw", in_specs=[VMEM_SPEC] * 4, out_specs=[VMEM_SPEC] * 4,
        out_shape=[jax.ShapeDtypeStruct((r, cdim), F32)] * 4,
        scratch_shapes=[pltpu.VMEM((2, r, cdim), F32), pltpu.VMEM((N_CHIPS, r, cdim), F32), pltpu.SemaphoreType.DMA,
                        pltpu.SemaphoreType.DMA, pltpu.SemaphoreType.DMA((3,)), pltpu.SemaphoreType.DMA((3,))],
        compiler_params=pltpu.CompilerParams(has_side_effects=True, vmem_limit_bytes=VMEM_LIMIT_V7X))(g, w, m, v)


GROUPS = (
    ("ffn1", ("ffn1_w_gate", "ffn1_w_up", "ffn1_w_down")),
    ("mixer", ("w_in", "pool_w", "w_out_a", "w_out_b", "w_o")),
    ("ffn2", ("ffn2_w_gate", "ffn2_w_up", "ffn2_w_down")),
    ("ple", ("ple_w_gate", "ple_w_proj")),
)
GATHERS = (
    ("ffn1_in", ("ffn1_w_gate", "ffn1_w_up")),
    ("ffn1_out", ("ffn1_w_down",)),
) + GROUPS[1:]
GAINS = ("ffn1_pre_g", "ffn1_post_g", "mix_pre_g", "sgu_norm_g", "pool_scale", "mix_post_g",
         "ffn2_pre_g", "ffn2_post_g", "ple_pre_g", "ple_post_g")
SMALL = GAINS + ("sgu_b", "sgu_w")
PACKED = SMALL + ("loss",)
WEIGHTS = ("ffn1_pre_g", "ffn1_w_gate", "ffn1_w_up", "ffn1_w_down", "ffn1_post_g", "mix_pre_g", "w_in", "sgu_norm_g",
           "sgu_w", "sgu_b", "pool_w", "pool_scale", "w_out_a", "w_out_b", "w_o", "mix_post_g", "ffn2_pre_g",
           "ffn2_w_gate", "ffn2_w_up", "ffn2_w_down", "ffn2_post_g", "ple_pre_g", "ple_w_gate", "ple_w_proj", "ple_post_g")
PACK_ROWS = 16


TRANSPOSED = ("ffn1_w_gate", "ffn1_w_up", "ffn2_w_gate", "ffn2_w_up")


def _shard2d(name, a):
    a = a[0]
    return a.T if name in TRANSPOSED else a.reshape(-1, a.shape[-1])


def _unshard2d(name, a2d, shape):
    return (a2d.T if name in TRANSPOSED else a2d).reshape(shape)


def pack_rows(gains, sgu_b, loss_tile, name):
    n = len(gains)
    d = gains[0].shape[1]
    g = sgu_b.shape[0]

    def body(*refs):
        o_ref = refs[-1]
        o_ref[...] = jnp.zeros_like(o_ref)
        for i in range(n):
            o_ref[i:i + 1, :] = refs[i][...]
        o_ref[n:n + g, 0:CHUNK] = refs[n][...]
        o_ref[n + g:n + g + 1, 0:CHUNK] = refs[n + 1][0:1, :]

    return pl.pallas_call(
        body, name=name, in_specs=[VMEM_SPEC] * (n + 2), out_specs=VMEM_SPEC,
        out_shape=jax.ShapeDtypeStruct((PACK_ROWS, d), F32))(*gains, sgu_b, loss_tile)


def _pack_small(parts, tag):
    d = parts[GAINS[0]].shape[-1]
    rows = pack_rows([parts[n] for n in GAINS], parts["sgu_b"].reshape(-1, CHUNK), parts["loss"], "pack_" + tag)
    return jnp.concatenate([rows, parts["sgu_w"].reshape(-1, d)], axis=0)


def _unpack_small(packed, like):
    n, g = len(GAINS), like["sgu_b"].size // CHUNK
    out = {name: packed[i:i + 1] for i, name in enumerate(GAINS)}
    out["sgu_b"] = packed[n:n + g, :CHUNK].reshape(like["sgu_b"].shape)
    out["loss"] = packed[n + g, 0]
    out["sgu_w"] = packed[PACK_ROWS:].reshape(like["sgu_w"].shape)
    return out


def _ffn_fwd(xn, h, w, pre, g_post, g_next, tag, between=None, first=None):
    a4, s4, t4 = ffn_gu(xn, w[pre + "w_gate"], w[pre + "w_up"], tag + "_gu") if first is None else first(xn)
    deps = ()
    if between is not None:
        more, deps = between(a4)
        w.update(more)
    f, h_new, xn_next = mm_norm_res(a4, w[pre + "w_down"], h, g_post, g_next, 0.5, tag + "_down", deps=deps)
    return dict(xn=xn, h=h, a4=a4, s4=s4, t4=t4, f=f), h_new, xn_next


def _ffn_bwd_w(dh, saved, w, pre, g_post, tag, deps):
    df, dg4, du4, d_post = ffn_bwd_a(dh, saved["f"], g_post, w[pre + "w_down"], saved["s4"], saved["t4"], 0.5, tag + "_bwd_a",
                                     deps=deps)
    grads, exchanging, token = _ffn_dw(saved, df, dg4, du4, pre, tag)
    return dg4, du4, grads, d_post, (exchanging, token)


def _ffn_dw(saved, df, dg4, du4, pre, tag):
    jobs = {pre + "w_down": (saved["a4"], df), pre + "w_gate": (dg4, saved["xn"]), pre + "w_up": (du4, saved["xn"])}
    grads, started, deps = {}, {}, ()
    for name, (x, dy) in jobs.items():
        grads[name] = dw_tn(x, dy, N_CHIPS, f"dw_{name}", x_kmajor=True, dy_mode="same", deps=deps)
        started[name], token = sibling_start([grads[name]], True, "exchange_start_" + name)
        deps = (token,)
    return grads, [started[pre + kind] for kind in ("w_gate", "w_up", "w_down")], token


def _ffn_bwd_x(dh, dg4, du4, saved, w, pre, g_pre, tag, deps):
    return dx_norm_bwd([(dg4, w[pre + "w_gate"], False), (du4, w[pre + "w_up"], False)], saved["h"], g_pre, dh,
                       tag + "_bwd_x", deps=deps)


def _start_gathers(weights, chip):
    first_tag, first_names = GATHERS[0]
    made, own = cast_place([_shard2d(n, weights[n]) for n in first_names], chip, "cast_" + first_tag, plain=True)
    sends, recvs, bufs, token = allgather_start(list(made), "allgather_start_" + first_tag, CHIPS_BARRIERS[0])
    gathering = {first_tag: (sends, recvs, bufs, 0)}
    later = [n for _, names in GATHERS[1:] for n in names]
    cast = cast_place([_shard2d(n, weights[n]) for n in later], chip, "cast_later", deps=(token,))
    sends, recvs, bufs, token = allgather_start(list(cast), "allgather_start_later", CHIPS_BARRIERS[1])
    first = 0
    for tag, names in GATHERS[1:]:
        gathering[tag] = (sends, recvs, bufs[first:first + len(names)], first)
        first += len(names)
    return gathering, token, dict(zip(first_names, own))


def _gathered(started, names, after, tag):
    sends, recvs, bufs, first = started
    landed = allgather_wait(sends, recvs, bufs, after, "allgather_wait_" + tag, first)
    return dict(zip(names, d2d_forward(landed, "d2d_forward_" + tag)))


def _forward_early(started, after, tag):
    sends, recvs, bufs, first = started
    landed = allgather_wait(sends, recvs, bufs, after, "allgather_wait_" + tag, first)
    return d2d_forward_start(landed, "d2d_forward_start_" + tag)


def _forwarded(forwarding, names, after, tag):
    return dict(zip(names, d2d_forward_wait(forwarding, after, "d2d_forward_wait_" + tag)))


def _exchange_start(names, big_g, tag):
    started, token = sibling_start([big_g[n] for n in names], True, "exchange_start_" + tag)
    return [started], token


def _scatter_begin(names, exchanging, after, core, tag):
    partial, from_sibling = [], []
    for e, started in enumerate(exchanging):
        mine, theirs = sibling_wait(started, True, after, f"exchange_wait_{tag}_{e}")
        partial += mine
        from_sibling += theirs
    chip_sums = list(add_halves(partial, from_sibling, core, "add_halves_" + tag))
    turn = len(GROUPS) - 1 - [g for g, _ in GROUPS].index(tag)
    sends, recvs, sums, lands, token = scatter_start(chip_sums, "scatter_start_" + tag, CHIPS_BARRIERS[turn % 2])
    return (names, sends, recvs, sums, lands), token


def _share_begin(scattering, after, chip, tag):
    names, sends, recvs, sums, lands = scattering
    sums, lands = scatter_wait(sends, recvs, sums, lands, after, "scatter_wait_" + tag)
    reduced = list(add_chips(sums, lands, chip, "add_chips_" + tag))
    sharing, token = sibling_start(reduced, False, "share_start_" + tag)
    return (names, sharing), token


def _update(shared, after, weights, moments_m, moments_v, tag, results):
    names, sharing = shared
    reduced, others = sibling_wait(sharing, False, after, "share_wait_" + tag)
    for b, idx in enumerate(_by_shape(reduced)):
        in_bucket = [names[i] for i in idx]
        outs = adamw_halves_sparsecore(
            [_shard2d(n, weights[n]) for n in in_bucket], [reduced[i] for i in idx], [others[i] for i in idx],
            [_shard2d(n, moments_m[n]) for n in in_bucket], [_shard2d(n, moments_v[n]) for n in in_bucket], f"adamw_{tag}_{b}")
        for n, per_weight in zip(in_bucket, outs):
            for store, value in zip(results, per_weight):
                store[n] = _unshard2d(n, value, weights[n].shape)


def _update_last(scattering, after, weights, moments_m, moments_v, core, chip, tag, results):
    names, sends, recvs, sums, lands = scattering
    sums, lands = scatter_wait(sends, recvs, sums, lands, after, "scatter_wait_" + tag)
    reduced = list(add_chips(sums, lands, chip, "add_chips_" + tag))
    (sends, recvs, reduced, zones), _ = sibling_start(reduced, False, "share_start_" + tag)
    state = [[_shard2d(n, held[n]) for n in names] for held in (weights, moments_m, moments_v)]
    first = adamw_one_half(state[0], reduced, state[1], state[2], core, f"adamw_{tag}_own", own=True)
    _, others = sibling_wait((sends, recvs, reduced, zones), False, first[1][-1], "share_wait_" + tag)
    outs = adamw_one_half(state[0], others, state[1], state[2], core, f"adamw_{tag}_other", own=False, into=first)
    for store, values in zip(results, outs):
        for n, value in zip(names, values):
            store[n] = _unshard2d(n, value, weights[n].shape)


def kernel(x, p, ffn1_pre_g, ffn1_w_gate, ffn1_w_up, ffn1_w_down, ffn1_post_g, mix_pre_g, w_in, sgu_norm_g, sgu_w, sgu_b, pool_w, pool_scale, w_out_a, w_out_b, w_o, mix_post_g, ffn2_pre_g, ffn2_w_gate, ffn2_w_up, ffn2_w_down, ffn2_post_g, ple_pre_g, ple_w_gate, ple_w_proj, ple_post_g, loss_target, m_ffn1_pre_g, m_ffn1_w_gate, m_ffn1_w_up, m_ffn1_w_down, m_ffn1_post_g, m_mix_pre_g, m_w_in, m_sgu_norm_g, m_sgu_w, m_sgu_b, m_pool_w, m_pool_scale, m_w_out_a, m_w_out_b, m_w_o, m_mix_post_g, m_ffn2_pre_g, m_ffn2_w_gate, m_ffn2_w_up, m_ffn2_w_down, m_ffn2_post_g, m_ple_pre_g, m_ple_w_gate, m_ple_w_proj, m_ple_post_g, v_ffn1_pre_g, v_ffn1_w_gate, v_ffn1_w_up, v_ffn1_w_down, v_ffn1_post_g, v_mix_pre_g, v_w_in, v_sgu_norm_g, v_sgu_w, v_sgu_b, v_pool_w, v_pool_scale, v_w_out_a, v_w_out_b, v_w_o, v_mix_post_g, v_ffn2_pre_g, v_ffn2_w_gate, v_ffn2_w_up, v_ffn2_w_down, v_ffn2_post_g, v_ple_pre_g, v_ple_w_gate, v_ple_w_proj, v_ple_post_g):
    given = dict(locals())
    weights = {n: given[n] for n in WEIGHTS}
    moments_m = {n: given["m_" + n] for n in WEIGHTS}
    moments_v = {n: given["v_" + n] for n in WEIGHTS}
    core = lax.axis_index("c").astype(jnp.int32).reshape(1)
    chip = (2 * lax.axis_index("x") + lax.axis_index("y")).astype(jnp.int32).reshape(1)

    d = x.shape[-1]
    gathering, last_start, own_first = _start_gathers(weights, chip)
    token = (last_start,)
    gain = {n: weights[n] for n in GAINS}
    sgu_w3 = sgu_w[0]
    sgu_b3 = sgu_b[0][:, :, None]
    groups = dict(GROUPS + GATHERS)

    h0 = x[0]
    tgt = loss_target[0]
    p_bf = p[0, 0].astype(BF16)
    xn1 = rms_cast(h0, gain["ffn1_pre_g"], "ffn1_pre_norm")
    w = {}

    def ffn1_first(xn):
        own = ffn_gu(xn, own_first["ffn1_w_gate"][None], own_first["ffn1_w_up"][None], "ffn1_gu_own", chip=chip)
        w.update(_gathered(gathering["ffn1_in"], groups["ffn1_in"], (own[0], last_start), "ffn1_in"))
        return ffn_gu(xn, w["ffn1_w_gate"], w["ffn1_w_up"], "ffn1_gu", chip=chip, into=own)

    s1, h1, xn2 = _ffn_fwd(xn1, h0, w, "ffn1_", gain["ffn1_post_g"], gain["mix_pre_g"], "ffn1",
                           lambda a4: (_gathered(gathering["ffn1_out"], groups["ffn1_out"], a4, "ffn1_out"), ()), ffn1_first)
    w.update(_gathered(gathering["mixer"], groups["mixer"], h1, "mixer"))
    full = {n: w[n].reshape(-1, d) for n in ("w_out_a", "w_out_b", "w_o")}
    n_groups = pool_w.shape[1]
    rows_per = pool_w.shape[2]
    dgp = pool_w.shape[3]
    pool_full = w["pool_w"].reshape(N_CHIPS, n_groups, rows_per, dgp).transpose(1, 0, 2, 3).reshape(n_groups, N_CHIPS * rows_per, dgp)
    z = mixer_in(xn2, w["w_in"], "mixer_in")
    a = sgu_fwd(z, gain["sgu_norm_g"], sgu_w3, sgu_b3, "sgu_fwd")
    diff, b = pool_fwd(z, pool_full, gain["pool_scale"], "pool_fwd")
    ya, yb, y = mixer_y(a, b, z, full["w_out_a"], full["w_out_b"], "mixer_y")
    forwarding, tok = _forward_early(gathering["ffn2"], y, "ffn2")
    m, h2, xn3 = mm_norm_res(y[None], full["w_o"][None], h1, gain["mix_post_g"], gain["ffn2_pre_g"], 1.0, "mixer_out",
                             deps=(tok,))
    w.update(_forwarded(forwarding, groups["ffn2"], h2, "ffn2"))
    early = {}

    def forward_ple(a4):
        early["ple"], tok_ple = _forward_early(gathering["ple"], a4, "ple")
        return {}, (tok_ple,)

    s2, h3, xn4 = _ffn_fwd(xn3, h2, w, "ffn2_", gain["ffn2_post_g"], gain["ple_pre_g"], "ffn2", forward_ple)
    w.update(_forwarded(early["ple"], groups["ple"], h3, "ple"))
    full["ple_w_gate"] = w["ple_w_gate"].reshape(-1, d)
    proj_full = w["ple_w_proj"].transpose(1, 0, 2).reshape(ple_w_proj.shape[1], -1)

    small_g, big_g = {}, {}
    ds, de, dh3, small_g["ple_post_g"], small_g["ple_pre_g"], loss_part = ple_loss(
        xn4, p_bf, full["ple_w_gate"], proj_full, h3, gain["ple_post_g"], gain["ple_pre_g"], tgt, "ple_loss")
    small_g["loss"] = loss_part
    grad, delta, new_m, new_v = {}, {}, {}, {}
    results = (grad, delta, new_m, new_v)
    update = functools.partial(_update, weights=weights, moments_m=moments_m, moments_v=moments_v, results=results)

    def reduce_behind(tag, previous, started=None):
        exchanging, tok = _exchange_start(groups[tag], big_g, tag) if started is None else started
        if previous is not None:
            shared, tok = _share_begin(previous[1], tok, chip, previous[0])
        scattering, tok_scatter = _scatter_begin(groups[tag], exchanging, tok, core, tag)
        if previous is not None:
            update(shared, tok_scatter, tag=previous[0])
        return (tag, scattering), (tok_scatter,)

    big_g["ple_w_gate"] = dw_tn(xn4, ds, 1, "dw_ple_gate").reshape(N_CHIPS, -1, d)
    big_g["ple_w_proj"] = dw_tn(p_bf, de, N_CHIPS, "dw_ple_proj", dy_mode="cols")
    reducing, deps = reduce_behind("ple", None)

    dg4, du4, g2, small_g["ffn2_post_g"], exchanging = _ffn_bwd_w(dh3, s2, w, "ffn2_", gain["ffn2_post_g"], "ffn2", deps)
    big_g.update(g2)
    dh2, small_g["ffn2_pre_g"] = _ffn_bwd_x(dh3, dg4, du4, s2, w, "ffn2_", gain["ffn2_pre_g"], "ffn2", ())
    reducing, deps = reduce_behind("ffn2", reducing, exchanging)

    dm, dya, dyb, dz, da, db, small_g["mix_post_g"] = mixer_bwd_y(
        dh2, m, gain["mix_post_g"], full["w_o"], ya, yb, z, full["w_out_a"], full["w_out_b"], "mixer_bwd_y", deps=deps)
    big_g["w_o"] = dw_tn(y, dm, 1, "dw_o").reshape(N_CHIPS, -1, d)
    big_g["w_out_a"] = dw_tn(a, dya, 1, "dw_out_a").reshape(N_CHIPS, -1, d)
    big_g["w_out_b"] = dw_tn(b, dyb, 1, "dw_out_b").reshape(N_CHIPS, -1, d)
    dz, d_sgu_w, d_sgu_b, small_g["sgu_norm_g"] = sgu_bwd(z, da, dz, gain["sgu_norm_g"], sgu_w3, sgu_b3, "sgu_bwd")
    dz, d_pool_w, small_g["pool_scale"] = pool_bwd(db, diff, dz, pool_full, gain["pool_scale"], "pool_bwd")
    big_g["pool_w"] = d_pool_w.astype(BF16).reshape(n_groups, N_CHIPS, rows_per, dgp).transpose(1, 0, 2, 3).reshape(
        N_CHIPS, n_groups * rows_per, dgp)
    big_g["w_in"] = dw_in_tiles(xn2, dz, N_CHIPS, "dw_in")
    dh1, small_g["mix_pre_g"] = dx_norm_bwd([(dz, w["w_in"], True)], h1, gain["mix_pre_g"], dh2, "mixer_bwd_x")
    reducing, deps = reduce_behind("mixer", reducing)

    dg4, du4, g1, small_g["ffn1_post_g"], exchanging = _ffn_bwd_w(dh1, s1, w, "ffn1_", gain["ffn1_post_g"], "ffn1", deps)
    big_g.update(g1)
    reducing, deps = reduce_behind("ffn1", reducing, exchanging)
    dh0, small_g["ffn1_pre_g"] = _ffn_bwd_x(dh1, dg4, du4, s1, w, "ffn1_", gain["ffn1_pre_g"], "ffn1", deps)
    small_g["sgu_w"] = d_sgu_w
    small_g["sgu_b"] = d_sgu_b[:, :, 0]

    no_state = {"loss": jnp.zeros((8, 128), F32)}
    packed = small_allreduce_adamw(
        _pack_small(small_g, "grads"), _pack_small({n: weights[n] for n in SMALL} | no_state, "weights"),
        _pack_small({n: moments_m[n] for n in SMALL} | no_state, "m"), _pack_small({n: moments_v[n] for n in SMALL} | no_state, "v"))
    like = {n: weights[n] for n in SMALL}
    for store, block in zip((grad, delta, new_m, new_v), packed):
        store.update(_unpack_small(block, like))
    _update_last(reducing[1], packed[0], weights, moments_m, moments_v, core, chip, "ffn1", results)

    return (grad["loss"], dh0[None], *[grad[n] for n in WEIGHTS], *[delta[n] for n in WEIGHTS],
            *[new_m[n] for n in WEIGHTS], *[new_v[n] for n in WEIGHTS])
```

```python
import functools

import jax
import jax.numpy as jnp
from jax import lax
from jax.experimental import pallas as pl
from jax.experimental.pallas import tpu as pltpu
from jax.experimental.pallas import tpu_sc as plsc

F32 = jnp.float32
BF16 = jnp.bfloat16
EPS = 1e-6
CHUNK = 128
POOL_WINDOWS = (2, 4, 8, 16)
HALO = 16
N_CHIPS = 4
ADAM_LR, ADAM_B1, ADAM_B2, ADAM_EPS, ADAM_WD, ADAM_STEP = 0.001, 0.9, 0.999, 1e-08, 0.01, 10
VMEM_LIMIT_V7X = 58 * 1024 * 1024
MESH_IDS = pl.DeviceIdType.MESH
HBM_SPEC = pl.BlockSpec(memory_space=pltpu.HBM)
VMEM_SPEC = pl.BlockSpec(memory_space=pltpu.VMEM)
SEM_SPEC = pl.BlockSpec(memory_space=pltpu.SEMAPHORE)
ANY_SPEC = pl.BlockSpec(memory_space=pl.ANY)
DATAFLOW = pltpu.SideEffectType.DATAFLOW_SIDE_EFFECTING
OTHER_CHIPS = ((1, 0), (0, 1), (1, 1))
DZ_SLOT = (2, 3, 4, 0, 1)
SC_TILES, SC_LANES, SC_ROWS = 32, 16, 8
ROW_STEPS = 4
SIBLING_BARRIER = 1
CHIPS_BARRIERS = (2, 3)
DW_TOKENS = 4096
DW_IN_TILE = 256

NT = (((1,), (1,)), ((), ()))
TN = (((0,), (0,)), ((), ()))


def _params(*sem, **more):
    return pltpu.CompilerParams(dimension_semantics=sem or None, vmem_limit_bytes=VMEM_LIMIT_V7X, **more)


def _tile(t, want):
    return max(c for c in range(8, min(t, want) + 1, 8) if t % c == 0)


def _const(shape):
    return pl.BlockSpec(shape, lambda *_: (0,) * len(shape))


def _rows(tm, d, col=0):
    return pl.BlockSpec((tm, d), lambda i: (i, col))


def _kmajor(nk, tm, kb):
    return pl.BlockSpec((nk, tm, kb), lambda i: (0, i, 0))


def _dot(a, b):
    return jnp.dot(a, b, preferred_element_type=F32)


def _dot_nt(a, b):
    return lax.dot_general(a, b, NT, preferred_element_type=F32)


def _dot_tn(a, b):
    return lax.dot_general(a, b, TN, preferred_element_type=F32)


def _gelu(x):
    return 0.5 * x * (1.0 + jnp.tanh(0.7978845608028654 * (x + 0.044715 * x * x * x)))


def _gelu_and_grad(x):
    k, kc = 0.7978845608028654, 0.7978845608028654 * 0.044715
    x2 = x * x
    th = jnp.tanh(x * (k + kc * x2))
    cdf = 0.5 + 0.5 * th
    return x * cdf, cdf + x * (0.5 - 0.5 * th * th) * (k + 3.0 * kc * x2)


def _sigmoid(x):
    return 1.0 / (1.0 + jnp.exp(-x))


def _rstd(h):
    return lax.rsqrt(jnp.mean(h * h, axis=-1, keepdims=True) + EPS)


def _rms_bwd(h, g, dy):
    r = _rstd(h)
    t = dy * g
    dh = r * t - h * (r * r * r) * jnp.mean(h * t, axis=-1, keepdims=True)
    return dh, jnp.sum(dy * h * r, axis=0, keepdims=True)


def _ordered_after(body, n_in, deps):
    if not deps:
        return body
    return lambda *refs: body(*refs[:n_in], *refs[n_in + len(deps):])


def _accumulate(ref, value):
    @pl.when(pl.program_id(0) == 0)
    def _():
        ref[...] = jnp.zeros_like(ref)

    ref[...] += value


def rms_cast(h, g, name):
    t, d = h.shape
    tm = _tile(t, 512)

    def body(h_ref, g_ref, o_ref):
        hv = h_ref[...]
        o_ref[...] = (hv * _rstd(hv) * g_ref[...]).astype(BF16)

    return pl.pallas_call(
        body, name=name, grid=(t // tm,), in_specs=[_rows(tm, d), _const((1, d))], out_specs=_rows(tm, d),
        out_shape=jax.ShapeDtypeStruct((t, d), BF16), compiler_params=_params("parallel"))(h, g)


def ffn_gu(xn, wgt, wut, name, chip=None, into=None):
    t, d = xn.shape
    nkw, fk, _ = wgt.shape
    tm = _tile(t, 512)
    count = N_CHIPS if chip is None else (1 if into is None else N_CHIPS - 1)
    first = 0 if into is None else 1

    def body(*refs):
        x_ref, wg_ref, wu_ref = refs[-6 - (0 if into is None else 3):][:3]
        a_ref, s_ref, t_ref = refs[-3:]
        xv = x_ref[...]
        g = _dot_nt(xv, wg_ref[0])
        sg = _sigmoid(g)
        s = g * sg
        s_ref[0] = s.astype(BF16)
        ds = sg * (1.0 + g * (1.0 - sg))
        u = _dot_nt(xv, wu_ref[0])
        a_ref[0] = (s * u).astype(BF16)
        t_ref[0] = (u * ds).astype(BF16)

    def slot(k, *chip_ref):
        return (chip_ref[0][0] + first + k) % N_CHIPS if chip_ref else k

    w_spec = pl.BlockSpec((1, fk, d), lambda k, i, *c: (slot(k, *c) if nkw > 1 else 0, 0, 0))
    o_spec = pl.BlockSpec((1, tm, fk), lambda k, i, *c: (slot(k, *c), i, 0))
    shape = jax.ShapeDtypeStruct((N_CHIPS, t, fk), BF16)
    prior = () if into is None else tuple(into)
    call = pl.pallas_call(
        body, name=name,
        grid_spec=pltpu.PrefetchScalarGridSpec(
            num_scalar_prefetch=0 if chip is None else 1, grid=(count, t // tm),
            in_specs=[pl.BlockSpec((tm, d), lambda k, i, *c: (i, 0)), w_spec, w_spec] + [ANY_SPEC] * len(prior),
            out_specs=[o_spec] * 3),
        out_shape=[shape] * 3, input_output_aliases={4 + j: j for j in range(len(prior))},
        compiler_params=_params("parallel", "parallel"))
    return call(*(() if chip is None else (chip,)), xn, wgt, wut, *prior)


def mm_norm_res(a3, w3, h_old, g_post, g_next, scale, name, deps=()):
    nk, t, kb = a3.shape
    d = w3.shape[2]
    tm = _tile(t, 256)

    def body(a_ref, w_ref, h_ref, gp_ref, gn_ref, f_ref, hn_ref, xn_ref):
        f = _dot(a_ref[0], w_ref[0])
        for k in range(1, nk):
            f += _dot(a_ref[k], w_ref[k])
        f_ref[...] = f.astype(BF16)
        hn = h_ref[...] + scale * (f * _rstd(f) * gp_ref[...])
        hn_ref[...] = hn
        xn_ref[...] = (hn * _rstd(hn) * gn_ref[...]).astype(BF16)

    return pl.pallas_call(
        _ordered_after(body, 5, deps), name=name, grid=(t // tm,),
        in_specs=[_kmajor(nk, tm, kb), _const((nk, kb, d)), _rows(tm, d), _const((1, d)), _const((1, d))] + [ANY_SPEC] * len(deps),
        out_specs=[_rows(tm, d)] * 3,
        out_shape=[jax.ShapeDtypeStruct((t, d), BF16), jax.ShapeDtypeStruct((t, d), F32), jax.ShapeDtypeStruct((t, d), BF16)],
        compiler_params=_params("parallel"))(a3, w3, h_old, g_post, g_next, *deps)


def mixer_in(xn, win4, name):
    t, d = xn.shape
    nk, _, nb = win4.shape
    tm = _tile(t, 512)

    def body(x_ref, w_ref, z_ref):
        z_ref[...] = _dot(x_ref[...], w_ref[0]).astype(BF16)

    return pl.pallas_call(
        body, name=name, grid=(nk, t // tm),
        in_specs=[pl.BlockSpec((tm, d), lambda k, i: (i, 0)), pl.BlockSpec((1, d, nb), lambda k, i: (k, 0, 0))],
        out_specs=pl.BlockSpec((tm, nb), lambda k, i: (i, k)), out_shape=jax.ShapeDtypeStruct((t, nk * nb), BF16),
        compiler_params=_params("parallel", "parallel"))(xn, win4)


def _causal_mask():
    row = lax.broadcasted_iota(jnp.int32, (CHUNK, CHUNK), 0)
    col = lax.broadcasted_iota(jnp.int32, (CHUNK, CHUNK), 1)
    return row >= col


def _layernorm_parts(v):
    mu = jnp.mean(v, axis=-1, keepdims=True)
    vc = v - mu
    r = lax.rsqrt(jnp.mean(vc * vc, axis=-1, keepdims=True) + EPS)
    return vc * r, r


def sgu_fwd(z, norm_g, sgu_w, sgu_b3, name):
    t = z.shape[0]
    d = norm_g.shape[1]
    ng = sgu_w.shape[0]
    dg = d // ng
    tm = _tile(t, 256)

    def body(zu_ref, zv_ref, ng_ref, w_ref, b_ref, a_ref):
        vhat, _ = _layernorm_parts(_gelu(zv_ref[...].astype(F32)))
        vn = (vhat * ng_ref[...]).astype(BF16)
        u = _gelu(zu_ref[...].astype(F32))
        mask = _causal_mask()
        for g in range(ng):
            wg = jnp.where(mask, w_ref[g], 0.0).astype(BF16)
            for ci in range(tm // CHUNK):
                rs, cs = slice(ci * CHUNK, (ci + 1) * CHUNK), slice(g * dg, (g + 1) * dg)
                sv = _dot(wg, vn[rs, cs]) + b_ref[g]
                a_ref[rs, cs] = (u[rs, cs] * sv).astype(BF16)

    return pl.pallas_call(
        body, name=name, grid=(t // tm,),
        in_specs=[_rows(tm, d, 0), _rows(tm, d, 1), _const((1, d)), _const((ng, CHUNK, CHUNK)), _const((ng, CHUNK, 1))],
        out_specs=_rows(tm, d), out_shape=jax.ShapeDtypeStruct((t, d), BF16),
        compiler_params=_params("parallel"))(z, z, norm_g, sgu_w, sgu_b3)


def pool_fwd(z, pool_w, pool_scale, name):
    t = z.shape[0]
    d = pool_scale.shape[1]
    ng = pool_w.shape[0]
    dg = d // ng
    tm = _tile(t, 256)
    per = tm // HALO

    def body(c_ref, prev_ref, w_ref, s_ref, diff_ref, b_ref):
        i = pl.program_id(0)
        cur = c_ref[...].astype(F32)
        prev = jnp.where(i > 0, prev_ref[...].astype(F32), 0.0)
        ext = jnp.concatenate([prev, cur], axis=0)
        tok = i * tm + lax.broadcasted_iota(jnp.int32, (tm, 1), 0)
        for g, win in enumerate(POOL_WINDOWS):
            cs = slice(g * dg, (g + 1) * dg)
            s = ext[:, cs]
            sh = 1
            while sh < win:
                s = s + pltpu.roll(s, sh, 0)
                sh *= 2
            per_count = 1.0 / jnp.minimum(tok + 1, win).astype(F32)
            diff = (s[HALO:] * per_count - cur[:, cs]).astype(BF16)
            diff_ref[:, cs] = diff
            b_ref[:, cs] = (_dot(diff, w_ref[g]) * s_ref[:, cs]).astype(BF16)

    return pl.pallas_call(
        body, name=name, grid=(t // tm,),
        in_specs=[_rows(tm, d, 2), pl.BlockSpec((HALO, d), lambda i: (jnp.maximum(i * per - 1, 0), 2)),
                  _const((ng, dg, dg)), _const((1, d))],
        out_specs=[_rows(tm, d)] * 2, out_shape=[jax.ShapeDtypeStruct((t, d), BF16)] * 2,
        compiler_params=_params("parallel"))(z, z, pool_w, pool_scale)


def mixer_y(a, b, z, woa, wob, name):
    t, d = a.shape
    tm = _tile(t, 256)

    def body(a_ref, b_ref, ga_ref, gb_ref, wa_ref, wb_ref, ya_ref, yb_ref, y_ref):
        ya = _dot(a_ref[...], wa_ref[...])
        yb = _dot(b_ref[...], wb_ref[...])
        ya_ref[...] = ya.astype(BF16)
        yb_ref[...] = yb.astype(BF16)
        y_ref[...] = (_sigmoid(ga_ref[...].astype(F32)) * ya + _sigmoid(gb_ref[...].astype(F32)) * yb).astype(BF16)

    return pl.pallas_call(
        body, name=name, grid=(t // tm,),
        in_specs=[_rows(tm, d), _rows(tm, d), _rows(tm, d, 3), _rows(tm, d, 4), _const((d, d)), _const((d, d))],
        out_specs=[_rows(tm, d)] * 3, out_shape=[jax.ShapeDtypeStruct((t, d), BF16)] * 3,
        compiler_params=_params("parallel"))(a, b, z, z, woa, wob)


def ple_loss(xn, p, wpg, wpp, h, g_post, g_pre, target, name):
    t, d = xn.shape
    dp = p.shape[1]
    tm = _tile(t, 256)

    def body(x_ref, p_ref, wg_ref, wp_ref, h_ref, gp_ref, gn_ref, tg_ref, ds_ref, de_ref, dhp_ref, dgp_ref, dgn_ref, loss_ref):
        gate = _sigmoid(_dot(x_ref[...], wg_ref[...]))
        e = _dot(p_ref[...], wp_ref[...])
        q = gate * e
        hv = h_ref[...]
        err = hv + q * _rstd(q) * gp_ref[...] - tg_ref[...]
        _accumulate(loss_ref, jnp.full(loss_ref.shape, (0.5 / d) * jnp.sum(err * err), F32))
        dhv = err * (1.0 / d)
        dq, dgp = _rms_bwd(q, gp_ref[...], dhv)
        ds = (dq * e * gate * (1.0 - gate)).astype(BF16)
        ds_ref[...] = ds
        de_ref[...] = (dq * gate).astype(BF16)
        dx, dgn = _rms_bwd(hv, gn_ref[...], _dot_nt(ds, wg_ref[...]))
        dhp_ref[...] = dhv + dx
        _accumulate(dgp_ref, dgp)
        _accumulate(dgn_ref, dgn)

    return pl.pallas_call(
        body, name=name, grid=(t // tm,),
        in_specs=[_rows(tm, d), _rows(tm, dp), _const((d, d)), _const((dp, d)), _rows(tm, d), _const((1, d)), _const((1, d)),
                  _rows(tm, d)],
        out_specs=[_rows(tm, d)] * 3 + [_const((1, d))] * 2 + [_const((8, 128))],
        out_shape=[jax.ShapeDtypeStruct((t, d), BF16), jax.ShapeDtypeStruct((t, d), BF16), jax.ShapeDtypeStruct((t, d), F32),
                   jax.ShapeDtypeStruct((1, d), F32), jax.ShapeDtypeStruct((1, d), F32), jax.ShapeDtypeStruct((8, 128), F32)],
        compiler_params=_params("arbitrary"))(xn, p, wpg, wpp, h, g_post, g_pre, target)


def ffn_bwd_a(dh, f, g_post, wd4, s4, t4, scale, name, deps=()):
    t, d = dh.shape
    nk, fk, _ = wd4.shape
    tm = _tile(t, 256)

    def body(dh_ref, f_ref, gp_ref, w_ref, s_ref, t_ref, df_ref, dg_ref, du_ref, dgp_ref):
        df, dgp = _rms_bwd(f_ref[...].astype(F32), gp_ref[...], dh_ref[...])
        df = (scale * df).astype(BF16)
        df_ref[...] = df
        _accumulate(dgp_ref, scale * dgp)
        for k in range(nk):
            da = _dot_nt(df, w_ref[k])
            du_ref[k] = (da * s_ref[k].astype(F32)).astype(BF16)
            dg_ref[k] = (da * t_ref[k].astype(F32)).astype(BF16)

    return pl.pallas_call(
        _ordered_after(body, 6, deps), name=name, grid=(t // tm,),
        in_specs=[_rows(tm, d), _rows(tm, d), _const((1, d)), _const((nk, fk, d)), _kmajor(nk, tm, fk), _kmajor(nk, tm, fk)]
        + [ANY_SPEC] * len(deps),
        out_specs=[_rows(tm, d), _kmajor(nk, tm, fk), _kmajor(nk, tm, fk), _const((1, d))],
        out_shape=[jax.ShapeDtypeStruct((t, d), BF16), jax.ShapeDtypeStruct((nk, t, fk), BF16),
                   jax.ShapeDtypeStruct((nk, t, fk), BF16), jax.ShapeDtypeStruct((1, d), F32)],
        compiler_params=_params("arbitrary"))(dh, f, g_post, wd4, s4, t4, *deps)


def dx_norm_bwd(pairs, h, g_pre, dh_in, name, deps=()):
    t, d = h.shape
    tm = _tile(t, 256)
    n = len(pairs)

    def body(*refs):
        dys, ws = refs[:n], refs[n:2 * n]
        h_ref, g_ref, dhi_ref, dho_ref, dg_ref = refs[2 * n:]
        acc = None
        for (_, w4, sections), dy_ref, w_ref in zip(pairs, dys, ws):
            if sections:
                wide = w4.shape[2]
                edges = sorted(set(range(0, 5 * d + 1, d)) | set(range(0, 5 * d + 1, wide)))
                parts = [_dot_nt(dy_ref[DZ_SLOT[lo // d], :, lo % d:lo % d + hi - lo], w_ref[lo // wide, :, lo % wide:lo % wide + hi - lo])
                         for lo, hi in zip(edges[:-1], edges[1:])]
            else:
                parts = [_dot(dy_ref[k], w_ref[k]) for k in range(w4.shape[0])]
            for part in parts:
                acc = part if acc is None else acc + part
        dx, dg = _rms_bwd(h_ref[...], g_ref[...], acc)
        dho_ref[...] = dhi_ref[...] + dx
        _accumulate(dg_ref, dg)

    dy_specs = [_kmajor(dy.shape[0], tm, dy.shape[2]) for dy, _, _ in pairs]
    return pl.pallas_call(
        _ordered_after(body, 2 * n + 3, deps), name=name, grid=(t // tm,),
        in_specs=dy_specs + [_const(w4.shape) for _, w4, _ in pairs] + [_rows(tm, d), _const((1, d)), _rows(tm, d)]
        + [ANY_SPEC] * len(deps),
        out_specs=[_rows(tm, d), _const((1, d))],
        out_shape=[jax.ShapeDtypeStruct((t, d), F32), jax.ShapeDtypeStruct((1, d), F32)],
        compiler_params=_params("arbitrary"))(*[dy for dy, _, _ in pairs], *[w4 for _, w4, _ in pairs], h, g_pre, dh_in, *deps)


def mixer_bwd_y(dh, m, g_post, w_o, ya, yb, z, woa, wob, name, deps=()):
    t, d = dh.shape
    tm = _tile(t, 256)

    def body(dh_ref, m_ref, gp_ref, wo_ref, ya_ref, yb_ref, ga_ref, gb_ref, wa_ref, wb_ref,
             dm_ref, dya_ref, dyb_ref, dz_ref, da_ref, db_ref, dgp_ref):
        dm, dgp = _rms_bwd(m_ref[...].astype(F32), gp_ref[...], dh_ref[...])
        dm = dm.astype(BF16)
        dm_ref[...] = dm
        _accumulate(dgp_ref, dgp)
        dy = _dot_nt(dm, wo_ref[...])
        sa = _sigmoid(ga_ref[...].astype(F32))
        sb = _sigmoid(gb_ref[...].astype(F32))
        dya = (dy * sa).astype(BF16)
        dyb = (dy * sb).astype(BF16)
        dya_ref[...] = dya
        dyb_ref[...] = dyb
        dz_ref[0] = (dy * ya_ref[...].astype(F32) * sa * (1.0 - sa)).astype(BF16)
        dz_ref[1] = (dy * yb_ref[...].astype(F32) * sb * (1.0 - sb)).astype(BF16)
        da_ref[...] = _dot_nt(dya, wa_ref[...]).astype(BF16)
        db_ref[...] = _dot_nt(dyb, wb_ref[...]).astype(BF16)

    return pl.pallas_call(
        _ordered_after(body, 10, deps), name=name, grid=(t // tm,),
        in_specs=[_rows(tm, d), _rows(tm, d), _const((1, d)), _const((d, d)), _rows(tm, d), _rows(tm, d),
                  _rows(tm, d, 3), _rows(tm, d, 4), _const((d, d)), _const((d, d))] + [ANY_SPEC] * len(deps),
        out_specs=[_rows(tm, d)] * 3 + [pl.BlockSpec((2, tm, d), lambda i: (0, i, 0))] + [_rows(tm, d)] * 2 + [_const((1, d))],
        out_shape=[jax.ShapeDtypeStruct((t, d), BF16)] * 3 + [jax.ShapeDtypeStruct((5, t, d), BF16)]
        + [jax.ShapeDtypeStruct((t, d), BF16)] * 2 + [jax.ShapeDtypeStruct((1, d), F32)],
        compiler_params=_params("arbitrary"))(dh, m, g_post, w_o, ya, yb, z, z, woa, wob, *deps)


def sgu_bwd(z, da, dz, norm_g, sgu_w, sgu_b3, name):
    t, d = da.shape
    ng = sgu_w.shape[0]
    dg = d // ng
    tm = _tile(t, 256)
    steps = t // tm

    def body(zu_ref, zv_ref, da_ref, ng_ref, w_ref, b_ref, _, dz_ref, dw_ref, db_ref, dng_ref, dvn_ref, dsv_ref):
        i = pl.program_id(0)
        zv = zv_ref[...].astype(F32)
        zu = zu_ref[...].astype(F32)
        v, gv = _gelu_and_grad(zv)
        vhat, r = _layernorm_parts(v)
        gain = ng_ref[...]
        vn = (vhat * gain).astype(BF16)
        u, gu = _gelu_and_grad(zu)
        dav = da_ref[...].astype(F32)
        mask = _causal_mask()

        @pl.when(i == 0)
        def _():
            dw_ref[...] = jnp.zeros_like(dw_ref)
            dsv_ref[...] = jnp.zeros_like(dsv_ref)

        for g in range(ng):
            wg = jnp.where(mask, w_ref[g], 0.0).astype(BF16)
            dw = jnp.zeros((CHUNK, CHUNK), F32)
            dsv_sum = jnp.zeros((CHUNK, dg), F32)
            for ci in range(tm // CHUNK):
                rs, cs = slice(ci * CHUNK, (ci + 1) * CHUNK), slice(g * dg, (g + 1) * dg)
                vn_blk = vn[rs, cs]
                sv = _dot(wg, vn_blk) + b_ref[g]
                dz_ref[0, rs, cs] = (dav[rs, cs] * sv * gu[rs, cs]).astype(BF16)
                dsv = dav[rs, cs] * u[rs, cs]
                dsv_sum += dsv
                dsv = dsv.astype(BF16)
                dw += _dot_nt(dsv, vn_blk)
                dvn_ref[rs, cs] = _dot_tn(wg, dsv)
            dw_ref[g] += dw
            dsv_ref[:, cs] += dsv_sum

        dvn = dvn_ref[...]
        _accumulate(dng_ref, jnp.sum(dvn * vhat, axis=0, keepdims=True))
        dvh = dvn * gain
        dv = r * (dvh - jnp.mean(dvh, axis=-1, keepdims=True) - vhat * jnp.mean(dvh * vhat, axis=-1, keepdims=True))
        dz_ref[1] = (dv * gv).astype(BF16)

        @pl.when(i == steps - 1)
        def _():
            for g in range(ng):
                dw_ref[g] = jnp.where(mask, dw_ref[g], 0.0)
                row_sum = jnp.sum(dsv_ref[:, g * dg:(g + 1) * dg], axis=1, keepdims=True)
                db_ref[g] = jnp.broadcast_to(row_sum, (CHUNK, CHUNK))

    return pl.pallas_call(
        body, name=name, grid=(steps,),
        in_specs=[_rows(tm, d, 0), _rows(tm, d, 1), _rows(tm, d), _const((1, d)), _const((ng, CHUNK, CHUNK)), _const((ng, CHUNK, 1)),
                  ANY_SPEC],
        out_specs=[pl.BlockSpec((2, tm, d), lambda i: (DZ_SLOT[0] // 2, i, 0)), _const((ng, CHUNK, CHUNK)),
                   _const((ng, CHUNK, CHUNK)), _const((1, d))],
        out_shape=[jax.ShapeDtypeStruct(dz.shape, BF16), jax.ShapeDtypeStruct((ng, CHUNK, CHUNK), F32),
                   jax.ShapeDtypeStruct((ng, CHUNK, CHUNK), F32), jax.ShapeDtypeStruct((1, d), F32)],
        scratch_shapes=[pltpu.VMEM((tm, d), F32), pltpu.VMEM((CHUNK, d), F32)], input_output_aliases={6: 0},
        compiler_params=_params("arbitrary"))(z, z, da, norm_g, sgu_w, sgu_b3, dz)


def pool_bwd(db, diff, dz, pool_w, pool_scale, name):
    t, d = db.shape
    ng = pool_w.shape[0]
    dg = d // ng
    tm = _tile(t, 256)
    per = tm // HALO
    steps = t // tm

    def body(db_ref, next_ref, diff_ref, w_ref, s_ref, _, dc_ref, dw_ref, ds_ref):
        i = pl.program_id(0)
        dbc = db_ref[...].astype(F32)
        nxt = jnp.where(i < steps - 1, next_ref[...].astype(F32), 0.0)
        ext = jnp.concatenate([dbc, nxt], axis=0)
        rows = tm + HALO
        tok = i * tm + lax.broadcasted_iota(jnp.int32, (rows, 1), 0)

        @pl.when(i == 0)
        def _():
            dw_ref[...] = jnp.zeros_like(dw_ref)
            ds_ref[...] = jnp.zeros_like(ds_ref)

        for g, win in enumerate(POOL_WINDOWS):
            cs = slice(g * dg, (g + 1) * dg)
            dp = (ext[:, cs] * s_ref[:, cs]).astype(BF16)
            dd = _dot_nt(dp, w_ref[g])
            s = dd * (1.0 / jnp.minimum(tok + 1, win).astype(F32))
            sh = 1
            while sh < win:
                s = s + pltpu.roll(s, rows - sh, 0)
                sh *= 2
            dc_ref[0, :, cs] = (s[:tm] - dd[:tm]).astype(BF16)
            dfg = diff_ref[:, cs]
            ds_ref[:, cs] += jnp.sum(dbc[:, cs] * _dot(dfg, w_ref[g]), axis=0, keepdims=True)
            dw_ref[g] += _dot_tn(dfg, dp[:tm])

    return pl.pallas_call(
        body, name=name, grid=(steps,),
        in_specs=[_rows(tm, d), pl.BlockSpec((HALO, d), lambda i: (jnp.minimum((i + 1) * per, t // HALO - 1), 0)),
                  _rows(tm, d), _const((ng, dg, dg)), _const((1, d)), ANY_SPEC],
        out_specs=[pl.BlockSpec((1, tm, d), lambda i: (DZ_SLOT[2], i, 0)), _const((ng, dg, dg)), _const((1, d))],
        out_shape=[jax.ShapeDtypeStruct(dz.shape, BF16), jax.ShapeDtypeStruct((ng, dg, dg), F32), jax.ShapeDtypeStruct((1, d), F32)],
        input_output_aliases={5: 0},
        compiler_params=_params("arbitrary"))(db, db, diff, pool_w, pool_scale, dz)


def dw_tn(x, dy, nk, name, x_kmajor=False, dy_mode="same", deps=()):
    t = x.shape[-2]
    kx = x.shape[-1]
    n = dy.shape[-1] // nk if dy_mode == "cols" else dy.shape[-1]
    tt = _tile(t, DW_TOKENS)
    steps = t // tt

    def body(x_ref, dy_ref, o_ref, acc_ref):
        s = pl.program_id(1)
        part = _dot_tn(x_ref[0] if x_kmajor else x_ref[...], dy_ref[0] if dy_mode == "kmajor" else dy_ref[...])
        if steps == 1:
            o_ref[0] = part.astype(BF16)
            return

        @pl.when(s == 0)
        def _():
            acc_ref[...] = jnp.zeros_like(acc_ref)

        acc_ref[...] += part

        @pl.when(s == steps - 1)
        def _():
            o_ref[0] = acc_ref[...].astype(BF16)

    x_spec = pl.BlockSpec((1, tt, kx), lambda k, s: (k, s, 0)) if x_kmajor else pl.BlockSpec((tt, kx), lambda k, s: (s, 0))
    dy_spec = {"kmajor": pl.BlockSpec((1, tt, n), lambda k, s: (k, s, 0)), "cols": pl.BlockSpec((tt, n), lambda k, s: (s, k)),
               "same": pl.BlockSpec((tt, n), lambda k, s: (s, 0))}[dy_mode]
    return pl.pallas_call(
        _ordered_after(body, 2, deps), name=name, grid=(nk, steps), in_specs=[x_spec, dy_spec] + [ANY_SPEC] * len(deps),
        out_specs=pl.BlockSpec((1, kx, n), lambda k, s: (k, 0, 0)), out_shape=jax.ShapeDtypeStruct((nk, kx, n), BF16),
        scratch_shapes=[pltpu.VMEM((kx, n) if steps > 1 else (8, 128), F32)],
        compiler_params=_params("parallel", "arbitrary"))(x, dy, *deps)


def dw_in_tiles(xn, dz, nk, name):
    t, d = xn.shape
    sections = len(DZ_SLOT)
    per_section = d // DW_IN_TILE
    per_shard = sections * per_section // nk

    def body(x_ref, dy_ref, o_ref):
        o_ref[0] = _dot_tn(x_ref[...], dy_ref[0]).astype(BF16)

    def slot(j):
        return (j // per_section + DZ_SLOT[0]) % sections

    return pl.pallas_call(
        body, name=name, grid=(sections * per_section,),
        in_specs=[pl.BlockSpec((t, d), lambda j: (0, 0)), pl.BlockSpec((1, t, DW_IN_TILE), lambda j: (slot(j), 0, j % per_section))],
        out_specs=pl.BlockSpec((1, d, DW_IN_TILE), lambda j: (j // per_shard, 0, j % per_shard)),
        out_shape=jax.ShapeDtypeStruct((nk, d, sections * d // nk), BF16), compiler_params=_params("parallel"))(xn, dz)


def _place():
    x, y, c = lax.axis_index("x"), lax.axis_index("y"), lax.axis_index("c")
    chips = [((1 - x) if fx else x, (1 - y) if fy else y) for fx, fy in OTHER_CHIPS]
    return x, y, c, chips


def _sibling_handshake(x, y, c):
    barrier = pltpu.get_barrier_semaphore()
    pl.semaphore_signal(barrier, inc=1, device_id=(x, y, 1 - c), device_id_type=MESH_IDS)
    pl.semaphore_wait(barrier, 1)


def _chips_handshake(c, chips):
    barrier = pltpu.get_barrier_semaphore()
    for cx, cy in chips:
        pl.semaphore_signal(barrier, inc=1, device_id=(cx, cy, c), device_id_type=MESH_IDS)
    pl.semaphore_wait(barrier, len(chips))


def _half(rows, which):
    return pl.ds(pl.multiple_of(which * (rows // 2), 16), rows // 2)


def _hbm(a):
    return pltpu.with_memory_space_constraint(a, pltpu.HBM)


def _by_shape(arrays):
    buckets = {}
    for i, a in enumerate(arrays):
        buckets.setdefault(a.shape, []).append(i)
    return list(buckets.values())


def cast_place(shards, chip, name, deps=(), plain=False):
    n = len(shards)

    def body(chip_ref, *refs):
        outs = refs[n + len(deps):]
        for a, w_ref in enumerate(refs[:n]):
            cast = w_ref[...].astype(BF16)
            outs[a][0] = cast
            if plain:
                outs[n + a][...] = cast

    def rows(s):
        return (s.shape[0] // ROW_STEPS, s.shape[1])

    out = pl.pallas_call(
        body, name=name,
        grid_spec=pltpu.PrefetchScalarGridSpec(
            num_scalar_prefetch=1, grid=(ROW_STEPS,),
            in_specs=[pl.BlockSpec(rows(s), lambda i, chip_ref: (i, 0)) for s in shards] + [ANY_SPEC] * len(deps),
            out_specs=[pl.BlockSpec((1,) + rows(s), lambda i, chip_ref: (chip_ref[0], i, 0)) for s in shards]
            + [pl.BlockSpec(rows(s), lambda i, chip_ref: (i, 0)) for s in shards] * plain),
        out_shape=[jax.ShapeDtypeStruct((N_CHIPS,) + s.shape, BF16) for s in shards]
        + [jax.ShapeDtypeStruct(s.shape, BF16) for s in shards] * plain,
        compiler_params=_params("parallel"))(chip, *shards, *deps)
    return (out[:n], out[n:]) if plain else out


def _gather_copy(buf, sends, recvs, i, j, me, chip_xy, c):
    cx, cy = chip_xy
    mine = _half(buf.shape[1], c)
    return pltpu.make_async_remote_copy(
        src_ref=buf.at[me, mine], dst_ref=buf.at[me, mine], send_sem=sends.at[3 * i + j], recv_sem=recvs.at[3 * i + j],
        device_id=(cx, cy, c), device_id_type=MESH_IDS)


def allgather_start(bufs, name, barrier):
    n = len(bufs)

    def body(*refs):
        ins = refs[:n]
        sends, recvs = refs[n], refs[n + 1]
        token = refs[2 * n + 2]
        x, y, c, chips = _place()
        _chips_handshake(c, chips)
        for i in range(n):
            for j, chip_xy in enumerate(chips):
                _gather_copy(ins[i], sends, recvs, i, j, 2 * x + y, chip_xy, c).start()
        token[...] = jnp.zeros_like(token)

    out = pl.pallas_call(
        body, name=name, in_specs=[HBM_SPEC] * n,
        out_specs=[SEM_SPEC, SEM_SPEC] + [HBM_SPEC] * n + [VMEM_SPEC],
        out_shape=[pltpu.SemaphoreType.DMA((3 * n,)), pltpu.SemaphoreType.DMA((3 * n,))]
        + [pltpu.HBM(b.shape, b.dtype) for b in bufs] + [jax.ShapeDtypeStruct((8, 128), F32)],
        input_output_aliases={i: i + 2 for i in range(n)},
        compiler_params=pltpu.CompilerParams(has_side_effects=DATAFLOW, collective_id=barrier))(*[_hbm(b) for b in bufs])
    return out[0], out[1], list(out[2:2 + n]), out[2 + n]


def allgather_wait(sends, recvs, bufs, after, name, first=0):
    n = len(bufs)
    after = tuple(after) if isinstance(after, (tuple, list)) else (after,)

    def body(*refs):
        ins = refs[:n]
        send_sems, recv_sems = refs[n], refs[n + 1]
        x, y, c, chips = _place()
        for i in range(n):
            for j, (cx, cy) in enumerate(chips):
                mine = _half(ins[i].shape[1], c)
                cp = pltpu.make_async_remote_copy(
                    src_ref=ins[i].at[2 * x + y, mine], dst_ref=ins[i].at[2 * cx + cy, mine],
                    send_sem=send_sems.at[3 * (first + i) + j], recv_sem=recv_sems.at[3 * (first + i) + j],
                    device_id=(cx, cy, c), device_id_type=MESH_IDS)
                cp.wait_send()
                cp.wait_recv()

    return pl.pallas_call(
        body, name=name, in_specs=[HBM_SPEC] * n + [SEM_SPEC, SEM_SPEC] + [ANY_SPEC] * len(after), out_specs=[HBM_SPEC] * n,
        out_shape=[pltpu.HBM(b.shape, b.dtype) for b in bufs], input_output_aliases={i: i for i in range(n)},
        compiler_params=pltpu.CompilerParams(has_side_effects=DATAFLOW))(*bufs, sends, recvs, *after)


def d2d_forward(bufs, name):
    n = len(bufs)

    def body(*refs):
        ins = refs[:n]
        send_sems, recv_sems = refs[2 * n:]
        x, y, c, chips = _place()
        _sibling_handshake(x, y, c)
        copies = []
        for i in range(n):
            mine = _half(ins[i].shape[1], c)
            for j, (cx, cy) in enumerate(chips):
                landed = ins[i].at[2 * cx + cy, mine]
                cp = pltpu.make_async_remote_copy(
                    src_ref=landed, dst_ref=landed, send_sem=send_sems.at[i, j], recv_sem=recv_sems.at[i, j],
                    device_id=(x, y, 1 - c), device_id_type=MESH_IDS)
                cp.start()
                copies.append(cp)
        for i in range(n):
            theirs = _half(ins[i].shape[1], 1 - c)
            for j, (cx, cy) in enumerate(chips):
                passed = ins[i].at[2 * cx + cy, theirs]
                pltpu.make_async_remote_copy(
                    src_ref=passed, dst_ref=passed, send_sem=send_sems.at[i, j], recv_sem=recv_sems.at[i, j],
                    device_id=(x, y, 1 - c), device_id_type=MESH_IDS).wait_recv()
        for cp in copies:
            cp.wait_send()

    return pl.pallas_call(
        body, name=name, in_specs=[HBM_SPEC] * n, out_specs=[HBM_SPEC] * n,
        out_shape=[jax.ShapeDtypeStruct(b.shape, b.dtype) for b in bufs], input_output_aliases={i: i for i in range(n)},
        scratch_shapes=[pltpu.SemaphoreType.DMA((n, 3))] * 2,
        compiler_params=pltpu.CompilerParams(has_side_effects=True, collective_id=SIBLING_BARRIER))(*bufs)


def _forward_copy(buf, sends, recvs, i, j, chip_xy, x, y, c):
    cx, cy = chip_xy
    rows = buf.shape[1]
    return pltpu.make_async_remote_copy(
        src_ref=buf.at[2 * cx + cy, _half(rows, c)], dst_ref=buf.at[2 * cx + cy, _half(rows, 1 - c)],
        send_sem=sends.at[3 * i + j], recv_sem=recvs.at[3 * i + j], device_id=(x, y, 1 - c), device_id_type=MESH_IDS)


def d2d_forward_start(bufs, name):
    n = len(bufs)

    def body(*refs):
        ins = refs[:n]
        sends, recvs = refs[n], refs[n + 1]
        token = refs[2 * n + 2]
        x, y, c, chips = _place()
        _sibling_handshake(x, y, c)
        for i in range(n):
            mine = _half(ins[i].shape[1], c)
            for j, (cx, cy) in enumerate(chips):
                landed = ins[i].at[2 * cx + cy, mine]
                pltpu.make_async_remote_copy(
                    src_ref=landed, dst_ref=landed, send_sem=sends.at[3 * i + j], recv_sem=recvs.at[3 * i + j],
                    device_id=(x, y, 1 - c), device_id_type=MESH_IDS).start()
        token[...] = jnp.zeros_like(token)

    out = pl.pallas_call(
        body, name=name, in_specs=[HBM_SPEC] * n,
        out_specs=[SEM_SPEC, SEM_SPEC] + [HBM_SPEC] * n + [VMEM_SPEC],
        out_shape=[pltpu.SemaphoreType.DMA((3 * n,)), pltpu.SemaphoreType.DMA((3 * n,))]
        + [pltpu.HBM(b.shape, b.dtype) for b in bufs] + [jax.ShapeDtypeStruct((8, 128), F32)],
        input_output_aliases={i: i + 2 for i in range(n)},
        compiler_params=pltpu.CompilerParams(has_side_effects=DATAFLOW, collective_id=SIBLING_BARRIER))(*[_hbm(b) for b in bufs])
    return (out[0], out[1], list(out[2:2 + n])), out[2 + n]


def d2d_forward_wait(started, after, name):
    sends, recvs, bufs = started
    n = len(bufs)

    def body(*refs):
        ins = refs[:n]
        send_sems, recv_sems = refs[n], refs[n + 1]
        x, y, c, chips = _place()
        for i in range(n):
            for j, chip_xy in enumerate(chips):
                cp = _forward_copy(ins[i], send_sems, recv_sems, i, j, chip_xy, x, y, c)
                cp.wait_send()
                cp.wait_recv()

    return pl.pallas_call(
        body, name=name, in_specs=[HBM_SPEC] * n + [SEM_SPEC, SEM_SPEC, ANY_SPEC], out_specs=[HBM_SPEC] * n,
        out_shape=[pltpu.HBM(b.shape, b.dtype) for b in bufs], input_output_aliases={i: i for i in range(n)},
        compiler_params=pltpu.CompilerParams(has_side_effects=DATAFLOW))(*bufs, sends, recvs, after)


def _sibling_copy(src, land, sends, recvs, i, x, y, c, halves):
    part = src.at[:, _half(src.shape[1], 1 - c)] if halves else src
    return pltpu.make_async_remote_copy(
        src_ref=part, dst_ref=land, send_sem=sends.at[i], recv_sem=recvs.at[i], device_id=(x, y, 1 - c),
        device_id_type=MESH_IDS)


def sibling_start(arrays, halves, name):
    n = len(arrays)
    lands = [lax.empty((a.shape[0], a.shape[1] // 2, a.shape[2]) if halves else a.shape, a.dtype) for a in arrays]

    def body(*refs):
        srcs, zones = refs[:n], refs[n:2 * n]
        sends, recvs = refs[2 * n], refs[2 * n + 1]
        token = refs[4 * n + 2]
        x, y, c, _ = _place()
        _sibling_handshake(x, y, c)
        for i in range(n):
            _sibling_copy(srcs[i], zones[i], sends, recvs, i, x, y, c, halves).start()
        token[...] = jnp.zeros_like(token)

    out = pl.pallas_call(
        body, name=name, in_specs=[HBM_SPEC] * (2 * n),
        out_specs=[SEM_SPEC, SEM_SPEC] + [HBM_SPEC] * (2 * n) + [VMEM_SPEC],
        out_shape=[pltpu.SemaphoreType.DMA((n,)), pltpu.SemaphoreType.DMA((n,))]
        + [pltpu.HBM(a.shape, a.dtype) for a in arrays + lands] + [jax.ShapeDtypeStruct((8, 128), F32)],
        input_output_aliases={i: i + 2 for i in range(2 * n)},
        compiler_params=pltpu.CompilerParams(has_side_effects=DATAFLOW, collective_id=SIBLING_BARRIER))(
            *[_hbm(a) for a in arrays + lands])
    return (out[0], out[1], list(out[2:2 + n]), list(out[2 + n:2 + 2 * n])), out[2 + 2 * n]


def sibling_wait(started, halves, after, name):
    sends, recvs, arrays, lands = started
    n = len(arrays)
    after = tuple(after) if isinstance(after, (tuple, list)) else (after,)

    def body(*refs):
        srcs, zones = refs[:n], refs[n:2 * n]
        send_sems, recv_sems = refs[2 * n], refs[2 * n + 1]
        x, y, c, _ = _place()
        for i in range(n):
            cp = _sibling_copy(srcs[i], zones[i], send_sems, recv_sems, i, x, y, c, halves)
            cp.wait_send()
            cp.wait_recv()

    out = pl.pallas_call(
        body, name=name, in_specs=[HBM_SPEC] * (2 * n) + [SEM_SPEC, SEM_SPEC] + [ANY_SPEC] * len(after),
        out_specs=[HBM_SPEC] * (2 * n),
        out_shape=[pltpu.HBM(a.shape, a.dtype) for a in arrays + lands], input_output_aliases={i: i for i in range(2 * n)},
        compiler_params=pltpu.CompilerParams(has_side_effects=DATAFLOW))(*arrays, *lands, sends, recvs, *after)
    return list(out[:n]), list(out[n:])


def _scatter_copy(src, land, sends, recvs, i, j, chip_xy, c):
    cx, cy = chip_xy
    return pltpu.make_async_remote_copy(
        src_ref=src.at[2 * cx + cy], dst_ref=land.at[j], send_sem=sends.at[3 * i + j], recv_sem=recvs.at[3 * i + j],
        device_id=(cx, cy, c), device_id_type=MESH_IDS)


def scatter_start(sums, name, barrier):
    n = len(sums)
    lands = [lax.empty((3,) + s.shape[1:], s.dtype) for s in sums]

    def body(*refs):
        srcs, zones = refs[:n], refs[n:2 * n]
        sends, recvs = refs[2 * n], refs[2 * n + 1]
        token = refs[4 * n + 2]
        _, _, c, chips = _place()
        _chips_handshake(c, chips)
        for i in range(n):
            for j, chip_xy in enumerate(chips):
                _scatter_copy(srcs[i], zones[i], sends, recvs, i, j, chip_xy, c).start()
        token[...] = jnp.zeros_like(token)

    out = pl.pallas_call(
        body, name=name, in_specs=[HBM_SPEC] * (2 * n),
        out_specs=[SEM_SPEC, SEM_SPEC] + [HBM_SPEC] * (2 * n) + [VMEM_SPEC],
        out_shape=[pltpu.SemaphoreType.DMA((3 * n,)), pltpu.SemaphoreType.DMA((3 * n,))]
        + [pltpu.HBM(a.shape, a.dtype) for a in sums + lands] + [jax.ShapeDtypeStruct((8, 128), F32)],
        input_output_aliases={i: i + 2 for i in range(2 * n)},
        compiler_params=pltpu.CompilerParams(has_side_effects=DATAFLOW, collective_id=barrier))(*[_hbm(a) for a in sums + lands])
    return out[0], out[1], list(out[2:2 + n]), list(out[2 + n:2 + 2 * n]), out[2 + 2 * n]


def scatter_wait(sends, recvs, sums, lands, after, name):
    n = len(sums)

    def body(*refs):
        srcs, zones = refs[:n], refs[n:2 * n]
        send_sems, recv_sems = refs[2 * n], refs[2 * n + 1]
        _, _, c, chips = _place()
        for i in range(n):
            for j, chip_xy in enumerate(chips):
                cp = _scatter_copy(srcs[i], zones[i], send_sems, recv_sems, i, j, chip_xy, c)
                cp.wait_send()
                cp.wait_recv()

    out = pl.pallas_call(
        body, name=name, in_specs=[HBM_SPEC] * (2 * n) + [SEM_SPEC, SEM_SPEC, ANY_SPEC], out_specs=[HBM_SPEC] * (2 * n),
        out_shape=[pltpu.HBM(a.shape, a.dtype) for a in sums + lands], input_output_aliases={i: i for i in range(2 * n)},
        compiler_params=pltpu.CompilerParams(has_side_effects=DATAFLOW))(*sums, *lands, sends, recvs, after)
    return list(out[:n]), list(out[n:])


def add_halves(grads, recvs, core, name):
    n = len(grads)
    nk = grads[0].shape[0]
    views = [g.reshape(nk, 2, g.shape[1] // 2, g.shape[2]) for g in grads]

    def body(core_ref, *refs):
        for g_ref, r_ref, o_ref in zip(refs[:n], refs[n:2 * n], refs[2 * n:]):
            o_ref[0] = (g_ref[0, 0].astype(F32) + r_ref[0].astype(F32)).astype(BF16)

    return pl.pallas_call(
        body, name=name,
        grid_spec=pltpu.PrefetchScalarGridSpec(
            num_scalar_prefetch=1, grid=(nk,),
            in_specs=[pl.BlockSpec((1, 1) + v.shape[2:], lambda k, core_ref: (k, core_ref[0], 0, 0)) for v in views]
            + [pl.BlockSpec((1,) + r.shape[1:], lambda k, core_ref: (k, 0, 0)) for r in recvs],
            out_specs=[pl.BlockSpec((1,) + r.shape[1:], lambda k, core_ref: (k, 0, 0)) for r in recvs]),
        out_shape=[jax.ShapeDtypeStruct(r.shape, BF16) for r in recvs], compiler_params=_params("parallel"))(core, *views, *recvs)


def add_chips(sums, lands, chip, name):
    n = len(sums)

    def body(chip_ref, *refs):
        for s_ref, r_ref, o_ref in zip(refs[:n], refs[n:2 * n], refs[2 * n:]):
            o_ref[...] = ((s_ref[0].astype(F32) + r_ref[0].astype(F32)) + r_ref[1].astype(F32)) + r_ref[2].astype(F32)

    def rows(s):
        return (s.shape[1] // ROW_STEPS, s.shape[2])

    return pl.pallas_call(
        body, name=name,
        grid_spec=pltpu.PrefetchScalarGridSpec(
            num_scalar_prefetch=1, grid=(ROW_STEPS,),
            in_specs=[pl.BlockSpec((1,) + rows(s), lambda i, chip_ref: (chip_ref[0], i, 0)) for s in sums]
            + [pl.BlockSpec((3,) + rows(s), lambda i, chip_ref: (0, i, 0)) for s in sums],
            out_specs=[pl.BlockSpec(rows(s), lambda i, chip_ref: (i, 0)) for s in sums]),
        out_shape=[jax.ShapeDtypeStruct(s.shape[1:], F32) for s in sums], compiler_params=_params("parallel"))(chip, *sums, *lands)


def _adamw_math(w, g, m, v):
    m = ADAM_B1 * m + (1.0 - ADAM_B1) * g
    v = ADAM_B2 * v + (1.0 - ADAM_B2) * (g * g)
    m_hat = m / (1.0 - ADAM_B1 ** ADAM_STEP)
    v_hat = v / (1.0 - ADAM_B2 ** ADAM_STEP)
    return -ADAM_LR * (m_hat / (jnp.sqrt(v_hat) + ADAM_EPS) + ADAM_WD * w), m, v


def adamw_one_half(ws, gs, ms, vs, core, name, own, into=None):
    n = len(ws)
    r, cdim = ws[0].shape
    half = r // 2
    tr = _tile(half, max(8, 256 // n))
    steps = half // tr
    prior = () if into is None else tuple(a for group in into for a in group)

    def body(core_ref, *refs):
        ins, outs = refs[:4 * n], refs[4 * n + len(prior):]
        for a in range(n):
            w_ref, g_ref, m_ref, v_ref = ins[a::n]
            g = g_ref[...]
            outs[a][...] = g
            outs[n + a][...], outs[2 * n + a][...], outs[3 * n + a][...] = _adamw_math(w_ref[...], g, m_ref[...], v_ref[...])

    def rows(i, core_ref):
        return ((core_ref[0] if own else 1 - core_ref[0]) * steps + i, 0)

    whole = pl.BlockSpec((tr, cdim), rows)
    part = pl.BlockSpec((tr, cdim), lambda i, core_ref: (i, 0))
    out = pl.pallas_call(
        body, name=name,
        grid_spec=pltpu.PrefetchScalarGridSpec(
            num_scalar_prefetch=1, grid=(steps,),
            in_specs=[whole] * n + [part] * n + [whole] * (2 * n) + [ANY_SPEC] * len(prior), out_specs=[whole] * (4 * n)),
        out_shape=[jax.ShapeDtypeStruct((r, cdim), F32)] * (4 * n),
        input_output_aliases={1 + 4 * n + j: j for j in range(len(prior))},
        compiler_params=_params("parallel"))(core, *ws, *gs, *ms, *vs, *prior)
    return [out[k * n:(k + 1) * n] for k in range(4)]


def adamw_halves_sparsecore(ws, owns, others, ms, vs, name):
    n = len(ws)
    r, cdim = ws[0].shape
    half_groups = r // 2 // SC_ROWS
    per_tile = -(-2 * half_groups // SC_TILES)

    def body(*refs):
        ins, outs, (wb, gb, mb, vb) = refs[:5 * n], refs[5 * n:9 * n], refs[9 * n:]
        tile = lax.axis_index("sc_tile") * 2 + lax.axis_index("sc_core")
        core = lax.axis_index("c")

        @pl.loop(0, per_tile)
        def _(it):
            group = tile + SC_TILES * it

            @pl.when(group < 2 * half_groups)
            def _():
                rows = pl.ds(group * SC_ROWS, SC_ROWS)
                in_own = (group < half_groups) == (core == 0)
                half_rows = pl.ds((group % half_groups) * SC_ROWS, SC_ROWS)
                for a in range(n):
                    w_hbm, own_hbm, other_hbm, m_hbm, v_hbm = ins[a::n]
                    g_out, d_out, m_out, v_out = outs[a::n]
                    pltpu.sync_copy(w_hbm.at[rows], wb)
                    pltpu.sync_copy(m_hbm.at[rows], mb)
                    pltpu.sync_copy(v_hbm.at[rows], vb)

                    @pl.when(in_own)
                    def _():
                        pltpu.sync_copy(own_hbm.at[half_rows], gb)

                    @pl.when(jnp.logical_not(in_own))
                    def _():
                        pltpu.sync_copy(other_hbm.at[half_rows], gb)

                    pltpu.sync_copy(gb, g_out.at[rows])

                    @pl.loop(0, SC_ROWS)
                    def _(row):
                        @pl.loop(0, cdim, step=SC_LANES)
                        def _(col):
                            at = (row, pl.ds(col, SC_LANES))
                            wb[at], mb[at], vb[at] = _adamw_math(wb[at], gb[at], mb[at], vb[at])

                    pltpu.sync_copy(wb, d_out.at[rows])
                    pltpu.sync_copy(mb, m_out.at[rows])
                    pltpu.sync_copy(vb, v_out.at[rows])

    out = pl.kernel(
        body, name=name, out_type=[jax.ShapeDtypeStruct((r, cdim), F32)] * (4 * n),
        mesh=plsc.VectorSubcoreMesh(core_axis_name="sc_core", subcore_axis_name="sc_tile"),
        scratch_types=[pltpu.VMEM((SC_ROWS, cdim), F32)] * 4)(*ws, *owns, *others, *ms, *vs)
    return [out[a::n] for a in range(n)]


def small_allreduce_adamw(g, w, m, v):
    r, cdim = g.shape

    def body(g_ref, w_ref, m_ref, v_ref, go_ref, d_ref, mo_ref, vo_ref, pair, quad, d2d_send, d2d_recv, ici_send, ici_recv):
        x, y, c, chips = _place()
        me = 2 * x + y
        pair[c] = g_ref[...]
        swap = pltpu.make_async_remote_copy(
            src_ref=g_ref, dst_ref=pair.at[c], send_sem=d2d_send, recv_sem=d2d_recv, device_id=(x, y, 1 - c),
            device_id_type=MESH_IDS)
        swap.start()
        swap.wait()
        quad[0] = pair[0] + pair[1]
        for stage, (cx, cy) in enumerate(chips[:2]):
            swap = pltpu.make_async_remote_copy(
                src_ref=quad.at[2 * stage], dst_ref=quad.at[2 * stage + 1], send_sem=ici_send.at[stage],
                recv_sem=ici_recv.at[stage], device_id=(cx, cy, c), device_id_type=MESH_IDS)
            swap.start()
            swap.wait()
            if stage == 0:
                quad[2] = quad[0] + quad[1]
        total = quad[2] + quad[3]
        go_ref[...] = total
        d_ref[...], mo_ref[...], vo_ref[...] = _adamw_math(w_ref[...], total, m_ref[...], v_ref[...])

    return pl.pallas_call(
        body, name="small_allreduce_adamw", in_specs=[VMEM_SPEC] * 4, out_specs=[VMEM_SPEC] * 4,
        out_shape=[jax.ShapeDtypeStruct((r, cdim), F32)] * 4,
        scratch_shapes=[pltpu.VMEM((2, r, cdim), F32), pltpu.VMEM((N_CHIPS, r, cdim), F32), pltpu.SemaphoreType.DMA,
                        pltpu.SemaphoreType.DMA, pltpu.SemaphoreType.DMA((3,)), pltpu.SemaphoreType.DMA((3,))],
        compiler_params=pltpu.CompilerParams(has_side_effects=True, vmem_limit_bytes=VMEM_LIMIT_V7X))(g, w, m, v)


GROUPS = (
    ("ffn1", ("ffn1_w_gate", "ffn1_w_up", "ffn1_w_down")),
    ("mixer", ("w_in", "pool_w", "w_out_a", "w_out_b", "w_o")),
    ("ffn2", ("ffn2_w_gate", "ffn2_w_up", "ffn2_w_down")),
    ("ple", ("ple_w_gate", "ple_w_proj")),
)
GATHERS = (
    ("ffn1_in", ("ffn1_w_gate", "ffn1_w_up")),
    ("ffn1_out", ("ffn1_w_down",)),
) + GROUPS[1:]
GAINS = ("ffn1_pre_g", "ffn1_post_g", "mix_pre_g", "sgu_norm_g", "pool_scale", "mix_post_g",
         "ffn2_pre_g", "ffn2_post_g", "ple_pre_g", "ple_post_g")
SMALL = GAINS + ("sgu_b", "sgu_w")
PACKED = SMALL + ("loss",)
WEIGHTS = ("ffn1_pre_g", "ffn1_w_gate", "ffn1_w_up", "ffn1_w_down", "ffn1_post_g", "mix_pre_g", "w_in", "sgu_norm_g",
           "sgu_w", "sgu_b", "pool_w", "pool_scale", "w_out_a", "w_out_b", "w_o", "mix_post_g", "ffn2_pre_g",
           "ffn2_w_gate", "ffn2_w_up", "ffn2_w_down", "ffn2_post_g", "ple_pre_g", "ple_w_gate", "ple_w_proj", "ple_post_g")
PACK_ROWS = 16


TRANSPOSED = ("ffn1_w_gate", "ffn1_w_up", "ffn2_w_gate", "ffn2_w_up")


def _shard2d(name, a):
    a = a[0]
    return a.T if name in TRANSPOSED else a.reshape(-1, a.shape[-1])


def _unshard2d(name, a2d, shape):
    return (a2d.T if name in TRANSPOSED else a2d).reshape(shape)


def pack_rows(gains, sgu_b, loss_tile, name):
    n = len(gains)
    d = gains[0].shape[1]
    g = sgu_b.shape[0]

    def body(*refs):
        o_ref = refs[-1]
        o_ref[...] = jnp.zeros_like(o_ref)
        for i in range(n):
            o_ref[i:i + 1, :] = refs[i][...]
        o_ref[n:n + g, 0:CHUNK] = refs[n][...]
        o_ref[n + g:n + g + 1, 0:CHUNK] = refs[n + 1][0:1, :]

    return pl.pallas_call(
        body, name=name, in_specs=[VMEM_SPEC] * (n + 2), out_specs=VMEM_SPEC,
        out_shape=jax.ShapeDtypeStruct((PACK_ROWS, d), F32))(*gains, sgu_b, loss_tile)


def _pack_small(parts, tag):
    d = parts[GAINS[0]].shape[-1]
    rows = pack_rows([parts[n] for n in GAINS], parts["sgu_b"].reshape(-1, CHUNK), parts["loss"], "pack_" + tag)
    return jnp.concatenate([rows, parts["sgu_w"].reshape(-1, d)], axis=0)


def _unpack_small(packed, like):
    n, g = len(GAINS), like["sgu_b"].size // CHUNK
    out = {name: packed[i:i + 1] for i, name in enumerate(GAINS)}
    out["sgu_b"] = packed[n:n + g, :CHUNK].reshape(like["sgu_b"].shape)
    out["loss"] = packed[n + g, 0]
    out["sgu_w"] = packed[PACK_ROWS:].reshape(like["sgu_w"].shape)
    return out


def _ffn_fwd(xn, h, w, pre, g_post, g_next, tag, between=None, first=None):
    a4, s4, t4 = ffn_gu(xn, w[pre + "w_gate"], w[pre + "w_up"], tag + "_gu") if first is None else first(xn)
    deps = ()
    if between is not None:
        more, deps = between(a4)
        w.update(more)
    f, h_new, xn_next = mm_norm_res(a4, w[pre + "w_down"], h, g_post, g_next, 0.5, tag + "_down", deps=deps)
    return dict(xn=xn, h=h, a4=a4, s4=s4, t4=t4, f=f), h_new, xn_next


def _ffn_bwd_w(dh, saved, w, pre, g_post, tag, deps):
    df, dg4, du4, d_post = ffn_bwd_a(dh, saved["f"], g_post, w[pre + "w_down"], saved["s4"], saved["t4"], 0.5, tag + "_bwd_a",
                                     deps=deps)
    grads, exchanging, token = _ffn_dw(saved, df, dg4, du4, pre, tag)
    return dg4, du4, grads, d_post, (exchanging, token)


def _ffn_dw(saved, df, dg4, du4, pre, tag):
    jobs = {pre + "w_down": (saved["a4"], df), pre + "w_gate": (dg4, saved["xn"]), pre + "w_up": (du4, saved["xn"])}
    grads, started, deps = {}, {}, ()
    for name, (x, dy) in jobs.items():
        grads[name] = dw_tn(x, dy, N_CHIPS, f"dw_{name}", x_kmajor=True, dy_mode="same", deps=deps)
        started[name], token = sibling_start([grads[name]], True, "exchange_start_" + name)
        deps = (token,)
    return grads, [started[pre + kind] for kind in ("w_gate", "w_up", "w_down")], token


def _ffn_bwd_x(dh, dg4, du4, saved, w, pre, g_pre, tag, deps):
    return dx_norm_bwd([(dg4, w[pre + "w_gate"], False), (du4, w[pre + "w_up"], False)], saved["h"], g_pre, dh,
                       tag + "_bwd_x", deps=deps)


def _start_gathers(weights, chip):
    first_tag, first_names = GATHERS[0]
    made, own = cast_place([_shard2d(n, weights[n]) for n in first_names], chip, "cast_" + first_tag, plain=True)
    sends, recvs, bufs, token = allgather_start(list(made), "allgather_start_" + first_tag, CHIPS_BARRIERS[0])
    gathering = {first_tag: (sends, recvs, bufs, 0)}
    later = [n for _, names in GATHERS[1:] for n in names]
    cast = cast_place([_shard2d(n, weights[n]) for n in later], chip, "cast_later", deps=(token,))
    sends, recvs, bufs, token = allgather_start(list(cast), "allgather_start_later", CHIPS_BARRIERS[1])
    first = 0
    for tag, names in GATHERS[1:]:
        gathering[tag] = (sends, recvs, bufs[first:first + len(names)], first)
        first += len(names)
    return gathering, token, dict(zip(first_names, own))


def _gathered(started, names, after, tag):
    sends, recvs, bufs, first = started
    landed = allgather_wait(sends, recvs, bufs, after, "allgather_wait_" + tag, first)
    return dict(zip(names, d2d_forward(landed, "d2d_forward_" + tag)))


def _forward_early(started, after, tag):
    sends, recvs, bufs, first = started
    landed = allgather_wait(sends, recvs, bufs, after, "allgather_wait_" + tag, first)
    return d2d_forward_start(landed, "d2d_forward_start_" + tag)


def _forwarded(forwarding, names, after, tag):
    return dict(zip(names, d2d_forward_wait(forwarding, after, "d2d_forward_wait_" + tag)))


def _exchange_start(names, big_g, tag):
    started, token = sibling_start([big_g[n] for n in names], True, "exchange_start_" + tag)
    return [started], token


def _scatter_begin(names, exchanging, after, core, tag):
    partial, from_sibling = [], []
    for e, started in enumerate(exchanging):
        mine, theirs = sibling_wait(started, True, after, f"exchange_wait_{tag}_{e}")
        partial += mine
        from_sibling += theirs
    chip_sums = list(add_halves(partial, from_sibling, core, "add_halves_" + tag))
    turn = len(GROUPS) - 1 - [g for g, _ in GROUPS].index(tag)
    sends, recvs, sums, lands, token = scatter_start(chip_sums, "scatter_start_" + tag, CHIPS_BARRIERS[turn % 2])
    return (names, sends, recvs, sums, lands), token


def _share_begin(scattering, after, chip, tag):
    names, sends, recvs, sums, lands = scattering
    sums, lands = scatter_wait(sends, recvs, sums, lands, after, "scatter_wait_" + tag)
    reduced = list(add_chips(sums, lands, chip, "add_chips_" + tag))
    sharing, token = sibling_start(reduced, False, "share_start_" + tag)
    return (names, sharing), token


def _update(shared, after, weights, moments_m, moments_v, tag, results):
    names, sharing = shared
    reduced, others = sibling_wait(sharing, False, after, "share_wait_" + tag)
    for b, idx in enumerate(_by_shape(reduced)):
        in_bucket = [names[i] for i in idx]
        outs = adamw_halves_sparsecore(
            [_shard2d(n, weights[n]) for n in in_bucket], [reduced[i] for i in idx], [others[i] for i in idx],
            [_shard2d(n, moments_m[n]) for n in in_bucket], [_shard2d(n, moments_v[n]) for n in in_bucket], f"adamw_{tag}_{b}")
        for n, per_weight in zip(in_bucket, outs):
            for store, value in zip(results, per_weight):
                store[n] = _unshard2d(n, value, weights[n].shape)


def _update_last(scattering, after, weights, moments_m, moments_v, core, chip, tag, results):
    names, sends, recvs, sums, lands = scattering
    sums, lands = scatter_wait(sends, recvs, sums, lands, after, "scatter_wait_" + tag)
    reduced = list(add_chips(sums, lands, chip, "add_chips_" + tag))
    (sends, recvs, reduced, zones), _ = sibling_start(reduced, False, "share_start_" + tag)
    state = [[_shard2d(n, held[n]) for n in names] for held in (weights, moments_m, moments_v)]
    first = adamw_one_half(state[0], reduced, state[1], state[2], core, f"adamw_{tag}_own", own=True)
    _, others = sibling_wait((sends, recvs, reduced, zones), False, first[1][-1], "share_wait_" + tag)
    outs = adamw_one_half(state[0], others, state[1], state[2], core, f"adamw_{tag}_other", own=False, into=first)
    for store, values in zip(results, outs):
        for n, value in zip(names, values):
            store[n] = _unshard2d(n, value, weights[n].shape)


def kernel(x, p, ffn1_pre_g, ffn1_w_gate, ffn1_w_up, ffn1_w_down, ffn1_post_g, mix_pre_g, w_in, sgu_norm_g, sgu_w, sgu_b, pool_w, pool_scale, w_out_a, w_out_b, w_o, mix_post_g, ffn2_pre_g, ffn2_w_gate, ffn2_w_up, ffn2_w_down, ffn2_post_g, ple_pre_g, ple_w_gate, ple_w_proj, ple_post_g, loss_target, m_ffn1_pre_g, m_ffn1_w_gate, m_ffn1_w_up, m_ffn1_w_down, m_ffn1_post_g, m_mix_pre_g, m_w_in, m_sgu_norm_g, m_sgu_w, m_sgu_b, m_pool_w, m_pool_scale, m_w_out_a, m_w_out_b, m_w_o, m_mix_post_g, m_ffn2_pre_g, m_ffn2_w_gate, m_ffn2_w_up, m_ffn2_w_down, m_ffn2_post_g, m_ple_pre_g, m_ple_w_gate, m_ple_w_proj, m_ple_post_g, v_ffn1_pre_g, v_ffn1_w_gate, v_ffn1_w_up, v_ffn1_w_down, v_ffn1_post_g, v_mix_pre_g, v_w_in, v_sgu_norm_g, v_sgu_w, v_sgu_b, v_pool_w, v_pool_scale, v_w_out_a, v_w_out_b, v_w_o, v_mix_post_g, v_ffn2_pre_g, v_ffn2_w_gate, v_ffn2_w_up, v_ffn2_w_down, v_ffn2_post_g, v_ple_pre_g, v_ple_w_gate, v_ple_w_proj, v_ple_post_g):
    given = dict(locals())
    weights = {n: given[n] for n in WEIGHTS}
    moments_m = {n: given["m_" + n] for n in WEIGHTS}
    moments_v = {n: given["v_" + n] for n in WEIGHTS}
    core = lax.axis_index("c").astype(jnp.int32).reshape(1)
    chip = (2 * lax.axis_index("x") + lax.axis_index("y")).astype(jnp.int32).reshape(1)

    d = x.shape[-1]
    gathering, last_start, own_first = _start_gathers(weights, chip)
    token = (last_start,)
    gain = {n: weights[n] for n in GAINS}
    sgu_w3 = sgu_w[0]
    sgu_b3 = sgu_b[0][:, :, None]
    groups = dict(GROUPS + GATHERS)

    h0 = x[0]
    tgt = loss_target[0]
    p_bf = p[0, 0].astype(BF16)
    xn1 = rms_cast(h0, gain["ffn1_pre_g"], "ffn1_pre_norm")
    w = {}

    def ffn1_first(xn):
        own = ffn_gu(xn, own_first["ffn1_w_gate"][None], own_first["ffn1_w_up"][None], "ffn1_gu_own", chip=chip)
        w.update(_gathered(gathering["ffn1_in"], groups["ffn1_in"], (own[0], last_start), "ffn1_in"))
        return ffn_gu(xn, w["ffn1_w_gate"], w["ffn1_w_up"], "ffn1_gu", chip=chip, into=own)

    s1, h1, xn2 = _ffn_fwd(xn1, h0, w, "ffn1_", gain["ffn1_post_g"], gain["mix_pre_g"], "ffn1",
                           lambda a4: (_gathered(gathering["ffn1_out"], groups["ffn1_out"], a4, "ffn1_out"), ()), ffn1_first)
    w.update(_gathered(gathering["mixer"], groups["mixer"], h1, "mixer"))
    full = {n: w[n].reshape(-1, d) for n in ("w_out_a", "w_out_b", "w_o")}
    n_groups = pool_w.shape[1]
    rows_per = pool_w.shape[2]
    dgp = pool_w.shape[3]
    pool_full = w["pool_w"].reshape(N_CHIPS, n_groups, rows_per, dgp).transpose(1, 0, 2, 3).reshape(n_groups, N_CHIPS * rows_per, dgp)
    z = mixer_in(xn2, w["w_in"], "mixer_in")
    a = sgu_fwd(z, gain["sgu_norm_g"], sgu_w3, sgu_b3, "sgu_fwd")
    diff, b = pool_fwd(z, pool_full, gain["pool_scale"], "pool_fwd")
    ya, yb, y = mixer_y(a, b, z, full["w_out_a"], full["w_out_b"], "mixer_y")
    forwarding, tok = _forward_early(gathering["ffn2"], y, "ffn2")
    m, h2, xn3 = mm_norm_res(y[None], full["w_o"][None], h1, gain["mix_post_g"], gain["ffn2_pre_g"], 1.0, "mixer_out",
                             deps=(tok,))
    w.update(_forwarded(forwarding, groups["ffn2"], h2, "ffn2"))
    early = {}

    def forward_ple(a4):
        early["ple"], tok_ple = _forward_early(gathering["ple"], a4, "ple")
        return {}, (tok_ple,)

    s2, h3, xn4 = _ffn_fwd(xn3, h2, w, "ffn2_", gain["ffn2_post_g"], gain["ple_pre_g"], "ffn2", forward_ple)
    w.update(_forwarded(early["ple"], groups["ple"], h3, "ple"))
    full["ple_w_gate"] = w["ple_w_gate"].reshape(-1, d)
    proj_full = w["ple_w_proj"].transpose(1, 0, 2).reshape(ple_w_proj.shape[1], -1)

    small_g, big_g = {}, {}
    ds, de, dh3, small_g["ple_post_g"], small_g["ple_pre_g"], loss_part = ple_loss(
        xn4, p_bf, full["ple_w_gate"], proj_full, h3, gain["ple_post_g"], gain["ple_pre_g"], tgt, "ple_loss")
    small_g["loss"] = loss_part
    grad, delta, new_m, new_v = {}, {}, {}, {}
    results = (grad, delta, new_m, new_v)
    update = functools.partial(_update, weights=weights, moments_m=moments_m, moments_v=moments_v, results=results)

    def reduce_behind(tag, previous, started=None):
        exchanging, tok = _exchange_start(groups[tag], big_g, tag) if started is None else started
        if previous is not None:
            shared, tok = _share_begin(previous[1], tok, chip, previous[0])
        scattering, tok_scatter = _scatter_begin(groups[tag], exchanging, tok, core, tag)
        if previous is not None:
            update(shared, tok_scatter, tag=previous[0])
        return (tag, scattering), (tok_scatter,)

    big_g["ple_w_gate"] = dw_tn(xn4, ds, 1, "dw_ple_gate").reshape(N_CHIPS, -1, d)
    big_g["ple_w_proj"] = dw_tn(p_bf, de, N_CHIPS, "dw_ple_proj", dy_mode="cols")
    reducing, deps = reduce_behind("ple", None)

    dg4, du4, g2, small_g["ffn2_post_g"], exchanging = _ffn_bwd_w(dh3, s2, w, "ffn2_", gain["ffn2_post_g"], "ffn2", deps)
    big_g.update(g2)
    dh2, small_g["ffn2_pre_g"] = _ffn_bwd_x(dh3, dg4, du4, s2, w, "ffn2_", gain["ffn2_pre_g"], "ffn2", ())
    reducing, deps = reduce_behind("ffn2", reducing, exchanging)

    dm, dya, dyb, dz, da, db, small_g["mix_post_g"] = mixer_bwd_y(
        dh2, m, gain["mix_post_g"], full["w_o"], ya, yb, z, full["w_out_a"], full["w_out_b"], "mixer_bwd_y", deps=deps)
    big_g["w_o"] = dw_tn(y, dm, 1, "dw_o").reshape(N_CHIPS, -1, d)
    big_g["w_out_a"] = dw_tn(a, dya, 1, "dw_out_a").reshape(N_CHIPS, -1, d)
    big_g["w_out_b"] = dw_tn(b, dyb, 1, "dw_out_b").reshape(N_CHIPS, -1, d)
    dz, d_sgu_w, d_sgu_b, small_g["sgu_norm_g"] = sgu_bwd(z, da, dz, gain["sgu_norm_g"], sgu_w3, sgu_b3, "sgu_bwd")
    dz, d_pool_w, small_g["pool_scale"] = pool_bwd(db, diff, dz, pool_full, gain["pool_scale"], "pool_bwd")
    big_g["pool_w"] = d_pool_w.astype(BF16).reshape(n_groups, N_CHIPS, rows_per, dgp).transpose(1, 0, 2, 3).reshape(
        N_CHIPS, n_groups * rows_per, dgp)
    big_g["w_in"] = dw_in_tiles(xn2, dz, N_CHIPS, "dw_in")
    dh1, small_g["mix_pre_g"] = dx_norm_bwd([(dz, w["w_in"], True)], h1, gain["mix_pre_g"], dh2, "mixer_bwd_x")
    reducing, deps = reduce_behind("mixer", reducing)

    dg4, du4, g1, small_g["ffn1_post_g"], exchanging = _ffn_bwd_w(dh1, s1, w, "ffn1_", gain["ffn1_post_g"], "ffn1", deps)
    big_g.update(g1)
    reducing, deps = reduce_behind("ffn1", reducing, exchanging)
    dh0, small_g["ffn1_pre_g"] = _ffn_bwd_x(dh1, dg4, du4, s1, w, "ffn1_", gain["ffn1_pre_g"], "ffn1", deps)
    small_g["sgu_w"] = d_sgu_w
    small_g["sgu_b"] = d_sgu_b[:, :, 0]

    no_state = {"loss": jnp.zeros((8, 128), F32)}
    packed = small_allreduce_adamw(
        _pack_small(small_g, "grads"), _pack_small({n: weights[n] for n in SMALL} | no_state, "weights"),
        _pack_small({n: moments_m[n] for n in SMALL} | no_state, "m"), _pack_small({n: moments_v[n] for n in SMALL} | no_state, "v"))
    like = {n: weights[n] for n in SMALL}
    for store, block in zip((grad, delta, new_m, new_v), packed):
        store.update(_unpack_small(block, like))
    _update_last(reducing[1], packed[0], weights, moments_m, moments_v, core, chip, "ffn1", results)

    return (grad["loss"], dh0[None], *[grad[n] for n in WEIGHTS], *[delta[n] for n in WEIGHTS],
            *[new_m[n] for n in WEIGHTS], *[new_v[n] for n in WEIGHTS])
```
